```python
import math
import jax, jax.numpy as jnp
from jax import lax
import numpy as np

D_MODEL = 1024
BATCH = 8
SEQ = 8192
DEPTH = 2

N_META = 16
CHUNK = 64
PAD = CHUNK - N_META
N_MIXERS = 2
RMS_EPS = 1e-6
RET_HEADS = 4
RET_DK = 256
RET_DV = 512
RET_QK = RET_HEADS * RET_DK
RET_V = RET_HEADS * RET_DV
RET_IN = 2 * RET_QK + 2 * RET_V
ROPE_BASE = 10000.0
DN_HEADS = 8
DN_DK = 128
DN_DV = 256
DN_QK = DN_HEADS * DN_DK
DN_V = DN_HEADS * DN_DV
DN_CONV_CH = 2 * DN_QK + DN_V
DN_IN = DN_CONV_CH + DN_V + 2 * DN_HEADS
CONV_K = 4
FFN_HIDDEN = ((8 * D_MODEL + 3 * 256 - 1) // (3 * 256)) * 256
N_RET_LAYERS = (DEPTH + 1) // 2
N_DN_LAYERS = DEPTH // 2

kernel_name = "hybrid_retention_gated_deltanet_meta"


def rmsnorm(x, w):
    xf = x.astype(jnp.float32)
    y = xf * lax.rsqrt(jnp.mean(xf * xf, axis=-1, keepdims=True) + RMS_EPS)
    return (y * w.astype(jnp.float32)).astype(x.dtype)


def l2norm(x):
    return x * lax.rsqrt(jnp.sum(x * x, axis=-1, keepdims=True) + RMS_EPS)


def rope(t, pos):
    half = t.shape[-1] // 2
    inv_freq = ROPE_BASE ** (-jnp.arange(half, dtype=jnp.float32) / half)
    ang = pos[:, None] * inv_freq[None, :]
    cos = jnp.cos(ang)[None, :, None, :]
    sin = jnp.sin(ang)[None, :, None, :]
    t1, t2 = t[..., :half], t[..., half:]
    return jnp.concatenate([t1 * cos - t2 * sin, t1 * sin + t2 * cos], axis=-1)


def to_chunks(t):
    b, l, h, d = t.shape
    return t.reshape(b, l // CHUNK, CHUNK, h, d).transpose(1, 0, 3, 2, 4)


def from_chunks(t):
    n, b, h, c, d = t.shape
    return t.transpose(1, 0, 3, 2, 4).reshape(b, n * c, h, d)


def causal_conv(x, w):
    ch = x.shape[-1]
    return lax.conv_general_dilated(
        x, w.astype(x.dtype)[:, None, :], window_strides=(1,), padding=[(CONV_K - 1, 0)],
        dimension_numbers=("NWC", "WIO", "NWC"), feature_group_count=ch)


def gated_head_norm(o, norm_w, gate):
    o = o * lax.rsqrt(jnp.mean(o * o, axis=-1, keepdims=True) + RMS_EPS) * norm_w.astype(jnp.float32)
    return o * jax.nn.silu(gate.astype(jnp.float32))


def retention(h, w_in, gn_w, w_out, valid, pos):
    b, l, _ = h.shape
    proj = h @ w_in
    q, k, v, g = jnp.split(proj, [RET_QK, 2 * RET_QK, 2 * RET_QK + RET_V], axis=-1)
    q = rope(q.reshape(b, l, RET_HEADS, RET_DK).astype(jnp.float32), pos)
    k = rope(k.reshape(b, l, RET_HEADS, RET_DK).astype(jnp.float32), pos)
    k = k * (RET_DK ** -0.5) * valid[None, :, None, None]
    v = v.reshape(b, l, RET_HEADS, RET_DV).astype(jnp.float32)

    log_gamma = jnp.log1p(-jnp.exp2(-5.0 - jnp.arange(RET_HEADS, dtype=jnp.float32)))
    idx = jnp.arange(CHUNK, dtype=jnp.float32)
    rel = idx[:, None] - idx[None, :]
    dmask = jnp.where((rel >= 0)[None], jnp.exp(log_gamma[:, None, None] * jnp.maximum(rel, 0.0)), 0.0)
    xi = jnp.exp(log_gamma[:, None] * (idx[None, :] + 1.0))[:, :, None]
    zeta = jnp.exp(log_gamma[:, None] * (CHUNK - 1.0 - idx[None, :]))[:, :, None]
    gamma_c = jnp.exp(log_gamma * CHUNK)[:, None, None]

    def step(state, inp):
        qc, kc, vc = inp
        scores = jnp.einsum("bhid,bhjd->bhij", qc, kc) * dmask
        o = jnp.einsum("bhij,bhjv->bhiv", scores, vc) + jnp.einsum("bhid,bhdv->bhiv", qc * xi, state)
        state = gamma_c * state + jnp.einsum("bhjd,bhjv->bhdv", kc * zeta, vc)
        return state, o

    s0 = jnp.zeros((b, RET_HEADS, RET_DK, RET_DV), jnp.float32)
    _, o = lax.scan(step, s0, (to_chunks(q), to_chunks(k), to_chunks(v)))
    o = from_chunks(o)
    o = gated_head_norm(o, gn_w, g.reshape(b, l, RET_HEADS, RET_DV))
    return o.reshape(b, l, RET_V).astype(h.dtype) @ w_out


def gated_deltanet(h, w_in, conv_w, a_log, dt_bias, norm_w, w_out, valid):
    b, l, _ = h.shape
    proj = h @ w_in
    qkv, gate, beta_in, a_in = jnp.split(
        proj, [DN_CONV_CH, DN_CONV_CH + DN_V, DN_CONV_CH + DN_V + DN_HEADS], axis=-1)
    qkv = qkv * valid[None, :, None].astype(qkv.dtype)
    qkv = jax.nn.silu(causal_conv(qkv, conv_w))
    q, k, v = jnp.split(qkv, [DN_QK, 2 * DN_QK], axis=-1)
    q = l2norm(q.reshape(b, l, DN_HEADS, DN_DK).astype(jnp.float32)) * (DN_DK ** -0.5)
    k = l2norm(k.reshape(b, l, DN_HEADS, DN_DK).astype(jnp.float32))
    v = v.reshape(b, l, DN_HEADS, DN_DV).astype(jnp.float32)
    vmask = valid[None, :, None]
    beta = (jax.nn.sigmoid(beta_in.astype(jnp.float32)) * vmask)[..., None]
    g = (-jnp.exp(a_log.astype(jnp.float32))
         * jax.nn.softplus(a_in.astype(jnp.float32) + dt_bias.astype(jnp.float32)) * vmask)[..., None]

    incl = jnp.tril(jnp.ones((CHUNK, CHUNK), dtype=bool))
    strict = jnp.tril(jnp.ones((CHUNK, CHUNK), dtype=bool), -1)
    eye = jnp.eye(CHUNK, dtype=jnp.float32)

    def step(state, inp):
        qc, kc, vc, bc, gc = inp
        gam = jnp.cumsum(gc, axis=-2)
        diff = gam - jnp.swapaxes(gam, -1, -2)
        decay = jnp.exp(jnp.where(incl, diff, -jnp.inf))
        kk = jnp.einsum("bhid,bhjd->bhij", kc, kc)
        a_mat = jnp.where(strict, bc * kk * decay, 0.0)
        rhs = jnp.concatenate([vc * bc, kc * bc * jnp.exp(gam)], axis=-1)
        sol = lax.linalg.triangular_solve(eye + a_mat, rhs, left_side=True, lower=True,
                                          unit_diagonal=True)
        u, w = sol[..., :DN_DV], sol[..., DN_DV:]
        v_new = u - jnp.einsum("bhik,bhkv->bhiv", w, state)
        qk = jnp.einsum("bhid,bhjd->bhij", qc, kc) * decay
        o = (jnp.einsum("bhid,bhdv->bhiv", qc * jnp.exp(gam), state)
             + jnp.einsum("bhij,bhjv->bhiv", qk, v_new))
        g_last = gam[..., -1:, :]
        state = state * jnp.exp(g_last) + jnp.einsum(
            "bhjd,bhjv->bhdv", kc * jnp.exp(g_last - gam), v_new)
        return state, o

    s0 = jnp.zeros((b, DN_HEADS, DN_DK, DN_DV), jnp.float32)
    _, o = lax.scan(step, s0, (to_chunks(q), to_chunks(k), to_chunks(v),
                               to_chunks(beta), to_chunks(g)))
    o = from_chunks(o)
    o = gated_head_norm(o, norm_w, gate.reshape(b, l, DN_HEADS, DN_DV))
    return o.reshape(b, l, DN_V).astype(h.dtype) @ w_out


def swiglu(h, w_gate, w_up, w_down):
    return (jax.nn.silu(h @ w_gate) * (h @ w_up)) @ w_down


def _fwd_setup_inputs(seed: int = 0) -> dict:
    key = jax.random.key(seed)
    ks = jax.random.split(key, 20)
    f32 = jnp.float32
    nrm = lambda k, shape, fan_in: jax.random.normal(k, shape, f32) * (fan_in ** -0.5)
    gain = lambda k, shape: 1.0 + 0.02 * jax.random.normal(k, shape, f32)
    dt = jnp.exp(jax.random.uniform(ks[9], (N_DN_LAYERS, DN_HEADS), f32)
                 * (math.log(0.1) - math.log(0.001)) + math.log(0.001))
    return {
        "x": jax.random.normal(ks[0], (BATCH, SEQ, D_MODEL), f32),
        "meta_tokens": jax.random.normal(ks[1], (N_META, D_MODEL), f32),
        "mix_norm_w": gain(ks[2], (DEPTH, D_MODEL)),
        "ffn_norm_w": gain(ks[3], (DEPTH, D_MODEL)),
        "ret_w_in": nrm(ks[4], (N_RET_LAYERS, D_MODEL, RET_IN), D_MODEL),
        "ret_gn_w": gain(ks[5], (N_RET_LAYERS, RET_DV)),
        "ret_w_out": nrm(ks[6], (N_RET_LAYERS, RET_V, D_MODEL), RET_V),
        "dn_w_in": nrm(ks[7], (N_DN_LAYERS, D_MODEL, DN_IN), D_MODEL),
        "dn_conv_w": nrm(ks[8], (N_DN_LAYERS, CONV_K, DN_CONV_CH), CONV_K),
        "dn_a_log": jnp.log(jax.random.uniform(ks[10], (N_DN_LAYERS, DN_HEADS), f32, 1.0, 16.0)),
        "dn_dt_bias": dt + jnp.log(-jnp.expm1(-dt)),
        "dn_norm_w": gain(ks[11], (N_DN_LAYERS, DN_DV)),
        "dn_w_out": nrm(ks[12], (N_DN_LAYERS, DN_V, D_MODEL), DN_V),
        "ffn_w_gate": nrm(ks[13], (DEPTH, D_MODEL, FFN_HIDDEN), D_MODEL),
        "ffn_w_up": nrm(ks[14], (DEPTH, D_MODEL, FFN_HIDDEN), D_MODEL),
        "ffn_w_down": nrm(ks[15], (DEPTH, FFN_HIDDEN, D_MODEL), FFN_HIDDEN),
        "final_norm_w": gain(ks[16], (D_MODEL,)),
    }


def _fwd_reference(x, meta_tokens, mix_norm_w, ffn_norm_w, ret_w_in, ret_gn_w, ret_w_out,
              dn_w_in, dn_conv_w, dn_a_log, dn_dt_bias, dn_norm_w, dn_w_out,
              ffn_w_gate, ffn_w_up, ffn_w_down, final_norm_w):
    b = x.shape[0]
    h = jnp.concatenate([
        jnp.zeros((b, PAD, D_MODEL), x.dtype),
        jnp.broadcast_to(meta_tokens.astype(x.dtype)[None], (b, N_META, D_MODEL)),
        x], axis=1)
    l = h.shape[1]
    pos_i = jnp.arange(l) - PAD
    valid = (pos_i >= 0).astype(jnp.float32)
    pos = pos_i.astype(jnp.float32)
    for i in range(DEPTH):
        hn = rmsnorm(h, mix_norm_w[i])
        if i % N_MIXERS == 0:
            j = i // N_MIXERS
            mix = retention(hn, ret_w_in[j], ret_gn_w[j], ret_w_out[j], valid, pos)
        else:
            j = i // N_MIXERS
            mix = gated_deltanet(hn, dn_w_in[j], dn_conv_w[j], dn_a_log[j], dn_dt_bias[j],
                                 dn_norm_w[j], dn_w_out[j], valid)
        h = h + mix
        h = h + swiglu(rmsnorm(h, ffn_norm_w[i]), ffn_w_gate[i], ffn_w_up[i], ffn_w_down[i])
    return rmsnorm(h, final_norm_w)[:, CHUNK:, :]


import jax as _jax
import jax.numpy as _jnp

TWIN_FORMAT = 'train_step'
FWD_PARAMS = ['x', 'meta_tokens', 'mix_norm_w', 'ffn_norm_w', 'ret_w_in', 'ret_gn_w', 'ret_w_out', 'dn_w_in', 'dn_conv_w', 'dn_a_log', 'dn_dt_bias', 'dn_norm_w', 'dn_w_out', 'ffn_w_gate', 'ffn_w_up', 'ffn_w_down', 'final_norm_w']
TWIN_WEIGHTS = ['meta_tokens', 'mix_norm_w', 'ffn_norm_w', 'ret_w_in', 'ret_gn_w', 'ret_w_out', 'dn_w_in', 'dn_conv_w', 'dn_a_log', 'dn_dt_bias', 'dn_norm_w', 'dn_w_out', 'ffn_w_gate', 'ffn_w_up', 'ffn_w_down', 'final_norm_w']
TWIN_DIFF_INPUT = 'x'
TWIN_INPUTS = ['x', 'meta_tokens', 'mix_norm_w', 'ffn_norm_w', 'ret_w_in', 'ret_gn_w', 'ret_w_out', 'dn_w_in', 'dn_conv_w', 'dn_a_log', 'dn_dt_bias', 'dn_norm_w', 'dn_w_out', 'ffn_w_gate', 'ffn_w_up', 'ffn_w_down', 'final_norm_w', 'loss_target', 'm_meta_tokens', 'm_mix_norm_w', 'm_ffn_norm_w', 'm_ret_w_in', 'm_ret_gn_w', 'm_ret_w_out', 'm_dn_w_in', 'm_dn_conv_w', 'm_dn_a_log', 'm_dn_dt_bias', 'm_dn_norm_w', 'm_dn_w_out', 'm_ffn_w_gate', 'm_ffn_w_up', 'm_ffn_w_down', 'm_final_norm_w', 'v_meta_tokens', 'v_mix_norm_w', 'v_ffn_norm_w', 'v_ret_w_in', 'v_ret_gn_w', 'v_ret_w_out', 'v_dn_w_in', 'v_dn_conv_w', 'v_dn_a_log', 'v_dn_dt_bias', 'v_dn_norm_w', 'v_dn_w_out', 'v_ffn_w_gate', 'v_ffn_w_up', 'v_ffn_w_down', 'v_final_norm_w']
TWIN_OUTPUTS = ['loss', 'grad_x', 'grad_meta_tokens', 'grad_mix_norm_w', 'grad_ffn_norm_w', 'grad_ret_w_in', 'grad_ret_gn_w', 'grad_ret_w_out', 'grad_dn_w_in', 'grad_dn_conv_w', 'grad_dn_a_log', 'grad_dn_dt_bias', 'grad_dn_norm_w', 'grad_dn_w_out', 'grad_ffn_w_gate', 'grad_ffn_w_up', 'grad_ffn_w_down', 'grad_final_norm_w', 'delta_meta_tokens', 'delta_mix_norm_w', 'delta_ffn_norm_w', 'delta_ret_w_in', 'delta_ret_gn_w', 'delta_ret_w_out', 'delta_dn_w_in', 'delta_dn_conv_w', 'delta_dn_a_log', 'delta_dn_dt_bias', 'delta_dn_norm_w', 'delta_dn_w_out', 'delta_ffn_w_gate', 'delta_ffn_w_up', 'delta_ffn_w_down', 'delta_final_norm_w', 'new_m_meta_tokens', 'new_m_mix_norm_w', 'new_m_ffn_norm_w', 'new_m_ret_w_in', 'new_m_ret_gn_w', 'new_m_ret_w_out', 'new_m_dn_w_in', 'new_m_dn_conv_w', 'new_m_dn_a_log', 'new_m_dn_dt_bias', 'new_m_dn_norm_w', 'new_m_dn_w_out', 'new_m_ffn_w_gate', 'new_m_ffn_w_up', 'new_m_ffn_w_down', 'new_m_final_norm_w', 'new_v_meta_tokens', 'new_v_mix_norm_w', 'new_v_ffn_norm_w', 'new_v_ret_w_in', 'new_v_ret_gn_w', 'new_v_ret_w_out', 'new_v_dn_w_in', 'new_v_dn_conv_w', 'new_v_dn_a_log', 'new_v_dn_dt_bias', 'new_v_dn_norm_w', 'new_v_dn_w_out', 'new_v_ffn_w_gate', 'new_v_ffn_w_up', 'new_v_ffn_w_down', 'new_v_final_norm_w']
TWIN_LEAF_KINDS = {'loss': 'loss', 'grad_x': 'grad_x', 'grad_meta_tokens': 'grad_w', 'grad_mix_norm_w': 'grad_w', 'grad_ffn_norm_w': 'grad_w', 'grad_ret_w_in': 'grad_w', 'grad_ret_gn_w': 'grad_w', 'grad_ret_w_out': 'grad_w', 'grad_dn_w_in': 'grad_w', 'grad_dn_conv_w': 'grad_w', 'grad_dn_a_log': 'grad_w', 'grad_dn_dt_bias': 'grad_w', 'grad_dn_norm_w': 'grad_w', 'grad_dn_w_out': 'grad_w', 'grad_ffn_w_gate': 'grad_w', 'grad_ffn_w_up': 'grad_w', 'grad_ffn_w_down': 'grad_w', 'grad_final_norm_w': 'grad_w', 'delta_meta_tokens': 'delta_w', 'delta_mix_norm_w': 'delta_w', 'delta_ffn_norm_w': 'delta_w', 'delta_ret_w_in': 'delta_w', 'delta_ret_gn_w': 'delta_w', 'delta_ret_w_out': 'delta_w', 'delta_dn_w_in': 'delta_w', 'delta_dn_conv_w': 'delta_w', 'delta_dn_a_log': 'delta_w', 'delta_dn_dt_bias': 'delta_w', 'delta_dn_norm_w': 'delta_w', 'delta_dn_w_out': 'delta_w', 'delta_ffn_w_gate': 'delta_w', 'delta_ffn_w_up': 'delta_w', 'delta_ffn_w_down': 'delta_w', 'delta_final_norm_w': 'delta_w', 'new_m_meta_tokens': 'new_m', 'new_m_mix_norm_w': 'new_m', 'new_m_ffn_norm_w': 'new_m', 'new_m_ret_w_in': 'new_m', 'new_m_ret_gn_w': 'new_m', 'new_m_ret_w_out': 'new_m', 'new_m_dn_w_in': 'new_m', 'new_m_dn_conv_w': 'new_m', 'new_m_dn_a_log': 'new_m', 'new_m_dn_dt_bias': 'new_m', 'new_m_dn_norm_w': 'new_m', 'new_m_dn_w_out': 'new_m', 'new_m_ffn_w_gate': 'new_m', 'new_m_ffn_w_up': 'new_m', 'new_m_ffn_w_down': 'new_m', 'new_m_final_norm_w': 'new_m', 'new_v_meta_tokens': 'new_v', 'new_v_mix_norm_w': 'new_v', 'new_v_ffn_norm_w': 'new_v', 'new_v_ret_w_in': 'new_v', 'new_v_ret_gn_w': 'new_v', 'new_v_ret_w_out': 'new_v', 'new_v_dn_w_in': 'new_v', 'new_v_dn_conv_w': 'new_v', 'new_v_dn_a_log': 'new_v', 'new_v_dn_dt_bias': 'new_v', 'new_v_dn_norm_w': 'new_v', 'new_v_dn_w_out': 'new_v', 'new_v_ffn_w_gate': 'new_v', 'new_v_ffn_w_up': 'new_v', 'new_v_ffn_w_down': 'new_v', 'new_v_final_norm_w': 'new_v'}


def _forward(args):
    return _fwd_reference(*[args[k] for k in FWD_PARAMS])


def _output_shape():
    out = _jax.eval_shape(lambda: _forward(_fwd_setup_inputs(0)))
    return out.shape, out.dtype

N_MICROBATCH = 1
ADAM_LR = 0.001
ADAM_B1 = 0.9
ADAM_B2 = 0.999
ADAM_EPS = 1e-08
ADAM_WD = 0.01
ADAM_STEP = 10
PER_EXAMPLE_BATCH_AXIS = {'x': 0, 'loss_target': 0}
SHARED_INPUTS = []
_WEIGHT_DTYPES = {'meta_tokens': _jnp.float32, 'mix_norm_w': _jnp.float32, 'ffn_norm_w': _jnp.float32, 'ret_w_in': _jnp.float32, 'ret_gn_w': _jnp.float32, 'ret_w_out': _jnp.float32, 'dn_w_in': _jnp.float32, 'dn_conv_w': _jnp.float32, 'dn_a_log': _jnp.float32, 'dn_dt_bias': _jnp.float32, 'dn_norm_w': _jnp.float32, 'dn_w_out': _jnp.float32, 'ffn_w_gate': _jnp.float32, 'ffn_w_up': _jnp.float32, 'ffn_w_down': _jnp.float32, 'final_norm_w': _jnp.float32}
MOMENT_SCALE = {'meta_tokens': 1.963064e-02, 'mix_norm_w': 2.993739e-01, 'ffn_norm_w': 1.778010e-01, 'ret_w_in': 1.444430e-01, 'ret_gn_w': 2.578950e-01, 'ret_w_out': 1.757160e-01, 'dn_w_in': 7.598295e-02, 'dn_conv_w': 7.702500e-02, 'dn_a_log': 7.015925e-01, 'dn_dt_bias': 6.418210e-01, 'dn_norm_w': 2.155978e-01, 'dn_w_out': 1.095928e-01, 'ffn_w_gate': 7.542607e-02, 'ffn_w_up': 7.288109e-02, 'ffn_w_down': 1.208259e-01, 'final_norm_w': 6.407944e+01}


def _to_microbatches(a, axis):
    t = _jnp.moveaxis(a, axis, 0)
    t = t.reshape((N_MICROBATCH, t.shape[0] // N_MICROBATCH) + t.shape[1:])
    return _jnp.moveaxis(t, 1, axis + 1)


def setup_inputs(seed: int = 0) -> dict:
    inp = _fwd_setup_inputs(seed)
    key = _jax.random.fold_in(_jax.random.key(seed), 7919)
    shape, _ = _output_shape()
    out = dict(inp)
    out["loss_target"] = _jax.random.normal(_jax.random.fold_in(key, 0), shape, _jnp.float32)
    for i, name in enumerate(TWIN_WEIGHTS):
        w = inp[name].astype(_jnp.float32)
        if MOMENT_SCALE is None:
            s = _jnp.sqrt(_jnp.mean(_jnp.square(w)) + 1e-30)
        else:
            s = MOMENT_SCALE[name]
        km, kv = _jax.random.split(_jax.random.fold_in(key, i + 1))
        out[name] = w
        out["m_" + name] = s * _jax.random.normal(km, w.shape, _jnp.float32)
        out["v_" + name] = (s * s) * _jax.random.uniform(kv, w.shape, _jnp.float32, 0.5, 1.5)
    if N_MICROBATCH > 1:
        for name, axis in PER_EXAMPLE_BATCH_AXIS.items():
            out[name] = _to_microbatches(out[name], axis)
    return {'x': out['x'], 'meta_tokens': out['meta_tokens'], 'mix_norm_w': out['mix_norm_w'], 'ffn_norm_w': out['ffn_norm_w'], 'ret_w_in': out['ret_w_in'], 'ret_gn_w': out['ret_gn_w'], 'ret_w_out': out['ret_w_out'], 'dn_w_in': out['dn_w_in'], 'dn_conv_w': out['dn_conv_w'], 'dn_a_log': out['dn_a_log'], 'dn_dt_bias': out['dn_dt_bias'], 'dn_norm_w': out['dn_norm_w'], 'dn_w_out': out['dn_w_out'], 'ffn_w_gate': out['ffn_w_gate'], 'ffn_w_up': out['ffn_w_up'], 'ffn_w_down': out['ffn_w_down'], 'final_norm_w': out['final_norm_w'], 'loss_target': out['loss_target'], 'm_meta_tokens': out['m_meta_tokens'], 'm_mix_norm_w': out['m_mix_norm_w'], 'm_ffn_norm_w': out['m_ffn_norm_w'], 'm_ret_w_in': out['m_ret_w_in'], 'm_ret_gn_w': out['m_ret_gn_w'], 'm_ret_w_out': out['m_ret_w_out'], 'm_dn_w_in': out['m_dn_w_in'], 'm_dn_conv_w': out['m_dn_conv_w'], 'm_dn_a_log': out['m_dn_a_log'], 'm_dn_dt_bias': out['m_dn_dt_bias'], 'm_dn_norm_w': out['m_dn_norm_w'], 'm_dn_w_out': out['m_dn_w_out'], 'm_ffn_w_gate': out['m_ffn_w_gate'], 'm_ffn_w_up': out['m_ffn_w_up'], 'm_ffn_w_down': out['m_ffn_w_down'], 'm_final_norm_w': out['m_final_norm_w'], 'v_meta_tokens': out['v_meta_tokens'], 'v_mix_norm_w': out['v_mix_norm_w'], 'v_ffn_norm_w': out['v_ffn_norm_w'], 'v_ret_w_in': out['v_ret_w_in'], 'v_ret_gn_w': out['v_ret_gn_w'], 'v_ret_w_out': out['v_ret_w_out'], 'v_dn_w_in': out['v_dn_w_in'], 'v_dn_conv_w': out['v_dn_conv_w'], 'v_dn_a_log': out['v_dn_a_log'], 'v_dn_dt_bias': out['v_dn_dt_bias'], 'v_dn_norm_w': out['v_dn_norm_w'], 'v_dn_w_out': out['v_dn_w_out'], 'v_ffn_w_gate': out['v_ffn_w_gate'], 'v_ffn_w_up': out['v_ffn_w_up'], 'v_ffn_w_down': out['v_ffn_w_down'], 'v_final_norm_w': out['v_final_norm_w']}


def _loss(weights, diff, rest, loss_target):
    with _jax.named_scope("forward"):
        args = {**rest, TWIN_DIFF_INPUT: diff, **{k: w.astype(_WEIGHT_DTYPES[k]) for k, w in weights.items()}}
        y = _forward(args)
    with _jax.named_scope("loss_head"):
        err = _jnp.square(y.astype(_jnp.float32) - loss_target)
        return 0.5 * _jnp.sum(_jnp.mean(err, axis=-1)) if err.ndim else 0.5 * err


def _adamw(w, g, m, v):
    m = ADAM_B1 * m + (1.0 - ADAM_B1) * g
    v = ADAM_B2 * v + (1.0 - ADAM_B2) * _jnp.square(g)
    m_hat = m / (1.0 - ADAM_B1 ** ADAM_STEP)
    v_hat = v / (1.0 - ADAM_B2 ** ADAM_STEP)
    delta = -ADAM_LR * (m_hat / (_jnp.sqrt(v_hat) + ADAM_EPS) + ADAM_WD * w)
    return delta, m, v


def reference(x, meta_tokens, mix_norm_w, ffn_norm_w, ret_w_in, ret_gn_w, ret_w_out, dn_w_in, dn_conv_w, dn_a_log, dn_dt_bias, dn_norm_w, dn_w_out, ffn_w_gate, ffn_w_up, ffn_w_down, final_norm_w, loss_target, m_meta_tokens, m_mix_norm_w, m_ffn_norm_w, m_ret_w_in, m_ret_gn_w, m_ret_w_out, m_dn_w_in, m_dn_conv_w, m_dn_a_log, m_dn_dt_bias, m_dn_norm_w, m_dn_w_out, m_ffn_w_gate, m_ffn_w_up, m_ffn_w_down, m_final_norm_w, v_meta_tokens, v_mix_norm_w, v_ffn_norm_w, v_ret_w_in, v_ret_gn_w, v_ret_w_out, v_dn_w_in, v_dn_conv_w, v_dn_a_log, v_dn_dt_bias, v_dn_norm_w, v_dn_w_out, v_ffn_w_gate, v_ffn_w_up, v_ffn_w_down, v_final_norm_w):
    given = dict(x=x, meta_tokens=meta_tokens, mix_norm_w=mix_norm_w, ffn_norm_w=ffn_norm_w, ret_w_in=ret_w_in, ret_gn_w=ret_gn_w, ret_w_out=ret_w_out, dn_w_in=dn_w_in, dn_conv_w=dn_conv_w, dn_a_log=dn_a_log, dn_dt_bias=dn_dt_bias, dn_norm_w=dn_norm_w, dn_w_out=dn_w_out, ffn_w_gate=ffn_w_gate, ffn_w_up=ffn_w_up, ffn_w_down=ffn_w_down, final_norm_w=final_norm_w, loss_target=loss_target, m_meta_tokens=m_meta_tokens, m_mix_norm_w=m_mix_norm_w, m_ffn_norm_w=m_ffn_norm_w, m_ret_w_in=m_ret_w_in, m_ret_gn_w=m_ret_gn_w, m_ret_w_out=m_ret_w_out, m_dn_w_in=m_dn_w_in, m_dn_conv_w=m_dn_conv_w, m_dn_a_log=m_dn_a_log, m_dn_dt_bias=m_dn_dt_bias, m_dn_norm_w=m_dn_norm_w, m_dn_w_out=m_dn_w_out, m_ffn_w_gate=m_ffn_w_gate, m_ffn_w_up=m_ffn_w_up, m_ffn_w_down=m_ffn_w_down, m_final_norm_w=m_final_norm_w, v_meta_tokens=v_meta_tokens, v_mix_norm_w=v_mix_norm_w, v_ffn_norm_w=v_ffn_norm_w, v_ret_w_in=v_ret_w_in, v_ret_gn_w=v_ret_gn_w, v_ret_w_out=v_ret_w_out, v_dn_w_in=v_dn_w_in, v_dn_conv_w=v_dn_conv_w, v_dn_a_log=v_dn_a_log, v_dn_dt_bias=v_dn_dt_bias, v_dn_norm_w=v_dn_norm_w, v_dn_w_out=v_dn_w_out, v_ffn_w_gate=v_ffn_w_gate, v_ffn_w_up=v_ffn_w_up, v_ffn_w_down=v_ffn_w_down, v_final_norm_w=v_final_norm_w)
    weights = {n: given[n] for n in TWIN_WEIGHTS}
    shared = {n: given[n] for n in SHARED_INPUTS}
    per_example = {n: given[n] for n in ['x']}
    grad_fn = _jax.value_and_grad(_loss, argnums=(0, 1))

    def one_microbatch(ex, loss_target):
        ex = dict(ex)
        diff = ex.pop(TWIN_DIFF_INPUT)
        return grad_fn(weights, diff, {**shared, **ex}, loss_target)

    if N_MICROBATCH == 1:
        loss, (grad_w, grad_x) = one_microbatch(per_example, given["loss_target"])
    else:
        def body(carry, xs):
            loss_sum, grad_sum = carry
            l_k, (gw_k, gx_k) = one_microbatch(xs[0], xs[1])
            with _jax.named_scope("update"):
                return (loss_sum + l_k, _jax.tree.map(_jnp.add, grad_sum, gw_k)), gx_k

        init = (_jnp.zeros((), _jnp.float32), _jax.tree.map(_jnp.zeros_like, weights))
        (loss, grad_w), grad_x = _jax.lax.scan(body, init, (per_example, given["loss_target"]))
    with _jax.named_scope("update"):
        delta_w, new_m, new_v = {}, {}, {}
        for n in TWIN_WEIGHTS:
            delta_w[n], new_m[n], new_v[n] = _adamw(weights[n], grad_w[n], given["m_" + n], given["v_" + n])
    return (loss, grad_x, *[grad_w[n] for n in TWIN_WEIGHTS], *[delta_w[n] for n in TWIN_WEIGHTS],
            *[new_m[n] for n in TWIN_WEIGHTS], *[new_v[n] for n in TWIN_WEIGHTS])
```

```python
import functools
import math

import jax
import jax.numpy as jnp
from jax import lax
from jax.experimental import pallas as pl
from jax.experimental.pallas import tpu as pltpu

f32 = jnp.float32
bf16 = jnp.bfloat16
HI = lax.Precision.HIGHEST

N_META = 16
CHUNK = 64
PAD = CHUNK - N_META
RMS_EPS = 1e-6
RET_HEADS, RET_DK, RET_DV = 4, 256, 512
RET_QK, RET_V = RET_HEADS * RET_DK, RET_HEADS * RET_DV
DN_HEADS, DN_DK, DN_DV = 8, 128, 256
DN_QK, DN_V = DN_HEADS * DN_DK, DN_HEADS * DN_DV
DN_CONV_CH = 2 * DN_QK + DN_V
CONV_K = 4
ROPE_BASE = 10000.0
ADAM_LR, ADAM_B1, ADAM_B2, ADAM_EPS, ADAM_WD, ADAM_STEP = 0.001, 0.9, 0.999, 1e-08, 0.01, 10
N_DEV = 8
AXES = ("x", "y", "c")
LANES = 128
MIB = 1024 * 1024


def _tile(n_rows, cap):
    nch = n_rows // CHUNK
    best = 1
    for d in range(1, nch + 1):
        if nch % d == 0 and d * CHUNK <= cap:
            best = d
    return best * CHUNK


def _div_tile(n, cap, align):
    best = None
    for d in range(align, min(n, cap) + 1, align):
        if n % d == 0:
            best = d
    return best if best is not None else n


def _params(sem, vmem_mb):
    return pltpu.CompilerParams(dimension_semantics=sem, vmem_limit_bytes=int(vmem_mb * MIB))


def _nn(a, b, precision=None):
    return jnp.dot(a, b, preferred_element_type=f32, precision=precision)


def _nt(a, b, precision=None):
    return lax.dot_general(a, b, (((1,), (1,)), ((), ())), preferred_element_type=f32, precision=precision)


def _tn(a, b, precision=None):
    return lax.dot_general(a, b, (((0,), (0,)), ((), ())), preferred_element_type=f32, precision=precision)


def _b(x):
    return x.astype(bf16)


def _sigmoid(x):
    return 1.0 / (1.0 + jnp.exp(-x))


def _silu(x):
    return x * _sigmoid(x)


def _dsilu(x):
    s = _sigmoid(x)
    return s * (1.0 + x * (1.0 - s))


def _matmul(a, b, *, mode, tm, tn, tk, name, out_dtype=f32, res=None, vmem_mb=48):
    if mode == "nn":
        (m, k), (k2, n) = a.shape, b.shape
    elif mode == "nt":
        (m, k), (n, k2) = a.shape, b.shape
    else:
        (k, m), (k2, n) = a.shape, b.shape
    assert k == k2 and m % tm == 0 and n % tn == 0 and k % tk == 0, (name, a.shape, b.shape, tm, tn, tk)
    nk = k // tk
    dot = {"nn": _nn, "nt": _nt, "tn": _tn}[mode]
    a_spec = {"nn": pl.BlockSpec((tm, tk), lambda i, j, kk: (i, kk)),
              "nt": pl.BlockSpec((tm, tk), lambda i, j, kk: (i, kk)),
              "tn": pl.BlockSpec((tk, tm), lambda i, j, kk: (kk, i))}[mode]
    b_spec = {"nn": pl.BlockSpec((tk, tn), lambda i, j, kk: (kk, j)),
              "nt": pl.BlockSpec((tn, tk), lambda i, j, kk: (j, kk)),
              "tn": pl.BlockSpec((tk, tn), lambda i, j, kk: (kk, j))}[mode]
    o_spec = pl.BlockSpec((tm, tn), lambda i, j, kk: (i, j))
    has_res = res is not None

    def body(*refs):
        if has_res:
            a_ref, b_ref, r_ref, o_ref = refs[:4]
            rest = refs[4:]
        else:
            a_ref, b_ref, o_ref = refs[:3]
            r_ref = None
            rest = refs[3:]
        part = dot(_b(a_ref[...]), _b(b_ref[...]))
        if nk == 1:
            if has_res:
                part = part + r_ref[...]
            o_ref[...] = part.astype(out_dtype)
        else:
            acc_ref = rest[0]
            kk = pl.program_id(2)

            @pl.when(kk == 0)
            def _():
                acc_ref[...] = part

            @pl.when(kk > 0)
            def _():
                acc_ref[...] += part

            @pl.when(kk == nk - 1)
            def _():
                tot = acc_ref[...]
                if has_res:
                    tot = tot + r_ref[...]
                o_ref[...] = tot.astype(out_dtype)

    in_specs = [a_spec, b_spec] + ([o_spec] if has_res else [])
    args = (a, b) + ((res,) if has_res else ())
    return pl.pallas_call(
        body, name=name, grid=(m // tm, n // tn, nk), in_specs=in_specs, out_specs=o_spec,
        out_shape=jax.ShapeDtypeStruct((m, n), out_dtype),
        scratch_shapes=[pltpu.VMEM((tm, tn), f32)] if nk > 1 else [],
        compiler_params=_params(("parallel", "parallel", "arbitrary"), vmem_mb),
    )(*args)


def _rms_fwd(h, w, name):
    l, d = h.shape
    tr = _tile(l, 512)

    def body(h_ref, w_ref, o_ref):
        x = h_ref[...]
        r = lax.rsqrt(jnp.mean(x * x, axis=-1, keepdims=True) + RMS_EPS)
        o_ref[...] = _b(x * r * w_ref[...])

    return pl.pallas_call(
        body, name=name, grid=(l // tr,),
        in_specs=[pl.BlockSpec((tr, d), lambda i: (i, 0)), pl.BlockSpec((1, d), lambda i: (0, 0))],
        out_specs=pl.BlockSpec((tr, d), lambda i: (i, 0)),
        out_shape=jax.ShapeDtypeStruct((l, d), bf16),
        compiler_params=_params(("parallel",), 32),
    )(h, w)


def _rms_bwd(h, w, dhn, dres, name):
    l, d = h.shape
    tr = _tile(l, 512)

    def body(h_ref, w_ref, g_ref, r_ref, dh_ref, dhb_ref, dw_ref):
        x = h_ref[...]
        r = lax.rsqrt(jnp.mean(x * x, axis=-1, keepdims=True) + RMS_EPS)
        xh = x * r
        g = g_ref[...]
        dxh = g * w_ref[...]
        dx = r * (dxh - xh * jnp.mean(dxh * xh, axis=-1, keepdims=True))
        dh = r_ref[...] + dx
        dh_ref[...] = dh
        dhb_ref[...] = _b(dh)
        dw = jnp.sum(g * xh, axis=0, keepdims=True)

        @pl.when(pl.program_id(0) == 0)
        def _():
            dw_ref[...] = dw

        @pl.when(pl.program_id(0) > 0)
        def _():
            dw_ref[...] += dw

    row = pl.BlockSpec((tr, d), lambda i: (i, 0))
    vec = pl.BlockSpec((1, d), lambda i: (0, 0))
    return pl.pallas_call(
        body, name=name, grid=(l // tr,), in_specs=[row, vec, row, row], out_specs=[row, row, vec],
        out_shape=[jax.ShapeDtypeStruct((l, d), f32), jax.ShapeDtypeStruct((l, d), bf16),
                   jax.ShapeDtypeStruct((1, d), f32)],
        compiler_params=_params(("arbitrary",), 40),
    )(h, w, dhn, dres)


def _final_loss(h, w, target, name):
    l, d = h.shape
    nch = l // CHUNK

    def body(h_ref, w_ref, t_ref, dh_ref, dhb_ref, dw_ref, loss_ref):
        n = pl.program_id(0)
        live = (n > 0).astype(f32)
        x = h_ref[...]
        r = lax.rsqrt(jnp.mean(x * x, axis=-1, keepdims=True) + RMS_EPS)
        xh = x * r
        wv = w_ref[...]
        err = (xh * wv - t_ref[...]) * live
        dy = err * (1.0 / d)
        dxh = dy * wv
        dx = r * (dxh - xh * jnp.mean(dxh * xh, axis=-1, keepdims=True))
        dh_ref[...] = dx
        dhb_ref[...] = _b(dx)
        dw = jnp.sum(dy * xh, axis=0, keepdims=True)
        part = 0.5 * jnp.sum(jnp.sum(err * err, axis=-1, keepdims=True) * (1.0 / d), axis=0, keepdims=True)
        part = jnp.broadcast_to(part, (1, LANES))

        @pl.when(n == 0)
        def _():
            dw_ref[...] = dw
            loss_ref[...] = part

        @pl.when(n > 0)
        def _():
            dw_ref[...] += dw
            loss_ref[...] += part

    row = pl.BlockSpec((CHUNK, d), lambda i: (i, 0))
    vec = pl.BlockSpec((1, d), lambda i: (0, 0))
    return pl.pallas_call(
        body, name=name, grid=(nch,),
        in_specs=[row, vec, pl.BlockSpec((CHUNK, d), lambda i: (jnp.maximum(i - 1, 0), 0))],
        out_specs=[row, row, vec, pl.BlockSpec((1, LANES), lambda i: (0, 0))],
        out_shape=[jax.ShapeDtypeStruct((l, d), f32), jax.ShapeDtypeStruct((l, d), bf16),
                   jax.ShapeDtypeStruct((1, d), f32), jax.ShapeDtypeStruct((1, LANES), f32)],
        compiler_params=_params(("arbitrary",), 32),
    )(h, w, target)


def _swiglu_fwd(ab, name):
    l, two_f = ab.shape
    fh = two_f // 2
    tr = _tile(l, 256)

    def body(a_ref, b_ref, o_ref):
        o_ref[...] = _b(_silu(a_ref[...]) * b_ref[...])

    return pl.pallas_call(
        body, name=name, grid=(l // tr,),
        in_specs=[pl.BlockSpec((tr, fh), lambda i: (i, 0)), pl.BlockSpec((tr, fh), lambda i: (i, 1))],
        out_specs=pl.BlockSpec((tr, fh), lambda i: (i, 0)),
        out_shape=jax.ShapeDtypeStruct((l, fh), bf16),
        compiler_params=_params(("parallel",), 32),
    )(ab, ab)


def _swiglu_bwd(ab, ds, name):
    l, two_f = ab.shape
    fh = two_f // 2
    tr = _tile(l, 256)

    def body(a_ref, b_ref, ds_ref, o_ref):
        a, bb, g = a_ref[...], b_ref[...], ds_ref[...]
        o_ref[:, :fh] = _b(g * bb * _dsilu(a))
        o_ref[:, fh:] = _b(g * _silu(a))

    lo = pl.BlockSpec((tr, fh), lambda i: (i, 0))
    hi = pl.BlockSpec((tr, fh), lambda i: (i, 1))
    return pl.pallas_call(
        body, name=name, grid=(l // tr,), in_specs=[lo, hi, lo],
        out_specs=pl.BlockSpec((tr, two_f), lambda i: (i, 0)),
        out_shape=jax.ShapeDtypeStruct((l, two_f), bf16),
        compiler_params=_params(("parallel",), 40),
    )(ab, ab, ds)


def _gnorm_fwd(o, proj, nw, heads, dv, gate_blk, name):
    l, hv = o.shape
    tr = _tile(l, 256)

    def body(o_ref, g_ref, w_ref, y_ref):
        wv = w_ref[...]
        for h in range(heads):
            sl = slice(h * dv, (h + 1) * dv)
            oh = o_ref[:, sl]
            r = lax.rsqrt(jnp.mean(oh * oh, axis=-1, keepdims=True) + RMS_EPS)
            y_ref[:, sl] = _b(oh * r * wv * _silu(g_ref[:, sl]))

    return pl.pallas_call(
        body, name=name, grid=(l // tr,),
        in_specs=[pl.BlockSpec((tr, hv), lambda i: (i, 0)), pl.BlockSpec((tr, hv), lambda i: (i, gate_blk)),
                  pl.BlockSpec((1, dv), lambda i: (0, 0))],
        out_specs=pl.BlockSpec((tr, hv), lambda i: (i, 0)),
        out_shape=jax.ShapeDtypeStruct((l, hv), bf16),
        compiler_params=_params(("parallel",), 32),
    )(o, proj, nw)


def _gnorm_bwd(o, proj, nw, dy, heads, dv, gate_blk, name):
    l, hv = o.shape
    tr = _tile(l, 256)

    def body(o_ref, g_ref, w_ref, dy_ref, do_ref, dg_ref, dw_ref):
        wv = w_ref[...]
        dw = jnp.zeros((1, dv), f32)
        for h in range(heads):
            sl = slice(h * dv, (h + 1) * dv)
            oh = o_ref[:, sl]
            g = g_ref[:, sl]
            dyh = dy_ref[:, sl]
            r = lax.rsqrt(jnp.mean(oh * oh, axis=-1, keepdims=True) + RMS_EPS)
            xh = oh * r
            dn = dyh * _silu(g)
            dg_ref[:, sl] = _b(dyh * (xh * wv) * _dsilu(g))
            dxh = dn * wv
            do_ref[:, sl] = r * (dxh - xh * jnp.mean(dxh * xh, axis=-1, keepdims=True))
            dw = dw + jnp.sum(dn * xh, axis=0, keepdims=True)

        @pl.when(pl.program_id(0) == 0)
        def _():
            dw_ref[...] = dw

        @pl.when(pl.program_id(0) > 0)
        def _():
            dw_ref[...] += dw

    row = pl.BlockSpec((tr, hv), lambda i: (i, 0))
    vec = pl.BlockSpec((1, dv), lambda i: (0, 0))
    return pl.pallas_call(
        body, name=name, grid=(l // tr,),
        in_specs=[row, pl.BlockSpec((tr, hv), lambda i: (i, gate_blk)), vec, row],
        out_specs=[row, row, vec],
        out_shape=[jax.ShapeDtypeStruct((l, hv), f32), jax.ShapeDtypeStruct((l, hv), bf16),
                   jax.ShapeDtypeStruct((1, dv), f32)],
        compiler_params=_params(("arbitrary",), 40),
    )(o, proj, nw, dy)


def _ret_prep(proj, cos, sin, name):
    l = proj.shape[0]
    tr = _tile(l, 256)
    half = RET_DK // 2
    scale = RET_DK ** -0.5

    def body(p_ref, c_ref, s_ref, o_ref):
        rows = pl.program_id(0) * tr + lax.broadcasted_iota(jnp.int32, (tr, 1), 0)
        kmul = jnp.where(rows >= PAD, scale, 0.0).astype(f32)
        c, s = c_ref[...], s_ref[...]
        for j in range(2 * RET_HEADS):
            t1 = p_ref[:, j * RET_DK: j * RET_DK + half]
            t2 = p_ref[:, j * RET_DK + half: (j + 1) * RET_DK]
            o1 = t1 * c - t2 * s
            o2 = t1 * s + t2 * c
            if j >= RET_HEADS:
                o1, o2 = o1 * kmul, o2 * kmul
            o_ref[:, j * RET_DK: j * RET_DK + half] = o1
            o_ref[:, j * RET_DK + half: (j + 1) * RET_DK] = o2

    wide = pl.BlockSpec((tr, 2 * RET_QK), lambda i: (i, 0))
    tab = pl.BlockSpec((tr, half), lambda i: (i, 0))
    return pl.pallas_call(
        body, name=name, grid=(l // tr,), in_specs=[wide, tab, tab], out_specs=wide,
        out_shape=jax.ShapeDtypeStruct((l, 2 * RET_QK), f32),
        compiler_params=_params(("parallel",), 32),
    )(proj, cos, sin)


def _ret_prep_bwd(dq, dk, cos, sin, name):
    l = dq.shape[0]
    tr = _tile(l, 256)
    half = RET_DK // 2
    scale = RET_DK ** -0.5

    def body(dq_ref, dk_ref, c_ref, s_ref, o_ref):
        rows = pl.program_id(0) * tr + lax.broadcasted_iota(jnp.int32, (tr, 1), 0)
        kmul = jnp.where(rows >= PAD, scale, 0.0).astype(f32)
        c, s = c_ref[...], s_ref[...]
        for j in range(2 * RET_HEADS):
            d_ref = dq_ref if j < RET_HEADS else dk_ref
            jj = j % RET_HEADS
            d1 = d_ref[:, jj * RET_DK: jj * RET_DK + half]
            d2 = d_ref[:, jj * RET_DK + half: (jj + 1) * RET_DK]
            if j >= RET_HEADS:
                d1, d2 = d1 * kmul, d2 * kmul
            o_ref[:, j * RET_DK: j * RET_DK + half] = _b(d1 * c + d2 * s)
            o_ref[:, j * RET_DK + half: (j + 1) * RET_DK] = _b(d2 * c - d1 * s)

    nar = pl.BlockSpec((tr, RET_QK), lambda i: (i, 0))
    wide = pl.BlockSpec((tr, 2 * RET_QK), lambda i: (i, 0))
    tab = pl.BlockSpec((tr, half), lambda i: (i, 0))
    return pl.pallas_call(
        body, name=name, grid=(l // tr,), in_specs=[nar, nar, tab, tab], out_specs=wide,
        out_shape=jax.ShapeDtypeStruct((l, 2 * RET_QK), bf16),
        compiler_params=_params(("parallel",), 32),
    )(dq, dk, cos, sin)


def _ret_decay(lg):
    idx = lax.broadcasted_iota(jnp.int32, (CHUNK, 1), 0).astype(f32)
    ri = lax.broadcasted_iota(jnp.int32, (CHUNK, CHUNK), 0)
    ci = lax.broadcasted_iota(jnp.int32, (CHUNK, CHUNK), 1)
    rel = (ri - ci).astype(f32)
    dmask = jnp.where(ri >= ci, jnp.exp(lg * jnp.maximum(rel, 0.0)), 0.0)
    xi = jnp.exp(lg * (idx + 1.0))
    zeta = jnp.exp(lg * (CHUNK - 1.0 - idx))
    return dmask, xi, zeta


def _ret_scan_fwd(qk, proj, lgs, gcs, name):
    l = qk.shape[0]
    nch = l // CHUNK

    def body(lg_ref, gc_ref, q_ref, k_ref, v_ref, o_ref, st_ref, s_ref):
        h, n = pl.program_id(0), pl.program_id(1)

        @pl.when(n == 0)
        def _():
            s_ref[...] = jnp.zeros_like(s_ref)

        dmask, xi, zeta = _ret_decay(lg_ref[h])
        q, k, v = q_ref[...], k_ref[...], v_ref[...]
        s = s_ref[...]
        sb = _b(s)
        st_ref[0, 0] = sb
        scores = _nt(_b(q), _b(k)) * dmask
        o_ref[...] = _nn(_b(scores), _b(v)) + _nn(_b(q * xi), sb)
        s_ref[...] = gc_ref[h] * s + _tn(_b(k * zeta), _b(v))

    smem = pl.BlockSpec(memory_space=pltpu.SMEM)
    return pl.pallas_call(
        body, name=name, grid=(RET_HEADS, nch),
        in_specs=[smem, smem,
                  pl.BlockSpec((CHUNK, RET_DK), lambda h, n: (n, h)),
                  pl.BlockSpec((CHUNK, RET_DK), lambda h, n: (n, RET_HEADS + h)),
                  pl.BlockSpec((CHUNK, RET_DV), lambda h, n: (n, RET_HEADS + h))],
        out_specs=[pl.BlockSpec((CHUNK, RET_DV), lambda h, n: (n, h)),
                   pl.BlockSpec((1, 1, RET_DK, RET_DV), lambda h, n: (n, h, 0, 0))],
        out_shape=[jax.ShapeDtypeStruct((l, RET_V), f32),
                   jax.ShapeDtypeStruct((nch, RET_HEADS, RET_DK, RET_DV), bf16)],
        scratch_shapes=[pltpu.VMEM((RET_DK, RET_DV), f32)],
        compiler_params=_params(("arbitrary", "arbitrary"), 32),
    )(lgs, gcs, qk, qk, proj)


def _ret_scan_bwd(qk, proj, states, do, lgs, gcs, name):
    l = qk.shape[0]
    nch = l // CHUNK

    def body(lg_ref, gc_ref, q_ref, k_ref, v_ref, st_ref, do_ref, dq_ref, dk_ref, dv_ref, ds_ref):
        h, step = pl.program_id(0), pl.program_id(1)

        @pl.when(step == 0)
        def _():
            ds_ref[...] = jnp.zeros_like(ds_ref)

        dmask, xi, zeta = _ret_decay(lg_ref[h])
        q, k, v, do_ = q_ref[...], k_ref[...], v_ref[...], do_ref[...]
        qb, kb, vb, dob = _b(q), _b(k), _b(v), _b(do_)
        sb = st_ref[0, 0]
        dsp = ds_ref[...]
        dspb = _b(dsp)
        scores = _nt(qb, kb) * dmask
        dscores = _nt(dob, vb) * dmask
        dq_ref[...] = _nn(_b(dscores), kb) + _nt(dob, sb) * xi
        dk_ref[...] = _tn(_b(dscores), qb) + _nt(vb, dspb) * zeta
        dv_ref[...] = _b(_tn(_b(scores), dob) + _nn(_b(k * zeta), dspb))
        ds_ref[...] = gc_ref[h] * dsp + _tn(_b(q * xi), dob)

    smem = pl.BlockSpec(memory_space=pltpu.SMEM)
    rev = lambda n: nch - 1 - n
    return pl.pallas_call(
        body, name=name, grid=(RET_HEADS, nch),
        in_specs=[smem, smem,
                  pl.BlockSpec((CHUNK, RET_DK), lambda h, n: (rev(n), h)),
                  pl.BlockSpec((CHUNK, RET_DK), lambda h, n: (rev(n), RET_HEADS + h)),
                  pl.BlockSpec((CHUNK, RET_DV), lambda h, n: (rev(n), RET_HEADS + h)),
                  pl.BlockSpec((1, 1, RET_DK, RET_DV), lambda h, n: (rev(n), h, 0, 0)),
                  pl.BlockSpec((CHUNK, RET_DV), lambda h, n: (rev(n), h))],
        out_specs=[pl.BlockSpec((CHUNK, RET_DK), lambda h, n: (rev(n), h)),
                   pl.BlockSpec((CHUNK, RET_DK), lambda h, n: (rev(n), h)),
                   pl.BlockSpec((CHUNK, RET_DV), lambda h, n: (rev(n), h))],
        out_shape=[jax.ShapeDtypeStruct((l, RET_QK), f32), jax.ShapeDtypeStruct((l, RET_QK), f32),
                   jax.ShapeDtypeStruct((l, RET_V), bf16)],
        scratch_shapes=[pltpu.VMEM((RET_DK, RET_DV), f32)],
        compiler_params=_params(("arbitrary", "arbitrary"), 32),
    )(lgs, gcs, qk, qk, proj, states, do)


CONV_BLK = 1024
HALO = 8


def _shift_down(cur, prev, s):
    full = pltpu.roll(cur, s, 0)
    hr = pltpu.roll(cur[:HALO], s, 0)
    pr = pltpu.roll(prev, s, 0)
    rows = lax.broadcasted_iota(jnp.int32, (HALO, 1), 0)
    return full, jnp.where(rows < s, pr, hr)


def _conv_rows(x_ref, p_ref, w_ref, c_ref, i, tr):
    rows = i * tr + lax.broadcasted_iota(jnp.int32, (tr, 1), 0)
    cur = jnp.where(rows >= PAD, x_ref[...], 0.0)
    prow = i * tr - HALO + lax.broadcasted_iota(jnp.int32, (HALO, 1), 0)
    prev = jnp.where(prow >= PAD, p_ref[...], 0.0)
    w = w_ref[...]
    acc = cur * w[CONV_K - 1:CONV_K, :]
    head = cur[:HALO] * w[CONV_K - 1:CONV_K, :]
    for s in range(1, CONV_K):
        full, hd = _shift_down(cur, prev, s)
        wk = w[CONV_K - 1 - s:CONV_K - s, :]
        acc = acc + full * wk
        head = head + hd * wk
    c_ref[...] = acc
    c_ref[0:HALO, :] = head


def _l2_heads(a, scale):
    outs = []
    for h in range(CONV_BLK // DN_DK):
        ah = a[:, h * DN_DK:(h + 1) * DN_DK]
        outs.append(ah * (lax.rsqrt(jnp.sum(ah * ah, axis=-1, keepdims=True) + RMS_EPS) * scale))
    return outs


def _dn_conv_fwd(proj, conv_w, name):
    l = proj.shape[0]
    tr = _tile(l, 256)
    nblk = DN_CONV_CH // CONV_BLK

    def body(x_ref, p_ref, w_ref, o_ref, c_ref):
        i, j = pl.program_id(0), pl.program_id(1)
        _conv_rows(x_ref, p_ref, w_ref, c_ref, i, tr)
        a = _silu(c_ref[...])

        @pl.when(j == 0)
        def _():
            for h, v in enumerate(_l2_heads(a, DN_DK ** -0.5)):
                o_ref[:, h * DN_DK:(h + 1) * DN_DK] = v

        @pl.when(j == 1)
        def _():
            for h, v in enumerate(_l2_heads(a, 1.0)):
                o_ref[:, h * DN_DK:(h + 1) * DN_DK] = v

        @pl.when(j >= 2)
        def _():
            o_ref[...] = a

    hb = tr // HALO
    return pl.pallas_call(
        body, name=name, grid=(l // tr, nblk),
        in_specs=[pl.BlockSpec((tr, CONV_BLK), lambda i, j: (i, j)),
                  pl.BlockSpec((HALO, CONV_BLK), lambda i, j: (jnp.maximum(i * hb - 1, 0), j)),
                  pl.BlockSpec((CONV_K, CONV_BLK), lambda i, j: (0, j))],
        out_specs=pl.BlockSpec((tr, CONV_BLK), lambda i, j: (i, j)),
        out_shape=jax.ShapeDtypeStruct((l, DN_CONV_CH), f32),
        scratch_shapes=[pltpu.VMEM((tr, CONV_BLK), f32)],
        compiler_params=_params(("parallel", "parallel"), 32),
    )(proj, proj, conv_w)


def _dn_conv_bwd_a(proj, conv_w, dqkv, name):
    l = proj.shape[0]
    tr = _tile(l, 256)
    nblk = DN_CONV_CH // CONV_BLK

    def body(x_ref, p_ref, w_ref, d_ref, dc_ref, dw_ref, c_ref):
        j, i = pl.program_id(0), pl.program_id(1)
        _conv_rows(x_ref, p_ref, w_ref, c_ref, i, tr)
        c = c_ref[...]
        a = _silu(c)
        dsl = _dsilu(c)

        def l2_bwd(scale):
            for h in range(CONV_BLK // DN_DK):
                sl = slice(h * DN_DK, (h + 1) * DN_DK)
                ah = a[:, sl]
                r = lax.rsqrt(jnp.sum(ah * ah, axis=-1, keepdims=True) + RMS_EPS)
                yh = ah * r
                dy = d_ref[:, sl]
                da = (r * scale) * (dy - yh * jnp.sum(dy * yh, axis=-1, keepdims=True))
                dc_ref[:, sl] = da * dsl[:, sl]

        @pl.when(j == 0)
        def _():
            l2_bwd(DN_DK ** -0.5)

        @pl.when(j == 1)
        def _():
            l2_bwd(1.0)

        @pl.when(j >= 2)
        def _():
            dc_ref[...] = d_ref[...] * dsl

        dc = dc_ref[...]
        rows = i * tr + lax.broadcasted_iota(jnp.int32, (tr, 1), 0)
        cur = jnp.where(rows >= PAD, x_ref[...], 0.0)
        prow = i * tr - HALO + lax.broadcasted_iota(jnp.int32, (HALO, 1), 0)
        prev = jnp.where(prow >= PAD, p_ref[...], 0.0)
        hsel = lax.broadcasted_iota(jnp.int32, (tr, 1), 0) >= HALO
        parts = [None] * CONV_K
        parts[CONV_K - 1] = jnp.sum(dc * cur, axis=0, keepdims=True)
        for s in range(1, CONV_K):
            full, hd = _shift_down(cur, prev, s)
            tot = jnp.sum(jnp.where(hsel, dc * full, 0.0), axis=0, keepdims=True)
            tot = tot + jnp.sum(dc[:HALO] * hd, axis=0, keepdims=True)
            parts[CONV_K - 1 - s] = tot
        ksel = lax.broadcasted_iota(jnp.int32, (CONV_K, 1), 0)
        dw = jnp.zeros((CONV_K, CONV_BLK), f32)
        for k in range(CONV_K):
            dw = dw + jnp.where(ksel == k, parts[k], 0.0)

        @pl.when(i == 0)
        def _():
            dw_ref[...] = dw

        @pl.when(i > 0)
        def _():
            dw_ref[...] += dw

    hb = tr // HALO
    blk = pl.BlockSpec((tr, CONV_BLK), lambda j, i: (i, j))
    return pl.pallas_call(
        body, name=name, grid=(nblk, l // tr),
        in_specs=[blk, pl.BlockSpec((HALO, CONV_BLK), lambda j, i: (jnp.maximum(i * hb - 1, 0), j)),
                  pl.BlockSpec((CONV_K, CONV_BLK), lambda j, i: (0, j)), blk],
        out_specs=[blk, pl.BlockSpec((CONV_K, CONV_BLK), lambda j, i: (0, j))],
        out_shape=[jax.ShapeDtypeStruct((l, DN_CONV_CH), f32), jax.ShapeDtypeStruct((CONV_K, DN_CONV_CH), f32)],
        scratch_shapes=[pltpu.VMEM((tr, CONV_BLK), f32)],
        compiler_params=_params(("parallel", "arbitrary"), 40),
    )(proj, proj, conv_w, dqkv)


def _dn_conv_bwd_b(dc, conv_w, name):
    l = dc.shape[0]
    tr = _tile(l, 256)
    nblk = DN_CONV_CH // CONV_BLK
    nrow = l // tr

    def body(d_ref, n_ref, w_ref, o_ref, t_ref):
        i = pl.program_id(0)
        cur = d_ref[...]
        nxt = jnp.where(i < nrow - 1, n_ref[...], 0.0)
        w = w_ref[...]
        acc = cur * w[CONV_K - 1:CONV_K, :]
        tail = cur[tr - HALO:] * w[CONV_K - 1:CONV_K, :]
        rows8 = lax.broadcasted_iota(jnp.int32, (HALO, 1), 0)
        for s in range(1, CONV_K):
            wk = w[CONV_K - 1 - s:CONV_K - s, :]
            acc = acc + pltpu.roll(cur, tr - s, 0) * wk
            tl = jnp.where(rows8 >= HALO - s, pltpu.roll(nxt, HALO - s, 0), pltpu.roll(cur[tr - HALO:], HALO - s, 0))
            tail = tail + tl * wk
        t_ref[...] = acc
        t_ref[tr - HALO:, :] = tail
        rows = i * tr + lax.broadcasted_iota(jnp.int32, (tr, 1), 0)
        o_ref[...] = _b(jnp.where(rows >= PAD, t_ref[...], 0.0))

    hb = tr // HALO
    nh = l // HALO
    return pl.pallas_call(
        body, name=name, grid=(nrow, nblk),
        in_specs=[pl.BlockSpec((tr, CONV_BLK), lambda i, j: (i, j)),
                  pl.BlockSpec((HALO, CONV_BLK), lambda i, j: (jnp.minimum((i + 1) * hb, nh - 1), j)),
                  pl.BlockSpec((CONV_K, CONV_BLK), lambda i, j: (0, j))],
        out_specs=pl.BlockSpec((tr, CONV_BLK), lambda i, j: (i, j)),
        out_shape=jax.ShapeDtypeStruct((l, DN_CONV_CH), bf16),
        scratch_shapes=[pltpu.VMEM((tr, CONV_BLK), f32)],
        compiler_params=_params(("parallel", "parallel"), 32),
    )(dc, dc, conv_w)


BA_W = LANES


def _dn_gates(ba_ref, al_ref, dt_ref, n):
    rows = n * CHUNK + lax.broadcasted_iota(jnp.int32, (CHUNK, 1), 0)
    vm = (rows >= PAD).astype(f32)
    bin_ = ba_ref[:, 0:DN_HEADS]
    z = ba_ref[:, DN_HEADS:2 * DN_HEADS] + dt_ref[...]
    sp = jnp.maximum(z, 0.0) + jnp.log1p(jnp.exp(-jnp.abs(z)))
    ea = jnp.exp(al_ref[...])
    beta = _sigmoid(bin_) * vm
    g = -ea * sp * vm
    return vm, bin_, z, ea, beta, g


def _tri(lower_incl):
    ri = lax.broadcasted_iota(jnp.int32, (CHUNK, CHUNK), 0)
    ci = lax.broadcasted_iota(jnp.int32, (CHUNK, CHUNK), 1)
    return ri, ci


def _unit_lower_inverse(a):
    ri, ci = _tri(True)
    eye = (ri == ci).astype(f32)
    t = eye - a
    p = a
    steps = int(math.log2(CHUNK)) - 1
    for _ in range(steps):
        p = _nn(p, p, HI)
        t = t + _nn(t, p, HI)
    return t


def _dn_scan_fwd(qkv, ba, a_log, dt_bias, name):
    l = qkv.shape[0]
    nch = l // CHUNK

    def body(q_ref, k_ref, v_ref, ba_ref, al_ref, dt_ref, o_ref, st_ref, t_ref, u_ref, w_ref, s_ref):
        n = pl.program_id(0)

        @pl.when(n == 0)
        def _():
            s_ref[...] = jnp.zeros_like(s_ref)

        _, _, _, _, beta, g = _dn_gates(ba_ref, al_ref, dt_ref, n)
        ri, ci = _tri(True)
        incl, strict = ri >= ci, ri > ci
        gam = _nn(incl.astype(f32), g, HI)
        gam_t = gam.T
        for h in range(DN_HEADS):
            gc, gr, bh = gam[:, h:h + 1], gam_t[h:h + 1, :], beta[:, h:h + 1]
            qh = q_ref[:, h * DN_DK:(h + 1) * DN_DK]
            kh = k_ref[:, h * DN_DK:(h + 1) * DN_DK]
            vh = v_ref[:, h * DN_DV:(h + 1) * DN_DV]
            decay = jnp.exp(jnp.where(incl, gc - gr, -jnp.inf))
            kb = _b(kh)
            kk = _nt(kb, kb)
            a = jnp.where(strict, bh * kk * decay, 0.0)
            tinv = _unit_lower_inverse(a)
            eg = jnp.exp(gc)
            u = _nn(tinv, vh * bh, HI)
            w = _nn(tinv, kh * (bh * eg), HI)
            s = s_ref[h]
            sb = _b(s)
            vn = u - _nn(_b(w), sb)
            p = _nt(_b(qh), kb) * decay
            o_ref[:, h * DN_DV:(h + 1) * DN_DV] = _nn(_b(qh * eg), sb) + _nn(_b(p), _b(vn))
            gl = gc[CHUNK - 1:CHUNK, :]
            s_ref[h] = s * jnp.exp(gl) + _tn(_b(kh * jnp.exp(gl - gc)), _b(vn))
            st_ref[0, h] = sb
            t_ref[0, h] = tinv
            u_ref[:, h * DN_DV:(h + 1) * DN_DV] = u
            w_ref[:, h * DN_DK:(h + 1) * DN_DK] = w

    vec = pl.BlockSpec((1, DN_HEADS), lambda n: (0, 0))
    return pl.pallas_call(
        body, name=name, grid=(nch,),
        in_specs=[pl.BlockSpec((CHUNK, DN_QK), lambda n: (n, 0)), pl.BlockSpec((CHUNK, DN_QK), lambda n: (n, 1)),
                  pl.BlockSpec((CHUNK, DN_V), lambda n: (n, 1)), pl.BlockSpec((CHUNK, BA_W), lambda n: (n, 0)),
                  vec, vec],
        out_specs=[pl.BlockSpec((CHUNK, DN_V), lambda n: (n, 0)),
                   pl.BlockSpec((1, DN_HEADS, DN_DK, DN_DV), lambda n: (n, 0, 0, 0)),
                   pl.BlockSpec((1, DN_HEADS, CHUNK, CHUNK), lambda n: (n, 0, 0, 0)),
                   pl.BlockSpec((CHUNK, DN_V), lambda n: (n, 0)),
                   pl.BlockSpec((CHUNK, DN_QK), lambda n: (n, 0))],
        out_shape=[jax.ShapeDtypeStruct((l, DN_V), f32),
                   jax.ShapeDtypeStruct((nch, DN_HEADS, DN_DK, DN_DV), bf16),
                   jax.ShapeDtypeStruct((nch, DN_HEADS, CHUNK, CHUNK), f32),
                   jax.ShapeDtypeStruct((l, DN_V), f32), jax.ShapeDtypeStruct((l, DN_QK), f32)],
        scratch_shapes=[pltpu.VMEM((DN_HEADS, DN_DK, DN_DV), f32)],
        compiler_params=_params(("arbitrary",), 40),
    )(qkv, qkv, qkv, ba, a_log, dt_bias)


def _dn_scan_bwd(qkv, ba, a_log, dt_bias, states, tinv_all, u_all, w_all, do, name):
    l = qkv.shape[0]
    nch = l // CHUNK

    def body(q_ref, k_ref, v_ref, ba_ref, al_ref, dt_ref, st_ref, t_ref, u_ref, w_ref, do_ref,
             dqkv_ref, dba_ref, dal_ref, ddt_ref, ds_ref):
        step = pl.program_id(0)
        n = nch - 1 - step

        @pl.when(step == 0)
        def _():
            ds_ref[...] = jnp.zeros_like(ds_ref)

        vm, bin_, z, ea, beta, g = _dn_gates(ba_ref, al_ref, dt_ref, n)
        ri, ci = _tri(True)
        incl, strict = ri >= ci, ri > ci
        gam = _nn(incl.astype(f32), g, HI)
        gam_t = gam.T
        lane8 = lax.broadcasted_iota(jnp.int32, (1, DN_HEADS), 1)
        sub8 = lax.broadcasted_iota(jnp.int32, (DN_HEADS, 1), 0)
        dbeta = jnp.zeros((CHUNK, DN_HEADS), f32)
        dgam = jnp.zeros((CHUNK, DN_HEADS), f32)
        dgam_neg_t = jnp.zeros((DN_HEADS, CHUNK), f32)
        last = (lax.broadcasted_iota(jnp.int32, (CHUNK, 1), 0) == CHUNK - 1).astype(f32)
        for h in range(DN_HEADS):
            gc, gr, bh = gam[:, h:h + 1], gam_t[h:h + 1, :], beta[:, h:h + 1]
            qh = q_ref[:, h * DN_DK:(h + 1) * DN_DK]
            kh = k_ref[:, h * DN_DK:(h + 1) * DN_DK]
            vh = v_ref[:, h * DN_DV:(h + 1) * DN_DV]
            doh = _b(do_ref[:, h * DN_DV:(h + 1) * DN_DV])
            u = u_ref[:, h * DN_DV:(h + 1) * DN_DV]
            w = w_ref[:, h * DN_DK:(h + 1) * DN_DK]
            tinv = t_ref[0, h]
            sb = st_ref[0, h]
            dsp = ds_ref[h]
            dspb = _b(dsp)
            decay = jnp.exp(jnp.where(incl, gc - gr, -jnp.inf))
            qb, kb = _b(qh), _b(kh)
            kk = _nt(kb, kb)
            a = jnp.where(strict, bh * kk * decay, 0.0)
            eg = jnp.exp(gc)
            gl = gc[CHUNK - 1:CHUNK, :]
            egl = jnp.exp(gl)
            ekd = jnp.exp(gl - gc)
            wb = _b(w)
            vn = u - _nn(wb, sb)
            vnb = _b(vn)
            p = _nt(qb, kb) * decay
            qe, ke = qh * eg, kh * ekd
            dvn = _tn(_b(p), doh) + _nn(_b(ke), dspb)
            dvnb = _b(dvn)
            dpraw = _nt(doh, vnb)
            dqk = _b(dpraw * decay)
            dqe = _nt(doh, sb)
            dke = _nt(vnb, dspb)
            dqkv_ref[:, h * DN_DK:(h + 1) * DN_DK] = _nn(dqk, kb) + dqe * eg
            dk = _tn(dqk, qb) + dke * ekd
            ds_ref[h] = egl * dsp + _tn(_b(qe), doh) - _tn(wb, dvnb)
            dw = -_nt(dvnb, sb)
            dru = _tn(tinv, dvn, HI)
            drw = _tn(tinv, dw, HI)
            da = -(_nt(_b(dru), _b(u)) + _nt(_b(drw), wb))
            da = jnp.where(strict, da, 0.0)
            dqkv_ref[:, 2 * DN_QK + h * DN_DV:2 * DN_QK + (h + 1) * DN_DV] = bh * dru
            keg = kh * eg
            db_h = (jnp.sum(dru * vh, axis=-1, keepdims=True) + jnp.sum(drw * keg, axis=-1, keepdims=True)
                    + jnp.sum(da * kk * decay, axis=-1, keepdims=True))
            dkk = _b(da * bh * decay)
            dk = dk + (bh * eg) * drw + _nn(dkk, kb) + _tn(dkk, kb)
            dqkv_ref[:, DN_QK + h * DN_DK:DN_QK + (h + 1) * DN_DK] = dk
            mm = da * a + dpraw * p
            dgl = (jnp.sum(jnp.sum(dke * ke, axis=-1, keepdims=True), axis=0, keepdims=True)
                   + egl * jnp.sum(jnp.sum(dsp * sb.astype(f32), axis=-1, keepdims=True), axis=0, keepdims=True))
            dg_h = (jnp.sum(mm, axis=-1, keepdims=True) + jnp.sum(drw * keg, axis=-1, keepdims=True) * bh
                    + jnp.sum(dqe * qe, axis=-1, keepdims=True) - jnp.sum(dke * ke, axis=-1, keepdims=True)
                    + last * dgl)
            dbeta = dbeta + jnp.where(lane8 == h, db_h, 0.0)
            dgam = dgam + jnp.where(lane8 == h, dg_h, 0.0)
            dgam_neg_t = dgam_neg_t + jnp.where(sub8 == h, jnp.sum(mm, axis=0, keepdims=True), 0.0)
        dgam = dgam - dgam_neg_t.T
        dg = _nn((ri <= ci).astype(f32), dgam, HI)
        sg = _sigmoid(bin_)
        dbin = dbeta * vm * sg * (1.0 - sg)
        dain = dg * (-ea) * vm * _sigmoid(z)
        dba_ref[...] = jnp.zeros_like(dba_ref)
        dba_ref[:, 0:DN_HEADS] = dbin
        dba_ref[:, DN_HEADS:2 * DN_HEADS] = dain
        dal = jnp.sum(dg * g, axis=0, keepdims=True)
        ddt = jnp.sum(dain, axis=0, keepdims=True)

        @pl.when(step == 0)
        def _():
            dal_ref[...] = dal
            ddt_ref[...] = ddt

        @pl.when(step > 0)
        def _():
            dal_ref[...] += dal
            ddt_ref[...] += ddt

    rev = lambda s: nch - 1 - s
    vec = pl.BlockSpec((1, DN_HEADS), lambda s: (0, 0))
    qs = pl.BlockSpec((CHUNK, DN_QK), lambda s: (rev(s), 0))
    ks = pl.BlockSpec((CHUNK, DN_QK), lambda s: (rev(s), 1))
    vs = pl.BlockSpec((CHUNK, DN_V), lambda s: (rev(s), 1))
    v0 = pl.BlockSpec((CHUNK, DN_V), lambda s: (rev(s), 0))
    return pl.pallas_call(
        body, name=name, grid=(nch,),
        in_specs=[qs, ks, vs, pl.BlockSpec((CHUNK, BA_W), lambda s: (rev(s), 0)), vec, vec,
                  pl.BlockSpec((1, DN_HEADS, DN_DK, DN_DV), lambda s: (rev(s), 0, 0, 0)),
                  pl.BlockSpec((1, DN_HEADS, CHUNK, CHUNK), lambda s: (rev(s), 0, 0, 0)),
                  v0, qs, v0],
        out_specs=[pl.BlockSpec((CHUNK, DN_CONV_CH), lambda s: (rev(s), 0)),
                   pl.BlockSpec((CHUNK, BA_W), lambda s: (rev(s), 0)), vec, vec],
        out_shape=[jax.ShapeDtypeStruct((l, DN_CONV_CH), f32), jax.ShapeDtypeStruct((l, BA_W), f32),
                   jax.ShapeDtypeStruct((1, DN_HEADS), f32), jax.ShapeDtypeStruct((1, DN_HEADS), f32)],
        scratch_shapes=[pltpu.VMEM((DN_HEADS, DN_DK, DN_DV), f32)],
        compiler_params=_params(("arbitrary",), 48),
    )(qkv, qkv, qkv, ba, a_log, dt_bias, states, tinv_all, u_all, w_all, do)


def _ffn_fwd(h, nw, wgu, wd, tb, th, tag):
    d = h.shape[1]
    fh = wd.shape[0]
    hn = _rms_fwd(h, nw, f"{tag}_norm")
    ab = _matmul(hn, wgu, mode="nn", tm=tb, tn=512, tk=d, name=f"{tag}_gu")
    s = _swiglu_fwd(ab, f"{tag}_act")
    out = _matmul(s, wd, mode="nn", tm=th, tn=512, tk=fh // 2, res=h, name=f"{tag}_down")
    return out, (hn, ab, s)


def _ffn_bwd(dh, dhb, h, nw, wgu, wd, saved, tb, th, tag):
    hn, ab, s = saved
    d = h.shape[1]
    fh = wd.shape[0]
    ds = _matmul(dhb, wd, mode="nt", tm=th, tn=fh // 2, tk=d, name=f"{tag}_b_ds")
    dwd = _matmul(s, dhb, mode="tn", tm=fh // 2, tn=512, tk=th, name=f"{tag}_b_dwd")
    dab = _swiglu_bwd(ab, ds, f"{tag}_b_act")
    dhn = _matmul(dab, wgu, mode="nt", tm=th, tn=d, tk=512, name=f"{tag}_b_dhn")
    dwgu = _matmul(hn, dab, mode="tn", tm=d, tn=512, tk=tb, name=f"{tag}_b_dwgu")
    dh2, dh2b, dnw = _rms_bwd(h, nw, dhn, dh, f"{tag}_b_norm")
    return dh2, dh2b, dnw, dwgu, dwd


def _local_step(x2, target, wts):
    s_len, d = x2.shape
    l = s_len + CHUNK
    tb = _tile(l, 3072)
    th = tb // 2 if (tb // 2) % 16 == 0 else tb
    half = RET_DK // 2
    inv_freq = ROPE_BASE ** (-jnp.arange(half, dtype=f32) / half)
    ang = (jnp.arange(l) - PAD).astype(f32)[:, None] * inv_freq[None, :]
    cos, sin = jnp.cos(ang), jnp.sin(ang)
    lgs = jnp.log1p(-jnp.exp2(-5.0 - jnp.arange(RET_HEADS, dtype=f32)))
    gcs = jnp.exp(lgs * CHUNK)

    h0 = jnp.concatenate([jnp.zeros((PAD, d), f32), wts["meta"], x2], axis=0)
    mixw, ffnw = wts["mix_norm"], wts["ffn_norm"]

    hn0 = _rms_fwd(h0, mixw[0:1], "l0_norm")
    proj0 = _matmul(hn0, wts["ret_in"], mode="nn", tm=tb, tn=512, tk=d, name="ret_proj")
    qk0 = _ret_prep(proj0, cos, sin, "ret_prep")
    o0, st0 = _ret_scan_fwd(qk0, proj0, lgs, gcs, "ret_scan")
    y0 = _gnorm_fwd(o0, proj0, wts["ret_gn"], RET_HEADS, RET_DV, 2, "ret_gnorm")
    h1 = _matmul(y0, wts["ret_out"], mode="nn", tm=th, tn=512, tk=RET_V, res=h0, name="ret_out")
    h2, ffn0 = _ffn_fwd(h1, ffnw[0:1], wts["gu0"], wts["down0"], tb, th, "ffn0")

    hn2 = _rms_fwd(h2, mixw[1:2], "l1_norm")
    proj1 = _matmul(hn2, wts["dn_main"], mode="nn", tm=tb, tn=512, tk=d, name="dn_proj")
    ba = _matmul(hn2, wts["dn_ba"], mode="nn", tm=tb, tn=BA_W, tk=d, name="dn_proj_ba")
    qkv1 = _dn_conv_fwd(proj1, wts["conv_w"], "dn_conv")
    o1, st1, tinv1, u1, w1 = _dn_scan_fwd(qkv1, ba, wts["a_log"], wts["dt_bias"], "dn_scan")
    y1 = _gnorm_fwd(o1, proj1, wts["dn_norm"], DN_HEADS, DN_DV, 2, "dn_gnorm")
    h3 = _matmul(y1, wts["dn_out"], mode="nn", tm=th, tn=512, tk=DN_V, res=h2, name="dn_out")
    h4, ffn1 = _ffn_fwd(h3, ffnw[1:2], wts["gu1"], wts["down1"], tb, th, "ffn1")

    dh4, dh4b, dfinal, loss = _final_loss(h4, wts["final_norm"], target, "final_loss")
    dh3, dh3b, dffn1, dwgu1, dwd1 = _ffn_bwd(dh4, dh4b, h3, ffnw[1:2], wts["gu1"], wts["down1"], ffn1, tb, th, "ffn1")

    dy1 = _matmul(dh3b, wts["dn_out"], mode="nt", tm=th, tn=1024, tk=d, name="dn_b_dy")
    dw_dn_out = _matmul(y1, dh3b, mode="tn", tm=1024, tn=512, tk=tb, name="dn_b_dwout")
    do1, dgate1, ddn_norm = _gnorm_bwd(o1, proj1, wts["dn_norm"], dy1, DN_HEADS, DN_DV, 2, "dn_b_gnorm")
    dqkv1, dba, dalog, ddt = _dn_scan_bwd(qkv1, ba, wts["a_log"], wts["dt_bias"], st1, tinv1, u1, w1, do1, "dn_b_scan")
    dc1, dconv = _dn_conv_bwd_a(proj1, wts["conv_w"], dqkv1, "dn_b_conv_a")
    dx1 = _dn_conv_bwd_b(dc1, wts["conv_w"], "dn_b_conv_b")
    dproj1 = jnp.concatenate([dx1, dgate1, dba.astype(bf16)], axis=1)
    w_dn_full = jnp.concatenate([wts["dn_main"], wts["dn_ba"]], axis=1)
    nfull = dproj1.shape[1]
    tkf = _div_tile(nfull, 1024, LANES)
    dhn2 = _matmul(dproj1, w_dn_full, mode="nt", tm=th, tn=d, tk=tkf, name="dn_b_dhn")
    dw_dn_in = _matmul(hn2, dproj1, mode="tn", tm=d, tn=tkf, tk=tb, name="dn_b_dwin")
    dh2, dh2b, dmix1 = _rms_bwd(h2, mixw[1:2], dhn2, dh3, "l1_b_norm")

    dh1, dh1b, dffn0, dwgu0, dwd0 = _ffn_bwd(dh2, dh2b, h1, ffnw[0:1], wts["gu0"], wts["down0"], ffn0, tb, th, "ffn0")

    dy0 = _matmul(dh1b, wts["ret_out"], mode="nt", tm=th, tn=1024, tk=d, name="ret_b_dy")
    dw_ret_out = _matmul(y0, dh1b, mode="tn", tm=1024, tn=512, tk=tb, name="ret_b_dwout")
    do0, dgate0, dret_gn = _gnorm_bwd(o0, proj0, wts["ret_gn"], dy0, RET_HEADS, RET_DV, 2, "ret_b_gnorm")
    dq0, dk0, dv0 = _ret_scan_bwd(qk0, proj0, st0, do0, lgs, gcs, "ret_b_scan")
    dqk0 = _ret_prep_bwd(dq0, dk0, cos, sin, "ret_b_prep")
    dproj0 = jnp.concatenate([dqk0, dv0, dgate0], axis=1)
    dhn0 = _matmul(dproj0, wts["ret_in"], mode="nt", tm=th, tn=d, tk=512, name="ret_b_dhn")
    dw_ret_in = _matmul(hn0, dproj0, mode="tn", tm=d, tn=512, tk=tb, name="ret_b_dwin")
    dh0, _, dmix0 = _rms_bwd(h0, mixw[0:1], dhn0, dh1, "l0_b_norm")

    grads = dict(
        meta=dh0[PAD:CHUNK], mix_norm=jnp.concatenate([dmix0, dmix1], axis=0),
        ffn_norm=jnp.concatenate([dffn0, dffn1], axis=0), ret_in=dw_ret_in, ret_gn=dret_gn, ret_out=dw_ret_out,
        dn_in=dw_dn_in, conv_w=dconv, a_log=dalog, dt_bias=ddt, dn_norm=ddn_norm, dn_out=dw_dn_out,
        gu0=dwgu0, gu1=dwgu1, down0=dwd0, down1=dwd1, final_norm=dfinal)
    return loss, dh0[CHUNK:], grads


def _peer(k):
    x, y, c = lax.axis_index("x"), lax.axis_index("y"), lax.axis_index("c")
    px = 1 - x if k & 4 else x
    py = 1 - y if k & 2 else y
    pc = 1 - c if k & 1 else c
    return (px, py, pc), 4 * px + 2 * py + pc


def _exchange(arrs, gather, name):
    n = len(arrs)
    out_shapes = [jax.ShapeDtypeStruct(((N_DEV,) + a.shape) if gather else a.shape, a.dtype) for a in arrs]

    def body(*refs):
        ins, outs = refs[:n], refs[n:2 * n]
        send_sems, recv_sems, local_sems = refs[2 * n:]
        me = 4 * lax.axis_index("x") + 2 * lax.axis_index("y") + lax.axis_index("c")

        def src(a, dest_idx):
            return ins[a] if gather else ins[a].at[dest_idx]

        local = [pltpu.make_async_copy(src(a, me), outs[a].at[me], local_sems.at[a]) for a in range(n)]
        for cp in local:
            cp.start()
        sends = []
        for k in range(1, N_DEV):
            peer, pidx = _peer(k)
            for a in range(n):
                cp = pltpu.make_async_remote_copy(
                    src_ref=src(a, pidx), dst_ref=outs[a].at[me],
                    send_sem=send_sems.at[a, k - 1], recv_sem=recv_sems.at[a, k - 1],
                    device_id=peer, device_id_type=pl.DeviceIdType.MESH)
                cp.start()
                sends.append(cp)
        for k in range(1, N_DEV):
            peer, pidx = _peer(k)
            for a in range(n):
                pltpu.make_async_remote_copy(
                    src_ref=src(a, pidx), dst_ref=outs[a].at[pidx],
                    send_sem=send_sems.at[a, k - 1], recv_sem=recv_sems.at[a, k - 1],
                    device_id=peer, device_id_type=pl.DeviceIdType.MESH).wait_recv()
        for cp in sends:
            cp.wait_send()
        for cp in local:
            cp.wait()

    hbm = pl.BlockSpec(memory_space=pltpu.HBM)
    return pl.pallas_call(
        body, name=name, in_specs=[hbm] * n, out_specs=[hbm] * n, out_shape=out_shapes,
        scratch_shapes=[pltpu.SemaphoreType.DMA((n, N_DEV - 1)), pltpu.SemaphoreType.DMA((n, N_DEV - 1)),
                        pltpu.SemaphoreType.DMA((n,))],
    )(*arrs)


def _adamw_reduce(parts, w, m, v, name):
    _, r, c = parts.shape
    c_pad = -(-c // LANES) * LANES
    tr = _div_tile(r, max(8, (3 * MIB // 16) // c_pad // 8 * 8), 8)

    def body(p_ref, w_ref, m_ref, v_ref, g_ref, d_ref, nm_ref, nv_ref):
        g = p_ref[0]
        for s in range(1, N_DEV):
            g = g + p_ref[s]
        mm = ADAM_B1 * m_ref[...] + (1.0 - ADAM_B1) * g
        vv = ADAM_B2 * v_ref[...] + (1.0 - ADAM_B2) * (g * g)
        m_hat = mm / (1.0 - ADAM_B1 ** ADAM_STEP)
        v_hat = vv / (1.0 - ADAM_B2 ** ADAM_STEP)
        g_ref[...] = g
        d_ref[...] = -ADAM_LR * (m_hat / (jnp.sqrt(v_hat) + ADAM_EPS) + ADAM_WD * w_ref[...])
        nm_ref[...] = mm
        nv_ref[...] = vv

    blk = pl.BlockSpec((tr, c), lambda i: (i, 0))
    return pl.pallas_call(
        body, name=name, grid=(r // tr,),
        in_specs=[pl.BlockSpec((N_DEV, tr, c), lambda i: (0, i, 0)), blk, blk, blk], out_specs=[blk] * 4,
        out_shape=[jax.ShapeDtypeStruct((r, c), f32)] * 4,
        compiler_params=_params(("parallel",), 48),
    )(parts, w, m, v)


def _dev_major_cols(g, width):
    r = g.shape[0]
    return g[:, :N_DEV * width].reshape(r, N_DEV, width).transpose(1, 0, 2)


def kernel(x, meta_tokens, mix_norm_w, ffn_norm_w, ret_w_in, ret_gn_w, ret_w_out, dn_w_in, dn_conv_w, dn_a_log, dn_dt_bias, dn_norm_w, dn_w_out, ffn_w_gate, ffn_w_up, ffn_w_down, final_norm_w, loss_target, m_meta_tokens, m_mix_norm_w, m_ffn_norm_w, m_ret_w_in, m_ret_gn_w, m_ret_w_out, m_dn_w_in, m_dn_conv_w, m_dn_a_log, m_dn_dt_bias, m_dn_norm_w, m_dn_w_out, m_ffn_w_gate, m_ffn_w_up, m_ffn_w_down, m_final_norm_w, v_meta_tokens, v_mix_norm_w, v_ffn_norm_w, v_ret_w_in, v_ret_gn_w, v_ret_w_out, v_dn_w_in, v_dn_conv_w, v_dn_a_log, v_dn_dt_bias, v_dn_norm_w, v_dn_w_out, v_ffn_w_gate, v_ffn_w_up, v_ffn_w_down, v_final_norm_w):
    d = x.shape[-1]
    me = 4 * lax.axis_index("x") + 2 * lax.axis_index("y") + lax.axis_index("c")
    fsh = ffn_w_gate.shape[-1]
    dn_sh = dn_w_in.shape[-1]
    ret_sh = ret_w_in.shape[-1]
    fh = N_DEV * fsh

    g_ret_in, g_ret_out, g_dn_in, g_dn_out, g_gate, g_up, g_down, g_meta, g_conv, g_dnn = _exchange(
        [ret_w_in[0].astype(bf16), ret_w_out[0].astype(bf16), dn_w_in[0].astype(bf16), dn_w_out[0].astype(bf16),
         ffn_w_gate.astype(bf16), ffn_w_up.astype(bf16), ffn_w_down.astype(bf16),
         meta_tokens, dn_conv_w[0], dn_norm_w], True, "gather_weights")
    cols = lambda g: g.transpose(1, 0, 2).reshape(g.shape[1], N_DEV * g.shape[2])
    dn_full = cols(g_dn_in)
    n_main = DN_CONV_CH + DN_V
    gate_f = g_gate.transpose(1, 2, 0, 3).reshape(2, d, fh)
    up_f = g_up.transpose(1, 2, 0, 3).reshape(2, d, fh)
    down_f = g_down.transpose(1, 0, 2, 3).reshape(2, fh, d)
    wts = dict(
        meta=cols(g_meta), mix_norm=mix_norm_w, ffn_norm=ffn_norm_w,
        ret_in=cols(g_ret_in), ret_gn=ret_gn_w, ret_out=g_ret_out.reshape(RET_V, d),
        dn_main=dn_full[:, :n_main],
        dn_ba=jnp.pad(dn_full[:, n_main:], ((0, 0), (0, BA_W - (dn_full.shape[1] - n_main)))),
        conv_w=cols(g_conv), a_log=dn_a_log, dt_bias=dn_dt_bias, dn_norm=cols(g_dnn),
        dn_out=g_dn_out.reshape(DN_V, d),
        gu0=jnp.concatenate([gate_f[0], up_f[0]], axis=1), gu1=jnp.concatenate([gate_f[1], up_f[1]], axis=1),
        down0=down_f[0], down1=down_f[1], final_norm=final_norm_w.reshape(1, d))

    loss_part, grad_x, gr = _local_step(x[0], loss_target[0], wts)
    loss = lax.psum(loss_part[0, 0], AXES)

    stack2 = lambda a, b: jnp.stack([a, b], axis=1)
    big = [
        _dev_major_cols(gr["ret_in"], ret_sh),
        gr["ret_out"].reshape(N_DEV, RET_V // N_DEV, d),
        _dev_major_cols(gr["dn_in"], dn_sh),
        gr["dn_out"].reshape(N_DEV, DN_V // N_DEV, d),
        stack2(_dev_major_cols(gr["gu0"][:, :fh], fsh), _dev_major_cols(gr["gu1"][:, :fh], fsh)).reshape(N_DEV, 2 * d, fsh),
        stack2(_dev_major_cols(gr["gu0"][:, fh:], fsh), _dev_major_cols(gr["gu1"][:, fh:], fsh)).reshape(N_DEV, 2 * d, fsh),
        stack2(gr["down0"].reshape(N_DEV, fsh, d), gr["down1"].reshape(N_DEV, fsh, d)).reshape(N_DEV, 2 * fsh, d),
    ]
    big_parts = _exchange(big, False, "scatter_grads")
    big_names = ["ret_w_in", "ret_w_out", "dn_w_in", "dn_w_out", "ffn_w_gate", "ffn_w_up", "ffn_w_down"]
    big_w = [ret_w_in, ret_w_out, dn_w_in, dn_w_out, ffn_w_gate, ffn_w_up, ffn_w_down]
    big_m = [m_ret_w_in, m_ret_w_out, m_dn_w_in, m_dn_w_out, m_ffn_w_gate, m_ffn_w_up, m_ffn_w_down]
    big_v = [v_ret_w_in, v_ret_w_out, v_dn_w_in, v_dn_w_out, v_ffn_w_gate, v_ffn_w_up, v_ffn_w_down]
    res = {}
    for nm, parts, w_, m_, v_ in zip(big_names, big_parts, big_w, big_m, big_v):
        r2, c2 = parts.shape[1], parts.shape[2]
        outs = _adamw_reduce(parts, w_.reshape(r2, c2), m_.reshape(r2, c2), v_.reshape(r2, c2), f"adamw_{nm}")
        res[nm] = [o.reshape(w_.shape) for o in outs]

    small_names = ["meta_tokens", "mix_norm_w", "ffn_norm_w", "ret_gn_w", "dn_conv_w", "dn_a_log", "dn_dt_bias",
                   "dn_norm_w", "final_norm_w"]
    small_g = [gr["meta"], gr["mix_norm"], gr["ffn_norm"], gr["ret_gn"], gr["conv_w"], gr["a_log"], gr["dt_bias"],
               gr["dn_norm"], gr["final_norm"]]
    small_w = [meta_tokens, mix_norm_w, ffn_norm_w, ret_gn_w, dn_conv_w, dn_a_log, dn_dt_bias, dn_norm_w, final_norm_w]
    small_m = [m_meta_tokens, m_mix_norm_w, m_ffn_norm_w, m_ret_gn_w, m_dn_conv_w, m_dn_a_log, m_dn_dt_bias,
               m_dn_norm_w, m_final_norm_w]
    small_v = [v_meta_tokens, v_mix_norm_w, v_ffn_norm_w, v_ret_gn_w, v_dn_conv_w, v_dn_a_log, v_dn_dt_bias,
               v_dn_norm_w, v_final_norm_w]
    sharded = {"meta_tokens", "dn_conv_w", "dn_norm_w"}
    flat = jnp.concatenate([g.reshape(-1) for g in small_g])
    row = 8 * LANES
    n_flat = flat.shape[0]
    flat = jnp.pad(flat, (0, -n_flat % row)).reshape(-1, row)
    (gathered,) = _exchange([flat], True, "gather_small_grads")
    gathered = gathered.reshape(N_DEV, -1)
    pieces, off = [], 0
    for nm, g, w_ in zip(small_names, small_g, small_w):
        full = gathered[:, off:off + g.size].reshape((N_DEV,) + g.shape)
        off += g.size
        if nm in sharded:
            wloc = w_.shape[-1]
            full = lax.dynamic_slice_in_dim(full, me * wloc, wloc, axis=full.ndim - 1)
        pieces.append(full.reshape(N_DEV, -1))
    sizes = [p.shape[1] for p in pieces]
    n_loc = sum(sizes)
    pad_loc = -n_loc % row

    def pack(vs, lead):
        cat = jnp.concatenate([a.reshape(lead + (-1,)) for a in vs], axis=-1)
        cat = jnp.pad(cat, [(0, 0)] * len(lead) + [(0, pad_loc)])
        return cat.reshape(lead + (-1, row))

    outs = _adamw_reduce(pack(pieces, (N_DEV,)), pack(small_w, ()), pack(small_m, ()), pack(small_v, ()), "adamw_small")
    off = 0
    for nm, sz, w_ in zip(small_names, sizes, small_w):
        res[nm] = [o.reshape(-1)[off:off + sz].reshape(w_.shape) for o in outs]
        off += sz

    order = ["meta_tokens", "mix_norm_w", "ffn_norm_w", "ret_w_in", "ret_gn_w", "ret_w_out", "dn_w_in", "dn_conv_w",
             "dn_a_log", "dn_dt_bias", "dn_norm_w", "dn_w_out", "ffn_w_gate", "ffn_w_up", "ffn_w_down", "final_norm_w"]
    grad_x = grad_x.reshape(x.shape)
    return (loss, grad_x, *[res[nm][0] for nm in order], *[res[nm][1] for nm in order],
            *[res[nm][2] for nm in order], *[res[nm][3] for nm in order])
```

```python
import functools
import math

import jax
import jax.numpy as jnp
from jax import lax
from jax.experimental import pallas as pl
from jax.experimental.pallas import tpu as pltpu

f32 = jnp.float32
bf16 = jnp.bfloat16
HI = lax.Precision.HIGHEST

N_META = 16
CHUNK = 64
PAD = CHUNK - N_META
RMS_EPS = 1e-6
RET_HEADS, RET_DK, RET_DV = 4, 256, 512
RET_QK, RET_V = RET_HEADS * RET_DK, RET_HEADS * RET_DV
DN_HEADS, DN_DK, DN_DV = 8, 128, 256
DN_QK, DN_V = DN_HEADS * DN_DK, DN_HEADS * DN_DV
DN_CONV_CH = 2 * DN_QK + DN_V
CONV_K = 4
ROPE_BASE = 10000.0
ADAM_LR, ADAM_B1, ADAM_B2, ADAM_EPS, ADAM_WD, ADAM_STEP = 0.001, 0.9, 0.999, 1e-08, 0.01, 10
N_DEV = 8
AXES = ("x", "y", "c")
LANES = 128
MIB = 1024 * 1024


def _tile(n_rows, cap):
    nch = n_rows // CHUNK
    best = 1
    for d in range(1, nch + 1):
        if nch % d == 0 and d * CHUNK <= cap:
            best = d
    return best * CHUNK


def _div_tile(n, cap, align):
    best = None
    for d in range(align, min(n, cap) + 1, align):
        if n % d == 0:
            best = d
    return best if best is not None else n


def _params(sem, vmem_mb):
    return pltpu.CompilerParams(dimension_semantics=sem, vmem_limit_bytes=int(vmem_mb * MIB))


def _nn(a, b, precision=None):
    return jnp.dot(a, b, preferred_element_type=f32, precision=precision)


def _nt(a, b, precision=None):
    return lax.dot_general(a, b, (((1,), (1,)), ((), ())), preferred_element_type=f32, precision=precision)


def _tn(a, b, precision=None):
    return lax.dot_general(a, b, (((0,), (0,)), ((), ())), preferred_element_type=f32, precision=precision)


def _b(x):
    return x.astype(bf16)


def _sigmoid(x):
    return 1.0 / (1.0 + jnp.exp(-x))


def _silu(x):
    return x * _sigmoid(x)


def _dsilu(x):
    s = _sigmoid(x)
    return s * (1.0 + x * (1.0 - s))


def _peer(k):
    x, y, c = lax.axis_index("x"), lax.axis_index("y"), lax.axis_index("c")
    px = 1 - x if k & 4 else x
    py = 1 - y if k & 2 else y
    pc = 1 - c if k & 1 else c
    return (px, py, pc), 4 * px + 2 * py + pc


class _Exchange:
    def __init__(self, arrs, gather):
        self.arrs, self.gather, self.n = list(arrs), gather, len(arrs)
        self.out_shapes = [jax.ShapeDtypeStruct(((N_DEV,) + a.shape) if gather else a.shape, a.dtype) for a in arrs]
        self.specs = [pl.BlockSpec(memory_space=pltpu.HBM)] * self.n
        self.scratch = [pltpu.SemaphoreType.DMA((self.n, N_DEV - 1)), pltpu.SemaphoreType.DMA((self.n, N_DEV - 1)),
                        pltpu.SemaphoreType.DMA((self.n,))]

    def _copies(self, ins, outs, sems):
        send_sems, recv_sems, local_sems = sems
        me = 4 * lax.axis_index("x") + 2 * lax.axis_index("y") + lax.axis_index("c")
        src = (lambda a, dest: ins[a]) if self.gather else (lambda a, dest: ins[a].at[dest])
        local = [pltpu.make_async_copy(src(a, me), outs[a].at[me], local_sems.at[a]) for a in range(self.n)]
        sends, lands = [], []
        for k in range(1, N_DEV):
            peer, pidx = _peer(k)
            for a in range(self.n):
                for dst, lst in ((outs[a].at[me], sends), (outs[a].at[pidx], lands)):
                    lst.append(pltpu.make_async_remote_copy(
                        src_ref=src(a, pidx), dst_ref=dst, send_sem=send_sems.at[a, k - 1],
                        recv_sem=recv_sems.at[a, k - 1], device_id=peer, device_id_type=pl.DeviceIdType.MESH))
        return local, sends, lands

    def start(self, ins, outs, sems):
        local, sends, _ = self._copies(ins, outs, sems)
        for cp in local + sends:
            cp.start()

    def wait(self, ins, outs, sems):
        local, sends, lands = self._copies(ins, outs, sems)
        for cp in lands:
            cp.wait_recv()
        for cp in sends:
            cp.wait_send()
        for cp in local:
            cp.wait()


def _call(body, args, *, name, grid, in_specs, out_specs, out_shape, scratch=(), sem, vmem_mb, comm=None):
    if comm is None:
        out = pl.pallas_call(body, name=name, grid=grid, in_specs=list(in_specs), out_specs=list(out_specs),
                             out_shape=list(out_shape), scratch_shapes=list(scratch),
                             compiler_params=_params(sem, vmem_mb))(*args)
        return list(out)
    n_in, n_out, n_scr, nc = len(args), len(out_shape), len(scratch), comm.n

    def carried(*refs):
        ins, cin = refs[:n_in], refs[n_in:n_in + nc]
        o0 = n_in + nc
        outs, cout = refs[o0:o0 + n_out], refs[o0 + n_out:o0 + n_out + nc]
        s0 = o0 + n_out + nc
        scr, sems = refs[s0:s0 + n_scr], refs[s0 + n_scr:]
        first = functools.reduce(jnp.logical_and, [pl.program_id(i) == 0 for i in range(len(grid))])
        last = functools.reduce(jnp.logical_and, [pl.program_id(i) == grid[i] - 1 for i in range(len(grid))])

        @pl.when(first)
        def _():
            comm.start(cin, cout, sems)

        body(*ins, *outs, *scr)

        @pl.when(last)
        def _():
            comm.wait(cin, cout, sems)

    out = pl.pallas_call(
        carried, name=name, grid=grid, in_specs=list(in_specs) + comm.specs, out_specs=list(out_specs) + comm.specs,
        out_shape=list(out_shape) + comm.out_shapes, scratch_shapes=list(scratch) + comm.scratch,
        compiler_params=_params(("arbitrary",) * len(grid), vmem_mb))(*args, *comm.arrs)
    return list(out)


def _exchange(arrs, gather, name):
    comm = _Exchange(arrs, gather)

    def body(*refs):
        ins, outs, sems = refs[:comm.n], refs[comm.n:2 * comm.n], refs[2 * comm.n:]
        comm.start(ins, outs, sems)
        comm.wait(ins, outs, sems)

    return pl.pallas_call(body, name=name, in_specs=comm.specs, out_specs=comm.specs, out_shape=comm.out_shapes,
                          scratch_shapes=comm.scratch)(*comm.arrs)


def _matmul(a, b, *, mode, tm, tn, tk, name, out_dtype=f32, res=None, vmem_mb=48, comm=None):
    if mode == "nn":
        (m, k), (k2, n) = a.shape, b.shape
    elif mode == "nt":
        (m, k), (n, k2) = a.shape, b.shape
    else:
        (k, m), (k2, n) = a.shape, b.shape
    assert k == k2 and m % tm == 0 and n % tn == 0 and k % tk == 0, (name, a.shape, b.shape, tm, tn, tk)
    nk = k // tk
    dot = {"nn": _nn, "nt": _nt, "tn": _tn}[mode]
    a_spec = {"nn": pl.BlockSpec((tm, tk), lambda i, j, kk: (i, kk)),
              "nt": pl.BlockSpec((tm, tk), lambda i, j, kk: (i, kk)),
              "tn": pl.BlockSpec((tk, tm), lambda i, j, kk: (kk, i))}[mode]
    b_spec = {"nn": pl.BlockSpec((tk, tn), lambda i, j, kk: (kk, j)),
              "nt": pl.BlockSpec((tn, tk), lambda i, j, kk: (j, kk)),
              "tn": pl.BlockSpec((tk, tn), lambda i, j, kk: (kk, j))}[mode]
    o_spec = pl.BlockSpec((tm, tn), lambda i, j, kk: (i, j))
    has_res = res is not None

    def body(*refs):
        if has_res:
            a_ref, b_ref, r_ref, o_ref = refs[:4]
            rest = refs[4:]
        else:
            a_ref, b_ref, o_ref = refs[:3]
            r_ref = None
            rest = refs[3:]
        part = dot(_b(a_ref[...]), _b(b_ref[...]))
        if nk == 1:
            if has_res:
                part = part + r_ref[...]
            o_ref[...] = part.astype(out_dtype)
        else:
            acc_ref = rest[0]
            kk = pl.program_id(2)

            @pl.when(kk == 0)
            def _():
                acc_ref[...] = part

            @pl.when(kk > 0)
            def _():
                acc_ref[...] += part

            @pl.when(kk == nk - 1)
            def _():
                tot = acc_ref[...]
                if has_res:
                    tot = tot + r_ref[...]
                o_ref[...] = tot.astype(out_dtype)

    in_specs = [a_spec, b_spec] + ([o_spec] if has_res else [])
    args = (a, b) + ((res,) if has_res else ())
    out = _call(body, args, name=name, grid=(m // tm, n // tn, nk), in_specs=in_specs, out_specs=[o_spec],
                out_shape=[jax.ShapeDtypeStruct((m, n), out_dtype)],
                scratch=[pltpu.VMEM((tm, tn), f32)] if nk > 1 else [],
                sem=("parallel", "parallel", "arbitrary"), vmem_mb=vmem_mb, comm=comm)
    return out[0] if comm is None else (out[0], out[1:])


def _rms_fwd(h, w, name):
    l, d = h.shape
    tr = _tile(l, 512)

    def body(h_ref, w_ref, o_ref):
        x = h_ref[...]
        r = lax.rsqrt(jnp.mean(x * x, axis=-1, keepdims=True) + RMS_EPS)
        o_ref[...] = _b(x * r * w_ref[...])

    return pl.pallas_call(
        body, name=name, grid=(l // tr,),
        in_specs=[pl.BlockSpec((tr, d), lambda i: (i, 0)), pl.BlockSpec((1, d), lambda i: (0, 0))],
        out_specs=pl.BlockSpec((tr, d), lambda i: (i, 0)),
        out_shape=jax.ShapeDtypeStruct((l, d), bf16),
        compiler_params=_params(("parallel",), 32),
    )(h, w)


def _rms_bwd(h, w, dhn, dres, name):
    l, d = h.shape
    tr = _tile(l, 512)

    def body(h_ref, w_ref, g_ref, r_ref, dh_ref, dhb_ref, dw_ref):
        x = h_ref[...]
        r = lax.rsqrt(jnp.mean(x * x, axis=-1, keepdims=True) + RMS_EPS)
        xh = x * r
        g = g_ref[...]
        dxh = g * w_ref[...]
        dx = r * (dxh - xh * jnp.mean(dxh * xh, axis=-1, keepdims=True))
        dh = r_ref[...] + dx
        dh_ref[...] = dh
        dhb_ref[...] = _b(dh)
        dw = jnp.sum(g * xh, axis=0, keepdims=True)

        @pl.when(pl.program_id(0) == 0)
        def _():
            dw_ref[...] = dw

        @pl.when(pl.program_id(0) > 0)
        def _():
            dw_ref[...] += dw

    row = pl.BlockSpec((tr, d), lambda i: (i, 0))
    vec = pl.BlockSpec((1, d), lambda i: (0, 0))
    return pl.pallas_call(
        body, name=name, grid=(l // tr,), in_specs=[row, vec, row, row], out_specs=[row, row, vec],
        out_shape=[jax.ShapeDtypeStruct((l, d), f32), jax.ShapeDtypeStruct((l, d), bf16),
                   jax.ShapeDtypeStruct((1, d), f32)],
        compiler_params=_params(("arbitrary",), 40),
    )(h, w, dhn, dres)


def _final_loss(h, w, target, name):
    l, d = h.shape
    nch = l // CHUNK

    def body(h_ref, w_ref, t_ref, dh_ref, dhb_ref, dw_ref, loss_ref):
        n = pl.program_id(0)
        live = (n > 0).astype(f32)
        x = h_ref[...]
        r = lax.rsqrt(jnp.mean(x * x, axis=-1, keepdims=True) + RMS_EPS)
        xh = x * r
        wv = w_ref[...]
        err = (xh * wv - t_ref[...]) * live
        dy = err * (1.0 / d)
        dxh = dy * wv
        dx = r * (dxh - xh * jnp.mean(dxh * xh, axis=-1, keepdims=True))
        dh_ref[...] = dx
        dhb_ref[...] = _b(dx)
        dw = jnp.sum(dy * xh, axis=0, keepdims=True)
        part = 0.5 * jnp.sum(jnp.sum(err * err, axis=-1, keepdims=True) * (1.0 / d), axis=0, keepdims=True)
        part = jnp.broadcast_to(part, (1, LANES))

        @pl.when(n == 0)
        def _():
            dw_ref[...] = dw
            loss_ref[...] = part

        @pl.when(n > 0)
        def _():
            dw_ref[...] += dw
            loss_ref[...] += part

    row = pl.BlockSpec((CHUNK, d), lambda i: (i, 0))
    vec = pl.BlockSpec((1, d), lambda i: (0, 0))
    return pl.pallas_call(
        body, name=name, grid=(nch,),
        in_specs=[row, vec, pl.BlockSpec((CHUNK, d), lambda i: (jnp.maximum(i - 1, 0), 0))],
        out_specs=[row, row, vec, pl.BlockSpec((1, LANES), lambda i: (0, 0))],
        out_shape=[jax.ShapeDtypeStruct((l, d), f32), jax.ShapeDtypeStruct((l, d), bf16),
                   jax.ShapeDtypeStruct((1, d), f32), jax.ShapeDtypeStruct((1, LANES), f32)],
        compiler_params=_params(("arbitrary",), 32),
    )(h, w, target)


def _swiglu_fwd(ab, name):
    l, two_f = ab.shape
    fh = two_f // 2
    tr = _tile(l, 256)

    def body(a_ref, b_ref, o_ref):
        o_ref[...] = _b(_silu(a_ref[...]) * b_ref[...])

    return pl.pallas_call(
        body, name=name, grid=(l // tr,),
        in_specs=[pl.BlockSpec((tr, fh), lambda i: (i, 0)), pl.BlockSpec((tr, fh), lambda i: (i, 1))],
        out_specs=pl.BlockSpec((tr, fh), lambda i: (i, 0)),
        out_shape=jax.ShapeDtypeStruct((l, fh), bf16),
        compiler_params=_params(("parallel",), 32),
    )(ab, ab)


def _swiglu_bwd(ab, ds, name):
    l, two_f = ab.shape
    fh = two_f // 2
    tr = _tile(l, 256)

    def body(a_ref, b_ref, ds_ref, o_ref):
        a, bb, g = a_ref[...], b_ref[...], ds_ref[...]
        o_ref[:, :fh] = _b(g * bb * _dsilu(a))
        o_ref[:, fh:] = _b(g * _silu(a))

    lo = pl.BlockSpec((tr, fh), lambda i: (i, 0))
    hi = pl.BlockSpec((tr, fh), lambda i: (i, 1))
    return pl.pallas_call(
        body, name=name, grid=(l // tr,), in_specs=[lo, hi, lo],
        out_specs=pl.BlockSpec((tr, two_f), lambda i: (i, 0)),
        out_shape=jax.ShapeDtypeStruct((l, two_f), bf16),
        compiler_params=_params(("parallel",), 40),
    )(ab, ab, ds)


def _gnorm_fwd(o, proj, nw, heads, dv, gate_blk, name):
    l, hv = o.shape
    tr = _tile(l, 256)

    def body(o_ref, g_ref, w_ref, y_ref):
        wv = w_ref[...]
        for h in range(heads):
            sl = slice(h * dv, (h + 1) * dv)
            oh = o_ref[:, sl]
            r = lax.rsqrt(jnp.mean(oh * oh, axis=-1, keepdims=True) + RMS_EPS)
            y_ref[:, sl] = _b(oh * r * wv * _silu(g_ref[:, sl]))

    return pl.pallas_call(
        body, name=name, grid=(l // tr,),
        in_specs=[pl.BlockSpec((tr, hv), lambda i: (i, 0)), pl.BlockSpec((tr, hv), lambda i: (i, gate_blk)),
                  pl.BlockSpec((1, dv), lambda i: (0, 0))],
        out_specs=pl.BlockSpec((tr, hv), lambda i: (i, 0)),
        out_shape=jax.ShapeDtypeStruct((l, hv), bf16),
        compiler_params=_params(("parallel",), 32),
    )(o, proj, nw)


def _gnorm_bwd(o, proj, nw, dy, heads, dv, gate_blk, name):
    l, hv = o.shape
    tr = _tile(l, 256)

    def body(o_ref, g_ref, w_ref, dy_ref, do_ref, dg_ref, dw_ref):
        wv = w_ref[...]
        dw = jnp.zeros((1, dv), f32)
        for h in range(heads):
            sl = slice(h * dv, (h + 1) * dv)
            oh = o_ref[:, sl]
            g = g_ref[:, sl]
            dyh = dy_ref[:, sl]
            r = lax.rsqrt(jnp.mean(oh * oh, axis=-1, keepdims=True) + RMS_EPS)
            xh = oh * r
            dn = dyh * _silu(g)
            dg_ref[:, sl] = _b(dyh * (xh * wv) * _dsilu(g))
            dxh = dn * wv
            do_ref[:, sl] = r * (dxh - xh * jnp.mean(dxh * xh, axis=-1, keepdims=True))
            dw = dw + jnp.sum(dn * xh, axis=0, keepdims=True)

        @pl.when(pl.program_id(0) == 0)
        def _():
            dw_ref[...] = dw

        @pl.when(pl.program_id(0) > 0)
        def _():
            dw_ref[...] += dw

    row = pl.BlockSpec((tr, hv), lambda i: (i, 0))
    vec = pl.BlockSpec((1, dv), lambda i: (0, 0))
    return pl.pallas_call(
        body, name=name, grid=(l // tr,),
        in_specs=[row, pl.BlockSpec((tr, hv), lambda i: (i, gate_blk)), vec, row],
        out_specs=[row, row, vec],
        out_shape=[jax.ShapeDtypeStruct((l, hv), f32), jax.ShapeDtypeStruct((l, hv), bf16),
                   jax.ShapeDtypeStruct((1, dv), f32)],
        compiler_params=_params(("arbitrary",), 40),
    )(o, proj, nw, dy)


def _ret_prep(proj, cos, sin, name):
    l = proj.shape[0]
    tr = _tile(l, 256)
    half = RET_DK // 2
    scale = RET_DK ** -0.5

    def body(p_ref, c_ref, s_ref, o_ref):
        rows = pl.program_id(0) * tr + lax.broadcasted_iota(jnp.int32, (tr, 1), 0)
        kmul = jnp.where(rows >= PAD, scale, 0.0).astype(f32)
        c, s = c_ref[...], s_ref[...]
        for j in range(2 * RET_HEADS):
            t1 = p_ref[:, j * RET_DK: j * RET_DK + half]
            t2 = p_ref[:, j * RET_DK + half: (j + 1) * RET_DK]
            o1 = t1 * c - t2 * s
            o2 = t1 * s + t2 * c
            if j >= RET_HEADS:
                o1, o2 = o1 * kmul, o2 * kmul
            o_ref[:, j * RET_DK: j * RET_DK + half] = o1
            o_ref[:, j * RET_DK + half: (j + 1) * RET_DK] = o2

    wide = pl.BlockSpec((tr, 2 * RET_QK), lambda i: (i, 0))
    tab = pl.BlockSpec((tr, half), lambda i: (i, 0))
    return pl.pallas_call(
        body, name=name, grid=(l // tr,), in_specs=[wide, tab, tab], out_specs=wide,
        out_shape=jax.ShapeDtypeStruct((l, 2 * RET_QK), f32),
        compiler_params=_params(("parallel",), 32),
    )(proj, cos, sin)


def _ret_prep_bwd(dq, dk, cos, sin, name):
    l = dq.shape[0]
    tr = _tile(l, 256)
    half = RET_DK // 2
    scale = RET_DK ** -0.5

    def body(dq_ref, dk_ref, c_ref, s_ref, o_ref):
        rows = pl.program_id(0) * tr + lax.broadcasted_iota(jnp.int32, (tr, 1), 0)
        kmul = jnp.where(rows >= PAD, scale, 0.0).astype(f32)
        c, s = c_ref[...], s_ref[...]
        for j in range(2 * RET_HEADS):
            d_ref = dq_ref if j < RET_HEADS else dk_ref
            jj = j % RET_HEADS
            d1 = d_ref[:, jj * RET_DK: jj * RET_DK + half]
            d2 = d_ref[:, jj * RET_DK + half: (jj + 1) * RET_DK]
            if j >= RET_HEADS:
                d1, d2 = d1 * kmul, d2 * kmul
            o_ref[:, j * RET_DK: j * RET_DK + half] = _b(d1 * c + d2 * s)
            o_ref[:, j * RET_DK + half: (j + 1) * RET_DK] = _b(d2 * c - d1 * s)

    nar = pl.BlockSpec((tr, RET_QK), lambda i: (i, 0))
    wide = pl.BlockSpec((tr, 2 * RET_QK), lambda i: (i, 0))
    tab = pl.BlockSpec((tr, half), lambda i: (i, 0))
    return pl.pallas_call(
        body, name=name, grid=(l // tr,), in_specs=[nar, nar, tab, tab], out_specs=wide,
        out_shape=jax.ShapeDtypeStruct((l, 2 * RET_QK), bf16),
        compiler_params=_params(("parallel",), 32),
    )(dq, dk, cos, sin)


def _ret_decay(lg):
    idx = lax.broadcasted_iota(jnp.int32, (CHUNK, 1), 0).astype(f32)
    ri = lax.broadcasted_iota(jnp.int32, (CHUNK, CHUNK), 0)
    ci = lax.broadcasted_iota(jnp.int32, (CHUNK, CHUNK), 1)
    rel = (ri - ci).astype(f32)
    dmask = jnp.where(ri >= ci, jnp.exp(lg * jnp.maximum(rel, 0.0)), 0.0)
    xi = jnp.exp(lg * (idx + 1.0))
    zeta = jnp.exp(lg * (CHUNK - 1.0 - idx))
    return dmask, xi, zeta


def _ret_scan_fwd(qk, proj, lgs, gcs, name, comm=None):
    l = qk.shape[0]
    nch = l // CHUNK

    def body(lg_ref, gc_ref, q_ref, k_ref, v_ref, o_ref, st_ref, s_ref):
        h, n = pl.program_id(0), pl.program_id(1)

        @pl.when(n == 0)
        def _():
            s_ref[...] = jnp.zeros_like(s_ref)

        dmask, xi, zeta = _ret_decay(lg_ref[h])
        q, k, v = q_ref[...], k_ref[...], v_ref[...]
        s = s_ref[...]
        sb = _b(s)
        st_ref[0, 0] = sb
        scores = _nt(_b(q), _b(k)) * dmask
        o_ref[...] = _nn(_b(scores), _b(v)) + _nn(_b(q * xi), sb)
        s_ref[...] = gc_ref[h] * s + _tn(_b(k * zeta), _b(v))

    smem = pl.BlockSpec(memory_space=pltpu.SMEM)
    return _call(
        body, (lgs, gcs, qk, qk, proj), name=name, grid=(RET_HEADS, nch),
        in_specs=[smem, smem,
                  pl.BlockSpec((CHUNK, RET_DK), lambda h, n: (n, h)),
                  pl.BlockSpec((CHUNK, RET_DK), lambda h, n: (n, RET_HEADS + h)),
                  pl.BlockSpec((CHUNK, RET_DV), lambda h, n: (n, RET_HEADS + h))],
        out_specs=[pl.BlockSpec((CHUNK, RET_DV), lambda h, n: (n, h)),
                   pl.BlockSpec((1, 1, RET_DK, RET_DV), lambda h, n: (n, h, 0, 0))],
        out_shape=[jax.ShapeDtypeStruct((l, RET_V), f32),
                   jax.ShapeDtypeStruct((nch, RET_HEADS, RET_DK, RET_DV), bf16)],
        scratch=[pltpu.VMEM((RET_DK, RET_DV), f32)], sem=("arbitrary", "arbitrary"), vmem_mb=32, comm=comm)


def _ret_scan_bwd(qk, proj, states, do, lgs, gcs, name, comm=None):
    l = qk.shape[0]
    nch = l // CHUNK

    def body(lg_ref, gc_ref, q_ref, k_ref, v_ref, st_ref, do_ref, dq_ref, dk_ref, dv_ref, ds_ref):
        h, step = pl.program_id(0), pl.program_id(1)

        @pl.when(step == 0)
        def _():
            ds_ref[...] = jnp.zeros_like(ds_ref)

        dmask, xi, zeta = _ret_decay(lg_ref[h])
        q, k, v, do_ = q_ref[...], k_ref[...], v_ref[...], do_ref[...]
        qb, kb, vb, dob = _b(q), _b(k), _b(v), _b(do_)
        sb = st_ref[0, 0]
        dsp = ds_ref[...]
        dspb = _b(dsp)
        scores = _nt(qb, kb) * dmask
        dscores = _nt(dob, vb) * dmask
        dq_ref[...] = _nn(_b(dscores), kb) + _nt(dob, sb) * xi
        dk_ref[...] = _tn(_b(dscores), qb) + _nt(vb, dspb) * zeta
        dv_ref[...] = _b(_tn(_b(scores), dob) + _nn(_b(k * zeta), dspb))
        ds_ref[...] = gc_ref[h] * dsp + _tn(_b(q * xi), dob)

    smem = pl.BlockSpec(memory_space=pltpu.SMEM)
    rev = lambda n: nch - 1 - n
    return _call(
        body, (lgs, gcs, qk, qk, proj, states, do), name=name, grid=(RET_HEADS, nch),
        in_specs=[smem, smem,
                  pl.BlockSpec((CHUNK, RET_DK), lambda h, n: (rev(n), h)),
                  pl.BlockSpec((CHUNK, RET_DK), lambda h, n: (rev(n), RET_HEADS + h)),
                  pl.BlockSpec((CHUNK, RET_DV), lambda h, n: (rev(n), RET_HEADS + h)),
                  pl.BlockSpec((1, 1, RET_DK, RET_DV), lambda h, n: (rev(n), h, 0, 0)),
                  pl.BlockSpec((CHUNK, RET_DV), lambda h, n: (rev(n), h))],
        out_specs=[pl.BlockSpec((CHUNK, RET_DK), lambda h, n: (rev(n), h)),
                   pl.BlockSpec((CHUNK, RET_DK), lambda h, n: (rev(n), h)),
                   pl.BlockSpec((CHUNK, RET_DV), lambda h, n: (rev(n), h))],
        out_shape=[jax.ShapeDtypeStruct((l, RET_QK), f32), jax.ShapeDtypeStruct((l, RET_QK), f32),
                   jax.ShapeDtypeStruct((l, RET_V), bf16)],
        scratch=[pltpu.VMEM((RET_DK, RET_DV), f32)], sem=("arbitrary", "arbitrary"), vmem_mb=32, comm=comm)


CONV_BLK = 1024
HALO = 8


def _shift_down(cur, prev, s):
    full = pltpu.roll(cur, s, 0)
    hr = pltpu.roll(cur[:HALO], s, 0)
    pr = pltpu.roll(prev, s, 0)
    rows = lax.broadcasted_iota(jnp.int32, (HALO, 1), 0)
    return full, jnp.where(rows < s, pr, hr)


def _conv_rows(x_ref, p_ref, w_ref, c_ref, i, tr):
    rows = i * tr + lax.broadcasted_iota(jnp.int32, (tr, 1), 0)
    cur = jnp.where(rows >= PAD, x_ref[...], 0.0)
    prow = i * tr - HALO + lax.broadcasted_iota(jnp.int32, (HALO, 1), 0)
    prev = jnp.where(prow >= PAD, p_ref[...], 0.0)
    w = w_ref[...]
    acc = cur * w[CONV_K - 1:CONV_K, :]
    head = cur[:HALO] * w[CONV_K - 1:CONV_K, :]
    for s in range(1, CONV_K):
        full, hd = _shift_down(cur, prev, s)
        wk = w[CONV_K - 1 - s:CONV_K - s, :]
        acc = acc + full * wk
        head = head + hd * wk
    c_ref[...] = acc
    c_ref[0:HALO, :] = head


def _l2_heads(a, scale):
    outs = []
    for h in range(CONV_BLK // DN_DK):
        ah = a[:, h * DN_DK:(h + 1) * DN_DK]
        outs.append(ah * (lax.rsqrt(jnp.sum(ah * ah, axis=-1, keepdims=True) + RMS_EPS) * scale))
    return outs


def _dn_conv_fwd(proj, conv_w, name):
    l = proj.shape[0]
    tr = _tile(l, 256)
    nblk = DN_CONV_CH // CONV_BLK

    def body(x_ref, p_ref, w_ref, o_ref, c_ref):
        i, j = pl.program_id(0), pl.program_id(1)
        _conv_rows(x_ref, p_ref, w_ref, c_ref, i, tr)
        a = _silu(c_ref[...])

        @pl.when(j == 0)
        def _():
            for h, v in enumerate(_l2_heads(a, DN_DK ** -0.5)):
                o_ref[:, h * DN_DK:(h + 1) * DN_DK] = v

        @pl.when(j == 1)
        def _():
            for h, v in enumerate(_l2_heads(a, 1.0)):
                o_ref[:, h * DN_DK:(h + 1) * DN_DK] = v

        @pl.when(j >= 2)
        def _():
            o_ref[...] = a

    hb = tr // HALO
    return pl.pallas_call(
        body, name=name, grid=(l // tr, nblk),
        in_specs=[pl.BlockSpec((tr, CONV_BLK), lambda i, j: (i, j)),
                  pl.BlockSpec((HALO, CONV_BLK), lambda i, j: (jnp.maximum(i * hb - 1, 0), j)),
                  pl.BlockSpec((CONV_K, CONV_BLK), lambda i, j: (0, j))],
        out_specs=pl.BlockSpec((tr, CONV_BLK), lambda i, j: (i, j)),
        out_shape=jax.ShapeDtypeStruct((l, DN_CONV_CH), f32),
        scratch_shapes=[pltpu.VMEM((tr, CONV_BLK), f32)],
        compiler_params=_params(("parallel", "parallel"), 32),
    )(proj, proj, conv_w)


def _dn_conv_bwd_a(proj, conv_w, dqkv, name):
    l = proj.shape[0]
    tr = _tile(l, 256)
    nblk = DN_CONV_CH // CONV_BLK

    def body(x_ref, p_ref, w_ref, d_ref, dc_ref, dw_ref, c_ref):
        j, i = pl.program_id(0), pl.program_id(1)
        _conv_rows(x_ref, p_ref, w_ref, c_ref, i, tr)
        c = c_ref[...]
        a = _silu(c)
        dsl = _dsilu(c)

        def l2_bwd(scale):
            for h in range(CONV_BLK // DN_DK):
                sl = slice(h * DN_DK, (h + 1) * DN_DK)
                ah = a[:, sl]
                r = lax.rsqrt(jnp.sum(ah * ah, axis=-1, keepdims=True) + RMS_EPS)
                yh = ah * r
                dy = d_ref[:, sl]
                da = (r * scale) * (dy - yh * jnp.sum(dy * yh, axis=-1, keepdims=True))
                dc_ref[:, sl] = da * dsl[:, sl]

        @pl.when(j == 0)
        def _():
            l2_bwd(DN_DK ** -0.5)

        @pl.when(j == 1)
        def _():
            l2_bwd(1.0)

        @pl.when(j >= 2)
        def _():
            dc_ref[...] = d_ref[...] * dsl

        dc = dc_ref[...]
        rows = i * tr + lax.broadcasted_iota(jnp.int32, (tr, 1), 0)
        cur = jnp.where(rows >= PAD, x_ref[...], 0.0)
        prow = i * tr - HALO + lax.broadcasted_iota(jnp.int32, (HALO, 1), 0)
        prev = jnp.where(prow >= PAD, p_ref[...], 0.0)
        hsel = lax.broadcasted_iota(jnp.int32, (tr, 1), 0) >= HALO
        parts = [None] * CONV_K
        parts[CONV_K - 1] = jnp.sum(dc * cur, axis=0, keepdims=True)
        for s in range(1, CONV_K):
            full, hd = _shift_down(cur, prev, s)
            tot = jnp.sum(jnp.where(hsel, dc * full, 0.0), axis=0, keepdims=True)
            tot = tot + jnp.sum(dc[:HALO] * hd, axis=0, keepdims=True)
            parts[CONV_K - 1 - s] = tot
        ksel = lax.broadcasted_iota(jnp.int32, (CONV_K, 1), 0)
        dw = jnp.zeros((CONV_K, CONV_BLK), f32)
        for k in range(CONV_K):
            dw = dw + jnp.where(ksel == k, parts[k], 0.0)

        @pl.when(i == 0)
        def _():
            dw_ref[...] = dw

        @pl.when(i > 0)
        def _():
            dw_ref[...] += dw

    hb = tr // HALO
    blk = pl.BlockSpec((tr, CONV_BLK), lambda j, i: (i, j))
    return pl.pallas_call(
        body, name=name, grid=(nblk, l // tr),
        in_specs=[blk, pl.BlockSpec((HALO, CONV_BLK), lambda j, i: (jnp.maximum(i * hb - 1, 0), j)),
                  pl.BlockSpec((CONV_K, CONV_BLK), lambda j, i: (0, j)), blk],
        out_specs=[blk, pl.BlockSpec((CONV_K, CONV_BLK), lambda j, i: (0, j))],
        out_shape=[jax.ShapeDtypeStruct((l, DN_CONV_CH), f32), jax.ShapeDtypeStruct((CONV_K, DN_CONV_CH), f32)],
        scratch_shapes=[pltpu.VMEM((tr, CONV_BLK), f32)],
        compiler_params=_params(("parallel", "arbitrary"), 40),
    )(proj, proj, conv_w, dqkv)


def _dn_conv_bwd_b(dc, conv_w, name):
    l = dc.shape[0]
    tr = _tile(l, 256)
    nblk = DN_CONV_CH // CONV_BLK
    nrow = l // tr

    def body(d_ref, n_ref, w_ref, o_ref, t_ref):
        i = pl.program_id(0)
        cur = d_ref[...]
        nxt = jnp.where(i < nrow - 1, n_ref[...], 0.0)
        w = w_ref[...]
        acc = cur * w[CONV_K - 1:CONV_K, :]
        tail = cur[tr - HALO:] * w[CONV_K - 1:CONV_K, :]
        rows8 = lax.broadcasted_iota(jnp.int32, (HALO, 1), 0)
        for s in range(1, CONV_K):
            wk = w[CONV_K - 1 - s:CONV_K - s, :]
            acc = acc + pltpu.roll(cur, tr - s, 0) * wk
            tl = jnp.where(rows8 >= HALO - s, pltpu.roll(nxt, HALO - s, 0), pltpu.roll(cur[tr - HALO:], HALO - s, 0))
            tail = tail + tl * wk
        t_ref[...] = acc
        t_ref[tr - HALO:, :] = tail
        rows = i * tr + lax.broadcasted_iota(jnp.int32, (tr, 1), 0)
        o_ref[...] = _b(jnp.where(rows >= PAD, t_ref[...], 0.0))

    hb = tr // HALO
    nh = l // HALO
    return pl.pallas_call(
        body, name=name, grid=(nrow, nblk),
        in_specs=[pl.BlockSpec((tr, CONV_BLK), lambda i, j: (i, j)),
                  pl.BlockSpec((HALO, CONV_BLK), lambda i, j: (jnp.minimum((i + 1) * hb, nh - 1), j)),
                  pl.BlockSpec((CONV_K, CONV_BLK), lambda i, j: (0, j))],
        out_specs=pl.BlockSpec((tr, CONV_BLK), lambda i, j: (i, j)),
        out_shape=jax.ShapeDtypeStruct((l, DN_CONV_CH), bf16),
        scratch_shapes=[pltpu.VMEM((tr, CONV_BLK), f32)],
        compiler_params=_params(("parallel", "parallel"), 32),
    )(dc, dc, conv_w)


BA_W = LANES


def _dn_gates(ba_ref, al_ref, dt_ref, n):
    rows = n * CHUNK + lax.broadcasted_iota(jnp.int32, (CHUNK, 1), 0)
    vm = (rows >= PAD).astype(f32)
    bin_ = ba_ref[:, 0:DN_HEADS]
    z = ba_ref[:, DN_HEADS:2 * DN_HEADS] + dt_ref[...]
    sp = jnp.maximum(z, 0.0) + jnp.log1p(jnp.exp(-jnp.abs(z)))
    ea = jnp.exp(al_ref[...])
    beta = _sigmoid(bin_) * vm
    g = -ea * sp * vm
    return vm, bin_, z, ea, beta, g


def _tri():
    ri = lax.broadcasted_iota(jnp.int32, (CHUNK, CHUNK), 0)
    ci = lax.broadcasted_iota(jnp.int32, (CHUNK, CHUNK), 1)
    return ri, ci


def _split(a):
    hi = _b(a)
    return hi, _b(a - hi.astype(f32))


def _mm3(a, b, dot=_nn):
    (ah, al), (bh, bl) = _split(a), _split(b)
    return dot(ah, bh) + (dot(ah, bl) + dot(al, bh))


def _cumsum_rows(tri, g):
    tb = _b(tri)
    g1 = _b(g)
    r1 = g - g1.astype(f32)
    g2 = _b(r1)
    g3 = _b(r1 - g2.astype(f32))
    return _nn(tb, g1) + (_nn(tb, g2) + _nn(tb, g3))


DN_SCAN_CHUNKS = 3


def _scan_chunks(nch):
    return DN_SCAN_CHUNKS if nch % DN_SCAN_CHUNKS == 0 else 1


def _dn_prep(qkv, ba, a_log, dt_bias, name):
    l = qkv.shape[0]
    nch = l // CHUNK
    heads = range(DN_HEADS)

    def body(q_ref, k_ref, v_ref, ba_ref, al_ref, dt_ref, t_ref, u_ref, wq_ref, pk_ref, eg_ref, kpt_ref, qwt_ref):
        n = pl.program_id(0)
        _, _, _, _, beta, g = _dn_gates(ba_ref, al_ref, dt_ref, n)
        ri, ci = _tri()
        incl, strict = ri >= ci, ri > ci
        eye = (ri == ci).astype(f32)
        gam = _cumsum_rows(incl.astype(f32), g)
        gam_t = gam.T
        gc = [gam[:, h:h + 1] for h in heads]
        bh = [beta[:, h:h + 1] for h in heads]
        kh = [k_ref[:, h * DN_DK:(h + 1) * DN_DK] for h in heads]
        kb = [_b(k) for k in kh]
        decay = [jnp.exp(jnp.where(incl, gc[h] - gam_t[h:h + 1, :], -jnp.inf)) for h in heads]
        a = [jnp.where(strict, bh[h] * _nt(kb[h], kb[h]) * decay[h], 0.0) for h in heads]
        t = [eye - a[h] for h in heads]
        p = a
        for _ in range(int(math.log2(CHUNK)) - 1):
            p = [_mm3(p[h], p[h]) for h in heads]
            t = [t[h] + _mm3(t[h], p[h]) for h in heads]
        eg = [jnp.exp(gc[h]) for h in heads]
        for h in heads:
            t_ref[0, h] = t[h]
            u_ref[:, h * DN_DV:(h + 1) * DN_DV] = _mm3(t[h], v_ref[:, h * DN_DV:(h + 1) * DN_DV] * bh[h])
            w = _mm3(t[h], kh[h] * (bh[h] * eg[h]))
            wq_ref[0, h, 0:CHUNK, :] = _b(w)
            qwt_ref[0, h, DN_DK:2 * DN_DK, :] = _b(w.T)
        for h in heads:
            qh = q_ref[:, h * DN_DK:(h + 1) * DN_DK]
            gl = gc[h][CHUNK - 1:CHUNK, :]
            qe = qh * eg[h]
            ke = kh[h] * jnp.exp(gl - gc[h])
            pmat = _nt(_b(qh), kb[h]) * decay[h]
            wq_ref[0, h, CHUNK:2 * CHUNK, :] = _b(qe)
            qwt_ref[0, h, 0:DN_DK, :] = _b(qe.T)
            pk_ref[0, h, 0:CHUNK, :] = _b(pmat)
            pk_ref[0, h, CHUNK:CHUNK + DN_DK, :] = _b(ke.T)
            kpt_ref[0, h, :, 0:DN_DK] = _b(ke)
            kpt_ref[0, h, :, DN_DK:DN_DK + CHUNK] = _b(pmat.T)
            eg_ref[0, h] = jnp.broadcast_to(jnp.exp(gl), (8, LANES))

    vec = pl.BlockSpec((1, DN_HEADS), lambda n: (0, 0))
    return pl.pallas_call(
        body, name=name, grid=(nch,),
        in_specs=[pl.BlockSpec((CHUNK, DN_QK), lambda n: (n, 0)), pl.BlockSpec((CHUNK, DN_QK), lambda n: (n, 1)),
                  pl.BlockSpec((CHUNK, DN_V), lambda n: (n, 1)), pl.BlockSpec((CHUNK, BA_W), lambda n: (n, 0)),
                  vec, vec],
        out_specs=[pl.BlockSpec((1, DN_HEADS, CHUNK, CHUNK), lambda n: (n, 0, 0, 0)),
                   pl.BlockSpec((CHUNK, DN_V), lambda n: (n, 0)),
                   pl.BlockSpec((1, DN_HEADS, 2 * CHUNK, DN_DK), lambda n: (n, 0, 0, 0)),
                   pl.BlockSpec((1, DN_HEADS, CHUNK + DN_DK, CHUNK), lambda n: (n, 0, 0, 0)),
                   pl.BlockSpec((1, DN_HEADS, 8, LANES), lambda n: (n, 0, 0, 0)),
                   pl.BlockSpec((1, DN_HEADS, CHUNK, DN_DK + CHUNK), lambda n: (n, 0, 0, 0)),
                   pl.BlockSpec((1, DN_HEADS, 2 * DN_DK, CHUNK), lambda n: (n, 0, 0, 0))],
        out_shape=[jax.ShapeDtypeStruct((nch, DN_HEADS, CHUNK, CHUNK), f32),
                   jax.ShapeDtypeStruct((l, DN_V), f32),
                   jax.ShapeDtypeStruct((nch, DN_HEADS, 2 * CHUNK, DN_DK), bf16),
                   jax.ShapeDtypeStruct((nch, DN_HEADS, CHUNK + DN_DK, CHUNK), bf16),
                   jax.ShapeDtypeStruct((nch, DN_HEADS, 8, LANES), f32),
                   jax.ShapeDtypeStruct((nch, DN_HEADS, CHUNK, DN_DK + CHUNK), bf16),
                   jax.ShapeDtypeStruct((nch, DN_HEADS, 2 * DN_DK, CHUNK), bf16)],
        compiler_params=_params(("parallel",), 40),
    )(qkv, qkv, qkv, ba, a_log, dt_bias)


def _dn_scan_fwd(u, wq, pk, egl, name):
    l = u.shape[0]
    nch = l // CHUNK
    cs = _scan_chunks(nch)

    def body(u_ref, wq_ref, pk_ref, eg_ref, o_ref, st_ref, vn_ref, s_ref):
        @pl.when(pl.program_id(0) == 0)
        def _():
            s_ref[...] = jnp.zeros_like(s_ref)

        for c in range(cs):
            rows = slice(c * CHUNK, (c + 1) * CHUNK)
            for h in range(DN_HEADS):
                cols = slice(h * DN_DV, (h + 1) * DN_DV)
                s = s_ref[h]
                sb = _b(s)
                st_ref[c, h] = sb
                x = _nn(wq_ref[c, h], sb)
                vnb = _b(u_ref[rows, cols] - x[0:CHUNK])
                vn_ref[rows, cols] = vnb
                y = _nn(pk_ref[c, h], vnb)
                o_ref[rows, cols] = x[CHUNK:2 * CHUNK] + y[0:CHUNK]
                s_ref[h] = eg_ref[c, h][0:1, 0:1] * s + y[CHUNK:CHUNK + DN_DK]

    return pl.pallas_call(
        body, name=name, grid=(nch // cs,),
        in_specs=[pl.BlockSpec((cs * CHUNK, DN_V), lambda n: (n, 0)),
                  pl.BlockSpec((cs, DN_HEADS, 2 * CHUNK, DN_DK), lambda n: (n, 0, 0, 0)),
                  pl.BlockSpec((cs, DN_HEADS, CHUNK + DN_DK, CHUNK), lambda n: (n, 0, 0, 0)),
                  pl.BlockSpec((cs, DN_HEADS, 8, LANES), lambda n: (n, 0, 0, 0))],
        out_specs=[pl.BlockSpec((cs * CHUNK, DN_V), lambda n: (n, 0)),
                   pl.BlockSpec((cs, DN_HEADS, DN_DK, DN_DV), lambda n: (n, 0, 0, 0)),
                   pl.BlockSpec((cs * CHUNK, DN_V), lambda n: (n, 0))],
        out_shape=[jax.ShapeDtypeStruct((l, DN_V), f32),
                   jax.ShapeDtypeStruct((nch, DN_HEADS, DN_DK, DN_DV), bf16),
                   jax.ShapeDtypeStruct((l, DN_V), bf16)],
        scratch_shapes=[pltpu.VMEM((DN_HEADS, DN_DK, DN_DV), f32)],
        compiler_params=_params(("arbitrary",), 40),
    )(u, wq, pk, egl)


def _dn_scan_bwd(do, kpt, qwt, egl, name):
    l = do.shape[0]
    nch = l // CHUNK
    cs = _scan_chunks(nch)
    nblk = nch // cs

    def body(do_ref, kpt_ref, qwt_ref, eg_ref, dvn_ref, dsp_ref, ds_ref):
        @pl.when(pl.program_id(0) == 0)
        def _():
            ds_ref[...] = jnp.zeros_like(ds_ref)

        for c in reversed(range(cs)):
            rows = slice(c * CHUNK, (c + 1) * CHUNK)
            for h in range(DN_HEADS):
                cols = slice(h * DN_DV, (h + 1) * DN_DV)
                dsp = ds_ref[h]
                dspb = _b(dsp)
                dsp_ref[c, h] = dspb
                dob = _b(do_ref[rows, cols])
                kpt_h = kpt_ref[c, h]
                dvn = _nn(kpt_h[:, 0:DN_DK], dspb) + _nn(kpt_h[:, DN_DK:DN_DK + CHUNK], dob)
                dvn_ref[rows, cols] = dvn
                qwt_h = qwt_ref[c, h]
                ds_ref[h] = (eg_ref[c, h][0:1, 0:1] * dsp + _nn(qwt_h[0:DN_DK], dob)
                             - _nn(qwt_h[DN_DK:2 * DN_DK], _b(dvn)))

    rev = lambda s: nblk - 1 - s
    return pl.pallas_call(
        body, name=name, grid=(nblk,),
        in_specs=[pl.BlockSpec((cs * CHUNK, DN_V), lambda s: (rev(s), 0)),
                  pl.BlockSpec((cs, DN_HEADS, CHUNK, DN_DK + CHUNK), lambda s: (rev(s), 0, 0, 0)),
                  pl.BlockSpec((cs, DN_HEADS, 2 * DN_DK, CHUNK), lambda s: (rev(s), 0, 0, 0)),
                  pl.BlockSpec((cs, DN_HEADS, 8, LANES), lambda s: (rev(s), 0, 0, 0))],
        out_specs=[pl.BlockSpec((cs * CHUNK, DN_V), lambda s: (rev(s), 0)),
                   pl.BlockSpec((cs, DN_HEADS, DN_DK, DN_DV), lambda s: (rev(s), 0, 0, 0))],
        out_shape=[jax.ShapeDtypeStruct((l, DN_V), f32),
                   jax.ShapeDtypeStruct((nch, DN_HEADS, DN_DK, DN_DV), bf16)],
        scratch_shapes=[pltpu.VMEM((DN_HEADS, DN_DK, DN_DV), f32)],
        compiler_params=_params(("arbitrary",), 40),
    )(do, kpt, qwt, egl)


def _dn_post_bwd(qkv, ba, a_log, dt_bias, states, dsp_all, tinv_all, u_all, wq, vn_all, do, dvn_all, name):
    l = qkv.shape[0]
    nch = l // CHUNK

    def body(q_ref, k_ref, v_ref, ba_ref, al_ref, dt_ref, st_ref, dsp_ref, t_ref, u_ref, wq_ref, vn_ref, do_ref,
             dvn_ref, dqkv_ref, dba_ref, dal_ref, ddt_ref):
        step = pl.program_id(0)
        n = step

        vm, bin_, z, ea, beta, g = _dn_gates(ba_ref, al_ref, dt_ref, n)
        ri, ci = _tri()
        incl, strict = ri >= ci, ri > ci
        gam = _cumsum_rows(incl.astype(f32), g)
        gam_t = gam.T
        lane8 = lax.broadcasted_iota(jnp.int32, (1, DN_HEADS), 1)
        sub8 = lax.broadcasted_iota(jnp.int32, (DN_HEADS, 1), 0)
        dbeta = jnp.zeros((CHUNK, DN_HEADS), f32)
        dgam = jnp.zeros((CHUNK, DN_HEADS), f32)
        dgam_neg_t = jnp.zeros((DN_HEADS, CHUNK), f32)
        last = (lax.broadcasted_iota(jnp.int32, (CHUNK, 1), 0) == CHUNK - 1).astype(f32)
        for h in range(DN_HEADS):
            gc, gr, bh = gam[:, h:h + 1], gam_t[h:h + 1, :], beta[:, h:h + 1]
            qh = q_ref[:, h * DN_DK:(h + 1) * DN_DK]
            kh = k_ref[:, h * DN_DK:(h + 1) * DN_DK]
            vh = v_ref[:, h * DN_DV:(h + 1) * DN_DV]
            doh = _b(do_ref[:, h * DN_DV:(h + 1) * DN_DV])
            u = u_ref[:, h * DN_DV:(h + 1) * DN_DV]
            wb = wq_ref[0, h, 0:CHUNK, :]
            tinv = t_ref[0, h]
            sb = st_ref[0, h]
            dspb = dsp_ref[0, h]
            vnb = vn_ref[:, h * DN_DV:(h + 1) * DN_DV]
            dvn = dvn_ref[:, h * DN_DV:(h + 1) * DN_DV]
            decay = jnp.exp(jnp.where(incl, gc - gr, -jnp.inf))
            qb, kb = _b(qh), _b(kh)
            kk = _nt(kb, kb)
            a = jnp.where(strict, bh * kk * decay, 0.0)
            eg = jnp.exp(gc)
            gl = gc[CHUNK - 1:CHUNK, :]
            egl = jnp.exp(gl)
            ekd = jnp.exp(gl - gc)
            p = _nt(qb, kb) * decay
            qe, ke = qh * eg, kh * ekd
            dvnb = _b(dvn)
            dpraw = _nt(doh, vnb)
            dqk = _b(dpraw * decay)
            dqe = _nt(doh, sb)
            dke = _nt(vnb, dspb)
            dqkv_ref[:, h * DN_DK:(h + 1) * DN_DK] = _nn(dqk, kb) + dqe * eg
            dk = _tn(dqk, qb) + dke * ekd
            dw = -_nt(dvnb, sb)
            dru = _mm3(tinv, dvn, _tn)
            drw = _mm3(tinv, dw, _tn)
            da = -(_nt(_b(dru), _b(u)) + _nt(_b(drw), wb))
            da = jnp.where(strict, da, 0.0)
            dqkv_ref[:, 2 * DN_QK + h * DN_DV:2 * DN_QK + (h + 1) * DN_DV] = bh * dru
            keg = kh * eg
            db_h = (jnp.sum(dru * vh, axis=-1, keepdims=True) + jnp.sum(drw * keg, axis=-1, keepdims=True)
                    + jnp.sum(da * kk * decay, axis=-1, keepdims=True))
            dkk = _b(da * bh * decay)
            dk = dk + (bh * eg) * drw + _nn(dkk, kb) + _tn(dkk, kb)
            dqkv_ref[:, DN_QK + h * DN_DK:DN_QK + (h + 1) * DN_DK] = dk
            mm = da * a + dpraw * p
            dgl = (jnp.sum(jnp.sum(dke * ke, axis=-1, keepdims=True), axis=0, keepdims=True)
                   + egl * jnp.sum(jnp.sum(dspb.astype(f32) * sb.astype(f32), axis=-1, keepdims=True),
                                   axis=0, keepdims=True))
            dg_h = (jnp.sum(mm, axis=-1, keepdims=True) + jnp.sum(drw * keg, axis=-1, keepdims=True) * bh
                    + jnp.sum(dqe * qe, axis=-1, keepdims=True) - jnp.sum(dke * ke, axis=-1, keepdims=True)
                    + last * dgl)
            dbeta = dbeta + jnp.where(lane8 == h, db_h, 0.0)
            dgam = dgam + jnp.where(lane8 == h, dg_h, 0.0)
            dgam_neg_t = dgam_neg_t + jnp.where(sub8 == h, jnp.sum(mm, axis=0, keepdims=True), 0.0)
        dgam = dgam - dgam_neg_t.T
        dg = _cumsum_rows((ri <= ci).astype(f32), dgam)
        sg = _sigmoid(bin_)
        dbin = dbeta * vm * sg * (1.0 - sg)
        dain = dg * (-ea) * vm * _sigmoid(z)
        dba_ref[...] = jnp.zeros_like(dba_ref)
        dba_ref[:, 0:DN_HEADS] = dbin
        dba_ref[:, DN_HEADS:2 * DN_HEADS] = dain
        dal = jnp.sum(dg * g, axis=0, keepdims=True)
        ddt = jnp.sum(dain, axis=0, keepdims=True)

        @pl.when(step == 0)
        def _():
            dal_ref[...] = dal
            ddt_ref[...] = ddt

        @pl.when(step > 0)
        def _():
            dal_ref[...] += dal
            ddt_ref[...] += ddt

    vec = pl.BlockSpec((1, DN_HEADS), lambda s: (0, 0))
    qs = pl.BlockSpec((CHUNK, DN_QK), lambda s: (s, 0))
    ks = pl.BlockSpec((CHUNK, DN_QK), lambda s: (s, 1))
    vs = pl.BlockSpec((CHUNK, DN_V), lambda s: (s, 1))
    v0 = pl.BlockSpec((CHUNK, DN_V), lambda s: (s, 0))
    st = pl.BlockSpec((1, DN_HEADS, DN_DK, DN_DV), lambda s: (s, 0, 0, 0))
    return pl.pallas_call(
        body, name=name, grid=(nch,),
        in_specs=[qs, ks, vs, pl.BlockSpec((CHUNK, BA_W), lambda s: (s, 0)), vec, vec, st, st,
                  pl.BlockSpec((1, DN_HEADS, CHUNK, CHUNK), lambda s: (s, 0, 0, 0)),
                  v0, pl.BlockSpec((1, DN_HEADS, 2 * CHUNK, DN_DK), lambda s: (s, 0, 0, 0)), v0, v0, v0],
        out_specs=[pl.BlockSpec((CHUNK, DN_CONV_CH), lambda s: (s, 0)),
                   pl.BlockSpec((CHUNK, BA_W), lambda s: (s, 0)), vec, vec],
        out_shape=[jax.ShapeDtypeStruct((l, DN_CONV_CH), f32), jax.ShapeDtypeStruct((l, BA_W), f32),
                   jax.ShapeDtypeStruct((1, DN_HEADS), f32), jax.ShapeDtypeStruct((1, DN_HEADS), f32)],
        compiler_params=_params(("arbitrary",), 48),
    )(qkv, qkv, qkv, ba, a_log, dt_bias, states, dsp_all, tinv_all, u_all, wq, vn_all, do, dvn_all)


def _ffn_fwd(h, nw, wgu, wd, tb, th, tag):
    d = h.shape[1]
    fh = wd.shape[0]
    hn = _rms_fwd(h, nw, f"{tag}_norm")
    ab = _matmul(hn, wgu, mode="nn", tm=tb, tn=512, tk=d, name=f"{tag}_gu")
    s = _swiglu_fwd(ab, f"{tag}_act")
    out = _matmul(s, wd, mode="nn", tm=th, tn=512, tk=fh // 2, res=h, name=f"{tag}_down")
    return out, (hn, ab, s)


def _ffn_bwd(dh, dhb, h, nw, wgu, wd, saved, tb, th, tag, plan):
    hn, ab, s = saved
    d = h.shape[1]
    fh = wd.shape[0]
    ds = _matmul(dhb, wd, mode="nt", tm=th, tn=fh // 2, tk=d, name=f"{tag}_b_ds")
    dwd = _matmul(s, dhb, mode="tn", tm=fh // 2, tn=512, tk=th, out_dtype=bf16, name=f"{tag}_b_dwd")
    dab = _swiglu_bwd(ab, ds, f"{tag}_b_act")
    dhn = plan.matmul(f"{tag}_b_dhn", dab, wgu, mode="nt", tm=th, tn=d, tk=512)
    dwgu = _matmul(hn, dab, mode="tn", tm=d, tn=512, tk=tb, out_dtype=bf16, name=f"{tag}_b_dwgu")
    dh2, dh2b, dnw = _rms_bwd(h, nw, dhn, dh, f"{tag}_b_norm")
    return dh2, dh2b, dnw, dwgu, dwd


class _Plan:
    GATHERS = {"ret_proj": ("ret_out", "gate0", "up0"),
               "ret_scan": ("down0", "dn_in", "dn_out", "gate1", "up1", "down1")}
    SCATTERS = {"dn_b_dhn": ("gate1", "up1", "down1"), "ffn0_b_dhn": ("dn_out", "dn_in"),
                "ret_b_scan": ("gate0", "up0", "down0", "ret_out"), "ret_b_dhn": ("ret_in",)}

    def __init__(self, shards, wts):
        self.shards, self.wts, self.grads, self.parts = shards, wts, {}, {}

    def _exchange(self, stage):
        if self.shards is None:
            return None
        if stage in self.GATHERS:
            return _Exchange([self.shards[n] for n in self.GATHERS[stage]], True)
        if stage in self.SCATTERS:
            return _Exchange([self._dev_major(n) for n in self.SCATTERS[stage]], False)
        return None

    def _dev_major(self, name):
        g = self.grads
        if name[:-1] in ("gate", "up"):
            gu = g["gu" + name[-1]]
            fh = gu.shape[1] // 2
            part = gu[:, :fh] if name.startswith("gate") else gu[:, fh:]
            return _dev_major_cols(part, fh // N_DEV)
        if name[:-1] == "down":
            dwd = g[name]
            return dwd.reshape(N_DEV, dwd.shape[0] // N_DEV, dwd.shape[1])
        if name in ("ret_out", "dn_out"):
            return g[name].reshape(N_DEV, g[name].shape[0] // N_DEV, g[name].shape[1])
        return _dev_major_cols(g[name], self.shards[name].shape[-1])

    def _landed(self, stage, outs):
        if stage in self.SCATTERS:
            self.parts.update(zip(self.SCATTERS[stage], outs))
            return
        w = self.wts
        cols = lambda t: t.transpose(1, 0, 2).reshape(t.shape[1], N_DEV * t.shape[2])
        rows = lambda t: t.reshape(N_DEV * t.shape[1], t.shape[2])
        for name, t in zip(self.GATHERS[stage], outs):
            if name in ("ret_out", "dn_out") or name.startswith("down"):
                w[name] = rows(t)
            elif name == "dn_in":
                full = cols(t)
                n_main = DN_CONV_CH + DN_V
                w["dn_main"] = full[:, :n_main]
                w["dn_ba"] = jnp.pad(full[:, n_main:], ((0, 0), (0, BA_W - (full.shape[1] - n_main))))
            else:
                w[name] = cols(t)
        for layer in "01":
            if "gate" + layer in w and "up" + layer in w and "gu" + layer not in w:
                w["gu" + layer] = jnp.concatenate([w["gate" + layer], w["up" + layer]], axis=1)

    def matmul(self, stage, a, b, **kw):
        comm = self._exchange(stage)
        if comm is None:
            return _matmul(a, b, name=stage, **kw)
        out, landed = _matmul(a, b, name=stage, comm=comm, **kw)
        self._landed(stage, landed)
        return out

    def call(self, stage, fn, *args, n_out):
        comm = self._exchange(stage)
        out = fn(*args, stage, comm=comm)
        if comm is not None:
            self._landed(stage, out[n_out:])
        return out[:n_out]


def _local_step(x2, target, wts, shards=None):
    plan = _Plan(shards, wts)
    s_len, d = x2.shape
    l = s_len + CHUNK
    tb = _tile(l, 3072)
    th = tb // 2 if (tb // 2) % 16 == 0 else tb
    half = RET_DK // 2
    inv_freq = ROPE_BASE ** (-jnp.arange(half, dtype=f32) / half)
    ang = (jnp.arange(l) - PAD).astype(f32)[:, None] * inv_freq[None, :]
    cos, sin = jnp.cos(ang), jnp.sin(ang)
    lgs = jnp.log1p(-jnp.exp2(-5.0 - jnp.arange(RET_HEADS, dtype=f32)))
    gcs = jnp.exp(lgs * CHUNK)

    h0 = jnp.concatenate([jnp.zeros((PAD, d), f32), wts["meta"], x2], axis=0)
    mixw, ffnw = wts["mix_norm"], wts["ffn_norm"]

    hn0 = _rms_fwd(h0, mixw[0:1], "l0_norm")
    proj0 = plan.matmul("ret_proj", hn0, wts["ret_in"], mode="nn", tm=tb, tn=512, tk=d)
    qk0 = _ret_prep(proj0, cos, sin, "ret_prep")
    o0, st0 = plan.call("ret_scan", _ret_scan_fwd, qk0, proj0, lgs, gcs, n_out=2)
    y0 = _gnorm_fwd(o0, proj0, wts["ret_gn"], RET_HEADS, RET_DV, 2, "ret_gnorm")
    h1 = _matmul(y0, wts["ret_out"], mode="nn", tm=th, tn=512, tk=RET_V, res=h0, name="ret_out")
    h2, ffn0 = _ffn_fwd(h1, ffnw[0:1], wts["gu0"], wts["down0"], tb, th, "ffn0")

    hn2 = _rms_fwd(h2, mixw[1:2], "l1_norm")
    proj1 = _matmul(hn2, wts["dn_main"], mode="nn", tm=tb, tn=512, tk=d, name="dn_proj")
    ba = _matmul(hn2, wts["dn_ba"], mode="nn", tm=tb, tn=BA_W, tk=d, name="dn_proj_ba")
    qkv1 = _dn_conv_fwd(proj1, wts["conv_w"], "dn_conv")
    tinv1, u1, wq1, pk1, egl1, kpt1, qwt1 = _dn_prep(qkv1, ba, wts["a_log"], wts["dt_bias"], "dn_prep")
    o1, st1, vn1 = _dn_scan_fwd(u1, wq1, pk1, egl1, "dn_scan")
    y1 = _gnorm_fwd(o1, proj1, wts["dn_norm"], DN_HEADS, DN_DV, 2, "dn_gnorm")
    h3 = _matmul(y1, wts["dn_out"], mode="nn", tm=th, tn=512, tk=DN_V, res=h2, name="dn_out")
    h4, ffn1 = _ffn_fwd(h3, ffnw[1:2], wts["gu1"], wts["down1"], tb, th, "ffn1")

    dh4, dh4b, dfinal, loss = _final_loss(h4, wts["final_norm"], target, "final_loss")
    gr = plan.grads
    dh3, dh3b, dffn1, gr["gu1"], gr["down1"] = _ffn_bwd(dh4, dh4b, h3, ffnw[1:2], wts["gu1"], wts["down1"], ffn1,
                                                       tb, th, "ffn1", plan)

    dy1 = _matmul(dh3b, wts["dn_out"], mode="nt", tm=th, tn=1024, tk=d, name="dn_b_dy")
    gr["dn_out"] = _matmul(y1, dh3b, mode="tn", tm=1024, tn=512, tk=tb, out_dtype=bf16, name="dn_b_dwout")
    do1, dgate1, ddn_norm = _gnorm_bwd(o1, proj1, wts["dn_norm"], dy1, DN_HEADS, DN_DV, 2, "dn_b_gnorm")
    dvn1, dsp1 = _dn_scan_bwd(do1, kpt1, qwt1, egl1, "dn_b_scan")
    dqkv1, dba, dalog, ddt = _dn_post_bwd(qkv1, ba, wts["a_log"], wts["dt_bias"], st1, dsp1, tinv1, u1, wq1, vn1,
                                          do1, dvn1, "dn_b_post")
    dc1, dconv = _dn_conv_bwd_a(proj1, wts["conv_w"], dqkv1, "dn_b_conv_a")
    dx1 = _dn_conv_bwd_b(dc1, wts["conv_w"], "dn_b_conv_b")
    dproj1 = jnp.concatenate([dx1, dgate1, dba.astype(bf16)], axis=1)
    w_dn_full = jnp.concatenate([wts["dn_main"], wts["dn_ba"]], axis=1)
    nfull = dproj1.shape[1]
    tkf = _div_tile(nfull, 1024, LANES)
    dhn2 = plan.matmul("dn_b_dhn", dproj1, w_dn_full, mode="nt", tm=th, tn=d, tk=tkf)
    gr["dn_in"] = _matmul(hn2, dproj1, mode="tn", tm=d, tn=tkf, tk=tb, out_dtype=bf16, name="dn_b_dwin")
    dh2, dh2b, dmix1 = _rms_bwd(h2, mixw[1:2], dhn2, dh3, "l1_b_norm")

    dh1, dh1b, dffn0, gr["gu0"], gr["down0"] = _ffn_bwd(dh2, dh2b, h1, ffnw[0:1], wts["gu0"], wts["down0"], ffn0,
                                                       tb, th, "ffn0", plan)

    dy0 = _matmul(dh1b, wts["ret_out"], mode="nt", tm=th, tn=1024, tk=d, name="ret_b_dy")
    gr["ret_out"] = _matmul(y0, dh1b, mode="tn", tm=1024, tn=512, tk=tb, out_dtype=bf16, name="ret_b_dwout")
    do0, dgate0, dret_gn = _gnorm_bwd(o0, proj0, wts["ret_gn"], dy0, RET_HEADS, RET_DV, 2, "ret_b_gnorm")
    dq0, dk0, dv0 = plan.call("ret_b_scan", _ret_scan_bwd, qk0, proj0, st0, do0, lgs, gcs, n_out=3)
    dqk0 = _ret_prep_bwd(dq0, dk0, cos, sin, "ret_b_prep")
    dproj0 = jnp.concatenate([dqk0, dv0, dgate0], axis=1)
    gr["ret_in"] = _matmul(hn0, dproj0, mode="tn", tm=d, tn=512, tk=tb, out_dtype=bf16, name="ret_b_dwin")
    dhn0 = plan.matmul("ret_b_dhn", dproj0, wts["ret_in"], mode="nt", tm=th, tn=d, tk=512)
    dh0, _, dmix0 = _rms_bwd(h0, mixw[0:1], dhn0, dh1, "l0_b_norm")

    gr.update(meta=dh0[PAD:CHUNK], mix_norm=jnp.concatenate([dmix0, dmix1], axis=0),
              ffn_norm=jnp.concatenate([dffn0, dffn1], axis=0), ret_gn=dret_gn, conv_w=dconv, a_log=dalog,
              dt_bias=ddt, dn_norm=ddn_norm, final_norm=dfinal)
    return loss, dh0[CHUNK:], gr, plan


def _adamw_reduce(parts, w, m, v, name):
    _, r, c = parts.shape
    c_pad = -(-c // LANES) * LANES
    tr = _div_tile(r, max(8, (3 * MIB // 16) // c_pad // 8 * 8), 16)

    def body(p_ref, w_ref, m_ref, v_ref, g_ref, d_ref, nm_ref, nv_ref):
        g = p_ref[0].astype(f32)
        for s in range(1, N_DEV):
            g = g + p_ref[s].astype(f32)
        mm = ADAM_B1 * m_ref[...] + (1.0 - ADAM_B1) * g
        vv = ADAM_B2 * v_ref[...] + (1.0 - ADAM_B2) * (g * g)
        m_hat = mm / (1.0 - ADAM_B1 ** ADAM_STEP)
        v_hat = vv / (1.0 - ADAM_B2 ** ADAM_STEP)
        g_ref[...] = g
        d_ref[...] = -ADAM_LR * (m_hat / (jnp.sqrt(v_hat) + ADAM_EPS) + ADAM_WD * w_ref[...])
        nm_ref[...] = mm
        nv_ref[...] = vv

    blk = pl.BlockSpec((tr, c), lambda i: (i, 0))
    return pl.pallas_call(
        body, name=name, grid=(r // tr,),
        in_specs=[pl.BlockSpec((N_DEV, tr, c), lambda i: (0, i, 0)), blk, blk, blk], out_specs=[blk] * 4,
        out_shape=[jax.ShapeDtypeStruct((r, c), f32)] * 4,
        compiler_params=_params(("parallel",), 48),
    )(parts, w, m, v)


def _dev_major_cols(g, width):
    r = g.shape[0]
    return g[:, :N_DEV * width].reshape(r, N_DEV, width).transpose(1, 0, 2)


def kernel(x, meta_tokens, mix_norm_w, ffn_norm_w, ret_w_in, ret_gn_w, ret_w_out, dn_w_in, dn_conv_w, dn_a_log, dn_dt_bias, dn_norm_w, dn_w_out, ffn_w_gate, ffn_w_up, ffn_w_down, final_norm_w, loss_target, m_meta_tokens, m_mix_norm_w, m_ffn_norm_w, m_ret_w_in, m_ret_gn_w, m_ret_w_out, m_dn_w_in, m_dn_conv_w, m_dn_a_log, m_dn_dt_bias, m_dn_norm_w, m_dn_w_out, m_ffn_w_gate, m_ffn_w_up, m_ffn_w_down, m_final_norm_w, v_meta_tokens, v_mix_norm_w, v_ffn_norm_w, v_ret_w_in, v_ret_gn_w, v_ret_w_out, v_dn_w_in, v_dn_conv_w, v_dn_a_log, v_dn_dt_bias, v_dn_norm_w, v_dn_w_out, v_ffn_w_gate, v_ffn_w_up, v_ffn_w_down, v_final_norm_w):
    d = x.shape[-1]
    me = 4 * lax.axis_index("x") + 2 * lax.axis_index("y") + lax.axis_index("c")

    shards = dict(ret_in=ret_w_in[0].astype(bf16), ret_out=ret_w_out[0].astype(bf16),
                  dn_in=dn_w_in[0].astype(bf16), dn_out=dn_w_out[0].astype(bf16))
    for layer in (0, 1):
        shards[f"gate{layer}"] = ffn_w_gate[layer].astype(bf16)
        shards[f"up{layer}"] = ffn_w_up[layer].astype(bf16)
        shards[f"down{layer}"] = ffn_w_down[layer].astype(bf16)
    g_ret_in, g_meta, g_conv, g_dnn = _exchange([shards["ret_in"], meta_tokens, dn_conv_w[0], dn_norm_w], True,
                                                "gather_first")
    cols = lambda g: g.transpose(1, 0, 2).reshape(g.shape[1], N_DEV * g.shape[2])
    wts = dict(meta=cols(g_meta), mix_norm=mix_norm_w, ffn_norm=ffn_norm_w, ret_in=cols(g_ret_in), ret_gn=ret_gn_w,
               conv_w=cols(g_conv), a_log=dn_a_log, dt_bias=dn_dt_bias, dn_norm=cols(g_dnn),
               final_norm=final_norm_w.reshape(1, d))

    loss_part, grad_x, gr, plan = _local_step(x[0], loss_target[0], wts, shards)
    loss = lax.psum(loss_part[0, 0], AXES)

    pp = plan.parts
    both = lambda name: jnp.concatenate([pp[name + "0"], pp[name + "1"]], axis=1)
    big_parts = [pp["ret_in"], pp["ret_out"], pp["dn_in"], pp["dn_out"], both("gate"), both("up"), both("down")]
    big_names = ["ret_w_in", "ret_w_out", "dn_w_in", "dn_w_out", "ffn_w_gate", "ffn_w_up", "ffn_w_down"]
    big_w = [ret_w_in, ret_w_out, dn_w_in, dn_w_out, ffn_w_gate, ffn_w_up, ffn_w_down]
    big_m = [m_ret_w_in, m_ret_w_out, m_dn_w_in, m_dn_w_out, m_ffn_w_gate, m_ffn_w_up, m_ffn_w_down]
    big_v = [v_ret_w_in, v_ret_w_out, v_dn_w_in, v_dn_w_out, v_ffn_w_gate, v_ffn_w_up, v_ffn_w_down]
    res = {}
    for nm, parts, w_, m_, v_ in zip(big_names, big_parts, big_w, big_m, big_v):
        r2, c2 = parts.shape[1], parts.shape[2]
        outs = _adamw_reduce(parts, w_.reshape(r2, c2), m_.reshape(r2, c2), v_.reshape(r2, c2), f"adamw_{nm}")
        res[nm] = [o.reshape(w_.shape) for o in outs]

    small_names = ["meta_tokens", "mix_norm_w", "ffn_norm_w", "ret_gn_w", "dn_conv_w", "dn_a_log", "dn_dt_bias",
                   "dn_norm_w", "final_norm_w"]
    small_g = [gr["meta"], gr["mix_norm"], gr["ffn_norm"], gr["ret_gn"], gr["conv_w"], gr["a_log"], gr["dt_bias"],
               gr["dn_norm"], gr["final_norm"]]
    small_w = [meta_tokens, mix_norm_w, ffn_norm_w, ret_gn_w, dn_conv_w, dn_a_log, dn_dt_bias, dn_norm_w, final_norm_w]
    small_m = [m_meta_tokens, m_mix_norm_w, m_ffn_norm_w, m_ret_gn_w, m_dn_conv_w, m_dn_a_log, m_dn_dt_bias,
               m_dn_norm_w, m_final_norm_w]
    small_v = [v_meta_tokens, v_mix_norm_w, v_ffn_norm_w, v_ret_gn_w, v_dn_conv_w, v_dn_a_log, v_dn_dt_bias,
               v_dn_norm_w, v_final_norm_w]
    sharded = {"meta_tokens", "dn_conv_w", "dn_norm_w"}
    flat = jnp.concatenate([g.reshape(-1) for g in small_g])
    row = 8 * LANES
    n_flat = flat.shape[0]
    flat = jnp.pad(flat, (0, -n_flat % row)).reshape(-1, row)
    (gathered,) = _exchange([flat], True, "gather_small_grads")
    gathered = gathered.reshape(N_DEV, -1)
    pieces, off = [], 0
    for nm, g, w_ in zip(small_names, small_g, small_w):
        full = gathered[:, off:off + g.size].reshape((N_DEV,) + g.shape)
        off += g.size
        if nm in sharded:
            wloc = w_.shape[-1]
            full = lax.dynamic_slice_in_dim(full, me * wloc, wloc, axis=full.ndim - 1)
        pieces.append(full.reshape(N_DEV, -1))
    sizes = [p.shape[1] for p in pieces]
    n_loc = sum(sizes)
    pad_loc = -n_loc % row

    def pack(vs, lead):
        cat = jnp.concatenate([a.reshape(lead + (-1,)) for a in vs], axis=-1)
        cat = jnp.pad(cat, [(0, 0)] * len(lead) + [(0, pad_loc)])
        return cat.reshape(lead + (-1, row))

    outs = _adamw_reduce(pack(pieces, (N_DEV,)), pack(small_w, ()), pack(small_m, ()), pack(small_v, ()), "adamw_small")
    off = 0
    for nm, sz, w_ in zip(small_names, sizes, small_w):
        res[nm] = [o.reshape(-1)[off:off + sz].reshape(w_.shape) for o in outs]
        off += sz

    order = ["meta_tokens", "mix_norm_w", "ffn_norm_w", "ret_w_in", "ret_gn_w", "ret_w_out", "dn_w_in", "dn_conv_w",
             "dn_a_log", "dn_dt_bias", "dn_norm_w", "dn_w_out", "ffn_w_gate", "ffn_w_up", "ffn_w_down", "final_norm_w"]
    grad_x = grad_x.reshape(x.shape)
    return (loss, grad_x, *[res[nm][0] for nm in order], *[res[nm][1] for nm in order],
            *[res[nm][2] for nm in order], *[res[nm][3] for nm in order])
```

```python
import functools
import math

import jax
import jax.numpy as jnp
from jax import lax
from jax.experimental import pallas as pl
from jax.experimental.pallas import tpu as pltpu

f32 = jnp.float32
bf16 = jnp.bfloat16
HI = lax.Precision.HIGHEST

N_META = 16
CHUNK = 64
PAD = CHUNK - N_META
RMS_EPS = 1e-6
RET_HEADS, RET_DK, RET_DV = 4, 256, 512
RET_QK, RET_V = RET_HEADS * RET_DK, RET_HEADS * RET_DV
DN_HEADS, DN_DK, DN_DV = 8, 128, 256
DN_QK, DN_V = DN_HEADS * DN_DK, DN_HEADS * DN_DV
DN_CONV_CH = 2 * DN_QK + DN_V
CONV_K = 4
ROPE_BASE = 10000.0
ADAM_LR, ADAM_B1, ADAM_B2, ADAM_EPS, ADAM_WD, ADAM_STEP = 0.001, 0.9, 0.999, 1e-08, 0.01, 10
N_DEV = 8
AXES = ("x", "y", "c")
LANES = 128
MIB = 1024 * 1024


def _tile(n_rows, cap):
    nch = n_rows // CHUNK
    best = 1
    for d in range(1, nch + 1):
        if nch % d == 0 and d * CHUNK <= cap:
            best = d
    return best * CHUNK


def _div_tile(n, cap, align):
    best = None
    for d in range(align, min(n, cap) + 1, align):
        if n % d == 0:
            best = d
    return best if best is not None else n


def _params(sem, vmem_mb):
    return pltpu.CompilerParams(dimension_semantics=sem, vmem_limit_bytes=int(vmem_mb * MIB))


def _nn(a, b, precision=None):
    return jnp.dot(a, b, preferred_element_type=f32, precision=precision)


def _nt(a, b, precision=None):
    return lax.dot_general(a, b, (((1,), (1,)), ((), ())), preferred_element_type=f32, precision=precision)


def _tn(a, b, precision=None):
    return lax.dot_general(a, b, (((0,), (0,)), ((), ())), preferred_element_type=f32, precision=precision)


def _b(x):
    return x.astype(bf16)


def _sigmoid(x):
    return 1.0 / (1.0 + jnp.exp(-x))


def _silu(x):
    return x * _sigmoid(x)


def _dsilu(x):
    s = _sigmoid(x)
    return s * (1.0 + x * (1.0 - s))


def _peer(k):
    x, y, c = lax.axis_index("x"), lax.axis_index("y"), lax.axis_index("c")
    px = 1 - x if k & 4 else x
    py = 1 - y if k & 2 else y
    pc = 1 - c if k & 1 else c
    return (px, py, pc), 4 * px + 2 * py + pc


class _Exchange:
    def __init__(self, arrs, gather):
        self.arrs, self.gather, self.n = list(arrs), gather, len(arrs)
        self.out_shapes = [jax.ShapeDtypeStruct(((N_DEV,) + a.shape) if gather else a.shape, a.dtype) for a in arrs]
        self.specs = [pl.BlockSpec(memory_space=pltpu.HBM)] * self.n
        self.scratch = [pltpu.SemaphoreType.DMA((self.n, N_DEV - 1)), pltpu.SemaphoreType.DMA((self.n, N_DEV - 1)),
                        pltpu.SemaphoreType.DMA((self.n,))]

    def _copies(self, ins, outs, sems):
        send_sems, recv_sems, local_sems = sems
        me = 4 * lax.axis_index("x") + 2 * lax.axis_index("y") + lax.axis_index("c")
        src = (lambda a, dest: ins[a]) if self.gather else (lambda a, dest: ins[a].at[dest])
        local = [pltpu.make_async_copy(src(a, me), outs[a].at[me], local_sems.at[a]) for a in range(self.n)]
        sends, lands = [], []
        for k in range(1, N_DEV):
            peer, pidx = _peer(k)
            for a in range(self.n):
                for dst, lst in ((outs[a].at[me], sends), (outs[a].at[pidx], lands)):
                    lst.append(pltpu.make_async_remote_copy(
                        src_ref=src(a, pidx), dst_ref=dst, send_sem=send_sems.at[a, k - 1],
                        recv_sem=recv_sems.at[a, k - 1], device_id=peer, device_id_type=pl.DeviceIdType.MESH))
        return local, sends, lands

    def start(self, ins, outs, sems):
        local, sends, _ = self._copies(ins, outs, sems)
        for cp in local + sends:
            cp.start()

    def wait(self, ins, outs, sems):
        local, sends, lands = self._copies(ins, outs, sems)
        for cp in lands:
            cp.wait_recv()
        for cp in sends:
            cp.wait_send()
        for cp in local:
            cp.wait()


def _call(body, args, *, name, grid, in_specs, out_specs, out_shape, scratch=(), sem, vmem_mb, comm=None):
    if comm is None:
        out = pl.pallas_call(body, name=name, grid=grid, in_specs=list(in_specs), out_specs=list(out_specs),
                             out_shape=list(out_shape), scratch_shapes=list(scratch),
                             compiler_params=_params(sem, vmem_mb))(*args)
        return list(out)
    n_in, n_out, n_scr, nc = len(args), len(out_shape), len(scratch), comm.n

    def carried(*refs):
        ins, cin = refs[:n_in], refs[n_in:n_in + nc]
        o0 = n_in + nc
        outs, cout = refs[o0:o0 + n_out], refs[o0 + n_out:o0 + n_out + nc]
        s0 = o0 + n_out + nc
        scr, sems = refs[s0:s0 + n_scr], refs[s0 + n_scr:]
        first = functools.reduce(jnp.logical_and, [pl.program_id(i) == 0 for i in range(len(grid))])
        last = functools.reduce(jnp.logical_and, [pl.program_id(i) == grid[i] - 1 for i in range(len(grid))])

        @pl.when(first)
        def _():
            comm.start(cin, cout, sems)

        body(*ins, *outs, *scr)

        @pl.when(last)
        def _():
            comm.wait(cin, cout, sems)

    out = pl.pallas_call(
        carried, name=name, grid=grid, in_specs=list(in_specs) + comm.specs, out_specs=list(out_specs) + comm.specs,
        out_shape=list(out_shape) + comm.out_shapes, scratch_shapes=list(scratch) + comm.scratch,
        compiler_params=_params(("arbitrary",) * len(grid), vmem_mb))(*args, *comm.arrs)
    return list(out)


def _exchange(arrs, gather, name):
    comm = _Exchange(arrs, gather)

    def body(*refs):
        ins, outs, sems = refs[:comm.n], refs[comm.n:2 * comm.n], refs[2 * comm.n:]
        comm.start(ins, outs, sems)
        comm.wait(ins, outs, sems)

    return pl.pallas_call(body, name=name, in_specs=comm.specs, out_specs=comm.specs, out_shape=comm.out_shapes,
                          scratch_shapes=comm.scratch)(*comm.arrs)


def _matmul(a, b, *, mode, tm, tn, tk, name, out_dtype=f32, res=None, vmem_mb=48, comm=None):
    if mode == "nn":
        (m, k), (k2, n) = a.shape, b.shape
    elif mode == "nt":
        (m, k), (n, k2) = a.shape, b.shape
    else:
        (k, m), (k2, n) = a.shape, b.shape
    assert k == k2 and m % tm == 0 and n % tn == 0 and k % tk == 0, (name, a.shape, b.shape, tm, tn, tk)
    nk = k // tk
    dot = {"nn": _nn, "nt": _nt, "tn": _tn}[mode]
    a_spec = {"nn": pl.BlockSpec((tm, tk), lambda i, j, kk: (i, kk)),
              "nt": pl.BlockSpec((tm, tk), lambda i, j, kk: (i, kk)),
              "tn": pl.BlockSpec((tk, tm), lambda i, j, kk: (kk, i))}[mode]
    b_spec = {"nn": pl.BlockSpec((tk, tn), lambda i, j, kk: (kk, j)),
              "nt": pl.BlockSpec((tn, tk), lambda i, j, kk: (j, kk)),
              "tn": pl.BlockSpec((tk, tn), lambda i, j, kk: (kk, j))}[mode]
    o_spec = pl.BlockSpec((tm, tn), lambda i, j, kk: (i, j))
    has_res = res is not None

    def body(*refs):
        if has_res:
            a_ref, b_ref, r_ref, o_ref = refs[:4]
            rest = refs[4:]
        else:
            a_ref, b_ref, o_ref = refs[:3]
            r_ref = None
            rest = refs[3:]
        part = dot(_b(a_ref[...]), _b(b_ref[...]))
        if nk == 1:
            if has_res:
                part = part + r_ref[...]
            o_ref[...] = part.astype(out_dtype)
        else:
            acc_ref = rest[0]
            kk = pl.program_id(2)

            @pl.when(kk == 0)
            def _():
                acc_ref[...] = part

            @pl.when(kk > 0)
            def _():
                acc_ref[...] += part

            @pl.when(kk == nk - 1)
            def _():
                tot = acc_ref[...]
                if has_res:
                    tot = tot + r_ref[...]
                o_ref[...] = tot.astype(out_dtype)

    in_specs = [a_spec, b_spec] + ([o_spec] if has_res else [])
    args = (a, b) + ((res,) if has_res else ())
    out = _call(body, args, name=name, grid=(m // tm, n // tn, nk), in_specs=in_specs, out_specs=[o_spec],
                out_shape=[jax.ShapeDtypeStruct((m, n), out_dtype)],
                scratch=[pltpu.VMEM((tm, tn), f32)] if nk > 1 else [],
                sem=("parallel", "parallel", "arbitrary"), vmem_mb=vmem_mb, comm=comm)
    return out[0] if comm is None else (out[0], out[1:])


def _rms_fwd(h, w, name):
    l, d = h.shape
    tr = _tile(l, 512)

    def body(h_ref, w_ref, o_ref):
        x = h_ref[...]
        r = lax.rsqrt(jnp.mean(x * x, axis=-1, keepdims=True) + RMS_EPS)
        o_ref[...] = _b(x * r * w_ref[...])

    return pl.pallas_call(
        body, name=name, grid=(l // tr,),
        in_specs=[pl.BlockSpec((tr, d), lambda i: (i, 0)), pl.BlockSpec((1, d), lambda i: (0, 0))],
        out_specs=pl.BlockSpec((tr, d), lambda i: (i, 0)),
        out_shape=jax.ShapeDtypeStruct((l, d), bf16),
        compiler_params=_params(("parallel",), 32),
    )(h, w)


def _rms_bwd(h, w, dhn, dres, name):
    l, d = h.shape
    tr = _tile(l, 512)

    def body(h_ref, w_ref, g_ref, r_ref, dh_ref, dhb_ref, dw_ref):
        x = h_ref[...]
        r = lax.rsqrt(jnp.mean(x * x, axis=-1, keepdims=True) + RMS_EPS)
        xh = x * r
        g = g_ref[...]
        dxh = g * w_ref[...]
        dx = r * (dxh - xh * jnp.mean(dxh * xh, axis=-1, keepdims=True))
        dh = r_ref[...] + dx
        dh_ref[...] = dh
        dhb_ref[...] = _b(dh)
        dw = jnp.sum(g * xh, axis=0, keepdims=True)

        @pl.when(pl.program_id(0) == 0)
        def _():
            dw_ref[...] = dw

        @pl.when(pl.program_id(0) > 0)
        def _():
            dw_ref[...] += dw

    row = pl.BlockSpec((tr, d), lambda i: (i, 0))
    vec = pl.BlockSpec((1, d), lambda i: (0, 0))
    return pl.pallas_call(
        body, name=name, grid=(l // tr,), in_specs=[row, vec, row, row], out_specs=[row, row, vec],
        out_shape=[jax.ShapeDtypeStruct((l, d), f32), jax.ShapeDtypeStruct((l, d), bf16),
                   jax.ShapeDtypeStruct((1, d), f32)],
        compiler_params=_params(("arbitrary",), 40),
    )(h, w, dhn, dres)


def _final_loss(h, w, target, name):
    l, d = h.shape
    nch = l // CHUNK

    def body(h_ref, w_ref, t_ref, dh_ref, dhb_ref, dw_ref, loss_ref):
        n = pl.program_id(0)
        live = (n > 0).astype(f32)
        x = h_ref[...]
        r = lax.rsqrt(jnp.mean(x * x, axis=-1, keepdims=True) + RMS_EPS)
        xh = x * r
        wv = w_ref[...]
        err = (xh * wv - t_ref[...]) * live
        dy = err * (1.0 / d)
        dxh = dy * wv
        dx = r * (dxh - xh * jnp.mean(dxh * xh, axis=-1, keepdims=True))
        dh_ref[...] = dx
        dhb_ref[...] = _b(dx)
        dw = jnp.sum(dy * xh, axis=0, keepdims=True)
        part = 0.5 * jnp.sum(jnp.sum(err * err, axis=-1, keepdims=True) * (1.0 / d), axis=0, keepdims=True)
        part = jnp.broadcast_to(part, (1, LANES))

        @pl.when(n == 0)
        def _():
            dw_ref[...] = dw
            loss_ref[...] = part

        @pl.when(n > 0)
        def _():
            dw_ref[...] += dw
            loss_ref[...] += part

    row = pl.BlockSpec((CHUNK, d), lambda i: (i, 0))
    vec = pl.BlockSpec((1, d), lambda i: (0, 0))
    return pl.pallas_call(
        body, name=name, grid=(nch,),
        in_specs=[row, vec, pl.BlockSpec((CHUNK, d), lambda i: (jnp.maximum(i - 1, 0), 0))],
        out_specs=[row, row, vec, pl.BlockSpec((1, LANES), lambda i: (0, 0))],
        out_shape=[jax.ShapeDtypeStruct((l, d), f32), jax.ShapeDtypeStruct((l, d), bf16),
                   jax.ShapeDtypeStruct((1, d), f32), jax.ShapeDtypeStruct((1, LANES), f32)],
        compiler_params=_params(("arbitrary",), 32),
    )(h, w, target)


def _swiglu_fwd(ab, name):
    l, two_f = ab.shape
    fh = two_f // 2
    tr = _tile(l, 256)

    def body(a_ref, b_ref, o_ref):
        o_ref[...] = _b(_silu(a_ref[...]) * b_ref[...])

    return pl.pallas_call(
        body, name=name, grid=(l // tr,),
        in_specs=[pl.BlockSpec((tr, fh), lambda i: (i, 0)), pl.BlockSpec((tr, fh), lambda i: (i, 1))],
        out_specs=pl.BlockSpec((tr, fh), lambda i: (i, 0)),
        out_shape=jax.ShapeDtypeStruct((l, fh), bf16),
        compiler_params=_params(("parallel",), 32),
    )(ab, ab)


def _swiglu_bwd(ab, ds, name):
    l, two_f = ab.shape
    fh = two_f // 2
    tr = _tile(l, 256)

    def body(a_ref, b_ref, ds_ref, o_ref):
        a, bb, g = a_ref[...], b_ref[...], ds_ref[...]
        o_ref[:, :fh] = _b(g * bb * _dsilu(a))
        o_ref[:, fh:] = _b(g * _silu(a))

    lo = pl.BlockSpec((tr, fh), lambda i: (i, 0))
    hi = pl.BlockSpec((tr, fh), lambda i: (i, 1))
    return pl.pallas_call(
        body, name=name, grid=(l // tr,), in_specs=[lo, hi, lo],
        out_specs=pl.BlockSpec((tr, two_f), lambda i: (i, 0)),
        out_shape=jax.ShapeDtypeStruct((l, two_f), bf16),
        compiler_params=_params(("parallel",), 40),
    )(ab, ab, ds)


def _gnorm_fwd(o, proj, nw, heads, dv, gate_blk, name):
    l, hv = o.shape
    tr = _tile(l, 256)

    def body(o_ref, g_ref, w_ref, y_ref):
        wv = w_ref[...]
        for h in range(heads):
            sl = slice(h * dv, (h + 1) * dv)
            oh = o_ref[:, sl]
            r = lax.rsqrt(jnp.mean(oh * oh, axis=-1, keepdims=True) + RMS_EPS)
            y_ref[:, sl] = _b(oh * r * wv * _silu(g_ref[:, sl]))

    return pl.pallas_call(
        body, name=name, grid=(l // tr,),
        in_specs=[pl.BlockSpec((tr, hv), lambda i: (i, 0)), pl.BlockSpec((tr, hv), lambda i: (i, gate_blk)),
                  pl.BlockSpec((1, dv), lambda i: (0, 0))],
        out_specs=pl.BlockSpec((tr, hv), lambda i: (i, 0)),
        out_shape=jax.ShapeDtypeStruct((l, hv), bf16),
        compiler_params=_params(("parallel",), 32),
    )(o, proj, nw)


def _gnorm_bwd(o, proj, nw, dy, heads, dv, gate_blk, name):
    l, hv = o.shape
    tr = _tile(l, 256)

    def body(o_ref, g_ref, w_ref, dy_ref, do_ref, dg_ref, dw_ref):
        wv = w_ref[...]
        dw = jnp.zeros((1, dv), f32)
        for h in range(heads):
            sl = slice(h * dv, (h + 1) * dv)
            oh = o_ref[:, sl]
            g = g_ref[:, sl]
            dyh = dy_ref[:, sl]
            r = lax.rsqrt(jnp.mean(oh * oh, axis=-1, keepdims=True) + RMS_EPS)
            xh = oh * r
            dn = dyh * _silu(g)
            dg_ref[:, sl] = _b(dyh * (xh * wv) * _dsilu(g))
            dxh = dn * wv
            do_ref[:, sl] = r * (dxh - xh * jnp.mean(dxh * xh, axis=-1, keepdims=True))
            dw = dw + jnp.sum(dn * xh, axis=0, keepdims=True)

        @pl.when(pl.program_id(0) == 0)
        def _():
            dw_ref[...] = dw

        @pl.when(pl.program_id(0) > 0)
        def _():
            dw_ref[...] += dw

    row = pl.BlockSpec((tr, hv), lambda i: (i, 0))
    vec = pl.BlockSpec((1, dv), lambda i: (0, 0))
    return pl.pallas_call(
        body, name=name, grid=(l // tr,),
        in_specs=[row, pl.BlockSpec((tr, hv), lambda i: (i, gate_blk)), vec, row],
        out_specs=[row, row, vec],
        out_shape=[jax.ShapeDtypeStruct((l, hv), f32), jax.ShapeDtypeStruct((l, hv), bf16),
                   jax.ShapeDtypeStruct((1, dv), f32)],
        compiler_params=_params(("arbitrary",), 40),
    )(o, proj, nw, dy)


def _ret_prep(proj, cos, sin, name):
    l = proj.shape[0]
    tr = _tile(l, 256)
    half = RET_DK // 2
    scale = RET_DK ** -0.5

    def body(p_ref, c_ref, s_ref, o_ref):
        rows = pl.program_id(0) * tr + lax.broadcasted_iota(jnp.int32, (tr, 1), 0)
        kmul = jnp.where(rows >= PAD, scale, 0.0).astype(f32)
        c, s = c_ref[...], s_ref[...]
        for j in range(2 * RET_HEADS):
            t1 = p_ref[:, j * RET_DK: j * RET_DK + half]
            t2 = p_ref[:, j * RET_DK + half: (j + 1) * RET_DK]
            o1 = t1 * c - t2 * s
            o2 = t1 * s + t2 * c
            if j >= RET_HEADS:
                o1, o2 = o1 * kmul, o2 * kmul
            o_ref[:, j * RET_DK: j * RET_DK + half] = o1
            o_ref[:, j * RET_DK + half: (j + 1) * RET_DK] = o2

    wide = pl.BlockSpec((tr, 2 * RET_QK), lambda i: (i, 0))
    tab = pl.BlockSpec((tr, half), lambda i: (i, 0))
    return pl.pallas_call(
        body, name=name, grid=(l // tr,), in_specs=[wide, tab, tab], out_specs=wide,
        out_shape=jax.ShapeDtypeStruct((l, 2 * RET_QK), f32),
        compiler_params=_params(("parallel",), 32),
    )(proj, cos, sin)


def _ret_prep_bwd(dq, dk, cos, sin, name):
    l = dq.shape[0]
    tr = _tile(l, 256)
    half = RET_DK // 2
    scale = RET_DK ** -0.5

    def body(dq_ref, dk_ref, c_ref, s_ref, o_ref):
        rows = pl.program_id(0) * tr + lax.broadcasted_iota(jnp.int32, (tr, 1), 0)
        kmul = jnp.where(rows >= PAD, scale, 0.0).astype(f32)
        c, s = c_ref[...], s_ref[...]
        for j in range(2 * RET_HEADS):
            d_ref = dq_ref if j < RET_HEADS else dk_ref
            jj = j % RET_HEADS
            d1 = d_ref[:, jj * RET_DK: jj * RET_DK + half]
            d2 = d_ref[:, jj * RET_DK + half: (jj + 1) * RET_DK]
            if j >= RET_HEADS:
                d1, d2 = d1 * kmul, d2 * kmul
            o_ref[:, j * RET_DK: j * RET_DK + half] = _b(d1 * c + d2 * s)
            o_ref[:, j * RET_DK + half: (j + 1) * RET_DK] = _b(d2 * c - d1 * s)

    nar = pl.BlockSpec((tr, RET_QK), lambda i: (i, 0))
    wide = pl.BlockSpec((tr, 2 * RET_QK), lambda i: (i, 0))
    tab = pl.BlockSpec((tr, half), lambda i: (i, 0))
    return pl.pallas_call(
        body, name=name, grid=(l // tr,), in_specs=[nar, nar, tab, tab], out_specs=wide,
        out_shape=jax.ShapeDtypeStruct((l, 2 * RET_QK), bf16),
        compiler_params=_params(("parallel",), 32),
    )(dq, dk, cos, sin)


RET_BLOCK_CHUNKS = 3


def _ret_block(l):
    nch = l // CHUNK
    return RET_BLOCK_CHUNKS * CHUNK if nch % RET_BLOCK_CHUNKS == 0 else CHUNK


def _ret_decay(lg, rb):
    idx = lax.broadcasted_iota(jnp.int32, (rb, 1), 0).astype(f32)
    ri = lax.broadcasted_iota(jnp.int32, (rb, rb), 0)
    ci = lax.broadcasted_iota(jnp.int32, (rb, rb), 1)
    rel = (ri - ci).astype(f32)
    dmask = jnp.where(ri >= ci, jnp.exp(lg * jnp.maximum(rel, 0.0)), 0.0)
    xi = jnp.exp(lg * (idx + 1.0))
    zeta = jnp.exp(lg * (rb - 1.0 - idx))
    return dmask, xi, zeta


def _ret_scan_fwd(qk, proj, lgs, gcs, name, comm=None):
    l = qk.shape[0]
    rb = _ret_block(l)
    nb = l // rb

    def body(lg_ref, gc_ref, q_ref, k_ref, v_ref, o_ref, st_ref, s_ref):
        @pl.when(pl.program_id(0) == 0)
        def _():
            s_ref[...] = jnp.zeros_like(s_ref)

        for h in range(RET_HEADS):
            dmask, xi, zeta = _ret_decay(lg_ref[h], rb)
            q = q_ref[:, h * RET_DK:(h + 1) * RET_DK]
            k = k_ref[:, h * RET_DK:(h + 1) * RET_DK]
            vb = _b(v_ref[:, h * RET_DV:(h + 1) * RET_DV])
            s = s_ref[h]
            sb = _b(s)
            st_ref[0, h] = sb
            scores = _nt(_b(q), _b(k)) * dmask
            o_ref[:, h * RET_DV:(h + 1) * RET_DV] = _nn(_b(scores), vb) + _nn(_b(q * xi), sb)
            s_ref[h] = gc_ref[h] * s + _tn(_b(k * zeta), vb)

    smem = pl.BlockSpec(memory_space=pltpu.SMEM)
    return _call(
        body, (lgs, gcs, qk, qk, proj), name=name, grid=(nb,),
        in_specs=[smem, smem,
                  pl.BlockSpec((rb, RET_QK), lambda n: (n, 0)),
                  pl.BlockSpec((rb, RET_QK), lambda n: (n, 1)),
                  pl.BlockSpec((rb, RET_V), lambda n: (n, 1))],
        out_specs=[pl.BlockSpec((rb, RET_V), lambda n: (n, 0)),
                   pl.BlockSpec((1, RET_HEADS, RET_DK, RET_DV), lambda n: (n, 0, 0, 0))],
        out_shape=[jax.ShapeDtypeStruct((l, RET_V), f32),
                   jax.ShapeDtypeStruct((nb, RET_HEADS, RET_DK, RET_DV), bf16)],
        scratch=[pltpu.VMEM((RET_HEADS, RET_DK, RET_DV), f32)], sem=("arbitrary",), vmem_mb=40, comm=comm)


def _ret_scan_bwd(qk, proj, states, do, lgs, gcs, name, comm=None):
    l = qk.shape[0]
    rb = _ret_block(l)
    nb = l // rb

    def body(lg_ref, gc_ref, q_ref, k_ref, v_ref, st_ref, do_ref, dq_ref, dk_ref, dv_ref, ds_ref):
        @pl.when(pl.program_id(0) == 0)
        def _():
            ds_ref[...] = jnp.zeros_like(ds_ref)

        for h in range(RET_HEADS):
            dmask, xi, zeta = _ret_decay(lg_ref[h], rb)
            q = q_ref[:, h * RET_DK:(h + 1) * RET_DK]
            k = k_ref[:, h * RET_DK:(h + 1) * RET_DK]
            qb, kb = _b(q), _b(k)
            vb = _b(v_ref[:, h * RET_DV:(h + 1) * RET_DV])
            dob = _b(do_ref[:, h * RET_DV:(h + 1) * RET_DV])
            sb = st_ref[0, h]
            dsp = ds_ref[h]
            dspb = _b(dsp)
            scores = _nt(qb, kb) * dmask
            dscores = _b(_nt(dob, vb) * dmask)
            dq_ref[:, h * RET_DK:(h + 1) * RET_DK] = _nn(dscores, kb) + _nt(dob, sb) * xi
            dk_ref[:, h * RET_DK:(h + 1) * RET_DK] = _tn(dscores, qb) + _nt(vb, dspb) * zeta
            dv_ref[:, h * RET_DV:(h + 1) * RET_DV] = _b(_tn(_b(scores), dob) + _nn(_b(k * zeta), dspb))
            ds_ref[h] = gc_ref[h] * dsp + _tn(_b(q * xi), dob)

    smem = pl.BlockSpec(memory_space=pltpu.SMEM)
    rev = lambda n: nb - 1 - n
    return _call(
        body, (lgs, gcs, qk, qk, proj, states, do), name=name, grid=(nb,),
        in_specs=[smem, smem,
                  pl.BlockSpec((rb, RET_QK), lambda n: (rev(n), 0)),
                  pl.BlockSpec((rb, RET_QK), lambda n: (rev(n), 1)),
                  pl.BlockSpec((rb, RET_V), lambda n: (rev(n), 1)),
                  pl.BlockSpec((1, RET_HEADS, RET_DK, RET_DV), lambda n: (rev(n), 0, 0, 0)),
                  pl.BlockSpec((rb, RET_V), lambda n: (rev(n), 0))],
        out_specs=[pl.BlockSpec((rb, RET_QK), lambda n: (rev(n), 0)),
                   pl.BlockSpec((rb, RET_QK), lambda n: (rev(n), 0)),
                   pl.BlockSpec((rb, RET_V), lambda n: (rev(n), 0))],
        out_shape=[jax.ShapeDtypeStruct((l, RET_QK), f32), jax.ShapeDtypeStruct((l, RET_QK), f32),
                   jax.ShapeDtypeStruct((l, RET_V), bf16)],
        scratch=[pltpu.VMEM((RET_HEADS, RET_DK, RET_DV), f32)], sem=("arbitrary",), vmem_mb=40, comm=comm)


CONV_BLK = 1024
HALO = 8


def _shift_down(cur, prev, s):
    full = pltpu.roll(cur, s, 0)
    hr = pltpu.roll(cur[:HALO], s, 0)
    pr = pltpu.roll(prev, s, 0)
    rows = lax.broadcasted_iota(jnp.int32, (HALO, 1), 0)
    return full, jnp.where(rows < s, pr, hr)


def _conv_rows(x_ref, p_ref, w_ref, c_ref, i, tr):
    rows = i * tr + lax.broadcasted_iota(jnp.int32, (tr, 1), 0)
    cur = jnp.where(rows >= PAD, x_ref[...], 0.0)
    prow = i * tr - HALO + lax.broadcasted_iota(jnp.int32, (HALO, 1), 0)
    prev = jnp.where(prow >= PAD, p_ref[...], 0.0)
    w = w_ref[...]
    acc = cur * w[CONV_K - 1:CONV_K, :]
    head = cur[:HALO] * w[CONV_K - 1:CONV_K, :]
    for s in range(1, CONV_K):
        full, hd = _shift_down(cur, prev, s)
        wk = w[CONV_K - 1 - s:CONV_K - s, :]
        acc = acc + full * wk
        head = head + hd * wk
    c_ref[...] = acc
    c_ref[0:HALO, :] = head


def _l2_heads(a, scale):
    outs = []
    for h in range(CONV_BLK // DN_DK):
        ah = a[:, h * DN_DK:(h + 1) * DN_DK]
        outs.append(ah * (lax.rsqrt(jnp.sum(ah * ah, axis=-1, keepdims=True) + RMS_EPS) * scale))
    return outs


def _dn_conv_fwd(proj, conv_w, name):
    l = proj.shape[0]
    tr = _tile(l, 256)
    nblk = DN_CONV_CH // CONV_BLK

    def body(x_ref, p_ref, w_ref, o_ref, c_ref):
        i, j = pl.program_id(0), pl.program_id(1)
        _conv_rows(x_ref, p_ref, w_ref, c_ref, i, tr)
        a = _silu(c_ref[...])

        @pl.when(j == 0)
        def _():
            for h, v in enumerate(_l2_heads(a, DN_DK ** -0.5)):
                o_ref[:, h * DN_DK:(h + 1) * DN_DK] = v

        @pl.when(j == 1)
        def _():
            for h, v in enumerate(_l2_heads(a, 1.0)):
                o_ref[:, h * DN_DK:(h + 1) * DN_DK] = v

        @pl.when(j >= 2)
        def _():
            o_ref[...] = a

    hb = tr // HALO
    return pl.pallas_call(
        body, name=name, grid=(l // tr, nblk),
        in_specs=[pl.BlockSpec((tr, CONV_BLK), lambda i, j: (i, j)),
                  pl.BlockSpec((HALO, CONV_BLK), lambda i, j: (jnp.maximum(i * hb - 1, 0), j)),
                  pl.BlockSpec((CONV_K, CONV_BLK), lambda i, j: (0, j))],
        out_specs=pl.BlockSpec((tr, CONV_BLK), lambda i, j: (i, j)),
        out_shape=jax.ShapeDtypeStruct((l, DN_CONV_CH), f32),
        scratch_shapes=[pltpu.VMEM((tr, CONV_BLK), f32)],
        compiler_params=_params(("parallel", "parallel"), 32),
    )(proj, proj, conv_w)


def _dn_conv_bwd_a(proj, conv_w, dqkv, name):
    l = proj.shape[0]
    tr = _tile(l, 256)
    nblk = DN_CONV_CH // CONV_BLK

    def body(x_ref, p_ref, w_ref, d_ref, dc_ref, dw_ref, c_ref):
        j, i = pl.program_id(0), pl.program_id(1)
        _conv_rows(x_ref, p_ref, w_ref, c_ref, i, tr)
        c = c_ref[...]
        a = _silu(c)
        dsl = _dsilu(c)

        def l2_bwd(scale):
            for h in range(CONV_BLK // DN_DK):
                sl = slice(h * DN_DK, (h + 1) * DN_DK)
                ah = a[:, sl]
                r = lax.rsqrt(jnp.sum(ah * ah, axis=-1, keepdims=True) + RMS_EPS)
                yh = ah * r
                dy = d_ref[:, sl]
                da = (r * scale) * (dy - yh * jnp.sum(dy * yh, axis=-1, keepdims=True))
                dc_ref[:, sl] = da * dsl[:, sl]

        @pl.when(j == 0)
        def _():
            l2_bwd(DN_DK ** -0.5)

        @pl.when(j == 1)
        def _():
            l2_bwd(1.0)

        @pl.when(j >= 2)
        def _():
            dc_ref[...] = d_ref[...] * dsl

        dc = dc_ref[...]
        rows = i * tr + lax.broadcasted_iota(jnp.int32, (tr, 1), 0)
        cur = jnp.where(rows >= PAD, x_ref[...], 0.0)
        prow = i * tr - HALO + lax.broadcasted_iota(jnp.int32, (HALO, 1), 0)
        prev = jnp.where(prow >= PAD, p_ref[...], 0.0)
        hsel = lax.broadcasted_iota(jnp.int32, (tr, 1), 0) >= HALO
        parts = [None] * CONV_K
        parts[CONV_K - 1] = jnp.sum(dc * cur, axis=0, keepdims=True)
        for s in range(1, CONV_K):
            full, hd = _shift_down(cur, prev, s)
            tot = jnp.sum(jnp.where(hsel, dc * full, 0.0), axis=0, keepdims=True)
            tot = tot + jnp.sum(dc[:HALO] * hd, axis=0, keepdims=True)
            parts[CONV_K - 1 - s] = tot
        ksel = lax.broadcasted_iota(jnp.int32, (CONV_K, 1), 0)
        dw = jnp.zeros((CONV_K, CONV_BLK), f32)
        for k in range(CONV_K):
            dw = dw + jnp.where(ksel == k, parts[k], 0.0)

        @pl.when(i == 0)
        def _():
            dw_ref[...] = dw

        @pl.when(i > 0)
        def _():
            dw_ref[...] += dw

    hb = tr // HALO
    blk = pl.BlockSpec((tr, CONV_BLK), lambda j, i: (i, j))
    return pl.pallas_call(
        body, name=name, grid=(nblk, l // tr),
        in_specs=[blk, pl.BlockSpec((HALO, CONV_BLK), lambda j, i: (jnp.maximum(i * hb - 1, 0), j)),
                  pl.BlockSpec((CONV_K, CONV_BLK), lambda j, i: (0, j)), blk],
        out_specs=[blk, pl.BlockSpec((CONV_K, CONV_BLK), lambda j, i: (0, j))],
        out_shape=[jax.ShapeDtypeStruct((l, DN_CONV_CH), f32), jax.ShapeDtypeStruct((CONV_K, DN_CONV_CH), f32)],
        scratch_shapes=[pltpu.VMEM((tr, CONV_BLK), f32)],
        compiler_params=_params(("parallel", "arbitrary"), 40),
    )(proj, proj, conv_w, dqkv)


def _dn_conv_bwd_b(dc, conv_w, name):
    l = dc.shape[0]
    tr = _tile(l, 256)
    nblk = DN_CONV_CH // CONV_BLK
    nrow = l // tr

    def body(d_ref, n_ref, w_ref, o_ref, t_ref):
        i = pl.program_id(0)
        cur = d_ref[...]
        nxt = jnp.where(i < nrow - 1, n_ref[...], 0.0)
        w = w_ref[...]
        acc = cur * w[CONV_K - 1:CONV_K, :]
        tail = cur[tr - HALO:] * w[CONV_K - 1:CONV_K, :]
        rows8 = lax.broadcasted_iota(jnp.int32, (HALO, 1), 0)
        for s in range(1, CONV_K):
            wk = w[CONV_K - 1 - s:CONV_K - s, :]
            acc = acc + pltpu.roll(cur, tr - s, 0) * wk
            tl = jnp.where(rows8 >= HALO - s, pltpu.roll(nxt, HALO - s, 0), pltpu.roll(cur[tr - HALO:], HALO - s, 0))
            tail = tail + tl * wk
        t_ref[...] = acc
        t_ref[tr - HALO:, :] = tail
        rows = i * tr + lax.broadcasted_iota(jnp.int32, (tr, 1), 0)
        o_ref[...] = _b(jnp.where(rows >= PAD, t_ref[...], 0.0))

    hb = tr // HALO
    nh = l // HALO
    return pl.pallas_call(
        body, name=name, grid=(nrow, nblk),
        in_specs=[pl.BlockSpec((tr, CONV_BLK), lambda i, j: (i, j)),
                  pl.BlockSpec((HALO, CONV_BLK), lambda i, j: (jnp.minimum((i + 1) * hb, nh - 1), j)),
                  pl.BlockSpec((CONV_K, CONV_BLK), lambda i, j: (0, j))],
        out_specs=pl.BlockSpec((tr, CONV_BLK), lambda i, j: (i, j)),
        out_shape=jax.ShapeDtypeStruct((l, DN_CONV_CH), bf16),
        scratch_shapes=[pltpu.VMEM((tr, CONV_BLK), f32)],
        compiler_params=_params(("parallel", "parallel"), 32),
    )(dc, dc, conv_w)


BA_W = LANES


def _dn_gates(ba_ref, al_ref, dt_ref, n):
    rows = n * CHUNK + lax.broadcasted_iota(jnp.int32, (CHUNK, 1), 0)
    vm = (rows >= PAD).astype(f32)
    bin_ = ba_ref[:, 0:DN_HEADS]
    z = ba_ref[:, DN_HEADS:2 * DN_HEADS] + dt_ref[...]
    sp = jnp.maximum(z, 0.0) + jnp.log1p(jnp.exp(-jnp.abs(z)))
    ea = jnp.exp(al_ref[...])
    beta = _sigmoid(bin_) * vm
    g = -ea * sp * vm
    return vm, bin_, z, ea, beta, g


def _tri():
    ri = lax.broadcasted_iota(jnp.int32, (CHUNK, CHUNK), 0)
    ci = lax.broadcasted_iota(jnp.int32, (CHUNK, CHUNK), 1)
    return ri, ci


def _split(a):
    hi = _b(a)
    return hi, _b(a - hi.astype(f32))


def _mm3(a, b, dot=_nn):
    (ah, al), (bh, bl) = _split(a), _split(b)
    return dot(ah, bh) + (dot(ah, bl) + dot(al, bh))


def _cumsum_rows(tri, g):
    tb = _b(tri)
    g1 = _b(g)
    r1 = g - g1.astype(f32)
    g2 = _b(r1)
    g3 = _b(r1 - g2.astype(f32))
    return _nn(tb, g1) + (_nn(tb, g2) + _nn(tb, g3))


DN_SCAN_CHUNKS = 3


def _scan_chunks(nch):
    return DN_SCAN_CHUNKS if nch % DN_SCAN_CHUNKS == 0 else 1


def _dn_prep(qkv, ba, a_log, dt_bias, name):
    l = qkv.shape[0]
    nch = l // CHUNK
    heads = range(DN_HEADS)

    def body(q_ref, k_ref, v_ref, ba_ref, al_ref, dt_ref, t_ref, u_ref, wq_ref, pk_ref, eg_ref, kpt_ref, qwt_ref):
        n = pl.program_id(0)
        _, _, _, _, beta, g = _dn_gates(ba_ref, al_ref, dt_ref, n)
        ri, ci = _tri()
        incl, strict = ri >= ci, ri > ci
        eye = (ri == ci).astype(f32)
        gam = _cumsum_rows(incl.astype(f32), g)
        gam_t = gam.T
        gc = [gam[:, h:h + 1] for h in heads]
        bh = [beta[:, h:h + 1] for h in heads]
        kh = [k_ref[:, h * DN_DK:(h + 1) * DN_DK] for h in heads]
        kb = [_b(k) for k in kh]
        decay = [jnp.exp(jnp.where(incl, gc[h] - gam_t[h:h + 1, :], -jnp.inf)) for h in heads]
        a = [jnp.where(strict, bh[h] * _nt(kb[h], kb[h]) * decay[h], 0.0) for h in heads]
        t = [eye - a[h] for h in heads]
        p = a
        for _ in range(int(math.log2(CHUNK)) - 1):
            p = [_mm3(p[h], p[h]) for h in heads]
            t = [t[h] + _mm3(t[h], p[h]) for h in heads]
        eg = [jnp.exp(gc[h]) for h in heads]
        for h in heads:
            t_ref[0, h] = t[h]
            u_ref[:, h * DN_DV:(h + 1) * DN_DV] = _mm3(t[h], v_ref[:, h * DN_DV:(h + 1) * DN_DV] * bh[h])
            w = _mm3(t[h], kh[h] * (bh[h] * eg[h]))
            wq_ref[0, h, 0:CHUNK, :] = _b(w)
            qwt_ref[0, h, DN_DK:2 * DN_DK, :] = _b(w.T)
        for h in heads:
            qh = q_ref[:, h * DN_DK:(h + 1) * DN_DK]
            gl = gc[h][CHUNK - 1:CHUNK, :]
            qe = qh * eg[h]
            ke = kh[h] * jnp.exp(gl - gc[h])
            pmat = _nt(_b(qh), kb[h]) * decay[h]
            wq_ref[0, h, CHUNK:2 * CHUNK, :] = _b(qe)
            qwt_ref[0, h, 0:DN_DK, :] = _b(qe.T)
            pk_ref[0, h, 0:CHUNK, :] = _b(pmat)
            pk_ref[0, h, CHUNK:CHUNK + DN_DK, :] = _b(ke.T)
            kpt_ref[0, h, :, 0:DN_DK] = _b(ke)
            kpt_ref[0, h, :, DN_DK:DN_DK + CHUNK] = _b(pmat.T)
            eg_ref[0, h] = jnp.broadcast_to(jnp.exp(gl), (8, LANES))

    vec = pl.BlockSpec((1, DN_HEADS), lambda n: (0, 0))
    return pl.pallas_call(
        body, name=name, grid=(nch,),
        in_specs=[pl.BlockSpec((CHUNK, DN_QK), lambda n: (n, 0)), pl.BlockSpec((CHUNK, DN_QK), lambda n: (n, 1)),
                  pl.BlockSpec((CHUNK, DN_V), lambda n: (n, 1)), pl.BlockSpec((CHUNK, BA_W), lambda n: (n, 0)),
                  vec, vec],
        out_specs=[pl.BlockSpec((1, DN_HEADS, CHUNK, CHUNK), lambda n: (n, 0, 0, 0)),
                   pl.BlockSpec((CHUNK, DN_V), lambda n: (n, 0)),
                   pl.BlockSpec((1, DN_HEADS, 2 * CHUNK, DN_DK), lambda n: (n, 0, 0, 0)),
                   pl.BlockSpec((1, DN_HEADS, CHUNK + DN_DK, CHUNK), lambda n: (n, 0, 0, 0)),
                   pl.BlockSpec((1, DN_HEADS, 8, LANES), lambda n: (n, 0, 0, 0)),
                   pl.BlockSpec((1, DN_HEADS, CHUNK, DN_DK + CHUNK), lambda n: (n, 0, 0, 0)),
                   pl.BlockSpec((1, DN_HEADS, 2 * DN_DK, CHUNK), lambda n: (n, 0, 0, 0))],
        out_shape=[jax.ShapeDtypeStruct((nch, DN_HEADS, CHUNK, CHUNK), f32),
                   jax.ShapeDtypeStruct((l, DN_V), f32),
                   jax.ShapeDtypeStruct((nch, DN_HEADS, 2 * CHUNK, DN_DK), bf16),
                   jax.ShapeDtypeStruct((nch, DN_HEADS, CHUNK + DN_DK, CHUNK), bf16),
                   jax.ShapeDtypeStruct((nch, DN_HEADS, 8, LANES), f32),
                   jax.ShapeDtypeStruct((nch, DN_HEADS, CHUNK, DN_DK + CHUNK), bf16),
                   jax.ShapeDtypeStruct((nch, DN_HEADS, 2 * DN_DK, CHUNK), bf16)],
        compiler_params=_params(("parallel",), 40),
    )(qkv, qkv, qkv, ba, a_log, dt_bias)


def _dn_scan_fwd(u, wq, pk, egl, name):
    l = u.shape[0]
    nch = l // CHUNK
    cs = _scan_chunks(nch)

    def body(u_ref, wq_ref, pk_ref, eg_ref, o_ref, st_ref, vn_ref, s_ref):
        @pl.when(pl.program_id(0) == 0)
        def _():
            s_ref[...] = jnp.zeros_like(s_ref)

        for c in range(cs):
            rows = slice(c * CHUNK, (c + 1) * CHUNK)
            for h in range(DN_HEADS):
                cols = slice(h * DN_DV, (h + 1) * DN_DV)
                s = s_ref[h]
                sb = _b(s)
                st_ref[c, h] = sb
                x = _nn(wq_ref[c, h], sb)
                vnb = _b(u_ref[rows, cols] - x[0:CHUNK])
                vn_ref[rows, cols] = vnb
                y = _nn(pk_ref[c, h], vnb)
                o_ref[rows, cols] = x[CHUNK:2 * CHUNK] + y[0:CHUNK]
                s_ref[h] = eg_ref[c, h][0:1, 0:1] * s + y[CHUNK:CHUNK + DN_DK]

    return pl.pallas_call(
        body, name=name, grid=(nch // cs,),
        in_specs=[pl.BlockSpec((cs * CHUNK, DN_V), lambda n: (n, 0)),
                  pl.BlockSpec((cs, DN_HEADS, 2 * CHUNK, DN_DK), lambda n: (n, 0, 0, 0)),
                  pl.BlockSpec((cs, DN_HEADS, CHUNK + DN_DK, CHUNK), lambda n: (n, 0, 0, 0)),
                  pl.BlockSpec((cs, DN_HEADS, 8, LANES), lambda n: (n, 0, 0, 0))],
        out_specs=[pl.BlockSpec((cs * CHUNK, DN_V), lambda n: (n, 0)),
                   pl.BlockSpec((cs, DN_HEADS, DN_DK, DN_DV), lambda n: (n, 0, 0, 0)),
                   pl.BlockSpec((cs * CHUNK, DN_V), lambda n: (n, 0))],
        out_shape=[jax.ShapeDtypeStruct((l, DN_V), f32),
                   jax.ShapeDtypeStruct((nch, DN_HEADS, DN_DK, DN_DV), bf16),
                   jax.ShapeDtypeStruct((l, DN_V), bf16)],
        scratch_shapes=[pltpu.VMEM((DN_HEADS, DN_DK, DN_DV), f32)],
        compiler_params=_params(("arbitrary",), 40),
    )(u, wq, pk, egl)


def _dn_scan_bwd(do, kpt, qwt, egl, name):
    l = do.shape[0]
    nch = l // CHUNK
    cs = _scan_chunks(nch)
    nblk = nch // cs

    def body(do_ref, kpt_ref, qwt_ref, eg_ref, dvn_ref, dsp_ref, ds_ref):
        @pl.when(pl.program_id(0) == 0)
        def _():
            ds_ref[...] = jnp.zeros_like(ds_ref)

        for c in reversed(range(cs)):
            rows = slice(c * CHUNK, (c + 1) * CHUNK)
            for h in range(DN_HEADS):
                cols = slice(h * DN_DV, (h + 1) * DN_DV)
                dsp = ds_ref[h]
                dspb = _b(dsp)
                dsp_ref[c, h] = dspb
                dob = _b(do_ref[rows, cols])
                kpt_h = kpt_ref[c, h]
                dvn = _nn(kpt_h[:, 0:DN_DK], dspb) + _nn(kpt_h[:, DN_DK:DN_DK + CHUNK], dob)
                dvn_ref[rows, cols] = dvn
                qwt_h = qwt_ref[c, h]
                ds_ref[h] = (eg_ref[c, h][0:1, 0:1] * dsp + _nn(qwt_h[0:DN_DK], dob)
                             - _nn(qwt_h[DN_DK:2 * DN_DK], _b(dvn)))

    rev = lambda s: nblk - 1 - s
    return pl.pallas_call(
        body, name=name, grid=(nblk,),
        in_specs=[pl.BlockSpec((cs * CHUNK, DN_V), lambda s: (rev(s), 0)),
                  pl.BlockSpec((cs, DN_HEADS, CHUNK, DN_DK + CHUNK), lambda s: (rev(s), 0, 0, 0)),
                  pl.BlockSpec((cs, DN_HEADS, 2 * DN_DK, CHUNK), lambda s: (rev(s), 0, 0, 0)),
                  pl.BlockSpec((cs, DN_HEADS, 8, LANES), lambda s: (rev(s), 0, 0, 0))],
        out_specs=[pl.BlockSpec((cs * CHUNK, DN_V), lambda s: (rev(s), 0)),
                   pl.BlockSpec((cs, DN_HEADS, DN_DK, DN_DV), lambda s: (rev(s), 0, 0, 0))],
        out_shape=[jax.ShapeDtypeStruct((l, DN_V), f32),
                   jax.ShapeDtypeStruct((nch, DN_HEADS, DN_DK, DN_DV), bf16)],
        scratch_shapes=[pltpu.VMEM((DN_HEADS, DN_DK, DN_DV), f32)],
        compiler_params=_params(("arbitrary",), 40),
    )(do, kpt, qwt, egl)


def _dn_post_bwd(qkv, ba, a_log, dt_bias, states, dsp_all, tinv_all, u_all, wq, vn_all, do, dvn_all, name):
    l = qkv.shape[0]
    nch = l // CHUNK

    def body(q_ref, k_ref, v_ref, ba_ref, al_ref, dt_ref, st_ref, dsp_ref, t_ref, u_ref, wq_ref, vn_ref, do_ref,
             dvn_ref, dqkv_ref, dba_ref, dal_ref, ddt_ref):
        step = pl.program_id(0)
        n = step

        vm, bin_, z, ea, beta, g = _dn_gates(ba_ref, al_ref, dt_ref, n)
        ri, ci = _tri()
        incl, strict = ri >= ci, ri > ci
        gam = _cumsum_rows(incl.astype(f32), g)
        gam_t = gam.T
        lane8 = lax.broadcasted_iota(jnp.int32, (1, DN_HEADS), 1)
        sub8 = lax.broadcasted_iota(jnp.int32, (DN_HEADS, 1), 0)
        dbeta = jnp.zeros((CHUNK, DN_HEADS), f32)
        dgam = jnp.zeros((CHUNK, DN_HEADS), f32)
        dgam_neg_t = jnp.zeros((DN_HEADS, CHUNK), f32)
        last = (lax.broadcasted_iota(jnp.int32, (CHUNK, 1), 0) == CHUNK - 1).astype(f32)
        for h in range(DN_HEADS):
            gc, gr, bh = gam[:, h:h + 1], gam_t[h:h + 1, :], beta[:, h:h + 1]
            qh = q_ref[:, h * DN_DK:(h + 1) * DN_DK]
            kh = k_ref[:, h * DN_DK:(h + 1) * DN_DK]
            vh = v_ref[:, h * DN_DV:(h + 1) * DN_DV]
            doh = _b(do_ref[:, h * DN_DV:(h + 1) * DN_DV])
            u = u_ref[:, h * DN_DV:(h + 1) * DN_DV]
            wb = wq_ref[0, h, 0:CHUNK, :]
            tinv = t_ref[0, h]
            sb = st_ref[0, h]
            dspb = dsp_ref[0, h]
            vnb = vn_ref[:, h * DN_DV:(h + 1) * DN_DV]
            dvn = dvn_ref[:, h * DN_DV:(h + 1) * DN_DV]
            decay = jnp.exp(jnp.where(incl, gc - gr, -jnp.inf))
            qb, kb = _b(qh), _b(kh)
            kk = _nt(kb, kb)
            a = jnp.where(strict, bh * kk * decay, 0.0)
            eg = jnp.exp(gc)
            gl = gc[CHUNK - 1:CHUNK, :]
            egl = jnp.exp(gl)
            ekd = jnp.exp(gl - gc)
            p = _nt(qb, kb) * decay
            qe, ke = qh * eg, kh * ekd
            dvnb = _b(dvn)
            dpraw = _nt(doh, vnb)
            dqk = _b(dpraw * decay)
            dqe = _nt(doh, sb)
            dke = _nt(vnb, dspb)
            dqkv_ref[:, h * DN_DK:(h + 1) * DN_DK] = _nn(dqk, kb) + dqe * eg
            dk = _tn(dqk, qb) + dke * ekd
            dw = -_nt(dvnb, sb)
            dru = _mm3(tinv, dvn, _tn)
            drw = _mm3(tinv, dw, _tn)
            da = -(_nt(_b(dru), _b(u)) + _nt(_b(drw), wb))
            da = jnp.where(strict, da, 0.0)
            dqkv_ref[:, 2 * DN_QK + h * DN_DV:2 * DN_QK + (h + 1) * DN_DV] = bh * dru
            keg = kh * eg
            db_h = (jnp.sum(dru * vh, axis=-1, keepdims=True) + jnp.sum(drw * keg, axis=-1, keepdims=True)
                    + jnp.sum(da * kk * decay, axis=-1, keepdims=True))
            dkk = _b(da * bh * decay)
            dk = dk + (bh * eg) * drw + _nn(dkk, kb) + _tn(dkk, kb)
            dqkv_ref[:, DN_QK + h * DN_DK:DN_QK + (h + 1) * DN_DK] = dk
            mm = da * a + dpraw * p
            dgl = (jnp.sum(jnp.sum(dke * ke, axis=-1, keepdims=True), axis=0, keepdims=True)
                   + egl * jnp.sum(jnp.sum(dspb.astype(f32) * sb.astype(f32), axis=-1, keepdims=True),
                                   axis=0, keepdims=True))
            dg_h = (jnp.sum(mm, axis=-1, keepdims=True) + jnp.sum(drw * keg, axis=-1, keepdims=True) * bh
                    + jnp.sum(dqe * qe, axis=-1, keepdims=True) - jnp.sum(dke * ke, axis=-1, keepdims=True)
                    + last * dgl)
            dbeta = dbeta + jnp.where(lane8 == h, db_h, 0.0)
            dgam = dgam + jnp.where(lane8 == h, dg_h, 0.0)
            dgam_neg_t = dgam_neg_t + jnp.where(sub8 == h, jnp.sum(mm, axis=0, keepdims=True), 0.0)
        dgam = dgam - dgam_neg_t.T
        dg = _cumsum_rows((ri <= ci).astype(f32), dgam)
        sg = _sigmoid(bin_)
        dbin = dbeta * vm * sg * (1.0 - sg)
        dain = dg * (-ea) * vm * _sigmoid(z)
        dba_ref[...] = jnp.zeros_like(dba_ref)
        dba_ref[:, 0:DN_HEADS] = dbin
        dba_ref[:, DN_HEADS:2 * DN_HEADS] = dain
        dal = jnp.sum(dg * g, axis=0, keepdims=True)
        ddt = jnp.sum(dain, axis=0, keepdims=True)

        @pl.when(step == 0)
        def _():
            dal_ref[...] = dal
            ddt_ref[...] = ddt

        @pl.when(step > 0)
        def _():
            dal_ref[...] += dal
            ddt_ref[...] += ddt

    vec = pl.BlockSpec((1, DN_HEADS), lambda s: (0, 0))
    qs = pl.BlockSpec((CHUNK, DN_QK), lambda s: (s, 0))
    ks = pl.BlockSpec((CHUNK, DN_QK), lambda s: (s, 1))
    vs = pl.BlockSpec((CHUNK, DN_V), lambda s: (s, 1))
    v0 = pl.BlockSpec((CHUNK, DN_V), lambda s: (s, 0))
    st = pl.BlockSpec((1, DN_HEADS, DN_DK, DN_DV), lambda s: (s, 0, 0, 0))
    return pl.pallas_call(
        body, name=name, grid=(nch,),
        in_specs=[qs, ks, vs, pl.BlockSpec((CHUNK, BA_W), lambda s: (s, 0)), vec, vec, st, st,
                  pl.BlockSpec((1, DN_HEADS, CHUNK, CHUNK), lambda s: (s, 0, 0, 0)),
                  v0, pl.BlockSpec((1, DN_HEADS, 2 * CHUNK, DN_DK), lambda s: (s, 0, 0, 0)), v0, v0, v0],
        out_specs=[pl.BlockSpec((CHUNK, DN_CONV_CH), lambda s: (s, 0)),
                   pl.BlockSpec((CHUNK, BA_W), lambda s: (s, 0)), vec, vec],
        out_shape=[jax.ShapeDtypeStruct((l, DN_CONV_CH), f32), jax.ShapeDtypeStruct((l, BA_W), f32),
                   jax.ShapeDtypeStruct((1, DN_HEADS), f32), jax.ShapeDtypeStruct((1, DN_HEADS), f32)],
        compiler_params=_params(("arbitrary",), 48),
    )(qkv, qkv, qkv, ba, a_log, dt_bias, states, dsp_all, tinv_all, u_all, wq, vn_all, do, dvn_all)


def _ffn_fwd(h, nw, wgu, wd, tb, th, tag):
    d = h.shape[1]
    fh = wd.shape[0]
    hn = _rms_fwd(h, nw, f"{tag}_norm")
    ab = _matmul(hn, wgu, mode="nn", tm=tb, tn=512, tk=d, name=f"{tag}_gu")
    s = _swiglu_fwd(ab, f"{tag}_act")
    out = _matmul(s, wd, mode="nn", tm=th, tn=512, tk=fh // 2, res=h, name=f"{tag}_down")
    return out, (hn, ab, s)


def _ffn_bwd(dh, dhb, h, nw, wgu, wd, saved, tb, th, tag, plan):
    hn, ab, s = saved
    d = h.shape[1]
    fh = wd.shape[0]
    ds = _matmul(dhb, wd, mode="nt", tm=th, tn=fh // 2, tk=d, name=f"{tag}_b_ds")
    dwd = _matmul(s, dhb, mode="tn", tm=fh // 2, tn=d, tk=th, out_dtype=bf16, name=f"{tag}_b_dwd")
    dab = _swiglu_bwd(ab, ds, f"{tag}_b_act")
    dhn = plan.matmul(f"{tag}_b_dhn", dab, wgu, mode="nt", tm=th, tn=d, tk=fh // 2)
    dwgu = _matmul(hn, dab, mode="tn", tm=d, tn=fh // 2, tk=th, out_dtype=bf16, name=f"{tag}_b_dwgu")
    dh2, dh2b, dnw = _rms_bwd(h, nw, dhn, dh, f"{tag}_b_norm")
    return dh2, dh2b, dnw, dwgu, dwd


class _Plan:
    GATHERS = {"ret_proj": ("ret_out", "gate0", "up0"),
               "ret_scan": ("down0", "dn_in", "dn_out", "gate1", "up1", "down1")}
    SCATTERS = {"dn_b_dhn": ("gate1", "up1", "down1"), "ffn0_b_dhn": ("dn_out", "dn_in"),
                "ret_b_scan": ("gate0", "up0", "down0", "ret_out"), "ret_b_dhn": ("ret_in",)}

    def __init__(self, shards, wts):
        self.shards, self.wts, self.grads, self.parts = shards, wts, {}, {}

    def _exchange(self, stage):
        if self.shards is None:
            return None
        if stage in self.GATHERS:
            return _Exchange([self.shards[n] for n in self.GATHERS[stage]], True)
        if stage in self.SCATTERS:
            return _Exchange([self._dev_major(n) for n in self.SCATTERS[stage]], False)
        return None

    def _dev_major(self, name):
        g = self.grads
        if name[:-1] in ("gate", "up"):
            gu = g["gu" + name[-1]]
            fh = gu.shape[1] // 2
            part = gu[:, :fh] if name.startswith("gate") else gu[:, fh:]
            return _dev_major_cols(part, fh // N_DEV)
        if name[:-1] == "down":
            dwd = g[name]
            return dwd.reshape(N_DEV, dwd.shape[0] // N_DEV, dwd.shape[1])
        if name in ("ret_out", "dn_out"):
            return g[name].reshape(N_DEV, g[name].shape[0] // N_DEV, g[name].shape[1])
        return _dev_major_cols(g[name], self.shards[name].shape[-1])

    def _landed(self, stage, outs):
        if stage in self.SCATTERS:
            self.parts.update(zip(self.SCATTERS[stage], outs))
            return
        w = self.wts
        cols = lambda t: t.transpose(1, 0, 2).reshape(t.shape[1], N_DEV * t.shape[2])
        rows = lambda t: t.reshape(N_DEV * t.shape[1], t.shape[2])
        for name, t in zip(self.GATHERS[stage], outs):
            if name in ("ret_out", "dn_out") or name.startswith("down"):
                w[name] = rows(t)
            elif name == "dn_in":
                full = cols(t)
                n_main = DN_CONV_CH + DN_V
                w["dn_main"] = full[:, :n_main]
                w["dn_ba"] = jnp.pad(full[:, n_main:], ((0, 0), (0, BA_W - (full.shape[1] - n_main))))
            else:
                w[name] = cols(t)
        for layer in "01":
            if "gate" + layer in w and "up" + layer in w and "gu" + layer not in w:
                w["gu" + layer] = jnp.concatenate([w["gate" + layer], w["up" + layer]], axis=1)

    def matmul(self, stage, a, b, **kw):
        comm = self._exchange(stage)
        if comm is None:
            return _matmul(a, b, name=stage, **kw)
        out, landed = _matmul(a, b, name=stage, comm=comm, **kw)
        self._landed(stage, landed)
        return out

    def call(self, stage, fn, *args, n_out):
        comm = self._exchange(stage)
        out = fn(*args, stage, comm=comm)
        if comm is not None:
            self._landed(stage, out[n_out:])
        return out[:n_out]


def _local_step(x2, target, wts, shards=None):
    plan = _Plan(shards, wts)
    s_len, d = x2.shape
    l = s_len + CHUNK
    tb = _tile(l, 3072)
    th = tb // 2 if (tb // 2) % 16 == 0 else tb
    half = RET_DK // 2
    inv_freq = ROPE_BASE ** (-jnp.arange(half, dtype=f32) / half)
    ang = (jnp.arange(l) - PAD).astype(f32)[:, None] * inv_freq[None, :]
    cos, sin = jnp.cos(ang), jnp.sin(ang)
    lgs = jnp.log1p(-jnp.exp2(-5.0 - jnp.arange(RET_HEADS, dtype=f32)))
    gcs = jnp.exp(lgs * _ret_block(l))

    h0 = jnp.concatenate([jnp.zeros((PAD, d), f32), wts["meta"], x2], axis=0)
    mixw, ffnw = wts["mix_norm"], wts["ffn_norm"]

    hn0 = _rms_fwd(h0, mixw[0:1], "l0_norm")
    proj0 = plan.matmul("ret_proj", hn0, wts["ret_in"], mode="nn", tm=tb, tn=512, tk=d)
    qk0 = _ret_prep(proj0, cos, sin, "ret_prep")
    o0, st0 = plan.call("ret_scan", _ret_scan_fwd, qk0, proj0, lgs, gcs, n_out=2)
    y0 = _gnorm_fwd(o0, proj0, wts["ret_gn"], RET_HEADS, RET_DV, 2, "ret_gnorm")
    h1 = _matmul(y0, wts["ret_out"], mode="nn", tm=th, tn=512, tk=RET_V, res=h0, name="ret_out")
    h2, ffn0 = _ffn_fwd(h1, ffnw[0:1], wts["gu0"], wts["down0"], tb, th, "ffn0")

    hn2 = _rms_fwd(h2, mixw[1:2], "l1_norm")
    proj1 = _matmul(hn2, wts["dn_main"], mode="nn", tm=tb, tn=512, tk=d, name="dn_proj")
    ba = _matmul(hn2, wts["dn_ba"], mode="nn", tm=tb, tn=BA_W, tk=d, name="dn_proj_ba")
    qkv1 = _dn_conv_fwd(proj1, wts["conv_w"], "dn_conv")
    tinv1, u1, wq1, pk1, egl1, kpt1, qwt1 = _dn_prep(qkv1, ba, wts["a_log"], wts["dt_bias"], "dn_prep")
    o1, st1, vn1 = _dn_scan_fwd(u1, wq1, pk1, egl1, "dn_scan")
    y1 = _gnorm_fwd(o1, proj1, wts["dn_norm"], DN_HEADS, DN_DV, 2, "dn_gnorm")
    h3 = _matmul(y1, wts["dn_out"], mode="nn", tm=th, tn=512, tk=DN_V, res=h2, name="dn_out")
    h4, ffn1 = _ffn_fwd(h3, ffnw[1:2], wts["gu1"], wts["down1"], tb, th, "ffn1")

    dh4, dh4b, dfinal, loss = _final_loss(h4, wts["final_norm"], target, "final_loss")
    gr = plan.grads
    dh3, dh3b, dffn1, gr["gu1"], gr["down1"] = _ffn_bwd(dh4, dh4b, h3, ffnw[1:2], wts["gu1"], wts["down1"], ffn1,
                                                       tb, th, "ffn1", plan)

    dy1 = _matmul(dh3b, wts["dn_out"], mode="nt", tm=th, tn=1024, tk=d, name="dn_b_dy")
    gr["dn_out"] = _matmul(y1, dh3b, mode="tn", tm=1024, tn=d, tk=tb, out_dtype=bf16, name="dn_b_dwout")
    do1, dgate1, ddn_norm = _gnorm_bwd(o1, proj1, wts["dn_norm"], dy1, DN_HEADS, DN_DV, 2, "dn_b_gnorm")
    dvn1, dsp1 = _dn_scan_bwd(do1, kpt1, qwt1, egl1, "dn_b_scan")
    dqkv1, dba, dalog, ddt = _dn_post_bwd(qkv1, ba, wts["a_log"], wts["dt_bias"], st1, dsp1, tinv1, u1, wq1, vn1,
                                          do1, dvn1, "dn_b_post")
    dc1, dconv = _dn_conv_bwd_a(proj1, wts["conv_w"], dqkv1, "dn_b_conv_a")
    dx1 = _dn_conv_bwd_b(dc1, wts["conv_w"], "dn_b_conv_b")
    dproj1 = jnp.concatenate([dx1, dgate1], axis=1)
    dbab = dba.astype(bf16)
    n_main = dproj1.shape[1]
    dhn2 = plan.matmul("dn_b_dhn", dproj1, wts["dn_main"], mode="nt", tm=th, tn=d, tk=n_main // 4)
    dhn2 = _matmul(dbab, wts["dn_ba"], mode="nt", tm=th, tn=d, tk=BA_W, res=dhn2, name="dn_b_dhn_ba")
    dw_main = _matmul(hn2, dproj1, mode="tn", tm=d, tn=n_main // 4, tk=th, out_dtype=bf16, name="dn_b_dwin")
    dw_ba = _matmul(hn2, dbab, mode="tn", tm=d, tn=BA_W, tk=tb, out_dtype=bf16, name="dn_b_dwin_ba")
    gr["dn_in"] = jnp.concatenate([dw_main, dw_ba], axis=1)
    dh2, dh2b, dmix1 = _rms_bwd(h2, mixw[1:2], dhn2, dh3, "l1_b_norm")

    dh1, dh1b, dffn0, gr["gu0"], gr["down0"] = _ffn_bwd(dh2, dh2b, h1, ffnw[0:1], wts["gu0"], wts["down0"], ffn0,
                                                       tb, th, "ffn0", plan)

    dy0 = _matmul(dh1b, wts["ret_out"], mode="nt", tm=th, tn=1024, tk=d, name="ret_b_dy")
    gr["ret_out"] = _matmul(y0, dh1b, mode="tn", tm=1024, tn=d, tk=tb, out_dtype=bf16, name="ret_b_dwout")
    do0, dgate0, dret_gn = _gnorm_bwd(o0, proj0, wts["ret_gn"], dy0, RET_HEADS, RET_DV, 2, "ret_b_gnorm")
    dq0, dk0, dv0 = plan.call("ret_b_scan", _ret_scan_bwd, qk0, proj0, st0, do0, lgs, gcs, n_out=3)
    dqk0 = _ret_prep_bwd(dq0, dk0, cos, sin, "ret_b_prep")
    dproj0 = jnp.concatenate([dqk0, dv0, dgate0], axis=1)
    n_in = dproj0.shape[1]
    gr["ret_in"] = _matmul(hn0, dproj0, mode="tn", tm=d, tn=n_in // 4, tk=th, out_dtype=bf16, name="ret_b_dwin")
    dhn0 = plan.matmul("ret_b_dhn", dproj0, wts["ret_in"], mode="nt", tm=th, tn=d, tk=n_in // 4)
    dh0, _, dmix0 = _rms_bwd(h0, mixw[0:1], dhn0, dh1, "l0_b_norm")

    gr.update(meta=dh0[PAD:CHUNK], mix_norm=jnp.concatenate([dmix0, dmix1], axis=0),
              ffn_norm=jnp.concatenate([dffn0, dffn1], axis=0), ret_gn=dret_gn, conv_w=dconv, a_log=dalog,
              dt_bias=ddt, dn_norm=ddn_norm, final_norm=dfinal)
    return loss, dh0[CHUNK:], gr, plan


def _adamw_reduce(parts, w, m, v, name):
    _, r, c = parts.shape
    c_pad = -(-c // LANES) * LANES
    tr = _div_tile(r, max(8, (3 * MIB // 16) // c_pad // 8 * 8), 16)

    def body(p_ref, w_ref, m_ref, v_ref, g_ref, d_ref, nm_ref, nv_ref):
        g = p_ref[0].astype(f32)
        for s in range(1, N_DEV):
            g = g + p_ref[s].astype(f32)
        mm = ADAM_B1 * m_ref[...] + (1.0 - ADAM_B1) * g
        vv = ADAM_B2 * v_ref[...] + (1.0 - ADAM_B2) * (g * g)
        m_hat = mm / (1.0 - ADAM_B1 ** ADAM_STEP)
        v_hat = vv / (1.0 - ADAM_B2 ** ADAM_STEP)
        g_ref[...] = g
        d_ref[...] = -ADAM_LR * (m_hat / (jnp.sqrt(v_hat) + ADAM_EPS) + ADAM_WD * w_ref[...])
        nm_ref[...] = mm
        nv_ref[...] = vv

    blk = pl.BlockSpec((tr, c), lambda i: (i, 0))
    return pl.pallas_call(
        body, name=name, grid=(r // tr,),
        in_specs=[pl.BlockSpec((N_DEV, tr, c), lambda i: (0, i, 0)), blk, blk, blk], out_specs=[blk] * 4,
        out_shape=[jax.ShapeDtypeStruct((r, c), f32)] * 4,
        compiler_params=_params(("parallel",), 48),
    )(parts, w, m, v)


def _dev_major_cols(g, width):
    r = g.shape[0]
    return g[:, :N_DEV * width].reshape(r, N_DEV, width).transpose(1, 0, 2)


def kernel(x, meta_tokens, mix_norm_w, ffn_norm_w, ret_w_in, ret_gn_w, ret_w_out, dn_w_in, dn_conv_w, dn_a_log, dn_dt_bias, dn_norm_w, dn_w_out, ffn_w_gate, ffn_w_up, ffn_w_down, final_norm_w, loss_target, m_meta_tokens, m_mix_norm_w, m_ffn_norm_w, m_ret_w_in, m_ret_gn_w, m_ret_w_out, m_dn_w_in, m_dn_conv_w, m_dn_a_log, m_dn_dt_bias, m_dn_norm_w, m_dn_w_out, m_ffn_w_gate, m_ffn_w_up, m_ffn_w_down, m_final_norm_w, v_meta_tokens, v_mix_norm_w, v_ffn_norm_w, v_ret_w_in, v_ret_gn_w, v_ret_w_out, v_dn_w_in, v_dn_conv_w, v_dn_a_log, v_dn_dt_bias, v_dn_norm_w, v_dn_w_out, v_ffn_w_gate, v_ffn_w_up, v_ffn_w_down, v_final_norm_w):
    d = x.shape[-1]
    me = 4 * lax.axis_index("x") + 2 * lax.axis_index("y") + lax.axis_index("c")

    shards = dict(ret_in=ret_w_in[0].astype(bf16), ret_out=ret_w_out[0].astype(bf16),
                  dn_in=dn_w_in[0].astype(bf16), dn_out=dn_w_out[0].astype(bf16))
    for layer in (0, 1):
        shards[f"gate{layer}"] = ffn_w_gate[layer].astype(bf16)
        shards[f"up{layer}"] = ffn_w_up[layer].astype(bf16)
        shards[f"down{layer}"] = ffn_w_down[layer].astype(bf16)
    g_ret_in, g_meta, g_conv, g_dnn = _exchange([shards["ret_in"], meta_tokens, dn_conv_w[0], dn_norm_w], True,
                                                "gather_first")
    cols = lambda g: g.transpose(1, 0, 2).reshape(g.shape[1], N_DEV * g.shape[2])
    wts = dict(meta=cols(g_meta), mix_norm=mix_norm_w, ffn_norm=ffn_norm_w, ret_in=cols(g_ret_in), ret_gn=ret_gn_w,
               conv_w=cols(g_conv), a_log=dn_a_log, dt_bias=dn_dt_bias, dn_norm=cols(g_dnn),
               final_norm=final_norm_w.reshape(1, d))

    loss_part, grad_x, gr, plan = _local_step(x[0], loss_target[0], wts, shards)
    loss = lax.psum(loss_part[0, 0], AXES)

    pp = plan.parts
    both = lambda name: jnp.concatenate([pp[name + "0"], pp[name + "1"]], axis=1)
    big_parts = [pp["ret_in"], pp["ret_out"], pp["dn_in"], pp["dn_out"], both("gate"), both("up"), both("down")]
    big_names = ["ret_w_in", "ret_w_out", "dn_w_in", "dn_w_out", "ffn_w_gate", "ffn_w_up", "ffn_w_down"]
    big_w = [ret_w_in, ret_w_out, dn_w_in, dn_w_out, ffn_w_gate, ffn_w_up, ffn_w_down]
    big_m = [m_ret_w_in, m_ret_w_out, m_dn_w_in, m_dn_w_out, m_ffn_w_gate, m_ffn_w_up, m_ffn_w_down]
    big_v = [v_ret_w_in, v_ret_w_out, v_dn_w_in, v_dn_w_out, v_ffn_w_gate, v_ffn_w_up, v_ffn_w_down]
    res = {}
    for nm, parts, w_, m_, v_ in zip(big_names, big_parts, big_w, big_m, big_v):
        r2, c2 = parts.shape[1], parts.shape[2]
        outs = _adamw_reduce(parts, w_.reshape(r2, c2), m_.reshape(r2, c2), v_.reshape(r2, c2), f"adamw_{nm}")
        res[nm] = [o.reshape(w_.shape) for o in outs]

    small_names = ["meta_tokens", "mix_norm_w", "ffn_norm_w", "ret_gn_w", "dn_conv_w", "dn_a_log", "dn_dt_bias",
                   "dn_norm_w", "final_norm_w"]
    small_g = [gr["meta"], gr["mix_norm"], gr["ffn_norm"], gr["ret_gn"], gr["conv_w"], gr["a_log"], gr["dt_bias"],
               gr["dn_norm"], gr["final_norm"]]
    small_w = [meta_tokens, mix_norm_w, ffn_norm_w, ret_gn_w, dn_conv_w, dn_a_log, dn_dt_bias, dn_norm_w, final_norm_w]
    small_m = [m_meta_tokens, m_mix_norm_w, m_ffn_norm_w, m_ret_gn_w, m_dn_conv_w, m_dn_a_log, m_dn_dt_bias,
               m_dn_norm_w, m_final_norm_w]
    small_v = [v_meta_tokens, v_mix_norm_w, v_ffn_norm_w, v_ret_gn_w, v_dn_conv_w, v_dn_a_log, v_dn_dt_bias,
               v_dn_norm_w, v_final_norm_w]
    sharded = {"meta_tokens", "dn_conv_w", "dn_norm_w"}
    flat = jnp.concatenate([g.reshape(-1) for g in small_g])
    row = 8 * LANES
    n_flat = flat.shape[0]
    flat = jnp.pad(flat, (0, -n_flat % row)).reshape(-1, row)
    (gathered,) = _exchange([flat], True, "gather_small_grads")
    gathered = gathered.reshape(N_DEV, -1)
    pieces, off = [], 0
    for nm, g, w_ in zip(small_names, small_g, small_w):
        full = gathered[:, off:off + g.size].reshape((N_DEV,) + g.shape)
        off += g.size
        if nm in sharded:
            wloc = w_.shape[-1]
            full = lax.dynamic_slice_in_dim(full, me * wloc, wloc, axis=full.ndim - 1)
        pieces.append(full.reshape(N_DEV, -1))
    sizes = [p.shape[1] for p in pieces]
    n_loc = sum(sizes)
    pad_loc = -n_loc % row

    def pack(vs, lead):
        cat = jnp.concatenate([a.reshape(lead + (-1,)) for a in vs], axis=-1)
        cat = jnp.pad(cat, [(0, 0)] * len(lead) + [(0, pad_loc)])
        return cat.reshape(lead + (-1, row))

    outs = _adamw_reduce(pack(pieces, (N_DEV,)), pack(small_w, ()), pack(small_m, ()), pack(small_v, ()), "adamw_small")
    off = 0
    for nm, sz, w_ in zip(small_names, sizes, small_w):
        res[nm] = [o.reshape(-1)[off:off + sz].reshape(w_.shape) for o in outs]
        off += sz

    order = ["meta_tokens", "mix_norm_w", "ffn_norm_w", "ret_w_in", "ret_gn_w", "ret_w_out", "dn_w_in", "dn_conv_w",
             "dn_a_log", "dn_dt_bias", "dn_norm_w", "dn_w_out", "ffn_w_gate", "ffn_w_up", "ffn_w_down", "final_norm_w"]
    grad_x = grad_x.reshape(x.shape)
    return (loss, grad_x, *[res[nm][0] for nm in order], *[res[nm][1] for nm in order],
            *[res[nm][2] for nm in order], *[res[nm][3] for nm in order])
```

```python
import functools
import math

import jax
import jax.numpy as jnp
from jax import lax
from jax.experimental import pallas as pl
from jax.experimental.pallas import tpu as pltpu

f32 = jnp.float32
bf16 = jnp.bfloat16
HI = lax.Precision.HIGHEST

N_META = 16
CHUNK = 64
PAD = CHUNK - N_META
RMS_EPS = 1e-6
RET_HEADS, RET_DK, RET_DV = 4, 256, 512
RET_QK, RET_V = RET_HEADS * RET_DK, RET_HEADS * RET_DV
DN_HEADS, DN_DK, DN_DV = 8, 128, 256
DN_QK, DN_V = DN_HEADS * DN_DK, DN_HEADS * DN_DV
DN_CONV_CH = 2 * DN_QK + DN_V
CONV_K = 4
ROPE_BASE = 10000.0
ADAM_LR, ADAM_B1, ADAM_B2, ADAM_EPS, ADAM_WD, ADAM_STEP = 0.001, 0.9, 0.999, 1e-08, 0.01, 10
N_DEV = 8
AXES = ("x", "y", "c")
LANES = 128
MIB = 1024 * 1024


def _tile(n_rows, cap):
    nch = n_rows // CHUNK
    best = 1
    for d in range(1, nch + 1):
        if nch % d == 0 and d * CHUNK <= cap:
            best = d
    return best * CHUNK


def _div_tile(n, cap, align):
    best = None
    for d in range(align, min(n, cap) + 1, align):
        if n % d == 0:
            best = d
    return best if best is not None else n


def _params(sem, vmem_mb):
    return pltpu.CompilerParams(dimension_semantics=sem, vmem_limit_bytes=int(vmem_mb * MIB))


def _nn(a, b, precision=None):
    return jnp.dot(a, b, preferred_element_type=f32, precision=precision)


def _nt(a, b, precision=None):
    return lax.dot_general(a, b, (((1,), (1,)), ((), ())), preferred_element_type=f32, precision=precision)


def _tn(a, b, precision=None):
    return lax.dot_general(a, b, (((0,), (0,)), ((), ())), preferred_element_type=f32, precision=precision)


def _b(x):
    return x.astype(bf16)


def _sigmoid(x):
    return 1.0 / (1.0 + jnp.exp(-x))


def _silu(x):
    return x * _sigmoid(x)


def _dsilu(x):
    s = _sigmoid(x)
    return s * (1.0 + x * (1.0 - s))


def _peer(k):
    x, y, c = lax.axis_index("x"), lax.axis_index("y"), lax.axis_index("c")
    px = 1 - x if k & 4 else x
    py = 1 - y if k & 2 else y
    pc = 1 - c if k & 1 else c
    return (px, py, pc), 4 * px + 2 * py + pc


class _Exchange:
    def __init__(self, arrs, gather):
        self.arrs, self.gather, self.n = list(arrs), gather, len(arrs)
        self.out_shapes = [jax.ShapeDtypeStruct(((N_DEV,) + a.shape) if gather else a.shape, a.dtype) for a in arrs]
        self.specs = [pl.BlockSpec(memory_space=pltpu.HBM)] * self.n
        self.scratch = [pltpu.SemaphoreType.DMA((self.n, N_DEV - 1)), pltpu.SemaphoreType.DMA((self.n, N_DEV - 1)),
                        pltpu.SemaphoreType.DMA((self.n,))]

    def _copies(self, ins, outs, sems):
        send_sems, recv_sems, local_sems = sems
        me = 4 * lax.axis_index("x") + 2 * lax.axis_index("y") + lax.axis_index("c")
        src = (lambda a, dest: ins[a]) if self.gather else (lambda a, dest: ins[a].at[dest])
        local = [pltpu.make_async_copy(src(a, me), outs[a].at[me], local_sems.at[a]) for a in range(self.n)]
        sends, lands = [], []
        for k in range(1, N_DEV):
            peer, pidx = _peer(k)
            for a in range(self.n):
                for dst, lst in ((outs[a].at[me], sends), (outs[a].at[pidx], lands)):
                    lst.append(pltpu.make_async_remote_copy(
                        src_ref=src(a, pidx), dst_ref=dst, send_sem=send_sems.at[a, k - 1],
                        recv_sem=recv_sems.at[a, k - 1], device_id=peer, device_id_type=pl.DeviceIdType.MESH))
        return local, sends, lands

    def start(self, ins, outs, sems):
        local, sends, _ = self._copies(ins, outs, sems)
        for cp in local + sends:
            cp.start()

    def wait(self, ins, outs, sems):
        local, sends, lands = self._copies(ins, outs, sems)
        for cp in lands:
            cp.wait_recv()
        for cp in sends:
            cp.wait_send()
        for cp in local:
            cp.wait()


def _call(body, args, *, name, grid, in_specs, out_specs, out_shape, scratch=(), sem, vmem_mb, comm=None,
          aliases=None):
    aliases = aliases or {}
    if comm is None:
        out = pl.pallas_call(body, name=name, grid=grid, in_specs=list(in_specs), out_specs=list(out_specs),
                             out_shape=list(out_shape), scratch_shapes=list(scratch), input_output_aliases=aliases,
                             compiler_params=_params(sem, vmem_mb))(*args)
        return list(out)
    n_in, n_out, n_scr, nc = len(args), len(out_shape), len(scratch), comm.n

    def carried(*refs):
        ins, cin = refs[:n_in], refs[n_in:n_in + nc]
        o0 = n_in + nc
        outs, cout = refs[o0:o0 + n_out], refs[o0 + n_out:o0 + n_out + nc]
        s0 = o0 + n_out + nc
        scr, sems = refs[s0:s0 + n_scr], refs[s0 + n_scr:]
        first = functools.reduce(jnp.logical_and, [pl.program_id(i) == 0 for i in range(len(grid))])
        last = functools.reduce(jnp.logical_and, [pl.program_id(i) == grid[i] - 1 for i in range(len(grid))])

        @pl.when(first)
        def _():
            comm.start(cin, cout, sems)

        body(*ins, *outs, *scr)

        @pl.when(last)
        def _():
            comm.wait(cin, cout, sems)

    out = pl.pallas_call(
        carried, name=name, grid=grid, in_specs=list(in_specs) + comm.specs, out_specs=list(out_specs) + comm.specs,
        out_shape=list(out_shape) + comm.out_shapes, scratch_shapes=list(scratch) + comm.scratch,
        input_output_aliases=aliases,
        compiler_params=_params(("arbitrary",) * len(grid), vmem_mb))(*args, *comm.arrs)
    return list(out)


def _exchange(arrs, gather, name):
    comm = _Exchange(arrs, gather)

    def body(*refs):
        ins, outs, sems = refs[:comm.n], refs[comm.n:2 * comm.n], refs[2 * comm.n:]
        comm.start(ins, outs, sems)
        comm.wait(ins, outs, sems)

    return pl.pallas_call(body, name=name, in_specs=comm.specs, out_specs=comm.specs, out_shape=comm.out_shapes,
                          scratch_shapes=comm.scratch)(*comm.arrs)


def _matmul(a, b, *, mode, tm, tn, tk, name, out_dtype=f32, res=None, vmem_mb=48, comm=None):
    if mode == "nn":
        (m, k), (k2, n) = a.shape, b.shape
    elif mode == "nt":
        (m, k), (n, k2) = a.shape, b.shape
    else:
        (k, m), (k2, n) = a.shape, b.shape
    assert k == k2 and m % tm == 0 and n % tn == 0 and k % tk == 0, (name, a.shape, b.shape, tm, tn, tk)
    nk = k // tk
    dot = {"nn": _nn, "nt": _nt, "tn": _tn}[mode]
    a_spec = {"nn": pl.BlockSpec((tm, tk), lambda i, j, kk: (i, kk)),
              "nt": pl.BlockSpec((tm, tk), lambda i, j, kk: (i, kk)),
              "tn": pl.BlockSpec((tk, tm), lambda i, j, kk: (kk, i))}[mode]
    b_spec = {"nn": pl.BlockSpec((tk, tn), lambda i, j, kk: (kk, j)),
              "nt": pl.BlockSpec((tn, tk), lambda i, j, kk: (j, kk)),
              "tn": pl.BlockSpec((tk, tn), lambda i, j, kk: (kk, j))}[mode]
    o_spec = pl.BlockSpec((tm, tn), lambda i, j, kk: (i, j))
    has_res = res is not None

    def body(*refs):
        if has_res:
            a_ref, b_ref, r_ref, o_ref = refs[:4]
            rest = refs[4:]
        else:
            a_ref, b_ref, o_ref = refs[:3]
            r_ref = None
            rest = refs[3:]
        part = dot(_b(a_ref[...]), _b(b_ref[...]))
        if nk == 1:
            if has_res:
                part = part + r_ref[...]
            o_ref[...] = part.astype(out_dtype)
        else:
            acc_ref = rest[0]
            kk = pl.program_id(2)

            @pl.when(kk == 0)
            def _():
                acc_ref[...] = part

            @pl.when(kk > 0)
            def _():
                acc_ref[...] += part

            @pl.when(kk == nk - 1)
            def _():
                tot = acc_ref[...]
                if has_res:
                    tot = tot + r_ref[...]
                o_ref[...] = tot.astype(out_dtype)

    in_specs = [a_spec, b_spec] + ([o_spec] if has_res else [])
    args = (a, b) + ((res,) if has_res else ())
    out = _call(body, args, name=name, grid=(m // tm, n // tn, nk), in_specs=in_specs, out_specs=[o_spec],
                out_shape=[jax.ShapeDtypeStruct((m, n), out_dtype)],
                scratch=[pltpu.VMEM((tm, tn), f32)] if nk > 1 else [],
                sem=("parallel", "parallel", "arbitrary"), vmem_mb=vmem_mb, comm=comm)
    return out[0] if comm is None else (out[0], out[1:])


def _rms_fwd(h, w, name):
    l, d = h.shape
    tr = _tile(l, 512)

    def body(h_ref, w_ref, o_ref):
        x = h_ref[...]
        r = lax.rsqrt(jnp.mean(x * x, axis=-1, keepdims=True) + RMS_EPS)
        o_ref[...] = _b(x * r * w_ref[...])

    return pl.pallas_call(
        body, name=name, grid=(l // tr,),
        in_specs=[pl.BlockSpec((tr, d), lambda i: (i, 0)), pl.BlockSpec((1, d), lambda i: (0, 0))],
        out_specs=pl.BlockSpec((tr, d), lambda i: (i, 0)),
        out_shape=jax.ShapeDtypeStruct((l, d), bf16),
        compiler_params=_params(("parallel",), 32),
    )(h, w)


def _rms_bwd(h, w, dhn, dres, name):
    l, d = h.shape
    tr = _tile(l, 512)

    def body(h_ref, w_ref, g_ref, r_ref, dh_ref, dhb_ref, dw_ref):
        x = h_ref[...]
        r = lax.rsqrt(jnp.mean(x * x, axis=-1, keepdims=True) + RMS_EPS)
        xh = x * r
        g = g_ref[...]
        dxh = g * w_ref[...]
        dx = r * (dxh - xh * jnp.mean(dxh * xh, axis=-1, keepdims=True))
        dh = r_ref[...] + dx
        dh_ref[...] = dh
        dhb_ref[...] = _b(dh)
        dw = jnp.sum(g * xh, axis=0, keepdims=True)

        @pl.when(pl.program_id(0) == 0)
        def _():
            dw_ref[...] = dw

        @pl.when(pl.program_id(0) > 0)
        def _():
            dw_ref[...] += dw

    row = pl.BlockSpec((tr, d), lambda i: (i, 0))
    vec = pl.BlockSpec((1, d), lambda i: (0, 0))
    return pl.pallas_call(
        body, name=name, grid=(l // tr,), in_specs=[row, vec, row, row], out_specs=[row, row, vec],
        out_shape=[jax.ShapeDtypeStruct((l, d), f32), jax.ShapeDtypeStruct((l, d), bf16),
                   jax.ShapeDtypeStruct((1, d), f32)],
        compiler_params=_params(("arbitrary",), 40),
    )(h, w, dhn, dres)


def _final_loss(h, w, target, name):
    l, d = h.shape
    nch = l // CHUNK

    def body(h_ref, w_ref, t_ref, dh_ref, dhb_ref, dw_ref, loss_ref):
        n = pl.program_id(0)
        live = (n > 0).astype(f32)
        x = h_ref[...]
        r = lax.rsqrt(jnp.mean(x * x, axis=-1, keepdims=True) + RMS_EPS)
        xh = x * r
        wv = w_ref[...]
        err = (xh * wv - t_ref[...]) * live
        dy = err * (1.0 / d)
        dxh = dy * wv
        dx = r * (dxh - xh * jnp.mean(dxh * xh, axis=-1, keepdims=True))
        dh_ref[...] = dx
        dhb_ref[...] = _b(dx)
        dw = jnp.sum(dy * xh, axis=0, keepdims=True)
        part = 0.5 * jnp.sum(jnp.sum(err * err, axis=-1, keepdims=True) * (1.0 / d), axis=0, keepdims=True)
        part = jnp.broadcast_to(part, (1, LANES))

        @pl.when(n == 0)
        def _():
            dw_ref[...] = dw
            loss_ref[...] = part

        @pl.when(n > 0)
        def _():
            dw_ref[...] += dw
            loss_ref[...] += part

    row = pl.BlockSpec((CHUNK, d), lambda i: (i, 0))
    vec = pl.BlockSpec((1, d), lambda i: (0, 0))
    return pl.pallas_call(
        body, name=name, grid=(nch,),
        in_specs=[row, vec, pl.BlockSpec((CHUNK, d), lambda i: (jnp.maximum(i - 1, 0), 0))],
        out_specs=[row, row, vec, pl.BlockSpec((1, LANES), lambda i: (0, 0))],
        out_shape=[jax.ShapeDtypeStruct((l, d), f32), jax.ShapeDtypeStruct((l, d), bf16),
                   jax.ShapeDtypeStruct((1, d), f32), jax.ShapeDtypeStruct((1, LANES), f32)],
        compiler_params=_params(("arbitrary",), 32),
    )(h, w, target)


def _swiglu_fwd(ab, name):
    l, two_f = ab.shape
    fh = two_f // 2
    tr = _tile(l, 256)

    def body(a_ref, b_ref, o_ref):
        o_ref[...] = _b(_silu(a_ref[...]) * b_ref[...])

    return pl.pallas_call(
        body, name=name, grid=(l // tr,),
        in_specs=[pl.BlockSpec((tr, fh), lambda i: (i, 0)), pl.BlockSpec((tr, fh), lambda i: (i, 1))],
        out_specs=pl.BlockSpec((tr, fh), lambda i: (i, 0)),
        out_shape=jax.ShapeDtypeStruct((l, fh), bf16),
        compiler_params=_params(("parallel",), 32),
    )(ab, ab)


def _swiglu_bwd(ab, ds, name):
    l, two_f = ab.shape
    fh = two_f // 2
    tr = _tile(l, 256)

    def body(a_ref, b_ref, ds_ref, o_ref):
        a, bb, g = a_ref[...], b_ref[...], ds_ref[...]
        o_ref[:, :fh] = _b(g * bb * _dsilu(a))
        o_ref[:, fh:] = _b(g * _silu(a))

    lo = pl.BlockSpec((tr, fh), lambda i: (i, 0))
    hi = pl.BlockSpec((tr, fh), lambda i: (i, 1))
    return pl.pallas_call(
        body, name=name, grid=(l // tr,), in_specs=[lo, hi, lo],
        out_specs=pl.BlockSpec((tr, two_f), lambda i: (i, 0)),
        out_shape=jax.ShapeDtypeStruct((l, two_f), bf16),
        compiler_params=_params(("parallel",), 40),
    )(ab, ab, ds)


def _gnorm_fwd(o, proj, nw, heads, dv, gate_blk, name):
    l, hv = o.shape
    tr = _tile(l, 256)

    def body(o_ref, g_ref, w_ref, y_ref):
        wv = w_ref[...]
        for h in range(heads):
            sl = slice(h * dv, (h + 1) * dv)
            oh = o_ref[:, sl]
            r = lax.rsqrt(jnp.mean(oh * oh, axis=-1, keepdims=True) + RMS_EPS)
            y_ref[:, sl] = _b(oh * r * wv * _silu(g_ref[:, sl]))

    return pl.pallas_call(
        body, name=name, grid=(l // tr,),
        in_specs=[pl.BlockSpec((tr, hv), lambda i: (i, 0)), pl.BlockSpec((tr, hv), lambda i: (i, gate_blk)),
                  pl.BlockSpec((1, dv), lambda i: (0, 0))],
        out_specs=pl.BlockSpec((tr, hv), lambda i: (i, 0)),
        out_shape=jax.ShapeDtypeStruct((l, hv), bf16),
        compiler_params=_params(("parallel",), 32),
    )(o, proj, nw)


def _gnorm_bwd(o, proj, nw, dy, heads, dv, gate_blk, name):
    l, hv = o.shape
    tr = _tile(l, 256)

    def body(o_ref, g_ref, w_ref, dy_ref, do_ref, dg_ref, dw_ref):
        wv = w_ref[...]
        dw = jnp.zeros((1, dv), f32)
        for h in range(heads):
            sl = slice(h * dv, (h + 1) * dv)
            oh = o_ref[:, sl]
            g = g_ref[:, sl]
            dyh = dy_ref[:, sl]
            r = lax.rsqrt(jnp.mean(oh * oh, axis=-1, keepdims=True) + RMS_EPS)
            xh = oh * r
            dn = dyh * _silu(g)
            dg_ref[:, sl] = _b(dyh * (xh * wv) * _dsilu(g))
            dxh = dn * wv
            do_ref[:, sl] = r * (dxh - xh * jnp.mean(dxh * xh, axis=-1, keepdims=True))
            dw = dw + jnp.sum(dn * xh, axis=0, keepdims=True)

        @pl.when(pl.program_id(0) == 0)
        def _():
            dw_ref[...] = dw

        @pl.when(pl.program_id(0) > 0)
        def _():
            dw_ref[...] += dw

    row = pl.BlockSpec((tr, hv), lambda i: (i, 0))
    gate = pl.BlockSpec((tr, hv), lambda i: (i, gate_blk))
    vec = pl.BlockSpec((1, dv), lambda i: (0, 0))
    return pl.pallas_call(
        body, name=name, grid=(l // tr,),
        in_specs=[row, gate, vec, row],
        out_specs=[row, gate, vec],
        out_shape=[jax.ShapeDtypeStruct((l, hv), f32), jax.ShapeDtypeStruct(proj.shape, bf16),
                   jax.ShapeDtypeStruct((1, dv), f32)],
        compiler_params=_params(("arbitrary",), 40),
    )(o, proj, nw, dy)


def _ret_prep(proj, cos, sin, name):
    l = proj.shape[0]
    tr = _tile(l, 256)
    half = RET_DK // 2
    scale = RET_DK ** -0.5

    def body(p_ref, c_ref, s_ref, o_ref):
        rows = pl.program_id(0) * tr + lax.broadcasted_iota(jnp.int32, (tr, 1), 0)
        kmul = jnp.where(rows >= PAD, scale, 0.0).astype(f32)
        c, s = c_ref[...], s_ref[...]
        for j in range(2 * RET_HEADS):
            t1 = p_ref[:, j * RET_DK: j * RET_DK + half]
            t2 = p_ref[:, j * RET_DK + half: (j + 1) * RET_DK]
            o1 = t1 * c - t2 * s
            o2 = t1 * s + t2 * c
            if j >= RET_HEADS:
                o1, o2 = o1 * kmul, o2 * kmul
            o_ref[:, j * RET_DK: j * RET_DK + half] = o1
            o_ref[:, j * RET_DK + half: (j + 1) * RET_DK] = o2

    wide = pl.BlockSpec((tr, 2 * RET_QK), lambda i: (i, 0))
    tab = pl.BlockSpec((tr, half), lambda i: (i, 0))
    return pl.pallas_call(
        body, name=name, grid=(l // tr,), in_specs=[wide, tab, tab], out_specs=wide,
        out_shape=jax.ShapeDtypeStruct((l, 2 * RET_QK), f32),
        compiler_params=_params(("parallel",), 32),
    )(proj, cos, sin)


def _ret_prep_bwd(dq, dk, cos, sin, dproj, name):
    l = dq.shape[0]
    tr = _tile(l, 256)
    half = RET_DK // 2
    scale = RET_DK ** -0.5

    def body(dq_ref, dk_ref, c_ref, s_ref, _, o_ref):
        rows = pl.program_id(0) * tr + lax.broadcasted_iota(jnp.int32, (tr, 1), 0)
        kmul = jnp.where(rows >= PAD, scale, 0.0).astype(f32)
        c, s = c_ref[...], s_ref[...]
        for j in range(2 * RET_HEADS):
            d_ref = dq_ref if j < RET_HEADS else dk_ref
            jj = j % RET_HEADS
            d1 = d_ref[:, jj * RET_DK: jj * RET_DK + half]
            d2 = d_ref[:, jj * RET_DK + half: (jj + 1) * RET_DK]
            if j >= RET_HEADS:
                d1, d2 = d1 * kmul, d2 * kmul
            o_ref[:, j * RET_DK: j * RET_DK + half] = _b(d1 * c + d2 * s)
            o_ref[:, j * RET_DK + half: (j + 1) * RET_DK] = _b(d2 * c - d1 * s)

    nar = pl.BlockSpec((tr, RET_QK), lambda i: (i, 0))
    wide = pl.BlockSpec((tr, 2 * RET_QK), lambda i: (i, 0))
    tab = pl.BlockSpec((tr, half), lambda i: (i, 0))
    return pl.pallas_call(
        body, name=name, grid=(l // tr,), in_specs=[nar, nar, tab, tab, pl.BlockSpec(memory_space=pl.ANY)],
        out_specs=wide, out_shape=jax.ShapeDtypeStruct(dproj.shape, dproj.dtype), input_output_aliases={4: 0},
        compiler_params=_params(("parallel",), 32),
    )(dq, dk, cos, sin, dproj)


RET_BLOCK_CHUNKS = 3


def _ret_block(l):
    nch = l // CHUNK
    return RET_BLOCK_CHUNKS * CHUNK if nch % RET_BLOCK_CHUNKS == 0 else CHUNK


def _ret_decay(lg, rb):
    idx = lax.broadcasted_iota(jnp.int32, (rb, 1), 0).astype(f32)
    ri = lax.broadcasted_iota(jnp.int32, (rb, rb), 0)
    ci = lax.broadcasted_iota(jnp.int32, (rb, rb), 1)
    rel = (ri - ci).astype(f32)
    dmask = jnp.where(ri >= ci, jnp.exp(lg * jnp.maximum(rel, 0.0)), 0.0)
    xi = jnp.exp(lg * (idx + 1.0))
    zeta = jnp.exp(lg * (rb - 1.0 - idx))
    return dmask, xi, zeta


def _ret_scan_fwd(qk, proj, lgs, gcs, name, comm=None):
    l = qk.shape[0]
    rb = _ret_block(l)
    nb = l // rb

    def body(lg_ref, gc_ref, q_ref, k_ref, v_ref, o_ref, st_ref, s_ref):
        @pl.when(pl.program_id(0) == 0)
        def _():
            s_ref[...] = jnp.zeros_like(s_ref)

        for h in range(RET_HEADS):
            dmask, xi, zeta = _ret_decay(lg_ref[h], rb)
            q = q_ref[:, h * RET_DK:(h + 1) * RET_DK]
            k = k_ref[:, h * RET_DK:(h + 1) * RET_DK]
            vb = _b(v_ref[:, h * RET_DV:(h + 1) * RET_DV])
            s = s_ref[h]
            sb = _b(s)
            st_ref[0, h] = sb
            scores = _nt(_b(q), _b(k)) * dmask
            o_ref[:, h * RET_DV:(h + 1) * RET_DV] = _nn(_b(scores), vb) + _nn(_b(q * xi), sb)
            s_ref[h] = gc_ref[h] * s + _tn(_b(k * zeta), vb)

    smem = pl.BlockSpec(memory_space=pltpu.SMEM)
    return _call(
        body, (lgs, gcs, qk, qk, proj), name=name, grid=(nb,),
        in_specs=[smem, smem,
                  pl.BlockSpec((rb, RET_QK), lambda n: (n, 0)),
                  pl.BlockSpec((rb, RET_QK), lambda n: (n, 1)),
                  pl.BlockSpec((rb, RET_V), lambda n: (n, 1))],
        out_specs=[pl.BlockSpec((rb, RET_V), lambda n: (n, 0)),
                   pl.BlockSpec((1, RET_HEADS, RET_DK, RET_DV), lambda n: (n, 0, 0, 0))],
        out_shape=[jax.ShapeDtypeStruct((l, RET_V), f32),
                   jax.ShapeDtypeStruct((nb, RET_HEADS, RET_DK, RET_DV), bf16)],
        scratch=[pltpu.VMEM((RET_HEADS, RET_DK, RET_DV), f32)], sem=("arbitrary",), vmem_mb=40, comm=comm)


def _ret_scan_bwd(qk, proj, states, do, dproj, lgs, gcs, name, comm=None):
    l = qk.shape[0]
    rb = _ret_block(l)
    nb = l // rb

    def body(lg_ref, gc_ref, q_ref, k_ref, v_ref, st_ref, do_ref, _, dq_ref, dk_ref, dv_ref, ds_ref):
        @pl.when(pl.program_id(0) == 0)
        def _():
            ds_ref[...] = jnp.zeros_like(ds_ref)

        for h in range(RET_HEADS):
            dmask, xi, zeta = _ret_decay(lg_ref[h], rb)
            q = q_ref[:, h * RET_DK:(h + 1) * RET_DK]
            k = k_ref[:, h * RET_DK:(h + 1) * RET_DK]
            qb, kb = _b(q), _b(k)
            vb = _b(v_ref[:, h * RET_DV:(h + 1) * RET_DV])
            dob = _b(do_ref[:, h * RET_DV:(h + 1) * RET_DV])
            sb = st_ref[0, h]
            dsp = ds_ref[h]
            dspb = _b(dsp)
            scores = _nt(qb, kb) * dmask
            dscores = _b(_nt(dob, vb) * dmask)
            dq_ref[:, h * RET_DK:(h + 1) * RET_DK] = _nn(dscores, kb) + _nt(dob, sb) * xi
            dk_ref[:, h * RET_DK:(h + 1) * RET_DK] = _tn(dscores, qb) + _nt(vb, dspb) * zeta
            dv_ref[:, h * RET_DV:(h + 1) * RET_DV] = _b(_tn(_b(scores), dob) + _nn(_b(k * zeta), dspb))
            ds_ref[h] = gc_ref[h] * dsp + _tn(_b(q * xi), dob)

    smem = pl.BlockSpec(memory_space=pltpu.SMEM)
    rev = lambda n: nb - 1 - n
    return _call(
        body, (lgs, gcs, qk, qk, proj, states, do, dproj), name=name, grid=(nb,),
        in_specs=[smem, smem,
                  pl.BlockSpec((rb, RET_QK), lambda n: (rev(n), 0)),
                  pl.BlockSpec((rb, RET_QK), lambda n: (rev(n), 1)),
                  pl.BlockSpec((rb, RET_V), lambda n: (rev(n), 1)),
                  pl.BlockSpec((1, RET_HEADS, RET_DK, RET_DV), lambda n: (rev(n), 0, 0, 0)),
                  pl.BlockSpec((rb, RET_V), lambda n: (rev(n), 0)),
                  pl.BlockSpec(memory_space=pl.ANY)],
        out_specs=[pl.BlockSpec((rb, RET_QK), lambda n: (rev(n), 0)),
                   pl.BlockSpec((rb, RET_QK), lambda n: (rev(n), 0)),
                   pl.BlockSpec((rb, RET_V), lambda n: (rev(n), 1))],
        out_shape=[jax.ShapeDtypeStruct((l, RET_QK), f32), jax.ShapeDtypeStruct((l, RET_QK), f32),
                   jax.ShapeDtypeStruct(dproj.shape, dproj.dtype)],
        scratch=[pltpu.VMEM((RET_HEADS, RET_DK, RET_DV), f32)], sem=("arbitrary",), vmem_mb=40, comm=comm,
        aliases={7: 2})


CONV_BLK = 1024
HALO = 8


def _shift_down(cur, prev, s):
    full = pltpu.roll(cur, s, 0)
    hr = pltpu.roll(cur[:HALO], s, 0)
    pr = pltpu.roll(prev, s, 0)
    rows = lax.broadcasted_iota(jnp.int32, (HALO, 1), 0)
    return full, jnp.where(rows < s, pr, hr)


def _conv_rows(x_ref, p_ref, w_ref, c_ref, i, tr):
    rows = i * tr + lax.broadcasted_iota(jnp.int32, (tr, 1), 0)
    cur = jnp.where(rows >= PAD, x_ref[...], 0.0)
    prow = i * tr - HALO + lax.broadcasted_iota(jnp.int32, (HALO, 1), 0)
    prev = jnp.where(prow >= PAD, p_ref[...], 0.0)
    w = w_ref[...]
    acc = cur * w[CONV_K - 1:CONV_K, :]
    head = cur[:HALO] * w[CONV_K - 1:CONV_K, :]
    for s in range(1, CONV_K):
        full, hd = _shift_down(cur, prev, s)
        wk = w[CONV_K - 1 - s:CONV_K - s, :]
        acc = acc + full * wk
        head = head + hd * wk
    c_ref[...] = acc
    c_ref[0:HALO, :] = head


def _l2_heads(a, scale):
    outs = []
    for h in range(CONV_BLK // DN_DK):
        ah = a[:, h * DN_DK:(h + 1) * DN_DK]
        outs.append(ah * (lax.rsqrt(jnp.sum(ah * ah, axis=-1, keepdims=True) + RMS_EPS) * scale))
    return outs


def _dn_conv_fwd(proj, conv_w, name, comm=None):
    l = proj.shape[0]
    tr = _tile(l, 256)
    nblk = DN_CONV_CH // CONV_BLK

    def body(x_ref, p_ref, w_ref, o_ref, c_ref):
        i, j = pl.program_id(0), pl.program_id(1)
        _conv_rows(x_ref, p_ref, w_ref, c_ref, i, tr)
        a = _silu(c_ref[...])

        @pl.when(j == 0)
        def _():
            for h, v in enumerate(_l2_heads(a, DN_DK ** -0.5)):
                o_ref[:, h * DN_DK:(h + 1) * DN_DK] = v

        @pl.when(j == 1)
        def _():
            for h, v in enumerate(_l2_heads(a, 1.0)):
                o_ref[:, h * DN_DK:(h + 1) * DN_DK] = v

        @pl.when(j >= 2)
        def _():
            o_ref[...] = a

    hb = tr // HALO
    return _call(
        body, (proj, proj, conv_w), name=name, grid=(l // tr, nblk),
        in_specs=[pl.BlockSpec((tr, CONV_BLK), lambda i, j: (i, j)),
                  pl.BlockSpec((HALO, CONV_BLK), lambda i, j: (jnp.maximum(i * hb - 1, 0), j)),
                  pl.BlockSpec((CONV_K, CONV_BLK), lambda i, j: (0, j))],
        out_specs=[pl.BlockSpec((tr, CONV_BLK), lambda i, j: (i, j))],
        out_shape=[jax.ShapeDtypeStruct((l, DN_CONV_CH), f32)],
        scratch=[pltpu.VMEM((tr, CONV_BLK), f32)], sem=("parallel", "parallel"), vmem_mb=32, comm=comm)


def _dn_conv_bwd_a(proj, conv_w, dqkv, name, comm=None):
    l = proj.shape[0]
    tr = _tile(l, 256)
    nblk = DN_CONV_CH // CONV_BLK

    def body(x_ref, p_ref, w_ref, d_ref, dc_ref, dw_ref, c_ref):
        j, i = pl.program_id(0), pl.program_id(1)
        _conv_rows(x_ref, p_ref, w_ref, c_ref, i, tr)
        c = c_ref[...]
        a = _silu(c)
        dsl = _dsilu(c)

        def l2_bwd(scale):
            for h in range(CONV_BLK // DN_DK):
                sl = slice(h * DN_DK, (h + 1) * DN_DK)
                ah = a[:, sl]
                r = lax.rsqrt(jnp.sum(ah * ah, axis=-1, keepdims=True) + RMS_EPS)
                yh = ah * r
                dy = d_ref[:, sl]
                da = (r * scale) * (dy - yh * jnp.sum(dy * yh, axis=-1, keepdims=True))
                dc_ref[:, sl] = da * dsl[:, sl]

        @pl.when(j == 0)
        def _():
            l2_bwd(DN_DK ** -0.5)

        @pl.when(j == 1)
        def _():
            l2_bwd(1.0)

        @pl.when(j >= 2)
        def _():
            dc_ref[...] = d_ref[...] * dsl

        dc = dc_ref[...]
        rows = i * tr + lax.broadcasted_iota(jnp.int32, (tr, 1), 0)
        cur = jnp.where(rows >= PAD, x_ref[...], 0.0)
        prow = i * tr - HALO + lax.broadcasted_iota(jnp.int32, (HALO, 1), 0)
        prev = jnp.where(prow >= PAD, p_ref[...], 0.0)
        hsel = lax.broadcasted_iota(jnp.int32, (tr, 1), 0) >= HALO
        parts = [None] * CONV_K
        parts[CONV_K - 1] = jnp.sum(dc * cur, axis=0, keepdims=True)
        for s in range(1, CONV_K):
            full, hd = _shift_down(cur, prev, s)
            tot = jnp.sum(jnp.where(hsel, dc * full, 0.0), axis=0, keepdims=True)
            tot = tot + jnp.sum(dc[:HALO] * hd, axis=0, keepdims=True)
            parts[CONV_K - 1 - s] = tot
        ksel = lax.broadcasted_iota(jnp.int32, (CONV_K, 1), 0)
        dw = jnp.zeros((CONV_K, CONV_BLK), f32)
        for k in range(CONV_K):
            dw = dw + jnp.where(ksel == k, parts[k], 0.0)

        @pl.when(i == 0)
        def _():
            dw_ref[...] = dw

        @pl.when(i > 0)
        def _():
            dw_ref[...] += dw

    hb = tr // HALO
    blk = pl.BlockSpec((tr, CONV_BLK), lambda j, i: (i, j))
    return _call(
        body, (proj, proj, conv_w, dqkv), name=name, grid=(nblk, l // tr),
        in_specs=[blk, pl.BlockSpec((HALO, CONV_BLK), lambda j, i: (jnp.maximum(i * hb - 1, 0), j)),
                  pl.BlockSpec((CONV_K, CONV_BLK), lambda j, i: (0, j)), blk],
        out_specs=[blk, pl.BlockSpec((CONV_K, CONV_BLK), lambda j, i: (0, j))],
        out_shape=[jax.ShapeDtypeStruct((l, DN_CONV_CH), f32), jax.ShapeDtypeStruct((CONV_K, DN_CONV_CH), f32)],
        scratch=[pltpu.VMEM((tr, CONV_BLK), f32)], sem=("parallel", "arbitrary"), vmem_mb=40, comm=comm)


def _dn_conv_bwd_b(dc, conv_w, dproj, name):
    l = dc.shape[0]
    tr = _tile(l, 256)
    nblk = DN_CONV_CH // CONV_BLK
    nrow = l // tr

    def body(d_ref, n_ref, w_ref, _, o_ref, t_ref):
        i = pl.program_id(0)
        cur = d_ref[...]
        nxt = jnp.where(i < nrow - 1, n_ref[...], 0.0)
        w = w_ref[...]
        acc = cur * w[CONV_K - 1:CONV_K, :]
        tail = cur[tr - HALO:] * w[CONV_K - 1:CONV_K, :]
        rows8 = lax.broadcasted_iota(jnp.int32, (HALO, 1), 0)
        for s in range(1, CONV_K):
            wk = w[CONV_K - 1 - s:CONV_K - s, :]
            acc = acc + pltpu.roll(cur, tr - s, 0) * wk
            tl = jnp.where(rows8 >= HALO - s, pltpu.roll(nxt, HALO - s, 0), pltpu.roll(cur[tr - HALO:], HALO - s, 0))
            tail = tail + tl * wk
        t_ref[...] = acc
        t_ref[tr - HALO:, :] = tail
        rows = i * tr + lax.broadcasted_iota(jnp.int32, (tr, 1), 0)
        o_ref[...] = _b(jnp.where(rows >= PAD, t_ref[...], 0.0))

    hb = tr // HALO
    nh = l // HALO
    return pl.pallas_call(
        body, name=name, grid=(nrow, nblk),
        in_specs=[pl.BlockSpec((tr, CONV_BLK), lambda i, j: (i, j)),
                  pl.BlockSpec((HALO, CONV_BLK), lambda i, j: (jnp.minimum((i + 1) * hb, nh - 1), j)),
                  pl.BlockSpec((CONV_K, CONV_BLK), lambda i, j: (0, j)),
                  pl.BlockSpec(memory_space=pl.ANY)],
        out_specs=pl.BlockSpec((tr, CONV_BLK), lambda i, j: (i, j)),
        out_shape=jax.ShapeDtypeStruct(dproj.shape, dproj.dtype), input_output_aliases={3: 0},
        scratch_shapes=[pltpu.VMEM((tr, CONV_BLK), f32)],
        compiler_params=_params(("parallel", "parallel"), 32),
    )(dc, dc, conv_w, dproj)


BA_W = LANES


def _dn_gates(ba_ref, al_ref, dt_ref, n):
    rows = n * CHUNK + lax.broadcasted_iota(jnp.int32, (CHUNK, 1), 0)
    vm = (rows >= PAD).astype(f32)
    bin_ = ba_ref[:, 0:DN_HEADS]
    z = ba_ref[:, DN_HEADS:2 * DN_HEADS] + dt_ref[...]
    sp = jnp.maximum(z, 0.0) + jnp.log1p(jnp.exp(-jnp.abs(z)))
    ea = jnp.exp(al_ref[...])
    beta = _sigmoid(bin_) * vm
    g = -ea * sp * vm
    return vm, bin_, z, ea, beta, g


def _tri():
    ri = lax.broadcasted_iota(jnp.int32, (CHUNK, CHUNK), 0)
    ci = lax.broadcasted_iota(jnp.int32, (CHUNK, CHUNK), 1)
    return ri, ci


def _split(a):
    hi = _b(a)
    return hi, _b(a - hi.astype(f32))


def _mm3(a, b, dot=_nn):
    (ah, al), (bh, bl) = _split(a), _split(b)
    return dot(ah, bh) + (dot(ah, bl) + dot(al, bh))


def _cumsum_rows(tri, g):
    tb = _b(tri)
    g1 = _b(g)
    r1 = g - g1.astype(f32)
    g2 = _b(r1)
    g3 = _b(r1 - g2.astype(f32))
    return _nn(tb, g1) + (_nn(tb, g2) + _nn(tb, g3))


DN_SCAN_CHUNKS = 3


def _scan_chunks(nch):
    return DN_SCAN_CHUNKS if nch % DN_SCAN_CHUNKS == 0 else 1


def _dn_prep(qkv, ba, a_log, dt_bias, name):
    l = qkv.shape[0]
    nch = l // CHUNK
    heads = range(DN_HEADS)

    def body(q_ref, k_ref, v_ref, ba_ref, al_ref, dt_ref, t_ref, u_ref, wq_ref, pk_ref, eg_ref, kpt_ref, qwt_ref):
        n = pl.program_id(0)
        _, _, _, _, beta, g = _dn_gates(ba_ref, al_ref, dt_ref, n)
        ri, ci = _tri()
        incl, strict = ri >= ci, ri > ci
        eye = (ri == ci).astype(f32)
        gam = _cumsum_rows(incl.astype(f32), g)
        gam_t = gam.T
        gc = [gam[:, h:h + 1] for h in heads]
        bh = [beta[:, h:h + 1] for h in heads]
        kh = [k_ref[:, h * DN_DK:(h + 1) * DN_DK] for h in heads]
        kb = [_b(k) for k in kh]
        decay = [jnp.exp(jnp.where(incl, gc[h] - gam_t[h:h + 1, :], -jnp.inf)) for h in heads]
        a = [jnp.where(strict, bh[h] * _nt(kb[h], kb[h]) * decay[h], 0.0) for h in heads]
        t = [eye - a[h] for h in heads]
        p = a
        for _ in range(int(math.log2(CHUNK)) - 1):
            p = [_mm3(p[h], p[h]) for h in heads]
            t = [t[h] + _mm3(t[h], p[h]) for h in heads]
        eg = [jnp.exp(gc[h]) for h in heads]
        for h in heads:
            t_ref[0, h] = t[h]
            u_ref[:, h * DN_DV:(h + 1) * DN_DV] = _mm3(t[h], v_ref[:, h * DN_DV:(h + 1) * DN_DV] * bh[h])
            w = _mm3(t[h], kh[h] * (bh[h] * eg[h]))
            wq_ref[0, h, 0:CHUNK, :] = _b(w)
            qwt_ref[0, h, DN_DK:2 * DN_DK, :] = _b(w.T)
        for h in heads:
            qh = q_ref[:, h * DN_DK:(h + 1) * DN_DK]
            gl = gc[h][CHUNK - 1:CHUNK, :]
            qe = qh * eg[h]
            ke = kh[h] * jnp.exp(gl - gc[h])
            pmat = _nt(_b(qh), kb[h]) * decay[h]
            wq_ref[0, h, CHUNK:2 * CHUNK, :] = _b(qe)
            qwt_ref[0, h, 0:DN_DK, :] = _b(qe.T)
            pk_ref[0, h, 0:CHUNK, :] = _b(pmat)
            pk_ref[0, h, CHUNK:CHUNK + DN_DK, :] = _b(ke.T)
            kpt_ref[0, h, :, 0:DN_DK] = _b(ke)
            kpt_ref[0, h, :, DN_DK:DN_DK + CHUNK] = _b(pmat.T)
            eg_ref[0, h] = jnp.broadcast_to(jnp.exp(gl), (8, LANES))

    vec = pl.BlockSpec((1, DN_HEADS), lambda n: (0, 0))
    return pl.pallas_call(
        body, name=name, grid=(nch,),
        in_specs=[pl.BlockSpec((CHUNK, DN_QK), lambda n: (n, 0)), pl.BlockSpec((CHUNK, DN_QK), lambda n: (n, 1)),
                  pl.BlockSpec((CHUNK, DN_V), lambda n: (n, 1)), pl.BlockSpec((CHUNK, BA_W), lambda n: (n, 0)),
                  vec, vec],
        out_specs=[pl.BlockSpec((1, DN_HEADS, CHUNK, CHUNK), lambda n: (n, 0, 0, 0)),
                   pl.BlockSpec((CHUNK, DN_V), lambda n: (n, 0)),
                   pl.BlockSpec((1, DN_HEADS, 2 * CHUNK, DN_DK), lambda n: (n, 0, 0, 0)),
                   pl.BlockSpec((1, DN_HEADS, CHUNK + DN_DK, CHUNK), lambda n: (n, 0, 0, 0)),
                   pl.BlockSpec((1, DN_HEADS, 8, LANES), lambda n: (n, 0, 0, 0)),
                   pl.BlockSpec((1, DN_HEADS, CHUNK, DN_DK + CHUNK), lambda n: (n, 0, 0, 0)),
                   pl.BlockSpec((1, DN_HEADS, 2 * DN_DK, CHUNK), lambda n: (n, 0, 0, 0))],
        out_shape=[jax.ShapeDtypeStruct((nch, DN_HEADS, CHUNK, CHUNK), f32),
                   jax.ShapeDtypeStruct((l, DN_V), f32),
                   jax.ShapeDtypeStruct((nch, DN_HEADS, 2 * CHUNK, DN_DK), bf16),
                   jax.ShapeDtypeStruct((nch, DN_HEADS, CHUNK + DN_DK, CHUNK), bf16),
                   jax.ShapeDtypeStruct((nch, DN_HEADS, 8, LANES), f32),
                   jax.ShapeDtypeStruct((nch, DN_HEADS, CHUNK, DN_DK + CHUNK), bf16),
                   jax.ShapeDtypeStruct((nch, DN_HEADS, 2 * DN_DK, CHUNK), bf16)],
        compiler_params=_params(("parallel",), 40),
    )(qkv, qkv, qkv, ba, a_log, dt_bias)


def _dn_scan_fwd(u, wq, pk, egl, name):
    l = u.shape[0]
    nch = l // CHUNK
    cs = _scan_chunks(nch)

    def body(u_ref, wq_ref, pk_ref, eg_ref, o_ref, st_ref, vn_ref, s_ref):
        @pl.when(pl.program_id(0) == 0)
        def _():
            s_ref[...] = jnp.zeros_like(s_ref)

        for c in range(cs):
            rows = slice(c * CHUNK, (c + 1) * CHUNK)
            for h in range(DN_HEADS):
                cols = slice(h * DN_DV, (h + 1) * DN_DV)
                s = s_ref[h]
                sb = _b(s)
                st_ref[c, h] = sb
                x = _nn(wq_ref[c, h], sb)
                vnb = _b(u_ref[rows, cols] - x[0:CHUNK])
                vn_ref[rows, cols] = vnb
                y = _nn(pk_ref[c, h], vnb)
                o_ref[rows, cols] = x[CHUNK:2 * CHUNK] + y[0:CHUNK]
                s_ref[h] = eg_ref[c, h][0:1, 0:1] * s + y[CHUNK:CHUNK + DN_DK]

    return pl.pallas_call(
        body, name=name, grid=(nch // cs,),
        in_specs=[pl.BlockSpec((cs * CHUNK, DN_V), lambda n: (n, 0)),
                  pl.BlockSpec((cs, DN_HEADS, 2 * CHUNK, DN_DK), lambda n: (n, 0, 0, 0)),
                  pl.BlockSpec((cs, DN_HEADS, CHUNK + DN_DK, CHUNK), lambda n: (n, 0, 0, 0)),
                  pl.BlockSpec((cs, DN_HEADS, 8, LANES), lambda n: (n, 0, 0, 0))],
        out_specs=[pl.BlockSpec((cs * CHUNK, DN_V), lambda n: (n, 0)),
                   pl.BlockSpec((cs, DN_HEADS, DN_DK, DN_DV), lambda n: (n, 0, 0, 0)),
                   pl.BlockSpec((cs * CHUNK, DN_V), lambda n: (n, 0))],
        out_shape=[jax.ShapeDtypeStruct((l, DN_V), f32),
                   jax.ShapeDtypeStruct((nch, DN_HEADS, DN_DK, DN_DV), bf16),
                   jax.ShapeDtypeStruct((l, DN_V), bf16)],
        scratch_shapes=[pltpu.VMEM((DN_HEADS, DN_DK, DN_DV), f32)],
        compiler_params=_params(("arbitrary",), 40),
    )(u, wq, pk, egl)


def _dn_scan_bwd(do, kpt, qwt, egl, name):
    l = do.shape[0]
    nch = l // CHUNK
    cs = _scan_chunks(nch)
    nblk = nch // cs

    def body(do_ref, kpt_ref, qwt_ref, eg_ref, dvn_ref, dsp_ref, ds_ref):
        @pl.when(pl.program_id(0) == 0)
        def _():
            ds_ref[...] = jnp.zeros_like(ds_ref)

        for c in reversed(range(cs)):
            rows = slice(c * CHUNK, (c + 1) * CHUNK)
            for h in range(DN_HEADS):
                cols = slice(h * DN_DV, (h + 1) * DN_DV)
                dsp = ds_ref[h]
                dspb = _b(dsp)
                dsp_ref[c, h] = dspb
                dob = _b(do_ref[rows, cols])
                kpt_h = kpt_ref[c, h]
                dvn = _nn(kpt_h[:, 0:DN_DK], dspb) + _nn(kpt_h[:, DN_DK:DN_DK + CHUNK], dob)
                dvn_ref[rows, cols] = dvn
                qwt_h = qwt_ref[c, h]
                ds_ref[h] = (eg_ref[c, h][0:1, 0:1] * dsp + _nn(qwt_h[0:DN_DK], dob)
                             - _nn(qwt_h[DN_DK:2 * DN_DK], _b(dvn)))

    rev = lambda s: nblk - 1 - s
    return pl.pallas_call(
        body, name=name, grid=(nblk,),
        in_specs=[pl.BlockSpec((cs * CHUNK, DN_V), lambda s: (rev(s), 0)),
                  pl.BlockSpec((cs, DN_HEADS, CHUNK, DN_DK + CHUNK), lambda s: (rev(s), 0, 0, 0)),
                  pl.BlockSpec((cs, DN_HEADS, 2 * DN_DK, CHUNK), lambda s: (rev(s), 0, 0, 0)),
                  pl.BlockSpec((cs, DN_HEADS, 8, LANES), lambda s: (rev(s), 0, 0, 0))],
        out_specs=[pl.BlockSpec((cs * CHUNK, DN_V), lambda s: (rev(s), 0)),
                   pl.BlockSpec((cs, DN_HEADS, DN_DK, DN_DV), lambda s: (rev(s), 0, 0, 0))],
        out_shape=[jax.ShapeDtypeStruct((l, DN_V), f32),
                   jax.ShapeDtypeStruct((nch, DN_HEADS, DN_DK, DN_DV), bf16)],
        scratch_shapes=[pltpu.VMEM((DN_HEADS, DN_DK, DN_DV), f32)],
        compiler_params=_params(("arbitrary",), 40),
    )(do, kpt, qwt, egl)


def _dn_post_bwd(qkv, ba, a_log, dt_bias, states, dsp_all, tinv_all, u_all, wq, vn_all, do, dvn_all, name):
    l = qkv.shape[0]
    nch = l // CHUNK

    def body(q_ref, k_ref, v_ref, ba_ref, al_ref, dt_ref, st_ref, dsp_ref, t_ref, u_ref, wq_ref, vn_ref, do_ref,
             dvn_ref, dqkv_ref, dba_ref, dal_ref, ddt_ref):
        step = pl.program_id(0)
        n = step

        vm, bin_, z, ea, beta, g = _dn_gates(ba_ref, al_ref, dt_ref, n)
        ri, ci = _tri()
        incl, strict = ri >= ci, ri > ci
        gam = _cumsum_rows(incl.astype(f32), g)
        gam_t = gam.T
        lane8 = lax.broadcasted_iota(jnp.int32, (1, DN_HEADS), 1)
        sub8 = lax.broadcasted_iota(jnp.int32, (DN_HEADS, 1), 0)
        dbeta = jnp.zeros((CHUNK, DN_HEADS), f32)
        dgam = jnp.zeros((CHUNK, DN_HEADS), f32)
        dgam_neg_t = jnp.zeros((DN_HEADS, CHUNK), f32)
        last = (lax.broadcasted_iota(jnp.int32, (CHUNK, 1), 0) == CHUNK - 1).astype(f32)
        for h in range(DN_HEADS):
            gc, gr, bh = gam[:, h:h + 1], gam_t[h:h + 1, :], beta[:, h:h + 1]
            qh = q_ref[:, h * DN_DK:(h + 1) * DN_DK]
            kh = k_ref[:, h * DN_DK:(h + 1) * DN_DK]
            vh = v_ref[:, h * DN_DV:(h + 1) * DN_DV]
            doh = _b(do_ref[:, h * DN_DV:(h + 1) * DN_DV])
            u = u_ref[:, h * DN_DV:(h + 1) * DN_DV]
            wb = wq_ref[0, h, 0:CHUNK, :]
            tinv = t_ref[0, h]
            sb = st_ref[0, h]
            dspb = dsp_ref[0, h]
            vnb = vn_ref[:, h * DN_DV:(h + 1) * DN_DV]
            dvn = dvn_ref[:, h * DN_DV:(h + 1) * DN_DV]
            decay = jnp.exp(jnp.where(incl, gc - gr, -jnp.inf))
            qb, kb = _b(qh), _b(kh)
            kk = _nt(kb, kb)
            a = jnp.where(strict, bh * kk * decay, 0.0)
            eg = jnp.exp(gc)
            gl = gc[CHUNK - 1:CHUNK, :]
            egl = jnp.exp(gl)
            ekd = jnp.exp(gl - gc)
            p = _nt(qb, kb) * decay
            qe, ke = qh * eg, kh * ekd
            dvnb = _b(dvn)
            dpraw = _nt(doh, vnb)
            dqk = _b(dpraw * decay)
            dqe = _nt(doh, sb)
            dke = _nt(vnb, dspb)
            dqkv_ref[:, h * DN_DK:(h + 1) * DN_DK] = _nn(dqk, kb) + dqe * eg
            dk = _tn(dqk, qb) + dke * ekd
            dw = -_nt(dvnb, sb)
            dru = _mm3(tinv, dvn, _tn)
            drw = _mm3(tinv, dw, _tn)
            da = -(_nt(_b(dru), _b(u)) + _nt(_b(drw), wb))
            da = jnp.where(strict, da, 0.0)
            dqkv_ref[:, 2 * DN_QK + h * DN_DV:2 * DN_QK + (h + 1) * DN_DV] = bh * dru
            keg = kh * eg
            db_h = (jnp.sum(dru * vh, axis=-1, keepdims=True) + jnp.sum(drw * keg, axis=-1, keepdims=True)
                    + jnp.sum(da * kk * decay, axis=-1, keepdims=True))
            dkk = _b(da * bh * decay)
            dk = dk + (bh * eg) * drw + _nn(dkk, kb) + _tn(dkk, kb)
            dqkv_ref[:, DN_QK + h * DN_DK:DN_QK + (h + 1) * DN_DK] = dk
            mm = da * a + dpraw * p
            dgl = (jnp.sum(jnp.sum(dke * ke, axis=-1, keepdims=True), axis=0, keepdims=True)
                   + egl * jnp.sum(jnp.sum(dspb.astype(f32) * sb.astype(f32), axis=-1, keepdims=True),
                                   axis=0, keepdims=True))
            dg_h = (jnp.sum(mm, axis=-1, keepdims=True) + jnp.sum(drw * keg, axis=-1, keepdims=True) * bh
                    + jnp.sum(dqe * qe, axis=-1, keepdims=True) - jnp.sum(dke * ke, axis=-1, keepdims=True)
                    + last * dgl)
            dbeta = dbeta + jnp.where(lane8 == h, db_h, 0.0)
            dgam = dgam + jnp.where(lane8 == h, dg_h, 0.0)
            dgam_neg_t = dgam_neg_t + jnp.where(sub8 == h, jnp.sum(mm, axis=0, keepdims=True), 0.0)
        dgam = dgam - dgam_neg_t.T
        dg = _cumsum_rows((ri <= ci).astype(f32), dgam)
        sg = _sigmoid(bin_)
        dbin = dbeta * vm * sg * (1.0 - sg)
        dain = dg * (-ea) * vm * _sigmoid(z)
        dba_ref[...] = jnp.zeros_like(dba_ref)
        dba_ref[:, 0:DN_HEADS] = dbin
        dba_ref[:, DN_HEADS:2 * DN_HEADS] = dain
        dal = jnp.sum(dg * g, axis=0, keepdims=True)
        ddt = jnp.sum(dain, axis=0, keepdims=True)

        @pl.when(step == 0)
        def _():
            dal_ref[...] = dal
            ddt_ref[...] = ddt

        @pl.when(step > 0)
        def _():
            dal_ref[...] += dal
            ddt_ref[...] += ddt

    vec = pl.BlockSpec((1, DN_HEADS), lambda s: (0, 0))
    qs = pl.BlockSpec((CHUNK, DN_QK), lambda s: (s, 0))
    ks = pl.BlockSpec((CHUNK, DN_QK), lambda s: (s, 1))
    vs = pl.BlockSpec((CHUNK, DN_V), lambda s: (s, 1))
    v0 = pl.BlockSpec((CHUNK, DN_V), lambda s: (s, 0))
    st = pl.BlockSpec((1, DN_HEADS, DN_DK, DN_DV), lambda s: (s, 0, 0, 0))
    return pl.pallas_call(
        body, name=name, grid=(nch,),
        in_specs=[qs, ks, vs, pl.BlockSpec((CHUNK, BA_W), lambda s: (s, 0)), vec, vec, st, st,
                  pl.BlockSpec((1, DN_HEADS, CHUNK, CHUNK), lambda s: (s, 0, 0, 0)),
                  v0, pl.BlockSpec((1, DN_HEADS, 2 * CHUNK, DN_DK), lambda s: (s, 0, 0, 0)), v0, v0, v0],
        out_specs=[pl.BlockSpec((CHUNK, DN_CONV_CH), lambda s: (s, 0)),
                   pl.BlockSpec((CHUNK, BA_W), lambda s: (s, 0)), vec, vec],
        out_shape=[jax.ShapeDtypeStruct((l, DN_CONV_CH), f32), jax.ShapeDtypeStruct((l, BA_W), f32),
                   jax.ShapeDtypeStruct((1, DN_HEADS), f32), jax.ShapeDtypeStruct((1, DN_HEADS), f32)],
        compiler_params=_params(("arbitrary",), 48),
    )(qkv, qkv, qkv, ba, a_log, dt_bias, states, dsp_all, tinv_all, u_all, wq, vn_all, do, dvn_all)


def _ffn_fwd(h, nw, wgu, wd, tb, th, tag, plan):
    d = h.shape[1]
    fh = wd.shape[0]
    hn = _rms_fwd(h, nw, f"{tag}_norm")
    ab = plan.matmul(f"{tag}_gu", hn, wgu, mode="nn", tm=tb, tn=512, tk=d)
    s = _swiglu_fwd(ab, f"{tag}_act")
    out = plan.matmul(f"{tag}_down", s, wd, mode="nn", tm=th, tn=512, tk=fh, res=h)
    return out, (hn, ab, s)


def _ffn_bwd(dh, dhb, h, nw, wgu, wd, saved, tb, th, tag, plan):
    hn, ab, s = saved
    d = h.shape[1]
    fh = wd.shape[0]
    ds = _matmul(dhb, wd, mode="nt", tm=th, tn=fh // 2, tk=d, name=f"{tag}_b_ds")
    dwd = _matmul(s, dhb, mode="tn", tm=fh // 2, tn=d, tk=th, out_dtype=bf16, name=f"{tag}_b_dwd")
    plan.grads["down" + tag[-1]] = dwd
    dab = _swiglu_bwd(ab, ds, f"{tag}_b_act")
    dhn = plan.matmul(f"{tag}_b_dhn", dab, wgu, mode="nt", tm=th, tn=d, tk=fh // 2)
    dwgu = _matmul(hn, dab, mode="tn", tm=d, tn=512, tk=tb, out_dtype=bf16, name=f"{tag}_b_dwgu")
    dh2, dh2b, dnw = _rms_bwd(h, nw, dhn, dh, f"{tag}_b_norm")
    return dh2, dh2b, dnw, dwgu, dwd


class _Plan:
    GATHERS = {"ret_proj": ("ret_out", "gate0"), "ret_scan": ("up0", "down0"), "ffn0_gu": ("dn_in",),
               "ffn0_down": ("dn_out",), "dn_proj": ("gate1", "up1"), "dn_conv": ("down1",)}
    SCATTERS = {"ffn1_b_dhn": ("down1",), "dn_b_conv_a": ("gate1", "up1", "dn_out"), "ffn0_b_dhn": ("dn_in",),
                "ret_b_scan": ("gate0", "up0", "down0", "ret_out"), "ret_b_dhn": ("ret_in",)}

    def __init__(self, shards, wts):
        self.shards, self.wts, self.grads, self.parts = shards, wts, {}, {}

    def _exchange(self, stage):
        if self.shards is None:
            return None
        if stage in self.GATHERS:
            return _Exchange([self.shards[n] for n in self.GATHERS[stage]], True)
        if stage in self.SCATTERS:
            return _Exchange([self._dev_major(n) for n in self.SCATTERS[stage]], False)
        return None

    def _dev_major(self, name):
        g = self.grads
        if name[:-1] in ("gate", "up"):
            gu = g["gu" + name[-1]]
            fh = gu.shape[1] // 2
            part = gu[:, :fh] if name.startswith("gate") else gu[:, fh:]
            return _dev_major_cols(part, fh // N_DEV)
        if name[:-1] == "down":
            dwd = g[name]
            return dwd.reshape(N_DEV, dwd.shape[0] // N_DEV, dwd.shape[1])
        if name in ("ret_out", "dn_out"):
            return g[name].reshape(N_DEV, g[name].shape[0] // N_DEV, g[name].shape[1])
        return _dev_major_cols(g[name], self.shards[name].shape[-1])

    def _landed(self, stage, outs):
        if stage in self.SCATTERS:
            self.parts.update(zip(self.SCATTERS[stage], outs))
            return
        w = self.wts
        cols = lambda t: t.transpose(1, 0, 2).reshape(t.shape[1], N_DEV * t.shape[2])
        rows = lambda t: t.reshape(N_DEV * t.shape[1], t.shape[2])
        for name, t in zip(self.GATHERS[stage], outs):
            if name in ("ret_out", "dn_out") or name.startswith("down"):
                w[name] = rows(t)
            elif name == "dn_in":
                full = cols(t)
                n_main = DN_CONV_CH + DN_V
                w["dn_main"] = full[:, :n_main]
                w["dn_ba"] = jnp.pad(full[:, n_main:], ((0, 0), (0, BA_W - (full.shape[1] - n_main))))
            else:
                w[name] = cols(t)
        for layer in "01":
            if "gate" + layer in w and "up" + layer in w and "gu" + layer not in w:
                w["gu" + layer] = jnp.concatenate([w["gate" + layer], w["up" + layer]], axis=1)

    def matmul(self, stage, a, b, **kw):
        comm = self._exchange(stage)
        if comm is None:
            return _matmul(a, b, name=stage, **kw)
        out, landed = _matmul(a, b, name=stage, comm=comm, **kw)
        self._landed(stage, landed)
        return out

    def call(self, stage, fn, *args, n_out):
        comm = self._exchange(stage)
        out = fn(*args, stage, comm=comm)
        if comm is not None:
            self._landed(stage, out[n_out:])
        return out[:n_out]


def _local_step(x2, target, wts, shards=None):
    plan = _Plan(shards, wts)
    s_len, d = x2.shape
    l = s_len + CHUNK
    tb = _tile(l, 3072)
    th = tb // 2 if (tb // 2) % 16 == 0 else tb
    half = RET_DK // 2
    inv_freq = ROPE_BASE ** (-jnp.arange(half, dtype=f32) / half)
    ang = (jnp.arange(l) - PAD).astype(f32)[:, None] * inv_freq[None, :]
    cos, sin = jnp.cos(ang), jnp.sin(ang)
    lgs = jnp.log1p(-jnp.exp2(-5.0 - jnp.arange(RET_HEADS, dtype=f32)))
    gcs = jnp.exp(lgs * _ret_block(l))

    h0 = jnp.concatenate([jnp.zeros((PAD, d), f32), wts["meta"], x2], axis=0)
    mixw, ffnw = wts["mix_norm"], wts["ffn_norm"]

    hn0 = _rms_fwd(h0, mixw[0:1], "l0_norm")
    proj0 = plan.matmul("ret_proj", hn0, wts["ret_in"], mode="nn", tm=tb, tn=512, tk=d)
    qk0 = _ret_prep(proj0, cos, sin, "ret_prep")
    o0, st0 = plan.call("ret_scan", _ret_scan_fwd, qk0, proj0, lgs, gcs, n_out=2)
    y0 = _gnorm_fwd(o0, proj0, wts["ret_gn"], RET_HEADS, RET_DV, 2, "ret_gnorm")
    h1 = _matmul(y0, wts["ret_out"], mode="nn", tm=th, tn=512, tk=RET_V, res=h0, name="ret_out")
    h2, ffn0 = _ffn_fwd(h1, ffnw[0:1], wts["gu0"], wts["down0"], tb, th, "ffn0", plan)

    hn2 = _rms_fwd(h2, mixw[1:2], "l1_norm")
    proj1 = plan.matmul("dn_proj", hn2, wts["dn_main"], mode="nn", tm=tb, tn=512, tk=d)
    ba = _matmul(hn2, wts["dn_ba"], mode="nn", tm=tb, tn=BA_W, tk=d, name="dn_proj_ba")
    (qkv1,) = plan.call("dn_conv", _dn_conv_fwd, proj1, wts["conv_w"], n_out=1)
    tinv1, u1, wq1, pk1, egl1, kpt1, qwt1 = _dn_prep(qkv1, ba, wts["a_log"], wts["dt_bias"], "dn_prep")
    o1, st1, vn1 = _dn_scan_fwd(u1, wq1, pk1, egl1, "dn_scan")
    y1 = _gnorm_fwd(o1, proj1, wts["dn_norm"], DN_HEADS, DN_DV, 2, "dn_gnorm")
    h3 = _matmul(y1, wts["dn_out"], mode="nn", tm=th, tn=512, tk=DN_V, res=h2, name="dn_out")
    h4, ffn1 = _ffn_fwd(h3, ffnw[1:2], wts["gu1"], wts["down1"], tb, th, "ffn1", plan)

    dh4, dh4b, dfinal, loss = _final_loss(h4, wts["final_norm"], target, "final_loss")
    gr = plan.grads
    dh3, dh3b, dffn1, gr["gu1"], gr["down1"] = _ffn_bwd(dh4, dh4b, h3, ffnw[1:2], wts["gu1"], wts["down1"], ffn1,
                                                       tb, th, "ffn1", plan)

    dy1 = _matmul(dh3b, wts["dn_out"], mode="nt", tm=th, tn=1024, tk=d, name="dn_b_dy")
    gr["dn_out"] = _matmul(y1, dh3b, mode="tn", tm=1024, tn=d, tk=tb, out_dtype=bf16, name="dn_b_dwout")
    do1, dproj1, ddn_norm = _gnorm_bwd(o1, proj1, wts["dn_norm"], dy1, DN_HEADS, DN_DV, 2, "dn_b_gnorm")
    dvn1, dsp1 = _dn_scan_bwd(do1, kpt1, qwt1, egl1, "dn_b_scan")
    dqkv1, dba, dalog, ddt = _dn_post_bwd(qkv1, ba, wts["a_log"], wts["dt_bias"], st1, dsp1, tinv1, u1, wq1, vn1,
                                          do1, dvn1, "dn_b_post")
    dc1, dconv = plan.call("dn_b_conv_a", _dn_conv_bwd_a, proj1, wts["conv_w"], dqkv1, n_out=2)
    dproj1 = _dn_conv_bwd_b(dc1, wts["conv_w"], dproj1, "dn_b_conv_b")
    dbab = dba.astype(bf16)
    n_main = dproj1.shape[1]
    dhn2 = plan.matmul("dn_b_dhn", dproj1, wts["dn_main"], mode="nt", tm=th, tn=d, tk=n_main // 4)
    dhn2 = _matmul(dbab, wts["dn_ba"], mode="nt", tm=th, tn=d, tk=BA_W, res=dhn2, name="dn_b_dhn_ba")
    dw_main = _matmul(hn2, dproj1, mode="tn", tm=d, tn=512, tk=tb, out_dtype=bf16, name="dn_b_dwin")
    dw_ba = _matmul(hn2, dbab, mode="tn", tm=d, tn=BA_W, tk=tb, out_dtype=bf16, name="dn_b_dwin_ba")
    gr["dn_in"] = jnp.concatenate([dw_main, dw_ba], axis=1)
    dh2, dh2b, dmix1 = _rms_bwd(h2, mixw[1:2], dhn2, dh3, "l1_b_norm")

    dh1, dh1b, dffn0, gr["gu0"], gr["down0"] = _ffn_bwd(dh2, dh2b, h1, ffnw[0:1], wts["gu0"], wts["down0"], ffn0,
                                                       tb, th, "ffn0", plan)

    dy0 = _matmul(dh1b, wts["ret_out"], mode="nt", tm=th, tn=1024, tk=d, name="ret_b_dy")
    gr["ret_out"] = _matmul(y0, dh1b, mode="tn", tm=1024, tn=d, tk=tb, out_dtype=bf16, name="ret_b_dwout")
    do0, dproj0, dret_gn = _gnorm_bwd(o0, proj0, wts["ret_gn"], dy0, RET_HEADS, RET_DV, 2, "ret_b_gnorm")
    dq0, dk0, dproj0 = plan.call("ret_b_scan", _ret_scan_bwd, qk0, proj0, st0, do0, dproj0, lgs, gcs, n_out=3)
    dproj0 = _ret_prep_bwd(dq0, dk0, cos, sin, dproj0, "ret_b_prep")
    n_in = dproj0.shape[1]
    gr["ret_in"] = _matmul(hn0, dproj0, mode="tn", tm=d, tn=512, tk=tb, out_dtype=bf16, name="ret_b_dwin")
    dhn0 = plan.matmul("ret_b_dhn", dproj0, wts["ret_in"], mode="nt", tm=th, tn=d, tk=n_in // 4)
    dh0, _, dmix0 = _rms_bwd(h0, mixw[0:1], dhn0, dh1, "l0_b_norm")

    gr.update(meta=dh0[PAD:CHUNK], mix_norm=jnp.concatenate([dmix0, dmix1], axis=0),
              ffn_norm=jnp.concatenate([dffn0, dffn1], axis=0), ret_gn=dret_gn, conv_w=dconv, a_log=dalog,
              dt_bias=ddt, dn_norm=ddn_norm, final_norm=dfinal)
    return loss, dh0[CHUNK:], gr, plan


def _adamw_reduce(parts, w, m, v, name):
    _, r, c = parts.shape
    c_pad = -(-c // LANES) * LANES
    tr = _div_tile(r, max(8, (3 * MIB // 16) // c_pad // 8 * 8), 16)

    def body(p_ref, w_ref, m_ref, v_ref, g_ref, d_ref, nm_ref, nv_ref):
        g = p_ref[0].astype(f32)
        for s in range(1, N_DEV):
            g = g + p_ref[s].astype(f32)
        mm = ADAM_B1 * m_ref[...] + (1.0 - ADAM_B1) * g
        vv = ADAM_B2 * v_ref[...] + (1.0 - ADAM_B2) * (g * g)
        m_hat = mm / (1.0 - ADAM_B1 ** ADAM_STEP)
        v_hat = vv / (1.0 - ADAM_B2 ** ADAM_STEP)
        g_ref[...] = g
        d_ref[...] = -ADAM_LR * (m_hat / (jnp.sqrt(v_hat) + ADAM_EPS) + ADAM_WD * w_ref[...])
        nm_ref[...] = mm
        nv_ref[...] = vv

    blk = pl.BlockSpec((tr, c), lambda i: (i, 0))
    return pl.pallas_call(
        body, name=name, grid=(r // tr,),
        in_specs=[pl.BlockSpec((N_DEV, tr, c), lambda i: (0, i, 0)), blk, blk, blk], out_specs=[blk] * 4,
        out_shape=[jax.ShapeDtypeStruct((r, c), f32)] * 4,
        compiler_params=_params(("parallel",), 48),
    )(parts, w, m, v)


def _dev_major_cols(g, width):
    r = g.shape[0]
    return g[:, :N_DEV * width].reshape(r, N_DEV, width).transpose(1, 0, 2)


def kernel(x, meta_tokens, mix_norm_w, ffn_norm_w, ret_w_in, ret_gn_w, ret_w_out, dn_w_in, dn_conv_w, dn_a_log, dn_dt_bias, dn_norm_w, dn_w_out, ffn_w_gate, ffn_w_up, ffn_w_down, final_norm_w, loss_target, m_meta_tokens, m_mix_norm_w, m_ffn_norm_w, m_ret_w_in, m_ret_gn_w, m_ret_w_out, m_dn_w_in, m_dn_conv_w, m_dn_a_log, m_dn_dt_bias, m_dn_norm_w, m_dn_w_out, m_ffn_w_gate, m_ffn_w_up, m_ffn_w_down, m_final_norm_w, v_meta_tokens, v_mix_norm_w, v_ffn_norm_w, v_ret_w_in, v_ret_gn_w, v_ret_w_out, v_dn_w_in, v_dn_conv_w, v_dn_a_log, v_dn_dt_bias, v_dn_norm_w, v_dn_w_out, v_ffn_w_gate, v_ffn_w_up, v_ffn_w_down, v_final_norm_w):
    d = x.shape[-1]
    me = 4 * lax.axis_index("x") + 2 * lax.axis_index("y") + lax.axis_index("c")

    shards = dict(ret_in=ret_w_in[0].astype(bf16), ret_out=ret_w_out[0].astype(bf16),
                  dn_in=dn_w_in[0].astype(bf16), dn_out=dn_w_out[0].astype(bf16))
    for layer in (0, 1):
        shards[f"gate{layer}"] = ffn_w_gate[layer].astype(bf16)
        shards[f"up{layer}"] = ffn_w_up[layer].astype(bf16)
        shards[f"down{layer}"] = ffn_w_down[layer].astype(bf16)
    g_ret_in, g_meta, g_conv, g_dnn = _exchange([shards["ret_in"], meta_tokens, dn_conv_w[0], dn_norm_w], True,
                                                "gather_first")
    cols = lambda g: g.transpose(1, 0, 2).reshape(g.shape[1], N_DEV * g.shape[2])
    wts = dict(meta=cols(g_meta), mix_norm=mix_norm_w, ffn_norm=ffn_norm_w, ret_in=cols(g_ret_in), ret_gn=ret_gn_w,
               conv_w=cols(g_conv), a_log=dn_a_log, dt_bias=dn_dt_bias, dn_norm=cols(g_dnn),
               final_norm=final_norm_w.reshape(1, d))

    loss_part, grad_x, gr, plan = _local_step(x[0], loss_target[0], wts, shards)
    loss = lax.psum(loss_part[0, 0], AXES)

    pp = plan.parts
    both = lambda name: jnp.concatenate([pp[name + "0"], pp[name + "1"]], axis=1)
    big_parts = [pp["ret_in"], pp["ret_out"], pp["dn_in"], pp["dn_out"], both("gate"), both("up"), both("down")]
    big_names = ["ret_w_in", "ret_w_out", "dn_w_in", "dn_w_out", "ffn_w_gate", "ffn_w_up", "ffn_w_down"]
    big_w = [ret_w_in, ret_w_out, dn_w_in, dn_w_out, ffn_w_gate, ffn_w_up, ffn_w_down]
    big_m = [m_ret_w_in, m_ret_w_out, m_dn_w_in, m_dn_w_out, m_ffn_w_gate, m_ffn_w_up, m_ffn_w_down]
    big_v = [v_ret_w_in, v_ret_w_out, v_dn_w_in, v_dn_w_out, v_ffn_w_gate, v_ffn_w_up, v_ffn_w_down]
    res = {}
    for nm, parts, w_, m_, v_ in zip(big_names, big_parts, big_w, big_m, big_v):
        r2, c2 = parts.shape[1], parts.shape[2]
        outs = _adamw_reduce(parts, w_.reshape(r2, c2), m_.reshape(r2, c2), v_.reshape(r2, c2), f"adamw_{nm}")
        res[nm] = [o.reshape(w_.shape) for o in outs]

    small_names = ["meta_tokens", "mix_norm_w", "ffn_norm_w", "ret_gn_w", "dn_conv_w", "dn_a_log", "dn_dt_bias",
                   "dn_norm_w", "final_norm_w"]
    small_g = [gr["meta"], gr["mix_norm"], gr["ffn_norm"], gr["ret_gn"], gr["conv_w"], gr["a_log"], gr["dt_bias"],
               gr["dn_norm"], gr["final_norm"]]
    small_w = [meta_tokens, mix_norm_w, ffn_norm_w, ret_gn_w, dn_conv_w, dn_a_log, dn_dt_bias, dn_norm_w, final_norm_w]
    small_m = [m_meta_tokens, m_mix_norm_w, m_ffn_norm_w, m_ret_gn_w, m_dn_conv_w, m_dn_a_log, m_dn_dt_bias,
               m_dn_norm_w, m_final_norm_w]
    small_v = [v_meta_tokens, v_mix_norm_w, v_ffn_norm_w, v_ret_gn_w, v_dn_conv_w, v_dn_a_log, v_dn_dt_bias,
               v_dn_norm_w, v_final_norm_w]
    sharded = {"meta_tokens", "dn_conv_w", "dn_norm_w"}
    flat = jnp.concatenate([g.reshape(-1) for g in small_g])
    row = 8 * LANES
    n_flat = flat.shape[0]
    flat = jnp.pad(flat, (0, -n_flat % row)).reshape(-1, row)
    (gathered,) = _exchange([flat], True, "gather_small_grads")
    gathered = gathered.reshape(N_DEV, -1)
    pieces, off = [], 0
    for nm, g, w_ in zip(small_names, small_g, small_w):
        full = gathered[:, off:off + g.size].reshape((N_DEV,) + g.shape)
        off += g.size
        if nm in sharded:
            wloc = w_.shape[-1]
            full = lax.dynamic_slice_in_dim(full, me * wloc, wloc, axis=full.ndim - 1)
        pieces.append(full.reshape(N_DEV, -1))
    sizes = [p.shape[1] for p in pieces]
    n_loc = sum(sizes)
    pad_loc = -n_loc % row

    def pack(vs, lead):
        cat = jnp.concatenate([a.reshape(lead + (-1,)) for a in vs], axis=-1)
        cat = jnp.pad(cat, [(0, 0)] * len(lead) + [(0, pad_loc)])
        return cat.reshape(lead + (-1, row))

    outs = _adamw_reduce(pack(pieces, (N_DEV,)), pack(small_w, ()), pack(small_m, ()), pack(small_v, ()), "adamw_small")
    off = 0
    for nm, sz, w_ in zip(small_names, sizes, small_w):
        res[nm] = [o.reshape(-1)[off:off + sz].reshape(w_.shape) for o in outs]
        off += sz

    order = ["meta_tokens", "mix_norm_w", "ffn_norm_w", "ret_w_in", "ret_gn_w", "ret_w_out", "dn_w_in", "dn_conv_w",
             "dn_a_log", "dn_dt_bias", "dn_norm_w", "dn_w_out", "ffn_w_gate", "ffn_w_up", "ffn_w_down", "final_norm_w"]
    grad_x = grad_x.reshape(x.shape)
    return (loss, grad_x, *[res[nm][0] for nm in order], *[res[nm][1] for nm in order],
            *[res[nm][2] for nm in order], *[res[nm][3] for nm in order])
```

```python
import functools
import math

import jax
import jax.numpy as jnp
from jax import lax
from jax.experimental import pallas as pl
from jax.experimental.pallas import tpu as pltpu

f32 = jnp.float32
bf16 = jnp.bfloat16
HI = lax.Precision.HIGHEST

N_META = 16
CHUNK = 64
PAD = CHUNK - N_META
RMS_EPS = 1e-6
RET_HEADS, RET_DK, RET_DV = 4, 256, 512
RET_QK, RET_V = RET_HEADS * RET_DK, RET_HEADS * RET_DV
DN_HEADS, DN_DK, DN_DV = 8, 128, 256
DN_QK, DN_V = DN_HEADS * DN_DK, DN_HEADS * DN_DV
DN_CONV_CH = 2 * DN_QK + DN_V
CONV_K = 4
ROPE_BASE = 10000.0
ADAM_LR, ADAM_B1, ADAM_B2, ADAM_EPS, ADAM_WD, ADAM_STEP = 0.001, 0.9, 0.999, 1e-08, 0.01, 10
N_DEV = 8
AXES = ("x", "y", "c")
LANES = 128
MIB = 1024 * 1024


def _tile(n_rows, cap):
    nch = n_rows // CHUNK
    best = 1
    for d in range(1, nch + 1):
        if nch % d == 0 and d * CHUNK <= cap:
            best = d
    return best * CHUNK


def _div_tile(n, cap, align):
    best = None
    for d in range(align, min(n, cap) + 1, align):
        if n % d == 0:
            best = d
    return best if best is not None else n


def _params(sem, vmem_mb):
    return pltpu.CompilerParams(dimension_semantics=sem, vmem_limit_bytes=int(vmem_mb * MIB))


def _nn(a, b, precision=None):
    return jnp.dot(a, b, preferred_element_type=f32, precision=precision)


def _nt(a, b, precision=None):
    return lax.dot_general(a, b, (((1,), (1,)), ((), ())), preferred_element_type=f32, precision=precision)


def _tn(a, b, precision=None):
    return lax.dot_general(a, b, (((0,), (0,)), ((), ())), preferred_element_type=f32, precision=precision)


def _b(x):
    return x.astype(bf16)


def _sigmoid(x):
    return 1.0 / (1.0 + jnp.exp(-x))


def _silu(x):
    return x * _sigmoid(x)


def _dsilu(x):
    s = _sigmoid(x)
    return s * (1.0 + x * (1.0 - s))


def _peer(k):
    x, y, c = lax.axis_index("x"), lax.axis_index("y"), lax.axis_index("c")
    px = 1 - x if k & 4 else x
    py = 1 - y if k & 2 else y
    pc = 1 - c if k & 1 else c
    return (px, py, pc), 4 * px + 2 * py + pc


class _Exchange:
    def __init__(self, arrs, gather):
        self.arrs, self.gather, self.n = list(arrs), gather, len(arrs)
        self.out_shapes = [jax.ShapeDtypeStruct(((N_DEV,) + a.shape) if gather else a.shape, a.dtype) for a in arrs]
        self.specs = [pl.BlockSpec(memory_space=pltpu.HBM)] * self.n
        self.scratch = [pltpu.SemaphoreType.DMA((self.n, N_DEV - 1)), pltpu.SemaphoreType.DMA((self.n, N_DEV - 1)),
                        pltpu.SemaphoreType.DMA((self.n,))]

    def _copies(self, ins, outs, sems):
        send_sems, recv_sems, local_sems = sems
        me = 4 * lax.axis_index("x") + 2 * lax.axis_index("y") + lax.axis_index("c")
        src = (lambda a, dest: ins[a]) if self.gather else (lambda a, dest: ins[a].at[dest])
        local = [pltpu.make_async_copy(src(a, me), outs[a].at[me], local_sems.at[a]) for a in range(self.n)]
        sends, lands = [], []
        for k in range(1, N_DEV):
            peer, pidx = _peer(k)
            for a in range(self.n):
                for dst, lst in ((outs[a].at[me], sends), (outs[a].at[pidx], lands)):
                    lst.append(pltpu.make_async_remote_copy(
                        src_ref=src(a, pidx), dst_ref=dst, send_sem=send_sems.at[a, k - 1],
                        recv_sem=recv_sems.at[a, k - 1], device_id=peer, device_id_type=pl.DeviceIdType.MESH))
        return local, sends, lands

    def start(self, ins, outs, sems):
        local, sends, _ = self._copies(ins, outs, sems)
        for cp in local + sends:
            cp.start()

    def wait(self, ins, outs, sems):
        local, sends, lands = self._copies(ins, outs, sems)
        for cp in lands:
            cp.wait_recv()
        for cp in sends:
            cp.wait_send()
        for cp in local:
            cp.wait()


def _call(body, args, *, name, grid, in_specs, out_specs, out_shape, scratch=(), sem, vmem_mb, comm=None,
          aliases=None):
    aliases = aliases or {}
    if comm is None:
        out = pl.pallas_call(body, name=name, grid=grid, in_specs=list(in_specs), out_specs=list(out_specs),
                             out_shape=list(out_shape), scratch_shapes=list(scratch), input_output_aliases=aliases,
                             compiler_params=_params(sem, vmem_mb))(*args)
        return list(out)
    n_in, n_out, n_scr, nc = len(args), len(out_shape), len(scratch), comm.n

    def carried(*refs):
        ins, cin = refs[:n_in], refs[n_in:n_in + nc]
        o0 = n_in + nc
        outs, cout = refs[o0:o0 + n_out], refs[o0 + n_out:o0 + n_out + nc]
        s0 = o0 + n_out + nc
        scr, sems = refs[s0:s0 + n_scr], refs[s0 + n_scr:]
        first = functools.reduce(jnp.logical_and, [pl.program_id(i) == 0 for i in range(len(grid))])
        last = functools.reduce(jnp.logical_and, [pl.program_id(i) == grid[i] - 1 for i in range(len(grid))])

        @pl.when(first)
        def _():
            comm.start(cin, cout, sems)

        body(*ins, *outs, *scr)

        @pl.when(last)
        def _():
            comm.wait(cin, cout, sems)

    out = pl.pallas_call(
        carried, name=name, grid=grid, in_specs=list(in_specs) + comm.specs, out_specs=list(out_specs) + comm.specs,
        out_shape=list(out_shape) + comm.out_shapes, scratch_shapes=list(scratch) + comm.scratch,
        input_output_aliases=aliases,
        compiler_params=_params(("arbitrary",) * len(grid), vmem_mb))(*args, *comm.arrs)
    return list(out)


def _exchange(arrs, gather, name):
    comm = _Exchange(arrs, gather)

    def body(*refs):
        ins, outs, sems = refs[:comm.n], refs[comm.n:2 * comm.n], refs[2 * comm.n:]
        comm.start(ins, outs, sems)
        comm.wait(ins, outs, sems)

    return pl.pallas_call(body, name=name, in_specs=comm.specs, out_specs=comm.specs, out_shape=comm.out_shapes,
                          scratch_shapes=comm.scratch)(*comm.arrs)


def _matmul(a, b, *, mode, tm, tn, tk, name, out_dtype=f32, res=None, vmem_mb=48, comm=None):
    if mode == "nn":
        (m, k), (k2, n) = a.shape, b.shape
    elif mode == "nt":
        (m, k), (n, k2) = a.shape, b.shape
    else:
        (k, m), (k2, n) = a.shape, b.shape
    assert k == k2 and m % tm == 0 and n % tn == 0 and k % tk == 0, (name, a.shape, b.shape, tm, tn, tk)
    nk = k // tk
    dot = {"nn": _nn, "nt": _nt, "tn": _tn}[mode]
    a_spec = {"nn": pl.BlockSpec((tm, tk), lambda i, j, kk: (i, kk)),
              "nt": pl.BlockSpec((tm, tk), lambda i, j, kk: (i, kk)),
              "tn": pl.BlockSpec((tk, tm), lambda i, j, kk: (kk, i))}[mode]
    b_spec = {"nn": pl.BlockSpec((tk, tn), lambda i, j, kk: (kk, j)),
              "nt": pl.BlockSpec((tn, tk), lambda i, j, kk: (j, kk)),
              "tn": pl.BlockSpec((tk, tn), lambda i, j, kk: (kk, j))}[mode]
    o_spec = pl.BlockSpec((tm, tn), lambda i, j, kk: (i, j))
    has_res = res is not None

    def body(*refs):
        if has_res:
            a_ref, b_ref, r_ref, o_ref = refs[:4]
            rest = refs[4:]
        else:
            a_ref, b_ref, o_ref = refs[:3]
            r_ref = None
            rest = refs[3:]
        part = dot(_b(a_ref[...]), _b(b_ref[...]))
        if nk == 1:
            if has_res:
                part = part + r_ref[...]
            o_ref[...] = part.astype(out_dtype)
        else:
            acc_ref = rest[0]
            kk = pl.program_id(2)

            @pl.when(kk == 0)
            def _():
                acc_ref[...] = part

            @pl.when(kk > 0)
            def _():
                acc_ref[...] += part

            @pl.when(kk == nk - 1)
            def _():
                tot = acc_ref[...]
                if has_res:
                    tot = tot + r_ref[...]
                o_ref[...] = tot.astype(out_dtype)

    in_specs = [a_spec, b_spec] + ([o_spec] if has_res else [])
    args = (a, b) + ((res,) if has_res else ())
    out = _call(body, args, name=name, grid=(m // tm, n // tn, nk), in_specs=in_specs, out_specs=[o_spec],
                out_shape=[jax.ShapeDtypeStruct((m, n), out_dtype)],
                scratch=[pltpu.VMEM((tm, tn), f32)] if nk > 1 else [],
                sem=("parallel", "parallel", "arbitrary"), vmem_mb=vmem_mb, comm=comm)
    return out[0] if comm is None else (out[0], out[1:])


def _rms_fwd(h, w, name):
    l, d = h.shape
    tr = _tile(l, 512)

    def body(h_ref, w_ref, o_ref):
        x = h_ref[...]
        r = lax.rsqrt(jnp.mean(x * x, axis=-1, keepdims=True) + RMS_EPS)
        o_ref[...] = _b(x * r * w_ref[...])

    return pl.pallas_call(
        body, name=name, grid=(l // tr,),
        in_specs=[pl.BlockSpec((tr, d), lambda i: (i, 0)), pl.BlockSpec((1, d), lambda i: (0, 0))],
        out_specs=pl.BlockSpec((tr, d), lambda i: (i, 0)),
        out_shape=jax.ShapeDtypeStruct((l, d), bf16),
        compiler_params=_params(("parallel",), 32),
    )(h, w)


def _rms_bwd(h, w, dhn, dres, name):
    l, d = h.shape
    tr = _tile(l, 512)

    def body(h_ref, w_ref, g_ref, r_ref, dh_ref, dhb_ref, dw_ref):
        x = h_ref[...]
        r = lax.rsqrt(jnp.mean(x * x, axis=-1, keepdims=True) + RMS_EPS)
        xh = x * r
        g = g_ref[...]
        dxh = g * w_ref[...]
        dx = r * (dxh - xh * jnp.mean(dxh * xh, axis=-1, keepdims=True))
        dh = r_ref[...] + dx
        dh_ref[...] = dh
        dhb_ref[...] = _b(dh)
        dw = jnp.sum(g * xh, axis=0, keepdims=True)

        @pl.when(pl.program_id(0) == 0)
        def _():
            dw_ref[...] = dw

        @pl.when(pl.program_id(0) > 0)
        def _():
            dw_ref[...] += dw

    row = pl.BlockSpec((tr, d), lambda i: (i, 0))
    vec = pl.BlockSpec((1, d), lambda i: (0, 0))
    return pl.pallas_call(
        body, name=name, grid=(l // tr,), in_specs=[row, vec, row, row], out_specs=[row, row, vec],
        out_shape=[jax.ShapeDtypeStruct((l, d), f32), jax.ShapeDtypeStruct((l, d), bf16),
                   jax.ShapeDtypeStruct((1, d), f32)],
        compiler_params=_params(("arbitrary",), 40),
    )(h, w, dhn, dres)


def _final_loss(h, w, target, name):
    l, d = h.shape
    nch = l // CHUNK

    def body(h_ref, w_ref, t_ref, dh_ref, dhb_ref, dw_ref, loss_ref):
        n = pl.program_id(0)
        live = (n > 0).astype(f32)
        x = h_ref[...]
        r = lax.rsqrt(jnp.mean(x * x, axis=-1, keepdims=True) + RMS_EPS)
        xh = x * r
        wv = w_ref[...]
        err = (xh * wv - t_ref[...]) * live
        dy = err * (1.0 / d)
        dxh = dy * wv
        dx = r * (dxh - xh * jnp.mean(dxh * xh, axis=-1, keepdims=True))
        dh_ref[...] = dx
        dhb_ref[...] = _b(dx)
        dw = jnp.sum(dy * xh, axis=0, keepdims=True)
        part = 0.5 * jnp.sum(jnp.sum(err * err, axis=-1, keepdims=True) * (1.0 / d), axis=0, keepdims=True)
        part = jnp.broadcast_to(part, (1, LANES))

        @pl.when(n == 0)
        def _():
            dw_ref[...] = dw
            loss_ref[...] = part

        @pl.when(n > 0)
        def _():
            dw_ref[...] += dw
            loss_ref[...] += part

    row = pl.BlockSpec((CHUNK, d), lambda i: (i, 0))
    vec = pl.BlockSpec((1, d), lambda i: (0, 0))
    return pl.pallas_call(
        body, name=name, grid=(nch,),
        in_specs=[row, vec, pl.BlockSpec((CHUNK, d), lambda i: (jnp.maximum(i - 1, 0), 0))],
        out_specs=[row, row, vec, pl.BlockSpec((1, LANES), lambda i: (0, 0))],
        out_shape=[jax.ShapeDtypeStruct((l, d), f32), jax.ShapeDtypeStruct((l, d), bf16),
                   jax.ShapeDtypeStruct((1, d), f32), jax.ShapeDtypeStruct((1, LANES), f32)],
        compiler_params=_params(("arbitrary",), 32),
    )(h, w, target)


def _swiglu_fwd(ab, name):
    l, two_f = ab.shape
    fh = two_f // 2
    tr = _tile(l, 256)

    def body(a_ref, b_ref, o_ref):
        o_ref[...] = _b(_silu(a_ref[...]) * b_ref[...])

    return pl.pallas_call(
        body, name=name, grid=(l // tr,),
        in_specs=[pl.BlockSpec((tr, fh), lambda i: (i, 0)), pl.BlockSpec((tr, fh), lambda i: (i, 1))],
        out_specs=pl.BlockSpec((tr, fh), lambda i: (i, 0)),
        out_shape=jax.ShapeDtypeStruct((l, fh), bf16),
        compiler_params=_params(("parallel",), 32),
    )(ab, ab)


def _swiglu_bwd(ab, ds, name):
    l, two_f = ab.shape
    fh = two_f // 2
    tr = _tile(l, 256)

    def body(a_ref, b_ref, ds_ref, o_ref):
        a, bb, g = a_ref[...], b_ref[...], ds_ref[...]
        o_ref[:, :fh] = _b(g * bb * _dsilu(a))
        o_ref[:, fh:] = _b(g * _silu(a))

    lo = pl.BlockSpec((tr, fh), lambda i: (i, 0))
    hi = pl.BlockSpec((tr, fh), lambda i: (i, 1))
    return pl.pallas_call(
        body, name=name, grid=(l // tr,), in_specs=[lo, hi, lo],
        out_specs=pl.BlockSpec((tr, two_f), lambda i: (i, 0)),
        out_shape=jax.ShapeDtypeStruct((l, two_f), bf16),
        compiler_params=_params(("parallel",), 40),
    )(ab, ab, ds)


def _gnorm_fwd(o, proj, nw, heads, dv, gate_blk, name):
    l, hv = o.shape
    tr = _tile(l, 256)

    def body(o_ref, g_ref, w_ref, y_ref):
        wv = w_ref[...]
        for h in range(heads):
            sl = slice(h * dv, (h + 1) * dv)
            oh = o_ref[:, sl]
            r = lax.rsqrt(jnp.mean(oh * oh, axis=-1, keepdims=True) + RMS_EPS)
            y_ref[:, sl] = _b(oh * r * wv * _silu(g_ref[:, sl]))

    return pl.pallas_call(
        body, name=name, grid=(l // tr,),
        in_specs=[pl.BlockSpec((tr, hv), lambda i: (i, 0)), pl.BlockSpec((tr, hv), lambda i: (i, gate_blk)),
                  pl.BlockSpec((1, dv), lambda i: (0, 0))],
        out_specs=pl.BlockSpec((tr, hv), lambda i: (i, 0)),
        out_shape=jax.ShapeDtypeStruct((l, hv), bf16),
        compiler_params=_params(("parallel",), 32),
    )(o, proj, nw)


def _gnorm_bwd(o, proj, nw, dy, heads, dv, gate_blk, name):
    l, hv = o.shape
    tr = _tile(l, 256)

    def body(o_ref, g_ref, w_ref, dy_ref, do_ref, dg_ref, dw_ref):
        wv = w_ref[...]
        dw = jnp.zeros((1, dv), f32)
        for h in range(heads):
            sl = slice(h * dv, (h + 1) * dv)
            oh = o_ref[:, sl]
            g = g_ref[:, sl]
            dyh = dy_ref[:, sl]
            r = lax.rsqrt(jnp.mean(oh * oh, axis=-1, keepdims=True) + RMS_EPS)
            xh = oh * r
            dn = dyh * _silu(g)
            dg_ref[:, sl] = _b(dyh * (xh * wv) * _dsilu(g))
            dxh = dn * wv
            do_ref[:, sl] = r * (dxh - xh * jnp.mean(dxh * xh, axis=-1, keepdims=True))
            dw = dw + jnp.sum(dn * xh, axis=0, keepdims=True)

        @pl.when(pl.program_id(0) == 0)
        def _():
            dw_ref[...] = dw

        @pl.when(pl.program_id(0) > 0)
        def _():
            dw_ref[...] += dw

    row = pl.BlockSpec((tr, hv), lambda i: (i, 0))
    gate = pl.BlockSpec((tr, hv), lambda i: (i, gate_blk))
    vec = pl.BlockSpec((1, dv), lambda i: (0, 0))
    return pl.pallas_call(
        body, name=name, grid=(l // tr,),
        in_specs=[row, gate, vec, row],
        out_specs=[row, gate, vec],
        out_shape=[jax.ShapeDtypeStruct((l, hv), f32), jax.ShapeDtypeStruct(proj.shape, bf16),
                   jax.ShapeDtypeStruct((1, dv), f32)],
        compiler_params=_params(("arbitrary",), 40),
    )(o, proj, nw, dy)


def _ret_prep(proj, cos, sin, name):
    l = proj.shape[0]
    tr = _tile(l, 256)
    half = RET_DK // 2
    scale = RET_DK ** -0.5

    def body(p_ref, c_ref, s_ref, o_ref):
        rows = pl.program_id(0) * tr + lax.broadcasted_iota(jnp.int32, (tr, 1), 0)
        kmul = jnp.where(rows >= PAD, scale, 0.0).astype(f32)
        c, s = c_ref[...], s_ref[...]
        for j in range(2 * RET_HEADS):
            t1 = p_ref[:, j * RET_DK: j * RET_DK + half]
            t2 = p_ref[:, j * RET_DK + half: (j + 1) * RET_DK]
            o1 = t1 * c - t2 * s
            o2 = t1 * s + t2 * c
            if j >= RET_HEADS:
                o1, o2 = o1 * kmul, o2 * kmul
            o_ref[:, j * RET_DK: j * RET_DK + half] = o1
            o_ref[:, j * RET_DK + half: (j + 1) * RET_DK] = o2

    wide = pl.BlockSpec((tr, 2 * RET_QK), lambda i: (i, 0))
    tab = pl.BlockSpec((tr, half), lambda i: (i, 0))
    return pl.pallas_call(
        body, name=name, grid=(l // tr,), in_specs=[wide, tab, tab], out_specs=wide,
        out_shape=jax.ShapeDtypeStruct((l, 2 * RET_QK), f32),
        compiler_params=_params(("parallel",), 32),
    )(proj, cos, sin)


def _ret_prep_bwd(dq, dk, cos, sin, dproj, name):
    l = dq.shape[0]
    tr = _tile(l, 256)
    half = RET_DK // 2
    scale = RET_DK ** -0.5

    def body(dq_ref, dk_ref, c_ref, s_ref, _, o_ref):
        rows = pl.program_id(0) * tr + lax.broadcasted_iota(jnp.int32, (tr, 1), 0)
        kmul = jnp.where(rows >= PAD, scale, 0.0).astype(f32)
        c, s = c_ref[...], s_ref[...]
        for j in range(2 * RET_HEADS):
            d_ref = dq_ref if j < RET_HEADS else dk_ref
            jj = j % RET_HEADS
            d1 = d_ref[:, jj * RET_DK: jj * RET_DK + half]
            d2 = d_ref[:, jj * RET_DK + half: (jj + 1) * RET_DK]
            if j >= RET_HEADS:
                d1, d2 = d1 * kmul, d2 * kmul
            o_ref[:, j * RET_DK: j * RET_DK + half] = _b(d1 * c + d2 * s)
            o_ref[:, j * RET_DK + half: (j + 1) * RET_DK] = _b(d2 * c - d1 * s)

    nar = pl.BlockSpec((tr, RET_QK), lambda i: (i, 0))
    wide = pl.BlockSpec((tr, 2 * RET_QK), lambda i: (i, 0))
    tab = pl.BlockSpec((tr, half), lambda i: (i, 0))
    return pl.pallas_call(
        body, name=name, grid=(l // tr,), in_specs=[nar, nar, tab, tab, pl.BlockSpec(memory_space=pl.ANY)],
        out_specs=wide, out_shape=jax.ShapeDtypeStruct(dproj.shape, dproj.dtype), input_output_aliases={4: 0},
        compiler_params=_params(("parallel",), 32),
    )(dq, dk, cos, sin, dproj)


RET_BLOCK_CHUNKS = 3


def _ret_block(l):
    nch = l // CHUNK
    return RET_BLOCK_CHUNKS * CHUNK if nch % RET_BLOCK_CHUNKS == 0 else CHUNK


def _ret_decay(lg, rb):
    idx = lax.broadcasted_iota(jnp.int32, (rb, 1), 0).astype(f32)
    ri = lax.broadcasted_iota(jnp.int32, (rb, rb), 0)
    ci = lax.broadcasted_iota(jnp.int32, (rb, rb), 1)
    rel = (ri - ci).astype(f32)
    dmask = jnp.where(ri >= ci, jnp.exp(lg * jnp.maximum(rel, 0.0)), 0.0)
    xi = jnp.exp(lg * (idx + 1.0))
    zeta = jnp.exp(lg * (rb - 1.0 - idx))
    return dmask, xi, zeta


def _ret_scan_fwd(qk, proj, lgs, gcs, name, comm=None):
    l = qk.shape[0]
    rb = _ret_block(l)
    nb = l // rb

    def body(lg_ref, gc_ref, q_ref, k_ref, v_ref, o_ref, st_ref, s_ref):
        @pl.when(pl.program_id(0) == 0)
        def _():
            s_ref[...] = jnp.zeros_like(s_ref)

        for h in range(RET_HEADS):
            dmask, xi, zeta = _ret_decay(lg_ref[h], rb)
            q = q_ref[:, h * RET_DK:(h + 1) * RET_DK]
            k = k_ref[:, h * RET_DK:(h + 1) * RET_DK]
            vb = _b(v_ref[:, h * RET_DV:(h + 1) * RET_DV])
            s = s_ref[h]
            sb = _b(s)
            st_ref[0, h] = sb
            scores = _nt(_b(q), _b(k)) * dmask
            o_ref[:, h * RET_DV:(h + 1) * RET_DV] = _nn(_b(scores), vb) + _nn(_b(q * xi), sb)
            s_ref[h] = gc_ref[h] * s + _tn(_b(k * zeta), vb)

    smem = pl.BlockSpec(memory_space=pltpu.SMEM)
    return _call(
        body, (lgs, gcs, qk, qk, proj), name=name, grid=(nb,),
        in_specs=[smem, smem,
                  pl.BlockSpec((rb, RET_QK), lambda n: (n, 0)),
                  pl.BlockSpec((rb, RET_QK), lambda n: (n, 1)),
                  pl.BlockSpec((rb, RET_V), lambda n: (n, 1))],
        out_specs=[pl.BlockSpec((rb, RET_V), lambda n: (n, 0)),
                   pl.BlockSpec((1, RET_HEADS, RET_DK, RET_DV), lambda n: (n, 0, 0, 0))],
        out_shape=[jax.ShapeDtypeStruct((l, RET_V), f32),
                   jax.ShapeDtypeStruct((nb, RET_HEADS, RET_DK, RET_DV), bf16)],
        scratch=[pltpu.VMEM((RET_HEADS, RET_DK, RET_DV), f32)], sem=("arbitrary",), vmem_mb=40, comm=comm)


def _ret_scan_bwd(qk, proj, states, do, dproj, lgs, gcs, name, comm=None):
    l = qk.shape[0]
    rb = _ret_block(l)
    nb = l // rb

    def body(lg_ref, gc_ref, q_ref, k_ref, v_ref, st_ref, do_ref, _, dq_ref, dk_ref, dv_ref, ds_ref):
        @pl.when(pl.program_id(0) == 0)
        def _():
            ds_ref[...] = jnp.zeros_like(ds_ref)

        for h in range(RET_HEADS):
            dmask, xi, zeta = _ret_decay(lg_ref[h], rb)
            q = q_ref[:, h * RET_DK:(h + 1) * RET_DK]
            k = k_ref[:, h * RET_DK:(h + 1) * RET_DK]
            qb, kb = _b(q), _b(k)
            vb = _b(v_ref[:, h * RET_DV:(h + 1) * RET_DV])
            dob = _b(do_ref[:, h * RET_DV:(h + 1) * RET_DV])
            sb = st_ref[0, h]
            dsp = ds_ref[h]
            dspb = _b(dsp)
            scores = _nt(qb, kb) * dmask
            dscores = _b(_nt(dob, vb) * dmask)
            dq_ref[:, h * RET_DK:(h + 1) * RET_DK] = _nn(dscores, kb) + _nt(dob, sb) * xi
            dk_ref[:, h * RET_DK:(h + 1) * RET_DK] = _tn(dscores, qb) + _nt(vb, dspb) * zeta
            dv_ref[:, h * RET_DV:(h + 1) * RET_DV] = _b(_tn(_b(scores), dob) + _nn(_b(k * zeta), dspb))
            ds_ref[h] = gc_ref[h] * dsp + _tn(_b(q * xi), dob)

    smem = pl.BlockSpec(memory_space=pltpu.SMEM)
    rev = lambda n: nb - 1 - n
    return _call(
        body, (lgs, gcs, qk, qk, proj, states, do, dproj), name=name, grid=(nb,),
        in_specs=[smem, smem,
                  pl.BlockSpec((rb, RET_QK), lambda n: (rev(n), 0)),
                  pl.BlockSpec((rb, RET_QK), lambda n: (rev(n), 1)),
                  pl.BlockSpec((rb, RET_V), lambda n: (rev(n), 1)),
                  pl.BlockSpec((1, RET_HEADS, RET_DK, RET_DV), lambda n: (rev(n), 0, 0, 0)),
                  pl.BlockSpec((rb, RET_V), lambda n: (rev(n), 0)),
                  pl.BlockSpec(memory_space=pl.ANY)],
        out_specs=[pl.BlockSpec((rb, RET_QK), lambda n: (rev(n), 0)),
                   pl.BlockSpec((rb, RET_QK), lambda n: (rev(n), 0)),
                   pl.BlockSpec((rb, RET_V), lambda n: (rev(n), 1))],
        out_shape=[jax.ShapeDtypeStruct((l, RET_QK), f32), jax.ShapeDtypeStruct((l, RET_QK), f32),
                   jax.ShapeDtypeStruct(dproj.shape, dproj.dtype)],
        scratch=[pltpu.VMEM((RET_HEADS, RET_DK, RET_DV), f32)], sem=("arbitrary",), vmem_mb=40, comm=comm,
        aliases={7: 2})


CONV_BLK = 1024
HALO = 8


def _slab_rows(r):
    return pl.ds(pl.multiple_of(r * HALO, HALO), HALO)


def _conv_slab(x_ref, p_ref, r, i, tr):
    cur = x_ref[_slab_rows(r), :]
    prev = jnp.where(r > 0, x_ref[_slab_rows(jnp.maximum(r - 1, 0)), :], p_ref[...])
    row0 = i * tr + r * HALO
    cur = jnp.where(row0 >= PAD, cur, 0.0)
    prev = jnp.where(row0 - HALO >= PAD, prev, 0.0)
    lrow = lax.broadcasted_iota(jnp.int32, (HALO, 1), 0)
    shifted = [jnp.where(lrow < s, pltpu.roll(prev, s, 0), pltpu.roll(cur, s, 0)) for s in range(1, CONV_K)]
    return [cur] + shifted


def _conv_of(xs, w):
    acc = xs[0] * w[CONV_K - 1:CONV_K, :]
    for s in range(1, CONV_K):
        acc = acc + xs[s] * w[CONV_K - 1 - s:CONV_K - s, :]
    return acc


def _slab_loop(n_slabs, fn, init=None):
    return lax.fori_loop(0, n_slabs, fn, init, unroll=2)


def _dn_conv_fwd(proj, conv_w, name, comm=None):
    l = proj.shape[0]
    tr = _tile(l, 256)
    nblk = DN_CONV_CH // CONV_BLK
    heads = CONV_BLK // DN_DK

    def body(x_ref, p_ref, w_ref, o_ref):
        i, j = pl.program_id(0), pl.program_id(1)
        w = w_ref[...]

        def act(r):
            return _silu(_conv_of(_conv_slab(x_ref, p_ref, r, i, tr), w))

        def normed(scale):
            def slab(r, carry):
                a = act(r)
                outs = []
                for h in range(heads):
                    ah = a[:, h * DN_DK:(h + 1) * DN_DK]
                    outs.append(ah * (lax.rsqrt(jnp.sum(ah * ah, axis=-1, keepdims=True) + RMS_EPS) * scale))
                o_ref[_slab_rows(r), :] = jnp.concatenate(outs, axis=1)
                return carry
            return slab

        def plain(r, carry):
            o_ref[_slab_rows(r), :] = act(r)
            return carry

        @pl.when(j == 0)
        def _():
            _slab_loop(tr // HALO, normed(DN_DK ** -0.5))

        @pl.when(j == 1)
        def _():
            _slab_loop(tr // HALO, normed(1.0))

        @pl.when(j >= 2)
        def _():
            _slab_loop(tr // HALO, plain)

    hb = tr // HALO
    return _call(
        body, (proj, proj, conv_w), name=name, grid=(l // tr, nblk),
        in_specs=[pl.BlockSpec((tr, CONV_BLK), lambda i, j: (i, j)),
                  pl.BlockSpec((HALO, CONV_BLK), lambda i, j: (jnp.maximum(i * hb - 1, 0), j)),
                  pl.BlockSpec((CONV_K, CONV_BLK), lambda i, j: (0, j))],
        out_specs=[pl.BlockSpec((tr, CONV_BLK), lambda i, j: (i, j))],
        out_shape=[jax.ShapeDtypeStruct((l, DN_CONV_CH), f32)],
        scratch=[], sem=("parallel", "parallel"), vmem_mb=32, comm=comm)


def _dn_conv_bwd_a(proj, conv_w, dqkv, name, comm=None):
    l = proj.shape[0]
    tr = _tile(l, 256)
    nblk = DN_CONV_CH // CONV_BLK
    heads = CONV_BLK // DN_DK

    def body(x_ref, p_ref, w_ref, d_ref, dc_ref, dw_ref, acc_ref):
        j, i = pl.program_id(0), pl.program_id(1)
        w = w_ref[...]
        acc_ref[...] = jnp.zeros_like(acc_ref)

        def slab_of(l2_scale):
            def slab(r, carry):
                xs = _conv_slab(x_ref, p_ref, r, i, tr)
                c = _conv_of(xs, w)
                a = _silu(c)
                dy = d_ref[_slab_rows(r), :]
                if l2_scale is None:
                    da = dy
                else:
                    parts = []
                    for h in range(heads):
                        sl = slice(h * DN_DK, (h + 1) * DN_DK)
                        ah, dyh = a[:, sl], dy[:, sl]
                        rn = lax.rsqrt(jnp.sum(ah * ah, axis=-1, keepdims=True) + RMS_EPS)
                        yh = ah * rn
                        parts.append((rn * l2_scale) * (dyh - yh * jnp.sum(dyh * yh, axis=-1, keepdims=True)))
                    da = jnp.concatenate(parts, axis=1)
                dc = da * _dsilu(c)
                dc_ref[_slab_rows(r), :] = dc
                for s in range(CONV_K):
                    acc_ref[CONV_K - 1 - s] += dc * xs[s]
                return carry
            return slab

        @pl.when(j == 0)
        def _():
            _slab_loop(tr // HALO, slab_of(DN_DK ** -0.5))

        @pl.when(j == 1)
        def _():
            _slab_loop(tr // HALO, slab_of(1.0))

        @pl.when(j >= 2)
        def _():
            _slab_loop(tr // HALO, slab_of(None))

        ksel = lax.broadcasted_iota(jnp.int32, (CONV_K, 1), 0)
        dw = jnp.zeros((CONV_K, CONV_BLK), f32)
        for k in range(CONV_K):
            dw = dw + jnp.where(ksel == k, jnp.sum(acc_ref[k], axis=0, keepdims=True), 0.0)

        @pl.when(i == 0)
        def _():
            dw_ref[...] = dw

        @pl.when(i > 0)
        def _():
            dw_ref[...] += dw

    hb = tr // HALO
    blk = pl.BlockSpec((tr, CONV_BLK), lambda j, i: (i, j))
    return _call(
        body, (proj, proj, conv_w, dqkv), name=name, grid=(nblk, l // tr),
        in_specs=[blk, pl.BlockSpec((HALO, CONV_BLK), lambda j, i: (jnp.maximum(i * hb - 1, 0), j)),
                  pl.BlockSpec((CONV_K, CONV_BLK), lambda j, i: (0, j)), blk],
        out_specs=[blk, pl.BlockSpec((CONV_K, CONV_BLK), lambda j, i: (0, j))],
        out_shape=[jax.ShapeDtypeStruct((l, DN_CONV_CH), f32), jax.ShapeDtypeStruct((CONV_K, DN_CONV_CH), f32)],
        scratch=[pltpu.VMEM((CONV_K, HALO, CONV_BLK), f32)], sem=("parallel", "arbitrary"), vmem_mb=40, comm=comm)


def _dn_conv_bwd_b(dc, conv_w, dproj, name):
    l = dc.shape[0]
    tr = _tile(l, 256)
    nblk = DN_CONV_CH // CONV_BLK
    nrow = l // tr

    n_slabs = tr // HALO
    pair = 2 * HALO

    def body(d_ref, n_ref, w_ref, _, o_ref):
        i = pl.program_id(0)
        w = w_ref[...]
        nxt_tile = jnp.where(i < nrow - 1, n_ref[...], 0.0)
        lrow = lax.broadcasted_iota(jnp.int32, (HALO, 1), 0)

        def one(r):
            cur = d_ref[_slab_rows(r), :]
            nxt = jnp.where(r < n_slabs - 1, d_ref[_slab_rows(jnp.minimum(r + 1, n_slabs - 1)), :], nxt_tile)
            acc = cur * w[CONV_K - 1:CONV_K, :]
            for s in range(1, CONV_K):
                up = jnp.where(lrow >= HALO - s, pltpu.roll(nxt, HALO - s, 0), pltpu.roll(cur, HALO - s, 0))
                acc = acc + up * w[CONV_K - 1 - s:CONV_K - s, :]
            return jnp.where(i * tr + r * HALO >= PAD, acc, 0.0)

        def two(q, carry):
            rows = pl.ds(pl.multiple_of(q * pair, pair), pair)
            o_ref[rows, :] = _b(jnp.concatenate([one(2 * q), one(2 * q + 1)], axis=0))
            return carry

        lax.fori_loop(0, n_slabs // 2, two, None)

    hb = tr // HALO
    nh = l // HALO
    return pl.pallas_call(
        body, name=name, grid=(nrow, nblk),
        in_specs=[pl.BlockSpec((tr, CONV_BLK), lambda i, j: (i, j)),
                  pl.BlockSpec((HALO, CONV_BLK), lambda i, j: (jnp.minimum((i + 1) * hb, nh - 1), j)),
                  pl.BlockSpec((CONV_K, CONV_BLK), lambda i, j: (0, j)),
                  pl.BlockSpec(memory_space=pl.ANY)],
        out_specs=pl.BlockSpec((tr, CONV_BLK), lambda i, j: (i, j)),
        out_shape=jax.ShapeDtypeStruct(dproj.shape, dproj.dtype), input_output_aliases={3: 0},
        compiler_params=_params(("parallel", "parallel"), 32),
    )(dc, dc, conv_w, dproj)


BA_W = LANES


def _dn_gates(ba_ref, al_ref, dt_ref, n):
    rows = n * CHUNK + lax.broadcasted_iota(jnp.int32, (CHUNK, 1), 0)
    vm = (rows >= PAD).astype(f32)
    bin_ = ba_ref[:, 0:DN_HEADS]
    z = ba_ref[:, DN_HEADS:2 * DN_HEADS] + dt_ref[...]
    sp = jnp.maximum(z, 0.0) + jnp.log1p(jnp.exp(-jnp.abs(z)))
    ea = jnp.exp(al_ref[...])
    beta = _sigmoid(bin_) * vm
    g = -ea * sp * vm
    return vm, bin_, z, ea, beta, g


def _tri():
    ri = lax.broadcasted_iota(jnp.int32, (CHUNK, CHUNK), 0)
    ci = lax.broadcasted_iota(jnp.int32, (CHUNK, CHUNK), 1)
    return ri, ci


def _split(a):
    hi = _b(a)
    return hi, _b(a - hi.astype(f32))


def _mm3(a, b, dot=_nn):
    (ah, al), (bh, bl) = _split(a), _split(b)
    return dot(ah, bh) + (dot(ah, bl) + dot(al, bh))


def _cumsum_rows(tri, g):
    tb = _b(tri)
    g1 = _b(g)
    r1 = g - g1.astype(f32)
    g2 = _b(r1)
    g3 = _b(r1 - g2.astype(f32))
    return _nn(tb, g1) + (_nn(tb, g2) + _nn(tb, g3))


DN_SCAN_CHUNKS = 3


def _scan_chunks(nch):
    return DN_SCAN_CHUNKS if nch % DN_SCAN_CHUNKS == 0 else 1


def _dn_prep(qkv, ba, a_log, dt_bias, name):
    l = qkv.shape[0]
    nch = l // CHUNK
    heads = range(DN_HEADS)

    def body(q_ref, k_ref, v_ref, ba_ref, al_ref, dt_ref, t_ref, u_ref, wq_ref, pk_ref, eg_ref, kpt_ref, qwt_ref):
        n = pl.program_id(0)
        _, _, _, _, beta, g = _dn_gates(ba_ref, al_ref, dt_ref, n)
        ri, ci = _tri()
        incl, strict = ri >= ci, ri > ci
        eye = (ri == ci).astype(f32)
        gam = _cumsum_rows(incl.astype(f32), g)
        gam_t = gam.T
        gc = [gam[:, h:h + 1] for h in heads]
        bh = [beta[:, h:h + 1] for h in heads]
        kh = [k_ref[:, h * DN_DK:(h + 1) * DN_DK] for h in heads]
        kb = [_b(k) for k in kh]
        decay = [jnp.exp(jnp.where(incl, gc[h] - gam_t[h:h + 1, :], -jnp.inf)) for h in heads]
        a = [jnp.where(strict, bh[h] * _nt(kb[h], kb[h]) * decay[h], 0.0) for h in heads]
        t = [eye - a[h] for h in heads]
        p = a
        for _ in range(int(math.log2(CHUNK)) - 1):
            p = [_mm3(p[h], p[h]) for h in heads]
            t = [t[h] + _mm3(t[h], p[h]) for h in heads]
        eg = [jnp.exp(gc[h]) for h in heads]
        for h in heads:
            t_ref[0, h] = t[h]
            u_ref[:, h * DN_DV:(h + 1) * DN_DV] = _mm3(t[h], v_ref[:, h * DN_DV:(h + 1) * DN_DV] * bh[h])
            w = _mm3(t[h], kh[h] * (bh[h] * eg[h]))
            wq_ref[0, h, 0:CHUNK, :] = _b(w)
            qwt_ref[0, h, DN_DK:2 * DN_DK, :] = _b(w.T)
        for h in heads:
            qh = q_ref[:, h * DN_DK:(h + 1) * DN_DK]
            gl = gc[h][CHUNK - 1:CHUNK, :]
            qe = qh * eg[h]
            ke = kh[h] * jnp.exp(gl - gc[h])
            pmat = _nt(_b(qh), kb[h]) * decay[h]
            wq_ref[0, h, CHUNK:2 * CHUNK, :] = _b(qe)
            qwt_ref[0, h, 0:DN_DK, :] = _b(qe.T)
            pk_ref[0, h, 0:CHUNK, :] = _b(pmat)
            pk_ref[0, h, CHUNK:CHUNK + DN_DK, :] = _b(ke.T)
            kpt_ref[0, h, :, 0:DN_DK] = _b(ke)
            kpt_ref[0, h, :, DN_DK:DN_DK + CHUNK] = _b(pmat.T)
            eg_ref[0, h] = jnp.broadcast_to(jnp.exp(gl), (8, LANES))

    vec = pl.BlockSpec((1, DN_HEADS), lambda n: (0, 0))
    return pl.pallas_call(
        body, name=name, grid=(nch,),
        in_specs=[pl.BlockSpec((CHUNK, DN_QK), lambda n: (n, 0)), pl.BlockSpec((CHUNK, DN_QK), lambda n: (n, 1)),
                  pl.BlockSpec((CHUNK, DN_V), lambda n: (n, 1)), pl.BlockSpec((CHUNK, BA_W), lambda n: (n, 0)),
                  vec, vec],
        out_specs=[pl.BlockSpec((1, DN_HEADS, CHUNK, CHUNK), lambda n: (n, 0, 0, 0)),
                   pl.BlockSpec((CHUNK, DN_V), lambda n: (n, 0)),
                   pl.BlockSpec((1, DN_HEADS, 2 * CHUNK, DN_DK), lambda n: (n, 0, 0, 0)),
                   pl.BlockSpec((1, DN_HEADS, CHUNK + DN_DK, CHUNK), lambda n: (n, 0, 0, 0)),
                   pl.BlockSpec((1, DN_HEADS, 8, LANES), lambda n: (n, 0, 0, 0)),
                   pl.BlockSpec((1, DN_HEADS, CHUNK, DN_DK + CHUNK), lambda n: (n, 0, 0, 0)),
                   pl.BlockSpec((1, DN_HEADS, 2 * DN_DK, CHUNK), lambda n: (n, 0, 0, 0))],
        out_shape=[jax.ShapeDtypeStruct((nch, DN_HEADS, CHUNK, CHUNK), f32),
                   jax.ShapeDtypeStruct((l, DN_V), f32),
                   jax.ShapeDtypeStruct((nch, DN_HEADS, 2 * CHUNK, DN_DK), bf16),
                   jax.ShapeDtypeStruct((nch, DN_HEADS, CHUNK + DN_DK, CHUNK), bf16),
                   jax.ShapeDtypeStruct((nch, DN_HEADS, 8, LANES), f32),
                   jax.ShapeDtypeStruct((nch, DN_HEADS, CHUNK, DN_DK + CHUNK), bf16),
                   jax.ShapeDtypeStruct((nch, DN_HEADS, 2 * DN_DK, CHUNK), bf16)],
        compiler_params=_params(("parallel",), 40),
    )(qkv, qkv, qkv, ba, a_log, dt_bias)


def _dn_scan_fwd(u, wq, pk, egl, name):
    l = u.shape[0]
    nch = l // CHUNK
    cs = _scan_chunks(nch)

    def body(u_ref, wq_ref, pk_ref, eg_ref, o_ref, st_ref, vn_ref, s_ref):
        @pl.when(pl.program_id(0) == 0)
        def _():
            s_ref[...] = jnp.zeros_like(s_ref)

        for c in range(cs):
            rows = slice(c * CHUNK, (c + 1) * CHUNK)
            for h in range(DN_HEADS):
                cols = slice(h * DN_DV, (h + 1) * DN_DV)
                s = s_ref[h]
                sb = _b(s)
                st_ref[c, h] = sb
                x = _nn(wq_ref[c, h], sb)
                vnb = _b(u_ref[rows, cols] - x[0:CHUNK])
                vn_ref[rows, cols] = vnb
                y = _nn(pk_ref[c, h], vnb)
                o_ref[rows, cols] = x[CHUNK:2 * CHUNK] + y[0:CHUNK]
                s_ref[h] = eg_ref[c, h][0:1, 0:1] * s + y[CHUNK:CHUNK + DN_DK]

    return pl.pallas_call(
        body, name=name, grid=(nch // cs,),
        in_specs=[pl.BlockSpec((cs * CHUNK, DN_V), lambda n: (n, 0)),
                  pl.BlockSpec((cs, DN_HEADS, 2 * CHUNK, DN_DK), lambda n: (n, 0, 0, 0)),
                  pl.BlockSpec((cs, DN_HEADS, CHUNK + DN_DK, CHUNK), lambda n: (n, 0, 0, 0)),
                  pl.BlockSpec((cs, DN_HEADS, 8, LANES), lambda n: (n, 0, 0, 0))],
        out_specs=[pl.BlockSpec((cs * CHUNK, DN_V), lambda n: (n, 0)),
                   pl.BlockSpec((cs, DN_HEADS, DN_DK, DN_DV), lambda n: (n, 0, 0, 0)),
                   pl.BlockSpec((cs * CHUNK, DN_V), lambda n: (n, 0))],
        out_shape=[jax.ShapeDtypeStruct((l, DN_V), f32),
                   jax.ShapeDtypeStruct((nch, DN_HEADS, DN_DK, DN_DV), bf16),
                   jax.ShapeDtypeStruct((l, DN_V), bf16)],
        scratch_shapes=[pltpu.VMEM((DN_HEADS, DN_DK, DN_DV), f32)],
        compiler_params=_params(("arbitrary",), 40),
    )(u, wq, pk, egl)


def _dn_scan_bwd(do, kpt, qwt, egl, name):
    l = do.shape[0]
    nch = l // CHUNK
    cs = _scan_chunks(nch)
    nblk = nch // cs

    def body(do_ref, kpt_ref, qwt_ref, eg_ref, dvn_ref, dsp_ref, ds_ref):
        @pl.when(pl.program_id(0) == 0)
        def _():
            ds_ref[...] = jnp.zeros_like(ds_ref)

        for c in reversed(range(cs)):
            rows = slice(c * CHUNK, (c + 1) * CHUNK)
            for h in range(DN_HEADS):
                cols = slice(h * DN_DV, (h + 1) * DN_DV)
                dsp = ds_ref[h]
                dspb = _b(dsp)
                dsp_ref[c, h] = dspb
                dob = _b(do_ref[rows, cols])
                kpt_h = kpt_ref[c, h]
                dvn = _nn(kpt_h[:, 0:DN_DK], dspb) + _nn(kpt_h[:, DN_DK:DN_DK + CHUNK], dob)
                dvn_ref[rows, cols] = dvn
                qwt_h = qwt_ref[c, h]
                ds_ref[h] = (eg_ref[c, h][0:1, 0:1] * dsp + _nn(qwt_h[0:DN_DK], dob)
                             - _nn(qwt_h[DN_DK:2 * DN_DK], _b(dvn)))

    rev = lambda s: nblk - 1 - s
    return pl.pallas_call(
        body, name=name, grid=(nblk,),
        in_specs=[pl.BlockSpec((cs * CHUNK, DN_V), lambda s: (rev(s), 0)),
                  pl.BlockSpec((cs, DN_HEADS, CHUNK, DN_DK + CHUNK), lambda s: (rev(s), 0, 0, 0)),
                  pl.BlockSpec((cs, DN_HEADS, 2 * DN_DK, CHUNK), lambda s: (rev(s), 0, 0, 0)),
                  pl.BlockSpec((cs, DN_HEADS, 8, LANES), lambda s: (rev(s), 0, 0, 0))],
        out_specs=[pl.BlockSpec((cs * CHUNK, DN_V), lambda s: (rev(s), 0)),
                   pl.BlockSpec((cs, DN_HEADS, DN_DK, DN_DV), lambda s: (rev(s), 0, 0, 0))],
        out_shape=[jax.ShapeDtypeStruct((l, DN_V), f32),
                   jax.ShapeDtypeStruct((nch, DN_HEADS, DN_DK, DN_DV), bf16)],
        scratch_shapes=[pltpu.VMEM((DN_HEADS, DN_DK, DN_DV), f32)],
        compiler_params=_params(("arbitrary",), 40),
    )(do, kpt, qwt, egl)


def _dn_post_bwd(qkv, ba, a_log, dt_bias, states, dsp_all, tinv_all, u_all, wq, vn_all, do, dvn_all, name):
    l = qkv.shape[0]
    nch = l // CHUNK

    def body(q_ref, k_ref, v_ref, ba_ref, al_ref, dt_ref, st_ref, dsp_ref, t_ref, u_ref, wq_ref, vn_ref, do_ref,
             dvn_ref, dqkv_ref, dba_ref, dal_ref, ddt_ref):
        step = pl.program_id(0)
        n = step

        vm, bin_, z, ea, beta, g = _dn_gates(ba_ref, al_ref, dt_ref, n)
        ri, ci = _tri()
        incl, strict = ri >= ci, ri > ci
        gam = _cumsum_rows(incl.astype(f32), g)
        gam_t = gam.T
        lane8 = lax.broadcasted_iota(jnp.int32, (1, DN_HEADS), 1)
        sub8 = lax.broadcasted_iota(jnp.int32, (DN_HEADS, 1), 0)
        dbeta = jnp.zeros((CHUNK, DN_HEADS), f32)
        dgam = jnp.zeros((CHUNK, DN_HEADS), f32)
        dgam_neg_t = jnp.zeros((DN_HEADS, CHUNK), f32)
        last = (lax.broadcasted_iota(jnp.int32, (CHUNK, 1), 0) == CHUNK - 1).astype(f32)
        hs = range(DN_HEADS)
        each = lambda fn: [fn(h) for h in hs]
        rsum = lambda t: jnp.sum(t, axis=-1, keepdims=True)
        gc = each(lambda h: gam[:, h:h + 1])
        bh = each(lambda h: beta[:, h:h + 1])
        qh = each(lambda h: q_ref[:, h * DN_DK:(h + 1) * DN_DK])
        kh = each(lambda h: k_ref[:, h * DN_DK:(h + 1) * DN_DK])
        doh = each(lambda h: _b(do_ref[:, h * DN_DV:(h + 1) * DN_DV]))
        sb = each(lambda h: st_ref[0, h])
        dspb = each(lambda h: dsp_ref[0, h])
        vnb = each(lambda h: vn_ref[:, h * DN_DV:(h + 1) * DN_DV])
        dvn = each(lambda h: dvn_ref[:, h * DN_DV:(h + 1) * DN_DV])
        wb = each(lambda h: wq_ref[0, h, 0:CHUNK, :])
        decay = each(lambda h: jnp.exp(jnp.where(incl, gc[h] - gam_t[h:h + 1, :], -jnp.inf)))
        qb, kb = each(lambda h: _b(qh[h])), each(lambda h: _b(kh[h]))
        eg = each(lambda h: jnp.exp(gc[h]))
        gl = each(lambda h: gc[h][CHUNK - 1:CHUNK, :])
        ekd = each(lambda h: jnp.exp(gl[h] - gc[h]))
        dvnb = each(lambda h: _b(dvn[h]))
        kk = each(lambda h: _nt(kb[h], kb[h]))
        p = each(lambda h: _nt(qb[h], kb[h]) * decay[h])
        dpraw = each(lambda h: _nt(doh[h], vnb[h]))
        dqe = each(lambda h: _nt(doh[h], sb[h]))
        dke = each(lambda h: _nt(vnb[h], dspb[h]))
        dw = each(lambda h: -_nt(dvnb[h], sb[h]))
        dru = each(lambda h: _mm3(t_ref[0, h], dvn[h], _tn))
        drw = each(lambda h: _mm3(t_ref[0, h], dw[h], _tn))
        dqk = each(lambda h: _b(dpraw[h] * decay[h]))
        for h in hs:
            dqkv_ref[:, h * DN_DK:(h + 1) * DN_DK] = _nn(dqk[h], kb[h]) + dqe[h] * eg[h]
            dqkv_ref[:, 2 * DN_QK + h * DN_DV:2 * DN_QK + (h + 1) * DN_DV] = bh[h] * dru[h]
        da = each(lambda h: jnp.where(strict, -(_nt(_b(dru[h]), _b(u_ref[:, h * DN_DV:(h + 1) * DN_DV]))
                                                + _nt(_b(drw[h]), wb[h])), 0.0))
        dkk = each(lambda h: _b(da[h] * bh[h] * decay[h]))
        for h in hs:
            dqkv_ref[:, DN_QK + h * DN_DK:DN_QK + (h + 1) * DN_DK] = (
                _tn(dqk[h], qb[h]) + dke[h] * ekd[h] + (bh[h] * eg[h]) * drw[h]
                + _nn(dkk[h], kb[h]) + _tn(dkk[h], kb[h]))
        for h in hs:
            keg = kh[h] * eg[h]
            ke = kh[h] * ekd[h]
            rw = rsum(drw[h] * keg)
            rke = rsum(dke[h] * ke)
            db_h = rsum(dru[h] * v_ref[:, h * DN_DV:(h + 1) * DN_DV]) + rw + rsum(da[h] * kk[h] * decay[h])
            mm = da[h] * (bh[h] * kk[h] * decay[h]) + dpraw[h] * p[h]
            dgl = (jnp.sum(rke, axis=0, keepdims=True)
                   + jnp.exp(gl[h]) * jnp.sum(rsum(dspb[h].astype(f32) * sb[h].astype(f32)), axis=0, keepdims=True))
            dg_h = rsum(mm) + rw * bh[h] + rsum(dqe[h] * (qh[h] * eg[h])) - rke + last * dgl
            dbeta = dbeta + jnp.where(lane8 == h, db_h, 0.0)
            dgam = dgam + jnp.where(lane8 == h, dg_h, 0.0)
            dgam_neg_t = dgam_neg_t + jnp.where(sub8 == h, jnp.sum(mm, axis=0, keepdims=True), 0.0)
        dgam = dgam - dgam_neg_t.T
        dg = _cumsum_rows((ri <= ci).astype(f32), dgam)
        sg = _sigmoid(bin_)
        dbin = dbeta * vm * sg * (1.0 - sg)
        dain = dg * (-ea) * vm * _sigmoid(z)
        dba_ref[...] = jnp.zeros_like(dba_ref)
        dba_ref[:, 0:DN_HEADS] = dbin
        dba_ref[:, DN_HEADS:2 * DN_HEADS] = dain
        dal = jnp.sum(dg * g, axis=0, keepdims=True)
        ddt = jnp.sum(dain, axis=0, keepdims=True)

        @pl.when(step == 0)
        def _():
            dal_ref[...] = dal
            ddt_ref[...] = ddt

        @pl.when(step > 0)
        def _():
            dal_ref[...] += dal
            ddt_ref[...] += ddt

    vec = pl.BlockSpec((1, DN_HEADS), lambda s: (0, 0))
    qs = pl.BlockSpec((CHUNK, DN_QK), lambda s: (s, 0))
    ks = pl.BlockSpec((CHUNK, DN_QK), lambda s: (s, 1))
    vs = pl.BlockSpec((CHUNK, DN_V), lambda s: (s, 1))
    v0 = pl.BlockSpec((CHUNK, DN_V), lambda s: (s, 0))
    st = pl.BlockSpec((1, DN_HEADS, DN_DK, DN_DV), lambda s: (s, 0, 0, 0))
    return pl.pallas_call(
        body, name=name, grid=(nch,),
        in_specs=[qs, ks, vs, pl.BlockSpec((CHUNK, BA_W), lambda s: (s, 0)), vec, vec, st, st,
                  pl.BlockSpec((1, DN_HEADS, CHUNK, CHUNK), lambda s: (s, 0, 0, 0)),
                  v0, pl.BlockSpec((1, DN_HEADS, 2 * CHUNK, DN_DK), lambda s: (s, 0, 0, 0)), v0, v0, v0],
        out_specs=[pl.BlockSpec((CHUNK, DN_CONV_CH), lambda s: (s, 0)),
                   pl.BlockSpec((CHUNK, BA_W), lambda s: (s, 0)), vec, vec],
        out_shape=[jax.ShapeDtypeStruct((l, DN_CONV_CH), f32), jax.ShapeDtypeStruct((l, BA_W), f32),
                   jax.ShapeDtypeStruct((1, DN_HEADS), f32), jax.ShapeDtypeStruct((1, DN_HEADS), f32)],
        compiler_params=_params(("arbitrary",), 48),
    )(qkv, qkv, qkv, ba, a_log, dt_bias, states, dsp_all, tinv_all, u_all, wq, vn_all, do, dvn_all)


def _ffn_fwd(h, nw, wgu, wd, tb, th, tag, plan):
    d = h.shape[1]
    fh = wd.shape[0]
    hn = _rms_fwd(h, nw, f"{tag}_norm")
    ab = plan.matmul(f"{tag}_gu", hn, wgu, mode="nn", tm=tb, tn=512, tk=d)
    s = _swiglu_fwd(ab, f"{tag}_act")
    out = plan.matmul(f"{tag}_down", s, wd, mode="nn", tm=th, tn=512, tk=fh, res=h)
    return out, (hn, ab, s)


def _ffn_bwd(dh, dhb, h, nw, wgu, wd, saved, tb, th, tag, plan):
    hn, ab, s = saved
    d = h.shape[1]
    fh = wd.shape[0]
    ds = _matmul(dhb, wd, mode="nt", tm=th, tn=fh // 2, tk=d, name=f"{tag}_b_ds")
    dwd = _matmul(s, dhb, mode="tn", tm=fh // 2, tn=d, tk=th, out_dtype=bf16, name=f"{tag}_b_dwd")
    plan.grads["down" + tag[-1]] = dwd
    dab = _swiglu_bwd(ab, ds, f"{tag}_b_act")
    dhn = plan.matmul(f"{tag}_b_dhn", dab, wgu, mode="nt", tm=th, tn=d, tk=fh // 2)
    dwgu = _matmul(hn, dab, mode="tn", tm=d, tn=512, tk=tb, out_dtype=bf16, name=f"{tag}_b_dwgu")
    dh2, dh2b, dnw = _rms_bwd(h, nw, dhn, dh, f"{tag}_b_norm")
    return dh2, dh2b, dnw, dwgu, dwd


class _Plan:
    GATHERS = {"ret_proj": ("ret_out", "gate0"), "ret_scan": ("up0", "down0"), "ffn0_gu": ("dn_in",),
               "ffn0_down": ("dn_out",), "dn_proj": ("gate1", "up1"), "dn_conv": ("down1",)}
    SCATTERS = {"ffn1_b_dhn": ("down1",), "dn_b_conv_a": ("gate1", "up1", "dn_out"), "ffn0_b_dhn": ("dn_in",),
                "ret_b_scan": ("gate0", "up0", "down0", "ret_out"), "ret_b_dhn": ("ret_in",)}

    def __init__(self, shards, wts):
        self.shards, self.wts, self.grads, self.parts = shards, wts, {}, {}

    def _exchange(self, stage):
        if self.shards is None:
            return None
        if stage in self.GATHERS:
            return _Exchange([self.shards[n] for n in self.GATHERS[stage]], True)
        if stage in self.SCATTERS:
            return _Exchange([self._dev_major(n) for n in self.SCATTERS[stage]], False)
        return None

    def _dev_major(self, name):
        g = self.grads
        if name[:-1] in ("gate", "up"):
            gu = g["gu" + name[-1]]
            fh = gu.shape[1] // 2
            part = gu[:, :fh] if name.startswith("gate") else gu[:, fh:]
            return _dev_major_cols(part, fh // N_DEV)
        if name[:-1] == "down":
            dwd = g[name]
            return dwd.reshape(N_DEV, dwd.shape[0] // N_DEV, dwd.shape[1])
        if name in ("ret_out", "dn_out"):
            return g[name].reshape(N_DEV, g[name].shape[0] // N_DEV, g[name].shape[1])
        return _dev_major_cols(g[name], self.shards[name].shape[-1])

    def _landed(self, stage, outs):
        if stage in self.SCATTERS:
            self.parts.update(zip(self.SCATTERS[stage], outs))
            return
        w = self.wts
        cols = lambda t: t.transpose(1, 0, 2).reshape(t.shape[1], N_DEV * t.shape[2])
        rows = lambda t: t.reshape(N_DEV * t.shape[1], t.shape[2])
        for name, t in zip(self.GATHERS[stage], outs):
            if name in ("ret_out", "dn_out") or name.startswith("down"):
                w[name] = rows(t)
            elif name == "dn_in":
                full = cols(t)
                n_main = DN_CONV_CH + DN_V
                w["dn_main"] = full[:, :n_main]
                w["dn_ba"] = jnp.pad(full[:, n_main:], ((0, 0), (0, BA_W - (full.shape[1] - n_main))))
            else:
                w[name] = cols(t)
        for layer in "01":
            if "gate" + layer in w and "up" + layer in w and "gu" + layer not in w:
                w["gu" + layer] = jnp.concatenate([w["gate" + layer], w["up" + layer]], axis=1)

    def matmul(self, stage, a, b, **kw):
        comm = self._exchange(stage)
        if comm is None:
            return _matmul(a, b, name=stage, **kw)
        out, landed = _matmul(a, b, name=stage, comm=comm, **kw)
        self._landed(stage, landed)
        return out

    def call(self, stage, fn, *args, n_out):
        comm = self._exchange(stage)
        out = fn(*args, stage, comm=comm)
        if comm is not None:
            self._landed(stage, out[n_out:])
        return out[:n_out]


def _local_step(x2, target, wts, shards=None):
    plan = _Plan(shards, wts)
    s_len, d = x2.shape
    l = s_len + CHUNK
    tb = _tile(l, 3072)
    th = tb // 2 if (tb // 2) % 16 == 0 else tb
    half = RET_DK // 2
    inv_freq = ROPE_BASE ** (-jnp.arange(half, dtype=f32) / half)
    ang = (jnp.arange(l) - PAD).astype(f32)[:, None] * inv_freq[None, :]
    cos, sin = jnp.cos(ang), jnp.sin(ang)
    lgs = jnp.log1p(-jnp.exp2(-5.0 - jnp.arange(RET_HEADS, dtype=f32)))
    gcs = jnp.exp(lgs * _ret_block(l))

    h0 = jnp.concatenate([jnp.zeros((PAD, d), f32), wts["meta"], x2], axis=0)
    mixw, ffnw = wts["mix_norm"], wts["ffn_norm"]

    hn0 = _rms_fwd(h0, mixw[0:1], "l0_norm")
    proj0 = plan.matmul("ret_proj", hn0, wts["ret_in"], mode="nn", tm=tb, tn=512, tk=d)
    qk0 = _ret_prep(proj0, cos, sin, "ret_prep")
    o0, st0 = plan.call("ret_scan", _ret_scan_fwd, qk0, proj0, lgs, gcs, n_out=2)
    y0 = _gnorm_fwd(o0, proj0, wts["ret_gn"], RET_HEADS, RET_DV, 2, "ret_gnorm")
    h1 = _matmul(y0, wts["ret_out"], mode="nn", tm=th, tn=512, tk=RET_V, res=h0, name="ret_out")
    h2, ffn0 = _ffn_fwd(h1, ffnw[0:1], wts["gu0"], wts["down0"], tb, th, "ffn0", plan)

    hn2 = _rms_fwd(h2, mixw[1:2], "l1_norm")
    proj1 = plan.matmul("dn_proj", hn2, wts["dn_main"], mode="nn", tm=tb, tn=512, tk=d)
    ba = _matmul(hn2, wts["dn_ba"], mode="nn", tm=tb, tn=BA_W, tk=d, name="dn_proj_ba")
    (qkv1,) = plan.call("dn_conv", _dn_conv_fwd, proj1, wts["conv_w"], n_out=1)
    tinv1, u1, wq1, pk1, egl1, kpt1, qwt1 = _dn_prep(qkv1, ba, wts["a_log"], wts["dt_bias"], "dn_prep")
    o1, st1, vn1 = _dn_scan_fwd(u1, wq1, pk1, egl1, "dn_scan")
    y1 = _gnorm_fwd(o1, proj1, wts["dn_norm"], DN_HEADS, DN_DV, 2, "dn_gnorm")
    h3 = _matmul(y1, wts["dn_out"], mode="nn", tm=th, tn=512, tk=DN_V, res=h2, name="dn_out")
    h4, ffn1 = _ffn_fwd(h3, ffnw[1:2], wts["gu1"], wts["down1"], tb, th, "ffn1", plan)

    dh4, dh4b, dfinal, loss = _final_loss(h4, wts["final_norm"], target, "final_loss")
    gr = plan.grads
    dh3, dh3b, dffn1, gr["gu1"], gr["down1"] = _ffn_bwd(dh4, dh4b, h3, ffnw[1:2], wts["gu1"], wts["down1"], ffn1,
                                                       tb, th, "ffn1", plan)

    dy1 = _matmul(dh3b, wts["dn_out"], mode="nt", tm=th, tn=1024, tk=d, name="dn_b_dy")
    gr["dn_out"] = _matmul(y1, dh3b, mode="tn", tm=1024, tn=d, tk=tb, out_dtype=bf16, name="dn_b_dwout")
    do1, dproj1, ddn_norm = _gnorm_bwd(o1, proj1, wts["dn_norm"], dy1, DN_HEADS, DN_DV, 2, "dn_b_gnorm")
    dvn1, dsp1 = _dn_scan_bwd(do1, kpt1, qwt1, egl1, "dn_b_scan")
    dqkv1, dba, dalog, ddt = _dn_post_bwd(qkv1, ba, wts["a_log"], wts["dt_bias"], st1, dsp1, tinv1, u1, wq1, vn1,
                                          do1, dvn1, "dn_b_post")
    dc1, dconv = plan.call("dn_b_conv_a", _dn_conv_bwd_a, proj1, wts["conv_w"], dqkv1, n_out=2)
    dproj1 = _dn_conv_bwd_b(dc1, wts["conv_w"], dproj1, "dn_b_conv_b")
    dbab = dba.astype(bf16)
    n_main = dproj1.shape[1]
    dhn2 = plan.matmul("dn_b_dhn", dproj1, wts["dn_main"], mode="nt", tm=th, tn=d, tk=n_main // 4)
    dhn2 = _matmul(dbab, wts["dn_ba"], mode="nt", tm=th, tn=d, tk=BA_W, res=dhn2, name="dn_b_dhn_ba")
    dw_main = _matmul(hn2, dproj1, mode="tn", tm=d, tn=512, tk=tb, out_dtype=bf16, name="dn_b_dwin")
    dw_ba = _matmul(hn2, dbab, mode="tn", tm=d, tn=BA_W, tk=tb, out_dtype=bf16, name="dn_b_dwin_ba")
    gr["dn_in"] = jnp.concatenate([dw_main, dw_ba], axis=1)
    dh2, dh2b, dmix1 = _rms_bwd(h2, mixw[1:2], dhn2, dh3, "l1_b_norm")

    dh1, dh1b, dffn0, gr["gu0"], gr["down0"] = _ffn_bwd(dh2, dh2b, h1, ffnw[0:1], wts["gu0"], wts["down0"], ffn0,
                                                       tb, th, "ffn0", plan)

    dy0 = _matmul(dh1b, wts["ret_out"], mode="nt", tm=th, tn=1024, tk=d, name="ret_b_dy")
    gr["ret_out"] = _matmul(y0, dh1b, mode="tn", tm=1024, tn=d, tk=tb, out_dtype=bf16, name="ret_b_dwout")
    do0, dproj0, dret_gn = _gnorm_bwd(o0, proj0, wts["ret_gn"], dy0, RET_HEADS, RET_DV, 2, "ret_b_gnorm")
    dq0, dk0, dproj0 = plan.call("ret_b_scan", _ret_scan_bwd, qk0, proj0, st0, do0, dproj0, lgs, gcs, n_out=3)
    dproj0 = _ret_prep_bwd(dq0, dk0, cos, sin, dproj0, "ret_b_prep")
    n_in = dproj0.shape[1]
    gr["ret_in"] = _matmul(hn0, dproj0, mode="tn", tm=d, tn=512, tk=tb, out_dtype=bf16, name="ret_b_dwin")
    dhn0 = plan.matmul("ret_b_dhn", dproj0, wts["ret_in"], mode="nt", tm=th, tn=d, tk=n_in // 4)
    dh0, _, dmix0 = _rms_bwd(h0, mixw[0:1], dhn0, dh1, "l0_b_norm")

    gr.update(meta=dh0[PAD:CHUNK], mix_norm=jnp.concatenate([dmix0, dmix1], axis=0),
              ffn_norm=jnp.concatenate([dffn0, dffn1], axis=0), ret_gn=dret_gn, conv_w=dconv, a_log=dalog,
              dt_bias=ddt, dn_norm=ddn_norm, final_norm=dfinal)
    return loss, dh0[CHUNK:], gr, plan


def _adamw_reduce(parts, w, m, v, name):
    _, r, c = parts.shape
    c_pad = -(-c // LANES) * LANES
    tr = _div_tile(r, max(8, (3 * MIB // 16) // c_pad // 8 * 8), 16)

    def body(p_ref, w_ref, m_ref, v_ref, g_ref, d_ref, nm_ref, nv_ref):
        g = p_ref[0].astype(f32)
        for s in range(1, N_DEV):
            g = g + p_ref[s].astype(f32)
        mm = ADAM_B1 * m_ref[...] + (1.0 - ADAM_B1) * g
        vv = ADAM_B2 * v_ref[...] + (1.0 - ADAM_B2) * (g * g)
        m_hat = mm / (1.0 - ADAM_B1 ** ADAM_STEP)
        v_hat = vv / (1.0 - ADAM_B2 ** ADAM_STEP)
        g_ref[...] = g
        d_ref[...] = -ADAM_LR * (m_hat / (jnp.sqrt(v_hat) + ADAM_EPS) + ADAM_WD * w_ref[...])
        nm_ref[...] = mm
        nv_ref[...] = vv

    blk = pl.BlockSpec((tr, c), lambda i: (i, 0))
    return pl.pallas_call(
        body, name=name, grid=(r // tr,),
        in_specs=[pl.BlockSpec((N_DEV, tr, c), lambda i: (0, i, 0)), blk, blk, blk], out_specs=[blk] * 4,
        out_shape=[jax.ShapeDtypeStruct((r, c), f32)] * 4,
        compiler_params=_params(("parallel",), 48),
    )(parts, w, m, v)


def _dev_major_cols(g, width):
    r = g.shape[0]
    return g[:, :N_DEV * width].reshape(r, N_DEV, width).transpose(1, 0, 2)


def kernel(x, meta_tokens, mix_norm_w, ffn_norm_w, ret_w_in, ret_gn_w, ret_w_out, dn_w_in, dn_conv_w, dn_a_log, dn_dt_bias, dn_norm_w, dn_w_out, ffn_w_gate, ffn_w_up, ffn_w_down, final_norm_w, loss_target, m_meta_tokens, m_mix_norm_w, m_ffn_norm_w, m_ret_w_in, m_ret_gn_w, m_ret_w_out, m_dn_w_in, m_dn_conv_w, m_dn_a_log, m_dn_dt_bias, m_dn_norm_w, m_dn_w_out, m_ffn_w_gate, m_ffn_w_up, m_ffn_w_down, m_final_norm_w, v_meta_tokens, v_mix_norm_w, v_ffn_norm_w, v_ret_w_in, v_ret_gn_w, v_ret_w_out, v_dn_w_in, v_dn_conv_w, v_dn_a_log, v_dn_dt_bias, v_dn_norm_w, v_dn_w_out, v_ffn_w_gate, v_ffn_w_up, v_ffn_w_down, v_final_norm_w):
    d = x.shape[-1]
    me = 4 * lax.axis_index("x") + 2 * lax.axis_index("y") + lax.axis_index("c")

    shards = dict(ret_in=ret_w_in[0].astype(bf16), ret_out=ret_w_out[0].astype(bf16),
                  dn_in=dn_w_in[0].astype(bf16), dn_out=dn_w_out[0].astype(bf16))
    for layer in (0, 1):
        shards[f"gate{layer}"] = ffn_w_gate[layer].astype(bf16)
        shards[f"up{layer}"] = ffn_w_up[layer].astype(bf16)
        shards[f"down{layer}"] = ffn_w_down[layer].astype(bf16)
    g_ret_in, g_meta, g_conv, g_dnn = _exchange([shards["ret_in"], meta_tokens, dn_conv_w[0], dn_norm_w], True,
                                                "gather_first")
    cols = lambda g: g.transpose(1, 0, 2).reshape(g.shape[1], N_DEV * g.shape[2])
    wts = dict(meta=cols(g_meta), mix_norm=mix_norm_w, ffn_norm=ffn_norm_w, ret_in=cols(g_ret_in), ret_gn=ret_gn_w,
               conv_w=cols(g_conv), a_log=dn_a_log, dt_bias=dn_dt_bias, dn_norm=cols(g_dnn),
               final_norm=final_norm_w.reshape(1, d))

    loss_part, grad_x, gr, plan = _local_step(x[0], loss_target[0], wts, shards)
    loss = lax.psum(loss_part[0, 0], AXES)

    pp = plan.parts
    both = lambda name: jnp.concatenate([pp[name + "0"], pp[name + "1"]], axis=1)
    big_parts = [pp["ret_in"], pp["ret_out"], pp["dn_in"], pp["dn_out"], both("gate"), both("up"), both("down")]
    big_names = ["ret_w_in", "ret_w_out", "dn_w_in", "dn_w_out", "ffn_w_gate", "ffn_w_up", "ffn_w_down"]
    big_w = [ret_w_in, ret_w_out, dn_w_in, dn_w_out, ffn_w_gate, ffn_w_up, ffn_w_down]
    big_m = [m_ret_w_in, m_ret_w_out, m_dn_w_in, m_dn_w_out, m_ffn_w_gate, m_ffn_w_up, m_ffn_w_down]
    big_v = [v_ret_w_in, v_ret_w_out, v_dn_w_in, v_dn_w_out, v_ffn_w_gate, v_ffn_w_up, v_ffn_w_down]
    res = {}
    for nm, parts, w_, m_, v_ in zip(big_names, big_parts, big_w, big_m, big_v):
        r2, c2 = parts.shape[1], parts.shape[2]
        outs = _adamw_reduce(parts, w_.reshape(r2, c2), m_.reshape(r2, c2), v_.reshape(r2, c2), f"adamw_{nm}")
        res[nm] = [o.reshape(w_.shape) for o in outs]

    small_names = ["meta_tokens", "mix_norm_w", "ffn_norm_w", "ret_gn_w", "dn_conv_w", "dn_a_log", "dn_dt_bias",
                   "dn_norm_w", "final_norm_w"]
    small_g = [gr["meta"], gr["mix_norm"], gr["ffn_norm"], gr["ret_gn"], gr["conv_w"], gr["a_log"], gr["dt_bias"],
               gr["dn_norm"], gr["final_norm"]]
    small_w = [meta_tokens, mix_norm_w, ffn_norm_w, ret_gn_w, dn_conv_w, dn_a_log, dn_dt_bias, dn_norm_w, final_norm_w]
    small_m = [m_meta_tokens, m_mix_norm_w, m_ffn_norm_w, m_ret_gn_w, m_dn_conv_w, m_dn_a_log, m_dn_dt_bias,
               m_dn_norm_w, m_final_norm_w]
    small_v = [v_meta_tokens, v_mix_norm_w, v_ffn_norm_w, v_ret_gn_w, v_dn_conv_w, v_dn_a_log, v_dn_dt_bias,
               v_dn_norm_w, v_final_norm_w]
    sharded = {"meta_tokens", "dn_conv_w", "dn_norm_w"}
    flat = jnp.concatenate([g.reshape(-1) for g in small_g])
    row = 8 * LANES
    n_flat = flat.shape[0]
    flat = jnp.pad(flat, (0, -n_flat % row)).reshape(-1, row)
    (gathered,) = _exchange([flat], True, "gather_small_grads")
    gathered = gathered.reshape(N_DEV, -1)
    pieces, off = [], 0
    for nm, g, w_ in zip(small_names, small_g, small_w):
        full = gathered[:, off:off + g.size].reshape((N_DEV,) + g.shape)
        off += g.size
        if nm in sharded:
            wloc = w_.shape[-1]
            full = lax.dynamic_slice_in_dim(full, me * wloc, wloc, axis=full.ndim - 1)
        pieces.append(full.reshape(N_DEV, -1))
    sizes = [p.shape[1] for p in pieces]
    n_loc = sum(sizes)
    pad_loc = -n_loc % row

    def pack(vs, lead):
        cat = jnp.concatenate([a.reshape(lead + (-1,)) for a in vs], axis=-1)
        cat = jnp.pad(cat, [(0, 0)] * len(lead) + [(0, pad_loc)])
        return cat.reshape(lead + (-1, row))

    outs = _adamw_reduce(pack(pieces, (N_DEV,)), pack(small_w, ()), pack(small_m, ()), pack(small_v, ()), "adamw_small")
    off = 0
    for nm, sz, w_ in zip(small_names, sizes, small_w):
        res[nm] = [o.reshape(-1)[off:off + sz].reshape(w_.shape) for o in outs]
        off += sz

    order = ["meta_tokens", "mix_norm_w", "ffn_norm_w", "ret_w_in", "ret_gn_w", "ret_w_out", "dn_w_in", "dn_conv_w",
             "dn_a_log", "dn_dt_bias", "dn_norm_w", "dn_w_out", "ffn_w_gate", "ffn_w_up", "ffn_w_down", "final_norm_w"]
    grad_x = grad_x.reshape(x.shape)
    return (loss, grad_x, *[res[nm][0] for nm in order], *[res[nm][1] for nm in order],
            *[res[nm][2] for nm in order], *[res[nm][3] for nm in order])
```

```python
import functools
import math

import jax
import jax.numpy as jnp
import numpy as np
from jax import lax
from jax.experimental import pallas as pl
from jax.experimental.pallas import tpu as pltpu

f32 = jnp.float32
bf16 = jnp.bfloat16
HI = lax.Precision.HIGHEST

N_META = 16
CHUNK = 64
PAD = CHUNK - N_META
RMS_EPS = 1e-6
RET_HEADS, RET_DK, RET_DV = 4, 256, 512
RET_QK, RET_V = RET_HEADS * RET_DK, RET_HEADS * RET_DV
DN_HEADS, DN_DK, DN_DV = 8, 128, 256
DN_QK, DN_V = DN_HEADS * DN_DK, DN_HEADS * DN_DV
DN_CONV_CH = 2 * DN_QK + DN_V
CONV_K = 4
ROPE_BASE = 10000.0
ADAM_LR, ADAM_B1, ADAM_B2, ADAM_EPS, ADAM_WD, ADAM_STEP = 0.001, 0.9, 0.999, 1e-08, 0.01, 10
N_DEV = 8
AXES = ("x", "y", "c")
LANES = 128
MIB = 1024 * 1024


def _tile(n_rows, cap):
    nch = n_rows // CHUNK
    best = 1
    for d in range(1, nch + 1):
        if nch % d == 0 and d * CHUNK <= cap:
            best = d
    return best * CHUNK


def _div_tile(n, cap, align):
    best = None
    for d in range(align, min(n, cap) + 1, align):
        if n % d == 0:
            best = d
    return best if best is not None else n


def _params(sem, vmem_mb):
    return pltpu.CompilerParams(dimension_semantics=sem, vmem_limit_bytes=int(vmem_mb * MIB))


def _nn(a, b, precision=None):
    return jnp.dot(a, b, preferred_element_type=f32, precision=precision)


def _nt(a, b, precision=None):
    return lax.dot_general(a, b, (((1,), (1,)), ((), ())), preferred_element_type=f32, precision=precision)


def _tn(a, b, precision=None):
    return lax.dot_general(a, b, (((0,), (0,)), ((), ())), preferred_element_type=f32, precision=precision)


def _b(x):
    return x.astype(bf16)


def _sigmoid(x):
    return 1.0 / (1.0 + jnp.exp(-x))


def _silu(x):
    return x * _sigmoid(x)


def _dsilu(x):
    s = _sigmoid(x)
    return s * (1.0 + x * (1.0 - s))


def _peer(k):
    x, y, c = lax.axis_index("x"), lax.axis_index("y"), lax.axis_index("c")
    px = 1 - x if k & 4 else x
    py = 1 - y if k & 2 else y
    pc = 1 - c if k & 1 else c
    return (px, py, pc), 4 * px + 2 * py + pc


class _Exchange:
    def __init__(self, arrs, gather):
        self.arrs, self.gather, self.n = list(arrs), gather, len(arrs)
        self.out_shapes = [jax.ShapeDtypeStruct(((N_DEV,) + a.shape) if gather else a.shape, a.dtype) for a in arrs]
        self.specs = [pl.BlockSpec(memory_space=pltpu.HBM)] * self.n
        self.scratch = [pltpu.SemaphoreType.DMA((self.n, N_DEV - 1)), pltpu.SemaphoreType.DMA((self.n, N_DEV - 1)),
                        pltpu.SemaphoreType.DMA((self.n,))]

    def _copies(self, ins, outs, sems):
        send_sems, recv_sems, local_sems = sems
        me = 4 * lax.axis_index("x") + 2 * lax.axis_index("y") + lax.axis_index("c")
        src = (lambda a, dest: ins[a]) if self.gather else (lambda a, dest: ins[a].at[dest])
        local = [pltpu.make_async_copy(src(a, me), outs[a].at[me], local_sems.at[a]) for a in range(self.n)]
        sends, lands = [], []
        for k in range(1, N_DEV):
            peer, pidx = _peer(k)
            for a in range(self.n):
                for dst, lst in ((outs[a].at[me], sends), (outs[a].at[pidx], lands)):
                    lst.append(pltpu.make_async_remote_copy(
                        src_ref=src(a, pidx), dst_ref=dst, send_sem=send_sems.at[a, k - 1],
                        recv_sem=recv_sems.at[a, k - 1], device_id=peer, device_id_type=pl.DeviceIdType.MESH))
        return local, sends, lands

    def start(self, ins, outs, sems):
        local, sends, _ = self._copies(ins, outs, sems)
        for cp in local + sends:
            cp.start()

    def wait(self, ins, outs, sems):
        local, sends, lands = self._copies(ins, outs, sems)
        for cp in lands:
            cp.wait_recv()
        for cp in sends:
            cp.wait_send()
        for cp in local:
            cp.wait()


def _call(body, args, *, name, grid, in_specs, out_specs, out_shape, scratch=(), sem, vmem_mb, comm=None,
          aliases=None):
    aliases = aliases or {}
    if comm is None:
        out = pl.pallas_call(body, name=name, grid=grid, in_specs=list(in_specs), out_specs=list(out_specs),
                             out_shape=list(out_shape), scratch_shapes=list(scratch), input_output_aliases=aliases,
                             compiler_params=_params(sem, vmem_mb))(*args)
        return list(out)
    n_in, n_out, n_scr, nc = len(args), len(out_shape), len(scratch), comm.n

    def carried(*refs):
        ins, cin = refs[:n_in], refs[n_in:n_in + nc]
        o0 = n_in + nc
        outs, cout = refs[o0:o0 + n_out], refs[o0 + n_out:o0 + n_out + nc]
        s0 = o0 + n_out + nc
        scr, sems = refs[s0:s0 + n_scr], refs[s0 + n_scr:]
        first = functools.reduce(jnp.logical_and, [pl.program_id(i) == 0 for i in range(len(grid))])
        last = functools.reduce(jnp.logical_and, [pl.program_id(i) == grid[i] - 1 for i in range(len(grid))])

        @pl.when(first)
        def _():
            comm.start(cin, cout, sems)

        body(*ins, *outs, *scr)

        @pl.when(last)
        def _():
            comm.wait(cin, cout, sems)

    out = pl.pallas_call(
        carried, name=name, grid=grid, in_specs=list(in_specs) + comm.specs, out_specs=list(out_specs) + comm.specs,
        out_shape=list(out_shape) + comm.out_shapes, scratch_shapes=list(scratch) + comm.scratch,
        input_output_aliases=aliases,
        compiler_params=_params(("arbitrary",) * len(grid), vmem_mb))(*args, *comm.arrs)
    return list(out)


def _exchange(arrs, gather, name):
    comm = _Exchange(arrs, gather)

    def body(*refs):
        ins, outs, sems = refs[:comm.n], refs[comm.n:2 * comm.n], refs[2 * comm.n:]
        comm.start(ins, outs, sems)
        comm.wait(ins, outs, sems)

    return pl.pallas_call(body, name=name, in_specs=comm.specs, out_specs=comm.specs, out_shape=comm.out_shapes,
                          scratch_shapes=comm.scratch)(*comm.arrs)


def _matmul(a, b, *, mode, tm, tn, tk, name, out_dtype=f32, res=None, vmem_mb=48, comm=None, pair2=None):
    def dims(x, y):
        if mode == "nn":
            (m_, k_), (k2_, n_) = x.shape, y.shape
        elif mode == "nt":
            (m_, k_), (n_, k2_) = x.shape, y.shape
        else:
            (k_, m_), (k2_, n_) = x.shape, y.shape
        assert k_ == k2_ and m_ % tm == 0 and n_ % tn == 0 and k_ % tk == 0, (name, x.shape, y.shape, tm, tn, tk)
        return m_, n_, k_ // tk

    m, n, nk1 = dims(a, b)
    nk2 = dims(*pair2)[2] if pair2 is not None else 0
    nk = nk1 + nk2
    dot = {"nn": _nn, "nt": _nt, "tn": _tn}[mode]

    def specs(k_of):
        a_spec = {"nn": pl.BlockSpec((tm, tk), lambda i, j, kk: (i, k_of(kk))),
                  "nt": pl.BlockSpec((tm, tk), lambda i, j, kk: (i, k_of(kk))),
                  "tn": pl.BlockSpec((tk, tm), lambda i, j, kk: (k_of(kk), i))}[mode]
        b_spec = {"nn": pl.BlockSpec((tk, tn), lambda i, j, kk: (k_of(kk), j)),
                  "nt": pl.BlockSpec((tn, tk), lambda i, j, kk: (j, k_of(kk))),
                  "tn": pl.BlockSpec((tk, tn), lambda i, j, kk: (k_of(kk), j))}[mode]
        return [a_spec, b_spec]

    o_spec = pl.BlockSpec((tm, tn), lambda i, j, kk: (i, j))
    has_res = res is not None
    n_ops = 2 + (2 if pair2 is not None else 0) + (1 if has_res else 0)

    def body(*refs):
        a_ref, b_ref = refs[:2]
        a2_ref, b2_ref = refs[2:4] if pair2 is not None else (None, None)
        r_ref = refs[n_ops - 1] if has_res else None
        o_ref = refs[n_ops]
        rest = refs[n_ops + 1:]
        if nk == 1:
            part = dot(_b(a_ref[...]), _b(b_ref[...]))
            if has_res:
                part = part + r_ref[...]
            o_ref[...] = part.astype(out_dtype)
            return
        acc_ref = rest[0]
        kk = pl.program_id(2)

        @pl.when(kk == 0)
        def _():
            acc_ref[...] = dot(_b(a_ref[...]), _b(b_ref[...]))

        @pl.when(jnp.logical_and(kk > 0, kk < nk1))
        def _():
            acc_ref[...] += dot(_b(a_ref[...]), _b(b_ref[...]))

        if pair2 is not None:
            @pl.when(kk >= nk1)
            def _():
                acc_ref[...] += dot(_b(a2_ref[...]), _b(b2_ref[...]))

        @pl.when(kk == nk - 1)
        def _():
            tot = acc_ref[...]
            if has_res:
                tot = tot + r_ref[...]
            o_ref[...] = tot.astype(out_dtype)

    in_specs = specs(lambda kk: jnp.minimum(kk, nk1 - 1))
    args = (a, b)
    if pair2 is not None:
        in_specs += specs(lambda kk: jnp.maximum(kk - nk1, 0))
        args += tuple(pair2)
    if has_res:
        in_specs.append(o_spec)
        args += (res,)
    out = _call(body, args, name=name, grid=(m // tm, n // tn, nk), in_specs=in_specs, out_specs=[o_spec],
                out_shape=[jax.ShapeDtypeStruct((m, n), out_dtype)],
                scratch=[pltpu.VMEM((tm, tn), f32)] if nk > 1 else [],
                sem=("parallel", "parallel", "arbitrary"), vmem_mb=vmem_mb, comm=comm)
    return out[0] if comm is None else (out[0], out[1:])


def _rms_fwd(h, w, name):
    l, d = h.shape
    tr = _tile(l, 512)

    def body(h_ref, w_ref, o_ref):
        x = h_ref[...]
        r = lax.rsqrt(jnp.mean(x * x, axis=-1, keepdims=True) + RMS_EPS)
        o_ref[...] = _b(x * r * w_ref[...])

    return pl.pallas_call(
        body, name=name, grid=(l // tr,),
        in_specs=[pl.BlockSpec((tr, d), lambda i: (i, 0)), pl.BlockSpec((1, d), lambda i: (0, 0))],
        out_specs=pl.BlockSpec((tr, d), lambda i: (i, 0)),
        out_shape=jax.ShapeDtypeStruct((l, d), bf16),
        compiler_params=_params(("parallel",), 32),
    )(h, w)


def _rms_bwd(h, w, dhn, dres, name):
    l, d = h.shape
    tr = _tile(l, 512)

    def body(h_ref, w_ref, g_ref, r_ref, dh_ref, dhb_ref, dw_ref):
        x = h_ref[...]
        r = lax.rsqrt(jnp.mean(x * x, axis=-1, keepdims=True) + RMS_EPS)
        xh = x * r
        g = g_ref[...]
        dxh = g * w_ref[...]
        dx = r * (dxh - xh * jnp.mean(dxh * xh, axis=-1, keepdims=True))
        dh = r_ref[...] + dx
        dh_ref[...] = dh
        dhb_ref[...] = _b(dh)
        dw = jnp.sum(g * xh, axis=0, keepdims=True)

        @pl.when(pl.program_id(0) == 0)
        def _():
            dw_ref[...] = dw

        @pl.when(pl.program_id(0) > 0)
        def _():
            dw_ref[...] += dw

    row = pl.BlockSpec((tr, d), lambda i: (i, 0))
    vec = pl.BlockSpec((1, d), lambda i: (0, 0))
    return pl.pallas_call(
        body, name=name, grid=(l // tr,), in_specs=[row, vec, row, row], out_specs=[row, row, vec],
        out_shape=[jax.ShapeDtypeStruct((l, d), f32), jax.ShapeDtypeStruct((l, d), bf16),
                   jax.ShapeDtypeStruct((1, d), f32)],
        compiler_params=_params(("arbitrary",), 40),
    )(h, w, dhn, dres)


def _final_loss(h, w, target, name):
    l, d = h.shape
    nch = l // CHUNK
    cpt = _tile(l, 256) // CHUNK
    nt = nch // cpt

    def body(h_ref, w_ref, *rest):
        t_refs, (dh_ref, dhb_ref, dw_ref, loss_ref) = rest[:cpt], rest[cpt:]
        i = pl.program_id(0)
        wv = w_ref[...]
        dw = jnp.zeros((1, d), f32)
        part = jnp.zeros((1, 1), f32)
        for c in range(cpt):
            rows = slice(c * CHUNK, (c + 1) * CHUNK)
            live = (i * cpt + c > 0).astype(f32)
            x = h_ref[rows, :]
            r = lax.rsqrt(jnp.mean(x * x, axis=-1, keepdims=True) + RMS_EPS)
            xh = x * r
            err = (xh * wv - t_refs[c][...]) * live
            dy = err * (1.0 / d)
            dxh = dy * wv
            dx = r * (dxh - xh * jnp.mean(dxh * xh, axis=-1, keepdims=True))
            dh_ref[rows, :] = dx
            dhb_ref[rows, :] = _b(dx)
            dw = dw + jnp.sum(dy * xh, axis=0, keepdims=True)
            part = part + 0.5 * jnp.sum(jnp.sum(err * err, axis=-1, keepdims=True) * (1.0 / d), axis=0, keepdims=True)
        part = jnp.broadcast_to(part, (1, LANES))

        @pl.when(i == 0)
        def _():
            dw_ref[...] = dw
            loss_ref[...] = part

        @pl.when(i > 0)
        def _():
            dw_ref[...] += dw
            loss_ref[...] += part

    row = pl.BlockSpec((cpt * CHUNK, d), lambda i: (i, 0))
    vec = pl.BlockSpec((1, d), lambda i: (0, 0))
    t_specs = [pl.BlockSpec((CHUNK, d), functools.partial(lambda i, c: (jnp.maximum(i * cpt + c - 1, 0), 0), c=c))
               for c in range(cpt)]
    return pl.pallas_call(
        body, name=name, grid=(nt,),
        in_specs=[row, vec] + t_specs,
        out_specs=[row, row, vec, pl.BlockSpec((1, LANES), lambda i: (0, 0))],
        out_shape=[jax.ShapeDtypeStruct((l, d), f32), jax.ShapeDtypeStruct((l, d), bf16),
                   jax.ShapeDtypeStruct((1, d), f32), jax.ShapeDtypeStruct((1, LANES), f32)],
        compiler_params=_params(("arbitrary",), 32),
    )(h, w, *([target] * cpt))


def _ffn_gu(hn, wg, wu, name, *, tm, tn, comm=None):
    l, d = hn.shape
    fh = wg.shape[1]

    def body(h_ref, g_ref, u_ref, a_ref, b_ref, s_ref):
        hb = h_ref[...]
        a = _nn(hb, g_ref[...])
        bb = _nn(hb, u_ref[...])
        a_ref[...] = _b(a)
        b_ref[...] = _b(bb)
        s_ref[...] = _b(_silu(a) * bb)

    wspec = pl.BlockSpec((d, tn), lambda i, j: (0, j))
    ospec = pl.BlockSpec((tm, tn), lambda i, j: (i, j))
    return _call(body, (hn, wg, wu), name=name, grid=(l // tm, fh // tn),
                 in_specs=[pl.BlockSpec((tm, d), lambda i, j: (i, 0)), wspec, wspec], out_specs=[ospec] * 3,
                 out_shape=[jax.ShapeDtypeStruct((l, fh), bf16)] * 3, sem=("parallel", "parallel"), vmem_mb=48,
                 comm=comm)


def _ffn_ds(dhb, wd, a, b, *, tm, tn, name):
    l, d = dhb.shape
    fh = wd.shape[0]

    def body(g_ref, w_ref, a_ref, b_ref, da_ref, db_ref):
        ds = _nt(g_ref[...], w_ref[...])
        a = a_ref[...].astype(f32)
        da_ref[...] = _b(ds * b_ref[...].astype(f32) * _dsilu(a))
        db_ref[...] = _b(ds * _silu(a))

    ospec = pl.BlockSpec((tm, tn), lambda i, j: (i, j))
    return pl.pallas_call(
        body, name=name, grid=(l // tm, fh // tn),
        in_specs=[pl.BlockSpec((tm, d), lambda i, j: (i, 0)), pl.BlockSpec((tn, d), lambda i, j: (j, 0)), ospec, ospec],
        out_specs=[ospec, ospec], out_shape=[jax.ShapeDtypeStruct((l, fh), bf16)] * 2,
        compiler_params=_params(("parallel", "parallel"), 48),
    )(dhb, wd, a, b)


def _gnorm_fwd(o, proj, nw, heads, dv, gate_blk, name):
    l, hv = o.shape
    tr = _tile(l, 256)

    def body(o_ref, g_ref, w_ref, y_ref):
        wv = w_ref[...]
        for h in range(heads):
            sl = slice(h * dv, (h + 1) * dv)
            oh = o_ref[:, sl]
            r = lax.rsqrt(jnp.mean(oh * oh, axis=-1, keepdims=True) + RMS_EPS)
            y_ref[:, sl] = _b(oh * r * wv * _silu(g_ref[:, sl]))

    return pl.pallas_call(
        body, name=name, grid=(l // tr,),
        in_specs=[pl.BlockSpec((tr, hv), lambda i: (i, 0)), pl.BlockSpec((tr, hv), lambda i: (i, gate_blk)),
                  pl.BlockSpec((1, dv), lambda i: (0, 0))],
        out_specs=pl.BlockSpec((tr, hv), lambda i: (i, 0)),
        out_shape=jax.ShapeDtypeStruct((l, hv), bf16),
        compiler_params=_params(("parallel",), 32),
    )(o, proj, nw)


def _gnorm_bwd(o, proj, nw, dy, heads, dv, gate_blk, name):
    l, hv = o.shape
    tr = _tile(l, 256)

    def body(o_ref, g_ref, w_ref, dy_ref, do_ref, dg_ref, dw_ref):
        wv = w_ref[...]
        dw = jnp.zeros((1, dv), f32)
        for h in range(heads):
            sl = slice(h * dv, (h + 1) * dv)
            oh = o_ref[:, sl]
            g = g_ref[:, sl]
            dyh = dy_ref[:, sl]
            r = lax.rsqrt(jnp.mean(oh * oh, axis=-1, keepdims=True) + RMS_EPS)
            xh = oh * r
            dn = dyh * _silu(g)
            dg_ref[:, sl] = _b(dyh * (xh * wv) * _dsilu(g))
            dxh = dn * wv
            do_ref[:, sl] = r * (dxh - xh * jnp.mean(dxh * xh, axis=-1, keepdims=True))
            dw = dw + jnp.sum(dn * xh, axis=0, keepdims=True)

        @pl.when(pl.program_id(0) == 0)
        def _():
            dw_ref[...] = dw

        @pl.when(pl.program_id(0) > 0)
        def _():
            dw_ref[...] += dw

    row = pl.BlockSpec((tr, hv), lambda i: (i, 0))
    gate = pl.BlockSpec((tr, hv), lambda i: (i, gate_blk))
    vec = pl.BlockSpec((1, dv), lambda i: (0, 0))
    return pl.pallas_call(
        body, name=name, grid=(l // tr,),
        in_specs=[row, gate, vec, row],
        out_specs=[row, gate, vec],
        out_shape=[jax.ShapeDtypeStruct((l, hv), f32), jax.ShapeDtypeStruct(proj.shape, bf16),
                   jax.ShapeDtypeStruct((1, dv), f32)],
        compiler_params=_params(("arbitrary",), 40),
    )(o, proj, nw, dy)


def _ret_prep(proj, cos, sin, name):
    l = proj.shape[0]
    tr = _tile(l, 256)
    half = RET_DK // 2
    scale = RET_DK ** -0.5

    def body(p_ref, c_ref, s_ref, o_ref):
        rows = pl.program_id(0) * tr + lax.broadcasted_iota(jnp.int32, (tr, 1), 0)
        kmul = jnp.where(rows >= PAD, scale, 0.0).astype(f32)
        c, s = c_ref[...], s_ref[...]
        for j in range(2 * RET_HEADS):
            t1 = p_ref[:, j * RET_DK: j * RET_DK + half]
            t2 = p_ref[:, j * RET_DK + half: (j + 1) * RET_DK]
            o1 = t1 * c - t2 * s
            o2 = t1 * s + t2 * c
            if j >= RET_HEADS:
                o1, o2 = o1 * kmul, o2 * kmul
            o_ref[:, j * RET_DK: j * RET_DK + half] = o1
            o_ref[:, j * RET_DK + half: (j + 1) * RET_DK] = o2

    wide = pl.BlockSpec((tr, 2 * RET_QK), lambda i: (i, 0))
    tab = pl.BlockSpec((tr, half), lambda i: (i, 0))
    return pl.pallas_call(
        body, name=name, grid=(l // tr,), in_specs=[wide, tab, tab], out_specs=wide,
        out_shape=jax.ShapeDtypeStruct((l, 2 * RET_QK), f32),
        compiler_params=_params(("parallel",), 32),
    )(proj, cos, sin)


def _ret_prep_bwd(dq, dk, cos, sin, dproj, name):
    l = dq.shape[0]
    tr = _tile(l, 256)
    half = RET_DK // 2
    scale = RET_DK ** -0.5

    def body(dq_ref, dk_ref, c_ref, s_ref, _, o_ref):
        rows = pl.program_id(0) * tr + lax.broadcasted_iota(jnp.int32, (tr, 1), 0)
        kmul = jnp.where(rows >= PAD, scale, 0.0).astype(f32)
        c, s = c_ref[...], s_ref[...]
        for j in range(2 * RET_HEADS):
            d_ref = dq_ref if j < RET_HEADS else dk_ref
            jj = j % RET_HEADS
            d1 = d_ref[:, jj * RET_DK: jj * RET_DK + half]
            d2 = d_ref[:, jj * RET_DK + half: (jj + 1) * RET_DK]
            if j >= RET_HEADS:
                d1, d2 = d1 * kmul, d2 * kmul
            o_ref[:, j * RET_DK: j * RET_DK + half] = _b(d1 * c + d2 * s)
            o_ref[:, j * RET_DK + half: (j + 1) * RET_DK] = _b(d2 * c - d1 * s)

    nar = pl.BlockSpec((tr, RET_QK), lambda i: (i, 0))
    wide = pl.BlockSpec((tr, 2 * RET_QK), lambda i: (i, 0))
    tab = pl.BlockSpec((tr, half), lambda i: (i, 0))
    return pl.pallas_call(
        body, name=name, grid=(l // tr,), in_specs=[nar, nar, tab, tab, pl.BlockSpec(memory_space=pl.ANY)],
        out_specs=wide, out_shape=jax.ShapeDtypeStruct(dproj.shape, dproj.dtype), input_output_aliases={4: 0},
        compiler_params=_params(("parallel",), 32),
    )(dq, dk, cos, sin, dproj)


RET_BLOCK_CHUNKS = 3


def _ret_block(l):
    nch = l // CHUNK
    return RET_BLOCK_CHUNKS * CHUNK if nch % RET_BLOCK_CHUNKS == 0 else CHUNK


def _ret_decay(lg, rb):
    idx = lax.broadcasted_iota(jnp.int32, (rb, 1), 0).astype(f32)
    ri = lax.broadcasted_iota(jnp.int32, (rb, rb), 0)
    ci = lax.broadcasted_iota(jnp.int32, (rb, rb), 1)
    rel = (ri - ci).astype(f32)
    dmask = jnp.where(ri >= ci, jnp.exp(lg * jnp.maximum(rel, 0.0)), 0.0)
    xi = jnp.exp(lg * (idx + 1.0))
    zeta = jnp.exp(lg * (rb - 1.0 - idx))
    return dmask, xi, zeta


def _ret_scan_fwd(qk, proj, lgs, gcs, name, comm=None):
    l = qk.shape[0]
    rb = _ret_block(l)
    nb = l // rb

    def body(lg_ref, gc_ref, q_ref, k_ref, v_ref, o_ref, st_ref, s_ref):
        @pl.when(pl.program_id(0) == 0)
        def _():
            s_ref[...] = jnp.zeros_like(s_ref)

        for h in range(RET_HEADS):
            dmask, xi, zeta = _ret_decay(lg_ref[h], rb)
            q = q_ref[:, h * RET_DK:(h + 1) * RET_DK]
            k = k_ref[:, h * RET_DK:(h + 1) * RET_DK]
            vb = _b(v_ref[:, h * RET_DV:(h + 1) * RET_DV])
            s = s_ref[h]
            sb = _b(s)
            st_ref[0, h] = sb
            scores = _nt(_b(q), _b(k)) * dmask
            o_ref[:, h * RET_DV:(h + 1) * RET_DV] = _nn(_b(scores), vb) + _nn(_b(q * xi), sb)
            s_ref[h] = gc_ref[h] * s + _tn(_b(k * zeta), vb)

    smem = pl.BlockSpec(memory_space=pltpu.SMEM)
    return _call(
        body, (lgs, gcs, qk, qk, proj), name=name, grid=(nb,),
        in_specs=[smem, smem,
                  pl.BlockSpec((rb, RET_QK), lambda n: (n, 0)),
                  pl.BlockSpec((rb, RET_QK), lambda n: (n, 1)),
                  pl.BlockSpec((rb, RET_V), lambda n: (n, 1))],
        out_specs=[pl.BlockSpec((rb, RET_V), lambda n: (n, 0)),
                   pl.BlockSpec((1, RET_HEADS, RET_DK, RET_DV), lambda n: (n, 0, 0, 0))],
        out_shape=[jax.ShapeDtypeStruct((l, RET_V), f32),
                   jax.ShapeDtypeStruct((nb, RET_HEADS, RET_DK, RET_DV), bf16)],
        scratch=[pltpu.VMEM((RET_HEADS, RET_DK, RET_DV), f32)], sem=("arbitrary",), vmem_mb=40, comm=comm)


def _ret_scan_bwd(qk, proj, states, do, dproj, lgs, gcs, name, comm=None):
    l = qk.shape[0]
    rb = _ret_block(l)
    nb = l // rb

    def body(lg_ref, gc_ref, q_ref, k_ref, v_ref, st_ref, do_ref, _, dq_ref, dk_ref, dv_ref, ds_ref):
        @pl.when(pl.program_id(0) == 0)
        def _():
            ds_ref[...] = jnp.zeros_like(ds_ref)

        for h in range(RET_HEADS):
            dmask, xi, zeta = _ret_decay(lg_ref[h], rb)
            q = q_ref[:, h * RET_DK:(h + 1) * RET_DK]
            k = k_ref[:, h * RET_DK:(h + 1) * RET_DK]
            qb, kb = _b(q), _b(k)
            vb = _b(v_ref[:, h * RET_DV:(h + 1) * RET_DV])
            dob = _b(do_ref[:, h * RET_DV:(h + 1) * RET_DV])
            sb = st_ref[0, h]
            dsp = ds_ref[h]
            dspb = _b(dsp)
            scores = _nt(qb, kb) * dmask
            dscores = _b(_nt(dob, vb) * dmask)
            dq_ref[:, h * RET_DK:(h + 1) * RET_DK] = _nn(dscores, kb) + _nt(dob, sb) * xi
            dk_ref[:, h * RET_DK:(h + 1) * RET_DK] = _tn(dscores, qb) + _nt(vb, dspb) * zeta
            dv_ref[:, h * RET_DV:(h + 1) * RET_DV] = _b(_tn(_b(scores), dob) + _nn(_b(k * zeta), dspb))
            ds_ref[h] = gc_ref[h] * dsp + _tn(_b(q * xi), dob)

    smem = pl.BlockSpec(memory_space=pltpu.SMEM)
    rev = lambda n: nb - 1 - n
    return _call(
        body, (lgs, gcs, qk, qk, proj, states, do, dproj), name=name, grid=(nb,),
        in_specs=[smem, smem,
                  pl.BlockSpec((rb, RET_QK), lambda n: (rev(n), 0)),
                  pl.BlockSpec((rb, RET_QK), lambda n: (rev(n), 1)),
                  pl.BlockSpec((rb, RET_V), lambda n: (rev(n), 1)),
                  pl.BlockSpec((1, RET_HEADS, RET_DK, RET_DV), lambda n: (rev(n), 0, 0, 0)),
                  pl.BlockSpec((rb, RET_V), lambda n: (rev(n), 0)),
                  pl.BlockSpec(memory_space=pl.ANY)],
        out_specs=[pl.BlockSpec((rb, RET_QK), lambda n: (rev(n), 0)),
                   pl.BlockSpec((rb, RET_QK), lambda n: (rev(n), 0)),
                   pl.BlockSpec((rb, RET_V), lambda n: (rev(n), 1))],
        out_shape=[jax.ShapeDtypeStruct((l, RET_QK), f32), jax.ShapeDtypeStruct((l, RET_QK), f32),
                   jax.ShapeDtypeStruct(dproj.shape, dproj.dtype)],
        scratch=[pltpu.VMEM((RET_HEADS, RET_DK, RET_DV), f32)], sem=("arbitrary",), vmem_mb=40, comm=comm,
        aliases={7: 2})


CONV_BLK = 1024
HALO = 8


def _slab_rows(r):
    return pl.ds(pl.multiple_of(r * HALO, HALO), HALO)


def _conv_slab(x_ref, p_ref, r, i, tr):
    cur = x_ref[_slab_rows(r), :]
    prev = jnp.where(r > 0, x_ref[_slab_rows(jnp.maximum(r - 1, 0)), :], p_ref[...])
    row0 = i * tr + r * HALO
    cur = jnp.where(row0 >= PAD, cur, 0.0)
    prev = jnp.where(row0 - HALO >= PAD, prev, 0.0)
    lrow = lax.broadcasted_iota(jnp.int32, (HALO, 1), 0)
    shifted = [jnp.where(lrow < s, pltpu.roll(prev, s, 0), pltpu.roll(cur, s, 0)) for s in range(1, CONV_K)]
    return [cur] + shifted


def _conv_of(xs, w):
    acc = xs[0] * w[CONV_K - 1:CONV_K, :]
    for s in range(1, CONV_K):
        acc = acc + xs[s] * w[CONV_K - 1 - s:CONV_K - s, :]
    return acc


def _slab_loop(n_slabs, fn, init=None):
    return lax.fori_loop(0, n_slabs, fn, init, unroll=8)


def _dn_conv_fwd(proj, conv_w, name, comm=None):
    l = proj.shape[0]
    tr = _tile(l, 256)
    nblk = DN_CONV_CH // CONV_BLK
    heads = CONV_BLK // DN_DK

    def body(x_ref, p_ref, w_ref, o_ref):
        i, j = pl.program_id(0), pl.program_id(1)
        w = w_ref[...]

        def act(r):
            return _silu(_conv_of(_conv_slab(x_ref, p_ref, r, i, tr), w))

        def normed(scale):
            def slab(r, carry):
                a = act(r)
                outs = []
                for h in range(heads):
                    ah = a[:, h * DN_DK:(h + 1) * DN_DK]
                    outs.append(ah * (lax.rsqrt(jnp.sum(ah * ah, axis=-1, keepdims=True) + RMS_EPS) * scale))
                o_ref[_slab_rows(r), :] = jnp.concatenate(outs, axis=1)
                return carry
            return slab

        def plain(r, carry):
            o_ref[_slab_rows(r), :] = act(r)
            return carry

        @pl.when(j == 0)
        def _():
            _slab_loop(tr // HALO, normed(DN_DK ** -0.5))

        @pl.when(j == 1)
        def _():
            _slab_loop(tr // HALO, normed(1.0))

        @pl.when(j >= 2)
        def _():
            _slab_loop(tr // HALO, plain)

    hb = tr // HALO
    return _call(
        body, (proj, proj, conv_w), name=name, grid=(l // tr, nblk),
        in_specs=[pl.BlockSpec((tr, CONV_BLK), lambda i, j: (i, j)),
                  pl.BlockSpec((HALO, CONV_BLK), lambda i, j: (jnp.maximum(i * hb - 1, 0), j)),
                  pl.BlockSpec((CONV_K, CONV_BLK), lambda i, j: (0, j))],
        out_specs=[pl.BlockSpec((tr, CONV_BLK), lambda i, j: (i, j))],
        out_shape=[jax.ShapeDtypeStruct((l, DN_CONV_CH), f32)],
        scratch=[], sem=("parallel", "parallel"), vmem_mb=32, comm=comm)


def _dn_conv_bwd_a(proj, conv_w, dqkv, name, comm=None):
    l = proj.shape[0]
    tr = _tile(l, 256)
    nblk = DN_CONV_CH // CONV_BLK
    heads = CONV_BLK // DN_DK

    def body(x_ref, p_ref, w_ref, d_ref, dc_ref, dw_ref, acc_ref):
        j, i = pl.program_id(0), pl.program_id(1)
        w = w_ref[...]
        acc_ref[...] = jnp.zeros_like(acc_ref)

        def slab_of(l2_scale):
            def slab(r, carry):
                xs = _conv_slab(x_ref, p_ref, r, i, tr)
                c = _conv_of(xs, w)
                a = _silu(c)
                dy = d_ref[_slab_rows(r), :]
                if l2_scale is None:
                    da = dy
                else:
                    parts = []
                    for h in range(heads):
                        sl = slice(h * DN_DK, (h + 1) * DN_DK)
                        ah, dyh = a[:, sl], dy[:, sl]
                        rn = lax.rsqrt(jnp.sum(ah * ah, axis=-1, keepdims=True) + RMS_EPS)
                        yh = ah * rn
                        parts.append((rn * l2_scale) * (dyh - yh * jnp.sum(dyh * yh, axis=-1, keepdims=True)))
                    da = jnp.concatenate(parts, axis=1)
                dc = da * _dsilu(c)
                dc_ref[_slab_rows(r), :] = dc
                for s in range(CONV_K):
                    acc_ref[CONV_K - 1 - s] += dc * xs[s]
                return carry
            return slab

        @pl.when(j == 0)
        def _():
            _slab_loop(tr // HALO, slab_of(DN_DK ** -0.5))

        @pl.when(j == 1)
        def _():
            _slab_loop(tr // HALO, slab_of(1.0))

        @pl.when(j >= 2)
        def _():
            _slab_loop(tr // HALO, slab_of(None))

        ksel = lax.broadcasted_iota(jnp.int32, (CONV_K, 1), 0)
        dw = jnp.zeros((CONV_K, CONV_BLK), f32)
        for k in range(CONV_K):
            dw = dw + jnp.where(ksel == k, jnp.sum(acc_ref[k], axis=0, keepdims=True), 0.0)

        @pl.when(i == 0)
        def _():
            dw_ref[...] = dw

        @pl.when(i > 0)
        def _():
            dw_ref[...] += dw

    hb = tr // HALO
    blk = pl.BlockSpec((tr, CONV_BLK), lambda j, i: (i, j))
    return _call(
        body, (proj, proj, conv_w, dqkv), name=name, grid=(nblk, l // tr),
        in_specs=[blk, pl.BlockSpec((HALO, CONV_BLK), lambda j, i: (jnp.maximum(i * hb - 1, 0), j)),
                  pl.BlockSpec((CONV_K, CONV_BLK), lambda j, i: (0, j)), blk],
        out_specs=[blk, pl.BlockSpec((CONV_K, CONV_BLK), lambda j, i: (0, j))],
        out_shape=[jax.ShapeDtypeStruct((l, DN_CONV_CH), f32), jax.ShapeDtypeStruct((CONV_K, DN_CONV_CH), f32)],
        scratch=[pltpu.VMEM((CONV_K, HALO, CONV_BLK), f32)], sem=("parallel", "arbitrary"), vmem_mb=40, comm=comm)


def _dn_conv_bwd_b(dc, conv_w, dproj, name):
    l = dc.shape[0]
    tr = _tile(l, 256)
    nblk = DN_CONV_CH // CONV_BLK
    nrow = l // tr

    n_slabs = tr // HALO
    pair = 2 * HALO

    def body(d_ref, n_ref, w_ref, _, o_ref):
        i = pl.program_id(0)
        w = w_ref[...]
        nxt_tile = jnp.where(i < nrow - 1, n_ref[...], 0.0)
        lrow = lax.broadcasted_iota(jnp.int32, (HALO, 1), 0)

        def one(r):
            cur = d_ref[_slab_rows(r), :]
            nxt = jnp.where(r < n_slabs - 1, d_ref[_slab_rows(jnp.minimum(r + 1, n_slabs - 1)), :], nxt_tile)
            acc = cur * w[CONV_K - 1:CONV_K, :]
            for s in range(1, CONV_K):
                up = jnp.where(lrow >= HALO - s, pltpu.roll(nxt, HALO - s, 0), pltpu.roll(cur, HALO - s, 0))
                acc = acc + up * w[CONV_K - 1 - s:CONV_K - s, :]
            return jnp.where(i * tr + r * HALO >= PAD, acc, 0.0)

        def two(q, carry):
            rows = pl.ds(pl.multiple_of(q * pair, pair), pair)
            o_ref[rows, :] = _b(jnp.concatenate([one(2 * q), one(2 * q + 1)], axis=0))
            return carry

        lax.fori_loop(0, n_slabs // 2, two, None)

    hb = tr // HALO
    nh = l // HALO
    return pl.pallas_call(
        body, name=name, grid=(nrow, nblk),
        in_specs=[pl.BlockSpec((tr, CONV_BLK), lambda i, j: (i, j)),
                  pl.BlockSpec((HALO, CONV_BLK), lambda i, j: (jnp.minimum((i + 1) * hb, nh - 1), j)),
                  pl.BlockSpec((CONV_K, CONV_BLK), lambda i, j: (0, j)),
                  pl.BlockSpec(memory_space=pl.ANY)],
        out_specs=pl.BlockSpec((tr, CONV_BLK), lambda i, j: (i, j)),
        out_shape=jax.ShapeDtypeStruct(dproj.shape, dproj.dtype), input_output_aliases={3: 0},
        compiler_params=_params(("parallel", "parallel"), 32),
    )(dc, dc, conv_w, dproj)


BA_W = LANES


def _dn_gates(ba_ref, al_ref, dt_ref, n):
    rows = n * CHUNK + lax.broadcasted_iota(jnp.int32, (CHUNK, 1), 0)
    vm = (rows >= PAD).astype(f32)
    bin_ = ba_ref[:, 0:DN_HEADS]
    z = ba_ref[:, DN_HEADS:2 * DN_HEADS] + dt_ref[...]
    sp = jnp.maximum(z, 0.0) + jnp.log1p(jnp.exp(-jnp.abs(z)))
    ea = jnp.exp(al_ref[...])
    beta = _sigmoid(bin_) * vm
    g = -ea * sp * vm
    return vm, bin_, z, ea, beta, g


def _tri():
    ri = lax.broadcasted_iota(jnp.int32, (CHUNK, CHUNK), 0)
    ci = lax.broadcasted_iota(jnp.int32, (CHUNK, CHUNK), 1)
    return ri, ci


def _split(a):
    hi = _b(a)
    return hi, _b(a - hi.astype(f32))


def _mm3(a, b, dot=_nn):
    (ah, al), (bh, bl) = _split(a), _split(b)
    return dot(ah, bh) + (dot(ah, bl) + dot(al, bh))


def _cumsum_rows(tri, g):
    tb = _b(tri)
    g1 = _b(g)
    r1 = g - g1.astype(f32)
    g2 = _b(r1)
    g3 = _b(r1 - g2.astype(f32))
    return _nn(tb, g1) + (_nn(tb, g2) + _nn(tb, g3))


DN_SCAN_CHUNKS = 3


def _scan_chunks(nch):
    return DN_SCAN_CHUNKS if nch % DN_SCAN_CHUNKS == 0 else 1


def _dn_prep(qkv, ba, a_log, dt_bias, name):
    l = qkv.shape[0]
    nch = l // CHUNK
    heads = range(DN_HEADS)

    def body(q_ref, k_ref, v_ref, ba_ref, al_ref, dt_ref, t_ref, u_ref, wq_ref, pk_ref, eg_ref, kpt_ref, qwt_ref):
        n = pl.program_id(0)
        _, _, _, _, beta, g = _dn_gates(ba_ref, al_ref, dt_ref, n)
        ri, ci = _tri()
        incl, strict = ri >= ci, ri > ci
        eye = (ri == ci).astype(f32)
        gam = _cumsum_rows(incl.astype(f32), g)
        gam_t = gam.T
        gc = [gam[:, h:h + 1] for h in heads]
        bh = [beta[:, h:h + 1] for h in heads]
        kh = [k_ref[:, h * DN_DK:(h + 1) * DN_DK] for h in heads]
        kb = [_b(k) for k in kh]
        decay = [jnp.exp(jnp.where(incl, gc[h] - gam_t[h:h + 1, :], -jnp.inf)) for h in heads]
        a = [jnp.where(strict, bh[h] * _nt(kb[h], kb[h]) * decay[h], 0.0) for h in heads]
        t = [eye - a[h] for h in heads]
        p = a
        for _ in range(int(math.log2(CHUNK)) - 1):
            p = [_mm3(p[h], p[h]) for h in heads]
            t = [t[h] + _mm3(t[h], p[h]) for h in heads]
        eg = [jnp.exp(gc[h]) for h in heads]
        for h in heads:
            t_ref[0, h] = t[h]
            u_ref[:, h * DN_DV:(h + 1) * DN_DV] = _mm3(t[h], v_ref[:, h * DN_DV:(h + 1) * DN_DV] * bh[h])
            w = _mm3(t[h], kh[h] * (bh[h] * eg[h]))
            wq_ref[0, h, 0:CHUNK, :] = _b(w)
            qwt_ref[0, h, DN_DK:2 * DN_DK, :] = _b(w.T)
        for h in heads:
            qh = q_ref[:, h * DN_DK:(h + 1) * DN_DK]
            gl = gc[h][CHUNK - 1:CHUNK, :]
            qe = qh * eg[h]
            ke = kh[h] * jnp.exp(gl - gc[h])
            pmat = _nt(_b(qh), kb[h]) * decay[h]
            wq_ref[0, h, CHUNK:2 * CHUNK, :] = _b(qe)
            qwt_ref[0, h, 0:DN_DK, :] = _b(qe.T)
            pk_ref[0, h, 0:CHUNK, :] = _b(pmat)
            pk_ref[0, h, CHUNK:CHUNK + DN_DK, :] = _b(ke.T)
            kpt_ref[0, h, :, 0:DN_DK] = _b(ke)
            kpt_ref[0, h, :, DN_DK:DN_DK + CHUNK] = _b(pmat.T)
            eg_ref[0, h] = jnp.broadcast_to(jnp.exp(gl), (8, LANES))

    vec = pl.BlockSpec((1, DN_HEADS), lambda n: (0, 0))
    return pl.pallas_call(
        body, name=name, grid=(nch,),
        in_specs=[pl.BlockSpec((CHUNK, DN_QK), lambda n: (n, 0)), pl.BlockSpec((CHUNK, DN_QK), lambda n: (n, 1)),
                  pl.BlockSpec((CHUNK, DN_V), lambda n: (n, 1)), pl.BlockSpec((CHUNK, BA_W), lambda n: (n, 0)),
                  vec, vec],
        out_specs=[pl.BlockSpec((1, DN_HEADS, CHUNK, CHUNK), lambda n: (n, 0, 0, 0)),
                   pl.BlockSpec((CHUNK, DN_V), lambda n: (n, 0)),
                   pl.BlockSpec((1, DN_HEADS, 2 * CHUNK, DN_DK), lambda n: (n, 0, 0, 0)),
                   pl.BlockSpec((1, DN_HEADS, CHUNK + DN_DK, CHUNK), lambda n: (n, 0, 0, 0)),
                   pl.BlockSpec((1, DN_HEADS, 8, LANES), lambda n: (n, 0, 0, 0)),
                   pl.BlockSpec((1, DN_HEADS, CHUNK, DN_DK + CHUNK), lambda n: (n, 0, 0, 0)),
                   pl.BlockSpec((1, DN_HEADS, 2 * DN_DK, CHUNK), lambda n: (n, 0, 0, 0))],
        out_shape=[jax.ShapeDtypeStruct((nch, DN_HEADS, CHUNK, CHUNK), f32),
                   jax.ShapeDtypeStruct((l, DN_V), f32),
                   jax.ShapeDtypeStruct((nch, DN_HEADS, 2 * CHUNK, DN_DK), bf16),
                   jax.ShapeDtypeStruct((nch, DN_HEADS, CHUNK + DN_DK, CHUNK), bf16),
                   jax.ShapeDtypeStruct((nch, DN_HEADS, 8, LANES), f32),
                   jax.ShapeDtypeStruct((nch, DN_HEADS, CHUNK, DN_DK + CHUNK), bf16),
                   jax.ShapeDtypeStruct((nch, DN_HEADS, 2 * DN_DK, CHUNK), bf16)],
        compiler_params=_params(("parallel",), 40),
    )(qkv, qkv, qkv, ba, a_log, dt_bias)


def _dn_scan_fwd(u, wq, pk, egl, name):
    l = u.shape[0]
    nch = l // CHUNK
    cs = _scan_chunks(nch)

    def body(u_ref, wq_ref, pk_ref, eg_ref, o_ref, st_ref, vn_ref, s_ref):
        @pl.when(pl.program_id(0) == 0)
        def _():
            s_ref[...] = jnp.zeros_like(s_ref)

        for c in range(cs):
            rows = slice(c * CHUNK, (c + 1) * CHUNK)
            for h in range(DN_HEADS):
                cols = slice(h * DN_DV, (h + 1) * DN_DV)
                s = s_ref[h]
                sb = _b(s)
                st_ref[c, h] = sb
                x = _nn(wq_ref[c, h], sb)
                vnb = _b(u_ref[rows, cols] - x[0:CHUNK])
                vn_ref[rows, cols] = vnb
                y = _nn(pk_ref[c, h], vnb)
                o_ref[rows, cols] = x[CHUNK:2 * CHUNK] + y[0:CHUNK]
                s_ref[h] = eg_ref[c, h][0:1, 0:1] * s + y[CHUNK:CHUNK + DN_DK]

    return pl.pallas_call(
        body, name=name, grid=(nch // cs,),
        in_specs=[pl.BlockSpec((cs * CHUNK, DN_V), lambda n: (n, 0)),
                  pl.BlockSpec((cs, DN_HEADS, 2 * CHUNK, DN_DK), lambda n: (n, 0, 0, 0)),
                  pl.BlockSpec((cs, DN_HEADS, CHUNK + DN_DK, CHUNK), lambda n: (n, 0, 0, 0)),
                  pl.BlockSpec((cs, DN_HEADS, 8, LANES), lambda n: (n, 0, 0, 0))],
        out_specs=[pl.BlockSpec((cs * CHUNK, DN_V), lambda n: (n, 0)),
                   pl.BlockSpec((cs, DN_HEADS, DN_DK, DN_DV), lambda n: (n, 0, 0, 0)),
                   pl.BlockSpec((cs * CHUNK, DN_V), lambda n: (n, 0))],
        out_shape=[jax.ShapeDtypeStruct((l, DN_V), f32),
                   jax.ShapeDtypeStruct((nch, DN_HEADS, DN_DK, DN_DV), bf16),
                   jax.ShapeDtypeStruct((l, DN_V), bf16)],
        scratch_shapes=[pltpu.VMEM((DN_HEADS, DN_DK, DN_DV), f32)],
        compiler_params=_params(("arbitrary",), 40),
    )(u, wq, pk, egl)


def _dn_scan_bwd(do, kpt, qwt, egl, name):
    l = do.shape[0]
    nch = l // CHUNK
    cs = _scan_chunks(nch)
    nblk = nch // cs

    def body(do_ref, kpt_ref, qwt_ref, eg_ref, dvn_ref, dsp_ref, ds_ref):
        @pl.when(pl.program_id(0) == 0)
        def _():
            ds_ref[...] = jnp.zeros_like(ds_ref)

        for c in reversed(range(cs)):
            rows = slice(c * CHUNK, (c + 1) * CHUNK)
            for h in range(DN_HEADS):
                cols = slice(h * DN_DV, (h + 1) * DN_DV)
                dsp = ds_ref[h]
                dspb = _b(dsp)
                dsp_ref[c, h] = dspb
                dob = _b(do_ref[rows, cols])
                kpt_h = kpt_ref[c, h]
                dvn = _nn(kpt_h[:, 0:DN_DK], dspb) + _nn(kpt_h[:, DN_DK:DN_DK + CHUNK], dob)
                dvn_ref[rows, cols] = dvn
                qwt_h = qwt_ref[c, h]
                ds_ref[h] = (eg_ref[c, h][0:1, 0:1] * dsp + _nn(qwt_h[0:DN_DK], dob)
                             - _nn(qwt_h[DN_DK:2 * DN_DK], _b(dvn)))

    rev = lambda s: nblk - 1 - s
    return pl.pallas_call(
        body, name=name, grid=(nblk,),
        in_specs=[pl.BlockSpec((cs * CHUNK, DN_V), lambda s: (rev(s), 0)),
                  pl.BlockSpec((cs, DN_HEADS, CHUNK, DN_DK + CHUNK), lambda s: (rev(s), 0, 0, 0)),
                  pl.BlockSpec((cs, DN_HEADS, 2 * DN_DK, CHUNK), lambda s: (rev(s), 0, 0, 0)),
                  pl.BlockSpec((cs, DN_HEADS, 8, LANES), lambda s: (rev(s), 0, 0, 0))],
        out_specs=[pl.BlockSpec((cs * CHUNK, DN_V), lambda s: (rev(s), 0)),
                   pl.BlockSpec((cs, DN_HEADS, DN_DK, DN_DV), lambda s: (rev(s), 0, 0, 0))],
        out_shape=[jax.ShapeDtypeStruct((l, DN_V), f32),
                   jax.ShapeDtypeStruct((nch, DN_HEADS, DN_DK, DN_DV), bf16)],
        scratch_shapes=[pltpu.VMEM((DN_HEADS, DN_DK, DN_DV), f32)],
        compiler_params=_params(("arbitrary",), 40),
    )(do, kpt, qwt, egl)


def _dn_post_bwd(qkv, ba, a_log, dt_bias, states, dsp_all, tinv_all, u_all, wq, vn_all, do, dvn_all, name):
    l = qkv.shape[0]
    nch = l // CHUNK

    def body(q_ref, k_ref, v_ref, ba_ref, al_ref, dt_ref, st_ref, dsp_ref, t_ref, u_ref, wq_ref, vn_ref, do_ref,
             dvn_ref, dqkv_ref, dba_ref, dal_ref, ddt_ref):
        step = pl.program_id(0)
        n = step

        vm, bin_, z, ea, beta, g = _dn_gates(ba_ref, al_ref, dt_ref, n)
        ri, ci = _tri()
        incl, strict = ri >= ci, ri > ci
        gam = _cumsum_rows(incl.astype(f32), g)
        gam_t = gam.T
        lane8 = lax.broadcasted_iota(jnp.int32, (1, DN_HEADS), 1)
        sub8 = lax.broadcasted_iota(jnp.int32, (DN_HEADS, 1), 0)
        dbeta = jnp.zeros((CHUNK, DN_HEADS), f32)
        dgam = jnp.zeros((CHUNK, DN_HEADS), f32)
        dgam_neg_t = jnp.zeros((DN_HEADS, CHUNK), f32)
        last = (lax.broadcasted_iota(jnp.int32, (CHUNK, 1), 0) == CHUNK - 1).astype(f32)
        hs = range(DN_HEADS)
        each = lambda fn: [fn(h) for h in hs]
        rsum = lambda t: jnp.sum(t, axis=-1, keepdims=True)
        gc = each(lambda h: gam[:, h:h + 1])
        bh = each(lambda h: beta[:, h:h + 1])
        qh = each(lambda h: q_ref[:, h * DN_DK:(h + 1) * DN_DK])
        kh = each(lambda h: k_ref[:, h * DN_DK:(h + 1) * DN_DK])
        doh = each(lambda h: _b(do_ref[:, h * DN_DV:(h + 1) * DN_DV]))
        sb = each(lambda h: st_ref[0, h])
        dspb = each(lambda h: dsp_ref[0, h])
        vnb = each(lambda h: vn_ref[:, h * DN_DV:(h + 1) * DN_DV])
        dvn = each(lambda h: dvn_ref[:, h * DN_DV:(h + 1) * DN_DV])
        wb = each(lambda h: wq_ref[0, h, 0:CHUNK, :])
        decay = each(lambda h: jnp.exp(jnp.where(incl, gc[h] - gam_t[h:h + 1, :], -jnp.inf)))
        qb, kb = each(lambda h: _b(qh[h])), each(lambda h: _b(kh[h]))
        eg = each(lambda h: jnp.exp(gc[h]))
        gl = each(lambda h: gc[h][CHUNK - 1:CHUNK, :])
        ekd = each(lambda h: jnp.exp(gl[h] - gc[h]))
        dvnb = each(lambda h: _b(dvn[h]))
        kk = each(lambda h: _nt(kb[h], kb[h]))
        p = each(lambda h: _nt(qb[h], kb[h]) * decay[h])
        dpraw = each(lambda h: _nt(doh[h], vnb[h]))
        dqe = each(lambda h: _nt(doh[h], sb[h]))
        dke = each(lambda h: _nt(vnb[h], dspb[h]))
        dw = each(lambda h: -_nt(dvnb[h], sb[h]))
        dru = each(lambda h: _mm3(t_ref[0, h], dvn[h], _tn))
        drw = each(lambda h: _mm3(t_ref[0, h], dw[h], _tn))
        dqk = each(lambda h: _b(dpraw[h] * decay[h]))
        for h in hs:
            dqkv_ref[:, h * DN_DK:(h + 1) * DN_DK] = _nn(dqk[h], kb[h]) + dqe[h] * eg[h]
            dqkv_ref[:, 2 * DN_QK + h * DN_DV:2 * DN_QK + (h + 1) * DN_DV] = bh[h] * dru[h]
        da = each(lambda h: jnp.where(strict, -(_nt(_b(dru[h]), _b(u_ref[:, h * DN_DV:(h + 1) * DN_DV]))
                                                + _nt(_b(drw[h]), wb[h])), 0.0))
        dkk = each(lambda h: _b(da[h] * bh[h] * decay[h]))
        for h in hs:
            dqkv_ref[:, DN_QK + h * DN_DK:DN_QK + (h + 1) * DN_DK] = (
                _tn(dqk[h], qb[h]) + dke[h] * ekd[h] + (bh[h] * eg[h]) * drw[h]
                + _nn(dkk[h], kb[h]) + _tn(dkk[h], kb[h]))
        for h in hs:
            keg = kh[h] * eg[h]
            ke = kh[h] * ekd[h]
            rw = rsum(drw[h] * keg)
            rke = rsum(dke[h] * ke)
            db_h = rsum(dru[h] * v_ref[:, h * DN_DV:(h + 1) * DN_DV]) + rw + rsum(da[h] * kk[h] * decay[h])
            mm = da[h] * (bh[h] * kk[h] * decay[h]) + dpraw[h] * p[h]
            dgl = (jnp.sum(rke, axis=0, keepdims=True)
                   + jnp.exp(gl[h]) * jnp.sum(rsum(dspb[h].astype(f32) * sb[h].astype(f32)), axis=0, keepdims=True))
            dg_h = rsum(mm) + rw * bh[h] + rsum(dqe[h] * (qh[h] * eg[h])) - rke + last * dgl
            dbeta = dbeta + jnp.where(lane8 == h, db_h, 0.0)
            dgam = dgam + jnp.where(lane8 == h, dg_h, 0.0)
            dgam_neg_t = dgam_neg_t + jnp.where(sub8 == h, jnp.sum(mm, axis=0, keepdims=True), 0.0)
        dgam = dgam - dgam_neg_t.T
        dg = _cumsum_rows((ri <= ci).astype(f32), dgam)
        sg = _sigmoid(bin_)
        dbin = dbeta * vm * sg * (1.0 - sg)
        dain = dg * (-ea) * vm * _sigmoid(z)
        dba_ref[...] = jnp.zeros_like(dba_ref)
        dba_ref[:, 0:DN_HEADS] = dbin
        dba_ref[:, DN_HEADS:2 * DN_HEADS] = dain
        dal = jnp.sum(dg * g, axis=0, keepdims=True)
        ddt = jnp.sum(dain, axis=0, keepdims=True)

        @pl.when(step == 0)
        def _():
            dal_ref[...] = dal
            ddt_ref[...] = ddt

        @pl.when(step > 0)
        def _():
            dal_ref[...] += dal
            ddt_ref[...] += ddt

    vec = pl.BlockSpec((1, DN_HEADS), lambda s: (0, 0))
    qs = pl.BlockSpec((CHUNK, DN_QK), lambda s: (s, 0))
    ks = pl.BlockSpec((CHUNK, DN_QK), lambda s: (s, 1))
    vs = pl.BlockSpec((CHUNK, DN_V), lambda s: (s, 1))
    v0 = pl.BlockSpec((CHUNK, DN_V), lambda s: (s, 0))
    st = pl.BlockSpec((1, DN_HEADS, DN_DK, DN_DV), lambda s: (s, 0, 0, 0))
    return pl.pallas_call(
        body, name=name, grid=(nch,),
        in_specs=[qs, ks, vs, pl.BlockSpec((CHUNK, BA_W), lambda s: (s, 0)), vec, vec, st, st,
                  pl.BlockSpec((1, DN_HEADS, CHUNK, CHUNK), lambda s: (s, 0, 0, 0)),
                  v0, pl.BlockSpec((1, DN_HEADS, 2 * CHUNK, DN_DK), lambda s: (s, 0, 0, 0)), v0, v0, v0],
        out_specs=[pl.BlockSpec((CHUNK, DN_CONV_CH), lambda s: (s, 0)),
                   pl.BlockSpec((CHUNK, BA_W), lambda s: (s, 0)), vec, vec],
        out_shape=[jax.ShapeDtypeStruct((l, DN_CONV_CH), f32), jax.ShapeDtypeStruct((l, BA_W), f32),
                   jax.ShapeDtypeStruct((1, DN_HEADS), f32), jax.ShapeDtypeStruct((1, DN_HEADS), f32)],
        compiler_params=_params(("arbitrary",), 48),
    )(qkv, qkv, qkv, ba, a_log, dt_bias, states, dsp_all, tinv_all, u_all, wq, vn_all, do, dvn_all)


def _ffn_fwd(h, nw, wg, wu, wd, tb, th, tag, plan):
    fh = wd.shape[0]
    hn = _rms_fwd(h, nw, f"{tag}_norm")
    a, b, s = plan.call(f"{tag}_gu", functools.partial(_ffn_gu, tm=th // 2, tn=fh // 2), hn, wg, wu, n_out=3)
    out = plan.matmul(f"{tag}_down", s, wd, mode="nn", tm=th, tn=512, tk=fh, res=h)
    return out, (hn, a, b, s)


def _ffn_bwd(dh, dhb, h, nw, wg, wu, wd, saved, tb, th, tag, plan):
    hn, a, b, s = saved
    d = h.shape[1]
    fh = wd.shape[0]
    layer = tag[-1]
    gr = plan.grads
    da, db = _ffn_ds(dhb, wd, a, b, tm=th // 2, tn=fh // 2, name=f"{tag}_b_ds")
    gr["down" + layer] = _matmul(s, dhb, mode="tn", tm=fh // 2, tn=d, tk=th, out_dtype=bf16, name=f"{tag}_b_dwd")
    dhn = plan.matmul(f"{tag}_b_dhn", da, wg, mode="nt", tm=th // 2, tn=d, tk=fh // 2, pair2=(db, wu))
    gr["gate" + layer] = _matmul(hn, da, mode="tn", tm=d, tn=fh // 2, tk=th, out_dtype=bf16, name=f"{tag}_b_dwg")
    gr["up" + layer] = _matmul(hn, db, mode="tn", tm=d, tn=fh // 2, tk=th, out_dtype=bf16, name=f"{tag}_b_dwu")
    dh2, dh2b, dnw = _rms_bwd(h, nw, dhn, dh, f"{tag}_b_norm")
    return dh2, dh2b, dnw


class _Plan:
    GATHERS = {"ret_proj": ("ret_out", "gate0"), "ret_scan": ("up0", "down0"), "ffn0_gu": ("dn_in",),
               "ffn0_down": ("dn_out",), "dn_proj": ("gate1", "up1"), "dn_conv": ("down1",)}
    SCATTERS = {"ffn1_b_dhn": ("down1",), "dn_b_conv_a": ("gate1", "up1", "dn_out"), "ffn0_b_dhn": ("dn_in",),
                "ret_b_scan": ("gate0", "up0", "down0", "ret_out"), "ret_b_dhn": ("ret_in",)}

    def __init__(self, shards, wts):
        self.shards, self.wts, self.grads, self.parts = shards, wts, {}, {}

    def _exchange(self, stage):
        if self.shards is None:
            return None
        if stage in self.GATHERS:
            return _Exchange([self.shards[n] for n in self.GATHERS[stage]], True)
        if stage in self.SCATTERS:
            return _Exchange([self._dev_major(n) for n in self.SCATTERS[stage]], False)
        return None

    def _dev_major(self, name):
        g = self.grads
        if name[:-1] in ("gate", "up"):
            return _dev_major_cols(g[name], g[name].shape[1] // N_DEV)
        if name[:-1] == "down":
            dwd = g[name]
            return dwd.reshape(N_DEV, dwd.shape[0] // N_DEV, dwd.shape[1])
        if name in ("ret_out", "dn_out"):
            return g[name].reshape(N_DEV, g[name].shape[0] // N_DEV, g[name].shape[1])
        return _dev_major_cols(g[name], self.shards[name].shape[-1])

    def _landed(self, stage, outs):
        if stage in self.SCATTERS:
            self.parts.update(zip(self.SCATTERS[stage], outs))
            return
        w = self.wts
        cols = lambda t: t.transpose(1, 0, 2).reshape(t.shape[1], N_DEV * t.shape[2])
        rows = lambda t: t.reshape(N_DEV * t.shape[1], t.shape[2])
        for name, t in zip(self.GATHERS[stage], outs):
            if name in ("ret_out", "dn_out") or name.startswith("down"):
                w[name] = rows(t)
            elif name == "dn_in":
                full = cols(t)
                n_main = DN_CONV_CH + DN_V
                w["dn_main"] = full[:, :n_main]
                w["dn_ba"] = jnp.pad(full[:, n_main:], ((0, 0), (0, BA_W - (full.shape[1] - n_main))))
            else:
                w[name] = cols(t)

    def matmul(self, stage, a, b, **kw):
        comm = self._exchange(stage)
        if comm is None:
            return _matmul(a, b, name=stage, **kw)
        out, landed = _matmul(a, b, name=stage, comm=comm, **kw)
        self._landed(stage, landed)
        return out

    def call(self, stage, fn, *args, n_out):
        comm = self._exchange(stage)
        out = fn(*args, stage, comm=comm)
        if comm is not None:
            self._landed(stage, out[n_out:])
        return out[:n_out]


def _local_step(x2, target, wts, shards=None):
    plan = _Plan(shards, wts)
    s_len, d = x2.shape
    l = s_len + CHUNK
    tb = _tile(l, 3072)
    th = tb // 2 if (tb // 2) % 16 == 0 else tb
    half = RET_DK // 2
    inv_freq = (np.float32(ROPE_BASE) ** (-np.arange(half, dtype=np.float32) / np.float32(half))).astype(np.float32)
    ang = (np.arange(l) - PAD).astype(np.float32)[:, None] * inv_freq[None, :]
    cos, sin = jnp.asarray(np.cos(ang), f32), jnp.asarray(np.sin(ang), f32)
    lgs = jnp.log1p(-jnp.exp2(-5.0 - jnp.arange(RET_HEADS, dtype=f32)))
    gcs = jnp.exp(lgs * _ret_block(l))

    h0 = jnp.concatenate([jnp.zeros((PAD, d), f32), wts["meta"], x2], axis=0)
    mixw, ffnw = wts["mix_norm"], wts["ffn_norm"]

    hn0 = _rms_fwd(h0, mixw[0:1], "l0_norm")
    proj0 = plan.matmul("ret_proj", hn0, wts["ret_in"], mode="nn", tm=tb, tn=512, tk=d)
    qk0 = _ret_prep(proj0, cos, sin, "ret_prep")
    o0, st0 = plan.call("ret_scan", _ret_scan_fwd, qk0, proj0, lgs, gcs, n_out=2)
    y0 = _gnorm_fwd(o0, proj0, wts["ret_gn"], RET_HEADS, RET_DV, 2, "ret_gnorm")
    h1 = _matmul(y0, wts["ret_out"], mode="nn", tm=th, tn=512, tk=RET_V, res=h0, name="ret_out")
    h2, ffn0 = _ffn_fwd(h1, ffnw[0:1], wts["gate0"], wts["up0"], wts["down0"], tb, th, "ffn0", plan)

    hn2 = _rms_fwd(h2, mixw[1:2], "l1_norm")
    proj1 = plan.matmul("dn_proj", hn2, wts["dn_main"], mode="nn", tm=tb, tn=512, tk=d)
    ba = _matmul(hn2, wts["dn_ba"], mode="nn", tm=tb, tn=BA_W, tk=d, name="dn_proj_ba")
    (qkv1,) = plan.call("dn_conv", _dn_conv_fwd, proj1, wts["conv_w"], n_out=1)
    tinv1, u1, wq1, pk1, egl1, kpt1, qwt1 = _dn_prep(qkv1, ba, wts["a_log"], wts["dt_bias"], "dn_prep")
    o1, st1, vn1 = _dn_scan_fwd(u1, wq1, pk1, egl1, "dn_scan")
    y1 = _gnorm_fwd(o1, proj1, wts["dn_norm"], DN_HEADS, DN_DV, 2, "dn_gnorm")
    h3 = _matmul(y1, wts["dn_out"], mode="nn", tm=th, tn=512, tk=DN_V, res=h2, name="dn_out")
    h4, ffn1 = _ffn_fwd(h3, ffnw[1:2], wts["gate1"], wts["up1"], wts["down1"], tb, th, "ffn1", plan)

    dh4, dh4b, dfinal, loss = _final_loss(h4, wts["final_norm"], target, "final_loss")
    gr = plan.grads
    dh3, dh3b, dffn1 = _ffn_bwd(dh4, dh4b, h3, ffnw[1:2], wts["gate1"], wts["up1"], wts["down1"], ffn1,
                                tb, th, "ffn1", plan)

    dy1 = _matmul(dh3b, wts["dn_out"], mode="nt", tm=th, tn=1024, tk=d, name="dn_b_dy")
    gr["dn_out"] = _matmul(y1, dh3b, mode="tn", tm=1024, tn=d, tk=tb, out_dtype=bf16, name="dn_b_dwout")
    do1, dproj1, ddn_norm = _gnorm_bwd(o1, proj1, wts["dn_norm"], dy1, DN_HEADS, DN_DV, 2, "dn_b_gnorm")
    dvn1, dsp1 = _dn_scan_bwd(do1, kpt1, qwt1, egl1, "dn_b_scan")
    dqkv1, dba, dalog, ddt = _dn_post_bwd(qkv1, ba, wts["a_log"], wts["dt_bias"], st1, dsp1, tinv1, u1, wq1, vn1,
                                          do1, dvn1, "dn_b_post")
    dc1, dconv = plan.call("dn_b_conv_a", _dn_conv_bwd_a, proj1, wts["conv_w"], dqkv1, n_out=2)
    dproj1 = _dn_conv_bwd_b(dc1, wts["conv_w"], dproj1, "dn_b_conv_b")
    dbab = dba.astype(bf16)
    n_main = dproj1.shape[1]
    dhn2 = plan.matmul("dn_b_dhn", dproj1, wts["dn_main"], mode="nt", tm=th, tn=d, tk=n_main // 4)
    dhn2 = _matmul(dbab, wts["dn_ba"], mode="nt", tm=th, tn=d, tk=BA_W, res=dhn2, name="dn_b_dhn_ba")
    dw_main = _matmul(hn2, dproj1, mode="tn", tm=d, tn=512, tk=tb, out_dtype=bf16, name="dn_b_dwin")
    dw_ba = _matmul(hn2, dbab, mode="tn", tm=d, tn=BA_W, tk=tb, out_dtype=bf16, name="dn_b_dwin_ba")
    gr["dn_in"] = jnp.concatenate([dw_main, dw_ba], axis=1)
    dh2, dh2b, dmix1 = _rms_bwd(h2, mixw[1:2], dhn2, dh3, "l1_b_norm")

    dh1, dh1b, dffn0 = _ffn_bwd(dh2, dh2b, h1, ffnw[0:1], wts["gate0"], wts["up0"], wts["down0"], ffn0,
                                tb, th, "ffn0", plan)

    dy0 = _matmul(dh1b, wts["ret_out"], mode="nt", tm=th, tn=1024, tk=d, name="ret_b_dy")
    gr["ret_out"] = _matmul(y0, dh1b, mode="tn", tm=1024, tn=d, tk=tb, out_dtype=bf16, name="ret_b_dwout")
    do0, dproj0, dret_gn = _gnorm_bwd(o0, proj0, wts["ret_gn"], dy0, RET_HEADS, RET_DV, 2, "ret_b_gnorm")
    dq0, dk0, dproj0 = plan.call("ret_b_scan", _ret_scan_bwd, qk0, proj0, st0, do0, dproj0, lgs, gcs, n_out=3)
    dproj0 = _ret_prep_bwd(dq0, dk0, cos, sin, dproj0, "ret_b_prep")
    n_in = dproj0.shape[1]
    gr["ret_in"] = _matmul(hn0, dproj0, mode="tn", tm=d, tn=512, tk=tb, out_dtype=bf16, name="ret_b_dwin")
    dhn0 = plan.matmul("ret_b_dhn", dproj0, wts["ret_in"], mode="nt", tm=th, tn=d, tk=n_in // 4)
    dh0, _, dmix0 = _rms_bwd(h0, mixw[0:1], dhn0, dh1, "l0_b_norm")

    gr.update(meta=dh0[PAD:CHUNK], mix_norm=jnp.concatenate([dmix0, dmix1], axis=0),
              ffn_norm=jnp.concatenate([dffn0, dffn1], axis=0), ret_gn=dret_gn, conv_w=dconv, a_log=dalog,
              dt_bias=ddt, dn_norm=ddn_norm, final_norm=dfinal)
    return loss, dh0[CHUNK:], gr, plan


def _adamw_reduce(parts, w, m, v, name):
    _, r, c = parts.shape
    c_pad = -(-c // LANES) * LANES
    tr = _div_tile(r, max(8, (3 * MIB // 16) // c_pad // 8 * 8), 16)

    def body(p_ref, w_ref, m_ref, v_ref, g_ref, d_ref, nm_ref, nv_ref):
        g = p_ref[0].astype(f32)
        for s in range(1, N_DEV):
            g = g + p_ref[s].astype(f32)
        mm = ADAM_B1 * m_ref[...] + (1.0 - ADAM_B1) * g
        vv = ADAM_B2 * v_ref[...] + (1.0 - ADAM_B2) * (g * g)
        m_hat = mm / (1.0 - ADAM_B1 ** ADAM_STEP)
        v_hat = vv / (1.0 - ADAM_B2 ** ADAM_STEP)
        g_ref[...] = g
        d_ref[...] = -ADAM_LR * (m_hat / (jnp.sqrt(v_hat) + ADAM_EPS) + ADAM_WD * w_ref[...])
        nm_ref[...] = mm
        nv_ref[...] = vv

    blk = pl.BlockSpec((tr, c), lambda i: (i, 0))
    return pl.pallas_call(
        body, name=name, grid=(r // tr,),
        in_specs=[pl.BlockSpec((N_DEV, tr, c), lambda i: (0, i, 0)), blk, blk, blk], out_specs=[blk] * 4,
        out_shape=[jax.ShapeDtypeStruct((r, c), f32)] * 4,
        compiler_params=_params(("parallel",), 48),
    )(parts, w, m, v)


def _dev_major_cols(g, width):
    r = g.shape[0]
    return g[:, :N_DEV * width].reshape(r, N_DEV, width).transpose(1, 0, 2)


def kernel(x, meta_tokens, mix_norm_w, ffn_norm_w, ret_w_in, ret_gn_w, ret_w_out, dn_w_in, dn_conv_w, dn_a_log, dn_dt_bias, dn_norm_w, dn_w_out, ffn_w_gate, ffn_w_up, ffn_w_down, final_norm_w, loss_target, m_meta_tokens, m_mix_norm_w, m_ffn_norm_w, m_ret_w_in, m_ret_gn_w, m_ret_w_out, m_dn_w_in, m_dn_conv_w, m_dn_a_log, m_dn_dt_bias, m_dn_norm_w, m_dn_w_out, m_ffn_w_gate, m_ffn_w_up, m_ffn_w_down, m_final_norm_w, v_meta_tokens, v_mix_norm_w, v_ffn_norm_w, v_ret_w_in, v_ret_gn_w, v_ret_w_out, v_dn_w_in, v_dn_conv_w, v_dn_a_log, v_dn_dt_bias, v_dn_norm_w, v_dn_w_out, v_ffn_w_gate, v_ffn_w_up, v_ffn_w_down, v_final_norm_w):
    d = x.shape[-1]
    me = 4 * lax.axis_index("x") + 2 * lax.axis_index("y") + lax.axis_index("c")

    shards = dict(ret_in=ret_w_in[0].astype(bf16), ret_out=ret_w_out[0].astype(bf16),
                  dn_in=dn_w_in[0].astype(bf16), dn_out=dn_w_out[0].astype(bf16))
    for layer in (0, 1):
        shards[f"gate{layer}"] = ffn_w_gate[layer].astype(bf16)
        shards[f"up{layer}"] = ffn_w_up[layer].astype(bf16)
        shards[f"down{layer}"] = ffn_w_down[layer].astype(bf16)
    g_ret_in, g_meta, g_conv, g_dnn = _exchange([shards["ret_in"], meta_tokens, dn_conv_w[0], dn_norm_w], True,
                                                "gather_first")
    cols = lambda g: g.transpose(1, 0, 2).reshape(g.shape[1], N_DEV * g.shape[2])
    wts = dict(meta=cols(g_meta), mix_norm=mix_norm_w, ffn_norm=ffn_norm_w, ret_in=cols(g_ret_in), ret_gn=ret_gn_w,
               conv_w=cols(g_conv), a_log=dn_a_log, dt_bias=dn_dt_bias, dn_norm=cols(g_dnn),
               final_norm=final_norm_w.reshape(1, d))

    loss_part, grad_x, gr, plan = _local_step(x[0], loss_target[0], wts, shards)
    loss = lax.psum(loss_part[0, 0], AXES)

    pp = plan.parts
    both = lambda name: jnp.concatenate([pp[name + "0"], pp[name + "1"]], axis=1)
    big_parts = [pp["ret_in"], pp["ret_out"], pp["dn_in"], pp["dn_out"], both("gate"), both("up"), both("down")]
    big_names = ["ret_w_in", "ret_w_out", "dn_w_in", "dn_w_out", "ffn_w_gate", "ffn_w_up", "ffn_w_down"]
    big_w = [ret_w_in, ret_w_out, dn_w_in, dn_w_out, ffn_w_gate, ffn_w_up, ffn_w_down]
    big_m = [m_ret_w_in, m_ret_w_out, m_dn_w_in, m_dn_w_out, m_ffn_w_gate, m_ffn_w_up, m_ffn_w_down]
    big_v = [v_ret_w_in, v_ret_w_out, v_dn_w_in, v_dn_w_out, v_ffn_w_gate, v_ffn_w_up, v_ffn_w_down]
    res = {}
    for nm, parts, w_, m_, v_ in zip(big_names, big_parts, big_w, big_m, big_v):
        r2, c2 = parts.shape[1], parts.shape[2]
        outs = _adamw_reduce(parts, w_.reshape(r2, c2), m_.reshape(r2, c2), v_.reshape(r2, c2), f"adamw_{nm}")
        res[nm] = [o.reshape(w_.shape) for o in outs]

    small_names = ["meta_tokens", "mix_norm_w", "ffn_norm_w", "ret_gn_w", "dn_conv_w", "dn_a_log", "dn_dt_bias",
                   "dn_norm_w", "final_norm_w"]
    small_g = [gr["meta"], gr["mix_norm"], gr["ffn_norm"], gr["ret_gn"], gr["conv_w"], gr["a_log"], gr["dt_bias"],
               gr["dn_norm"], gr["final_norm"]]
    small_w = [meta_tokens, mix_norm_w, ffn_norm_w, ret_gn_w, dn_conv_w, dn_a_log, dn_dt_bias, dn_norm_w, final_norm_w]
    small_m = [m_meta_tokens, m_mix_norm_w, m_ffn_norm_w, m_ret_gn_w, m_dn_conv_w, m_dn_a_log, m_dn_dt_bias,
               m_dn_norm_w, m_final_norm_w]
    small_v = [v_meta_tokens, v_mix_norm_w, v_ffn_norm_w, v_ret_gn_w, v_dn_conv_w, v_dn_a_log, v_dn_dt_bias,
               v_dn_norm_w, v_final_norm_w]
    sharded = {"meta_tokens", "dn_conv_w", "dn_norm_w"}
    flat = jnp.concatenate([g.reshape(-1) for g in small_g])
    row = 8 * LANES
    n_flat = flat.shape[0]
    flat = jnp.pad(flat, (0, -n_flat % row)).reshape(-1, row)
    (gathered,) = _exchange([flat], True, "gather_small_grads")
    gathered = gathered.reshape(N_DEV, -1)
    pieces, off = [], 0
    for nm, g, w_ in zip(small_names, small_g, small_w):
        full = gathered[:, off:off + g.size].reshape((N_DEV,) + g.shape)
        off += g.size
        if nm in sharded:
            wloc = w_.shape[-1]
            full = lax.dynamic_slice_in_dim(full, me * wloc, wloc, axis=full.ndim - 1)
        pieces.append(full.reshape(N_DEV, -1))
    sizes = [p.shape[1] for p in pieces]
    n_loc = sum(sizes)
    pad_loc = -n_loc % row

    def pack(vs, lead):
        cat = jnp.concatenate([a.reshape(lead + (-1,)) for a in vs], axis=-1)
        cat = jnp.pad(cat, [(0, 0)] * len(lead) + [(0, pad_loc)])
        return cat.reshape(lead + (-1, row))

    outs = _adamw_reduce(pack(pieces, (N_DEV,)), pack(small_w, ()), pack(small_m, ()), pack(small_v, ()), "adamw_small")
    off = 0
    for nm, sz, w_ in zip(small_names, sizes, small_w):
        res[nm] = [o.reshape(-1)[off:off + sz].reshape(w_.shape) for o in outs]
        off += sz

    order = ["meta_tokens", "mix_norm_w", "ffn_norm_w", "ret_w_in", "ret_gn_w", "ret_w_out", "dn_w_in", "dn_conv_w",
             "dn_a_log", "dn_dt_bias", "dn_norm_w", "dn_w_out", "ffn_w_gate", "ffn_w_up", "ffn_w_down", "final_norm_w"]
    grad_x = grad_x.reshape(x.shape)
    return (loss, grad_x, *[res[nm][0] for nm in order], *[res[nm][1] for nm in order],
            *[res[nm][2] for nm in order], *[res[nm][3] for nm in order])
```

```python
import functools
import math

import jax
import jax.numpy as jnp
import numpy as np
from jax import lax
from jax.experimental import pallas as pl
from jax.experimental.pallas import tpu as pltpu

f32 = jnp.float32
bf16 = jnp.bfloat16
HI = lax.Precision.HIGHEST

N_META = 16
CHUNK = 64
PAD = CHUNK - N_META
RMS_EPS = 1e-6
RET_HEADS, RET_DK, RET_DV = 4, 256, 512
RET_QK, RET_V = RET_HEADS * RET_DK, RET_HEADS * RET_DV
DN_HEADS, DN_DK, DN_DV = 8, 128, 256
DN_QK, DN_V = DN_HEADS * DN_DK, DN_HEADS * DN_DV
DN_CONV_CH = 2 * DN_QK + DN_V
CONV_K = 4
ROPE_BASE = 10000.0
ADAM_LR, ADAM_B1, ADAM_B2, ADAM_EPS, ADAM_WD, ADAM_STEP = 0.001, 0.9, 0.999, 1e-08, 0.01, 10
N_DEV = 8
AXES = ("x", "y", "c")
LANES = 128
MIB = 1024 * 1024


def _tile(n_rows, cap):
    nch = n_rows // CHUNK
    best = 1
    for d in range(1, nch + 1):
        if nch % d == 0 and d * CHUNK <= cap:
            best = d
    return best * CHUNK


def _div_tile(n, cap, align):
    best = None
    for d in range(align, min(n, cap) + 1, align):
        if n % d == 0:
            best = d
    return best if best is not None else n


def _params(sem, vmem_mb):
    return pltpu.CompilerParams(dimension_semantics=sem, vmem_limit_bytes=int(vmem_mb * MIB))


def _nn(a, b, precision=None):
    return jnp.dot(a, b, preferred_element_type=f32, precision=precision)


def _nt(a, b, precision=None):
    return lax.dot_general(a, b, (((1,), (1,)), ((), ())), preferred_element_type=f32, precision=precision)


def _tn(a, b, precision=None):
    return lax.dot_general(a, b, (((0,), (0,)), ((), ())), preferred_element_type=f32, precision=precision)


def _b(x):
    return x.astype(bf16)


def _sigmoid(x):
    return 1.0 / (1.0 + jnp.exp(-x))


def _silu(x):
    return x * _sigmoid(x)


def _dsilu(x):
    s = _sigmoid(x)
    return s * (1.0 + x * (1.0 - s))


def _peer(k):
    x, y, c = lax.axis_index("x"), lax.axis_index("y"), lax.axis_index("c")
    px = 1 - x if k & 4 else x
    py = 1 - y if k & 2 else y
    pc = 1 - c if k & 1 else c
    return (px, py, pc), 4 * px + 2 * py + pc


class _Exchange:
    def __init__(self, arrs, gather):
        self.arrs, self.gather, self.n = list(arrs), gather, len(arrs)
        self.out_shapes = [jax.ShapeDtypeStruct(((N_DEV,) + a.shape) if gather else a.shape, a.dtype) for a in arrs]
        self.specs = [pl.BlockSpec(memory_space=pltpu.HBM)] * self.n
        self.scratch = [pltpu.SemaphoreType.DMA((self.n, N_DEV - 1)), pltpu.SemaphoreType.DMA((self.n, N_DEV - 1)),
                        pltpu.SemaphoreType.DMA((self.n,))]

    def _copies(self, ins, outs, sems):
        send_sems, recv_sems, local_sems = sems
        me = 4 * lax.axis_index("x") + 2 * lax.axis_index("y") + lax.axis_index("c")
        src = (lambda a, dest: ins[a]) if self.gather else (lambda a, dest: ins[a].at[dest])
        local = [pltpu.make_async_copy(src(a, me), outs[a].at[me], local_sems.at[a]) for a in range(self.n)]
        sends, lands = [], []
        for k in range(1, N_DEV):
            peer, pidx = _peer(k)
            for a in range(self.n):
                for dst, lst in ((outs[a].at[me], sends), (outs[a].at[pidx], lands)):
                    lst.append(pltpu.make_async_remote_copy(
                        src_ref=src(a, pidx), dst_ref=dst, send_sem=send_sems.at[a, k - 1],
                        recv_sem=recv_sems.at[a, k - 1], device_id=peer, device_id_type=pl.DeviceIdType.MESH))
        return local, sends, lands

    def start(self, ins, outs, sems):
        local, sends, _ = self._copies(ins, outs, sems)
        for cp in local + sends:
            cp.start()

    def wait(self, ins, outs, sems):
        local, sends, lands = self._copies(ins, outs, sems)
        for cp in lands:
            cp.wait_recv()
        for cp in sends:
            cp.wait_send()
        for cp in local:
            cp.wait()


def _call(body, args, *, name, grid, in_specs, out_specs, out_shape, scratch=(), sem, vmem_mb, comm=None,
          aliases=None):
    aliases = aliases or {}
    if comm is None:
        out = pl.pallas_call(body, name=name, grid=grid, in_specs=list(in_specs), out_specs=list(out_specs),
                             out_shape=list(out_shape), scratch_shapes=list(scratch), input_output_aliases=aliases,
                             compiler_params=_params(sem, vmem_mb))(*args)
        return list(out)
    n_in, n_out, n_scr, nc = len(args), len(out_shape), len(scratch), comm.n

    def carried(*refs):
        ins, cin = refs[:n_in], refs[n_in:n_in + nc]
        o0 = n_in + nc
        outs, cout = refs[o0:o0 + n_out], refs[o0 + n_out:o0 + n_out + nc]
        s0 = o0 + n_out + nc
        scr, sems = refs[s0:s0 + n_scr], refs[s0 + n_scr:]
        first = functools.reduce(jnp.logical_and, [pl.program_id(i) == 0 for i in range(len(grid))])
        last = functools.reduce(jnp.logical_and, [pl.program_id(i) == grid[i] - 1 for i in range(len(grid))])

        @pl.when(first)
        def _():
            comm.start(cin, cout, sems)

        body(*ins, *outs, *scr)

        @pl.when(last)
        def _():
            comm.wait(cin, cout, sems)

    out = pl.pallas_call(
        carried, name=name, grid=grid, in_specs=list(in_specs) + comm.specs, out_specs=list(out_specs) + comm.specs,
        out_shape=list(out_shape) + comm.out_shapes, scratch_shapes=list(scratch) + comm.scratch,
        input_output_aliases=aliases,
        compiler_params=_params(("arbitrary",) * len(grid), vmem_mb))(*args, *comm.arrs)
    return list(out)


def _exchange(arrs, gather, name):
    comm = _Exchange(arrs, gather)

    def body(*refs):
        ins, outs, sems = refs[:comm.n], refs[comm.n:2 * comm.n], refs[2 * comm.n:]
        comm.start(ins, outs, sems)
        comm.wait(ins, outs, sems)

    return pl.pallas_call(body, name=name, in_specs=comm.specs, out_specs=comm.specs, out_shape=comm.out_shapes,
                          scratch_shapes=comm.scratch)(*comm.arrs)


def _matmul(a, b, *, mode, tm, tn, tk, name, out_dtype=f32, res=None, vmem_mb=48, comm=None, pair2=None):
    def dims(x, y):
        if mode == "nn":
            (m_, k_), (k2_, n_) = x.shape, y.shape
        elif mode == "nt":
            (m_, k_), (n_, k2_) = x.shape, y.shape
        else:
            (k_, m_), (k2_, n_) = x.shape, y.shape
        assert k_ == k2_ and m_ % tm == 0 and n_ % tn == 0 and k_ % tk == 0, (name, x.shape, y.shape, tm, tn, tk)
        return m_, n_, k_ // tk

    m, n, nk1 = dims(a, b)
    nk2 = dims(*pair2)[2] if pair2 is not None else 0
    nk = nk1 + nk2
    dot = {"nn": _nn, "nt": _nt, "tn": _tn}[mode]

    def specs(k_of):
        a_spec = {"nn": pl.BlockSpec((tm, tk), lambda i, j, kk: (i, k_of(kk))),
                  "nt": pl.BlockSpec((tm, tk), lambda i, j, kk: (i, k_of(kk))),
                  "tn": pl.BlockSpec((tk, tm), lambda i, j, kk: (k_of(kk), i))}[mode]
        b_spec = {"nn": pl.BlockSpec((tk, tn), lambda i, j, kk: (k_of(kk), j)),
                  "nt": pl.BlockSpec((tn, tk), lambda i, j, kk: (j, k_of(kk))),
                  "tn": pl.BlockSpec((tk, tn), lambda i, j, kk: (k_of(kk), j))}[mode]
        return [a_spec, b_spec]

    o_spec = pl.BlockSpec((tm, tn), lambda i, j, kk: (i, j))
    has_res = res is not None
    n_ops = 2 + (2 if pair2 is not None else 0) + (1 if has_res else 0)

    def body(*refs):
        a_ref, b_ref = refs[:2]
        a2_ref, b2_ref = refs[2:4] if pair2 is not None else (None, None)
        r_ref = refs[n_ops - 1] if has_res else None
        o_ref = refs[n_ops]
        rest = refs[n_ops + 1:]
        if nk == 1:
            part = dot(_b(a_ref[...]), _b(b_ref[...]))
            if has_res:
                part = part + r_ref[...]
            o_ref[...] = part.astype(out_dtype)
            return
        acc_ref = rest[0]
        kk = pl.program_id(2)

        @pl.when(kk == 0)
        def _():
            acc_ref[...] = dot(_b(a_ref[...]), _b(b_ref[...]))

        @pl.when(jnp.logical_and(kk > 0, kk < nk1))
        def _():
            acc_ref[...] += dot(_b(a_ref[...]), _b(b_ref[...]))

        if pair2 is not None:
            @pl.when(kk >= nk1)
            def _():
                acc_ref[...] += dot(_b(a2_ref[...]), _b(b2_ref[...]))

        @pl.when(kk == nk - 1)
        def _():
            tot = acc_ref[...]
            if has_res:
                tot = tot + r_ref[...]
            o_ref[...] = tot.astype(out_dtype)

    in_specs = specs(lambda kk: jnp.minimum(kk, nk1 - 1))
    args = (a, b)
    if pair2 is not None:
        in_specs += specs(lambda kk: jnp.maximum(kk - nk1, 0))
        args += tuple(pair2)
    if has_res:
        in_specs.append(o_spec)
        args += (res,)
    out = _call(body, args, name=name, grid=(m // tm, n // tn, nk), in_specs=in_specs, out_specs=[o_spec],
                out_shape=[jax.ShapeDtypeStruct((m, n), out_dtype)],
                scratch=[pltpu.VMEM((tm, tn), f32)] if nk > 1 else [],
                sem=("parallel", "parallel", "arbitrary"), vmem_mb=vmem_mb, comm=comm)
    return out[0] if comm is None else (out[0], out[1:])


def _rms_fwd(h, w, name):
    l, d = h.shape
    tr = _tile(l, 512)

    def body(h_ref, w_ref, o_ref):
        x = h_ref[...]
        r = lax.rsqrt(jnp.mean(x * x, axis=-1, keepdims=True) + RMS_EPS)
        o_ref[...] = _b(x * r * w_ref[...])

    return pl.pallas_call(
        body, name=name, grid=(l // tr,),
        in_specs=[pl.BlockSpec((tr, d), lambda i: (i, 0)), pl.BlockSpec((1, d), lambda i: (0, 0))],
        out_specs=pl.BlockSpec((tr, d), lambda i: (i, 0)),
        out_shape=jax.ShapeDtypeStruct((l, d), bf16),
        compiler_params=_params(("parallel",), 32),
    )(h, w)


def _rms_bwd(h, w, dhn, dres, name):
    l, d = h.shape
    tr = _tile(l, 512)

    def body(h_ref, w_ref, g_ref, r_ref, dh_ref, dhb_ref, dw_ref):
        x = h_ref[...]
        r = lax.rsqrt(jnp.mean(x * x, axis=-1, keepdims=True) + RMS_EPS)
        xh = x * r
        g = g_ref[...]
        dxh = g * w_ref[...]
        dx = r * (dxh - xh * jnp.mean(dxh * xh, axis=-1, keepdims=True))
        dh = r_ref[...] + dx
        dh_ref[...] = dh
        dhb_ref[...] = _b(dh)
        dw = jnp.sum(g * xh, axis=0, keepdims=True)

        @pl.when(pl.program_id(0) == 0)
        def _():
            dw_ref[...] = dw

        @pl.when(pl.program_id(0) > 0)
        def _():
            dw_ref[...] += dw

    row = pl.BlockSpec((tr, d), lambda i: (i, 0))
    vec = pl.BlockSpec((1, d), lambda i: (0, 0))
    return pl.pallas_call(
        body, name=name, grid=(l // tr,), in_specs=[row, vec, row, row], out_specs=[row, row, vec],
        out_shape=[jax.ShapeDtypeStruct((l, d), f32), jax.ShapeDtypeStruct((l, d), bf16),
                   jax.ShapeDtypeStruct((1, d), f32)],
        compiler_params=_params(("arbitrary",), 40),
    )(h, w, dhn, dres)


def _dhn_norm_bwd(pairs, h, nw, dres, name, *, tm, tk, init=None, comm=None):
    l, d = h.shape
    nks = [a.shape[1] // tk for a, _ in pairs]
    starts = [sum(nks[:p]) for p in range(len(pairs))]
    nk = sum(nks)
    n_ops = 2 * len(pairs)
    has_init = init is not None

    def body(*refs):
        ops = refs[:n_ops]
        init_ref = refs[n_ops] if has_init else None
        h_ref, w_ref, r_ref, dh_ref, dhb_ref, dw_ref, acc_ref = refs[n_ops + has_init:]
        i, kk = pl.program_id(0), pl.program_id(1)

        @pl.when(kk == 0)
        def _():
            part = _nt(ops[0][...], ops[1][...])
            acc_ref[...] = part + init_ref[...] if has_init else part

        for p in range(len(pairs)):
            lo = max(starts[p], 1)

            @pl.when(jnp.logical_and(kk >= lo, kk < starts[p] + nks[p]))
            def _(a_ref=ops[2 * p], b_ref=ops[2 * p + 1]):
                acc_ref[...] += _nt(a_ref[...], b_ref[...])

        @pl.when(kk == nk - 1)
        def _():
            g = acc_ref[...]
            x = h_ref[...]
            r = lax.rsqrt(jnp.mean(x * x, axis=-1, keepdims=True) + RMS_EPS)
            xh = x * r
            dxh = g * w_ref[...]
            dh = r_ref[...] + r * (dxh - xh * jnp.mean(dxh * xh, axis=-1, keepdims=True))
            dh_ref[...] = dh
            dhb_ref[...] = _b(dh)
            dw = jnp.sum(g * xh, axis=0, keepdims=True)

            @pl.when(i == 0)
            def _():
                dw_ref[...] = dw

            @pl.when(i > 0)
            def _():
                dw_ref[...] += dw

    def k_of(p):
        return lambda kk: jnp.clip(kk - starts[p], 0, nks[p] - 1)

    in_specs, args = [], []
    for p, (a, b) in enumerate(pairs):
        in_specs += [pl.BlockSpec((tm, tk), functools.partial(lambda i, kk, f: (i, f(kk)), f=k_of(p))),
                     pl.BlockSpec((d, tk), functools.partial(lambda i, kk, f: (0, f(kk)), f=k_of(p)))]
        args += [a, b]
    row = pl.BlockSpec((tm, d), lambda i, kk: (i, 0))
    vec = pl.BlockSpec((1, d), lambda i, kk: (0, 0))
    if has_init:
        in_specs.append(row)
        args.append(init)
    return _call(body, tuple(args) + (h, nw, dres), name=name, grid=(l // tm, nk), in_specs=in_specs + [row, vec, row],
                 out_specs=[row, row, vec],
                 out_shape=[jax.ShapeDtypeStruct((l, d), f32), jax.ShapeDtypeStruct((l, d), bf16),
                            jax.ShapeDtypeStruct((1, d), f32)],
                 scratch=[pltpu.VMEM((tm, d), f32)], sem=("arbitrary", "arbitrary"), vmem_mb=48, comm=comm)


def _final_loss(h, w, target, name):
    l, d = h.shape
    nch = l // CHUNK
    cpt = _tile(l, 256) // CHUNK
    nt = nch // cpt

    def body(h_ref, w_ref, *rest):
        t_refs, (dh_ref, dhb_ref, dw_ref, loss_ref) = rest[:cpt], rest[cpt:]
        i = pl.program_id(0)
        wv = w_ref[...]
        dw = jnp.zeros((1, d), f32)
        part = jnp.zeros((1, 1), f32)
        for c in range(cpt):
            rows = slice(c * CHUNK, (c + 1) * CHUNK)
            live = (i * cpt + c > 0).astype(f32)
            x = h_ref[rows, :]
            r = lax.rsqrt(jnp.mean(x * x, axis=-1, keepdims=True) + RMS_EPS)
            xh = x * r
            err = (xh * wv - t_refs[c][...]) * live
            dy = err * (1.0 / d)
            dxh = dy * wv
            dx = r * (dxh - xh * jnp.mean(dxh * xh, axis=-1, keepdims=True))
            dh_ref[rows, :] = dx
            dhb_ref[rows, :] = _b(dx)
            dw = dw + jnp.sum(dy * xh, axis=0, keepdims=True)
            part = part + 0.5 * jnp.sum(jnp.sum(err * err, axis=-1, keepdims=True) * (1.0 / d), axis=0, keepdims=True)
        part = jnp.broadcast_to(part, (1, LANES))

        @pl.when(i == 0)
        def _():
            dw_ref[...] = dw
            loss_ref[...] = part

        @pl.when(i > 0)
        def _():
            dw_ref[...] += dw
            loss_ref[...] += part

    row = pl.BlockSpec((cpt * CHUNK, d), lambda i: (i, 0))
    vec = pl.BlockSpec((1, d), lambda i: (0, 0))
    t_specs = [pl.BlockSpec((CHUNK, d), functools.partial(lambda i, c: (jnp.maximum(i * cpt + c - 1, 0), 0), c=c))
               for c in range(cpt)]
    return pl.pallas_call(
        body, name=name, grid=(nt,),
        in_specs=[row, vec] + t_specs,
        out_specs=[row, row, vec, pl.BlockSpec((1, LANES), lambda i: (0, 0))],
        out_shape=[jax.ShapeDtypeStruct((l, d), f32), jax.ShapeDtypeStruct((l, d), bf16),
                   jax.ShapeDtypeStruct((1, d), f32), jax.ShapeDtypeStruct((1, LANES), f32)],
        compiler_params=_params(("arbitrary",), 32),
    )(h, w, *([target] * cpt))


def _ffn_gu(hn, wg, wu, name, *, tm, tn, comm=None):
    l, d = hn.shape
    fh = wg.shape[1]

    def body(h_ref, g_ref, u_ref, a_ref, b_ref, s_ref):
        hb = h_ref[...]
        a = _nn(hb, g_ref[...])
        bb = _nn(hb, u_ref[...])
        a_ref[...] = _b(a)
        b_ref[...] = _b(bb)
        s_ref[...] = _b(_silu(a) * bb)

    wspec = pl.BlockSpec((d, tn), lambda i, j: (0, j))
    ospec = pl.BlockSpec((tm, tn), lambda i, j: (i, j))
    return _call(body, (hn, wg, wu), name=name, grid=(l // tm, fh // tn),
                 in_specs=[pl.BlockSpec((tm, d), lambda i, j: (i, 0)), wspec, wspec], out_specs=[ospec] * 3,
                 out_shape=[jax.ShapeDtypeStruct((l, fh), bf16)] * 3, sem=("parallel", "parallel"), vmem_mb=48,
                 comm=comm)


def _ffn_ds(dhb, wd, a, b, *, tm, tn, name):
    l, d = dhb.shape
    fh = wd.shape[0]

    def body(g_ref, w_ref, a_ref, b_ref, da_ref, db_ref):
        ds = _nt(g_ref[...], w_ref[...])
        a = a_ref[...].astype(f32)
        da_ref[...] = _b(ds * b_ref[...].astype(f32) * _dsilu(a))
        db_ref[...] = _b(ds * _silu(a))

    ospec = pl.BlockSpec((tm, tn), lambda i, j: (i, j))
    return pl.pallas_call(
        body, name=name, grid=(l // tm, fh // tn),
        in_specs=[pl.BlockSpec((tm, d), lambda i, j: (i, 0)), pl.BlockSpec((tn, d), lambda i, j: (j, 0)), ospec, ospec],
        out_specs=[ospec, ospec], out_shape=[jax.ShapeDtypeStruct((l, fh), bf16)] * 2,
        compiler_params=_params(("parallel", "parallel"), 48),
    )(dhb, wd, a, b)


def _gnorm_fwd(o, proj, nw, heads, dv, gate_blk, name):
    l, hv = o.shape
    tr = _tile(l, 256)

    def body(o_ref, g_ref, w_ref, y_ref):
        wv = w_ref[...]
        for h in range(heads):
            sl = slice(h * dv, (h + 1) * dv)
            oh = o_ref[:, sl]
            r = lax.rsqrt(jnp.mean(oh * oh, axis=-1, keepdims=True) + RMS_EPS)
            y_ref[:, sl] = _b(oh * r * wv * _silu(g_ref[:, sl]))

    return pl.pallas_call(
        body, name=name, grid=(l // tr,),
        in_specs=[pl.BlockSpec((tr, hv), lambda i: (i, 0)), pl.BlockSpec((tr, hv), lambda i: (i, gate_blk)),
                  pl.BlockSpec((1, dv), lambda i: (0, 0))],
        out_specs=pl.BlockSpec((tr, hv), lambda i: (i, 0)),
        out_shape=jax.ShapeDtypeStruct((l, hv), bf16),
        compiler_params=_params(("parallel",), 32),
    )(o, proj, nw)


def _gnorm_bwd(o, proj, nw, dy, heads, dv, gate_blk, name):
    l, hv = o.shape
    tr = _tile(l, 256)

    def body(o_ref, g_ref, w_ref, dy_ref, do_ref, dg_ref, dw_ref):
        wv = w_ref[...]
        dw = jnp.zeros((1, dv), f32)
        for h in range(heads):
            sl = slice(h * dv, (h + 1) * dv)
            oh = o_ref[:, sl]
            g = g_ref[:, sl]
            dyh = dy_ref[:, sl]
            r = lax.rsqrt(jnp.mean(oh * oh, axis=-1, keepdims=True) + RMS_EPS)
            xh = oh * r
            dn = dyh * _silu(g)
            dg_ref[:, sl] = _b(dyh * (xh * wv) * _dsilu(g))
            dxh = dn * wv
            do_ref[:, sl] = r * (dxh - xh * jnp.mean(dxh * xh, axis=-1, keepdims=True))
            dw = dw + jnp.sum(dn * xh, axis=0, keepdims=True)

        @pl.when(pl.program_id(0) == 0)
        def _():
            dw_ref[...] = dw

        @pl.when(pl.program_id(0) > 0)
        def _():
            dw_ref[...] += dw

    row = pl.BlockSpec((tr, hv), lambda i: (i, 0))
    gate = pl.BlockSpec((tr, hv), lambda i: (i, gate_blk))
    vec = pl.BlockSpec((1, dv), lambda i: (0, 0))
    return pl.pallas_call(
        body, name=name, grid=(l // tr,),
        in_specs=[row, gate, vec, row],
        out_specs=[row, gate, vec],
        out_shape=[jax.ShapeDtypeStruct((l, hv), f32), jax.ShapeDtypeStruct(proj.shape, bf16),
                   jax.ShapeDtypeStruct((1, dv), f32)],
        compiler_params=_params(("arbitrary",), 40),
    )(o, proj, nw, dy)


def _ret_prep(proj, cos, sin, name):
    l = proj.shape[0]
    tr = _tile(l, 256)
    half = RET_DK // 2
    scale = RET_DK ** -0.5

    def body(p_ref, c_ref, s_ref, o_ref):
        rows = pl.program_id(0) * tr + lax.broadcasted_iota(jnp.int32, (tr, 1), 0)
        kmul = jnp.where(rows >= PAD, scale, 0.0).astype(f32)
        c, s = c_ref[...], s_ref[...]
        for j in range(2 * RET_HEADS):
            t1 = p_ref[:, j * RET_DK: j * RET_DK + half]
            t2 = p_ref[:, j * RET_DK + half: (j + 1) * RET_DK]
            o1 = t1 * c - t2 * s
            o2 = t1 * s + t2 * c
            if j >= RET_HEADS:
                o1, o2 = o1 * kmul, o2 * kmul
            o_ref[:, j * RET_DK: j * RET_DK + half] = o1
            o_ref[:, j * RET_DK + half: (j + 1) * RET_DK] = o2

    wide = pl.BlockSpec((tr, 2 * RET_QK), lambda i: (i, 0))
    tab = pl.BlockSpec((tr, half), lambda i: (i, 0))
    return pl.pallas_call(
        body, name=name, grid=(l // tr,), in_specs=[wide, tab, tab], out_specs=wide,
        out_shape=jax.ShapeDtypeStruct((l, 2 * RET_QK), f32),
        compiler_params=_params(("parallel",), 32),
    )(proj, cos, sin)


def _ret_prep_bwd(dq, dk, cos, sin, dproj, name):
    l = dq.shape[0]
    tr = _tile(l, 256)
    half = RET_DK // 2
    scale = RET_DK ** -0.5

    def body(dq_ref, dk_ref, c_ref, s_ref, _, o_ref):
        rows = pl.program_id(0) * tr + lax.broadcasted_iota(jnp.int32, (tr, 1), 0)
        kmul = jnp.where(rows >= PAD, scale, 0.0).astype(f32)
        c, s = c_ref[...], s_ref[...]
        for j in range(2 * RET_HEADS):
            d_ref = dq_ref if j < RET_HEADS else dk_ref
            jj = j % RET_HEADS
            d1 = d_ref[:, jj * RET_DK: jj * RET_DK + half]
            d2 = d_ref[:, jj * RET_DK + half: (jj + 1) * RET_DK]
            if j >= RET_HEADS:
                d1, d2 = d1 * kmul, d2 * kmul
            o_ref[:, j * RET_DK: j * RET_DK + half] = _b(d1 * c + d2 * s)
            o_ref[:, j * RET_DK + half: (j + 1) * RET_DK] = _b(d2 * c - d1 * s)

    nar = pl.BlockSpec((tr, RET_QK), lambda i: (i, 0))
    wide = pl.BlockSpec((tr, 2 * RET_QK), lambda i: (i, 0))
    tab = pl.BlockSpec((tr, half), lambda i: (i, 0))
    return pl.pallas_call(
        body, name=name, grid=(l // tr,), in_specs=[nar, nar, tab, tab, pl.BlockSpec(memory_space=pl.ANY)],
        out_specs=wide, out_shape=jax.ShapeDtypeStruct(dproj.shape, dproj.dtype), input_output_aliases={4: 0},
        compiler_params=_params(("parallel",), 32),
    )(dq, dk, cos, sin, dproj)


RET_BLOCK_CHUNKS = 3


def _ret_block(l):
    nch = l // CHUNK
    return RET_BLOCK_CHUNKS * CHUNK if nch % RET_BLOCK_CHUNKS == 0 else CHUNK


def _ret_decay(lg, rb):
    idx = lax.broadcasted_iota(jnp.int32, (rb, 1), 0).astype(f32)
    ri = lax.broadcasted_iota(jnp.int32, (rb, rb), 0)
    ci = lax.broadcasted_iota(jnp.int32, (rb, rb), 1)
    rel = (ri - ci).astype(f32)
    dmask = jnp.where(ri >= ci, jnp.exp(lg * jnp.maximum(rel, 0.0)), 0.0)
    xi = jnp.exp(lg * (idx + 1.0))
    zeta = jnp.exp(lg * (rb - 1.0 - idx))
    return dmask, xi, zeta


def _ret_scan_fwd(qk, proj, lgs, gcs, name, comm=None):
    l = qk.shape[0]
    rb = _ret_block(l)
    nb = l // rb

    def body(lg_ref, gc_ref, q_ref, k_ref, v_ref, o_ref, st_ref, s_ref):
        @pl.when(pl.program_id(0) == 0)
        def _():
            s_ref[...] = jnp.zeros_like(s_ref)

        for h in range(RET_HEADS):
            dmask, xi, zeta = _ret_decay(lg_ref[h], rb)
            q = q_ref[:, h * RET_DK:(h + 1) * RET_DK]
            k = k_ref[:, h * RET_DK:(h + 1) * RET_DK]
            vb = _b(v_ref[:, h * RET_DV:(h + 1) * RET_DV])
            s = s_ref[h]
            sb = _b(s)
            st_ref[0, h] = sb
            scores = _nt(_b(q), _b(k)) * dmask
            o_ref[:, h * RET_DV:(h + 1) * RET_DV] = _nn(_b(scores), vb) + _nn(_b(q * xi), sb)
            s_ref[h] = gc_ref[h] * s + _tn(_b(k * zeta), vb)

    smem = pl.BlockSpec(memory_space=pltpu.SMEM)
    return _call(
        body, (lgs, gcs, qk, qk, proj), name=name, grid=(nb,),
        in_specs=[smem, smem,
                  pl.BlockSpec((rb, RET_QK), lambda n: (n, 0)),
                  pl.BlockSpec((rb, RET_QK), lambda n: (n, 1)),
                  pl.BlockSpec((rb, RET_V), lambda n: (n, 1))],
        out_specs=[pl.BlockSpec((rb, RET_V), lambda n: (n, 0)),
                   pl.BlockSpec((1, RET_HEADS, RET_DK, RET_DV), lambda n: (n, 0, 0, 0))],
        out_shape=[jax.ShapeDtypeStruct((l, RET_V), f32),
                   jax.ShapeDtypeStruct((nb, RET_HEADS, RET_DK, RET_DV), bf16)],
        scratch=[pltpu.VMEM((RET_HEADS, RET_DK, RET_DV), f32)], sem=("arbitrary",), vmem_mb=40, comm=comm)


def _ret_scan_bwd(qk, proj, states, do, dproj, lgs, gcs, name, comm=None):
    l = qk.shape[0]
    rb = _ret_block(l)
    nb = l // rb

    def body(lg_ref, gc_ref, q_ref, k_ref, v_ref, st_ref, do_ref, _, dq_ref, dk_ref, dv_ref, ds_ref):
        @pl.when(pl.program_id(0) == 0)
        def _():
            ds_ref[...] = jnp.zeros_like(ds_ref)

        for h in range(RET_HEADS):
            dmask, xi, zeta = _ret_decay(lg_ref[h], rb)
            q = q_ref[:, h * RET_DK:(h + 1) * RET_DK]
            k = k_ref[:, h * RET_DK:(h + 1) * RET_DK]
            qb, kb = _b(q), _b(k)
            vb = _b(v_ref[:, h * RET_DV:(h + 1) * RET_DV])
            dob = _b(do_ref[:, h * RET_DV:(h + 1) * RET_DV])
            sb = st_ref[0, h]
            dsp = ds_ref[h]
            dspb = _b(dsp)
            scores = _nt(qb, kb) * dmask
            dscores = _b(_nt(dob, vb) * dmask)
            dq_ref[:, h * RET_DK:(h + 1) * RET_DK] = _nn(dscores, kb) + _nt(dob, sb) * xi
            dk_ref[:, h * RET_DK:(h + 1) * RET_DK] = _tn(dscores, qb) + _nt(vb, dspb) * zeta
            dv_ref[:, h * RET_DV:(h + 1) * RET_DV] = _b(_tn(_b(scores), dob) + _nn(_b(k * zeta), dspb))
            ds_ref[h] = gc_ref[h] * dsp + _tn(_b(q * xi), dob)

    smem = pl.BlockSpec(memory_space=pltpu.SMEM)
    rev = lambda n: nb - 1 - n
    return _call(
        body, (lgs, gcs, qk, qk, proj, states, do, dproj), name=name, grid=(nb,),
        in_specs=[smem, smem,
                  pl.BlockSpec((rb, RET_QK), lambda n: (rev(n), 0)),
                  pl.BlockSpec((rb, RET_QK), lambda n: (rev(n), 1)),
                  pl.BlockSpec((rb, RET_V), lambda n: (rev(n), 1)),
                  pl.BlockSpec((1, RET_HEADS, RET_DK, RET_DV), lambda n: (rev(n), 0, 0, 0)),
                  pl.BlockSpec((rb, RET_V), lambda n: (rev(n), 0)),
                  pl.BlockSpec(memory_space=pl.ANY)],
        out_specs=[pl.BlockSpec((rb, RET_QK), lambda n: (rev(n), 0)),
                   pl.BlockSpec((rb, RET_QK), lambda n: (rev(n), 0)),
                   pl.BlockSpec((rb, RET_V), lambda n: (rev(n), 1))],
        out_shape=[jax.ShapeDtypeStruct((l, RET_QK), f32), jax.ShapeDtypeStruct((l, RET_QK), f32),
                   jax.ShapeDtypeStruct(dproj.shape, dproj.dtype)],
        scratch=[pltpu.VMEM((RET_HEADS, RET_DK, RET_DV), f32)], sem=("arbitrary",), vmem_mb=40, comm=comm,
        aliases={7: 2})


CONV_BLK = 1024
HALO = 8


def _slab_rows(r):
    return pl.ds(pl.multiple_of(r * HALO, HALO), HALO)


def _conv_slab(x_ref, p_ref, r, i, tr):
    cur = x_ref[_slab_rows(r), :]
    prev = jnp.where(r > 0, x_ref[_slab_rows(jnp.maximum(r - 1, 0)), :], p_ref[...])
    row0 = i * tr + r * HALO
    cur = jnp.where(row0 >= PAD, cur, 0.0)
    prev = jnp.where(row0 - HALO >= PAD, prev, 0.0)
    lrow = lax.broadcasted_iota(jnp.int32, (HALO, 1), 0)
    shifted = [jnp.where(lrow < s, pltpu.roll(prev, s, 0), pltpu.roll(cur, s, 0)) for s in range(1, CONV_K)]
    return [cur] + shifted


def _conv_of(xs, w):
    acc = xs[0] * w[CONV_K - 1:CONV_K, :]
    for s in range(1, CONV_K):
        acc = acc + xs[s] * w[CONV_K - 1 - s:CONV_K - s, :]
    return acc


def _slab_loop(n_slabs, fn, init=None):
    return lax.fori_loop(0, n_slabs, fn, init, unroll=8)


def _dn_conv_fwd(proj, conv_w, name, comm=None):
    l = proj.shape[0]
    tr = _tile(l, 256)
    nblk = DN_CONV_CH // CONV_BLK
    heads = CONV_BLK // DN_DK

    def body(x_ref, p_ref, w_ref, o_ref):
        i, j = pl.program_id(0), pl.program_id(1)
        w = w_ref[...]

        def act(r):
            return _silu(_conv_of(_conv_slab(x_ref, p_ref, r, i, tr), w))

        def normed(scale):
            def slab(r, carry):
                a = act(r)
                outs = []
                for h in range(heads):
                    ah = a[:, h * DN_DK:(h + 1) * DN_DK]
                    outs.append(ah * (lax.rsqrt(jnp.sum(ah * ah, axis=-1, keepdims=True) + RMS_EPS) * scale))
                o_ref[_slab_rows(r), :] = jnp.concatenate(outs, axis=1)
                return carry
            return slab

        def plain(r, carry):
            o_ref[_slab_rows(r), :] = act(r)
            return carry

        @pl.when(j == 0)
        def _():
            _slab_loop(tr // HALO, normed(DN_DK ** -0.5))

        @pl.when(j == 1)
        def _():
            _slab_loop(tr // HALO, normed(1.0))

        @pl.when(j >= 2)
        def _():
            _slab_loop(tr // HALO, plain)

    hb = tr // HALO
    return _call(
        body, (proj, proj, conv_w), name=name, grid=(l // tr, nblk),
        in_specs=[pl.BlockSpec((tr, CONV_BLK), lambda i, j: (i, j)),
                  pl.BlockSpec((HALO, CONV_BLK), lambda i, j: (jnp.maximum(i * hb - 1, 0), j)),
                  pl.BlockSpec((CONV_K, CONV_BLK), lambda i, j: (0, j))],
        out_specs=[pl.BlockSpec((tr, CONV_BLK), lambda i, j: (i, j))],
        out_shape=[jax.ShapeDtypeStruct((l, DN_CONV_CH), f32)],
        scratch=[], sem=("parallel", "parallel"), vmem_mb=32, comm=comm)


def _dn_conv_bwd_a(proj, conv_w, dqkv, name, comm=None):
    l = proj.shape[0]
    tr = _tile(l, 256)
    nblk = DN_CONV_CH // CONV_BLK
    heads = CONV_BLK // DN_DK

    def body(x_ref, p_ref, w_ref, d_ref, dc_ref, dw_ref, acc_ref):
        j, i = pl.program_id(0), pl.program_id(1)
        w = w_ref[...]
        acc_ref[...] = jnp.zeros_like(acc_ref)

        def slab_of(l2_scale):
            def slab(r, carry):
                xs = _conv_slab(x_ref, p_ref, r, i, tr)
                c = _conv_of(xs, w)
                a = _silu(c)
                dy = d_ref[_slab_rows(r), :]
                if l2_scale is None:
                    da = dy
                else:
                    parts = []
                    for h in range(heads):
                        sl = slice(h * DN_DK, (h + 1) * DN_DK)
                        ah, dyh = a[:, sl], dy[:, sl]
                        rn = lax.rsqrt(jnp.sum(ah * ah, axis=-1, keepdims=True) + RMS_EPS)
                        yh = ah * rn
                        parts.append((rn * l2_scale) * (dyh - yh * jnp.sum(dyh * yh, axis=-1, keepdims=True)))
                    da = jnp.concatenate(parts, axis=1)
                dc = da * _dsilu(c)
                dc_ref[_slab_rows(r), :] = dc
                for s in range(CONV_K):
                    acc_ref[CONV_K - 1 - s] += dc * xs[s]
                return carry
            return slab

        @pl.when(j == 0)
        def _():
            _slab_loop(tr // HALO, slab_of(DN_DK ** -0.5))

        @pl.when(j == 1)
        def _():
            _slab_loop(tr // HALO, slab_of(1.0))

        @pl.when(j >= 2)
        def _():
            _slab_loop(tr // HALO, slab_of(None))

        ksel = lax.broadcasted_iota(jnp.int32, (CONV_K, 1), 0)
        dw = jnp.zeros((CONV_K, CONV_BLK), f32)
        for k in range(CONV_K):
            dw = dw + jnp.where(ksel == k, jnp.sum(acc_ref[k], axis=0, keepdims=True), 0.0)

        @pl.when(i == 0)
        def _():
            dw_ref[...] = dw

        @pl.when(i > 0)
        def _():
            dw_ref[...] += dw

    hb = tr // HALO
    blk = pl.BlockSpec((tr, CONV_BLK), lambda j, i: (i, j))
    return _call(
        body, (proj, proj, conv_w, dqkv), name=name, grid=(nblk, l // tr),
        in_specs=[blk, pl.BlockSpec((HALO, CONV_BLK), lambda j, i: (jnp.maximum(i * hb - 1, 0), j)),
                  pl.BlockSpec((CONV_K, CONV_BLK), lambda j, i: (0, j)), blk],
        out_specs=[blk, pl.BlockSpec((CONV_K, CONV_BLK), lambda j, i: (0, j))],
        out_shape=[jax.ShapeDtypeStruct((l, DN_CONV_CH), f32), jax.ShapeDtypeStruct((CONV_K, DN_CONV_CH), f32)],
        scratch=[pltpu.VMEM((CONV_K, HALO, CONV_BLK), f32)], sem=("parallel", "arbitrary"), vmem_mb=40, comm=comm)


def _dn_conv_bwd_b(dc, conv_w, dproj, name):
    l = dc.shape[0]
    tr = _tile(l, 256)
    nblk = DN_CONV_CH // CONV_BLK
    nrow = l // tr

    n_slabs = tr // HALO
    pair = 2 * HALO

    def body(d_ref, n_ref, w_ref, _, o_ref):
        i = pl.program_id(0)
        w = w_ref[...]
        nxt_tile = jnp.where(i < nrow - 1, n_ref[...], 0.0)
        lrow = lax.broadcasted_iota(jnp.int32, (HALO, 1), 0)

        def one(r):
            cur = d_ref[_slab_rows(r), :]
            nxt = jnp.where(r < n_slabs - 1, d_ref[_slab_rows(jnp.minimum(r + 1, n_slabs - 1)), :], nxt_tile)
            acc = cur * w[CONV_K - 1:CONV_K, :]
            for s in range(1, CONV_K):
                up = jnp.where(lrow >= HALO - s, pltpu.roll(nxt, HALO - s, 0), pltpu.roll(cur, HALO - s, 0))
                acc = acc + up * w[CONV_K - 1 - s:CONV_K - s, :]
            return jnp.where(i * tr + r * HALO >= PAD, acc, 0.0)

        def two(q, carry):
            rows = pl.ds(pl.multiple_of(q * pair, pair), pair)
            o_ref[rows, :] = _b(jnp.concatenate([one(2 * q), one(2 * q + 1)], axis=0))
            return carry

        lax.fori_loop(0, n_slabs // 2, two, None, unroll=4)

    hb = tr // HALO
    nh = l // HALO
    return pl.pallas_call(
        body, name=name, grid=(nrow, nblk),
        in_specs=[pl.BlockSpec((tr, CONV_BLK), lambda i, j: (i, j)),
                  pl.BlockSpec((HALO, CONV_BLK), lambda i, j: (jnp.minimum((i + 1) * hb, nh - 1), j)),
                  pl.BlockSpec((CONV_K, CONV_BLK), lambda i, j: (0, j)),
                  pl.BlockSpec(memory_space=pl.ANY)],
        out_specs=pl.BlockSpec((tr, CONV_BLK), lambda i, j: (i, j)),
        out_shape=jax.ShapeDtypeStruct(dproj.shape, dproj.dtype), input_output_aliases={3: 0},
        compiler_params=_params(("parallel", "parallel"), 32),
    )(dc, dc, conv_w, dproj)


BA_W = LANES


def _dn_gates(ba_ref, al_ref, dt_ref, n):
    rows = n * CHUNK + lax.broadcasted_iota(jnp.int32, (CHUNK, 1), 0)
    vm = (rows >= PAD).astype(f32)
    bin_ = ba_ref[:, 0:DN_HEADS]
    z = ba_ref[:, DN_HEADS:2 * DN_HEADS] + dt_ref[...]
    sp = jnp.maximum(z, 0.0) + jnp.log1p(jnp.exp(-jnp.abs(z)))
    ea = jnp.exp(al_ref[...])
    beta = _sigmoid(bin_) * vm
    g = -ea * sp * vm
    return vm, bin_, z, ea, beta, g


def _tri():
    ri = lax.broadcasted_iota(jnp.int32, (CHUNK, CHUNK), 0)
    ci = lax.broadcasted_iota(jnp.int32, (CHUNK, CHUNK), 1)
    return ri, ci


def _split(a):
    hi = _b(a)
    return hi, _b(a - hi.astype(f32))


def _mm3(a, b, dot=_nn):
    (ah, al), (bh, bl) = _split(a), _split(b)
    return dot(ah, bh) + (dot(ah, bl) + dot(al, bh))


def _cumsum_rows(tri, g):
    tb = _b(tri)
    g1 = _b(g)
    r1 = g - g1.astype(f32)
    g2 = _b(r1)
    g3 = _b(r1 - g2.astype(f32))
    return _nn(tb, g1) + (_nn(tb, g2) + _nn(tb, g3))


DN_SCAN_CHUNKS = 3


def _scan_chunks(nch):
    return DN_SCAN_CHUNKS if nch % DN_SCAN_CHUNKS == 0 else 1


def _dn_prep(qkv, ba, a_log, dt_bias, name, comm=None):
    l = qkv.shape[0]
    nch = l // CHUNK
    heads = range(DN_HEADS)

    def body(q_ref, k_ref, v_ref, ba_ref, al_ref, dt_ref, t_ref, u_ref, wq_ref, pk_ref, eg_ref, kpt_ref, qwt_ref):
        n = pl.program_id(0)
        _, _, _, _, beta, g = _dn_gates(ba_ref, al_ref, dt_ref, n)
        ri, ci = _tri()
        incl, strict = ri >= ci, ri > ci
        eye = (ri == ci).astype(f32)
        gam = _cumsum_rows(incl.astype(f32), g)
        gam_t = gam.T
        gc = [gam[:, h:h + 1] for h in heads]
        bh = [beta[:, h:h + 1] for h in heads]
        kh = [k_ref[:, h * DN_DK:(h + 1) * DN_DK] for h in heads]
        kb = [_b(k) for k in kh]
        decay = [jnp.exp(jnp.where(incl, gc[h] - gam_t[h:h + 1, :], -jnp.inf)) for h in heads]
        a = [jnp.where(strict, bh[h] * _nt(kb[h], kb[h]) * decay[h], 0.0) for h in heads]
        t = [eye - a[h] for h in heads]
        p = a
        for _ in range(int(math.log2(CHUNK)) - 1):
            p = [_mm3(p[h], p[h]) for h in heads]
            t = [t[h] + _mm3(t[h], p[h]) for h in heads]
        eg = [jnp.exp(gc[h]) for h in heads]
        for h in heads:
            t_ref[0, h] = t[h]
            u_ref[:, h * DN_DV:(h + 1) * DN_DV] = _mm3(t[h], v_ref[:, h * DN_DV:(h + 1) * DN_DV] * bh[h])
            w = _mm3(t[h], kh[h] * (bh[h] * eg[h]))
            wq_ref[0, h, 0:CHUNK, :] = _b(w)
            qwt_ref[0, h, DN_DK:2 * DN_DK, :] = _b(w.T)
        for h in heads:
            qh = q_ref[:, h * DN_DK:(h + 1) * DN_DK]
            gl = gc[h][CHUNK - 1:CHUNK, :]
            qe = qh * eg[h]
            ke = kh[h] * jnp.exp(gl - gc[h])
            pmat = _nt(_b(qh), kb[h]) * decay[h]
            wq_ref[0, h, CHUNK:2 * CHUNK, :] = _b(qe)
            qwt_ref[0, h, 0:DN_DK, :] = _b(qe.T)
            pk_ref[0, h, 0:CHUNK, :] = _b(pmat)
            pk_ref[0, h, CHUNK:CHUNK + DN_DK, :] = _b(ke.T)
            kpt_ref[0, h, :, 0:DN_DK] = _b(ke)
            kpt_ref[0, h, :, DN_DK:DN_DK + CHUNK] = _b(pmat.T)
            eg_ref[0, h] = jnp.broadcast_to(jnp.exp(gl), (8, LANES))

    vec = pl.BlockSpec((1, DN_HEADS), lambda n: (0, 0))
    return _call(
        body, (qkv, qkv, qkv, ba, a_log, dt_bias), name=name, grid=(nch,),
        in_specs=[pl.BlockSpec((CHUNK, DN_QK), lambda n: (n, 0)), pl.BlockSpec((CHUNK, DN_QK), lambda n: (n, 1)),
                  pl.BlockSpec((CHUNK, DN_V), lambda n: (n, 1)), pl.BlockSpec((CHUNK, BA_W), lambda n: (n, 0)),
                  vec, vec],
        out_specs=[pl.BlockSpec((1, DN_HEADS, CHUNK, CHUNK), lambda n: (n, 0, 0, 0)),
                   pl.BlockSpec((CHUNK, DN_V), lambda n: (n, 0)),
                   pl.BlockSpec((1, DN_HEADS, 2 * CHUNK, DN_DK), lambda n: (n, 0, 0, 0)),
                   pl.BlockSpec((1, DN_HEADS, CHUNK + DN_DK, CHUNK), lambda n: (n, 0, 0, 0)),
                   pl.BlockSpec((1, DN_HEADS, 8, LANES), lambda n: (n, 0, 0, 0)),
                   pl.BlockSpec((1, DN_HEADS, CHUNK, DN_DK + CHUNK), lambda n: (n, 0, 0, 0)),
                   pl.BlockSpec((1, DN_HEADS, 2 * DN_DK, CHUNK), lambda n: (n, 0, 0, 0))],
        out_shape=[jax.ShapeDtypeStruct((nch, DN_HEADS, CHUNK, CHUNK), f32),
                   jax.ShapeDtypeStruct((l, DN_V), f32),
                   jax.ShapeDtypeStruct((nch, DN_HEADS, 2 * CHUNK, DN_DK), bf16),
                   jax.ShapeDtypeStruct((nch, DN_HEADS, CHUNK + DN_DK, CHUNK), bf16),
                   jax.ShapeDtypeStruct((nch, DN_HEADS, 8, LANES), f32),
                   jax.ShapeDtypeStruct((nch, DN_HEADS, CHUNK, DN_DK + CHUNK), bf16),
                   jax.ShapeDtypeStruct((nch, DN_HEADS, 2 * DN_DK, CHUNK), bf16)],
        sem=("parallel",), vmem_mb=40, comm=comm)


def _dn_scan_fwd(u, wq, pk, egl, name):
    l = u.shape[0]
    nch = l // CHUNK
    cs = _scan_chunks(nch)

    def body(u_ref, wq_ref, pk_ref, eg_ref, o_ref, st_ref, vn_ref, s_ref):
        @pl.when(pl.program_id(0) == 0)
        def _():
            s_ref[...] = jnp.zeros_like(s_ref)

        for c in range(cs):
            rows = slice(c * CHUNK, (c + 1) * CHUNK)
            for h in range(DN_HEADS):
                cols = slice(h * DN_DV, (h + 1) * DN_DV)
                s = s_ref[h]
                sb = _b(s)
                st_ref[c, h] = sb
                x = _nn(wq_ref[c, h], sb)
                vnb = _b(u_ref[rows, cols] - x[0:CHUNK])
                vn_ref[rows, cols] = vnb
                y = _nn(pk_ref[c, h], vnb)
                o_ref[rows, cols] = x[CHUNK:2 * CHUNK] + y[0:CHUNK]
                s_ref[h] = eg_ref[c, h][0:1, 0:1] * s + y[CHUNK:CHUNK + DN_DK]

    return pl.pallas_call(
        body, name=name, grid=(nch // cs,),
        in_specs=[pl.BlockSpec((cs * CHUNK, DN_V), lambda n: (n, 0)),
                  pl.BlockSpec((cs, DN_HEADS, 2 * CHUNK, DN_DK), lambda n: (n, 0, 0, 0)),
                  pl.BlockSpec((cs, DN_HEADS, CHUNK + DN_DK, CHUNK), lambda n: (n, 0, 0, 0)),
                  pl.BlockSpec((cs, DN_HEADS, 8, LANES), lambda n: (n, 0, 0, 0))],
        out_specs=[pl.BlockSpec((cs * CHUNK, DN_V), lambda n: (n, 0)),
                   pl.BlockSpec((cs, DN_HEADS, DN_DK, DN_DV), lambda n: (n, 0, 0, 0)),
                   pl.BlockSpec((cs * CHUNK, DN_V), lambda n: (n, 0))],
        out_shape=[jax.ShapeDtypeStruct((l, DN_V), f32),
                   jax.ShapeDtypeStruct((nch, DN_HEADS, DN_DK, DN_DV), bf16),
                   jax.ShapeDtypeStruct((l, DN_V), bf16)],
        scratch_shapes=[pltpu.VMEM((DN_HEADS, DN_DK, DN_DV), f32)],
        compiler_params=_params(("arbitrary",), 40),
    )(u, wq, pk, egl)


def _dn_scan_bwd(do, kpt, qwt, egl, name):
    l = do.shape[0]
    nch = l // CHUNK
    cs = _scan_chunks(nch)
    nblk = nch // cs

    def body(do_ref, kpt_ref, qwt_ref, eg_ref, dvn_ref, dsp_ref, ds_ref):
        @pl.when(pl.program_id(0) == 0)
        def _():
            ds_ref[...] = jnp.zeros_like(ds_ref)

        for c in reversed(range(cs)):
            rows = slice(c * CHUNK, (c + 1) * CHUNK)
            for h in range(DN_HEADS):
                cols = slice(h * DN_DV, (h + 1) * DN_DV)
                dsp = ds_ref[h]
                dspb = _b(dsp)
                dsp_ref[c, h] = dspb
                dob = _b(do_ref[rows, cols])
                kpt_h = kpt_ref[c, h]
                dvn = _nn(kpt_h[:, 0:DN_DK], dspb) + _nn(kpt_h[:, DN_DK:DN_DK + CHUNK], dob)
                dvn_ref[rows, cols] = dvn
                qwt_h = qwt_ref[c, h]
                ds_ref[h] = (eg_ref[c, h][0:1, 0:1] * dsp + _nn(qwt_h[0:DN_DK], dob)
                             - _nn(qwt_h[DN_DK:2 * DN_DK], _b(dvn)))

    rev = lambda s: nblk - 1 - s
    return pl.pallas_call(
        body, name=name, grid=(nblk,),
        in_specs=[pl.BlockSpec((cs * CHUNK, DN_V), lambda s: (rev(s), 0)),
                  pl.BlockSpec((cs, DN_HEADS, CHUNK, DN_DK + CHUNK), lambda s: (rev(s), 0, 0, 0)),
                  pl.BlockSpec((cs, DN_HEADS, 2 * DN_DK, CHUNK), lambda s: (rev(s), 0, 0, 0)),
                  pl.BlockSpec((cs, DN_HEADS, 8, LANES), lambda s: (rev(s), 0, 0, 0))],
        out_specs=[pl.BlockSpec((cs * CHUNK, DN_V), lambda s: (rev(s), 0)),
                   pl.BlockSpec((cs, DN_HEADS, DN_DK, DN_DV), lambda s: (rev(s), 0, 0, 0))],
        out_shape=[jax.ShapeDtypeStruct((l, DN_V), f32),
                   jax.ShapeDtypeStruct((nch, DN_HEADS, DN_DK, DN_DV), bf16)],
        scratch_shapes=[pltpu.VMEM((DN_HEADS, DN_DK, DN_DV), f32)],
        compiler_params=_params(("arbitrary",), 40),
    )(do, kpt, qwt, egl)


def _dn_post_bwd(qkv, ba, a_log, dt_bias, states, dsp_all, tinv_all, u_all, wq, vn_all, do, dvn_all, name):
    l = qkv.shape[0]
    nch = l // CHUNK

    def body(q_ref, k_ref, v_ref, ba_ref, al_ref, dt_ref, st_ref, dsp_ref, t_ref, u_ref, wq_ref, vn_ref, do_ref,
             dvn_ref, dqkv_ref, dba_ref, dal_ref, ddt_ref):
        step = pl.program_id(0)
        n = step

        vm, bin_, z, ea, beta, g = _dn_gates(ba_ref, al_ref, dt_ref, n)
        ri, ci = _tri()
        incl, strict = ri >= ci, ri > ci
        gam = _cumsum_rows(incl.astype(f32), g)
        gam_t = gam.T
        lane8 = lax.broadcasted_iota(jnp.int32, (1, DN_HEADS), 1)
        sub8 = lax.broadcasted_iota(jnp.int32, (DN_HEADS, 1), 0)
        dbeta = jnp.zeros((CHUNK, DN_HEADS), f32)
        dgam = jnp.zeros((CHUNK, DN_HEADS), f32)
        dgam_neg_t = jnp.zeros((DN_HEADS, CHUNK), f32)
        last = (lax.broadcasted_iota(jnp.int32, (CHUNK, 1), 0) == CHUNK - 1).astype(f32)
        hs = range(DN_HEADS)
        each = lambda fn: [fn(h) for h in hs]
        rsum = lambda t: jnp.sum(t, axis=-1, keepdims=True)
        gc = each(lambda h: gam[:, h:h + 1])
        bh = each(lambda h: beta[:, h:h + 1])
        qh = each(lambda h: q_ref[:, h * DN_DK:(h + 1) * DN_DK])
        kh = each(lambda h: k_ref[:, h * DN_DK:(h + 1) * DN_DK])
        doh = each(lambda h: _b(do_ref[:, h * DN_DV:(h + 1) * DN_DV]))
        sb = each(lambda h: st_ref[0, h])
        dspb = each(lambda h: dsp_ref[0, h])
        vnb = each(lambda h: vn_ref[:, h * DN_DV:(h + 1) * DN_DV])
        dvn = each(lambda h: dvn_ref[:, h * DN_DV:(h + 1) * DN_DV])
        wb = each(lambda h: wq_ref[0, h, 0:CHUNK, :])
        decay = each(lambda h: jnp.exp(jnp.where(incl, gc[h] - gam_t[h:h + 1, :], -jnp.inf)))
        qb, kb = each(lambda h: _b(qh[h])), each(lambda h: _b(kh[h]))
        eg = each(lambda h: jnp.exp(gc[h]))
        gl = each(lambda h: gc[h][CHUNK - 1:CHUNK, :])
        ekd = each(lambda h: jnp.exp(gl[h] - gc[h]))
        dvnb = each(lambda h: _b(dvn[h]))
        kk = each(lambda h: _nt(kb[h], kb[h]))
        p = each(lambda h: _nt(qb[h], kb[h]) * decay[h])
        dpraw = each(lambda h: _nt(doh[h], vnb[h]))
        dqe = each(lambda h: _nt(doh[h], sb[h]))
        dke = each(lambda h: _nt(vnb[h], dspb[h]))
        dw = each(lambda h: -_nt(dvnb[h], sb[h]))
        dru = each(lambda h: _mm3(t_ref[0, h], dvn[h], _tn))
        drw = each(lambda h: _mm3(t_ref[0, h], dw[h], _tn))
        dqk = each(lambda h: _b(dpraw[h] * decay[h]))
        for h in hs:
            dqkv_ref[:, h * DN_DK:(h + 1) * DN_DK] = _nn(dqk[h], kb[h]) + dqe[h] * eg[h]
            dqkv_ref[:, 2 * DN_QK + h * DN_DV:2 * DN_QK + (h + 1) * DN_DV] = bh[h] * dru[h]
        da = each(lambda h: jnp.where(strict, -(_nt(_b(dru[h]), _b(u_ref[:, h * DN_DV:(h + 1) * DN_DV]))
                                                + _nt(_b(drw[h]), wb[h])), 0.0))
        dkk = each(lambda h: _b(da[h] * bh[h] * decay[h]))
        for h in hs:
            dqkv_ref[:, DN_QK + h * DN_DK:DN_QK + (h + 1) * DN_DK] = (
                _tn(dqk[h], qb[h]) + dke[h] * ekd[h] + (bh[h] * eg[h]) * drw[h]
                + _nn(dkk[h], kb[h]) + _tn(dkk[h], kb[h]))
        for h in hs:
            keg = kh[h] * eg[h]
            ke = kh[h] * ekd[h]
            rw = rsum(drw[h] * keg)
            rke = rsum(dke[h] * ke)
            db_h = rsum(dru[h] * v_ref[:, h * DN_DV:(h + 1) * DN_DV]) + rw + rsum(da[h] * kk[h] * decay[h])
            mm = da[h] * (bh[h] * kk[h] * decay[h]) + dpraw[h] * p[h]
            dgl = (jnp.sum(rke, axis=0, keepdims=True)
                   + jnp.exp(gl[h]) * jnp.sum(rsum(dspb[h].astype(f32) * sb[h].astype(f32)), axis=0, keepdims=True))
            dg_h = rsum(mm) + rw * bh[h] + rsum(dqe[h] * (qh[h] * eg[h])) - rke + last * dgl
            dbeta = dbeta + jnp.where(lane8 == h, db_h, 0.0)
            dgam = dgam + jnp.where(lane8 == h, dg_h, 0.0)
            dgam_neg_t = dgam_neg_t + jnp.where(sub8 == h, jnp.sum(mm, axis=0, keepdims=True), 0.0)
        dgam = dgam - dgam_neg_t.T
        dg = _cumsum_rows((ri <= ci).astype(f32), dgam)
        sg = _sigmoid(bin_)
        dbin = dbeta * vm * sg * (1.0 - sg)
        dain = dg * (-ea) * vm * _sigmoid(z)
        dba_ref[...] = jnp.zeros_like(dba_ref)
        dba_ref[:, 0:DN_HEADS] = dbin
        dba_ref[:, DN_HEADS:2 * DN_HEADS] = dain
        dal = jnp.sum(dg * g, axis=0, keepdims=True)
        ddt = jnp.sum(dain, axis=0, keepdims=True)

        @pl.when(step == 0)
        def _():
            dal_ref[...] = dal
            ddt_ref[...] = ddt

        @pl.when(step > 0)
        def _():
            dal_ref[...] += dal
            ddt_ref[...] += ddt

    vec = pl.BlockSpec((1, DN_HEADS), lambda s: (0, 0))
    qs = pl.BlockSpec((CHUNK, DN_QK), lambda s: (s, 0))
    ks = pl.BlockSpec((CHUNK, DN_QK), lambda s: (s, 1))
    vs = pl.BlockSpec((CHUNK, DN_V), lambda s: (s, 1))
    v0 = pl.BlockSpec((CHUNK, DN_V), lambda s: (s, 0))
    st = pl.BlockSpec((1, DN_HEADS, DN_DK, DN_DV), lambda s: (s, 0, 0, 0))
    return pl.pallas_call(
        body, name=name, grid=(nch,),
        in_specs=[qs, ks, vs, pl.BlockSpec((CHUNK, BA_W), lambda s: (s, 0)), vec, vec, st, st,
                  pl.BlockSpec((1, DN_HEADS, CHUNK, CHUNK), lambda s: (s, 0, 0, 0)),
                  v0, pl.BlockSpec((1, DN_HEADS, 2 * CHUNK, DN_DK), lambda s: (s, 0, 0, 0)), v0, v0, v0],
        out_specs=[pl.BlockSpec((CHUNK, DN_CONV_CH), lambda s: (s, 0)),
                   pl.BlockSpec((CHUNK, BA_W), lambda s: (s, 0)), vec, vec],
        out_shape=[jax.ShapeDtypeStruct((l, DN_CONV_CH), f32), jax.ShapeDtypeStruct((l, BA_W), f32),
                   jax.ShapeDtypeStruct((1, DN_HEADS), f32), jax.ShapeDtypeStruct((1, DN_HEADS), f32)],
        compiler_params=_params(("arbitrary",), 48),
    )(qkv, qkv, qkv, ba, a_log, dt_bias, states, dsp_all, tinv_all, u_all, wq, vn_all, do, dvn_all)


def _ffn_fwd(h, nw, wg, wu, wd, tb, th, tag, plan):
    fh = wd.shape[0]
    hn = _rms_fwd(h, nw, f"{tag}_norm")
    a, b, s = plan.call(f"{tag}_gu", functools.partial(_ffn_gu, tm=th // 2, tn=fh // 2), hn, wg, wu, n_out=3)
    out = plan.matmul(f"{tag}_down", s, wd, mode="nn", tm=th, tn=512, tk=fh, res=h)
    return out, (hn, a, b, s)


def _ffn_bwd(dh, dhb, h, nw, wg, wu, wd, saved, tb, th, tag, plan):
    hn, a, b, s = saved
    d = h.shape[1]
    fh = wd.shape[0]
    layer = tag[-1]
    gr = plan.grads
    da, db = _ffn_ds(dhb, wd, a, b, tm=th // 2, tn=fh // 2, name=f"{tag}_b_ds")
    gr["down" + layer] = _matmul(s, dhb, mode="tn", tm=fh // 2, tn=d, tk=th, out_dtype=bf16, name=f"{tag}_b_dwd")
    dh2, dh2b, dnw = plan.call(f"{tag}_b_dhn", functools.partial(_dhn_norm_bwd, tm=th // 2, tk=fh // 2),
                               [(da, wg), (db, wu)], h, nw, dh, n_out=3)
    gr["gate" + layer] = _matmul(hn, da, mode="tn", tm=d, tn=fh // 2, tk=th, out_dtype=bf16, name=f"{tag}_b_dwg")
    gr["up" + layer] = _matmul(hn, db, mode="tn", tm=d, tn=fh // 2, tk=th, out_dtype=bf16, name=f"{tag}_b_dwu")
    return dh2, dh2b, dnw


class _Plan:
    GATHERS = {"ret_proj": ("ret_out", "gate0"), "ret_scan": ("up0", "down0"), "ffn0_gu": ("dn_in_top",),
               "ffn0_down": ("dn_in_bottom",), "dn_proj": ("dn_out",), "dn_conv": ("gate1",),
               "dn_prep": ("up1", "down1")}
    SCATTERS = {"ffn1_b_dhn": ("down1",), "dn_b_conv_a": ("gate1", "up1", "dn_out"), "ffn0_b_dhn": ("dn_in",),
                "ret_b_scan": ("gate0", "up0"), "ret_b_dwin": ("down0", "ret_out"), "ret_b_dhn": ("ret_in",)}

    def __init__(self, shards, wts):
        self.shards, self.wts, self.grads, self.parts = shards, wts, {}, {}

    def _exchange(self, stage):
        if self.shards is None:
            return None
        if stage in self.GATHERS:
            return _Exchange([self.shards[n] for n in self.GATHERS[stage]], True)
        if stage in self.SCATTERS:
            return _Exchange([self._dev_major(n) for n in self.SCATTERS[stage]], False)
        return None

    def _dev_major(self, name):
        g = self.grads
        if name[:-1] in ("gate", "up"):
            return _dev_major_cols(g[name], g[name].shape[1] // N_DEV)
        if name[:-1] == "down":
            dwd = g[name]
            return dwd.reshape(N_DEV, dwd.shape[0] // N_DEV, dwd.shape[1])
        if name in ("ret_out", "dn_out"):
            return g[name].reshape(N_DEV, g[name].shape[0] // N_DEV, g[name].shape[1])
        return _dev_major_cols(g[name], self.shards[name].shape[-1])

    def _landed(self, stage, outs):
        if stage in self.SCATTERS:
            self.parts.update(zip(self.SCATTERS[stage], outs))
            return
        w = self.wts
        cols = lambda t: t.transpose(1, 0, 2).reshape(t.shape[1], N_DEV * t.shape[2])
        rows = lambda t: t.reshape(N_DEV * t.shape[1], t.shape[2])
        for name, t in zip(self.GATHERS[stage], outs):
            if name in ("ret_out", "dn_out") or name.startswith("down"):
                w[name] = rows(t)
            else:
                w[name] = cols(t)
        if "dn_in_top" in w and "dn_in_bottom" in w and "dn_main" not in w:
            full = jnp.concatenate([w["dn_in_top"], w["dn_in_bottom"]], axis=0)
            n_main = DN_CONV_CH + DN_V
            w["dn_main"] = full[:, :n_main]
            w["dn_ba"] = jnp.pad(full[:, n_main:], ((0, 0), (0, BA_W - (full.shape[1] - n_main))))

    def matmul(self, stage, a, b, **kw):
        comm = self._exchange(stage)
        if comm is None:
            return _matmul(a, b, name=stage, **kw)
        out, landed = _matmul(a, b, name=stage, comm=comm, **kw)
        self._landed(stage, landed)
        return out

    def call(self, stage, fn, *args, n_out):
        comm = self._exchange(stage)
        out = fn(*args, stage, comm=comm)
        if comm is not None:
            self._landed(stage, out[n_out:])
        return out[:n_out]


def _local_step(x2, target, wts, shards=None):
    plan = _Plan(shards, wts)
    s_len, d = x2.shape
    l = s_len + CHUNK
    tb = _tile(l, 3072)
    th = tb // 2 if (tb // 2) % 16 == 0 else tb
    half = RET_DK // 2
    inv_freq = (np.float32(ROPE_BASE) ** (-np.arange(half, dtype=np.float32) / np.float32(half))).astype(np.float32)
    ang = (np.arange(l) - PAD).astype(np.float32)[:, None] * inv_freq[None, :]
    cos, sin = jnp.asarray(np.cos(ang), f32), jnp.asarray(np.sin(ang), f32)
    lgs = jnp.log1p(-jnp.exp2(-5.0 - jnp.arange(RET_HEADS, dtype=f32)))
    gcs = jnp.exp(lgs * _ret_block(l))

    h0 = jnp.concatenate([jnp.zeros((PAD, d), f32), wts["meta"], x2], axis=0)
    mixw, ffnw = wts["mix_norm"], wts["ffn_norm"]

    hn0 = _rms_fwd(h0, mixw[0:1], "l0_norm")
    proj0 = plan.matmul("ret_proj", hn0, wts["ret_in"], mode="nn", tm=tb, tn=512, tk=d)
    qk0 = _ret_prep(proj0, cos, sin, "ret_prep")
    o0, st0 = plan.call("ret_scan", _ret_scan_fwd, qk0, proj0, lgs, gcs, n_out=2)
    y0 = _gnorm_fwd(o0, proj0, wts["ret_gn"], RET_HEADS, RET_DV, 2, "ret_gnorm")
    h1 = _matmul(y0, wts["ret_out"], mode="nn", tm=th, tn=512, tk=RET_V, res=h0, name="ret_out")
    h2, ffn0 = _ffn_fwd(h1, ffnw[0:1], wts["gate0"], wts["up0"], wts["down0"], tb, th, "ffn0", plan)

    hn2 = _rms_fwd(h2, mixw[1:2], "l1_norm")
    proj1 = plan.matmul("dn_proj", hn2, wts["dn_main"], mode="nn", tm=tb, tn=512, tk=d)
    ba = _matmul(hn2, wts["dn_ba"], mode="nn", tm=tb, tn=BA_W, tk=d, name="dn_proj_ba")
    (qkv1,) = plan.call("dn_conv", _dn_conv_fwd, proj1, wts["conv_w"], n_out=1)
    tinv1, u1, wq1, pk1, egl1, kpt1, qwt1 = plan.call("dn_prep", _dn_prep, qkv1, ba, wts["a_log"], wts["dt_bias"],
                                                      n_out=7)
    o1, st1, vn1 = _dn_scan_fwd(u1, wq1, pk1, egl1, "dn_scan")
    y1 = _gnorm_fwd(o1, proj1, wts["dn_norm"], DN_HEADS, DN_DV, 2, "dn_gnorm")
    h3 = _matmul(y1, wts["dn_out"], mode="nn", tm=th, tn=512, tk=DN_V, res=h2, name="dn_out")
    h4, ffn1 = _ffn_fwd(h3, ffnw[1:2], wts["gate1"], wts["up1"], wts["down1"], tb, th, "ffn1", plan)

    dh4, dh4b, dfinal, loss = _final_loss(h4, wts["final_norm"], target, "final_loss")
    gr = plan.grads
    dh3, dh3b, dffn1 = _ffn_bwd(dh4, dh4b, h3, ffnw[1:2], wts["gate1"], wts["up1"], wts["down1"], ffn1,
                                tb, th, "ffn1", plan)

    dy1 = _matmul(dh3b, wts["dn_out"], mode="nt", tm=th, tn=1024, tk=d, name="dn_b_dy")
    gr["dn_out"] = _matmul(y1, dh3b, mode="tn", tm=1024, tn=d, tk=tb, out_dtype=bf16, name="dn_b_dwout")
    do1, dproj1, ddn_norm = _gnorm_bwd(o1, proj1, wts["dn_norm"], dy1, DN_HEADS, DN_DV, 2, "dn_b_gnorm")
    dvn1, dsp1 = _dn_scan_bwd(do1, kpt1, qwt1, egl1, "dn_b_scan")
    dqkv1, dba, dalog, ddt = _dn_post_bwd(qkv1, ba, wts["a_log"], wts["dt_bias"], st1, dsp1, tinv1, u1, wq1, vn1,
                                          do1, dvn1, "dn_b_post")
    dc1, dconv = plan.call("dn_b_conv_a", _dn_conv_bwd_a, proj1, wts["conv_w"], dqkv1, n_out=2)
    dproj1 = _dn_conv_bwd_b(dc1, wts["conv_w"], dproj1, "dn_b_conv_b")
    dbab = dba.astype(bf16)
    n_main = dproj1.shape[1]
    dhn2_ba = _matmul(dbab, wts["dn_ba"], mode="nt", tm=th, tn=d, tk=BA_W, name="dn_b_dhn_ba")
    dh2, dh2b, dmix1 = plan.call("dn_b_dhn", functools.partial(_dhn_norm_bwd, tm=th // 2, tk=n_main // 4,
                                                              init=dhn2_ba),
                                 [(dproj1, wts["dn_main"])], h2, mixw[1:2], dh3, n_out=3)
    dw_main = _matmul(hn2, dproj1, mode="tn", tm=d, tn=512, tk=tb, out_dtype=bf16, name="dn_b_dwin")
    dw_ba = _matmul(hn2, dbab, mode="tn", tm=d, tn=BA_W, tk=tb, out_dtype=bf16, name="dn_b_dwin_ba")
    gr["dn_in"] = jnp.concatenate([dw_main, dw_ba], axis=1)

    dh1, dh1b, dffn0 = _ffn_bwd(dh2, dh2b, h1, ffnw[0:1], wts["gate0"], wts["up0"], wts["down0"], ffn0,
                                tb, th, "ffn0", plan)

    dy0 = _matmul(dh1b, wts["ret_out"], mode="nt", tm=th, tn=1024, tk=d, name="ret_b_dy")
    gr["ret_out"] = _matmul(y0, dh1b, mode="tn", tm=1024, tn=d, tk=tb, out_dtype=bf16, name="ret_b_dwout")
    do0, dproj0, dret_gn = _gnorm_bwd(o0, proj0, wts["ret_gn"], dy0, RET_HEADS, RET_DV, 2, "ret_b_gnorm")
    dq0, dk0, dproj0 = plan.call("ret_b_scan", _ret_scan_bwd, qk0, proj0, st0, do0, dproj0, lgs, gcs, n_out=3)
    dproj0 = _ret_prep_bwd(dq0, dk0, cos, sin, dproj0, "ret_b_prep")
    n_in = dproj0.shape[1]
    gr["ret_in"] = plan.matmul("ret_b_dwin", hn0, dproj0, mode="tn", tm=d, tn=512, tk=tb, out_dtype=bf16)
    dh0, _, dmix0 = plan.call("ret_b_dhn", functools.partial(_dhn_norm_bwd, tm=th // 2, tk=n_in // 4),
                              [(dproj0, wts["ret_in"])], h0, mixw[0:1], dh1, n_out=3)

    gr.update(meta=dh0[PAD:CHUNK], mix_norm=jnp.concatenate([dmix0, dmix1], axis=0),
              ffn_norm=jnp.concatenate([dffn0, dffn1], axis=0), ret_gn=dret_gn, conv_w=dconv, a_log=dalog,
              dt_bias=ddt, dn_norm=ddn_norm, final_norm=dfinal)
    return loss, dh0[CHUNK:], gr, plan


def _adamw_reduce(parts, w, m, v, name):
    _, r, c = parts.shape
    c_pad = -(-c // LANES) * LANES
    tr = _div_tile(r, max(8, (3 * MIB // 16) // c_pad // 8 * 8), 16)

    def body(p_ref, w_ref, m_ref, v_ref, g_ref, d_ref, nm_ref, nv_ref):
        g = p_ref[0].astype(f32)
        for s in range(1, N_DEV):
            g = g + p_ref[s].astype(f32)
        mm = ADAM_B1 * m_ref[...] + (1.0 - ADAM_B1) * g
        vv = ADAM_B2 * v_ref[...] + (1.0 - ADAM_B2) * (g * g)
        m_hat = mm / (1.0 - ADAM_B1 ** ADAM_STEP)
        v_hat = vv / (1.0 - ADAM_B2 ** ADAM_STEP)
        g_ref[...] = g
        d_ref[...] = -ADAM_LR * (m_hat / (jnp.sqrt(v_hat) + ADAM_EPS) + ADAM_WD * w_ref[...])
        nm_ref[...] = mm
        nv_ref[...] = vv

    blk = pl.BlockSpec((tr, c), lambda i: (i, 0))
    return pl.pallas_call(
        body, name=name, grid=(r // tr,),
        in_specs=[pl.BlockSpec((N_DEV, tr, c), lambda i: (0, i, 0)), blk, blk, blk], out_specs=[blk] * 4,
        out_shape=[jax.ShapeDtypeStruct((r, c), f32)] * 4,
        compiler_params=_params(("parallel",), 48),
    )(parts, w, m, v)


def _dev_major_cols(g, width):
    r = g.shape[0]
    return g[:, :N_DEV * width].reshape(r, N_DEV, width).transpose(1, 0, 2)


def kernel(x, meta_tokens, mix_norm_w, ffn_norm_w, ret_w_in, ret_gn_w, ret_w_out, dn_w_in, dn_conv_w, dn_a_log, dn_dt_bias, dn_norm_w, dn_w_out, ffn_w_gate, ffn_w_up, ffn_w_down, final_norm_w, loss_target, m_meta_tokens, m_mix_norm_w, m_ffn_norm_w, m_ret_w_in, m_ret_gn_w, m_ret_w_out, m_dn_w_in, m_dn_conv_w, m_dn_a_log, m_dn_dt_bias, m_dn_norm_w, m_dn_w_out, m_ffn_w_gate, m_ffn_w_up, m_ffn_w_down, m_final_norm_w, v_meta_tokens, v_mix_norm_w, v_ffn_norm_w, v_ret_w_in, v_ret_gn_w, v_ret_w_out, v_dn_w_in, v_dn_conv_w, v_dn_a_log, v_dn_dt_bias, v_dn_norm_w, v_dn_w_out, v_ffn_w_gate, v_ffn_w_up, v_ffn_w_down, v_final_norm_w):
    d = x.shape[-1]
    me = 4 * lax.axis_index("x") + 2 * lax.axis_index("y") + lax.axis_index("c")

    shards = dict(ret_in=ret_w_in[0].astype(bf16), ret_out=ret_w_out[0].astype(bf16),
                  dn_in=dn_w_in[0].astype(bf16), dn_out=dn_w_out[0].astype(bf16))
    shards["dn_in_top"], shards["dn_in_bottom"] = shards["dn_in"][:d // 2], shards["dn_in"][d // 2:]
    for layer in (0, 1):
        shards[f"gate{layer}"] = ffn_w_gate[layer].astype(bf16)
        shards[f"up{layer}"] = ffn_w_up[layer].astype(bf16)
        shards[f"down{layer}"] = ffn_w_down[layer].astype(bf16)
    g_ret_in, g_meta, g_conv, g_dnn = _exchange([shards["ret_in"], meta_tokens, dn_conv_w[0], dn_norm_w], True,
                                                "gather_first")
    cols = lambda g: g.transpose(1, 0, 2).reshape(g.shape[1], N_DEV * g.shape[2])
    wts = dict(meta=cols(g_meta), mix_norm=mix_norm_w, ffn_norm=ffn_norm_w, ret_in=cols(g_ret_in), ret_gn=ret_gn_w,
               conv_w=cols(g_conv), a_log=dn_a_log, dt_bias=dn_dt_bias, dn_norm=cols(g_dnn),
               final_norm=final_norm_w.reshape(1, d))

    loss_part, grad_x, gr, plan = _local_step(x[0], loss_target[0], wts, shards)
    loss = lax.psum(loss_part[0, 0], AXES)

    pp = plan.parts
    both = lambda name: jnp.concatenate([pp[name + "0"], pp[name + "1"]], axis=1)
    big_parts = [pp["ret_in"], pp["ret_out"], pp["dn_in"], pp["dn_out"], both("gate"), both("up"), both("down")]
    big_names = ["ret_w_in", "ret_w_out", "dn_w_in", "dn_w_out", "ffn_w_gate", "ffn_w_up", "ffn_w_down"]
    big_w = [ret_w_in, ret_w_out, dn_w_in, dn_w_out, ffn_w_gate, ffn_w_up, ffn_w_down]
    big_m = [m_ret_w_in, m_ret_w_out, m_dn_w_in, m_dn_w_out, m_ffn_w_gate, m_ffn_w_up, m_ffn_w_down]
    big_v = [v_ret_w_in, v_ret_w_out, v_dn_w_in, v_dn_w_out, v_ffn_w_gate, v_ffn_w_up, v_ffn_w_down]
    res = {}
    for nm, parts, w_, m_, v_ in zip(big_names, big_parts, big_w, big_m, big_v):
        r2, c2 = parts.shape[1], parts.shape[2]
        outs = _adamw_reduce(parts, w_.reshape(r2, c2), m_.reshape(r2, c2), v_.reshape(r2, c2), f"adamw_{nm}")
        res[nm] = [o.reshape(w_.shape) for o in outs]

    small_names = ["meta_tokens", "mix_norm_w", "ffn_norm_w", "ret_gn_w", "dn_conv_w", "dn_a_log", "dn_dt_bias",
                   "dn_norm_w", "final_norm_w"]
    small_g = [gr["meta"], gr["mix_norm"], gr["ffn_norm"], gr["ret_gn"], gr["conv_w"], gr["a_log"], gr["dt_bias"],
               gr["dn_norm"], gr["final_norm"]]
    small_w = [meta_tokens, mix_norm_w, ffn_norm_w, ret_gn_w, dn_conv_w, dn_a_log, dn_dt_bias, dn_norm_w, final_norm_w]
    small_m = [m_meta_tokens, m_mix_norm_w, m_ffn_norm_w, m_ret_gn_w, m_dn_conv_w, m_dn_a_log, m_dn_dt_bias,
               m_dn_norm_w, m_final_norm_w]
    small_v = [v_meta_tokens, v_mix_norm_w, v_ffn_norm_w, v_ret_gn_w, v_dn_conv_w, v_dn_a_log, v_dn_dt_bias,
               v_dn_norm_w, v_final_norm_w]
    sharded = {"meta_tokens", "dn_conv_w", "dn_norm_w"}
    flat = jnp.concatenate([g.reshape(-1) for g in small_g])
    row = 8 * LANES
    n_flat = flat.shape[0]
    flat = jnp.pad(flat, (0, -n_flat % row)).reshape(-1, row)
    (gathered,) = _exchange([flat], True, "gather_small_grads")
    gathered = gathered.reshape(N_DEV, -1)
    pieces, off = [], 0
    for nm, g, w_ in zip(small_names, small_g, small_w):
        full = gathered[:, off:off + g.size].reshape((N_DEV,) + g.shape)
        off += g.size
        if nm in sharded:
            wloc = w_.shape[-1]
            full = lax.dynamic_slice_in_dim(full, me * wloc, wloc, axis=full.ndim - 1)
        pieces.append(full.reshape(N_DEV, -1))
    sizes = [p.shape[1] for p in pieces]
    n_loc = sum(sizes)
    pad_loc = -n_loc % row

    def pack(vs, lead):
        cat = jnp.concatenate([a.reshape(lead + (-1,)) for a in vs], axis=-1)
        cat = jnp.pad(cat, [(0, 0)] * len(lead) + [(0, pad_loc)])
        return cat.reshape(lead + (-1, row))

    outs = _adamw_reduce(pack(pieces, (N_DEV,)), pack(small_w, ()), pack(small_m, ()), pack(small_v, ()), "adamw_small")
    off = 0
    for nm, sz, w_ in zip(small_names, sizes, small_w):
        res[nm] = [o.reshape(-1)[off:off + sz].reshape(w_.shape) for o in outs]
        off += sz

    order = ["meta_tokens", "mix_norm_w", "ffn_norm_w", "ret_w_in", "ret_gn_w", "ret_w_out", "dn_w_in", "dn_conv_w",
             "dn_a_log", "dn_dt_bias", "dn_norm_w", "dn_w_out", "ffn_w_gate", "ffn_w_up", "ffn_w_down", "final_norm_w"]
    grad_x = grad_x.reshape(x.shape)
    return (loss, grad_x, *[res[nm][0] for nm in order], *[res[nm][1] for nm in order],
            *[res[nm][2] for nm in order], *[res[nm][3] for nm in order])
```

```python
import functools
import math

import jax
import jax.numpy as jnp
import numpy as np
from jax import lax
from jax.experimental import pallas as pl
from jax.experimental.pallas import tpu as pltpu

f32 = jnp.float32
bf16 = jnp.bfloat16
HI = lax.Precision.HIGHEST

N_META = 16
CHUNK = 64
PAD = CHUNK - N_META
RMS_EPS = 1e-6
RET_HEADS, RET_DK, RET_DV = 4, 256, 512
RET_QK, RET_V = RET_HEADS * RET_DK, RET_HEADS * RET_DV
DN_HEADS, DN_DK, DN_DV = 8, 128, 256
DN_QK, DN_V = DN_HEADS * DN_DK, DN_HEADS * DN_DV
DN_CONV_CH = 2 * DN_QK + DN_V
CONV_K = 4
ROPE_BASE = 10000.0
ADAM_LR, ADAM_B1, ADAM_B2, ADAM_EPS, ADAM_WD, ADAM_STEP = 0.001, 0.9, 0.999, 1e-08, 0.01, 10
N_DEV = 8
AXES = ("x", "y", "c")
LANES = 128
MIB = 1024 * 1024


def _tile(n_rows, cap):
    nch = n_rows // CHUNK
    best = 1
    for d in range(1, nch + 1):
        if nch % d == 0 and d * CHUNK <= cap:
            best = d
    return best * CHUNK


def _div_tile(n, cap, align):
    best = None
    for d in range(align, min(n, cap) + 1, align):
        if n % d == 0:
            best = d
    return best if best is not None else n


def _params(sem, vmem_mb):
    return pltpu.CompilerParams(dimension_semantics=sem, vmem_limit_bytes=int(vmem_mb * MIB))


def _nn(a, b, precision=None):
    return jnp.dot(a, b, preferred_element_type=f32, precision=precision)


def _nt(a, b, precision=None):
    return lax.dot_general(a, b, (((1,), (1,)), ((), ())), preferred_element_type=f32, precision=precision)


def _tn(a, b, precision=None):
    return lax.dot_general(a, b, (((0,), (0,)), ((), ())), preferred_element_type=f32, precision=precision)


def _b(x):
    return x.astype(bf16)


def _sigmoid(x):
    return 0.5 * jnp.tanh(0.5 * x) + 0.5


def _silu(x):
    return x * _sigmoid(x)


def _dsilu(x):
    s = _sigmoid(x)
    return s * (1.0 + x * (1.0 - s))


def _peer(k):
    x, y, c = lax.axis_index("x"), lax.axis_index("y"), lax.axis_index("c")
    px = 1 - x if k & 4 else x
    py = 1 - y if k & 2 else y
    pc = 1 - c if k & 1 else c
    return (px, py, pc), 4 * px + 2 * py + pc


class _Exchange:
    def __init__(self, arrs, gather):
        self.arrs, self.gather, self.n = list(arrs), gather, len(arrs)
        self.out_shapes = [jax.ShapeDtypeStruct(((N_DEV,) + a.shape) if gather else a.shape, a.dtype) for a in arrs]
        self.specs = [pl.BlockSpec(memory_space=pltpu.HBM)] * self.n
        self.scratch = [pltpu.SemaphoreType.DMA((self.n, N_DEV - 1)), pltpu.SemaphoreType.DMA((self.n, N_DEV - 1)),
                        pltpu.SemaphoreType.DMA((self.n,))]

    def _copies(self, ins, outs, sems):
        send_sems, recv_sems, local_sems = sems
        me = 4 * lax.axis_index("x") + 2 * lax.axis_index("y") + lax.axis_index("c")
        src = (lambda a, dest: ins[a]) if self.gather else (lambda a, dest: ins[a].at[dest])
        local = [pltpu.make_async_copy(src(a, me), outs[a].at[me], local_sems.at[a]) for a in range(self.n)]
        sends, lands = [], []
        for k in range(1, N_DEV):
            peer, pidx = _peer(k)
            for a in range(self.n):
                for dst, lst in ((outs[a].at[me], sends), (outs[a].at[pidx], lands)):
                    lst.append(pltpu.make_async_remote_copy(
                        src_ref=src(a, pidx), dst_ref=dst, send_sem=send_sems.at[a, k - 1],
                        recv_sem=recv_sems.at[a, k - 1], device_id=peer, device_id_type=pl.DeviceIdType.MESH))
        return local, sends, lands

    def start(self, ins, outs, sems):
        local, sends, _ = self._copies(ins, outs, sems)
        for cp in local + sends:
            cp.start()

    def wait(self, ins, outs, sems):
        local, sends, lands = self._copies(ins, outs, sems)
        for cp in lands:
            cp.wait_recv()
        for cp in sends:
            cp.wait_send()
        for cp in local:
            cp.wait()


def _call(body, args, *, name, grid, in_specs, out_specs, out_shape, scratch=(), sem, vmem_mb, comm=None,
          aliases=None):
    aliases = aliases or {}
    if comm is None:
        out = pl.pallas_call(body, name=name, grid=grid, in_specs=list(in_specs), out_specs=list(out_specs),
                             out_shape=list(out_shape), scratch_shapes=list(scratch), input_output_aliases=aliases,
                             compiler_params=_params(sem, vmem_mb))(*args)
        return list(out)
    n_in, n_out, n_scr, nc = len(args), len(out_shape), len(scratch), comm.n

    def carried(*refs):
        ins, cin = refs[:n_in], refs[n_in:n_in + nc]
        o0 = n_in + nc
        outs, cout = refs[o0:o0 + n_out], refs[o0 + n_out:o0 + n_out + nc]
        s0 = o0 + n_out + nc
        scr, sems = refs[s0:s0 + n_scr], refs[s0 + n_scr:]
        first = functools.reduce(jnp.logical_and, [pl.program_id(i) == 0 for i in range(len(grid))])
        last = functools.reduce(jnp.logical_and, [pl.program_id(i) == grid[i] - 1 for i in range(len(grid))])

        @pl.when(first)
        def _():
            comm.start(cin, cout, sems)

        body(*ins, *outs, *scr)

        @pl.when(last)
        def _():
            comm.wait(cin, cout, sems)

    out = pl.pallas_call(
        carried, name=name, grid=grid, in_specs=list(in_specs) + comm.specs, out_specs=list(out_specs) + comm.specs,
        out_shape=list(out_shape) + comm.out_shapes, scratch_shapes=list(scratch) + comm.scratch,
        input_output_aliases=aliases,
        compiler_params=_params(("arbitrary",) * len(grid), vmem_mb))(*args, *comm.arrs)
    return list(out)


def _exchange(arrs, gather, name):
    comm = _Exchange(arrs, gather)

    def body(*refs):
        ins, outs, sems = refs[:comm.n], refs[comm.n:2 * comm.n], refs[2 * comm.n:]
        comm.start(ins, outs, sems)
        comm.wait(ins, outs, sems)

    return pl.pallas_call(body, name=name, in_specs=comm.specs, out_specs=comm.specs, out_shape=comm.out_shapes,
                          scratch_shapes=comm.scratch)(*comm.arrs)


def _matmul(a, b, *, mode, tm, tn, tk, name, out_dtype=f32, res=None, vmem_mb=48, comm=None, pair2=None,
            norm_w=None):
    def dims(x, y):
        if mode == "nn":
            (m_, k_), (k2_, n_) = x.shape, y.shape
        elif mode == "nt":
            (m_, k_), (n_, k2_) = x.shape, y.shape
        else:
            (k_, m_), (k2_, n_) = x.shape, y.shape
        assert k_ == k2_ and m_ % tm == 0 and n_ % tn == 0 and k_ % tk == 0, (name, x.shape, y.shape, tm, tn, tk)
        return m_, n_, k_ // tk

    m, n, nk1 = dims(a, b)
    nk2 = dims(*pair2)[2] if pair2 is not None else 0
    nk = nk1 + nk2
    dot = {"nn": _nn, "nt": _nt, "tn": _tn}[mode]

    def specs(k_of):
        a_spec = {"nn": pl.BlockSpec((tm, tk), lambda i, j, kk: (i, k_of(kk))),
                  "nt": pl.BlockSpec((tm, tk), lambda i, j, kk: (i, k_of(kk))),
                  "tn": pl.BlockSpec((tk, tm), lambda i, j, kk: (k_of(kk), i))}[mode]
        b_spec = {"nn": pl.BlockSpec((tk, tn), lambda i, j, kk: (k_of(kk), j)),
                  "nt": pl.BlockSpec((tn, tk), lambda i, j, kk: (j, k_of(kk))),
                  "tn": pl.BlockSpec((tk, tn), lambda i, j, kk: (k_of(kk), j))}[mode]
        return [a_spec, b_spec]

    o_spec = pl.BlockSpec((tm, tn), lambda i, j, kk: (i, j))
    has_res = res is not None
    has_norm = norm_w is not None
    assert not has_norm or tn == n
    n_mm = 2 + (2 if pair2 is not None else 0)
    n_ops = n_mm + has_res + has_norm

    def body(*refs):
        a_ref, b_ref = refs[:2]
        a2_ref, b2_ref = refs[2:4] if pair2 is not None else (None, None)
        r_ref = refs[n_mm] if has_res else None
        nw_ref = refs[n_mm + has_res] if has_norm else None
        o_ref = refs[n_ops]
        hn_ref = refs[n_ops + 1] if has_norm else None
        rest = refs[n_ops + 1 + has_norm:]

        def finish(tot):
            if has_res:
                tot = tot + r_ref[...]
            o_ref[...] = tot.astype(out_dtype)
            if has_norm:
                r = lax.rsqrt(jnp.mean(tot * tot, axis=-1, keepdims=True) + RMS_EPS)
                hn_ref[...] = _b(tot * r * nw_ref[...])

        if nk == 1:
            finish(dot(_b(a_ref[...]), _b(b_ref[...])))
            return
        acc_ref = rest[0]
        kk = pl.program_id(2)

        @pl.when(kk == 0)
        def _():
            acc_ref[...] = dot(_b(a_ref[...]), _b(b_ref[...]))

        @pl.when(jnp.logical_and(kk > 0, kk < nk1))
        def _():
            acc_ref[...] += dot(_b(a_ref[...]), _b(b_ref[...]))

        if pair2 is not None:
            @pl.when(kk >= nk1)
            def _():
                acc_ref[...] += dot(_b(a2_ref[...]), _b(b2_ref[...]))

        @pl.when(kk == nk - 1)
        def _():
            finish(acc_ref[...])

    in_specs = specs(lambda kk: jnp.minimum(kk, nk1 - 1))
    args = (a, b)
    if pair2 is not None:
        in_specs += specs(lambda kk: jnp.maximum(kk - nk1, 0))
        args += tuple(pair2)
    if has_res:
        in_specs.append(o_spec)
        args += (res,)
    out_specs, out_shape = [o_spec], [jax.ShapeDtypeStruct((m, n), out_dtype)]
    if has_norm:
        in_specs.append(pl.BlockSpec((1, tn), lambda i, j, kk: (0, j)))
        args += (norm_w,)
        out_specs.append(o_spec)
        out_shape.append(jax.ShapeDtypeStruct((m, n), bf16))
    out = _call(body, args, name=name, grid=(m // tm, n // tn, nk), in_specs=in_specs, out_specs=out_specs,
                out_shape=out_shape, scratch=[pltpu.VMEM((tm, tn), f32)] if nk > 1 else [],
                sem=("parallel", "parallel", "arbitrary"), vmem_mb=vmem_mb, comm=comm)
    n_own = len(out_shape)
    own = out[0] if n_own == 1 else tuple(out[:n_own])
    return own if comm is None else (own, out[n_own:])


def _rms_fwd(h, w, name):
    l, d = h.shape
    tr = _tile(l, 512)

    def body(h_ref, w_ref, o_ref):
        x = h_ref[...]
        r = lax.rsqrt(jnp.mean(x * x, axis=-1, keepdims=True) + RMS_EPS)
        o_ref[...] = _b(x * r * w_ref[...])

    return pl.pallas_call(
        body, name=name, grid=(l // tr,),
        in_specs=[pl.BlockSpec((tr, d), lambda i: (i, 0)), pl.BlockSpec((1, d), lambda i: (0, 0))],
        out_specs=pl.BlockSpec((tr, d), lambda i: (i, 0)),
        out_shape=jax.ShapeDtypeStruct((l, d), bf16),
        compiler_params=_params(("parallel",), 32),
    )(h, w)


def _rms_bwd(h, w, dhn, dres, name):
    l, d = h.shape
    tr = _tile(l, 512)

    def body(h_ref, w_ref, g_ref, r_ref, dh_ref, dhb_ref, dw_ref):
        x = h_ref[...]
        r = lax.rsqrt(jnp.mean(x * x, axis=-1, keepdims=True) + RMS_EPS)
        xh = x * r
        g = g_ref[...]
        dxh = g * w_ref[...]
        dx = r * (dxh - xh * jnp.mean(dxh * xh, axis=-1, keepdims=True))
        dh = r_ref[...] + dx
        dh_ref[...] = dh
        dhb_ref[...] = _b(dh)
        dw = jnp.sum(g * xh, axis=0, keepdims=True)

        @pl.when(pl.program_id(0) == 0)
        def _():
            dw_ref[...] = dw

        @pl.when(pl.program_id(0) > 0)
        def _():
            dw_ref[...] += dw

    row = pl.BlockSpec((tr, d), lambda i: (i, 0))
    vec = pl.BlockSpec((1, d), lambda i: (0, 0))
    return pl.pallas_call(
        body, name=name, grid=(l // tr,), in_specs=[row, vec, row, row], out_specs=[row, row, vec],
        out_shape=[jax.ShapeDtypeStruct((l, d), f32), jax.ShapeDtypeStruct((l, d), bf16),
                   jax.ShapeDtypeStruct((1, d), f32)],
        compiler_params=_params(("arbitrary",), 40),
    )(h, w, dhn, dres)


def _dhn_norm_bwd(pairs, h, nw, dres, name, *, tm, tk, init=None, comm=None):
    l, d = h.shape
    nks = [a.shape[1] // tk for a, _ in pairs]
    starts = [sum(nks[:p]) for p in range(len(pairs))]
    nk = sum(nks)
    n_ops = 2 * len(pairs)
    has_init = init is not None

    def body(*refs):
        ops = refs[:n_ops]
        init_ref = refs[n_ops] if has_init else None
        h_ref, w_ref, r_ref, dh_ref, dhb_ref, dw_ref, acc_ref = refs[n_ops + has_init:]
        i, kk = pl.program_id(0), pl.program_id(1)

        @pl.when(kk == 0)
        def _():
            part = _nt(ops[0][...], ops[1][...])
            acc_ref[...] = part + init_ref[...] if has_init else part

        for p in range(len(pairs)):
            lo = max(starts[p], 1)

            @pl.when(jnp.logical_and(kk >= lo, kk < starts[p] + nks[p]))
            def _(a_ref=ops[2 * p], b_ref=ops[2 * p + 1]):
                acc_ref[...] += _nt(a_ref[...], b_ref[...])

        @pl.when(kk == nk - 1)
        def _():
            g = acc_ref[...]
            x = h_ref[...]
            r = lax.rsqrt(jnp.mean(x * x, axis=-1, keepdims=True) + RMS_EPS)
            xh = x * r
            dxh = g * w_ref[...]
            dh = r_ref[...] + r * (dxh - xh * jnp.mean(dxh * xh, axis=-1, keepdims=True))
            dh_ref[...] = dh
            dhb_ref[...] = _b(dh)
            dw = jnp.sum(g * xh, axis=0, keepdims=True)

            @pl.when(i == 0)
            def _():
                dw_ref[...] = dw

            @pl.when(i > 0)
            def _():
                dw_ref[...] += dw

    def k_of(p):
        return lambda kk: jnp.clip(kk - starts[p], 0, nks[p] - 1)

    in_specs, args = [], []
    for p, (a, b) in enumerate(pairs):
        in_specs += [pl.BlockSpec((tm, tk), functools.partial(lambda i, kk, f: (i, f(kk)), f=k_of(p))),
                     pl.BlockSpec((d, tk), functools.partial(lambda i, kk, f: (0, f(kk)), f=k_of(p)))]
        args += [a, b]
    row = pl.BlockSpec((tm, d), lambda i, kk: (i, 0))
    vec = pl.BlockSpec((1, d), lambda i, kk: (0, 0))
    if has_init:
        in_specs.append(row)
        args.append(init)
    return _call(body, tuple(args) + (h, nw, dres), name=name, grid=(l // tm, nk), in_specs=in_specs + [row, vec, row],
                 out_specs=[row, row, vec],
                 out_shape=[jax.ShapeDtypeStruct((l, d), f32), jax.ShapeDtypeStruct((l, d), bf16),
                            jax.ShapeDtypeStruct((1, d), f32)],
                 scratch=[pltpu.VMEM((tm, d), f32)], sem=("arbitrary", "arbitrary"), vmem_mb=48, comm=comm)


def _final_loss(h, w, target, name):
    l, d = h.shape
    nch = l // CHUNK
    cpt = _tile(l, 256) // CHUNK
    nt = nch // cpt

    def body(h_ref, w_ref, *rest):
        t_refs, (dh_ref, dhb_ref, dw_ref, loss_ref) = rest[:cpt], rest[cpt:]
        i = pl.program_id(0)
        wv = w_ref[...]
        dw = jnp.zeros((1, d), f32)
        part = jnp.zeros((1, 1), f32)
        for c in range(cpt):
            rows = slice(c * CHUNK, (c + 1) * CHUNK)
            live = (i * cpt + c > 0).astype(f32)
            x = h_ref[rows, :]
            r = lax.rsqrt(jnp.mean(x * x, axis=-1, keepdims=True) + RMS_EPS)
            xh = x * r
            err = (xh * wv - t_refs[c][...]) * live
            dy = err * (1.0 / d)
            dxh = dy * wv
            dx = r * (dxh - xh * jnp.mean(dxh * xh, axis=-1, keepdims=True))
            dh_ref[rows, :] = dx
            dhb_ref[rows, :] = _b(dx)
            dw = dw + jnp.sum(dy * xh, axis=0, keepdims=True)
            part = part + 0.5 * jnp.sum(jnp.sum(err * err, axis=-1, keepdims=True) * (1.0 / d), axis=0, keepdims=True)
        part = jnp.broadcast_to(part, (1, LANES))

        @pl.when(i == 0)
        def _():
            dw_ref[...] = dw
            loss_ref[...] = part

        @pl.when(i > 0)
        def _():
            dw_ref[...] += dw
            loss_ref[...] += part

    row = pl.BlockSpec((cpt * CHUNK, d), lambda i: (i, 0))
    vec = pl.BlockSpec((1, d), lambda i: (0, 0))
    t_specs = [pl.BlockSpec((CHUNK, d), functools.partial(lambda i, c: (jnp.maximum(i * cpt + c - 1, 0), 0), c=c))
               for c in range(cpt)]
    return pl.pallas_call(
        body, name=name, grid=(nt,),
        in_specs=[row, vec] + t_specs,
        out_specs=[row, row, vec, pl.BlockSpec((1, LANES), lambda i: (0, 0))],
        out_shape=[jax.ShapeDtypeStruct((l, d), f32), jax.ShapeDtypeStruct((l, d), bf16),
                   jax.ShapeDtypeStruct((1, d), f32), jax.ShapeDtypeStruct((1, LANES), f32)],
        compiler_params=_params(("arbitrary",), 32),
    )(h, w, *([target] * cpt))


def _ffn_gu(hn, wg, wu, name, *, tm, tn, comm=None):
    l, d = hn.shape
    fh = wg.shape[1]

    def body(h_ref, g_ref, u_ref, a_ref, b_ref, s_ref):
        hb = h_ref[...]
        a = _nn(hb, g_ref[...])
        bb = _nn(hb, u_ref[...])
        a_ref[...] = _b(a)
        b_ref[...] = _b(bb)
        s_ref[...] = _b(_silu(a) * bb)

    wspec = pl.BlockSpec((d, tn), lambda i, j: (0, j))
    ospec = pl.BlockSpec((tm, tn), lambda i, j: (i, j))
    return _call(body, (hn, wg, wu), name=name, grid=(l // tm, fh // tn),
                 in_specs=[pl.BlockSpec((tm, d), lambda i, j: (i, 0)), wspec, wspec], out_specs=[ospec] * 3,
                 out_shape=[jax.ShapeDtypeStruct((l, fh), bf16)] * 3, sem=("parallel", "parallel"), vmem_mb=48,
                 comm=comm)


def _ffn_ds(dhb, wd, a, b, *, tm, tn, name):
    l, d = dhb.shape
    fh = wd.shape[0]

    def body(g_ref, w_ref, a_ref, b_ref, da_ref, db_ref):
        ds = _nt(g_ref[...], w_ref[...])
        a = a_ref[...].astype(f32)
        da_ref[...] = _b(ds * b_ref[...].astype(f32) * _dsilu(a))
        db_ref[...] = _b(ds * _silu(a))

    ospec = pl.BlockSpec((tm, tn), lambda i, j: (i, j))
    return pl.pallas_call(
        body, name=name, grid=(l // tm, fh // tn),
        in_specs=[pl.BlockSpec((tm, d), lambda i, j: (i, 0)), pl.BlockSpec((tn, d), lambda i, j: (j, 0)), ospec, ospec],
        out_specs=[ospec, ospec], out_shape=[jax.ShapeDtypeStruct((l, fh), bf16)] * 2,
        compiler_params=_params(("parallel", "parallel"), 48),
    )(dhb, wd, a, b)


def _gnorm_fwd(o, proj, nw, heads, dv, gate_blk, name):
    l, hv = o.shape
    tr = _tile(l, 256)

    def body(o_ref, g_ref, w_ref, y_ref):
        wv = w_ref[...]
        for h in range(heads):
            sl = slice(h * dv, (h + 1) * dv)
            oh = o_ref[:, sl]
            r = lax.rsqrt(jnp.mean(oh * oh, axis=-1, keepdims=True) + RMS_EPS)
            y_ref[:, sl] = _b(oh * r * wv * _silu(g_ref[:, sl]))

    return pl.pallas_call(
        body, name=name, grid=(l // tr,),
        in_specs=[pl.BlockSpec((tr, hv), lambda i: (i, 0)), pl.BlockSpec((tr, hv), lambda i: (i, gate_blk)),
                  pl.BlockSpec((1, dv), lambda i: (0, 0))],
        out_specs=pl.BlockSpec((tr, hv), lambda i: (i, 0)),
        out_shape=jax.ShapeDtypeStruct((l, hv), bf16),
        compiler_params=_params(("parallel",), 32),
    )(o, proj, nw)


def _gnorm_bwd(o, proj, nw, dy, heads, dv, gate_blk, name):
    l, hv = o.shape
    tr = _tile(l, 256)

    def body(o_ref, g_ref, w_ref, dy_ref, do_ref, dg_ref, dw_ref):
        wv = w_ref[...]
        dw = jnp.zeros((1, dv), f32)
        for h in range(heads):
            sl = slice(h * dv, (h + 1) * dv)
            oh = o_ref[:, sl]
            g = g_ref[:, sl]
            dyh = dy_ref[:, sl]
            r = lax.rsqrt(jnp.mean(oh * oh, axis=-1, keepdims=True) + RMS_EPS)
            xh = oh * r
            dn = dyh * _silu(g)
            dg_ref[:, sl] = _b(dyh * (xh * wv) * _dsilu(g))
            dxh = dn * wv
            do_ref[:, sl] = r * (dxh - xh * jnp.mean(dxh * xh, axis=-1, keepdims=True))
            dw = dw + jnp.sum(dn * xh, axis=0, keepdims=True)

        @pl.when(pl.program_id(0) == 0)
        def _():
            dw_ref[...] = dw

        @pl.when(pl.program_id(0) > 0)
        def _():
            dw_ref[...] += dw

    row = pl.BlockSpec((tr, hv), lambda i: (i, 0))
    gate = pl.BlockSpec((tr, hv), lambda i: (i, gate_blk))
    vec = pl.BlockSpec((1, dv), lambda i: (0, 0))
    return pl.pallas_call(
        body, name=name, grid=(l // tr,),
        in_specs=[row, gate, vec, row],
        out_specs=[row, gate, vec],
        out_shape=[jax.ShapeDtypeStruct((l, hv), f32), jax.ShapeDtypeStruct(proj.shape, bf16),
                   jax.ShapeDtypeStruct((1, dv), f32)],
        compiler_params=_params(("arbitrary",), 40),
    )(o, proj, nw, dy)


def _ret_prep(proj, cos, sin, name):
    l = proj.shape[0]
    tr = _tile(l, 256)
    half = RET_DK // 2
    scale = RET_DK ** -0.5

    def body(p_ref, c_ref, s_ref, o_ref):
        rows = pl.program_id(0) * tr + lax.broadcasted_iota(jnp.int32, (tr, 1), 0)
        kmul = jnp.where(rows >= PAD, scale, 0.0).astype(f32)
        c, s = c_ref[...], s_ref[...]
        for j in range(2 * RET_HEADS):
            t1 = p_ref[:, j * RET_DK: j * RET_DK + half]
            t2 = p_ref[:, j * RET_DK + half: (j + 1) * RET_DK]
            o1 = t1 * c - t2 * s
            o2 = t1 * s + t2 * c
            if j >= RET_HEADS:
                o1, o2 = o1 * kmul, o2 * kmul
            o_ref[:, j * RET_DK: j * RET_DK + half] = o1
            o_ref[:, j * RET_DK + half: (j + 1) * RET_DK] = o2

    wide = pl.BlockSpec((tr, 2 * RET_QK), lambda i: (i, 0))
    tab = pl.BlockSpec((tr, half), lambda i: (i, 0))
    return pl.pallas_call(
        body, name=name, grid=(l // tr,), in_specs=[wide, tab, tab], out_specs=wide,
        out_shape=jax.ShapeDtypeStruct((l, 2 * RET_QK), f32),
        compiler_params=_params(("parallel",), 32),
    )(proj, cos, sin)


def _ret_prep_bwd(dq, dk, cos, sin, dproj, name):
    l = dq.shape[0]
    tr = _tile(l, 256)
    half = RET_DK // 2
    scale = RET_DK ** -0.5

    def body(dq_ref, dk_ref, c_ref, s_ref, _, o_ref):
        rows = pl.program_id(0) * tr + lax.broadcasted_iota(jnp.int32, (tr, 1), 0)
        kmul = jnp.where(rows >= PAD, scale, 0.0).astype(f32)
        c, s = c_ref[...], s_ref[...]
        for j in range(2 * RET_HEADS):
            d_ref = dq_ref if j < RET_HEADS else dk_ref
            jj = j % RET_HEADS
            d1 = d_ref[:, jj * RET_DK: jj * RET_DK + half]
            d2 = d_ref[:, jj * RET_DK + half: (jj + 1) * RET_DK]
            if j >= RET_HEADS:
                d1, d2 = d1 * kmul, d2 * kmul
            o_ref[:, j * RET_DK: j * RET_DK + half] = _b(d1 * c + d2 * s)
            o_ref[:, j * RET_DK + half: (j + 1) * RET_DK] = _b(d2 * c - d1 * s)

    nar = pl.BlockSpec((tr, RET_QK), lambda i: (i, 0))
    wide = pl.BlockSpec((tr, 2 * RET_QK), lambda i: (i, 0))
    tab = pl.BlockSpec((tr, half), lambda i: (i, 0))
    return pl.pallas_call(
        body, name=name, grid=(l // tr,), in_specs=[nar, nar, tab, tab, pl.BlockSpec(memory_space=pl.ANY)],
        out_specs=wide, out_shape=jax.ShapeDtypeStruct(dproj.shape, dproj.dtype), input_output_aliases={4: 0},
        compiler_params=_params(("parallel",), 32),
    )(dq, dk, cos, sin, dproj)


RET_BLOCK_CHUNKS = 3


def _ret_block(l):
    nch = l // CHUNK
    return RET_BLOCK_CHUNKS * CHUNK if nch % RET_BLOCK_CHUNKS == 0 else CHUNK


def _ret_decay(lg, rb):
    idx = lax.broadcasted_iota(jnp.int32, (rb, 1), 0).astype(f32)
    ri = lax.broadcasted_iota(jnp.int32, (rb, rb), 0)
    ci = lax.broadcasted_iota(jnp.int32, (rb, rb), 1)
    rel = (ri - ci).astype(f32)
    dmask = jnp.where(ri >= ci, jnp.exp(lg * jnp.maximum(rel, 0.0)), 0.0)
    xi = jnp.exp(lg * (idx + 1.0))
    zeta = jnp.exp(lg * (rb - 1.0 - idx))
    return dmask, xi, zeta


def _ret_scan_fwd(qk, proj, lgs, gcs, name, comm=None):
    l = qk.shape[0]
    rb = _ret_block(l)
    nb = l // rb

    def body(lg_ref, gc_ref, q_ref, k_ref, v_ref, o_ref, st_ref, s_ref):
        @pl.when(pl.program_id(0) == 0)
        def _():
            s_ref[...] = jnp.zeros_like(s_ref)

        hs = range(RET_HEADS)
        dec = [_ret_decay(lg_ref[h], rb) for h in hs]
        q = [q_ref[:, h * RET_DK:(h + 1) * RET_DK] for h in hs]
        k = [k_ref[:, h * RET_DK:(h + 1) * RET_DK] for h in hs]
        vb = [_b(v_ref[:, h * RET_DV:(h + 1) * RET_DV]) for h in hs]
        s = [s_ref[h] for h in hs]
        sb = [_b(s[h]) for h in hs]
        scores = [_b(_nt(_b(q[h]), _b(k[h])) * dec[h][0]) for h in hs]
        inter = [_nn(_b(q[h] * dec[h][1]), sb[h]) for h in hs]
        kv = [_tn(_b(k[h] * dec[h][2]), vb[h]) for h in hs]
        for h in hs:
            st_ref[0, h] = sb[h]
            o_ref[:, h * RET_DV:(h + 1) * RET_DV] = _nn(scores[h], vb[h]) + inter[h]
            s_ref[h] = gc_ref[h] * s[h] + kv[h]

    smem = pl.BlockSpec(memory_space=pltpu.SMEM)
    return _call(
        body, (lgs, gcs, qk, qk, proj), name=name, grid=(nb,),
        in_specs=[smem, smem,
                  pl.BlockSpec((rb, RET_QK), lambda n: (n, 0)),
                  pl.BlockSpec((rb, RET_QK), lambda n: (n, 1)),
                  pl.BlockSpec((rb, RET_V), lambda n: (n, 1))],
        out_specs=[pl.BlockSpec((rb, RET_V), lambda n: (n, 0)),
                   pl.BlockSpec((1, RET_HEADS, RET_DK, RET_DV), lambda n: (n, 0, 0, 0))],
        out_shape=[jax.ShapeDtypeStruct((l, RET_V), f32),
                   jax.ShapeDtypeStruct((nb, RET_HEADS, RET_DK, RET_DV), bf16)],
        scratch=[pltpu.VMEM((RET_HEADS, RET_DK, RET_DV), f32)], sem=("arbitrary",), vmem_mb=40, comm=comm)


def _ret_scan_bwd(qk, proj, states, do, dproj, lgs, gcs, name, comm=None):
    l = qk.shape[0]
    rb = _ret_block(l)
    nb = l // rb

    def body(lg_ref, gc_ref, q_ref, k_ref, v_ref, st_ref, do_ref, _, dq_ref, dk_ref, dv_ref, ds_ref):
        @pl.when(pl.program_id(0) == 0)
        def _():
            ds_ref[...] = jnp.zeros_like(ds_ref)

        hs = range(RET_HEADS)
        dec = [_ret_decay(lg_ref[h], rb) for h in hs]
        q = [q_ref[:, h * RET_DK:(h + 1) * RET_DK] for h in hs]
        k = [k_ref[:, h * RET_DK:(h + 1) * RET_DK] for h in hs]
        qb, kb = [_b(t) for t in q], [_b(t) for t in k]
        vb = [_b(v_ref[:, h * RET_DV:(h + 1) * RET_DV]) for h in hs]
        dob = [_b(do_ref[:, h * RET_DV:(h + 1) * RET_DV]) for h in hs]
        dsp = [ds_ref[h] for h in hs]
        dspb = [_b(t) for t in dsp]
        scores = [_b(_nt(qb[h], kb[h]) * dec[h][0]) for h in hs]
        dscores = [_b(_nt(dob[h], vb[h]) * dec[h][0]) for h in hs]
        for h in hs:
            dq_ref[:, h * RET_DK:(h + 1) * RET_DK] = _nn(dscores[h], kb[h]) + _nt(dob[h], st_ref[0, h]) * dec[h][1]
        for h in hs:
            dk_ref[:, h * RET_DK:(h + 1) * RET_DK] = _tn(dscores[h], qb[h]) + _nt(vb[h], dspb[h]) * dec[h][2]
        for h in hs:
            dv_ref[:, h * RET_DV:(h + 1) * RET_DV] = _b(_tn(scores[h], dob[h]) + _nn(_b(k[h] * dec[h][2]), dspb[h]))
        for h in hs:
            ds_ref[h] = gc_ref[h] * dsp[h] + _tn(_b(q[h] * dec[h][1]), dob[h])

    smem = pl.BlockSpec(memory_space=pltpu.SMEM)
    rev = lambda n: nb - 1 - n
    return _call(
        body, (lgs, gcs, qk, qk, proj, states, do, dproj), name=name, grid=(nb,),
        in_specs=[smem, smem,
                  pl.BlockSpec((rb, RET_QK), lambda n: (rev(n), 0)),
                  pl.BlockSpec((rb, RET_QK), lambda n: (rev(n), 1)),
                  pl.BlockSpec((rb, RET_V), lambda n: (rev(n), 1)),
                  pl.BlockSpec((1, RET_HEADS, RET_DK, RET_DV), lambda n: (rev(n), 0, 0, 0)),
                  pl.BlockSpec((rb, RET_V), lambda n: (rev(n), 0)),
                  pl.BlockSpec(memory_space=pl.ANY)],
        out_specs=[pl.BlockSpec((rb, RET_QK), lambda n: (rev(n), 0)),
                   pl.BlockSpec((rb, RET_QK), lambda n: (rev(n), 0)),
                   pl.BlockSpec((rb, RET_V), lambda n: (rev(n), 1))],
        out_shape=[jax.ShapeDtypeStruct((l, RET_QK), f32), jax.ShapeDtypeStruct((l, RET_QK), f32),
                   jax.ShapeDtypeStruct(dproj.shape, dproj.dtype)],
        scratch=[pltpu.VMEM((RET_HEADS, RET_DK, RET_DV), f32)], sem=("arbitrary",), vmem_mb=40, comm=comm,
        aliases={7: 2})


CONV_BLK = 1024
HALO = 8


def _slab_rows(r):
    return pl.ds(pl.multiple_of(r * HALO, HALO), HALO)


def _conv_slab(x_ref, p_ref, r, i, tr):
    cur = x_ref[_slab_rows(r), :]
    prev = jnp.where(r > 0, x_ref[_slab_rows(jnp.maximum(r - 1, 0)), :], p_ref[...])
    row0 = i * tr + r * HALO
    cur = jnp.where(row0 >= PAD, cur, 0.0)
    prev = jnp.where(row0 - HALO >= PAD, prev, 0.0)
    lrow = lax.broadcasted_iota(jnp.int32, (HALO, 1), 0)
    shifted = [jnp.where(lrow < s, pltpu.roll(prev, s, 0), pltpu.roll(cur, s, 0)) for s in range(1, CONV_K)]
    return [cur] + shifted


def _conv_of(xs, w):
    acc = xs[0] * w[CONV_K - 1:CONV_K, :]
    for s in range(1, CONV_K):
        acc = acc + xs[s] * w[CONV_K - 1 - s:CONV_K - s, :]
    return acc


def _slab_loop(n_slabs, fn, init=None):
    return lax.fori_loop(0, n_slabs, fn, init, unroll=8)


def _dn_conv_fwd(proj, conv_w, name, comm=None):
    l = proj.shape[0]
    tr = _tile(l, 256)
    nblk = DN_CONV_CH // CONV_BLK
    heads = CONV_BLK // DN_DK

    def body(x_ref, p_ref, w_ref, o_ref):
        i, j = pl.program_id(0), pl.program_id(1)
        w = w_ref[...]

        def act(r):
            return _silu(_conv_of(_conv_slab(x_ref, p_ref, r, i, tr), w))

        def normed(scale):
            def slab(r, carry):
                a = act(r)
                outs = []
                for h in range(heads):
                    ah = a[:, h * DN_DK:(h + 1) * DN_DK]
                    outs.append(ah * (lax.rsqrt(jnp.sum(ah * ah, axis=-1, keepdims=True) + RMS_EPS) * scale))
                o_ref[_slab_rows(r), :] = jnp.concatenate(outs, axis=1)
                return carry
            return slab

        def plain(r, carry):
            o_ref[_slab_rows(r), :] = act(r)
            return carry

        @pl.when(j == 0)
        def _():
            _slab_loop(tr // HALO, normed(DN_DK ** -0.5))

        @pl.when(j == 1)
        def _():
            _slab_loop(tr // HALO, normed(1.0))

        @pl.when(j >= 2)
        def _():
            _slab_loop(tr // HALO, plain)

    hb = tr // HALO
    return _call(
        body, (proj, proj, conv_w), name=name, grid=(l // tr, nblk),
        in_specs=[pl.BlockSpec((tr, CONV_BLK), lambda i, j: (i, j)),
                  pl.BlockSpec((HALO, CONV_BLK), lambda i, j: (jnp.maximum(i * hb - 1, 0), j)),
                  pl.BlockSpec((CONV_K, CONV_BLK), lambda i, j: (0, j))],
        out_specs=[pl.BlockSpec((tr, CONV_BLK), lambda i, j: (i, j))],
        out_shape=[jax.ShapeDtypeStruct((l, DN_CONV_CH), f32)],
        scratch=[], sem=("parallel", "parallel"), vmem_mb=32, comm=comm)


def _dn_conv_bwd_a(proj, conv_w, dqkv, name, comm=None):
    l = proj.shape[0]
    tr = _tile(l, 256)
    nblk = DN_CONV_CH // CONV_BLK
    heads = CONV_BLK // DN_DK

    def body(x_ref, p_ref, w_ref, d_ref, dc_ref, dw_ref, acc_ref):
        j, i = pl.program_id(0), pl.program_id(1)
        w = w_ref[...]
        acc_ref[...] = jnp.zeros_like(acc_ref)

        def slab_of(l2_scale):
            def slab(r, carry):
                xs = _conv_slab(x_ref, p_ref, r, i, tr)
                c = _conv_of(xs, w)
                a = _silu(c)
                dy = d_ref[_slab_rows(r), :]
                if l2_scale is None:
                    da = dy
                else:
                    parts = []
                    for h in range(heads):
                        sl = slice(h * DN_DK, (h + 1) * DN_DK)
                        ah, dyh = a[:, sl], dy[:, sl]
                        rn = lax.rsqrt(jnp.sum(ah * ah, axis=-1, keepdims=True) + RMS_EPS)
                        yh = ah * rn
                        parts.append((rn * l2_scale) * (dyh - yh * jnp.sum(dyh * yh, axis=-1, keepdims=True)))
                    da = jnp.concatenate(parts, axis=1)
                dc = da * _dsilu(c)
                dc_ref[_slab_rows(r), :] = dc
                for s in range(CONV_K):
                    acc_ref[CONV_K - 1 - s] += dc * xs[s]
                return carry
            return slab

        @pl.when(j == 0)
        def _():
            _slab_loop(tr // HALO, slab_of(DN_DK ** -0.5))

        @pl.when(j == 1)
        def _():
            _slab_loop(tr // HALO, slab_of(1.0))

        @pl.when(j >= 2)
        def _():
            _slab_loop(tr // HALO, slab_of(None))

        ksel = lax.broadcasted_iota(jnp.int32, (CONV_K, 1), 0)
        dw = jnp.zeros((CONV_K, CONV_BLK), f32)
        for k in range(CONV_K):
            dw = dw + jnp.where(ksel == k, jnp.sum(acc_ref[k], axis=0, keepdims=True), 0.0)

        @pl.when(i == 0)
        def _():
            dw_ref[...] = dw

        @pl.when(i > 0)
        def _():
            dw_ref[...] += dw

    hb = tr // HALO
    blk = pl.BlockSpec((tr, CONV_BLK), lambda j, i: (i, j))
    return _call(
        body, (proj, proj, conv_w, dqkv), name=name, grid=(nblk, l // tr),
        in_specs=[blk, pl.BlockSpec((HALO, CONV_BLK), lambda j, i: (jnp.maximum(i * hb - 1, 0), j)),
                  pl.BlockSpec((CONV_K, CONV_BLK), lambda j, i: (0, j)), blk],
        out_specs=[blk, pl.BlockSpec((CONV_K, CONV_BLK), lambda j, i: (0, j))],
        out_shape=[jax.ShapeDtypeStruct((l, DN_CONV_CH), f32), jax.ShapeDtypeStruct((CONV_K, DN_CONV_CH), f32)],
        scratch=[pltpu.VMEM((CONV_K, HALO, CONV_BLK), f32)], sem=("parallel", "arbitrary"), vmem_mb=40, comm=comm)


def _dn_conv_bwd_b(dc, conv_w, dproj, name):
    l = dc.shape[0]
    tr = _tile(l, 256)
    nblk = DN_CONV_CH // CONV_BLK
    nrow = l // tr

    n_slabs = tr // HALO
    pair = 2 * HALO

    def body(d_ref, n_ref, w_ref, _, o_ref):
        i = pl.program_id(0)
        w = w_ref[...]
        nxt_tile = jnp.where(i < nrow - 1, n_ref[...], 0.0)
        lrow = lax.broadcasted_iota(jnp.int32, (HALO, 1), 0)

        def one(r):
            cur = d_ref[_slab_rows(r), :]
            nxt = jnp.where(r < n_slabs - 1, d_ref[_slab_rows(jnp.minimum(r + 1, n_slabs - 1)), :], nxt_tile)
            acc = cur * w[CONV_K - 1:CONV_K, :]
            for s in range(1, CONV_K):
                up = jnp.where(lrow >= HALO - s, pltpu.roll(nxt, HALO - s, 0), pltpu.roll(cur, HALO - s, 0))
                acc = acc + up * w[CONV_K - 1 - s:CONV_K - s, :]
            return jnp.where(i * tr + r * HALO >= PAD, acc, 0.0)

        def two(q, carry):
            rows = pl.ds(pl.multiple_of(q * pair, pair), pair)
            o_ref[rows, :] = _b(jnp.concatenate([one(2 * q), one(2 * q + 1)], axis=0))
            return carry

        lax.fori_loop(0, n_slabs // 2, two, None, unroll=4)

    hb = tr // HALO
    nh = l // HALO
    return pl.pallas_call(
        body, name=name, grid=(nrow, nblk),
        in_specs=[pl.BlockSpec((tr, CONV_BLK), lambda i, j: (i, j)),
                  pl.BlockSpec((HALO, CONV_BLK), lambda i, j: (jnp.minimum((i + 1) * hb, nh - 1), j)),
                  pl.BlockSpec((CONV_K, CONV_BLK), lambda i, j: (0, j)),
                  pl.BlockSpec(memory_space=pl.ANY)],
        out_specs=pl.BlockSpec((tr, CONV_BLK), lambda i, j: (i, j)),
        out_shape=jax.ShapeDtypeStruct(dproj.shape, dproj.dtype), input_output_aliases={3: 0},
        compiler_params=_params(("parallel", "parallel"), 32),
    )(dc, dc, conv_w, dproj)


BA_W = LANES


def _dn_gates(ba_ref, al_ref, dt_ref, n):
    rows = n * CHUNK + lax.broadcasted_iota(jnp.int32, (CHUNK, 1), 0)
    vm = (rows >= PAD).astype(f32)
    bin_ = ba_ref[:, 0:DN_HEADS]
    z = ba_ref[:, DN_HEADS:2 * DN_HEADS] + dt_ref[...]
    sp = jnp.maximum(z, 0.0) + jnp.log1p(jnp.exp(-jnp.abs(z)))
    ea = jnp.exp(al_ref[...])
    beta = _sigmoid(bin_) * vm
    g = -ea * sp * vm
    return vm, bin_, z, ea, beta, g


def _tri():
    ri = lax.broadcasted_iota(jnp.int32, (CHUNK, CHUNK), 0)
    ci = lax.broadcasted_iota(jnp.int32, (CHUNK, CHUNK), 1)
    return ri, ci


def _split(a):
    hi = _b(a)
    return hi, _b(a - hi.astype(f32))


def _mm3(a, b, dot=_nn):
    (ah, al), (bh, bl) = _split(a), _split(b)
    return dot(ah, bh) + (dot(ah, bl) + dot(al, bh))


def _cumsum_rows(tri, g):
    tb = _b(tri)
    g1 = _b(g)
    r1 = g - g1.astype(f32)
    g2 = _b(r1)
    g3 = _b(r1 - g2.astype(f32))
    return _nn(tb, g1) + (_nn(tb, g2) + _nn(tb, g3))


DN_SCAN_CHUNKS = 3


def _scan_chunks(nch):
    return DN_SCAN_CHUNKS if nch % DN_SCAN_CHUNKS == 0 else 1


def _dn_prep(qkv, ba, a_log, dt_bias, name, comm=None):
    l = qkv.shape[0]
    nch = l // CHUNK
    heads = range(DN_HEADS)

    def body(q_ref, k_ref, v_ref, ba_ref, al_ref, dt_ref, t_ref, u_ref, wq_ref, pk_ref, eg_ref, kpt_ref, qwt_ref):
        n = pl.program_id(0)
        _, _, _, _, beta, g = _dn_gates(ba_ref, al_ref, dt_ref, n)
        ri, ci = _tri()
        incl, strict = ri >= ci, ri > ci
        eye = (ri == ci).astype(f32)
        gam = _cumsum_rows(incl.astype(f32), g)
        gam_t = gam.T
        gc = [gam[:, h:h + 1] for h in heads]
        bh = [beta[:, h:h + 1] for h in heads]
        kh = [k_ref[:, h * DN_DK:(h + 1) * DN_DK] for h in heads]
        kb = [_b(k) for k in kh]
        decay = [jnp.exp(jnp.where(incl, gc[h] - gam_t[h:h + 1, :], -jnp.inf)) for h in heads]
        a = [jnp.where(strict, bh[h] * _nt(kb[h], kb[h]) * decay[h], 0.0) for h in heads]
        t = [eye - a[h] for h in heads]
        p = a
        for _ in range(int(math.log2(CHUNK)) - 1):
            p = [_mm3(p[h], p[h]) for h in heads]
            t = [t[h] + _mm3(t[h], p[h]) for h in heads]
        eg = [jnp.exp(gc[h]) for h in heads]
        for h in heads:
            t_ref[0, h] = t[h]
            u_ref[:, h * DN_DV:(h + 1) * DN_DV] = _mm3(t[h], v_ref[:, h * DN_DV:(h + 1) * DN_DV] * bh[h])
            w = _mm3(t[h], kh[h] * (bh[h] * eg[h]))
            wq_ref[0, h, 0:CHUNK, :] = _b(w)
            qwt_ref[0, h, DN_DK:2 * DN_DK, :] = _b(w.T)
        for h in heads:
            qh = q_ref[:, h * DN_DK:(h + 1) * DN_DK]
            gl = gc[h][CHUNK - 1:CHUNK, :]
            qe = qh * eg[h]
            ke = kh[h] * jnp.exp(gl - gc[h])
            pmat = _nt(_b(qh), kb[h]) * decay[h]
            wq_ref[0, h, CHUNK:2 * CHUNK, :] = _b(qe)
            qwt_ref[0, h, 0:DN_DK, :] = _b(qe.T)
            pk_ref[0, h, 0:CHUNK, :] = _b(pmat)
            pk_ref[0, h, CHUNK:CHUNK + DN_DK, :] = _b(ke.T)
            kpt_ref[0, h, :, 0:DN_DK] = _b(ke)
            kpt_ref[0, h, :, DN_DK:DN_DK + CHUNK] = _b(pmat.T)
            eg_ref[0, h] = jnp.broadcast_to(jnp.exp(gl), (8, LANES))

    vec = pl.BlockSpec((1, DN_HEADS), lambda n: (0, 0))
    return _call(
        body, (qkv, qkv, qkv, ba, a_log, dt_bias), name=name, grid=(nch,),
        in_specs=[pl.BlockSpec((CHUNK, DN_QK), lambda n: (n, 0)), pl.BlockSpec((CHUNK, DN_QK), lambda n: (n, 1)),
                  pl.BlockSpec((CHUNK, DN_V), lambda n: (n, 1)), pl.BlockSpec((CHUNK, BA_W), lambda n: (n, 0)),
                  vec, vec],
        out_specs=[pl.BlockSpec((1, DN_HEADS, CHUNK, CHUNK), lambda n: (n, 0, 0, 0)),
                   pl.BlockSpec((CHUNK, DN_V), lambda n: (n, 0)),
                   pl.BlockSpec((1, DN_HEADS, 2 * CHUNK, DN_DK), lambda n: (n, 0, 0, 0)),
                   pl.BlockSpec((1, DN_HEADS, CHUNK + DN_DK, CHUNK), lambda n: (n, 0, 0, 0)),
                   pl.BlockSpec((1, DN_HEADS, 8, LANES), lambda n: (n, 0, 0, 0)),
                   pl.BlockSpec((1, DN_HEADS, CHUNK, DN_DK + CHUNK), lambda n: (n, 0, 0, 0)),
                   pl.BlockSpec((1, DN_HEADS, 2 * DN_DK, CHUNK), lambda n: (n, 0, 0, 0))],
        out_shape=[jax.ShapeDtypeStruct((nch, DN_HEADS, CHUNK, CHUNK), f32),
                   jax.ShapeDtypeStruct((l, DN_V), f32),
                   jax.ShapeDtypeStruct((nch, DN_HEADS, 2 * CHUNK, DN_DK), bf16),
                   jax.ShapeDtypeStruct((nch, DN_HEADS, CHUNK + DN_DK, CHUNK), bf16),
                   jax.ShapeDtypeStruct((nch, DN_HEADS, 8, LANES), f32),
                   jax.ShapeDtypeStruct((nch, DN_HEADS, CHUNK, DN_DK + CHUNK), bf16),
                   jax.ShapeDtypeStruct((nch, DN_HEADS, 2 * DN_DK, CHUNK), bf16)],
        sem=("parallel",), vmem_mb=40, comm=comm)


def _dn_scan_fwd(u, wq, pk, egl, name):
    l = u.shape[0]
    nch = l // CHUNK
    cs = _scan_chunks(nch)

    def body(u_ref, wq_ref, pk_ref, eg_ref, o_ref, st_ref, vn_ref, s_ref):
        @pl.when(pl.program_id(0) == 0)
        def _():
            s_ref[...] = jnp.zeros_like(s_ref)

        hs = range(DN_HEADS)
        cols = [slice(h * DN_DV, (h + 1) * DN_DV) for h in hs]
        s = [s_ref[h] for h in hs]
        for c in range(cs):
            rows = slice(c * CHUNK, (c + 1) * CHUNK)
            sb = [_b(s[h]) for h in hs]
            x = [_nn(wq_ref[c, h], sb[h]) for h in hs]
            vnb = [_b(u_ref[rows, cols[h]] - x[h][0:CHUNK]) for h in hs]
            y = [_nn(pk_ref[c, h], vnb[h]) for h in hs]
            for h in hs:
                st_ref[c, h] = sb[h]
                vn_ref[rows, cols[h]] = vnb[h]
                o_ref[rows, cols[h]] = x[h][CHUNK:2 * CHUNK] + y[h][0:CHUNK]
            s = [eg_ref[c, h][0:1, 0:1] * s[h] + y[h][CHUNK:CHUNK + DN_DK] for h in hs]
        for h in hs:
            s_ref[h] = s[h]

    return pl.pallas_call(
        body, name=name, grid=(nch // cs,),
        in_specs=[pl.BlockSpec((cs * CHUNK, DN_V), lambda n: (n, 0)),
                  pl.BlockSpec((cs, DN_HEADS, 2 * CHUNK, DN_DK), lambda n: (n, 0, 0, 0)),
                  pl.BlockSpec((cs, DN_HEADS, CHUNK + DN_DK, CHUNK), lambda n: (n, 0, 0, 0)),
                  pl.BlockSpec((cs, DN_HEADS, 8, LANES), lambda n: (n, 0, 0, 0))],
        out_specs=[pl.BlockSpec((cs * CHUNK, DN_V), lambda n: (n, 0)),
                   pl.BlockSpec((cs, DN_HEADS, DN_DK, DN_DV), lambda n: (n, 0, 0, 0)),
                   pl.BlockSpec((cs * CHUNK, DN_V), lambda n: (n, 0))],
        out_shape=[jax.ShapeDtypeStruct((l, DN_V), f32),
                   jax.ShapeDtypeStruct((nch, DN_HEADS, DN_DK, DN_DV), bf16),
                   jax.ShapeDtypeStruct((l, DN_V), bf16)],
        scratch_shapes=[pltpu.VMEM((DN_HEADS, DN_DK, DN_DV), f32)],
        compiler_params=_params(("arbitrary",), 40),
    )(u, wq, pk, egl)


def _dn_scan_bwd(do, kpt, qwt, egl, name):
    l = do.shape[0]
    nch = l // CHUNK
    cs = _scan_chunks(nch)
    nblk = nch // cs

    def body(do_ref, kpt_ref, qwt_ref, eg_ref, dvn_ref, dsp_ref, ds_ref):
        @pl.when(pl.program_id(0) == 0)
        def _():
            ds_ref[...] = jnp.zeros_like(ds_ref)

        hs = range(DN_HEADS)
        cols = [slice(h * DN_DV, (h + 1) * DN_DV) for h in hs]
        ds = [ds_ref[h] for h in hs]
        for c in reversed(range(cs)):
            rows = slice(c * CHUNK, (c + 1) * CHUNK)
            dspb = [_b(ds[h]) for h in hs]
            dob = [_b(do_ref[rows, cols[h]]) for h in hs]
            dvn = [_nn(kpt_ref[c, h][:, 0:DN_DK], dspb[h]) + _nn(kpt_ref[c, h][:, DN_DK:DN_DK + CHUNK], dob[h])
                   for h in hs]
            for h in hs:
                dsp_ref[c, h] = dspb[h]
                dvn_ref[rows, cols[h]] = dvn[h]
            ds = [eg_ref[c, h][0:1, 0:1] * ds[h] + _nn(qwt_ref[c, h][0:DN_DK], dob[h])
                  - _nn(qwt_ref[c, h][DN_DK:2 * DN_DK], _b(dvn[h])) for h in hs]
        for h in hs:
            ds_ref[h] = ds[h]

    rev = lambda s: nblk - 1 - s
    return pl.pallas_call(
        body, name=name, grid=(nblk,),
        in_specs=[pl.BlockSpec((cs * CHUNK, DN_V), lambda s: (rev(s), 0)),
                  pl.BlockSpec((cs, DN_HEADS, CHUNK, DN_DK + CHUNK), lambda s: (rev(s), 0, 0, 0)),
                  pl.BlockSpec((cs, DN_HEADS, 2 * DN_DK, CHUNK), lambda s: (rev(s), 0, 0, 0)),
                  pl.BlockSpec((cs, DN_HEADS, 8, LANES), lambda s: (rev(s), 0, 0, 0))],
        out_specs=[pl.BlockSpec((cs * CHUNK, DN_V), lambda s: (rev(s), 0)),
                   pl.BlockSpec((cs, DN_HEADS, DN_DK, DN_DV), lambda s: (rev(s), 0, 0, 0))],
        out_shape=[jax.ShapeDtypeStruct((l, DN_V), f32),
                   jax.ShapeDtypeStruct((nch, DN_HEADS, DN_DK, DN_DV), bf16)],
        scratch_shapes=[pltpu.VMEM((DN_HEADS, DN_DK, DN_DV), f32)],
        compiler_params=_params(("arbitrary",), 40),
    )(do, kpt, qwt, egl)


def _dn_post_bwd(qkv, ba, a_log, dt_bias, states, dsp_all, tinv_all, u_all, wq, vn_all, do, dvn_all, name):
    l = qkv.shape[0]
    nch = l // CHUNK

    def body(q_ref, k_ref, v_ref, ba_ref, al_ref, dt_ref, st_ref, dsp_ref, t_ref, u_ref, wq_ref, vn_ref, do_ref,
             dvn_ref, dqkv_ref, dba_ref, dal_ref, ddt_ref):
        step = pl.program_id(0)
        n = step

        vm, bin_, z, ea, beta, g = _dn_gates(ba_ref, al_ref, dt_ref, n)
        ri, ci = _tri()
        incl, strict = ri >= ci, ri > ci
        gam = _cumsum_rows(incl.astype(f32), g)
        gam_t = gam.T
        lane8 = lax.broadcasted_iota(jnp.int32, (1, DN_HEADS), 1)
        sub8 = lax.broadcasted_iota(jnp.int32, (DN_HEADS, 1), 0)
        dbeta = jnp.zeros((CHUNK, DN_HEADS), f32)
        dgam = jnp.zeros((CHUNK, DN_HEADS), f32)
        dgam_neg_t = jnp.zeros((DN_HEADS, CHUNK), f32)
        last = (lax.broadcasted_iota(jnp.int32, (CHUNK, 1), 0) == CHUNK - 1).astype(f32)
        hs = range(DN_HEADS)
        each = lambda fn: [fn(h) for h in hs]
        rsum = lambda t: jnp.sum(t, axis=-1, keepdims=True)
        gc = each(lambda h: gam[:, h:h + 1])
        bh = each(lambda h: beta[:, h:h + 1])
        qh = each(lambda h: q_ref[:, h * DN_DK:(h + 1) * DN_DK])
        kh = each(lambda h: k_ref[:, h * DN_DK:(h + 1) * DN_DK])
        doh = each(lambda h: _b(do_ref[:, h * DN_DV:(h + 1) * DN_DV]))
        sb = each(lambda h: st_ref[0, h])
        dspb = each(lambda h: dsp_ref[0, h])
        vnb = each(lambda h: vn_ref[:, h * DN_DV:(h + 1) * DN_DV])
        dvn = each(lambda h: dvn_ref[:, h * DN_DV:(h + 1) * DN_DV])
        wb = each(lambda h: wq_ref[0, h, 0:CHUNK, :])
        decay = each(lambda h: jnp.exp(jnp.where(incl, gc[h] - gam_t[h:h + 1, :], -jnp.inf)))
        qb, kb = each(lambda h: _b(qh[h])), each(lambda h: _b(kh[h]))
        eg = each(lambda h: jnp.exp(gc[h]))
        gl = each(lambda h: gc[h][CHUNK - 1:CHUNK, :])
        ekd = each(lambda h: jnp.exp(gl[h] - gc[h]))
        dvnb = each(lambda h: _b(dvn[h]))
        kk = each(lambda h: _nt(kb[h], kb[h]))
        p = each(lambda h: _nt(qb[h], kb[h]) * decay[h])
        dpraw = each(lambda h: _nt(doh[h], vnb[h]))
        dqe = each(lambda h: _nt(doh[h], sb[h]))
        dke = each(lambda h: _nt(vnb[h], dspb[h]))
        dw = each(lambda h: -_nt(dvnb[h], sb[h]))
        dru = each(lambda h: _mm3(t_ref[0, h], dvn[h], _tn))
        drw = each(lambda h: _mm3(t_ref[0, h], dw[h], _tn))
        dqk = each(lambda h: _b(dpraw[h] * decay[h]))
        for h in hs:
            dqkv_ref[:, h * DN_DK:(h + 1) * DN_DK] = _nn(dqk[h], kb[h]) + dqe[h] * eg[h]
            dqkv_ref[:, 2 * DN_QK + h * DN_DV:2 * DN_QK + (h + 1) * DN_DV] = bh[h] * dru[h]
        da = each(lambda h: jnp.where(strict, -(_nt(_b(dru[h]), _b(u_ref[:, h * DN_DV:(h + 1) * DN_DV]))
                                                + _nt(_b(drw[h]), wb[h])), 0.0))
        dkk = each(lambda h: _b(da[h] * bh[h] * decay[h]))
        for h in hs:
            dqkv_ref[:, DN_QK + h * DN_DK:DN_QK + (h + 1) * DN_DK] = (
                _tn(dqk[h], qb[h]) + dke[h] * ekd[h] + (bh[h] * eg[h]) * drw[h]
                + _nn(dkk[h], kb[h]) + _tn(dkk[h], kb[h]))
        for h in hs:
            keg = kh[h] * eg[h]
            ke = kh[h] * ekd[h]
            rw = rsum(drw[h] * keg)
            rke = rsum(dke[h] * ke)
            db_h = rsum(dru[h] * v_ref[:, h * DN_DV:(h + 1) * DN_DV]) + rw + rsum(da[h] * kk[h] * decay[h])
            mm = da[h] * (bh[h] * kk[h] * decay[h]) + dpraw[h] * p[h]
            dgl = (jnp.sum(rke, axis=0, keepdims=True)
                   + jnp.exp(gl[h]) * jnp.sum(rsum(dspb[h].astype(f32) * sb[h].astype(f32)), axis=0, keepdims=True))
            dg_h = rsum(mm) + rw * bh[h] + rsum(dqe[h] * (qh[h] * eg[h])) - rke + last * dgl
            dbeta = dbeta + jnp.where(lane8 == h, db_h, 0.0)
            dgam = dgam + jnp.where(lane8 == h, dg_h, 0.0)
            dgam_neg_t = dgam_neg_t + jnp.where(sub8 == h, jnp.sum(mm, axis=0, keepdims=True), 0.0)
        dgam = dgam - dgam_neg_t.T
        dg = _cumsum_rows((ri <= ci).astype(f32), dgam)
        sg = _sigmoid(bin_)
        dbin = dbeta * vm * sg * (1.0 - sg)
        dain = dg * (-ea) * vm * _sigmoid(z)
        dba_ref[...] = jnp.zeros_like(dba_ref)
        dba_ref[:, 0:DN_HEADS] = dbin
        dba_ref[:, DN_HEADS:2 * DN_HEADS] = dain
        dal = jnp.sum(dg * g, axis=0, keepdims=True)
        ddt = jnp.sum(dain, axis=0, keepdims=True)

        @pl.when(step == 0)
        def _():
            dal_ref[...] = dal
            ddt_ref[...] = ddt

        @pl.when(step > 0)
        def _():
            dal_ref[...] += dal
            ddt_ref[...] += ddt

    vec = pl.BlockSpec((1, DN_HEADS), lambda s: (0, 0))
    qs = pl.BlockSpec((CHUNK, DN_QK), lambda s: (s, 0))
    ks = pl.BlockSpec((CHUNK, DN_QK), lambda s: (s, 1))
    vs = pl.BlockSpec((CHUNK, DN_V), lambda s: (s, 1))
    v0 = pl.BlockSpec((CHUNK, DN_V), lambda s: (s, 0))
    st = pl.BlockSpec((1, DN_HEADS, DN_DK, DN_DV), lambda s: (s, 0, 0, 0))
    return pl.pallas_call(
        body, name=name, grid=(nch,),
        in_specs=[qs, ks, vs, pl.BlockSpec((CHUNK, BA_W), lambda s: (s, 0)), vec, vec, st, st,
                  pl.BlockSpec((1, DN_HEADS, CHUNK, CHUNK), lambda s: (s, 0, 0, 0)),
                  v0, pl.BlockSpec((1, DN_HEADS, 2 * CHUNK, DN_DK), lambda s: (s, 0, 0, 0)), v0, v0, v0],
        out_specs=[pl.BlockSpec((CHUNK, DN_CONV_CH), lambda s: (s, 0)),
                   pl.BlockSpec((CHUNK, BA_W), lambda s: (s, 0)), vec, vec],
        out_shape=[jax.ShapeDtypeStruct((l, DN_CONV_CH), f32), jax.ShapeDtypeStruct((l, BA_W), f32),
                   jax.ShapeDtypeStruct((1, DN_HEADS), f32), jax.ShapeDtypeStruct((1, DN_HEADS), f32)],
        compiler_params=_params(("arbitrary",), 48),
    )(qkv, qkv, qkv, ba, a_log, dt_bias, states, dsp_all, tinv_all, u_all, wq, vn_all, do, dvn_all)


def _ffn_fwd(h, hn, wg, wu, wd, tb, th, tag, plan, next_norm_w=None):
    fh, d = wd.shape
    a, b, s = plan.call(f"{tag}_gu", functools.partial(_ffn_gu, tm=th // 2, tn=fh // 2), hn, wg, wu, n_out=3)
    out = plan.matmul(f"{tag}_down", s, wd, mode="nn", tm=th // 2, tn=d, tk=fh, res=h, norm_w=next_norm_w)
    return out, (hn, a, b, s)


def _ffn_bwd(dh, dhb, h, nw, wg, wu, wd, saved, tb, th, tag, plan):
    hn, a, b, s = saved
    d = h.shape[1]
    fh = wd.shape[0]
    layer = tag[-1]
    gr = plan.grads
    da, db = _ffn_ds(dhb, wd, a, b, tm=th // 2, tn=fh // 2, name=f"{tag}_b_ds")
    gr["down" + layer] = _matmul(s, dhb, mode="tn", tm=fh // 2, tn=d, tk=th, out_dtype=bf16, name=f"{tag}_b_dwd")
    dh2, dh2b, dnw = plan.call(f"{tag}_b_dhn", functools.partial(_dhn_norm_bwd, tm=th // 2, tk=fh // 2),
                               [(da, wg), (db, wu)], h, nw, dh, n_out=3)
    gr["gate" + layer] = _matmul(hn, da, mode="tn", tm=d, tn=fh // 2, tk=th, out_dtype=bf16, name=f"{tag}_b_dwg")
    gr["up" + layer] = _matmul(hn, db, mode="tn", tm=d, tn=fh // 2, tk=th, out_dtype=bf16, name=f"{tag}_b_dwu")
    return dh2, dh2b, dnw


class _Plan:
    GATHERS = {"ret_proj": ("ret_out", "gate0"), "ret_scan": ("up0", "down0"), "ffn0_gu": ("dn_in_top",),
               "ffn0_down": ("dn_in_bottom",), "dn_proj": ("dn_out",), "dn_conv": ("gate1",),
               "dn_prep": ("up1", "down1")}
    SCATTERS = {"ffn1_b_dhn": ("down1",), "dn_b_conv_a": ("gate1", "up1", "dn_out"), "ffn0_b_dhn": ("dn_in",),
                "ret_b_scan": ("gate0", "up0"), "ret_b_dwin": ("down0", "ret_out"), "ret_b_dhn": ("ret_in",)}

    def __init__(self, shards, wts):
        self.shards, self.wts, self.grads, self.parts = shards, wts, {}, {}

    def _exchange(self, stage):
        if self.shards is None:
            return None
        if stage in self.GATHERS:
            return _Exchange([self.shards[n] for n in self.GATHERS[stage]], True)
        if stage in self.SCATTERS:
            return _Exchange([self._dev_major(n) for n in self.SCATTERS[stage]], False)
        return None

    def _dev_major(self, name):
        g = self.grads
        if name[:-1] in ("gate", "up"):
            return _dev_major_cols(g[name], g[name].shape[1] // N_DEV)
        if name[:-1] == "down":
            dwd = g[name]
            return dwd.reshape(N_DEV, dwd.shape[0] // N_DEV, dwd.shape[1])
        if name in ("ret_out", "dn_out"):
            return g[name].reshape(N_DEV, g[name].shape[0] // N_DEV, g[name].shape[1])
        return _dev_major_cols(g[name], self.shards[name].shape[-1])

    def _landed(self, stage, outs):
        if stage in self.SCATTERS:
            self.parts.update(zip(self.SCATTERS[stage], outs))
            return
        w = self.wts
        cols = lambda t: t.transpose(1, 0, 2).reshape(t.shape[1], N_DEV * t.shape[2])
        rows = lambda t: t.reshape(N_DEV * t.shape[1], t.shape[2])
        for name, t in zip(self.GATHERS[stage], outs):
            if name in ("ret_out", "dn_out") or name.startswith("down"):
                w[name] = rows(t)
            else:
                w[name] = cols(t)
        if "dn_in_top" in w and "dn_in_bottom" in w and "dn_main" not in w:
            full = jnp.concatenate([w["dn_in_top"], w["dn_in_bottom"]], axis=0)
            n_main = DN_CONV_CH + DN_V
            w["dn_main"] = full[:, :n_main]
            w["dn_ba"] = jnp.pad(full[:, n_main:], ((0, 0), (0, BA_W - (full.shape[1] - n_main))))

    def matmul(self, stage, a, b, **kw):
        comm = self._exchange(stage)
        if comm is None:
            return _matmul(a, b, name=stage, **kw)
        out, landed = _matmul(a, b, name=stage, comm=comm, **kw)
        self._landed(stage, landed)
        return out

    def call(self, stage, fn, *args, n_out):
        comm = self._exchange(stage)
        out = fn(*args, stage, comm=comm)
        if comm is not None:
            self._landed(stage, out[n_out:])
        return out[:n_out]


def _local_step(x2, target, wts, shards=None):
    plan = _Plan(shards, wts)
    s_len, d = x2.shape
    l = s_len + CHUNK
    tb = _tile(l, 3072)
    th = tb // 2 if (tb // 2) % 16 == 0 else tb
    half = RET_DK // 2
    inv_freq = (np.float32(ROPE_BASE) ** (-np.arange(half, dtype=np.float32) / np.float32(half))).astype(np.float32)
    ang = (np.arange(l) - PAD).astype(np.float32)[:, None] * inv_freq[None, :]
    cos, sin = jnp.asarray(np.cos(ang), f32), jnp.asarray(np.sin(ang), f32)
    lgs = jnp.log1p(-jnp.exp2(-5.0 - jnp.arange(RET_HEADS, dtype=f32)))
    gcs = jnp.exp(lgs * _ret_block(l))

    h0 = jnp.concatenate([jnp.zeros((PAD, d), f32), wts["meta"], x2], axis=0)
    mixw, ffnw = wts["mix_norm"], wts["ffn_norm"]

    hn0 = _rms_fwd(h0, mixw[0:1], "l0_norm")
    proj0 = plan.matmul("ret_proj", hn0, wts["ret_in"], mode="nn", tm=tb, tn=512, tk=d)
    qk0 = _ret_prep(proj0, cos, sin, "ret_prep")
    o0, st0 = plan.call("ret_scan", _ret_scan_fwd, qk0, proj0, lgs, gcs, n_out=2)
    y0 = _gnorm_fwd(o0, proj0, wts["ret_gn"], RET_HEADS, RET_DV, 2, "ret_gnorm")
    h1, hn1 = _matmul(y0, wts["ret_out"], mode="nn", tm=th // 2, tn=d, tk=RET_V, res=h0, norm_w=ffnw[0:1],
                      name="ret_out")
    (h2, hn2), ffn0 = _ffn_fwd(h1, hn1, wts["gate0"], wts["up0"], wts["down0"], tb, th, "ffn0", plan,
                               next_norm_w=mixw[1:2])

    proj1 = plan.matmul("dn_proj", hn2, wts["dn_main"], mode="nn", tm=tb, tn=512, tk=d)
    ba = _matmul(hn2, wts["dn_ba"], mode="nn", tm=tb, tn=BA_W, tk=d, name="dn_proj_ba")
    (qkv1,) = plan.call("dn_conv", _dn_conv_fwd, proj1, wts["conv_w"], n_out=1)
    tinv1, u1, wq1, pk1, egl1, kpt1, qwt1 = plan.call("dn_prep", _dn_prep, qkv1, ba, wts["a_log"], wts["dt_bias"],
                                                      n_out=7)
    o1, st1, vn1 = _dn_scan_fwd(u1, wq1, pk1, egl1, "dn_scan")
    y1 = _gnorm_fwd(o1, proj1, wts["dn_norm"], DN_HEADS, DN_DV, 2, "dn_gnorm")
    h3, hn3 = _matmul(y1, wts["dn_out"], mode="nn", tm=th // 2, tn=d, tk=DN_V, res=h2, norm_w=ffnw[1:2],
                      name="dn_out")
    h4, ffn1 = _ffn_fwd(h3, hn3, wts["gate1"], wts["up1"], wts["down1"], tb, th, "ffn1", plan)

    dh4, dh4b, dfinal, loss = _final_loss(h4, wts["final_norm"], target, "final_loss")
    gr = plan.grads
    dh3, dh3b, dffn1 = _ffn_bwd(dh4, dh4b, h3, ffnw[1:2], wts["gate1"], wts["up1"], wts["down1"], ffn1,
                                tb, th, "ffn1", plan)

    dy1 = _matmul(dh3b, wts["dn_out"], mode="nt", tm=th, tn=1024, tk=d, name="dn_b_dy")
    gr["dn_out"] = _matmul(y1, dh3b, mode="tn", tm=1024, tn=d, tk=tb, out_dtype=bf16, name="dn_b_dwout")
    do1, dproj1, ddn_norm = _gnorm_bwd(o1, proj1, wts["dn_norm"], dy1, DN_HEADS, DN_DV, 2, "dn_b_gnorm")
    dvn1, dsp1 = _dn_scan_bwd(do1, kpt1, qwt1, egl1, "dn_b_scan")
    dqkv1, dba, dalog, ddt = _dn_post_bwd(qkv1, ba, wts["a_log"], wts["dt_bias"], st1, dsp1, tinv1, u1, wq1, vn1,
                                          do1, dvn1, "dn_b_post")
    dc1, dconv = plan.call("dn_b_conv_a", _dn_conv_bwd_a, proj1, wts["conv_w"], dqkv1, n_out=2)
    dproj1 = _dn_conv_bwd_b(dc1, wts["conv_w"], dproj1, "dn_b_conv_b")
    dbab = dba.astype(bf16)
    n_main = dproj1.shape[1]
    dhn2_ba = _matmul(dbab, wts["dn_ba"], mode="nt", tm=th, tn=d, tk=BA_W, name="dn_b_dhn_ba")
    dh2, dh2b, dmix1 = plan.call("dn_b_dhn", functools.partial(_dhn_norm_bwd, tm=th // 2, tk=n_main // 4,
                                                              init=dhn2_ba),
                                 [(dproj1, wts["dn_main"])], h2, mixw[1:2], dh3, n_out=3)
    dw_main = _matmul(hn2, dproj1, mode="tn", tm=d, tn=512, tk=tb, out_dtype=bf16, name="dn_b_dwin")
    dw_ba = _matmul(hn2, dbab, mode="tn", tm=d, tn=BA_W, tk=tb, out_dtype=bf16, name="dn_b_dwin_ba")
    gr["dn_in"] = jnp.concatenate([dw_main, dw_ba], axis=1)

    dh1, dh1b, dffn0 = _ffn_bwd(dh2, dh2b, h1, ffnw[0:1], wts["gate0"], wts["up0"], wts["down0"], ffn0,
                                tb, th, "ffn0", plan)

    dy0 = _matmul(dh1b, wts["ret_out"], mode="nt", tm=th, tn=1024, tk=d, name="ret_b_dy")
    gr["ret_out"] = _matmul(y0, dh1b, mode="tn", tm=1024, tn=d, tk=tb, out_dtype=bf16, name="ret_b_dwout")
    do0, dproj0, dret_gn = _gnorm_bwd(o0, proj0, wts["ret_gn"], dy0, RET_HEADS, RET_DV, 2, "ret_b_gnorm")
    dq0, dk0, dproj0 = plan.call("ret_b_scan", _ret_scan_bwd, qk0, proj0, st0, do0, dproj0, lgs, gcs, n_out=3)
    dproj0 = _ret_prep_bwd(dq0, dk0, cos, sin, dproj0, "ret_b_prep")
    n_in = dproj0.shape[1]
    gr["ret_in"] = plan.matmul("ret_b_dwin", hn0, dproj0, mode="tn", tm=d, tn=512, tk=tb, out_dtype=bf16)
    dh0, _, dmix0 = plan.call("ret_b_dhn", functools.partial(_dhn_norm_bwd, tm=th // 2, tk=n_in // 4),
                              [(dproj0, wts["ret_in"])], h0, mixw[0:1], dh1, n_out=3)

    gr.update(meta=dh0[PAD:CHUNK], mix_norm=jnp.concatenate([dmix0, dmix1], axis=0),
              ffn_norm=jnp.concatenate([dffn0, dffn1], axis=0), ret_gn=dret_gn, conv_w=dconv, a_log=dalog,
              dt_bias=ddt, dn_norm=ddn_norm, final_norm=dfinal)
    return loss, dh0[CHUNK:], gr, plan


def _adamw_reduce(parts, w, m, v, name):
    _, r, c = parts.shape
    c_pad = -(-c // LANES) * LANES
    tr = _div_tile(r, max(8, (3 * MIB // 16) // c_pad // 8 * 8), 16)

    def body(p_ref, w_ref, m_ref, v_ref, g_ref, d_ref, nm_ref, nv_ref):
        g = p_ref[0].astype(f32)
        for s in range(1, N_DEV):
            g = g + p_ref[s].astype(f32)
        mm = ADAM_B1 * m_ref[...] + (1.0 - ADAM_B1) * g
        vv = ADAM_B2 * v_ref[...] + (1.0 - ADAM_B2) * (g * g)
        m_hat = mm / (1.0 - ADAM_B1 ** ADAM_STEP)
        v_hat = vv / (1.0 - ADAM_B2 ** ADAM_STEP)
        g_ref[...] = g
        d_ref[...] = -ADAM_LR * (m_hat / (jnp.sqrt(v_hat) + ADAM_EPS) + ADAM_WD * w_ref[...])
        nm_ref[...] = mm
        nv_ref[...] = vv

    blk = pl.BlockSpec((tr, c), lambda i: (i, 0))
    return pl.pallas_call(
        body, name=name, grid=(r // tr,),
        in_specs=[pl.BlockSpec((N_DEV, tr, c), lambda i: (0, i, 0)), blk, blk, blk], out_specs=[blk] * 4,
        out_shape=[jax.ShapeDtypeStruct((r, c), f32)] * 4,
        compiler_params=_params(("parallel",), 48),
    )(parts, w, m, v)


def _dev_major_cols(g, width):
    r = g.shape[0]
    return g[:, :N_DEV * width].reshape(r, N_DEV, width).transpose(1, 0, 2)


def kernel(x, meta_tokens, mix_norm_w, ffn_norm_w, ret_w_in, ret_gn_w, ret_w_out, dn_w_in, dn_conv_w, dn_a_log, dn_dt_bias, dn_norm_w, dn_w_out, ffn_w_gate, ffn_w_up, ffn_w_down, final_norm_w, loss_target, m_meta_tokens, m_mix_norm_w, m_ffn_norm_w, m_ret_w_in, m_ret_gn_w, m_ret_w_out, m_dn_w_in, m_dn_conv_w, m_dn_a_log, m_dn_dt_bias, m_dn_norm_w, m_dn_w_out, m_ffn_w_gate, m_ffn_w_up, m_ffn_w_down, m_final_norm_w, v_meta_tokens, v_mix_norm_w, v_ffn_norm_w, v_ret_w_in, v_ret_gn_w, v_ret_w_out, v_dn_w_in, v_dn_conv_w, v_dn_a_log, v_dn_dt_bias, v_dn_norm_w, v_dn_w_out, v_ffn_w_gate, v_ffn_w_up, v_ffn_w_down, v_final_norm_w):
    d = x.shape[-1]
    me = 4 * lax.axis_index("x") + 2 * lax.axis_index("y") + lax.axis_index("c")

    shards = dict(ret_in=ret_w_in[0].astype(bf16), ret_out=ret_w_out[0].astype(bf16),
                  dn_in=dn_w_in[0].astype(bf16), dn_out=dn_w_out[0].astype(bf16))
    shards["dn_in_top"], shards["dn_in_bottom"] = shards["dn_in"][:d // 2], shards["dn_in"][d // 2:]
    for layer in (0, 1):
        shards[f"gate{layer}"] = ffn_w_gate[layer].astype(bf16)
        shards[f"up{layer}"] = ffn_w_up[layer].astype(bf16)
        shards[f"down{layer}"] = ffn_w_down[layer].astype(bf16)
    g_ret_in, g_meta, g_conv, g_dnn = _exchange([shards["ret_in"], meta_tokens, dn_conv_w[0], dn_norm_w], True,
                                                "gather_first")
    cols = lambda g: g.transpose(1, 0, 2).reshape(g.shape[1], N_DEV * g.shape[2])
    wts = dict(meta=cols(g_meta), mix_norm=mix_norm_w, ffn_norm=ffn_norm_w, ret_in=cols(g_ret_in), ret_gn=ret_gn_w,
               conv_w=cols(g_conv), a_log=dn_a_log, dt_bias=dn_dt_bias, dn_norm=cols(g_dnn),
               final_norm=final_norm_w.reshape(1, d))

    loss_part, grad_x, gr, plan = _local_step(x[0], loss_target[0], wts, shards)
    loss = lax.psum(loss_part[0, 0], AXES)

    pp = plan.parts
    both = lambda name: jnp.concatenate([pp[name + "0"], pp[name + "1"]], axis=1)
    big_parts = [pp["ret_in"], pp["ret_out"], pp["dn_in"], pp["dn_out"], both("gate"), both("up"), both("down")]
    big_names = ["ret_w_in", "ret_w_out", "dn_w_in", "dn_w_out", "ffn_w_gate", "ffn_w_up", "ffn_w_down"]
    big_w = [ret_w_in, ret_w_out, dn_w_in, dn_w_out, ffn_w_gate, ffn_w_up, ffn_w_down]
    big_m = [m_ret_w_in, m_ret_w_out, m_dn_w_in, m_dn_w_out, m_ffn_w_gate, m_ffn_w_up, m_ffn_w_down]
    big_v = [v_ret_w_in, v_ret_w_out, v_dn_w_in, v_dn_w_out, v_ffn_w_gate, v_ffn_w_up, v_ffn_w_down]
    res = {}
    for nm, parts, w_, m_, v_ in zip(big_names, big_parts, big_w, big_m, big_v):
        r2, c2 = parts.shape[1], parts.shape[2]
        outs = _adamw_reduce(parts, w_.reshape(r2, c2), m_.reshape(r2, c2), v_.reshape(r2, c2), f"adamw_{nm}")
        res[nm] = [o.reshape(w_.shape) for o in outs]

    small_names = ["meta_tokens", "mix_norm_w", "ffn_norm_w", "ret_gn_w", "dn_conv_w", "dn_a_log", "dn_dt_bias",
                   "dn_norm_w", "final_norm_w"]
    small_g = [gr["meta"], gr["mix_norm"], gr["ffn_norm"], gr["ret_gn"], gr["conv_w"], gr["a_log"], gr["dt_bias"],
               gr["dn_norm"], gr["final_norm"]]
    small_w = [meta_tokens, mix_norm_w, ffn_norm_w, ret_gn_w, dn_conv_w, dn_a_log, dn_dt_bias, dn_norm_w, final_norm_w]
    small_m = [m_meta_tokens, m_mix_norm_w, m_ffn_norm_w, m_ret_gn_w, m_dn_conv_w, m_dn_a_log, m_dn_dt_bias,
               m_dn_norm_w, m_final_norm_w]
    small_v = [v_meta_tokens, v_mix_norm_w, v_ffn_norm_w, v_ret_gn_w, v_dn_conv_w, v_dn_a_log, v_dn_dt_bias,
               v_dn_norm_w, v_final_norm_w]
    sharded = {"meta_tokens", "dn_conv_w", "dn_norm_w"}
    flat = jnp.concatenate([g.reshape(-1) for g in small_g])
    row = 8 * LANES
    n_flat = flat.shape[0]
    flat = jnp.pad(flat, (0, -n_flat % row)).reshape(-1, row)
    (gathered,) = _exchange([flat], True, "gather_small_grads")
    gathered = gathered.reshape(N_DEV, -1)
    pieces, off = [], 0
    for nm, g, w_ in zip(small_names, small_g, small_w):
        full = gathered[:, off:off + g.size].reshape((N_DEV,) + g.shape)
        off += g.size
        if nm in sharded:
            wloc = w_.shape[-1]
            full = lax.dynamic_slice_in_dim(full, me * wloc, wloc, axis=full.ndim - 1)
        pieces.append(full.reshape(N_DEV, -1))
    sizes = [p.shape[1] for p in pieces]
    n_loc = sum(sizes)
    pad_loc = -n_loc % row

    def pack(vs, lead):
        cat = jnp.concatenate([a.reshape(lead + (-1,)) for a in vs], axis=-1)
        cat = jnp.pad(cat, [(0, 0)] * len(lead) + [(0, pad_loc)])
        return cat.reshape(lead + (-1, row))

    outs = _adamw_reduce(pack(pieces, (N_DEV,)), pack(small_w, ()), pack(small_m, ()), pack(small_v, ()), "adamw_small")
    off = 0
    for nm, sz, w_ in zip(small_names, sizes, small_w):
        res[nm] = [o.reshape(-1)[off:off + sz].reshape(w_.shape) for o in outs]
        off += sz

    order = ["meta_tokens", "mix_norm_w", "ffn_norm_w", "ret_w_in", "ret_gn_w", "ret_w_out", "dn_w_in", "dn_conv_w",
             "dn_a_log", "dn_dt_bias", "dn_norm_w", "dn_w_out", "ffn_w_gate", "ffn_w_up", "ffn_w_down", "final_norm_w"]
    grad_x = grad_x.reshape(x.shape)
    return (loss, grad_x, *[res[nm][0] for nm in order], *[res[nm][1] for nm in order],
            *[res[nm][2] for nm in order], *[res[nm][3] for nm in order])
```

```python
import functools
import math

import jax
import jax.numpy as jnp
import numpy as np
from jax import lax
from jax.experimental import pallas as pl
from jax.experimental.pallas import tpu as pltpu

f32 = jnp.float32
bf16 = jnp.bfloat16
HI = lax.Precision.HIGHEST

N_META = 16
CHUNK = 64
PAD = CHUNK - N_META
RMS_EPS = 1e-6
RET_HEADS, RET_DK, RET_DV = 4, 256, 512
RET_QK, RET_V = RET_HEADS * RET_DK, RET_HEADS * RET_DV
DN_HEADS, DN_DK, DN_DV = 8, 128, 256
DN_QK, DN_V = DN_HEADS * DN_DK, DN_HEADS * DN_DV
DN_CONV_CH = 2 * DN_QK + DN_V
CONV_K = 4
ROPE_BASE = 10000.0
ADAM_LR, ADAM_B1, ADAM_B2, ADAM_EPS, ADAM_WD, ADAM_STEP = 0.001, 0.9, 0.999, 1e-08, 0.01, 10
N_DEV = 8
AXES = ("x", "y", "c")
LANES = 128
MIB = 1024 * 1024


def _tile(n_rows, cap):
    nch = n_rows // CHUNK
    best = 1
    for d in range(1, nch + 1):
        if nch % d == 0 and d * CHUNK <= cap:
            best = d
    return best * CHUNK


def _div_tile(n, cap, align):
    best = None
    for d in range(align, min(n, cap) + 1, align):
        if n % d == 0:
            best = d
    return best if best is not None else n


def _params(sem, vmem_mb):
    return pltpu.CompilerParams(dimension_semantics=sem, vmem_limit_bytes=int(vmem_mb * MIB))


def _nn(a, b, precision=None):
    return jnp.dot(a, b, preferred_element_type=f32, precision=precision)


def _nt(a, b, precision=None):
    return lax.dot_general(a, b, (((1,), (1,)), ((), ())), preferred_element_type=f32, precision=precision)


def _tn(a, b, precision=None):
    return lax.dot_general(a, b, (((0,), (0,)), ((), ())), preferred_element_type=f32, precision=precision)


def _b(x):
    return x.astype(bf16)


def _sigmoid(x):
    return 0.5 * jnp.tanh(0.5 * x) + 0.5


def _silu(x):
    return x * _sigmoid(x)


def _dsilu(x):
    s = _sigmoid(x)
    return s * (1.0 + x * (1.0 - s))


def _peer(k):
    x, y, c = lax.axis_index("x"), lax.axis_index("y"), lax.axis_index("c")
    px = 1 - x if k & 4 else x
    py = 1 - y if k & 2 else y
    pc = 1 - c if k & 1 else c
    return (px, py, pc), 4 * px + 2 * py + pc


class _Exchange:
    def __init__(self, arrs, gather):
        self.arrs, self.gather, self.n = list(arrs), gather, len(arrs)
        self.out_shapes = [jax.ShapeDtypeStruct(((N_DEV,) + a.shape) if gather else a.shape, a.dtype) for a in arrs]
        self.specs = [pl.BlockSpec(memory_space=pltpu.HBM)] * self.n
        self.scratch = [pltpu.SemaphoreType.DMA((self.n, N_DEV - 1)), pltpu.SemaphoreType.DMA((self.n, N_DEV - 1)),
                        pltpu.SemaphoreType.DMA((self.n,))]

    def _copies(self, ins, outs, sems):
        send_sems, recv_sems, local_sems = sems
        me = 4 * lax.axis_index("x") + 2 * lax.axis_index("y") + lax.axis_index("c")
        src = (lambda a, dest: ins[a]) if self.gather else (lambda a, dest: ins[a].at[dest])
        local = [pltpu.make_async_copy(src(a, me), outs[a].at[me], local_sems.at[a]) for a in range(self.n)]
        sends, lands = [], []
        for k in range(1, N_DEV):
            peer, pidx = _peer(k)
            for a in range(self.n):
                for dst, lst in ((outs[a].at[me], sends), (outs[a].at[pidx], lands)):
                    lst.append(pltpu.make_async_remote_copy(
                        src_ref=src(a, pidx), dst_ref=dst, send_sem=send_sems.at[a, k - 1],
                        recv_sem=recv_sems.at[a, k - 1], device_id=peer, device_id_type=pl.DeviceIdType.MESH))
        return local, sends, lands

    def start(self, ins, outs, sems):
        local, sends, _ = self._copies(ins, outs, sems)
        for cp in local + sends:
            cp.start()

    def wait(self, ins, outs, sems):
        local, sends, lands = self._copies(ins, outs, sems)
        for cp in lands:
            cp.wait_recv()
        for cp in sends:
            cp.wait_send()
        for cp in local:
            cp.wait()


def _call(body, args, *, name, grid, in_specs, out_specs, out_shape, scratch=(), sem, vmem_mb, comm=None,
          aliases=None):
    aliases = aliases or {}
    if comm is None:
        out = pl.pallas_call(body, name=name, grid=grid, in_specs=list(in_specs), out_specs=list(out_specs),
                             out_shape=list(out_shape), scratch_shapes=list(scratch), input_output_aliases=aliases,
                             compiler_params=_params(sem, vmem_mb))(*args)
        return list(out)
    n_in, n_out, n_scr, nc = len(args), len(out_shape), len(scratch), comm.n

    def carried(*refs):
        ins, cin = refs[:n_in], refs[n_in:n_in + nc]
        o0 = n_in + nc
        outs, cout = refs[o0:o0 + n_out], refs[o0 + n_out:o0 + n_out + nc]
        s0 = o0 + n_out + nc
        scr, sems = refs[s0:s0 + n_scr], refs[s0 + n_scr:]
        first = functools.reduce(jnp.logical_and, [pl.program_id(i) == 0 for i in range(len(grid))])
        last = functools.reduce(jnp.logical_and, [pl.program_id(i) == grid[i] - 1 for i in range(len(grid))])

        @pl.when(first)
        def _():
            comm.start(cin, cout, sems)

        body(*ins, *outs, *scr)

        @pl.when(last)
        def _():
            comm.wait(cin, cout, sems)

    out = pl.pallas_call(
        carried, name=name, grid=grid, in_specs=list(in_specs) + comm.specs, out_specs=list(out_specs) + comm.specs,
        out_shape=list(out_shape) + comm.out_shapes, scratch_shapes=list(scratch) + comm.scratch,
        input_output_aliases=aliases,
        compiler_params=_params(("arbitrary",) * len(grid), vmem_mb))(*args, *comm.arrs)
    return list(out)


def _exchange(arrs, gather, name):
    comm = _Exchange(arrs, gather)

    def body(*refs):
        ins, outs, sems = refs[:comm.n], refs[comm.n:2 * comm.n], refs[2 * comm.n:]
        comm.start(ins, outs, sems)
        comm.wait(ins, outs, sems)

    return pl.pallas_call(body, name=name, in_specs=comm.specs, out_specs=comm.specs, out_shape=comm.out_shapes,
                          scratch_shapes=comm.scratch)(*comm.arrs)


def _matmul(a, b, *, mode, tm, tn, tk, name, out_dtype=f32, res=None, vmem_mb=48, comm=None, pair2=None,
            norm_w=None):
    def dims(x, y):
        if mode == "nn":
            (m_, k_), (k2_, n_) = x.shape, y.shape
        elif mode == "nt":
            (m_, k_), (n_, k2_) = x.shape, y.shape
        else:
            (k_, m_), (k2_, n_) = x.shape, y.shape
        assert k_ == k2_ and m_ % tm == 0 and n_ % tn == 0 and k_ % tk == 0, (name, x.shape, y.shape, tm, tn, tk)
        return m_, n_, k_ // tk

    m, n, nk1 = dims(a, b)
    nk2 = dims(*pair2)[2] if pair2 is not None else 0
    nk = nk1 + nk2
    dot = {"nn": _nn, "nt": _nt, "tn": _tn}[mode]

    def specs(k_of):
        a_spec = {"nn": pl.BlockSpec((tm, tk), lambda i, j, kk: (i, k_of(kk))),
                  "nt": pl.BlockSpec((tm, tk), lambda i, j, kk: (i, k_of(kk))),
                  "tn": pl.BlockSpec((tk, tm), lambda i, j, kk: (k_of(kk), i))}[mode]
        b_spec = {"nn": pl.BlockSpec((tk, tn), lambda i, j, kk: (k_of(kk), j)),
                  "nt": pl.BlockSpec((tn, tk), lambda i, j, kk: (j, k_of(kk))),
                  "tn": pl.BlockSpec((tk, tn), lambda i, j, kk: (k_of(kk), j))}[mode]
        return [a_spec, b_spec]

    o_spec = pl.BlockSpec((tm, tn), lambda i, j, kk: (i, j))
    has_res = res is not None
    has_norm = norm_w is not None
    assert not has_norm or tn == n
    n_mm = 2 + (2 if pair2 is not None else 0)
    n_ops = n_mm + has_res + has_norm

    def body(*refs):
        a_ref, b_ref = refs[:2]
        a2_ref, b2_ref = refs[2:4] if pair2 is not None else (None, None)
        r_ref = refs[n_mm] if has_res else None
        nw_ref = refs[n_mm + has_res] if has_norm else None
        o_ref = refs[n_ops]
        hn_ref = refs[n_ops + 1] if has_norm else None
        rest = refs[n_ops + 1 + has_norm:]

        def finish(tot):
            if has_res:
                tot = tot + r_ref[...]
            o_ref[...] = tot.astype(out_dtype)
            if has_norm:
                r = lax.rsqrt(jnp.mean(tot * tot, axis=-1, keepdims=True) + RMS_EPS)
                hn_ref[...] = _b(tot * r * nw_ref[...])

        if nk == 1:
            finish(dot(_b(a_ref[...]), _b(b_ref[...])))
            return
        acc_ref = rest[0]
        kk = pl.program_id(2)

        @pl.when(kk == 0)
        def _():
            acc_ref[...] = dot(_b(a_ref[...]), _b(b_ref[...]))

        @pl.when(jnp.logical_and(kk > 0, kk < nk1))
        def _():
            acc_ref[...] += dot(_b(a_ref[...]), _b(b_ref[...]))

        if pair2 is not None:
            @pl.when(kk >= nk1)
            def _():
                acc_ref[...] += dot(_b(a2_ref[...]), _b(b2_ref[...]))

        @pl.when(kk == nk - 1)
        def _():
            finish(acc_ref[...])

    in_specs = specs(lambda kk: jnp.minimum(kk, nk1 - 1))
    args = (a, b)
    if pair2 is not None:
        in_specs += specs(lambda kk: jnp.maximum(kk - nk1, 0))
        args += tuple(pair2)
    if has_res:
        in_specs.append(o_spec)
        args += (res,)
    out_specs, out_shape = [o_spec], [jax.ShapeDtypeStruct((m, n), out_dtype)]
    if has_norm:
        in_specs.append(pl.BlockSpec((1, tn), lambda i, j, kk: (0, j)))
        args += (norm_w,)
        out_specs.append(o_spec)
        out_shape.append(jax.ShapeDtypeStruct((m, n), bf16))
    out = _call(body, args, name=name, grid=(m // tm, n // tn, nk), in_specs=in_specs, out_specs=out_specs,
                out_shape=out_shape, scratch=[pltpu.VMEM((tm, tn), f32)] if nk > 1 else [],
                sem=("parallel", "parallel", "arbitrary"), vmem_mb=vmem_mb, comm=comm)
    n_own = len(out_shape)
    own = out[0] if n_own == 1 else tuple(out[:n_own])
    return own if comm is None else (own, out[n_own:])


def _rms_fwd(h, w, name):
    l, d = h.shape
    tr = _tile(l, 512)

    def body(h_ref, w_ref, o_ref):
        x = h_ref[...]
        r = lax.rsqrt(jnp.mean(x * x, axis=-1, keepdims=True) + RMS_EPS)
        o_ref[...] = _b(x * r * w_ref[...])

    return pl.pallas_call(
        body, name=name, grid=(l // tr,),
        in_specs=[pl.BlockSpec((tr, d), lambda i: (i, 0)), pl.BlockSpec((1, d), lambda i: (0, 0))],
        out_specs=pl.BlockSpec((tr, d), lambda i: (i, 0)),
        out_shape=jax.ShapeDtypeStruct((l, d), bf16),
        compiler_params=_params(("parallel",), 32),
    )(h, w)


def _rms_bwd(h, w, dhn, dres, name):
    l, d = h.shape
    tr = _tile(l, 512)

    def body(h_ref, w_ref, g_ref, r_ref, dh_ref, dhb_ref, dw_ref):
        x = h_ref[...]
        r = lax.rsqrt(jnp.mean(x * x, axis=-1, keepdims=True) + RMS_EPS)
        xh = x * r
        g = g_ref[...]
        dxh = g * w_ref[...]
        dx = r * (dxh - xh * jnp.mean(dxh * xh, axis=-1, keepdims=True))
        dh = r_ref[...] + dx
        dh_ref[...] = dh
        dhb_ref[...] = _b(dh)
        dw = jnp.sum(g * xh, axis=0, keepdims=True)

        @pl.when(pl.program_id(0) == 0)
        def _():
            dw_ref[...] = dw

        @pl.when(pl.program_id(0) > 0)
        def _():
            dw_ref[...] += dw

    row = pl.BlockSpec((tr, d), lambda i: (i, 0))
    vec = pl.BlockSpec((1, d), lambda i: (0, 0))
    return pl.pallas_call(
        body, name=name, grid=(l // tr,), in_specs=[row, vec, row, row], out_specs=[row, row, vec],
        out_shape=[jax.ShapeDtypeStruct((l, d), f32), jax.ShapeDtypeStruct((l, d), bf16),
                   jax.ShapeDtypeStruct((1, d), f32)],
        compiler_params=_params(("arbitrary",), 40),
    )(h, w, dhn, dres)


def _dhn_norm_bwd(pairs, h, nw, dres, name, *, tm, tk, init=None, comm=None):
    l, d = h.shape
    nks = [a.shape[1] // tk for a, _ in pairs]
    starts = [sum(nks[:p]) for p in range(len(pairs))]
    nk = sum(nks)
    n_ops = 2 * len(pairs)
    has_init = init is not None

    def body(*refs):
        ops = refs[:n_ops]
        init_ref = refs[n_ops] if has_init else None
        h_ref, w_ref, r_ref, dh_ref, dhb_ref, dw_ref, acc_ref = refs[n_ops + has_init:]
        i, kk = pl.program_id(0), pl.program_id(1)

        @pl.when(kk == 0)
        def _():
            part = _nt(ops[0][...], ops[1][...])
            acc_ref[...] = part + init_ref[...] if has_init else part

        for p in range(len(pairs)):
            lo = max(starts[p], 1)

            @pl.when(jnp.logical_and(kk >= lo, kk < starts[p] + nks[p]))
            def _(a_ref=ops[2 * p], b_ref=ops[2 * p + 1]):
                acc_ref[...] += _nt(a_ref[...], b_ref[...])

        @pl.when(kk == nk - 1)
        def _():
            g = acc_ref[...]
            x = h_ref[...]
            r = lax.rsqrt(jnp.mean(x * x, axis=-1, keepdims=True) + RMS_EPS)
            xh = x * r
            dxh = g * w_ref[...]
            dh = r_ref[...] + r * (dxh - xh * jnp.mean(dxh * xh, axis=-1, keepdims=True))
            dh_ref[...] = dh
            dhb_ref[...] = _b(dh)
            dw = jnp.sum(g * xh, axis=0, keepdims=True)

            @pl.when(i == 0)
            def _():
                dw_ref[...] = dw

            @pl.when(i > 0)
            def _():
                dw_ref[...] += dw

    def k_of(p):
        return lambda kk: jnp.clip(kk - starts[p], 0, nks[p] - 1)

    in_specs, args = [], []
    for p, (a, b) in enumerate(pairs):
        in_specs += [pl.BlockSpec((tm, tk), functools.partial(lambda i, kk, f: (i, f(kk)), f=k_of(p))),
                     pl.BlockSpec((d, tk), functools.partial(lambda i, kk, f: (0, f(kk)), f=k_of(p)))]
        args += [a, b]
    row = pl.BlockSpec((tm, d), lambda i, kk: (i, 0))
    vec = pl.BlockSpec((1, d), lambda i, kk: (0, 0))
    if has_init:
        in_specs.append(row)
        args.append(init)
    return _call(body, tuple(args) + (h, nw, dres), name=name, grid=(l // tm, nk), in_specs=in_specs + [row, vec, row],
                 out_specs=[row, row, vec],
                 out_shape=[jax.ShapeDtypeStruct((l, d), f32), jax.ShapeDtypeStruct((l, d), bf16),
                            jax.ShapeDtypeStruct((1, d), f32)],
                 scratch=[pltpu.VMEM((tm, d), f32)], sem=("arbitrary", "arbitrary"), vmem_mb=48, comm=comm)


def _final_loss(h, w, target, name):
    l, d = h.shape
    nch = l // CHUNK
    cpt = _tile(l, 256) // CHUNK
    nt = nch // cpt

    def body(h_ref, w_ref, *rest):
        t_refs, (dh_ref, dhb_ref, dw_ref, loss_ref) = rest[:cpt], rest[cpt:]
        i = pl.program_id(0)
        wv = w_ref[...]
        dw = jnp.zeros((1, d), f32)
        part = jnp.zeros((1, 1), f32)
        for c in range(cpt):
            rows = slice(c * CHUNK, (c + 1) * CHUNK)
            live = (i * cpt + c > 0).astype(f32)
            x = h_ref[rows, :]
            r = lax.rsqrt(jnp.mean(x * x, axis=-1, keepdims=True) + RMS_EPS)
            xh = x * r
            err = (xh * wv - t_refs[c][...]) * live
            dy = err * (1.0 / d)
            dxh = dy * wv
            dx = r * (dxh - xh * jnp.mean(dxh * xh, axis=-1, keepdims=True))
            dh_ref[rows, :] = dx
            dhb_ref[rows, :] = _b(dx)
            dw = dw + jnp.sum(dy * xh, axis=0, keepdims=True)
            part = part + 0.5 * jnp.sum(jnp.sum(err * err, axis=-1, keepdims=True) * (1.0 / d), axis=0, keepdims=True)
        part = jnp.broadcast_to(part, (1, LANES))

        @pl.when(i == 0)
        def _():
            dw_ref[...] = dw
            loss_ref[...] = part

        @pl.when(i > 0)
        def _():
            dw_ref[...] += dw
            loss_ref[...] += part

    row = pl.BlockSpec((cpt * CHUNK, d), lambda i: (i, 0))
    vec = pl.BlockSpec((1, d), lambda i: (0, 0))
    t_specs = [pl.BlockSpec((CHUNK, d), functools.partial(lambda i, c: (jnp.maximum(i * cpt + c - 1, 0), 0), c=c))
               for c in range(cpt)]
    return pl.pallas_call(
        body, name=name, grid=(nt,),
        in_specs=[row, vec] + t_specs,
        out_specs=[row, row, vec, pl.BlockSpec((1, LANES), lambda i: (0, 0))],
        out_shape=[jax.ShapeDtypeStruct((l, d), f32), jax.ShapeDtypeStruct((l, d), bf16),
                   jax.ShapeDtypeStruct((1, d), f32), jax.ShapeDtypeStruct((1, LANES), f32)],
        compiler_params=_params(("arbitrary",), 32),
    )(h, w, *([target] * cpt))


def _ffn_gu(hn, wg, wu, name, *, tm, tn, comm=None):
    l, d = hn.shape
    fh = wg.shape[1]

    def body(h_ref, g_ref, u_ref, a_ref, b_ref, s_ref):
        hb = h_ref[...]
        a = _nn(hb, g_ref[...])
        bb = _nn(hb, u_ref[...])
        a_ref[...] = _b(a)
        b_ref[...] = _b(bb)
        s_ref[...] = _b(_silu(a) * bb)

    wspec = pl.BlockSpec((d, tn), lambda i, j: (0, j))
    ospec = pl.BlockSpec((tm, tn), lambda i, j: (i, j))
    return _call(body, (hn, wg, wu), name=name, grid=(l // tm, fh // tn),
                 in_specs=[pl.BlockSpec((tm, d), lambda i, j: (i, 0)), wspec, wspec], out_specs=[ospec] * 3,
                 out_shape=[jax.ShapeDtypeStruct((l, fh), bf16)] * 3, sem=("parallel", "parallel"), vmem_mb=48,
                 comm=comm)


def _ffn_ds(dhb, wd, a, b, *, tm, tn, name):
    l, d = dhb.shape
    fh = wd.shape[0]

    def body(g_ref, w_ref, a_ref, b_ref, da_ref, db_ref):
        ds = _nt(g_ref[...], w_ref[...])
        a = a_ref[...].astype(f32)
        da_ref[...] = _b(ds * b_ref[...].astype(f32) * _dsilu(a))
        db_ref[...] = _b(ds * _silu(a))

    ospec = pl.BlockSpec((tm, tn), lambda i, j: (i, j))
    return pl.pallas_call(
        body, name=name, grid=(l // tm, fh // tn),
        in_specs=[pl.BlockSpec((tm, d), lambda i, j: (i, 0)), pl.BlockSpec((tn, d), lambda i, j: (j, 0)), ospec, ospec],
        out_specs=[ospec, ospec], out_shape=[jax.ShapeDtypeStruct((l, fh), bf16)] * 2,
        compiler_params=_params(("parallel", "parallel"), 48),
    )(dhb, wd, a, b)


def _gnorm_fwd(o, proj, nw, heads, dv, gate_blk, name):
    l, hv = o.shape
    tr = _tile(l, 256)

    def body(o_ref, g_ref, w_ref, y_ref):
        wv = w_ref[...]
        for h in range(heads):
            sl = slice(h * dv, (h + 1) * dv)
            oh = o_ref[:, sl]
            r = lax.rsqrt(jnp.mean(oh * oh, axis=-1, keepdims=True) + RMS_EPS)
            y_ref[:, sl] = _b(oh * r * wv * _silu(g_ref[:, sl]))

    return pl.pallas_call(
        body, name=name, grid=(l // tr,),
        in_specs=[pl.BlockSpec((tr, hv), lambda i: (i, 0)), pl.BlockSpec((tr, hv), lambda i: (i, gate_blk)),
                  pl.BlockSpec((1, dv), lambda i: (0, 0))],
        out_specs=pl.BlockSpec((tr, hv), lambda i: (i, 0)),
        out_shape=jax.ShapeDtypeStruct((l, hv), bf16),
        compiler_params=_params(("parallel",), 32),
    )(o, proj, nw)


def _dy_gnorm_bwd(dhb, w_out, o, proj, nw, dv, gate_blk, name, *, tm, tn):
    l, hv = o.shape
    d = dhb.shape[1]
    nj = hv // tn
    heads = tn // dv

    def body(g_ref, w_ref, o_ref, gate_ref, nw_ref, do_ref, dg_ref, dw_ref):
        dy = _nt(g_ref[...], w_ref[...])
        wv = nw_ref[...]
        dw = jnp.zeros((1, dv), f32)
        for h in range(heads):
            sl = slice(h * dv, (h + 1) * dv)
            oh = o_ref[:, sl]
            g = gate_ref[:, sl]
            dyh = dy[:, sl]
            r = lax.rsqrt(jnp.mean(oh * oh, axis=-1, keepdims=True) + RMS_EPS)
            xh = oh * r
            dn = dyh * _silu(g)
            dg_ref[:, sl] = _b(dyh * (xh * wv) * _dsilu(g))
            dxh = dn * wv
            do_ref[:, sl] = r * (dxh - xh * jnp.mean(dxh * xh, axis=-1, keepdims=True))
            dw = dw + jnp.sum(dn * xh, axis=0, keepdims=True)
        first = jnp.logical_and(pl.program_id(0) == 0, pl.program_id(1) == 0)

        @pl.when(first)
        def _():
            dw_ref[...] = dw

        @pl.when(jnp.logical_not(first))
        def _():
            dw_ref[...] += dw

    tile = pl.BlockSpec((tm, tn), lambda i, j: (i, j))
    gate = pl.BlockSpec((tm, tn), lambda i, j: (i, gate_blk * nj + j))
    vec = pl.BlockSpec((1, dv), lambda i, j: (0, 0))
    return pl.pallas_call(
        body, name=name, grid=(l // tm, nj),
        in_specs=[pl.BlockSpec((tm, d), lambda i, j: (i, 0)), pl.BlockSpec((tn, d), lambda i, j: (j, 0)),
                  tile, gate, vec],
        out_specs=[tile, gate, vec],
        out_shape=[jax.ShapeDtypeStruct((l, hv), f32), jax.ShapeDtypeStruct(proj.shape, bf16),
                   jax.ShapeDtypeStruct((1, dv), f32)],
        compiler_params=_params(("arbitrary", "arbitrary"), 48),
    )(dhb, w_out, o, proj, nw)


def _ret_prep(proj, cos, sin, name):
    l = proj.shape[0]
    tr = _tile(l, 256)
    half = RET_DK // 2
    scale = RET_DK ** -0.5

    def body(p_ref, c_ref, s_ref, o_ref):
        rows = pl.program_id(0) * tr + lax.broadcasted_iota(jnp.int32, (tr, 1), 0)
        kmul = jnp.where(rows >= PAD, scale, 0.0).astype(f32)
        c, s = c_ref[...], s_ref[...]
        for j in range(2 * RET_HEADS):
            t1 = p_ref[:, j * RET_DK: j * RET_DK + half]
            t2 = p_ref[:, j * RET_DK + half: (j + 1) * RET_DK]
            o1 = t1 * c - t2 * s
            o2 = t1 * s + t2 * c
            if j >= RET_HEADS:
                o1, o2 = o1 * kmul, o2 * kmul
            o_ref[:, j * RET_DK: j * RET_DK + half] = o1
            o_ref[:, j * RET_DK + half: (j + 1) * RET_DK] = o2

    wide = pl.BlockSpec((tr, 2 * RET_QK), lambda i: (i, 0))
    tab = pl.BlockSpec((tr, half), lambda i: (i, 0))
    return pl.pallas_call(
        body, name=name, grid=(l // tr,), in_specs=[wide, tab, tab], out_specs=wide,
        out_shape=jax.ShapeDtypeStruct((l, 2 * RET_QK), f32),
        compiler_params=_params(("parallel",), 32),
    )(proj, cos, sin)


def _ret_prep_bwd(dq, dk, cos, sin, dproj, name):
    l = dq.shape[0]
    tr = _tile(l, 256)
    half = RET_DK // 2
    scale = RET_DK ** -0.5

    def body(dq_ref, dk_ref, c_ref, s_ref, _, o_ref):
        rows = pl.program_id(0) * tr + lax.broadcasted_iota(jnp.int32, (tr, 1), 0)
        kmul = jnp.where(rows >= PAD, scale, 0.0).astype(f32)
        c, s = c_ref[...], s_ref[...]
        for j in range(2 * RET_HEADS):
            d_ref = dq_ref if j < RET_HEADS else dk_ref
            jj = j % RET_HEADS
            d1 = d_ref[:, jj * RET_DK: jj * RET_DK + half]
            d2 = d_ref[:, jj * RET_DK + half: (jj + 1) * RET_DK]
            if j >= RET_HEADS:
                d1, d2 = d1 * kmul, d2 * kmul
            o_ref[:, j * RET_DK: j * RET_DK + half] = _b(d1 * c + d2 * s)
            o_ref[:, j * RET_DK + half: (j + 1) * RET_DK] = _b(d2 * c - d1 * s)

    nar = pl.BlockSpec((tr, RET_QK), lambda i: (i, 0))
    wide = pl.BlockSpec((tr, 2 * RET_QK), lambda i: (i, 0))
    tab = pl.BlockSpec((tr, half), lambda i: (i, 0))
    return pl.pallas_call(
        body, name=name, grid=(l // tr,), in_specs=[nar, nar, tab, tab, pl.BlockSpec(memory_space=pl.ANY)],
        out_specs=wide, out_shape=jax.ShapeDtypeStruct(dproj.shape, dproj.dtype), input_output_aliases={4: 0},
        compiler_params=_params(("parallel",), 32),
    )(dq, dk, cos, sin, dproj)


RET_BLOCK_CHUNKS = 3


def _ret_block(l):
    nch = l // CHUNK
    return RET_BLOCK_CHUNKS * CHUNK if nch % RET_BLOCK_CHUNKS == 0 else CHUNK


def _ret_decay(lg, rb):
    idx = lax.broadcasted_iota(jnp.int32, (rb, 1), 0).astype(f32)
    ri = lax.broadcasted_iota(jnp.int32, (rb, rb), 0)
    ci = lax.broadcasted_iota(jnp.int32, (rb, rb), 1)
    rel = (ri - ci).astype(f32)
    dmask = jnp.where(ri >= ci, jnp.exp(lg * jnp.maximum(rel, 0.0)), 0.0)
    xi = jnp.exp(lg * (idx + 1.0))
    zeta = jnp.exp(lg * (rb - 1.0 - idx))
    return dmask, xi, zeta


def _ret_scan_fwd(qk, proj, lgs, gcs, name, comm=None):
    l = qk.shape[0]
    rb = _ret_block(l)
    nb = l // rb

    def body(lg_ref, gc_ref, q_ref, k_ref, v_ref, o_ref, st_ref, s_ref):
        @pl.when(pl.program_id(0) == 0)
        def _():
            s_ref[...] = jnp.zeros_like(s_ref)

        hs = range(RET_HEADS)
        dec = [_ret_decay(lg_ref[h], rb) for h in hs]
        q = [q_ref[:, h * RET_DK:(h + 1) * RET_DK] for h in hs]
        k = [k_ref[:, h * RET_DK:(h + 1) * RET_DK] for h in hs]
        vb = [_b(v_ref[:, h * RET_DV:(h + 1) * RET_DV]) for h in hs]
        s = [s_ref[h] for h in hs]
        sb = [_b(s[h]) for h in hs]
        scores = [_b(_nt(_b(q[h]), _b(k[h])) * dec[h][0]) for h in hs]
        inter = [_nn(_b(q[h] * dec[h][1]), sb[h]) for h in hs]
        kv = [_tn(_b(k[h] * dec[h][2]), vb[h]) for h in hs]
        for h in hs:
            st_ref[0, h] = sb[h]
            o_ref[:, h * RET_DV:(h + 1) * RET_DV] = _nn(scores[h], vb[h]) + inter[h]
            s_ref[h] = gc_ref[h] * s[h] + kv[h]

    smem = pl.BlockSpec(memory_space=pltpu.SMEM)
    return _call(
        body, (lgs, gcs, qk, qk, proj), name=name, grid=(nb,),
        in_specs=[smem, smem,
                  pl.BlockSpec((rb, RET_QK), lambda n: (n, 0)),
                  pl.BlockSpec((rb, RET_QK), lambda n: (n, 1)),
                  pl.BlockSpec((rb, RET_V), lambda n: (n, 1))],
        out_specs=[pl.BlockSpec((rb, RET_V), lambda n: (n, 0)),
                   pl.BlockSpec((1, RET_HEADS, RET_DK, RET_DV), lambda n: (n, 0, 0, 0))],
        out_shape=[jax.ShapeDtypeStruct((l, RET_V), f32),
                   jax.ShapeDtypeStruct((nb, RET_HEADS, RET_DK, RET_DV), bf16)],
        scratch=[pltpu.VMEM((RET_HEADS, RET_DK, RET_DV), f32)], sem=("arbitrary",), vmem_mb=40, comm=comm)


def _ret_scan_bwd(qk, proj, states, do, dproj, lgs, gcs, name, comm=None):
    l = qk.shape[0]
    rb = _ret_block(l)
    nb = l // rb

    def body(lg_ref, gc_ref, q_ref, k_ref, v_ref, st_ref, do_ref, _, dq_ref, dk_ref, dv_ref, ds_ref):
        @pl.when(pl.program_id(0) == 0)
        def _():
            ds_ref[...] = jnp.zeros_like(ds_ref)

        hs = range(RET_HEADS)
        dec = [_ret_decay(lg_ref[h], rb) for h in hs]
        q = [q_ref[:, h * RET_DK:(h + 1) * RET_DK] for h in hs]
        k = [k_ref[:, h * RET_DK:(h + 1) * RET_DK] for h in hs]
        qb, kb = [_b(t) for t in q], [_b(t) for t in k]
        vb = [_b(v_ref[:, h * RET_DV:(h + 1) * RET_DV]) for h in hs]
        dob = [_b(do_ref[:, h * RET_DV:(h + 1) * RET_DV]) for h in hs]
        dsp = [ds_ref[h] for h in hs]
        dspb = [_b(t) for t in dsp]
        scores = [_b(_nt(qb[h], kb[h]) * dec[h][0]) for h in hs]
        dscores = [_b(_nt(dob[h], vb[h]) * dec[h][0]) for h in hs]
        for h in hs:
            dq_ref[:, h * RET_DK:(h + 1) * RET_DK] = _nn(dscores[h], kb[h]) + _nt(dob[h], st_ref[0, h]) * dec[h][1]
        for h in hs:
            dk_ref[:, h * RET_DK:(h + 1) * RET_DK] = _tn(dscores[h], qb[h]) + _nt(vb[h], dspb[h]) * dec[h][2]
        for h in hs:
            dv_ref[:, h * RET_DV:(h + 1) * RET_DV] = _b(_tn(scores[h], dob[h]) + _nn(_b(k[h] * dec[h][2]), dspb[h]))
        for h in hs:
            ds_ref[h] = gc_ref[h] * dsp[h] + _tn(_b(q[h] * dec[h][1]), dob[h])

    smem = pl.BlockSpec(memory_space=pltpu.SMEM)
    rev = lambda n: nb - 1 - n
    return _call(
        body, (lgs, gcs, qk, qk, proj, states, do, dproj), name=name, grid=(nb,),
        in_specs=[smem, smem,
                  pl.BlockSpec((rb, RET_QK), lambda n: (rev(n), 0)),
                  pl.BlockSpec((rb, RET_QK), lambda n: (rev(n), 1)),
                  pl.BlockSpec((rb, RET_V), lambda n: (rev(n), 1)),
                  pl.BlockSpec((1, RET_HEADS, RET_DK, RET_DV), lambda n: (rev(n), 0, 0, 0)),
                  pl.BlockSpec((rb, RET_V), lambda n: (rev(n), 0)),
                  pl.BlockSpec(memory_space=pl.ANY)],
        out_specs=[pl.BlockSpec((rb, RET_QK), lambda n: (rev(n), 0)),
                   pl.BlockSpec((rb, RET_QK), lambda n: (rev(n), 0)),
                   pl.BlockSpec((rb, RET_V), lambda n: (rev(n), 1))],
        out_shape=[jax.ShapeDtypeStruct((l, RET_QK), f32), jax.ShapeDtypeStruct((l, RET_QK), f32),
                   jax.ShapeDtypeStruct(dproj.shape, dproj.dtype)],
        scratch=[pltpu.VMEM((RET_HEADS, RET_DK, RET_DV), f32)], sem=("arbitrary",), vmem_mb=40, comm=comm,
        aliases={7: 2})


CONV_BLK = 1024
HALO = 8


def _slab_rows(r):
    return pl.ds(pl.multiple_of(r * HALO, HALO), HALO)


def _conv_slab(x_ref, p_ref, r, i, tr):
    cur = x_ref[_slab_rows(r), :]
    prev = jnp.where(r > 0, x_ref[_slab_rows(jnp.maximum(r - 1, 0)), :], p_ref[...])
    row0 = i * tr + r * HALO
    cur = jnp.where(row0 >= PAD, cur, 0.0)
    prev = jnp.where(row0 - HALO >= PAD, prev, 0.0)
    lrow = lax.broadcasted_iota(jnp.int32, (HALO, 1), 0)
    shifted = [jnp.where(lrow < s, pltpu.roll(prev, s, 0), pltpu.roll(cur, s, 0)) for s in range(1, CONV_K)]
    return [cur] + shifted


def _conv_of(xs, w):
    acc = xs[0] * w[CONV_K - 1:CONV_K, :]
    for s in range(1, CONV_K):
        acc = acc + xs[s] * w[CONV_K - 1 - s:CONV_K - s, :]
    return acc


def _slab_loop(n_slabs, fn, init=None):
    return lax.fori_loop(0, n_slabs, fn, init, unroll=8)


def _dn_conv_fwd(proj, conv_w, name, comm=None):
    l = proj.shape[0]
    tr = _tile(l, 256)
    nblk = DN_CONV_CH // CONV_BLK
    heads = CONV_BLK // DN_DK

    def body(x_ref, p_ref, w_ref, o_ref):
        i, j = pl.program_id(0), pl.program_id(1)
        w = w_ref[...]

        def act(r):
            return _silu(_conv_of(_conv_slab(x_ref, p_ref, r, i, tr), w))

        def normed(scale):
            def slab(r, carry):
                a = act(r)
                outs = []
                for h in range(heads):
                    ah = a[:, h * DN_DK:(h + 1) * DN_DK]
                    outs.append(ah * (lax.rsqrt(jnp.sum(ah * ah, axis=-1, keepdims=True) + RMS_EPS) * scale))
                o_ref[_slab_rows(r), :] = jnp.concatenate(outs, axis=1)
                return carry
            return slab

        def plain(r, carry):
            o_ref[_slab_rows(r), :] = act(r)
            return carry

        @pl.when(j == 0)
        def _():
            _slab_loop(tr // HALO, normed(DN_DK ** -0.5))

        @pl.when(j == 1)
        def _():
            _slab_loop(tr // HALO, normed(1.0))

        @pl.when(j >= 2)
        def _():
            _slab_loop(tr // HALO, plain)

    hb = tr // HALO
    return _call(
        body, (proj, proj, conv_w), name=name, grid=(l // tr, nblk),
        in_specs=[pl.BlockSpec((tr, CONV_BLK), lambda i, j: (i, j)),
                  pl.BlockSpec((HALO, CONV_BLK), lambda i, j: (jnp.maximum(i * hb - 1, 0), j)),
                  pl.BlockSpec((CONV_K, CONV_BLK), lambda i, j: (0, j))],
        out_specs=[pl.BlockSpec((tr, CONV_BLK), lambda i, j: (i, j))],
        out_shape=[jax.ShapeDtypeStruct((l, DN_CONV_CH), f32)],
        scratch=[], sem=("parallel", "parallel"), vmem_mb=32, comm=comm)


def _dn_conv_bwd_a(proj, conv_w, dqkv, name, comm=None):
    l = proj.shape[0]
    tr = _tile(l, 256)
    nblk = DN_CONV_CH // CONV_BLK
    heads = CONV_BLK // DN_DK

    def body(x_ref, p_ref, w_ref, d_ref, dc_ref, dw_ref, acc_ref):
        j, i = pl.program_id(0), pl.program_id(1)
        w = w_ref[...]
        acc_ref[...] = jnp.zeros_like(acc_ref)

        def slab_of(l2_scale):
            def slab(r, carry):
                xs = _conv_slab(x_ref, p_ref, r, i, tr)
                c = _conv_of(xs, w)
                a = _silu(c)
                dy = d_ref[_slab_rows(r), :]
                if l2_scale is None:
                    da = dy
                else:
                    parts = []
                    for h in range(heads):
                        sl = slice(h * DN_DK, (h + 1) * DN_DK)
                        ah, dyh = a[:, sl], dy[:, sl]
                        rn = lax.rsqrt(jnp.sum(ah * ah, axis=-1, keepdims=True) + RMS_EPS)
                        yh = ah * rn
                        parts.append((rn * l2_scale) * (dyh - yh * jnp.sum(dyh * yh, axis=-1, keepdims=True)))
                    da = jnp.concatenate(parts, axis=1)
                dc = da * _dsilu(c)
                dc_ref[_slab_rows(r), :] = dc
                for s in range(CONV_K):
                    acc_ref[CONV_K - 1 - s] += dc * xs[s]
                return carry
            return slab

        @pl.when(j == 0)
        def _():
            _slab_loop(tr // HALO, slab_of(DN_DK ** -0.5))

        @pl.when(j == 1)
        def _():
            _slab_loop(tr // HALO, slab_of(1.0))

        @pl.when(j >= 2)
        def _():
            _slab_loop(tr // HALO, slab_of(None))

        ksel = lax.broadcasted_iota(jnp.int32, (CONV_K, 1), 0)
        dw = jnp.zeros((CONV_K, CONV_BLK), f32)
        for k in range(CONV_K):
            dw = dw + jnp.where(ksel == k, jnp.sum(acc_ref[k], axis=0, keepdims=True), 0.0)

        @pl.when(i == 0)
        def _():
            dw_ref[...] = dw

        @pl.when(i > 0)
        def _():
            dw_ref[...] += dw

    hb = tr // HALO
    blk = pl.BlockSpec((tr, CONV_BLK), lambda j, i: (i, j))
    return _call(
        body, (proj, proj, conv_w, dqkv), name=name, grid=(nblk, l // tr),
        in_specs=[blk, pl.BlockSpec((HALO, CONV_BLK), lambda j, i: (jnp.maximum(i * hb - 1, 0), j)),
                  pl.BlockSpec((CONV_K, CONV_BLK), lambda j, i: (0, j)), blk],
        out_specs=[blk, pl.BlockSpec((CONV_K, CONV_BLK), lambda j, i: (0, j))],
        out_shape=[jax.ShapeDtypeStruct((l, DN_CONV_CH), f32), jax.ShapeDtypeStruct((CONV_K, DN_CONV_CH), f32)],
        scratch=[pltpu.VMEM((CONV_K, HALO, CONV_BLK), f32)], sem=("parallel", "arbitrary"), vmem_mb=40, comm=comm)


def _dn_conv_bwd_b(dc, conv_w, dproj, name):
    l = dc.shape[0]
    tr = _tile(l, 256)
    nblk = DN_CONV_CH // CONV_BLK
    nrow = l // tr

    n_slabs = tr // HALO
    pair = 2 * HALO

    def body(d_ref, n_ref, w_ref, _, o_ref):
        i = pl.program_id(0)
        w = w_ref[...]
        nxt_tile = jnp.where(i < nrow - 1, n_ref[...], 0.0)
        lrow = lax.broadcasted_iota(jnp.int32, (HALO, 1), 0)

        def one(r):
            cur = d_ref[_slab_rows(r), :]
            nxt = jnp.where(r < n_slabs - 1, d_ref[_slab_rows(jnp.minimum(r + 1, n_slabs - 1)), :], nxt_tile)
            acc = cur * w[CONV_K - 1:CONV_K, :]
            for s in range(1, CONV_K):
                up = jnp.where(lrow >= HALO - s, pltpu.roll(nxt, HALO - s, 0), pltpu.roll(cur, HALO - s, 0))
                acc = acc + up * w[CONV_K - 1 - s:CONV_K - s, :]
            return jnp.where(i * tr + r * HALO >= PAD, acc, 0.0)

        def two(q, carry):
            rows = pl.ds(pl.multiple_of(q * pair, pair), pair)
            o_ref[rows, :] = _b(jnp.concatenate([one(2 * q), one(2 * q + 1)], axis=0))
            return carry

        lax.fori_loop(0, n_slabs // 2, two, None, unroll=4)

    hb = tr // HALO
    nh = l // HALO
    return pl.pallas_call(
        body, name=name, grid=(nrow, nblk),
        in_specs=[pl.BlockSpec((tr, CONV_BLK), lambda i, j: (i, j)),
                  pl.BlockSpec((HALO, CONV_BLK), lambda i, j: (jnp.minimum((i + 1) * hb, nh - 1), j)),
                  pl.BlockSpec((CONV_K, CONV_BLK), lambda i, j: (0, j)),
                  pl.BlockSpec(memory_space=pl.ANY)],
        out_specs=pl.BlockSpec((tr, CONV_BLK), lambda i, j: (i, j)),
        out_shape=jax.ShapeDtypeStruct(dproj.shape, dproj.dtype), input_output_aliases={3: 0},
        compiler_params=_params(("parallel", "parallel"), 32),
    )(dc, dc, conv_w, dproj)


BA_W = LANES


def _dn_gates(ba_ref, al_ref, dt_ref, n):
    rows = n * CHUNK + lax.broadcasted_iota(jnp.int32, (CHUNK, 1), 0)
    vm = (rows >= PAD).astype(f32)
    bin_ = ba_ref[:, 0:DN_HEADS]
    z = ba_ref[:, DN_HEADS:2 * DN_HEADS] + dt_ref[...]
    sp = jnp.maximum(z, 0.0) + jnp.log1p(jnp.exp(-jnp.abs(z)))
    ea = jnp.exp(al_ref[...])
    beta = _sigmoid(bin_) * vm
    g = -ea * sp * vm
    return vm, bin_, z, ea, beta, g


def _tri():
    ri = lax.broadcasted_iota(jnp.int32, (CHUNK, CHUNK), 0)
    ci = lax.broadcasted_iota(jnp.int32, (CHUNK, CHUNK), 1)
    return ri, ci


def _split(a):
    hi = _b(a)
    return hi, _b(a - hi.astype(f32))


def _mm3(a, b, dot=_nn):
    (ah, al), (bh, bl) = _split(a), _split(b)
    return dot(ah, bh) + (dot(ah, bl) + dot(al, bh))


def _cumsum_rows(tri, g):
    tb = _b(tri)
    g1 = _b(g)
    r1 = g - g1.astype(f32)
    g2 = _b(r1)
    g3 = _b(r1 - g2.astype(f32))
    return _nn(tb, g1) + (_nn(tb, g2) + _nn(tb, g3))


DN_SCAN_CHUNKS = 3


def _scan_chunks(nch):
    return DN_SCAN_CHUNKS if nch % DN_SCAN_CHUNKS == 0 else 1


def _dn_prep(qkv, ba, a_log, dt_bias, name, comm=None):
    l = qkv.shape[0]
    nch = l // CHUNK
    heads = range(DN_HEADS)

    def body(q_ref, k_ref, v_ref, ba_ref, al_ref, dt_ref, t_ref, u_ref, wq_ref, pk_ref, eg_ref, kpt_ref, qwt_ref):
        n = pl.program_id(0)
        _, _, _, _, beta, g = _dn_gates(ba_ref, al_ref, dt_ref, n)
        ri, ci = _tri()
        incl, strict = ri >= ci, ri > ci
        eye = (ri == ci).astype(f32)
        gam = _cumsum_rows(incl.astype(f32), g)
        gam_t = gam.T
        gc = [gam[:, h:h + 1] for h in heads]
        bh = [beta[:, h:h + 1] for h in heads]
        kh = [k_ref[:, h * DN_DK:(h + 1) * DN_DK] for h in heads]
        kb = [_b(k) for k in kh]
        decay = [jnp.exp(jnp.where(incl, gc[h] - gam_t[h:h + 1, :], -jnp.inf)) for h in heads]
        a = [jnp.where(strict, bh[h] * _nt(kb[h], kb[h]) * decay[h], 0.0) for h in heads]
        t = [eye - a[h] for h in heads]
        p = a
        for level in range(int(math.log2(CHUNK)) - 1):
            mm = _mm3 if level < 2 else (lambda x, y: _nn(_b(x), _b(y)))
            p = [mm(p[h], p[h]) for h in heads]
            t = [t[h] + mm(t[h], p[h]) for h in heads]
        eg = [jnp.exp(gc[h]) for h in heads]
        for h in heads:
            t_ref[0, h] = t[h]
            u_ref[:, h * DN_DV:(h + 1) * DN_DV] = _mm3(t[h], v_ref[:, h * DN_DV:(h + 1) * DN_DV] * bh[h])
            w = _mm3(t[h], kh[h] * (bh[h] * eg[h]))
            wq_ref[0, h, 0:CHUNK, :] = _b(w)
            qwt_ref[0, h, DN_DK:2 * DN_DK, :] = _b(w.T)
        for h in heads:
            qh = q_ref[:, h * DN_DK:(h + 1) * DN_DK]
            gl = gc[h][CHUNK - 1:CHUNK, :]
            qe = qh * eg[h]
            ke = kh[h] * jnp.exp(gl - gc[h])
            pmat = _nt(_b(qh), kb[h]) * decay[h]
            wq_ref[0, h, CHUNK:2 * CHUNK, :] = _b(qe)
            qwt_ref[0, h, 0:DN_DK, :] = _b(qe.T)
            pk_ref[0, h, 0:CHUNK, :] = _b(pmat)
            pk_ref[0, h, CHUNK:CHUNK + DN_DK, :] = _b(ke.T)
            kpt_ref[0, h, :, 0:DN_DK] = _b(ke)
            kpt_ref[0, h, :, DN_DK:DN_DK + CHUNK] = _b(pmat.T)
            eg_ref[0, h] = jnp.broadcast_to(jnp.exp(gl), (8, LANES))

    vec = pl.BlockSpec((1, DN_HEADS), lambda n: (0, 0))
    return _call(
        body, (qkv, qkv, qkv, ba, a_log, dt_bias), name=name, grid=(nch,),
        in_specs=[pl.BlockSpec((CHUNK, DN_QK), lambda n: (n, 0)), pl.BlockSpec((CHUNK, DN_QK), lambda n: (n, 1)),
                  pl.BlockSpec((CHUNK, DN_V), lambda n: (n, 1)), pl.BlockSpec((CHUNK, BA_W), lambda n: (n, 0)),
                  vec, vec],
        out_specs=[pl.BlockSpec((1, DN_HEADS, CHUNK, CHUNK), lambda n: (n, 0, 0, 0)),
                   pl.BlockSpec((CHUNK, DN_V), lambda n: (n, 0)),
                   pl.BlockSpec((1, DN_HEADS, 2 * CHUNK, DN_DK), lambda n: (n, 0, 0, 0)),
                   pl.BlockSpec((1, DN_HEADS, CHUNK + DN_DK, CHUNK), lambda n: (n, 0, 0, 0)),
                   pl.BlockSpec((1, DN_HEADS, 8, LANES), lambda n: (n, 0, 0, 0)),
                   pl.BlockSpec((1, DN_HEADS, CHUNK, DN_DK + CHUNK), lambda n: (n, 0, 0, 0)),
                   pl.BlockSpec((1, DN_HEADS, 2 * DN_DK, CHUNK), lambda n: (n, 0, 0, 0))],
        out_shape=[jax.ShapeDtypeStruct((nch, DN_HEADS, CHUNK, CHUNK), f32),
                   jax.ShapeDtypeStruct((l, DN_V), f32),
                   jax.ShapeDtypeStruct((nch, DN_HEADS, 2 * CHUNK, DN_DK), bf16),
                   jax.ShapeDtypeStruct((nch, DN_HEADS, CHUNK + DN_DK, CHUNK), bf16),
                   jax.ShapeDtypeStruct((nch, DN_HEADS, 8, LANES), f32),
                   jax.ShapeDtypeStruct((nch, DN_HEADS, CHUNK, DN_DK + CHUNK), bf16),
                   jax.ShapeDtypeStruct((nch, DN_HEADS, 2 * DN_DK, CHUNK), bf16)],
        sem=("parallel",), vmem_mb=40, comm=comm)


def _dn_scan_fwd(u, wq, pk, egl, name):
    l = u.shape[0]
    nch = l // CHUNK
    cs = _scan_chunks(nch)

    def body(u_ref, wq_ref, pk_ref, eg_ref, o_ref, st_ref, vn_ref, s_ref):
        @pl.when(pl.program_id(0) == 0)
        def _():
            s_ref[...] = jnp.zeros_like(s_ref)

        hs = range(DN_HEADS)
        cols = [slice(h * DN_DV, (h + 1) * DN_DV) for h in hs]
        s = [s_ref[h] for h in hs]
        for c in range(cs):
            rows = slice(c * CHUNK, (c + 1) * CHUNK)
            sb = [_b(s[h]) for h in hs]
            x = [_nn(wq_ref[c, h], sb[h]) for h in hs]
            vnb = [_b(u_ref[rows, cols[h]] - x[h][0:CHUNK]) for h in hs]
            y = [_nn(pk_ref[c, h], vnb[h]) for h in hs]
            for h in hs:
                st_ref[c, h] = sb[h]
                vn_ref[rows, cols[h]] = vnb[h]
                o_ref[rows, cols[h]] = x[h][CHUNK:2 * CHUNK] + y[h][0:CHUNK]
            s = [eg_ref[c, h][0:1, 0:1] * s[h] + y[h][CHUNK:CHUNK + DN_DK] for h in hs]
        for h in hs:
            s_ref[h] = s[h]

    return pl.pallas_call(
        body, name=name, grid=(nch // cs,),
        in_specs=[pl.BlockSpec((cs * CHUNK, DN_V), lambda n: (n, 0)),
                  pl.BlockSpec((cs, DN_HEADS, 2 * CHUNK, DN_DK), lambda n: (n, 0, 0, 0)),
                  pl.BlockSpec((cs, DN_HEADS, CHUNK + DN_DK, CHUNK), lambda n: (n, 0, 0, 0)),
                  pl.BlockSpec((cs, DN_HEADS, 8, LANES), lambda n: (n, 0, 0, 0))],
        out_specs=[pl.BlockSpec((cs * CHUNK, DN_V), lambda n: (n, 0)),
                   pl.BlockSpec((cs, DN_HEADS, DN_DK, DN_DV), lambda n: (n, 0, 0, 0)),
                   pl.BlockSpec((cs * CHUNK, DN_V), lambda n: (n, 0))],
        out_shape=[jax.ShapeDtypeStruct((l, DN_V), f32),
                   jax.ShapeDtypeStruct((nch, DN_HEADS, DN_DK, DN_DV), bf16),
                   jax.ShapeDtypeStruct((l, DN_V), bf16)],
        scratch_shapes=[pltpu.VMEM((DN_HEADS, DN_DK, DN_DV), f32)],
        compiler_params=_params(("arbitrary",), 40),
    )(u, wq, pk, egl)


def _dn_scan_bwd(do, kpt, qwt, egl, name):
    l = do.shape[0]
    nch = l // CHUNK
    cs = _scan_chunks(nch)
    nblk = nch // cs

    def body(do_ref, kpt_ref, qwt_ref, eg_ref, dvn_ref, dsp_ref, ds_ref):
        @pl.when(pl.program_id(0) == 0)
        def _():
            ds_ref[...] = jnp.zeros_like(ds_ref)

        hs = range(DN_HEADS)
        cols = [slice(h * DN_DV, (h + 1) * DN_DV) for h in hs]
        ds = [ds_ref[h] for h in hs]
        for c in reversed(range(cs)):
            rows = slice(c * CHUNK, (c + 1) * CHUNK)
            dspb = [_b(ds[h]) for h in hs]
            dob = [_b(do_ref[rows, cols[h]]) for h in hs]
            dvn = [_nn(kpt_ref[c, h][:, 0:DN_DK], dspb[h]) + _nn(kpt_ref[c, h][:, DN_DK:DN_DK + CHUNK], dob[h])
                   for h in hs]
            for h in hs:
                dsp_ref[c, h] = dspb[h]
                dvn_ref[rows, cols[h]] = dvn[h]
            ds = [eg_ref[c, h][0:1, 0:1] * ds[h] + _nn(qwt_ref[c, h][0:DN_DK], dob[h])
                  - _nn(qwt_ref[c, h][DN_DK:2 * DN_DK], _b(dvn[h])) for h in hs]
        for h in hs:
            ds_ref[h] = ds[h]

    rev = lambda s: nblk - 1 - s
    return pl.pallas_call(
        body, name=name, grid=(nblk,),
        in_specs=[pl.BlockSpec((cs * CHUNK, DN_V), lambda s: (rev(s), 0)),
                  pl.BlockSpec((cs, DN_HEADS, CHUNK, DN_DK + CHUNK), lambda s: (rev(s), 0, 0, 0)),
                  pl.BlockSpec((cs, DN_HEADS, 2 * DN_DK, CHUNK), lambda s: (rev(s), 0, 0, 0)),
                  pl.BlockSpec((cs, DN_HEADS, 8, LANES), lambda s: (rev(s), 0, 0, 0))],
        out_specs=[pl.BlockSpec((cs * CHUNK, DN_V), lambda s: (rev(s), 0)),
                   pl.BlockSpec((cs, DN_HEADS, DN_DK, DN_DV), lambda s: (rev(s), 0, 0, 0))],
        out_shape=[jax.ShapeDtypeStruct((l, DN_V), f32),
                   jax.ShapeDtypeStruct((nch, DN_HEADS, DN_DK, DN_DV), bf16)],
        scratch_shapes=[pltpu.VMEM((DN_HEADS, DN_DK, DN_DV), f32)],
        compiler_params=_params(("arbitrary",), 40),
    )(do, kpt, qwt, egl)


def _dn_post_bwd(qkv, ba, a_log, dt_bias, states, dsp_all, tinv_all, u_all, wq, vn_all, do, dvn_all, name):
    l = qkv.shape[0]
    nch = l // CHUNK

    def body(q_ref, k_ref, v_ref, ba_ref, al_ref, dt_ref, st_ref, dsp_ref, t_ref, u_ref, wq_ref, vn_ref, do_ref,
             dvn_ref, dqkv_ref, dba_ref, dal_ref, ddt_ref):
        step = pl.program_id(0)
        n = step

        vm, bin_, z, ea, beta, g = _dn_gates(ba_ref, al_ref, dt_ref, n)
        ri, ci = _tri()
        incl, strict = ri >= ci, ri > ci
        gam = _cumsum_rows(incl.astype(f32), g)
        gam_t = gam.T
        lane8 = lax.broadcasted_iota(jnp.int32, (1, DN_HEADS), 1)
        sub8 = lax.broadcasted_iota(jnp.int32, (DN_HEADS, 1), 0)
        dbeta = jnp.zeros((CHUNK, DN_HEADS), f32)
        dgam = jnp.zeros((CHUNK, DN_HEADS), f32)
        dgam_neg_t = jnp.zeros((DN_HEADS, CHUNK), f32)
        last = (lax.broadcasted_iota(jnp.int32, (CHUNK, 1), 0) == CHUNK - 1).astype(f32)
        hs = range(DN_HEADS)
        each = lambda fn: [fn(h) for h in hs]
        rsum = lambda t: jnp.sum(t, axis=-1, keepdims=True)
        gc = each(lambda h: gam[:, h:h + 1])
        bh = each(lambda h: beta[:, h:h + 1])
        qh = each(lambda h: q_ref[:, h * DN_DK:(h + 1) * DN_DK])
        kh = each(lambda h: k_ref[:, h * DN_DK:(h + 1) * DN_DK])
        doh = each(lambda h: _b(do_ref[:, h * DN_DV:(h + 1) * DN_DV]))
        sb = each(lambda h: st_ref[0, h])
        dspb = each(lambda h: dsp_ref[0, h])
        vnb = each(lambda h: vn_ref[:, h * DN_DV:(h + 1) * DN_DV])
        dvn = each(lambda h: dvn_ref[:, h * DN_DV:(h + 1) * DN_DV])
        wb = each(lambda h: wq_ref[0, h, 0:CHUNK, :])
        decay = each(lambda h: jnp.exp(jnp.where(incl, gc[h] - gam_t[h:h + 1, :], -jnp.inf)))
        qb, kb = each(lambda h: _b(qh[h])), each(lambda h: _b(kh[h]))
        eg = each(lambda h: jnp.exp(gc[h]))
        gl = each(lambda h: gc[h][CHUNK - 1:CHUNK, :])
        ekd = each(lambda h: jnp.exp(gl[h] - gc[h]))
        dvnb = each(lambda h: _b(dvn[h]))
        kk = each(lambda h: _nt(kb[h], kb[h]))
        p = each(lambda h: _nt(qb[h], kb[h]) * decay[h])
        dpraw = each(lambda h: _nt(doh[h], vnb[h]))
        dqe = each(lambda h: _nt(doh[h], sb[h]))
        dke = each(lambda h: _nt(vnb[h], dspb[h]))
        dw = each(lambda h: -_nt(dvnb[h], sb[h]))
        dru = each(lambda h: _mm3(t_ref[0, h], dvn[h], _tn))
        drw = each(lambda h: _mm3(t_ref[0, h], dw[h], _tn))
        dqk = each(lambda h: _b(dpraw[h] * decay[h]))
        for h in hs:
            dqkv_ref[:, h * DN_DK:(h + 1) * DN_DK] = _nn(dqk[h], kb[h]) + dqe[h] * eg[h]
            dqkv_ref[:, 2 * DN_QK + h * DN_DV:2 * DN_QK + (h + 1) * DN_DV] = bh[h] * dru[h]
        da = each(lambda h: jnp.where(strict, -(_nt(_b(dru[h]), _b(u_ref[:, h * DN_DV:(h + 1) * DN_DV]))
                                                + _nt(_b(drw[h]), wb[h])), 0.0))
        dkk = each(lambda h: _b(da[h] * bh[h] * decay[h]))
        for h in hs:
            dqkv_ref[:, DN_QK + h * DN_DK:DN_QK + (h + 1) * DN_DK] = (
                _tn(dqk[h], qb[h]) + dke[h] * ekd[h] + (bh[h] * eg[h]) * drw[h]
                + _nn(dkk[h], kb[h]) + _tn(dkk[h], kb[h]))
        for h in hs:
            keg = kh[h] * eg[h]
            ke = kh[h] * ekd[h]
            rw = rsum(drw[h] * keg)
            rke = rsum(dke[h] * ke)
            db_h = rsum(dru[h] * v_ref[:, h * DN_DV:(h + 1) * DN_DV]) + rw + rsum(da[h] * kk[h] * decay[h])
            mm = da[h] * (bh[h] * kk[h] * decay[h]) + dpraw[h] * p[h]
            dgl = (jnp.sum(rke, axis=0, keepdims=True)
                   + jnp.exp(gl[h]) * jnp.sum(rsum(dspb[h].astype(f32) * sb[h].astype(f32)), axis=0, keepdims=True))
            dg_h = rsum(mm) + rw * bh[h] + rsum(dqe[h] * (qh[h] * eg[h])) - rke + last * dgl
            dbeta = dbeta + jnp.where(lane8 == h, db_h, 0.0)
            dgam = dgam + jnp.where(lane8 == h, dg_h, 0.0)
            dgam_neg_t = dgam_neg_t + jnp.where(sub8 == h, jnp.sum(mm, axis=0, keepdims=True), 0.0)
        dgam = dgam - dgam_neg_t.T
        dg = _cumsum_rows((ri <= ci).astype(f32), dgam)
        sg = _sigmoid(bin_)
        dbin = dbeta * vm * sg * (1.0 - sg)
        dain = dg * (-ea) * vm * _sigmoid(z)
        dba_ref[...] = jnp.zeros_like(dba_ref)
        dba_ref[:, 0:DN_HEADS] = dbin
        dba_ref[:, DN_HEADS:2 * DN_HEADS] = dain
        dal = jnp.sum(dg * g, axis=0, keepdims=True)
        ddt = jnp.sum(dain, axis=0, keepdims=True)

        @pl.when(step == 0)
        def _():
            dal_ref[...] = dal
            ddt_ref[...] = ddt

        @pl.when(step > 0)
        def _():
            dal_ref[...] += dal
            ddt_ref[...] += ddt

    vec = pl.BlockSpec((1, DN_HEADS), lambda s: (0, 0))
    qs = pl.BlockSpec((CHUNK, DN_QK), lambda s: (s, 0))
    ks = pl.BlockSpec((CHUNK, DN_QK), lambda s: (s, 1))
    vs = pl.BlockSpec((CHUNK, DN_V), lambda s: (s, 1))
    v0 = pl.BlockSpec((CHUNK, DN_V), lambda s: (s, 0))
    st = pl.BlockSpec((1, DN_HEADS, DN_DK, DN_DV), lambda s: (s, 0, 0, 0))
    return pl.pallas_call(
        body, name=name, grid=(nch,),
        in_specs=[qs, ks, vs, pl.BlockSpec((CHUNK, BA_W), lambda s: (s, 0)), vec, vec, st, st,
                  pl.BlockSpec((1, DN_HEADS, CHUNK, CHUNK), lambda s: (s, 0, 0, 0)),
                  v0, pl.BlockSpec((1, DN_HEADS, 2 * CHUNK, DN_DK), lambda s: (s, 0, 0, 0)), v0, v0, v0],
        out_specs=[pl.BlockSpec((CHUNK, DN_CONV_CH), lambda s: (s, 0)),
                   pl.BlockSpec((CHUNK, BA_W), lambda s: (s, 0)), vec, vec],
        out_shape=[jax.ShapeDtypeStruct((l, DN_CONV_CH), f32), jax.ShapeDtypeStruct((l, BA_W), f32),
                   jax.ShapeDtypeStruct((1, DN_HEADS), f32), jax.ShapeDtypeStruct((1, DN_HEADS), f32)],
        compiler_params=_params(("arbitrary",), 48),
    )(qkv, qkv, qkv, ba, a_log, dt_bias, states, dsp_all, tinv_all, u_all, wq, vn_all, do, dvn_all)


def _ffn_fwd(h, hn, wg, wu, wd, tb, th, tag, plan, next_norm_w=None):
    fh, d = wd.shape
    a, b, s = plan.call(f"{tag}_gu", functools.partial(_ffn_gu, tm=th // 2, tn=fh // 2), hn, wg, wu, n_out=3)
    out = plan.matmul(f"{tag}_down", s, wd, mode="nn", tm=th // 2, tn=d, tk=fh, res=h, norm_w=next_norm_w)
    return out, (hn, a, b, s)


def _ffn_bwd(dh, dhb, h, nw, wg, wu, wd, saved, tb, th, tag, plan):
    hn, a, b, s = saved
    d = h.shape[1]
    fh = wd.shape[0]
    layer = tag[-1]
    gr = plan.grads
    da, db = _ffn_ds(dhb, wd, a, b, tm=th // 2, tn=fh // 2, name=f"{tag}_b_ds")
    gr["down" + layer] = _matmul(s, dhb, mode="tn", tm=fh // 2, tn=d, tk=th, out_dtype=bf16, name=f"{tag}_b_dwd")
    dh2, dh2b, dnw = plan.call(f"{tag}_b_dhn", functools.partial(_dhn_norm_bwd, tm=th // 2, tk=fh // 2),
                               [(da, wg), (db, wu)], h, nw, dh, n_out=3)
    gr["gate" + layer] = _matmul(hn, da, mode="tn", tm=d, tn=fh // 2, tk=th, out_dtype=bf16, name=f"{tag}_b_dwg")
    gr["up" + layer] = _matmul(hn, db, mode="tn", tm=d, tn=fh // 2, tk=th, out_dtype=bf16, name=f"{tag}_b_dwu")
    return dh2, dh2b, dnw


class _Plan:
    GATHERS = {"ret_proj": ("ret_out", "gate0"), "ret_scan": ("up0", "down0"), "ffn0_gu": ("dn_in_top",),
               "ffn0_down": ("dn_in_bottom",), "dn_proj": ("dn_out",), "dn_conv": ("gate1",),
               "dn_prep": ("up1", "down1")}
    SCATTERS = {"ffn1_b_dhn": ("down1",), "dn_b_conv_a": ("gate1", "up1", "dn_out"), "ffn0_b_dhn": ("dn_in",),
                "ret_b_scan": ("gate0", "up0"), "ret_b_dwin": ("down0", "ret_out"), "ret_b_dhn": ("ret_in",)}

    def __init__(self, shards, wts):
        self.shards, self.wts, self.grads, self.parts = shards, wts, {}, {}

    def _exchange(self, stage):
        if self.shards is None:
            return None
        if stage in self.GATHERS:
            return _Exchange([self.shards[n] for n in self.GATHERS[stage]], True)
        if stage in self.SCATTERS:
            return _Exchange([self._dev_major(n) for n in self.SCATTERS[stage]], False)
        return None

    def _dev_major(self, name):
        g = self.grads
        if name[:-1] in ("gate", "up"):
            return _dev_major_cols(g[name], g[name].shape[1] // N_DEV)
        if name[:-1] == "down":
            dwd = g[name]
            return dwd.reshape(N_DEV, dwd.shape[0] // N_DEV, dwd.shape[1])
        if name in ("ret_out", "dn_out"):
            return g[name].reshape(N_DEV, g[name].shape[0] // N_DEV, g[name].shape[1])
        return _dev_major_cols(g[name], self.shards[name].shape[-1])

    def _landed(self, stage, outs):
        if stage in self.SCATTERS:
            self.parts.update(zip(self.SCATTERS[stage], outs))
            return
        w = self.wts
        cols = lambda t: t.transpose(1, 0, 2).reshape(t.shape[1], N_DEV * t.shape[2])
        rows = lambda t: t.reshape(N_DEV * t.shape[1], t.shape[2])
        for name, t in zip(self.GATHERS[stage], outs):
            if name in ("ret_out", "dn_out") or name.startswith("down"):
                w[name] = rows(t)
            else:
                w[name] = cols(t)
        if "dn_in_top" in w and "dn_in_bottom" in w and "dn_main" not in w:
            full = jnp.concatenate([w["dn_in_top"], w["dn_in_bottom"]], axis=0)
            n_main = DN_CONV_CH + DN_V
            w["dn_main"] = full[:, :n_main]
            w["dn_ba"] = jnp.pad(full[:, n_main:], ((0, 0), (0, BA_W - (full.shape[1] - n_main))))

    def matmul(self, stage, a, b, **kw):
        comm = self._exchange(stage)
        if comm is None:
            return _matmul(a, b, name=stage, **kw)
        out, landed = _matmul(a, b, name=stage, comm=comm, **kw)
        self._landed(stage, landed)
        return out

    def call(self, stage, fn, *args, n_out):
        comm = self._exchange(stage)
        out = fn(*args, stage, comm=comm)
        if comm is not None:
            self._landed(stage, out[n_out:])
        return out[:n_out]


def _local_step(x2, target, wts, shards=None):
    plan = _Plan(shards, wts)
    s_len, d = x2.shape
    l = s_len + CHUNK
    tb = _tile(l, 3072)
    th = tb // 2 if (tb // 2) % 16 == 0 else tb
    half = RET_DK // 2
    inv_freq = (np.float32(ROPE_BASE) ** (-np.arange(half, dtype=np.float32) / np.float32(half))).astype(np.float32)
    ang = (np.arange(l) - PAD).astype(np.float32)[:, None] * inv_freq[None, :]
    cos, sin = jnp.asarray(np.cos(ang), f32), jnp.asarray(np.sin(ang), f32)
    lgs = jnp.log1p(-jnp.exp2(-5.0 - jnp.arange(RET_HEADS, dtype=f32)))
    gcs = jnp.exp(lgs * _ret_block(l))

    h0 = jnp.concatenate([jnp.zeros((PAD, d), f32), wts["meta"], x2], axis=0)
    mixw, ffnw = wts["mix_norm"], wts["ffn_norm"]

    hn0 = _rms_fwd(h0, mixw[0:1], "l0_norm")
    proj0 = plan.matmul("ret_proj", hn0, wts["ret_in"], mode="nn", tm=tb, tn=512, tk=d)
    qk0 = _ret_prep(proj0, cos, sin, "ret_prep")
    o0, st0 = plan.call("ret_scan", _ret_scan_fwd, qk0, proj0, lgs, gcs, n_out=2)
    y0 = _gnorm_fwd(o0, proj0, wts["ret_gn"], RET_HEADS, RET_DV, 2, "ret_gnorm")
    h1, hn1 = _matmul(y0, wts["ret_out"], mode="nn", tm=th // 2, tn=d, tk=RET_V, res=h0, norm_w=ffnw[0:1],
                      name="ret_out")
    (h2, hn2), ffn0 = _ffn_fwd(h1, hn1, wts["gate0"], wts["up0"], wts["down0"], tb, th, "ffn0", plan,
                               next_norm_w=mixw[1:2])

    proj1 = plan.matmul("dn_proj", hn2, wts["dn_main"], mode="nn", tm=tb, tn=512, tk=d)
    ba = _matmul(hn2, wts["dn_ba"], mode="nn", tm=tb, tn=BA_W, tk=d, name="dn_proj_ba")
    (qkv1,) = plan.call("dn_conv", _dn_conv_fwd, proj1, wts["conv_w"], n_out=1)
    tinv1, u1, wq1, pk1, egl1, kpt1, qwt1 = plan.call("dn_prep", _dn_prep, qkv1, ba, wts["a_log"], wts["dt_bias"],
                                                      n_out=7)
    o1, st1, vn1 = _dn_scan_fwd(u1, wq1, pk1, egl1, "dn_scan")
    y1 = _gnorm_fwd(o1, proj1, wts["dn_norm"], DN_HEADS, DN_DV, 2, "dn_gnorm")
    h3, hn3 = _matmul(y1, wts["dn_out"], mode="nn", tm=th // 2, tn=d, tk=DN_V, res=h2, norm_w=ffnw[1:2],
                      name="dn_out")
    h4, ffn1 = _ffn_fwd(h3, hn3, wts["gate1"], wts["up1"], wts["down1"], tb, th, "ffn1", plan)

    dh4, dh4b, dfinal, loss = _final_loss(h4, wts["final_norm"], target, "final_loss")
    gr = plan.grads
    dh3, dh3b, dffn1 = _ffn_bwd(dh4, dh4b, h3, ffnw[1:2], wts["gate1"], wts["up1"], wts["down1"], ffn1,
                                tb, th, "ffn1", plan)

    gr["dn_out"] = _matmul(y1, dh3b, mode="tn", tm=1024, tn=d, tk=tb, out_dtype=bf16, name="dn_b_dwout")
    do1, dproj1, ddn_norm = _dy_gnorm_bwd(dh3b, wts["dn_out"], o1, proj1, wts["dn_norm"], DN_DV, 2, "dn_b_gnorm",
                                          tm=th // 2, tn=1024)
    dvn1, dsp1 = _dn_scan_bwd(do1, kpt1, qwt1, egl1, "dn_b_scan")
    dqkv1, dba, dalog, ddt = _dn_post_bwd(qkv1, ba, wts["a_log"], wts["dt_bias"], st1, dsp1, tinv1, u1, wq1, vn1,
                                          do1, dvn1, "dn_b_post")
    dc1, dconv = plan.call("dn_b_conv_a", _dn_conv_bwd_a, proj1, wts["conv_w"], dqkv1, n_out=2)
    dproj1 = _dn_conv_bwd_b(dc1, wts["conv_w"], dproj1, "dn_b_conv_b")
    dbab = dba.astype(bf16)
    n_main = dproj1.shape[1]
    dhn2_ba = _matmul(dbab, wts["dn_ba"], mode="nt", tm=th, tn=d, tk=BA_W, name="dn_b_dhn_ba")
    dh2, dh2b, dmix1 = plan.call("dn_b_dhn", functools.partial(_dhn_norm_bwd, tm=th // 2, tk=n_main // 4,
                                                              init=dhn2_ba),
                                 [(dproj1, wts["dn_main"])], h2, mixw[1:2], dh3, n_out=3)
    dw_main = _matmul(hn2, dproj1, mode="tn", tm=d, tn=512, tk=tb, out_dtype=bf16, name="dn_b_dwin")
    dw_ba = _matmul(hn2, dbab, mode="tn", tm=d, tn=BA_W, tk=tb, out_dtype=bf16, name="dn_b_dwin_ba")
    gr["dn_in"] = jnp.concatenate([dw_main, dw_ba], axis=1)

    dh1, dh1b, dffn0 = _ffn_bwd(dh2, dh2b, h1, ffnw[0:1], wts["gate0"], wts["up0"], wts["down0"], ffn0,
                                tb, th, "ffn0", plan)

    gr["ret_out"] = _matmul(y0, dh1b, mode="tn", tm=1024, tn=d, tk=tb, out_dtype=bf16, name="ret_b_dwout")
    do0, dproj0, dret_gn = _dy_gnorm_bwd(dh1b, wts["ret_out"], o0, proj0, wts["ret_gn"], RET_DV, 2, "ret_b_gnorm",
                                         tm=th // 2, tn=1024)
    dq0, dk0, dproj0 = plan.call("ret_b_scan", _ret_scan_bwd, qk0, proj0, st0, do0, dproj0, lgs, gcs, n_out=3)
    dproj0 = _ret_prep_bwd(dq0, dk0, cos, sin, dproj0, "ret_b_prep")
    n_in = dproj0.shape[1]
    gr["ret_in"] = plan.matmul("ret_b_dwin", hn0, dproj0, mode="tn", tm=d, tn=512, tk=tb, out_dtype=bf16)
    dh0, _, dmix0 = plan.call("ret_b_dhn", functools.partial(_dhn_norm_bwd, tm=th // 2, tk=n_in // 4),
                              [(dproj0, wts["ret_in"])], h0, mixw[0:1], dh1, n_out=3)

    gr.update(meta=dh0[PAD:CHUNK], mix_norm=jnp.concatenate([dmix0, dmix1], axis=0),
              ffn_norm=jnp.concatenate([dffn0, dffn1], axis=0), ret_gn=dret_gn, conv_w=dconv, a_log=dalog,
              dt_bias=ddt, dn_norm=ddn_norm, final_norm=dfinal)
    return loss, dh0[CHUNK:], gr, plan


def _adamw_reduce(parts, w, m, v, name):
    _, r, c = parts.shape
    c_pad = -(-c // LANES) * LANES
    tr = _div_tile(r, max(8, (3 * MIB // 16) // c_pad // 8 * 8), 16)

    def body(p_ref, w_ref, m_ref, v_ref, g_ref, d_ref, nm_ref, nv_ref):
        g = p_ref[0].astype(f32)
        for s in range(1, N_DEV):
            g = g + p_ref[s].astype(f32)
        mm = ADAM_B1 * m_ref[...] + (1.0 - ADAM_B1) * g
        vv = ADAM_B2 * v_ref[...] + (1.0 - ADAM_B2) * (g * g)
        m_hat = mm / (1.0 - ADAM_B1 ** ADAM_STEP)
        v_hat = vv / (1.0 - ADAM_B2 ** ADAM_STEP)
        g_ref[...] = g
        d_ref[...] = -ADAM_LR * (m_hat / (jnp.sqrt(v_hat) + ADAM_EPS) + ADAM_WD * w_ref[...])
        nm_ref[...] = mm
        nv_ref[...] = vv

    blk = pl.BlockSpec((tr, c), lambda i: (i, 0))
    return pl.pallas_call(
        body, name=name, grid=(r // tr,),
        in_specs=[pl.BlockSpec((N_DEV, tr, c), lambda i: (0, i, 0)), blk, blk, blk], out_specs=[blk] * 4,
        out_shape=[jax.ShapeDtypeStruct((r, c), f32)] * 4,
        compiler_params=_params(("parallel",), 48),
    )(parts, w, m, v)


def _dev_major_cols(g, width):
    r = g.shape[0]
    return g[:, :N_DEV * width].reshape(r, N_DEV, width).transpose(1, 0, 2)


def kernel(x, meta_tokens, mix_norm_w, ffn_norm_w, ret_w_in, ret_gn_w, ret_w_out, dn_w_in, dn_conv_w, dn_a_log, dn_dt_bias, dn_norm_w, dn_w_out, ffn_w_gate, ffn_w_up, ffn_w_down, final_norm_w, loss_target, m_meta_tokens, m_mix_norm_w, m_ffn_norm_w, m_ret_w_in, m_ret_gn_w, m_ret_w_out, m_dn_w_in, m_dn_conv_w, m_dn_a_log, m_dn_dt_bias, m_dn_norm_w, m_dn_w_out, m_ffn_w_gate, m_ffn_w_up, m_ffn_w_down, m_final_norm_w, v_meta_tokens, v_mix_norm_w, v_ffn_norm_w, v_ret_w_in, v_ret_gn_w, v_ret_w_out, v_dn_w_in, v_dn_conv_w, v_dn_a_log, v_dn_dt_bias, v_dn_norm_w, v_dn_w_out, v_ffn_w_gate, v_ffn_w_up, v_ffn_w_down, v_final_norm_w):
    d = x.shape[-1]
    me = 4 * lax.axis_index("x") + 2 * lax.axis_index("y") + lax.axis_index("c")

    shards = dict(ret_in=ret_w_in[0].astype(bf16), ret_out=ret_w_out[0].astype(bf16),
                  dn_in=dn_w_in[0].astype(bf16), dn_out=dn_w_out[0].astype(bf16))
    shards["dn_in_top"], shards["dn_in_bottom"] = shards["dn_in"][:d // 2], shards["dn_in"][d // 2:]
    for layer in (0, 1):
        shards[f"gate{layer}"] = ffn_w_gate[layer].astype(bf16)
        shards[f"up{layer}"] = ffn_w_up[layer].astype(bf16)
        shards[f"down{layer}"] = ffn_w_down[layer].astype(bf16)
    g_ret_in, g_meta, g_conv, g_dnn = _exchange([shards["ret_in"], meta_tokens, dn_conv_w[0], dn_norm_w], True,
                                                "gather_first")
    cols = lambda g: g.transpose(1, 0, 2).reshape(g.shape[1], N_DEV * g.shape[2])
    wts = dict(meta=cols(g_meta), mix_norm=mix_norm_w, ffn_norm=ffn_norm_w, ret_in=cols(g_ret_in), ret_gn=ret_gn_w,
               conv_w=cols(g_conv), a_log=dn_a_log, dt_bias=dn_dt_bias, dn_norm=cols(g_dnn),
               final_norm=final_norm_w.reshape(1, d))

    loss_part, grad_x, gr, plan = _local_step(x[0], loss_target[0], wts, shards)
    loss = lax.psum(loss_part[0, 0], AXES)

    pp = plan.parts
    both = lambda name: jnp.concatenate([pp[name + "0"], pp[name + "1"]], axis=1)
    big_parts = [pp["ret_in"], pp["ret_out"], pp["dn_in"], pp["dn_out"], both("gate"), both("up"), both("down")]
    big_names = ["ret_w_in", "ret_w_out", "dn_w_in", "dn_w_out", "ffn_w_gate", "ffn_w_up", "ffn_w_down"]
    big_w = [ret_w_in, ret_w_out, dn_w_in, dn_w_out, ffn_w_gate, ffn_w_up, ffn_w_down]
    big_m = [m_ret_w_in, m_ret_w_out, m_dn_w_in, m_dn_w_out, m_ffn_w_gate, m_ffn_w_up, m_ffn_w_down]
    big_v = [v_ret_w_in, v_ret_w_out, v_dn_w_in, v_dn_w_out, v_ffn_w_gate, v_ffn_w_up, v_ffn_w_down]
    res = {}
    for nm, parts, w_, m_, v_ in zip(big_names, big_parts, big_w, big_m, big_v):
        r2, c2 = parts.shape[1], parts.shape[2]
        outs = _adamw_reduce(parts, w_.reshape(r2, c2), m_.reshape(r2, c2), v_.reshape(r2, c2), f"adamw_{nm}")
        res[nm] = [o.reshape(w_.shape) for o in outs]

    small_names = ["meta_tokens", "mix_norm_w", "ffn_norm_w", "ret_gn_w", "dn_conv_w", "dn_a_log", "dn_dt_bias",
                   "dn_norm_w", "final_norm_w"]
    small_g = [gr["meta"], gr["mix_norm"], gr["ffn_norm"], gr["ret_gn"], gr["conv_w"], gr["a_log"], gr["dt_bias"],
               gr["dn_norm"], gr["final_norm"]]
    small_w = [meta_tokens, mix_norm_w, ffn_norm_w, ret_gn_w, dn_conv_w, dn_a_log, dn_dt_bias, dn_norm_w, final_norm_w]
    small_m = [m_meta_tokens, m_mix_norm_w, m_ffn_norm_w, m_ret_gn_w, m_dn_conv_w, m_dn_a_log, m_dn_dt_bias,
               m_dn_norm_w, m_final_norm_w]
    small_v = [v_meta_tokens, v_mix_norm_w, v_ffn_norm_w, v_ret_gn_w, v_dn_conv_w, v_dn_a_log, v_dn_dt_bias,
               v_dn_norm_w, v_final_norm_w]
    sharded = {"meta_tokens", "dn_conv_w", "dn_norm_w"}
    flat = jnp.concatenate([g.reshape(-1) for g in small_g])
    row = 8 * LANES
    n_flat = flat.shape[0]
    flat = jnp.pad(flat, (0, -n_flat % row)).reshape(-1, row)
    (gathered,) = _exchange([flat], True, "gather_small_grads")
    gathered = gathered.reshape(N_DEV, -1)
    pieces, off = [], 0
    for nm, g, w_ in zip(small_names, small_g, small_w):
        full = gathered[:, off:off + g.size].reshape((N_DEV,) + g.shape)
        off += g.size
        if nm in sharded:
            wloc = w_.shape[-1]
            full = lax.dynamic_slice_in_dim(full, me * wloc, wloc, axis=full.ndim - 1)
        pieces.append(full.reshape(N_DEV, -1))
    sizes = [p.shape[1] for p in pieces]
    n_loc = sum(sizes)
    pad_loc = -n_loc % row

    def pack(vs, lead):
        cat = jnp.concatenate([a.reshape(lead + (-1,)) for a in vs], axis=-1)
        cat = jnp.pad(cat, [(0, 0)] * len(lead) + [(0, pad_loc)])
        return cat.reshape(lead + (-1, row))

    outs = _adamw_reduce(pack(pieces, (N_DEV,)), pack(small_w, ()), pack(small_m, ()), pack(small_v, ()), "adamw_small")
    off = 0
    for nm, sz, w_ in zip(small_names, sizes, small_w):
        res[nm] = [o.reshape(-1)[off:off + sz].reshape(w_.shape) for o in outs]
        off += sz

    order = ["meta_tokens", "mix_norm_w", "ffn_norm_w", "ret_w_in", "ret_gn_w", "ret_w_out", "dn_w_in", "dn_conv_w",
             "dn_a_log", "dn_dt_bias", "dn_norm_w", "dn_w_out", "ffn_w_gate", "ffn_w_up", "ffn_w_down", "final_norm_w"]
    grad_x = grad_x.reshape(x.shape)
    return (loss, grad_x, *[res[nm][0] for nm in order], *[res[nm][1] for nm in order],
            *[res[nm][2] for nm in order], *[res[nm][3] for nm in order])
```

```python
import functools
import math

import jax
import jax.numpy as jnp
import numpy as np
from jax import lax
from jax.experimental import pallas as pl
from jax.experimental.pallas import tpu as pltpu

f32 = jnp.float32
bf16 = jnp.bfloat16
HI = lax.Precision.HIGHEST

N_META = 16
CHUNK = 64
PAD = CHUNK - N_META
RMS_EPS = 1e-6
RET_HEADS, RET_DK, RET_DV = 4, 256, 512
RET_QK, RET_V = RET_HEADS * RET_DK, RET_HEADS * RET_DV
DN_HEADS, DN_DK, DN_DV = 8, 128, 256
DN_QK, DN_V = DN_HEADS * DN_DK, DN_HEADS * DN_DV
DN_CONV_CH = 2 * DN_QK + DN_V
CONV_K = 4
ROPE_BASE = 10000.0
ADAM_LR, ADAM_B1, ADAM_B2, ADAM_EPS, ADAM_WD, ADAM_STEP = 0.001, 0.9, 0.999, 1e-08, 0.01, 10
N_DEV = 8
AXES = ("x", "y", "c")
LANES = 128
MIB = 1024 * 1024


def _tile(n_rows, cap):
    nch = n_rows // CHUNK
    best = 1
    for d in range(1, nch + 1):
        if nch % d == 0 and d * CHUNK <= cap:
            best = d
    return best * CHUNK


def _div_tile(n, cap, align):
    best = None
    for d in range(align, min(n, cap) + 1, align):
        if n % d == 0:
            best = d
    return best if best is not None else n


def _params(sem, vmem_mb):
    return pltpu.CompilerParams(dimension_semantics=sem, vmem_limit_bytes=int(vmem_mb * MIB))


def _nn(a, b, precision=None):
    return jnp.dot(a, b, preferred_element_type=f32, precision=precision)


def _nt(a, b, precision=None):
    return lax.dot_general(a, b, (((1,), (1,)), ((), ())), preferred_element_type=f32, precision=precision)


def _tn(a, b, precision=None):
    return lax.dot_general(a, b, (((0,), (0,)), ((), ())), preferred_element_type=f32, precision=precision)


def _b(x):
    return x.astype(bf16)


def _sigmoid(x):
    return 0.5 * jnp.tanh(0.5 * x) + 0.5


def _silu(x):
    return x * _sigmoid(x)


def _dsilu(x):
    s = _sigmoid(x)
    return s * (1.0 + x * (1.0 - s))


def _peer(k):
    x, y, c = lax.axis_index("x"), lax.axis_index("y"), lax.axis_index("c")
    px = 1 - x if k & 4 else x
    py = 1 - y if k & 2 else y
    pc = 1 - c if k & 1 else c
    return (px, py, pc), 4 * px + 2 * py + pc


class _Exchange:
    def __init__(self, arrs, gather):
        self.arrs, self.gather, self.n = list(arrs), gather, len(arrs)
        self.out_shapes = [jax.ShapeDtypeStruct(((N_DEV,) + a.shape) if gather else a.shape, a.dtype) for a in arrs]
        self.specs = [pl.BlockSpec(memory_space=pltpu.HBM)] * self.n
        self.scratch = [pltpu.SemaphoreType.DMA((self.n, N_DEV - 1)), pltpu.SemaphoreType.DMA((self.n, N_DEV - 1)),
                        pltpu.SemaphoreType.DMA((self.n,))]

    def _copies(self, ins, outs, sems):
        send_sems, recv_sems, local_sems = sems
        me = 4 * lax.axis_index("x") + 2 * lax.axis_index("y") + lax.axis_index("c")
        src = (lambda a, dest: ins[a]) if self.gather else (lambda a, dest: ins[a].at[dest])
        local = [pltpu.make_async_copy(src(a, me), outs[a].at[me], local_sems.at[a]) for a in range(self.n)]
        sends, lands = [], []
        for k in range(1, N_DEV):
            peer, pidx = _peer(k)
            for a in range(self.n):
                for dst, lst in ((outs[a].at[me], sends), (outs[a].at[pidx], lands)):
                    lst.append(pltpu.make_async_remote_copy(
                        src_ref=src(a, pidx), dst_ref=dst, send_sem=send_sems.at[a, k - 1],
                        recv_sem=recv_sems.at[a, k - 1], device_id=peer, device_id_type=pl.DeviceIdType.MESH))
        return local, sends, lands

    def start(self, ins, outs, sems):
        local, sends, _ = self._copies(ins, outs, sems)
        for cp in local + sends:
            cp.start()

    def wait(self, ins, outs, sems):
        local, sends, lands = self._copies(ins, outs, sems)
        for cp in lands:
            cp.wait_recv()
        for cp in sends:
            cp.wait_send()
        for cp in local:
            cp.wait()


def _call(body, args, *, name, grid, in_specs, out_specs, out_shape, scratch=(), sem, vmem_mb, comm=None,
          aliases=None):
    aliases = aliases or {}
    if comm is None:
        out = pl.pallas_call(body, name=name, grid=grid, in_specs=list(in_specs), out_specs=list(out_specs),
                             out_shape=list(out_shape), scratch_shapes=list(scratch), input_output_aliases=aliases,
                             compiler_params=_params(sem, vmem_mb))(*args)
        return list(out)
    n_in, n_out, n_scr, nc = len(args), len(out_shape), len(scratch), comm.n

    def carried(*refs):
        ins, cin = refs[:n_in], refs[n_in:n_in + nc]
        o0 = n_in + nc
        outs, cout = refs[o0:o0 + n_out], refs[o0 + n_out:o0 + n_out + nc]
        s0 = o0 + n_out + nc
        scr, sems = refs[s0:s0 + n_scr], refs[s0 + n_scr:]
        first = functools.reduce(jnp.logical_and, [pl.program_id(i) == 0 for i in range(len(grid))])
        last = functools.reduce(jnp.logical_and, [pl.program_id(i) == grid[i] - 1 for i in range(len(grid))])

        @pl.when(first)
        def _():
            comm.start(cin, cout, sems)

        body(*ins, *outs, *scr)

        @pl.when(last)
        def _():
            comm.wait(cin, cout, sems)

    out = pl.pallas_call(
        carried, name=name, grid=grid, in_specs=list(in_specs) + comm.specs, out_specs=list(out_specs) + comm.specs,
        out_shape=list(out_shape) + comm.out_shapes, scratch_shapes=list(scratch) + comm.scratch,
        input_output_aliases=aliases,
        compiler_params=_params(("arbitrary",) * len(grid), vmem_mb))(*args, *comm.arrs)
    return list(out)


def _exchange(arrs, gather, name):
    comm = _Exchange(arrs, gather)

    def body(*refs):
        ins, outs, sems = refs[:comm.n], refs[comm.n:2 * comm.n], refs[2 * comm.n:]
        comm.start(ins, outs, sems)
        comm.wait(ins, outs, sems)

    return pl.pallas_call(body, name=name, in_specs=comm.specs, out_specs=comm.specs, out_shape=comm.out_shapes,
                          scratch_shapes=comm.scratch)(*comm.arrs)


def _matmul(a, b, *, mode, tm, tn, tk, name, out_dtype=f32, res=None, vmem_mb=48, comm=None, pair2=None,
            norm_w=None):
    def dims(x, y):
        if mode == "nn":
            (m_, k_), (k2_, n_) = x.shape, y.shape
        elif mode == "nt":
            (m_, k_), (n_, k2_) = x.shape, y.shape
        else:
            (k_, m_), (k2_, n_) = x.shape, y.shape
        assert k_ == k2_ and m_ % tm == 0 and n_ % tn == 0 and k_ % tk == 0, (name, x.shape, y.shape, tm, tn, tk)
        return m_, n_, k_ // tk

    m, n, nk1 = dims(a, b)
    nk2 = dims(*pair2)[2] if pair2 is not None else 0
    nk = nk1 + nk2
    dot = {"nn": _nn, "nt": _nt, "tn": _tn}[mode]

    def specs(k_of):
        a_spec = {"nn": pl.BlockSpec((tm, tk), lambda i, j, kk: (i, k_of(kk))),
                  "nt": pl.BlockSpec((tm, tk), lambda i, j, kk: (i, k_of(kk))),
                  "tn": pl.BlockSpec((tk, tm), lambda i, j, kk: (k_of(kk), i))}[mode]
        b_spec = {"nn": pl.BlockSpec((tk, tn), lambda i, j, kk: (k_of(kk), j)),
                  "nt": pl.BlockSpec((tn, tk), lambda i, j, kk: (j, k_of(kk))),
                  "tn": pl.BlockSpec((tk, tn), lambda i, j, kk: (k_of(kk), j))}[mode]
        return [a_spec, b_spec]

    o_spec = pl.BlockSpec((tm, tn), lambda i, j, kk: (i, j))
    has_res = res is not None
    has_norm = norm_w is not None
    assert not has_norm or tn == n
    n_mm = 2 + (2 if pair2 is not None else 0)
    n_ops = n_mm + has_res + has_norm

    def body(*refs):
        a_ref, b_ref = refs[:2]
        a2_ref, b2_ref = refs[2:4] if pair2 is not None else (None, None)
        r_ref = refs[n_mm] if has_res else None
        nw_ref = refs[n_mm + has_res] if has_norm else None
        o_ref = refs[n_ops]
        hn_ref = refs[n_ops + 1] if has_norm else None
        rest = refs[n_ops + 1 + has_norm:]

        def finish(tot):
            if has_res:
                tot = tot + r_ref[...]
            o_ref[...] = tot.astype(out_dtype)
            if has_norm:
                r = lax.rsqrt(jnp.mean(tot * tot, axis=-1, keepdims=True) + RMS_EPS)
                hn_ref[...] = _b(tot * r * nw_ref[...])

        if nk == 1:
            finish(dot(_b(a_ref[...]), _b(b_ref[...])))
            return
        acc_ref = rest[0]
        kk = pl.program_id(2)

        @pl.when(kk == 0)
        def _():
            acc_ref[...] = dot(_b(a_ref[...]), _b(b_ref[...]))

        @pl.when(jnp.logical_and(kk > 0, kk < nk1))
        def _():
            acc_ref[...] += dot(_b(a_ref[...]), _b(b_ref[...]))

        if pair2 is not None:
            @pl.when(kk >= nk1)
            def _():
                acc_ref[...] += dot(_b(a2_ref[...]), _b(b2_ref[...]))

        @pl.when(kk == nk - 1)
        def _():
            finish(acc_ref[...])

    in_specs = specs(lambda kk: jnp.minimum(kk, nk1 - 1))
    args = (a, b)
    if pair2 is not None:
        in_specs += specs(lambda kk: jnp.maximum(kk - nk1, 0))
        args += tuple(pair2)
    if has_res:
        in_specs.append(o_spec)
        args += (res,)
    out_specs, out_shape = [o_spec], [jax.ShapeDtypeStruct((m, n), out_dtype)]
    if has_norm:
        in_specs.append(pl.BlockSpec((1, tn), lambda i, j, kk: (0, j)))
        args += (norm_w,)
        out_specs.append(o_spec)
        out_shape.append(jax.ShapeDtypeStruct((m, n), bf16))
    out = _call(body, args, name=name, grid=(m // tm, n // tn, nk), in_specs=in_specs, out_specs=out_specs,
                out_shape=out_shape, scratch=[pltpu.VMEM((tm, tn), f32)] if nk > 1 else [],
                sem=("parallel", "parallel", "arbitrary"), vmem_mb=vmem_mb, comm=comm)
    n_own = len(out_shape)
    own = out[0] if n_own == 1 else tuple(out[:n_own])
    return own if comm is None else (own, out[n_own:])


def _rms_fwd(h, w, name):
    l, d = h.shape
    tr = _tile(l, 512)

    def body(h_ref, w_ref, o_ref):
        x = h_ref[...]
        r = lax.rsqrt(jnp.mean(x * x, axis=-1, keepdims=True) + RMS_EPS)
        o_ref[...] = _b(x * r * w_ref[...])

    return pl.pallas_call(
        body, name=name, grid=(l // tr,),
        in_specs=[pl.BlockSpec((tr, d), lambda i: (i, 0)), pl.BlockSpec((1, d), lambda i: (0, 0))],
        out_specs=pl.BlockSpec((tr, d), lambda i: (i, 0)),
        out_shape=jax.ShapeDtypeStruct((l, d), bf16),
        compiler_params=_params(("parallel",), 32),
    )(h, w)


def _rms_bwd(h, w, dhn, dres, name):
    l, d = h.shape
    tr = _tile(l, 512)

    def body(h_ref, w_ref, g_ref, r_ref, dh_ref, dhb_ref, dw_ref):
        x = h_ref[...]
        r = lax.rsqrt(jnp.mean(x * x, axis=-1, keepdims=True) + RMS_EPS)
        xh = x * r
        g = g_ref[...]
        dxh = g * w_ref[...]
        dx = r * (dxh - xh * jnp.mean(dxh * xh, axis=-1, keepdims=True))
        dh = r_ref[...] + dx
        dh_ref[...] = dh
        dhb_ref[...] = _b(dh)
        dw = jnp.sum(g * xh, axis=0, keepdims=True)

        @pl.when(pl.program_id(0) == 0)
        def _():
            dw_ref[...] = dw

        @pl.when(pl.program_id(0) > 0)
        def _():
            dw_ref[...] += dw

    row = pl.BlockSpec((tr, d), lambda i: (i, 0))
    vec = pl.BlockSpec((1, d), lambda i: (0, 0))
    return pl.pallas_call(
        body, name=name, grid=(l // tr,), in_specs=[row, vec, row, row], out_specs=[row, row, vec],
        out_shape=[jax.ShapeDtypeStruct((l, d), f32), jax.ShapeDtypeStruct((l, d), bf16),
                   jax.ShapeDtypeStruct((1, d), f32)],
        compiler_params=_params(("arbitrary",), 40),
    )(h, w, dhn, dres)


def _dhn_norm_bwd(pairs, h, nw, dres, name, *, tm, tk, init=None, comm=None):
    l, d = h.shape
    nks = [a.shape[1] // tk for a, _ in pairs]
    starts = [sum(nks[:p]) for p in range(len(pairs))]
    nk = sum(nks)
    n_ops = 2 * len(pairs)
    has_init = init is not None

    def body(*refs):
        ops = refs[:n_ops]
        init_ref = refs[n_ops] if has_init else None
        h_ref, w_ref, r_ref, dh_ref, dhb_ref, dw_ref, acc_ref = refs[n_ops + has_init:]
        i, kk = pl.program_id(0), pl.program_id(1)

        @pl.when(kk == 0)
        def _():
            part = _nt(ops[0][...], ops[1][...])
            acc_ref[...] = part + init_ref[...] if has_init else part

        for p in range(len(pairs)):
            lo = max(starts[p], 1)

            @pl.when(jnp.logical_and(kk >= lo, kk < starts[p] + nks[p]))
            def _(a_ref=ops[2 * p], b_ref=ops[2 * p + 1]):
                acc_ref[...] += _nt(a_ref[...], b_ref[...])

        @pl.when(kk == nk - 1)
        def _():
            g = acc_ref[...]
            x = h_ref[...]
            r = lax.rsqrt(jnp.mean(x * x, axis=-1, keepdims=True) + RMS_EPS)
            xh = x * r
            dxh = g * w_ref[...]
            dh = r_ref[...] + r * (dxh - xh * jnp.mean(dxh * xh, axis=-1, keepdims=True))
            dh_ref[...] = dh
            dhb_ref[...] = _b(dh)
            dw = jnp.sum(g * xh, axis=0, keepdims=True)

            @pl.when(i == 0)
            def _():
                dw_ref[...] = dw

            @pl.when(i > 0)
            def _():
                dw_ref[...] += dw

    def k_of(p):
        return lambda kk: jnp.clip(kk - starts[p], 0, nks[p] - 1)

    in_specs, args = [], []
    for p, (a, b) in enumerate(pairs):
        in_specs += [pl.BlockSpec((tm, tk), functools.partial(lambda i, kk, f: (i, f(kk)), f=k_of(p))),
                     pl.BlockSpec((d, tk), functools.partial(lambda i, kk, f: (0, f(kk)), f=k_of(p)))]
        args += [a, b]
    row = pl.BlockSpec((tm, d), lambda i, kk: (i, 0))
    vec = pl.BlockSpec((1, d), lambda i, kk: (0, 0))
    if has_init:
        in_specs.append(row)
        args.append(init)
    return _call(body, tuple(args) + (h, nw, dres), name=name, grid=(l // tm, nk), in_specs=in_specs + [row, vec, row],
                 out_specs=[row, row, vec],
                 out_shape=[jax.ShapeDtypeStruct((l, d), f32), jax.ShapeDtypeStruct((l, d), bf16),
                            jax.ShapeDtypeStruct((1, d), f32)],
                 scratch=[pltpu.VMEM((tm, d), f32)], sem=("arbitrary", "arbitrary"), vmem_mb=48, comm=comm)


def _final_loss(h, w, target, name):
    l, d = h.shape
    nch = l // CHUNK
    cpt = _tile(l, 256) // CHUNK
    nt = nch // cpt

    def body(h_ref, w_ref, *rest):
        t_refs, (dh_ref, dhb_ref, dw_ref, loss_ref) = rest[:cpt], rest[cpt:]
        i = pl.program_id(0)
        wv = w_ref[...]
        dw = jnp.zeros((1, d), f32)
        part = jnp.zeros((1, 1), f32)
        for c in range(cpt):
            rows = slice(c * CHUNK, (c + 1) * CHUNK)
            live = (i * cpt + c > 0).astype(f32)
            x = h_ref[rows, :]
            r = lax.rsqrt(jnp.mean(x * x, axis=-1, keepdims=True) + RMS_EPS)
            xh = x * r
            err = (xh * wv - t_refs[c][...]) * live
            dy = err * (1.0 / d)
            dxh = dy * wv
            dx = r * (dxh - xh * jnp.mean(dxh * xh, axis=-1, keepdims=True))
            dh_ref[rows, :] = dx
            dhb_ref[rows, :] = _b(dx)
            dw = dw + jnp.sum(dy * xh, axis=0, keepdims=True)
            part = part + 0.5 * jnp.sum(jnp.sum(err * err, axis=-1, keepdims=True) * (1.0 / d), axis=0, keepdims=True)
        part = jnp.broadcast_to(part, (1, LANES))

        @pl.when(i == 0)
        def _():
            dw_ref[...] = dw
            loss_ref[...] = part

        @pl.when(i > 0)
        def _():
            dw_ref[...] += dw
            loss_ref[...] += part

    row = pl.BlockSpec((cpt * CHUNK, d), lambda i: (i, 0))
    vec = pl.BlockSpec((1, d), lambda i: (0, 0))
    t_specs = [pl.BlockSpec((CHUNK, d), functools.partial(lambda i, c: (jnp.maximum(i * cpt + c - 1, 0), 0), c=c))
               for c in range(cpt)]
    return pl.pallas_call(
        body, name=name, grid=(nt,),
        in_specs=[row, vec] + t_specs,
        out_specs=[row, row, vec, pl.BlockSpec((1, LANES), lambda i: (0, 0))],
        out_shape=[jax.ShapeDtypeStruct((l, d), f32), jax.ShapeDtypeStruct((l, d), bf16),
                   jax.ShapeDtypeStruct((1, d), f32), jax.ShapeDtypeStruct((1, LANES), f32)],
        compiler_params=_params(("arbitrary",), 32),
    )(h, w, *([target] * cpt))


def _ffn_gu(hn, wg, wu, name, *, tm, tn, comm=None):
    l, d = hn.shape
    fh = wg.shape[1]

    def body(h_ref, g_ref, u_ref, a_ref, b_ref, s_ref):
        hb = h_ref[...]
        a = _nn(hb, g_ref[...])
        bb = _nn(hb, u_ref[...])
        a_ref[...] = _b(a)
        b_ref[...] = _b(bb)
        s_ref[...] = _b(_silu(a) * bb)

    wspec = pl.BlockSpec((d, tn), lambda i, j: (0, j))
    ospec = pl.BlockSpec((tm, tn), lambda i, j: (i, j))
    return _call(body, (hn, wg, wu), name=name, grid=(l // tm, fh // tn),
                 in_specs=[pl.BlockSpec((tm, d), lambda i, j: (i, 0)), wspec, wspec], out_specs=[ospec] * 3,
                 out_shape=[jax.ShapeDtypeStruct((l, fh), bf16)] * 3, sem=("parallel", "parallel"), vmem_mb=48,
                 comm=comm)


def _ffn_ds(dhb, wd, a, b, *, tm, tn, name):
    l, d = dhb.shape
    fh = wd.shape[0]

    def body(g_ref, w_ref, a_ref, b_ref, da_ref, db_ref):
        ds = _nt(g_ref[...], w_ref[...])
        a = a_ref[...].astype(f32)
        da_ref[...] = _b(ds * b_ref[...].astype(f32) * _dsilu(a))
        db_ref[...] = _b(ds * _silu(a))

    ospec = pl.BlockSpec((tm, tn), lambda i, j: (i, j))
    return pl.pallas_call(
        body, name=name, grid=(l // tm, fh // tn),
        in_specs=[pl.BlockSpec((tm, d), lambda i, j: (i, 0)), pl.BlockSpec((tn, d), lambda i, j: (j, 0)), ospec, ospec],
        out_specs=[ospec, ospec], out_shape=[jax.ShapeDtypeStruct((l, fh), bf16)] * 2,
        compiler_params=_params(("parallel", "parallel"), 48),
    )(dhb, wd, a, b)


def _gnorm_fwd(o, proj, nw, heads, dv, gate_blk, name):
    l, hv = o.shape
    tr = _tile(l, 256)

    def body(o_ref, g_ref, w_ref, y_ref):
        wv = w_ref[...]
        for h in range(heads):
            sl = slice(h * dv, (h + 1) * dv)
            oh = o_ref[:, sl]
            r = lax.rsqrt(jnp.mean(oh * oh, axis=-1, keepdims=True) + RMS_EPS)
            y_ref[:, sl] = _b(oh * r * wv * _silu(g_ref[:, sl]))

    return pl.pallas_call(
        body, name=name, grid=(l // tr,),
        in_specs=[pl.BlockSpec((tr, hv), lambda i: (i, 0)), pl.BlockSpec((tr, hv), lambda i: (i, gate_blk)),
                  pl.BlockSpec((1, dv), lambda i: (0, 0))],
        out_specs=pl.BlockSpec((tr, hv), lambda i: (i, 0)),
        out_shape=jax.ShapeDtypeStruct((l, hv), bf16),
        compiler_params=_params(("parallel",), 32),
    )(o, proj, nw)


def _dy_gnorm_bwd(dhb, w_out, o, proj, nw, dv, gate_blk, name, *, tm, tn):
    l, hv = o.shape
    d = dhb.shape[1]
    nj = hv // tn
    heads = tn // dv

    def body(g_ref, w_ref, o_ref, gate_ref, nw_ref, do_ref, dg_ref, dw_ref):
        dy = _nt(g_ref[...], w_ref[...])
        wv = nw_ref[...]
        dw = jnp.zeros((1, dv), f32)
        for h in range(heads):
            sl = slice(h * dv, (h + 1) * dv)
            oh = o_ref[:, sl]
            g = gate_ref[:, sl]
            dyh = dy[:, sl]
            r = lax.rsqrt(jnp.mean(oh * oh, axis=-1, keepdims=True) + RMS_EPS)
            xh = oh * r
            dn = dyh * _silu(g)
            dg_ref[:, sl] = _b(dyh * (xh * wv) * _dsilu(g))
            dxh = dn * wv
            do_ref[:, sl] = r * (dxh - xh * jnp.mean(dxh * xh, axis=-1, keepdims=True))
            dw = dw + jnp.sum(dn * xh, axis=0, keepdims=True)
        first = jnp.logical_and(pl.program_id(0) == 0, pl.program_id(1) == 0)

        @pl.when(first)
        def _():
            dw_ref[...] = dw

        @pl.when(jnp.logical_not(first))
        def _():
            dw_ref[...] += dw

    tile = pl.BlockSpec((tm, tn), lambda i, j: (i, j))
    gate = pl.BlockSpec((tm, tn), lambda i, j: (i, gate_blk * nj + j))
    vec = pl.BlockSpec((1, dv), lambda i, j: (0, 0))
    return pl.pallas_call(
        body, name=name, grid=(l // tm, nj),
        in_specs=[pl.BlockSpec((tm, d), lambda i, j: (i, 0)), pl.BlockSpec((tn, d), lambda i, j: (j, 0)),
                  tile, gate, vec],
        out_specs=[tile, gate, vec],
        out_shape=[jax.ShapeDtypeStruct((l, hv), f32), jax.ShapeDtypeStruct(proj.shape, bf16),
                   jax.ShapeDtypeStruct((1, dv), f32)],
        compiler_params=_params(("arbitrary", "arbitrary"), 48),
    )(dhb, w_out, o, proj, nw)


def _ret_prep(proj, cos, sin, name):
    l = proj.shape[0]
    tr = _tile(l, 256)
    half = RET_DK // 2
    scale = RET_DK ** -0.5

    def body(p_ref, c_ref, s_ref, o_ref):
        rows = pl.program_id(0) * tr + lax.broadcasted_iota(jnp.int32, (tr, 1), 0)
        kmul = jnp.where(rows >= PAD, scale, 0.0).astype(f32)
        c, s = c_ref[...], s_ref[...]
        for j in range(2 * RET_HEADS):
            t1 = p_ref[:, j * RET_DK: j * RET_DK + half]
            t2 = p_ref[:, j * RET_DK + half: (j + 1) * RET_DK]
            o1 = t1 * c - t2 * s
            o2 = t1 * s + t2 * c
            if j >= RET_HEADS:
                o1, o2 = o1 * kmul, o2 * kmul
            o_ref[:, j * RET_DK: j * RET_DK + half] = o1
            o_ref[:, j * RET_DK + half: (j + 1) * RET_DK] = o2

    wide = pl.BlockSpec((tr, 2 * RET_QK), lambda i: (i, 0))
    tab = pl.BlockSpec((tr, half), lambda i: (i, 0))
    return pl.pallas_call(
        body, name=name, grid=(l // tr,), in_specs=[wide, tab, tab], out_specs=wide,
        out_shape=jax.ShapeDtypeStruct((l, 2 * RET_QK), f32),
        compiler_params=_params(("parallel",), 32),
    )(proj, cos, sin)


def _ret_prep_bwd(dq, dk, cos, sin, dproj, name):
    l = dq.shape[0]
    tr = _tile(l, 256)
    half = RET_DK // 2
    scale = RET_DK ** -0.5

    def body(dq_ref, dk_ref, c_ref, s_ref, _, o_ref):
        rows = pl.program_id(0) * tr + lax.broadcasted_iota(jnp.int32, (tr, 1), 0)
        kmul = jnp.where(rows >= PAD, scale, 0.0).astype(f32)
        c, s = c_ref[...], s_ref[...]
        for j in range(2 * RET_HEADS):
            d_ref = dq_ref if j < RET_HEADS else dk_ref
            jj = j % RET_HEADS
            d1 = d_ref[:, jj * RET_DK: jj * RET_DK + half]
            d2 = d_ref[:, jj * RET_DK + half: (jj + 1) * RET_DK]
            if j >= RET_HEADS:
                d1, d2 = d1 * kmul, d2 * kmul
            o_ref[:, j * RET_DK: j * RET_DK + half] = _b(d1 * c + d2 * s)
            o_ref[:, j * RET_DK + half: (j + 1) * RET_DK] = _b(d2 * c - d1 * s)

    nar = pl.BlockSpec((tr, RET_QK), lambda i: (i, 0))
    wide = pl.BlockSpec((tr, 2 * RET_QK), lambda i: (i, 0))
    tab = pl.BlockSpec((tr, half), lambda i: (i, 0))
    return pl.pallas_call(
        body, name=name, grid=(l // tr,), in_specs=[nar, nar, tab, tab, pl.BlockSpec(memory_space=pl.ANY)],
        out_specs=wide, out_shape=jax.ShapeDtypeStruct(dproj.shape, dproj.dtype), input_output_aliases={4: 0},
        compiler_params=_params(("parallel",), 32),
    )(dq, dk, cos, sin, dproj)


RET_BLOCK_CHUNKS = 3


def _ret_block(l):
    nch = l // CHUNK
    return RET_BLOCK_CHUNKS * CHUNK if nch % RET_BLOCK_CHUNKS == 0 else CHUNK


def _ret_decay(lg, rb):
    idx = lax.broadcasted_iota(jnp.int32, (rb, 1), 0).astype(f32)
    ri = lax.broadcasted_iota(jnp.int32, (rb, rb), 0)
    ci = lax.broadcasted_iota(jnp.int32, (rb, rb), 1)
    rel = (ri - ci).astype(f32)
    dmask = jnp.where(ri >= ci, jnp.exp(lg * jnp.maximum(rel, 0.0)), 0.0)
    xi = jnp.exp(lg * (idx + 1.0))
    zeta = jnp.exp(lg * (rb - 1.0 - idx))
    return dmask, xi, zeta


def _ret_scan_fwd(qk, proj, lgs, gcs, name, comm=None):
    l = qk.shape[0]
    rb = _ret_block(l)
    nb = l // rb

    def body(lg_ref, gc_ref, q_ref, k_ref, v_ref, o_ref, st_ref, s_ref):
        @pl.when(pl.program_id(0) == 0)
        def _():
            s_ref[...] = jnp.zeros_like(s_ref)

        hs = range(RET_HEADS)
        dec = [_ret_decay(lg_ref[h], rb) for h in hs]
        q = [q_ref[:, h * RET_DK:(h + 1) * RET_DK] for h in hs]
        k = [k_ref[:, h * RET_DK:(h + 1) * RET_DK] for h in hs]
        vb = [_b(v_ref[:, h * RET_DV:(h + 1) * RET_DV]) for h in hs]
        s = [s_ref[h] for h in hs]
        sb = [_b(s[h]) for h in hs]
        scores = [_b(_nt(_b(q[h]), _b(k[h])) * dec[h][0]) for h in hs]
        inter = [_nn(_b(q[h] * dec[h][1]), sb[h]) for h in hs]
        kv = [_tn(_b(k[h] * dec[h][2]), vb[h]) for h in hs]
        for h in hs:
            st_ref[0, h] = sb[h]
            o_ref[:, h * RET_DV:(h + 1) * RET_DV] = _nn(scores[h], vb[h]) + inter[h]
            s_ref[h] = gc_ref[h] * s[h] + kv[h]

    smem = pl.BlockSpec(memory_space=pltpu.SMEM)
    return _call(
        body, (lgs, gcs, qk, qk, proj), name=name, grid=(nb,),
        in_specs=[smem, smem,
                  pl.BlockSpec((rb, RET_QK), lambda n: (n, 0)),
                  pl.BlockSpec((rb, RET_QK), lambda n: (n, 1)),
                  pl.BlockSpec((rb, RET_V), lambda n: (n, 1))],
        out_specs=[pl.BlockSpec((rb, RET_V), lambda n: (n, 0)),
                   pl.BlockSpec((1, RET_HEADS, RET_DK, RET_DV), lambda n: (n, 0, 0, 0))],
        out_shape=[jax.ShapeDtypeStruct((l, RET_V), f32),
                   jax.ShapeDtypeStruct((nb, RET_HEADS, RET_DK, RET_DV), bf16)],
        scratch=[pltpu.VMEM((RET_HEADS, RET_DK, RET_DV), f32)], sem=("arbitrary",), vmem_mb=40, comm=comm)


def _ret_scan_bwd(qk, proj, states, do, dproj, lgs, gcs, name, comm=None):
    l = qk.shape[0]
    rb = _ret_block(l)
    nb = l // rb

    def body(lg_ref, gc_ref, q_ref, k_ref, v_ref, st_ref, do_ref, _, dq_ref, dk_ref, dv_ref, ds_ref):
        @pl.when(pl.program_id(0) == 0)
        def _():
            ds_ref[...] = jnp.zeros_like(ds_ref)

        hs = range(RET_HEADS)
        dec = [_ret_decay(lg_ref[h], rb) for h in hs]
        q = [q_ref[:, h * RET_DK:(h + 1) * RET_DK] for h in hs]
        k = [k_ref[:, h * RET_DK:(h + 1) * RET_DK] for h in hs]
        qb, kb = [_b(t) for t in q], [_b(t) for t in k]
        vb = [_b(v_ref[:, h * RET_DV:(h + 1) * RET_DV]) for h in hs]
        dob = [_b(do_ref[:, h * RET_DV:(h + 1) * RET_DV]) for h in hs]
        dsp = [ds_ref[h] for h in hs]
        dspb = [_b(t) for t in dsp]
        scores = [_b(_nt(qb[h], kb[h]) * dec[h][0]) for h in hs]
        dscores = [_b(_nt(dob[h], vb[h]) * dec[h][0]) for h in hs]
        for h in hs:
            dq_ref[:, h * RET_DK:(h + 1) * RET_DK] = _nn(dscores[h], kb[h]) + _nt(dob[h], st_ref[0, h]) * dec[h][1]
        for h in hs:
            dk_ref[:, h * RET_DK:(h + 1) * RET_DK] = _tn(dscores[h], qb[h]) + _nt(vb[h], dspb[h]) * dec[h][2]
        for h in hs:
            dv_ref[:, h * RET_DV:(h + 1) * RET_DV] = _b(_tn(scores[h], dob[h]) + _nn(_b(k[h] * dec[h][2]), dspb[h]))
        for h in hs:
            ds_ref[h] = gc_ref[h] * dsp[h] + _tn(_b(q[h] * dec[h][1]), dob[h])

    smem = pl.BlockSpec(memory_space=pltpu.SMEM)
    rev = lambda n: nb - 1 - n
    return _call(
        body, (lgs, gcs, qk, qk, proj, states, do, dproj), name=name, grid=(nb,),
        in_specs=[smem, smem,
                  pl.BlockSpec((rb, RET_QK), lambda n: (rev(n), 0)),
                  pl.BlockSpec((rb, RET_QK), lambda n: (rev(n), 1)),
                  pl.BlockSpec((rb, RET_V), lambda n: (rev(n), 1)),
                  pl.BlockSpec((1, RET_HEADS, RET_DK, RET_DV), lambda n: (rev(n), 0, 0, 0)),
                  pl.BlockSpec((rb, RET_V), lambda n: (rev(n), 0)),
                  pl.BlockSpec(memory_space=pl.ANY)],
        out_specs=[pl.BlockSpec((rb, RET_QK), lambda n: (rev(n), 0)),
                   pl.BlockSpec((rb, RET_QK), lambda n: (rev(n), 0)),
                   pl.BlockSpec((rb, RET_V), lambda n: (rev(n), 1))],
        out_shape=[jax.ShapeDtypeStruct((l, RET_QK), f32), jax.ShapeDtypeStruct((l, RET_QK), f32),
                   jax.ShapeDtypeStruct(dproj.shape, dproj.dtype)],
        scratch=[pltpu.VMEM((RET_HEADS, RET_DK, RET_DV), f32)], sem=("arbitrary",), vmem_mb=40, comm=comm,
        aliases={7: 2})


CONV_BLK = 1024
HALO = 8


def _slab_rows(r):
    return pl.ds(pl.multiple_of(r * HALO, HALO), HALO)


def _conv_slab(x_ref, p_ref, r, i, tr):
    cur = x_ref[_slab_rows(r), :]
    prev = jnp.where(r > 0, x_ref[_slab_rows(jnp.maximum(r - 1, 0)), :], p_ref[...])
    row0 = i * tr + r * HALO
    cur = jnp.where(row0 >= PAD, cur, 0.0)
    prev = jnp.where(row0 - HALO >= PAD, prev, 0.0)
    lrow = lax.broadcasted_iota(jnp.int32, (HALO, 1), 0)
    shifted = [jnp.where(lrow < s, pltpu.roll(prev, s, 0), pltpu.roll(cur, s, 0)) for s in range(1, CONV_K)]
    return [cur] + shifted


def _conv_of(xs, w):
    acc = xs[0] * w[CONV_K - 1:CONV_K, :]
    for s in range(1, CONV_K):
        acc = acc + xs[s] * w[CONV_K - 1 - s:CONV_K - s, :]
    return acc


def _slab_loop(n_slabs, fn, init=None):
    return lax.fori_loop(0, n_slabs, fn, init, unroll=8)


def _dn_conv_fwd(proj, conv_w, name, comm=None):
    l = proj.shape[0]
    tr = _tile(l, 256)
    nblk = DN_CONV_CH // CONV_BLK
    heads = CONV_BLK // DN_DK

    def body(x_ref, p_ref, w_ref, o_ref):
        i, j = pl.program_id(0), pl.program_id(1)
        w = w_ref[...]

        def act(r):
            return _silu(_conv_of(_conv_slab(x_ref, p_ref, r, i, tr), w))

        def normed(scale):
            def slab(r, carry):
                a = act(r)
                outs = []
                for h in range(heads):
                    ah = a[:, h * DN_DK:(h + 1) * DN_DK]
                    outs.append(ah * (lax.rsqrt(jnp.sum(ah * ah, axis=-1, keepdims=True) + RMS_EPS) * scale))
                o_ref[_slab_rows(r), :] = jnp.concatenate(outs, axis=1)
                return carry
            return slab

        def plain(r, carry):
            o_ref[_slab_rows(r), :] = act(r)
            return carry

        @pl.when(j == 0)
        def _():
            _slab_loop(tr // HALO, normed(DN_DK ** -0.5))

        @pl.when(j == 1)
        def _():
            _slab_loop(tr // HALO, normed(1.0))

        @pl.when(j >= 2)
        def _():
            _slab_loop(tr // HALO, plain)

    hb = tr // HALO
    return _call(
        body, (proj, proj, conv_w), name=name, grid=(l // tr, nblk),
        in_specs=[pl.BlockSpec((tr, CONV_BLK), lambda i, j: (i, j)),
                  pl.BlockSpec((HALO, CONV_BLK), lambda i, j: (jnp.maximum(i * hb - 1, 0), j)),
                  pl.BlockSpec((CONV_K, CONV_BLK), lambda i, j: (0, j))],
        out_specs=[pl.BlockSpec((tr, CONV_BLK), lambda i, j: (i, j))],
        out_shape=[jax.ShapeDtypeStruct((l, DN_CONV_CH), f32)],
        scratch=[], sem=("parallel", "parallel"), vmem_mb=32, comm=comm)


def _dn_conv_bwd_a(proj, conv_w, dqkv, name, comm=None):
    l = proj.shape[0]
    tr = _tile(l, 256)
    nblk = DN_CONV_CH // CONV_BLK
    heads = CONV_BLK // DN_DK

    def body(x_ref, p_ref, w_ref, d_ref, dc_ref, dw_ref, acc_ref):
        j, i = pl.program_id(0), pl.program_id(1)
        w = w_ref[...]
        acc_ref[...] = jnp.zeros_like(acc_ref)

        def slab_of(l2_scale):
            def slab(r, carry):
                xs = _conv_slab(x_ref, p_ref, r, i, tr)
                c = _conv_of(xs, w)
                a = _silu(c)
                dy = d_ref[_slab_rows(r), :]
                if l2_scale is None:
                    da = dy
                else:
                    parts = []
                    for h in range(heads):
                        sl = slice(h * DN_DK, (h + 1) * DN_DK)
                        ah, dyh = a[:, sl], dy[:, sl]
                        rn = lax.rsqrt(jnp.sum(ah * ah, axis=-1, keepdims=True) + RMS_EPS)
                        yh = ah * rn
                        parts.append((rn * l2_scale) * (dyh - yh * jnp.sum(dyh * yh, axis=-1, keepdims=True)))
                    da = jnp.concatenate(parts, axis=1)
                dc = da * _dsilu(c)
                dc_ref[_slab_rows(r), :] = dc
                for s in range(CONV_K):
                    acc_ref[CONV_K - 1 - s] += dc * xs[s]
                return carry
            return slab

        @pl.when(j == 0)
        def _():
            _slab_loop(tr // HALO, slab_of(DN_DK ** -0.5))

        @pl.when(j == 1)
        def _():
            _slab_loop(tr // HALO, slab_of(1.0))

        @pl.when(j >= 2)
        def _():
            _slab_loop(tr // HALO, slab_of(None))

        ksel = lax.broadcasted_iota(jnp.int32, (CONV_K, 1), 0)
        dw = jnp.zeros((CONV_K, CONV_BLK), f32)
        for k in range(CONV_K):
            dw = dw + jnp.where(ksel == k, jnp.sum(acc_ref[k], axis=0, keepdims=True), 0.0)

        @pl.when(i == 0)
        def _():
            dw_ref[...] = dw

        @pl.when(i > 0)
        def _():
            dw_ref[...] += dw

    hb = tr // HALO
    blk = pl.BlockSpec((tr, CONV_BLK), lambda j, i: (i, j))
    return _call(
        body, (proj, proj, conv_w, dqkv), name=name, grid=(nblk, l // tr),
        in_specs=[blk, pl.BlockSpec((HALO, CONV_BLK), lambda j, i: (jnp.maximum(i * hb - 1, 0), j)),
                  pl.BlockSpec((CONV_K, CONV_BLK), lambda j, i: (0, j)), blk],
        out_specs=[blk, pl.BlockSpec((CONV_K, CONV_BLK), lambda j, i: (0, j))],
        out_shape=[jax.ShapeDtypeStruct((l, DN_CONV_CH), f32), jax.ShapeDtypeStruct((CONV_K, DN_CONV_CH), f32)],
        scratch=[pltpu.VMEM((CONV_K, HALO, CONV_BLK), f32)], sem=("parallel", "arbitrary"), vmem_mb=40, comm=comm)


def _dn_conv_bwd_b(dc, conv_w, dproj, name):
    l = dc.shape[0]
    tr = _tile(l, 256)
    nblk = DN_CONV_CH // CONV_BLK
    nrow = l // tr

    n_slabs = tr // HALO
    pair = 2 * HALO

    def body(d_ref, n_ref, w_ref, _, o_ref):
        i = pl.program_id(0)
        w = w_ref[...]
        nxt_tile = jnp.where(i < nrow - 1, n_ref[...], 0.0)
        lrow = lax.broadcasted_iota(jnp.int32, (HALO, 1), 0)

        def one(r):
            cur = d_ref[_slab_rows(r), :]
            nxt = jnp.where(r < n_slabs - 1, d_ref[_slab_rows(jnp.minimum(r + 1, n_slabs - 1)), :], nxt_tile)
            acc = cur * w[CONV_K - 1:CONV_K, :]
            for s in range(1, CONV_K):
                up = jnp.where(lrow >= HALO - s, pltpu.roll(nxt, HALO - s, 0), pltpu.roll(cur, HALO - s, 0))
                acc = acc + up * w[CONV_K - 1 - s:CONV_K - s, :]
            return jnp.where(i * tr + r * HALO >= PAD, acc, 0.0)

        def two(q, carry):
            rows = pl.ds(pl.multiple_of(q * pair, pair), pair)
            o_ref[rows, :] = _b(jnp.concatenate([one(2 * q), one(2 * q + 1)], axis=0))
            return carry

        lax.fori_loop(0, n_slabs // 2, two, None, unroll=4)

    hb = tr // HALO
    nh = l // HALO
    return pl.pallas_call(
        body, name=name, grid=(nrow, nblk),
        in_specs=[pl.BlockSpec((tr, CONV_BLK), lambda i, j: (i, j)),
                  pl.BlockSpec((HALO, CONV_BLK), lambda i, j: (jnp.minimum((i + 1) * hb, nh - 1), j)),
                  pl.BlockSpec((CONV_K, CONV_BLK), lambda i, j: (0, j)),
                  pl.BlockSpec(memory_space=pl.ANY)],
        out_specs=pl.BlockSpec((tr, CONV_BLK), lambda i, j: (i, j)),
        out_shape=jax.ShapeDtypeStruct(dproj.shape, dproj.dtype), input_output_aliases={3: 0},
        compiler_params=_params(("parallel", "parallel"), 32),
    )(dc, dc, conv_w, dproj)


BA_W = LANES


def _dn_gates(ba_ref, al_ref, dt_ref, n):
    rows = n * CHUNK + lax.broadcasted_iota(jnp.int32, (CHUNK, 1), 0)
    vm = (rows >= PAD).astype(f32)
    bin_ = ba_ref[:, 0:DN_HEADS]
    z = ba_ref[:, DN_HEADS:2 * DN_HEADS] + dt_ref[...]
    sp = jnp.maximum(z, 0.0) + jnp.log1p(jnp.exp(-jnp.abs(z)))
    ea = jnp.exp(al_ref[...])
    beta = _sigmoid(bin_) * vm
    g = -ea * sp * vm
    return vm, bin_, z, ea, beta, g


def _tri():
    ri = lax.broadcasted_iota(jnp.int32, (CHUNK, CHUNK), 0)
    ci = lax.broadcasted_iota(jnp.int32, (CHUNK, CHUNK), 1)
    return ri, ci


def _split(a):
    hi = _b(a)
    return hi, _b(a - hi.astype(f32))


def _mm3(a, b, dot=_nn):
    (ah, al), (bh, bl) = _split(a), _split(b)
    return dot(ah, bh) + (dot(ah, bl) + dot(al, bh))


def _cumsum_rows(tri, g):
    tb = _b(tri)
    g1 = _b(g)
    r1 = g - g1.astype(f32)
    g2 = _b(r1)
    g3 = _b(r1 - g2.astype(f32))
    return _nn(tb, g1) + (_nn(tb, g2) + _nn(tb, g3))


DN_SCAN_CHUNKS = 3


def _scan_chunks(nch):
    return DN_SCAN_CHUNKS if nch % DN_SCAN_CHUNKS == 0 else 1


def _dn_prep(qkv, ba, a_log, dt_bias, name, comm=None):
    l = qkv.shape[0]
    nch = l // CHUNK
    heads = range(DN_HEADS)

    cb = _scan_chunks(nch)
    items = [(c, h) for c in range(cb) for h in heads]

    def body(q_ref, k_ref, v_ref, ba_ref, al_ref, dt_ref, t_ref, u_ref, wq_ref, pk_ref, eg_ref, kpt_ref, qwt_ref):
        n0 = pl.program_id(0) * cb
        ri, ci = _tri()
        incl, strict = ri >= ci, ri > ci
        eye = (ri == ci).astype(f32)
        rows = [slice(c * CHUNK, (c + 1) * CHUNK) for c in range(cb)]
        gam, gam_t, beta = [], [], []
        for c in range(cb):
            _, _, _, _, beta_c, g_c = _dn_gates(ba_ref[rows[c], :], al_ref, dt_ref, n0 + c)
            gam.append(_cumsum_rows(incl.astype(f32), g_c))
            gam_t.append(gam[c].T)
            beta.append(beta_c)
        gc = {(c, h): gam[c][:, h:h + 1] for c, h in items}
        bh = {(c, h): beta[c][:, h:h + 1] for c, h in items}
        kh = {(c, h): k_ref[rows[c], h * DN_DK:(h + 1) * DN_DK] for c, h in items}
        kb = {i: _b(kh[i]) for i in items}
        decay = {(c, h): jnp.exp(jnp.where(incl, gc[c, h] - gam_t[c][h:h + 1, :], -jnp.inf)) for c, h in items}
        a = {i: jnp.where(strict, bh[i] * _nt(kb[i], kb[i]) * decay[i], 0.0) for i in items}
        t = {i: eye - a[i] for i in items}
        p = a
        for level in range(int(math.log2(CHUNK)) - 1):
            mm = _mm3 if level < 2 else (lambda x, y: _nn(_b(x), _b(y)))
            p = {i: mm(p[i], p[i]) for i in items}
            t = {i: t[i] + mm(t[i], p[i]) for i in items}
        eg = {i: jnp.exp(gc[i]) for i in items}
        for c, h in items:
            i = (c, h)
            t_ref[c, h] = t[i]
            u_ref[rows[c], h * DN_DV:(h + 1) * DN_DV] = _mm3(t[i], v_ref[rows[c], h * DN_DV:(h + 1) * DN_DV] * bh[i])
            w = _mm3(t[i], kh[i] * (bh[i] * eg[i]))
            wq_ref[c, h, 0:CHUNK, :] = _b(w)
            qwt_ref[c, h, DN_DK:2 * DN_DK, :] = _b(w.T)
        for c, h in items:
            i = (c, h)
            qh = q_ref[rows[c], h * DN_DK:(h + 1) * DN_DK]
            gl = gc[i][CHUNK - 1:CHUNK, :]
            qe = qh * eg[i]
            ke = kh[i] * jnp.exp(gl - gc[i])
            pmat = _nt(_b(qh), kb[i]) * decay[i]
            wq_ref[c, h, CHUNK:2 * CHUNK, :] = _b(qe)
            qwt_ref[c, h, 0:DN_DK, :] = _b(qe.T)
            pk_ref[c, h, 0:CHUNK, :] = _b(pmat)
            pk_ref[c, h, CHUNK:CHUNK + DN_DK, :] = _b(ke.T)
            kpt_ref[c, h, :, 0:DN_DK] = _b(ke)
            kpt_ref[c, h, :, DN_DK:DN_DK + CHUNK] = _b(pmat.T)
            eg_ref[c, h] = jnp.broadcast_to(jnp.exp(gl), (8, LANES))

    vec = pl.BlockSpec((1, DN_HEADS), lambda n: (0, 0))
    return _call(
        body, (qkv, qkv, qkv, ba, a_log, dt_bias), name=name, grid=(nch // cb,),
        in_specs=[pl.BlockSpec((cb * CHUNK, DN_QK), lambda n: (n, 0)),
                  pl.BlockSpec((cb * CHUNK, DN_QK), lambda n: (n, 1)),
                  pl.BlockSpec((cb * CHUNK, DN_V), lambda n: (n, 1)),
                  pl.BlockSpec((cb * CHUNK, BA_W), lambda n: (n, 0)), vec, vec],
        out_specs=[pl.BlockSpec((cb, DN_HEADS, CHUNK, CHUNK), lambda n: (n, 0, 0, 0)),
                   pl.BlockSpec((cb * CHUNK, DN_V), lambda n: (n, 0)),
                   pl.BlockSpec((cb, DN_HEADS, 2 * CHUNK, DN_DK), lambda n: (n, 0, 0, 0)),
                   pl.BlockSpec((cb, DN_HEADS, CHUNK + DN_DK, CHUNK), lambda n: (n, 0, 0, 0)),
                   pl.BlockSpec((cb, DN_HEADS, 8, LANES), lambda n: (n, 0, 0, 0)),
                   pl.BlockSpec((cb, DN_HEADS, CHUNK, DN_DK + CHUNK), lambda n: (n, 0, 0, 0)),
                   pl.BlockSpec((cb, DN_HEADS, 2 * DN_DK, CHUNK), lambda n: (n, 0, 0, 0))],
        out_shape=[jax.ShapeDtypeStruct((nch, DN_HEADS, CHUNK, CHUNK), f32),
                   jax.ShapeDtypeStruct((l, DN_V), f32),
                   jax.ShapeDtypeStruct((nch, DN_HEADS, 2 * CHUNK, DN_DK), bf16),
                   jax.ShapeDtypeStruct((nch, DN_HEADS, CHUNK + DN_DK, CHUNK), bf16),
                   jax.ShapeDtypeStruct((nch, DN_HEADS, 8, LANES), f32),
                   jax.ShapeDtypeStruct((nch, DN_HEADS, CHUNK, DN_DK + CHUNK), bf16),
                   jax.ShapeDtypeStruct((nch, DN_HEADS, 2 * DN_DK, CHUNK), bf16)],
        sem=("parallel",), vmem_mb=40, comm=comm)


def _dn_scan_fwd(u, wq, pk, egl, name):
    l = u.shape[0]
    nch = l // CHUNK
    cs = _scan_chunks(nch)

    def body(u_ref, wq_ref, pk_ref, eg_ref, o_ref, st_ref, vn_ref, s_ref):
        @pl.when(pl.program_id(0) == 0)
        def _():
            s_ref[...] = jnp.zeros_like(s_ref)

        hs = range(DN_HEADS)
        cols = [slice(h * DN_DV, (h + 1) * DN_DV) for h in hs]
        s = [s_ref[h] for h in hs]
        for c in range(cs):
            rows = slice(c * CHUNK, (c + 1) * CHUNK)
            sb = [_b(s[h]) for h in hs]
            x = [_nn(wq_ref[c, h], sb[h]) for h in hs]
            vnb = [_b(u_ref[rows, cols[h]] - x[h][0:CHUNK]) for h in hs]
            y = [_nn(pk_ref[c, h], vnb[h]) for h in hs]
            for h in hs:
                st_ref[c, h] = sb[h]
                vn_ref[rows, cols[h]] = vnb[h]
                o_ref[rows, cols[h]] = x[h][CHUNK:2 * CHUNK] + y[h][0:CHUNK]
            s = [eg_ref[c, h][0:1, 0:1] * s[h] + y[h][CHUNK:CHUNK + DN_DK] for h in hs]
        for h in hs:
            s_ref[h] = s[h]

    return pl.pallas_call(
        body, name=name, grid=(nch // cs,),
        in_specs=[pl.BlockSpec((cs * CHUNK, DN_V), lambda n: (n, 0)),
                  pl.BlockSpec((cs, DN_HEADS, 2 * CHUNK, DN_DK), lambda n: (n, 0, 0, 0)),
                  pl.BlockSpec((cs, DN_HEADS, CHUNK + DN_DK, CHUNK), lambda n: (n, 0, 0, 0)),
                  pl.BlockSpec((cs, DN_HEADS, 8, LANES), lambda n: (n, 0, 0, 0))],
        out_specs=[pl.BlockSpec((cs * CHUNK, DN_V), lambda n: (n, 0)),
                   pl.BlockSpec((cs, DN_HEADS, DN_DK, DN_DV), lambda n: (n, 0, 0, 0)),
                   pl.BlockSpec((cs * CHUNK, DN_V), lambda n: (n, 0))],
        out_shape=[jax.ShapeDtypeStruct((l, DN_V), f32),
                   jax.ShapeDtypeStruct((nch, DN_HEADS, DN_DK, DN_DV), bf16),
                   jax.ShapeDtypeStruct((l, DN_V), bf16)],
        scratch_shapes=[pltpu.VMEM((DN_HEADS, DN_DK, DN_DV), f32)],
        compiler_params=_params(("arbitrary",), 40),
    )(u, wq, pk, egl)


def _dn_scan_bwd(do, kpt, qwt, egl, name):
    l = do.shape[0]
    nch = l // CHUNK
    cs = _scan_chunks(nch)
    nblk = nch // cs

    def body(do_ref, kpt_ref, qwt_ref, eg_ref, dvn_ref, dsp_ref, ds_ref):
        @pl.when(pl.program_id(0) == 0)
        def _():
            ds_ref[...] = jnp.zeros_like(ds_ref)

        hs = range(DN_HEADS)
        cols = [slice(h * DN_DV, (h + 1) * DN_DV) for h in hs]
        ds = [ds_ref[h] for h in hs]
        for c in reversed(range(cs)):
            rows = slice(c * CHUNK, (c + 1) * CHUNK)
            dspb = [_b(ds[h]) for h in hs]
            dob = [_b(do_ref[rows, cols[h]]) for h in hs]
            dvn = [_nn(kpt_ref[c, h][:, 0:DN_DK], dspb[h]) + _nn(kpt_ref[c, h][:, DN_DK:DN_DK + CHUNK], dob[h])
                   for h in hs]
            for h in hs:
                dsp_ref[c, h] = dspb[h]
                dvn_ref[rows, cols[h]] = dvn[h]
            ds = [eg_ref[c, h][0:1, 0:1] * ds[h] + _nn(qwt_ref[c, h][0:DN_DK], dob[h])
                  - _nn(qwt_ref[c, h][DN_DK:2 * DN_DK], _b(dvn[h])) for h in hs]
        for h in hs:
            ds_ref[h] = ds[h]

    rev = lambda s: nblk - 1 - s
    return pl.pallas_call(
        body, name=name, grid=(nblk,),
        in_specs=[pl.BlockSpec((cs * CHUNK, DN_V), lambda s: (rev(s), 0)),
                  pl.BlockSpec((cs, DN_HEADS, CHUNK, DN_DK + CHUNK), lambda s: (rev(s), 0, 0, 0)),
                  pl.BlockSpec((cs, DN_HEADS, 2 * DN_DK, CHUNK), lambda s: (rev(s), 0, 0, 0)),
                  pl.BlockSpec((cs, DN_HEADS, 8, LANES), lambda s: (rev(s), 0, 0, 0))],
        out_specs=[pl.BlockSpec((cs * CHUNK, DN_V), lambda s: (rev(s), 0)),
                   pl.BlockSpec((cs, DN_HEADS, DN_DK, DN_DV), lambda s: (rev(s), 0, 0, 0))],
        out_shape=[jax.ShapeDtypeStruct((l, DN_V), f32),
                   jax.ShapeDtypeStruct((nch, DN_HEADS, DN_DK, DN_DV), bf16)],
        scratch_shapes=[pltpu.VMEM((DN_HEADS, DN_DK, DN_DV), f32)],
        compiler_params=_params(("arbitrary",), 40),
    )(do, kpt, qwt, egl)


def _dn_post_bwd(qkv, ba, a_log, dt_bias, states, dsp_all, tinv_all, u_all, wq, vn_all, do, dvn_all, name):
    l = qkv.shape[0]
    nch = l // CHUNK
    cb = _scan_chunks(nch)
    items = [(c, h) for c in range(cb) for h in range(DN_HEADS)]

    def body(q_ref, k_ref, v_ref, ba_ref, al_ref, dt_ref, st_ref, dsp_ref, t_ref, u_ref, wq_ref, vn_ref, do_ref,
             dvn_ref, dqkv_ref, dba_ref, dal_ref, ddt_ref):
        step = pl.program_id(0)
        ri, ci = _tri()
        incl, strict = ri >= ci, ri > ci
        lane8 = lax.broadcasted_iota(jnp.int32, (1, DN_HEADS), 1)
        sub8 = lax.broadcasted_iota(jnp.int32, (DN_HEADS, 1), 0)
        last = (lax.broadcasted_iota(jnp.int32, (CHUNK, 1), 0) == CHUNK - 1).astype(f32)
        rsum = lambda t: jnp.sum(t, axis=-1, keepdims=True)
        rows = [slice(c * CHUNK, (c + 1) * CHUNK) for c in range(cb)]
        gates = [_dn_gates(ba_ref[rows[c], :], al_ref, dt_ref, step * cb + c) for c in range(cb)]
        gam = [_cumsum_rows(incl.astype(f32), gates[c][5]) for c in range(cb)]
        gam_t = [gam[c].T for c in range(cb)]
        each = lambda fn: {(c, h): fn(c, h) for c, h in items}
        dk_cols = lambda h: slice(h * DN_DK, (h + 1) * DN_DK)
        dv_cols = lambda h: slice(h * DN_DV, (h + 1) * DN_DV)
        gc = each(lambda c, h: gam[c][:, h:h + 1])
        bh = each(lambda c, h: gates[c][4][:, h:h + 1])
        qh = each(lambda c, h: q_ref[rows[c], dk_cols(h)])
        kh = each(lambda c, h: k_ref[rows[c], dk_cols(h)])
        doh = each(lambda c, h: _b(do_ref[rows[c], dv_cols(h)]))
        sb = each(lambda c, h: st_ref[c, h])
        dspb = each(lambda c, h: dsp_ref[c, h])
        vnb = each(lambda c, h: vn_ref[rows[c], dv_cols(h)])
        dvn = each(lambda c, h: dvn_ref[rows[c], dv_cols(h)])
        wb = each(lambda c, h: wq_ref[c, h, 0:CHUNK, :])
        decay = each(lambda c, h: jnp.exp(jnp.where(incl, gc[c, h] - gam_t[c][h:h + 1, :], -jnp.inf)))
        qb, kb = each(lambda c, h: _b(qh[c, h])), each(lambda c, h: _b(kh[c, h]))
        eg = each(lambda c, h: jnp.exp(gc[c, h]))
        gl = each(lambda c, h: gc[c, h][CHUNK - 1:CHUNK, :])
        ekd = each(lambda c, h: jnp.exp(gl[c, h] - gc[c, h]))
        dvnb = each(lambda c, h: _b(dvn[c, h]))
        kk = each(lambda c, h: _nt(kb[c, h], kb[c, h]))
        p = each(lambda c, h: _nt(qb[c, h], kb[c, h]) * decay[c, h])
        dpraw = each(lambda c, h: _nt(doh[c, h], vnb[c, h]))
        dqe = each(lambda c, h: _nt(doh[c, h], sb[c, h]))
        dke = each(lambda c, h: _nt(vnb[c, h], dspb[c, h]))
        dw = each(lambda c, h: -_nt(dvnb[c, h], sb[c, h]))
        dru = each(lambda c, h: _mm3(t_ref[c, h], dvn[c, h], _tn))
        drw = each(lambda c, h: _mm3(t_ref[c, h], dw[c, h], _tn))
        dqk = each(lambda c, h: _b(dpraw[c, h] * decay[c, h]))
        for c, h in items:
            i = (c, h)
            dqkv_ref[rows[c], dk_cols(h)] = _nn(dqk[i], kb[i]) + dqe[i] * eg[i]
            dqkv_ref[rows[c], 2 * DN_QK + h * DN_DV:2 * DN_QK + (h + 1) * DN_DV] = bh[i] * dru[i]
        da = each(lambda c, h: jnp.where(strict, -(_nt(_b(dru[c, h]), _b(u_ref[rows[c], dv_cols(h)]))
                                                   + _nt(_b(drw[c, h]), wb[c, h])), 0.0))
        dkk = each(lambda c, h: _b(da[c, h] * bh[c, h] * decay[c, h]))
        for c, h in items:
            i = (c, h)
            dqkv_ref[rows[c], DN_QK + h * DN_DK:DN_QK + (h + 1) * DN_DK] = (
                _tn(dqk[i], qb[i]) + dke[i] * ekd[i] + (bh[i] * eg[i]) * drw[i]
                + _nn(dkk[i], kb[i]) + _tn(dkk[i], kb[i]))
        dal = jnp.zeros((1, DN_HEADS), f32)
        ddt = jnp.zeros((1, DN_HEADS), f32)
        dba_ref[...] = jnp.zeros_like(dba_ref)
        for c in range(cb):
            vm, bin_, z, ea, _, g = gates[c]
            dbeta = jnp.zeros((CHUNK, DN_HEADS), f32)
            dgam = jnp.zeros((CHUNK, DN_HEADS), f32)
            dgam_neg_t = jnp.zeros((DN_HEADS, CHUNK), f32)
            for h in range(DN_HEADS):
                i = (c, h)
                keg = kh[i] * eg[i]
                ke = kh[i] * ekd[i]
                rw = rsum(drw[i] * keg)
                rke = rsum(dke[i] * ke)
                db_h = rsum(dru[i] * v_ref[rows[c], dv_cols(h)]) + rw + rsum(da[i] * kk[i] * decay[i])
                mm = da[i] * (bh[i] * kk[i] * decay[i]) + dpraw[i] * p[i]
                dgl = (jnp.sum(rke, axis=0, keepdims=True)
                       + jnp.exp(gl[i]) * jnp.sum(rsum(dspb[i].astype(f32) * sb[i].astype(f32)), axis=0,
                                                  keepdims=True))
                dg_h = rsum(mm) + rw * bh[i] + rsum(dqe[i] * (qh[i] * eg[i])) - rke + last * dgl
                dbeta = dbeta + jnp.where(lane8 == h, db_h, 0.0)
                dgam = dgam + jnp.where(lane8 == h, dg_h, 0.0)
                dgam_neg_t = dgam_neg_t + jnp.where(sub8 == h, jnp.sum(mm, axis=0, keepdims=True), 0.0)
            dgam = dgam - dgam_neg_t.T
            dg = _cumsum_rows((ri <= ci).astype(f32), dgam)
            sg = _sigmoid(bin_)
            dain = dg * (-ea) * vm * _sigmoid(z)
            dba_ref[rows[c], 0:DN_HEADS] = dbeta * vm * sg * (1.0 - sg)
            dba_ref[rows[c], DN_HEADS:2 * DN_HEADS] = dain
            dal = dal + jnp.sum(dg * g, axis=0, keepdims=True)
            ddt = ddt + jnp.sum(dain, axis=0, keepdims=True)

        @pl.when(step == 0)
        def _():
            dal_ref[...] = dal
            ddt_ref[...] = ddt

        @pl.when(step > 0)
        def _():
            dal_ref[...] += dal
            ddt_ref[...] += ddt

    vec = pl.BlockSpec((1, DN_HEADS), lambda s: (0, 0))
    qs = pl.BlockSpec((cb * CHUNK, DN_QK), lambda s: (s, 0))
    ks = pl.BlockSpec((cb * CHUNK, DN_QK), lambda s: (s, 1))
    vs = pl.BlockSpec((cb * CHUNK, DN_V), lambda s: (s, 1))
    v0 = pl.BlockSpec((cb * CHUNK, DN_V), lambda s: (s, 0))
    st = pl.BlockSpec((cb, DN_HEADS, DN_DK, DN_DV), lambda s: (s, 0, 0, 0))
    return pl.pallas_call(
        body, name=name, grid=(nch // cb,),
        in_specs=[qs, ks, vs, pl.BlockSpec((cb * CHUNK, BA_W), lambda s: (s, 0)), vec, vec, st, st,
                  pl.BlockSpec((cb, DN_HEADS, CHUNK, CHUNK), lambda s: (s, 0, 0, 0)),
                  v0, pl.BlockSpec((cb, DN_HEADS, 2 * CHUNK, DN_DK), lambda s: (s, 0, 0, 0)), v0, v0, v0],
        out_specs=[pl.BlockSpec((cb * CHUNK, DN_CONV_CH), lambda s: (s, 0)),
                   pl.BlockSpec((cb * CHUNK, BA_W), lambda s: (s, 0)), vec, vec],
        out_shape=[jax.ShapeDtypeStruct((l, DN_CONV_CH), f32), jax.ShapeDtypeStruct((l, BA_W), f32),
                   jax.ShapeDtypeStruct((1, DN_HEADS), f32), jax.ShapeDtypeStruct((1, DN_HEADS), f32)],
        compiler_params=_params(("arbitrary",), 48),
    )(qkv, qkv, qkv, ba, a_log, dt_bias, states, dsp_all, tinv_all, u_all, wq, vn_all, do, dvn_all)


def _ffn_fwd(h, hn, wg, wu, wd, tb, th, tag, plan, next_norm_w=None):
    fh, d = wd.shape
    a, b, s = plan.call(f"{tag}_gu", functools.partial(_ffn_gu, tm=th // 2, tn=fh // 2), hn, wg, wu, n_out=3)
    out = plan.matmul(f"{tag}_down", s, wd, mode="nn", tm=th // 2, tn=d, tk=fh, res=h, norm_w=next_norm_w)
    return out, (hn, a, b, s)


def _ffn_bwd(dh, dhb, h, nw, wg, wu, wd, saved, tb, th, tag, plan):
    hn, a, b, s = saved
    d = h.shape[1]
    fh = wd.shape[0]
    layer = tag[-1]
    gr = plan.grads
    da, db = _ffn_ds(dhb, wd, a, b, tm=th // 2, tn=fh // 2, name=f"{tag}_b_ds")
    gr["down" + layer] = _matmul(s, dhb, mode="tn", tm=fh // 2, tn=d, tk=th, out_dtype=bf16, name=f"{tag}_b_dwd")
    dh2, dh2b, dnw = plan.call(f"{tag}_b_dhn", functools.partial(_dhn_norm_bwd, tm=th // 2, tk=fh // 2),
                               [(da, wg), (db, wu)], h, nw, dh, n_out=3)
    gr["gate" + layer] = _matmul(hn, da, mode="tn", tm=d, tn=fh // 2, tk=th, out_dtype=bf16, name=f"{tag}_b_dwg")
    gr["up" + layer] = _matmul(hn, db, mode="tn", tm=d, tn=fh // 2, tk=th, out_dtype=bf16, name=f"{tag}_b_dwu")
    return dh2, dh2b, dnw


class _Plan:
    GATHERS = {"ret_proj": ("ret_out", "gate0"), "ret_scan": ("up0", "down0"), "ffn0_gu": ("dn_in_top",),
               "ffn0_down": ("dn_in_bottom",), "dn_proj": ("dn_out",), "dn_conv": ("gate1",),
               "dn_prep": ("up1", "down1")}
    SCATTERS = {"ffn1_b_dhn": ("down1",), "dn_b_conv_a": ("gate1", "up1", "dn_out"), "ffn0_b_dhn": ("dn_in",),
                "ret_b_scan": ("gate0", "up0"), "ret_b_dwin": ("down0", "ret_out"), "ret_b_dhn": ("ret_in",)}

    def __init__(self, shards, wts):
        self.shards, self.wts, self.grads, self.parts = shards, wts, {}, {}

    def _exchange(self, stage):
        if self.shards is None:
            return None
        if stage in self.GATHERS:
            return _Exchange([self.shards[n] for n in self.GATHERS[stage]], True)
        if stage in self.SCATTERS:
            return _Exchange([self._dev_major(n) for n in self.SCATTERS[stage]], False)
        return None

    def _dev_major(self, name):
        g = self.grads
        if name[:-1] in ("gate", "up"):
            return _dev_major_cols(g[name], g[name].shape[1] // N_DEV)
        if name[:-1] == "down":
            dwd = g[name]
            return dwd.reshape(N_DEV, dwd.shape[0] // N_DEV, dwd.shape[1])
        if name in ("ret_out", "dn_out"):
            return g[name].reshape(N_DEV, g[name].shape[0] // N_DEV, g[name].shape[1])
        return _dev_major_cols(g[name], self.shards[name].shape[-1])

    def _landed(self, stage, outs):
        if stage in self.SCATTERS:
            self.parts.update(zip(self.SCATTERS[stage], outs))
            return
        w = self.wts
        cols = lambda t: t.transpose(1, 0, 2).reshape(t.shape[1], N_DEV * t.shape[2])
        rows = lambda t: t.reshape(N_DEV * t.shape[1], t.shape[2])
        for name, t in zip(self.GATHERS[stage], outs):
            if name in ("ret_out", "dn_out") or name.startswith("down"):
                w[name] = rows(t)
            else:
                w[name] = cols(t)
        if "dn_in_top" in w and "dn_in_bottom" in w and "dn_main" not in w:
            full = jnp.concatenate([w["dn_in_top"], w["dn_in_bottom"]], axis=0)
            n_main = DN_CONV_CH + DN_V
            w["dn_main"] = full[:, :n_main]
            w["dn_ba"] = jnp.pad(full[:, n_main:], ((0, 0), (0, BA_W - (full.shape[1] - n_main))))

    def matmul(self, stage, a, b, **kw):
        comm = self._exchange(stage)
        if comm is None:
            return _matmul(a, b, name=stage, **kw)
        out, landed = _matmul(a, b, name=stage, comm=comm, **kw)
        self._landed(stage, landed)
        return out

    def call(self, stage, fn, *args, n_out):
        comm = self._exchange(stage)
        out = fn(*args, stage, comm=comm)
        if comm is not None:
            self._landed(stage, out[n_out:])
        return out[:n_out]


def _local_step(x2, target, wts, shards=None):
    plan = _Plan(shards, wts)
    s_len, d = x2.shape
    l = s_len + CHUNK
    tb = _tile(l, 3072)
    th = tb // 2 if (tb // 2) % 16 == 0 else tb
    half = RET_DK // 2
    inv_freq = (np.float32(ROPE_BASE) ** (-np.arange(half, dtype=np.float32) / np.float32(half))).astype(np.float32)
    ang = (np.arange(l) - PAD).astype(np.float32)[:, None] * inv_freq[None, :]
    cos, sin = jnp.asarray(np.cos(ang), f32), jnp.asarray(np.sin(ang), f32)
    lgs = jnp.log1p(-jnp.exp2(-5.0 - jnp.arange(RET_HEADS, dtype=f32)))
    gcs = jnp.exp(lgs * _ret_block(l))

    h0 = jnp.concatenate([jnp.zeros((PAD, d), f32), wts["meta"], x2], axis=0)
    mixw, ffnw = wts["mix_norm"], wts["ffn_norm"]

    hn0 = _rms_fwd(h0, mixw[0:1], "l0_norm")
    proj0 = plan.matmul("ret_proj", hn0, wts["ret_in"], mode="nn", tm=tb, tn=512, tk=d)
    qk0 = _ret_prep(proj0, cos, sin, "ret_prep")
    o0, st0 = plan.call("ret_scan", _ret_scan_fwd, qk0, proj0, lgs, gcs, n_out=2)
    y0 = _gnorm_fwd(o0, proj0, wts["ret_gn"], RET_HEADS, RET_DV, 2, "ret_gnorm")
    h1, hn1 = _matmul(y0, wts["ret_out"], mode="nn", tm=th // 2, tn=d, tk=RET_V, res=h0, norm_w=ffnw[0:1],
                      name="ret_out")
    (h2, hn2), ffn0 = _ffn_fwd(h1, hn1, wts["gate0"], wts["up0"], wts["down0"], tb, th, "ffn0", plan,
                               next_norm_w=mixw[1:2])

    proj1 = plan.matmul("dn_proj", hn2, wts["dn_main"], mode="nn", tm=tb, tn=512, tk=d)
    ba = _matmul(hn2, wts["dn_ba"], mode="nn", tm=tb, tn=BA_W, tk=d, name="dn_proj_ba")
    (qkv1,) = plan.call("dn_conv", _dn_conv_fwd, proj1, wts["conv_w"], n_out=1)
    tinv1, u1, wq1, pk1, egl1, kpt1, qwt1 = plan.call("dn_prep", _dn_prep, qkv1, ba, wts["a_log"], wts["dt_bias"],
                                                      n_out=7)
    o1, st1, vn1 = _dn_scan_fwd(u1, wq1, pk1, egl1, "dn_scan")
    y1 = _gnorm_fwd(o1, proj1, wts["dn_norm"], DN_HEADS, DN_DV, 2, "dn_gnorm")
    h3, hn3 = _matmul(y1, wts["dn_out"], mode="nn", tm=th // 2, tn=d, tk=DN_V, res=h2, norm_w=ffnw[1:2],
                      name="dn_out")
    h4, ffn1 = _ffn_fwd(h3, hn3, wts["gate1"], wts["up1"], wts["down1"], tb, th, "ffn1", plan)

    dh4, dh4b, dfinal, loss = _final_loss(h4, wts["final_norm"], target, "final_loss")
    gr = plan.grads
    dh3, dh3b, dffn1 = _ffn_bwd(dh4, dh4b, h3, ffnw[1:2], wts["gate1"], wts["up1"], wts["down1"], ffn1,
                                tb, th, "ffn1", plan)

    gr["dn_out"] = _matmul(y1, dh3b, mode="tn", tm=1024, tn=d, tk=tb, out_dtype=bf16, name="dn_b_dwout")
    do1, dproj1, ddn_norm = _dy_gnorm_bwd(dh3b, wts["dn_out"], o1, proj1, wts["dn_norm"], DN_DV, 2, "dn_b_gnorm",
                                          tm=th // 2, tn=1024)
    dvn1, dsp1 = _dn_scan_bwd(do1, kpt1, qwt1, egl1, "dn_b_scan")
    dqkv1, dba, dalog, ddt = _dn_post_bwd(qkv1, ba, wts["a_log"], wts["dt_bias"], st1, dsp1, tinv1, u1, wq1, vn1,
                                          do1, dvn1, "dn_b_post")
    dc1, dconv = plan.call("dn_b_conv_a", _dn_conv_bwd_a, proj1, wts["conv_w"], dqkv1, n_out=2)
    dproj1 = _dn_conv_bwd_b(dc1, wts["conv_w"], dproj1, "dn_b_conv_b")
    dbab = dba.astype(bf16)
    n_main = dproj1.shape[1]
    dhn2_ba = _matmul(dbab, wts["dn_ba"], mode="nt", tm=th, tn=d, tk=BA_W, name="dn_b_dhn_ba")
    dh2, dh2b, dmix1 = plan.call("dn_b_dhn", functools.partial(_dhn_norm_bwd, tm=th // 2, tk=n_main // 4,
                                                              init=dhn2_ba),
                                 [(dproj1, wts["dn_main"])], h2, mixw[1:2], dh3, n_out=3)
    dw_main = _matmul(hn2, dproj1, mode="tn", tm=d, tn=512, tk=tb, out_dtype=bf16, name="dn_b_dwin")
    dw_ba = _matmul(hn2, dbab, mode="tn", tm=d, tn=BA_W, tk=tb, out_dtype=bf16, name="dn_b_dwin_ba")
    gr["dn_in"] = jnp.concatenate([dw_main, dw_ba], axis=1)

    dh1, dh1b, dffn0 = _ffn_bwd(dh2, dh2b, h1, ffnw[0:1], wts["gate0"], wts["up0"], wts["down0"], ffn0,
                                tb, th, "ffn0", plan)

    gr["ret_out"] = _matmul(y0, dh1b, mode="tn", tm=1024, tn=d, tk=tb, out_dtype=bf16, name="ret_b_dwout")
    do0, dproj0, dret_gn = _dy_gnorm_bwd(dh1b, wts["ret_out"], o0, proj0, wts["ret_gn"], RET_DV, 2, "ret_b_gnorm",
                                         tm=th // 2, tn=1024)
    dq0, dk0, dproj0 = plan.call("ret_b_scan", _ret_scan_bwd, qk0, proj0, st0, do0, dproj0, lgs, gcs, n_out=3)
    dproj0 = _ret_prep_bwd(dq0, dk0, cos, sin, dproj0, "ret_b_prep")
    n_in = dproj0.shape[1]
    gr["ret_in"] = plan.matmul("ret_b_dwin", hn0, dproj0, mode="tn", tm=d, tn=512, tk=tb, out_dtype=bf16)
    dh0, _, dmix0 = plan.call("ret_b_dhn", functools.partial(_dhn_norm_bwd, tm=th // 2, tk=n_in // 4),
                              [(dproj0, wts["ret_in"])], h0, mixw[0:1], dh1, n_out=3)

    gr.update(meta=dh0[PAD:CHUNK], mix_norm=jnp.concatenate([dmix0, dmix1], axis=0),
              ffn_norm=jnp.concatenate([dffn0, dffn1], axis=0), ret_gn=dret_gn, conv_w=dconv, a_log=dalog,
              dt_bias=ddt, dn_norm=ddn_norm, final_norm=dfinal)
    return loss, dh0[CHUNK:], gr, plan


def _adamw_reduce(parts, w, m, v, name):
    _, r, c = parts.shape
    c_pad = -(-c // LANES) * LANES
    tr = _div_tile(r, max(8, (3 * MIB // 16) // c_pad // 8 * 8), 16)

    def body(p_ref, w_ref, m_ref, v_ref, g_ref, d_ref, nm_ref, nv_ref):
        g = p_ref[0].astype(f32)
        for s in range(1, N_DEV):
            g = g + p_ref[s].astype(f32)
        mm = ADAM_B1 * m_ref[...] + (1.0 - ADAM_B1) * g
        vv = ADAM_B2 * v_ref[...] + (1.0 - ADAM_B2) * (g * g)
        m_hat = mm / (1.0 - ADAM_B1 ** ADAM_STEP)
        v_hat = vv / (1.0 - ADAM_B2 ** ADAM_STEP)
        g_ref[...] = g
        d_ref[...] = -ADAM_LR * (m_hat / (jnp.sqrt(v_hat) + ADAM_EPS) + ADAM_WD * w_ref[...])
        nm_ref[...] = mm
        nv_ref[...] = vv

    blk = pl.BlockSpec((tr, c), lambda i: (i, 0))
    return pl.pallas_call(
        body, name=name, grid=(r // tr,),
        in_specs=[pl.BlockSpec((N_DEV, tr, c), lambda i: (0, i, 0)), blk, blk, blk], out_specs=[blk] * 4,
        out_shape=[jax.ShapeDtypeStruct((r, c), f32)] * 4,
        compiler_params=_params(("parallel",), 48),
    )(parts, w, m, v)


def _dev_major_cols(g, width):
    r = g.shape[0]
    return g[:, :N_DEV * width].reshape(r, N_DEV, width).transpose(1, 0, 2)


def kernel(x, meta_tokens, mix_norm_w, ffn_norm_w, ret_w_in, ret_gn_w, ret_w_out, dn_w_in, dn_conv_w, dn_a_log, dn_dt_bias, dn_norm_w, dn_w_out, ffn_w_gate, ffn_w_up, ffn_w_down, final_norm_w, loss_target, m_meta_tokens, m_mix_norm_w, m_ffn_norm_w, m_ret_w_in, m_ret_gn_w, m_ret_w_out, m_dn_w_in, m_dn_conv_w, m_dn_a_log, m_dn_dt_bias, m_dn_norm_w, m_dn_w_out, m_ffn_w_gate, m_ffn_w_up, m_ffn_w_down, m_final_norm_w, v_meta_tokens, v_mix_norm_w, v_ffn_norm_w, v_ret_w_in, v_ret_gn_w, v_ret_w_out, v_dn_w_in, v_dn_conv_w, v_dn_a_log, v_dn_dt_bias, v_dn_norm_w, v_dn_w_out, v_ffn_w_gate, v_ffn_w_up, v_ffn_w_down, v_final_norm_w):
    d = x.shape[-1]
    me = 4 * lax.axis_index("x") + 2 * lax.axis_index("y") + lax.axis_index("c")

    shards = dict(ret_in=ret_w_in[0].astype(bf16), ret_out=ret_w_out[0].astype(bf16),
                  dn_in=dn_w_in[0].astype(bf16), dn_out=dn_w_out[0].astype(bf16))
    shards["dn_in_top"], shards["dn_in_bottom"] = shards["dn_in"][:d // 2], shards["dn_in"][d // 2:]
    for layer in (0, 1):
        shards[f"gate{layer}"] = ffn_w_gate[layer].astype(bf16)
        shards[f"up{layer}"] = ffn_w_up[layer].astype(bf16)
        shards[f"down{layer}"] = ffn_w_down[layer].astype(bf16)
    g_ret_in, g_meta, g_conv, g_dnn = _exchange([shards["ret_in"], meta_tokens, dn_conv_w[0], dn_norm_w], True,
                                                "gather_first")
    cols = lambda g: g.transpose(1, 0, 2).reshape(g.shape[1], N_DEV * g.shape[2])
    wts = dict(meta=cols(g_meta), mix_norm=mix_norm_w, ffn_norm=ffn_norm_w, ret_in=cols(g_ret_in), ret_gn=ret_gn_w,
               conv_w=cols(g_conv), a_log=dn_a_log, dt_bias=dn_dt_bias, dn_norm=cols(g_dnn),
               final_norm=final_norm_w.reshape(1, d))

    loss_part, grad_x, gr, plan = _local_step(x[0], loss_target[0], wts, shards)
    loss = lax.psum(loss_part[0, 0], AXES)

    pp = plan.parts
    both = lambda name: jnp.concatenate([pp[name + "0"], pp[name + "1"]], axis=1)
    big_parts = [pp["ret_in"], pp["ret_out"], pp["dn_in"], pp["dn_out"], both("gate"), both("up"), both("down")]
    big_names = ["ret_w_in", "ret_w_out", "dn_w_in", "dn_w_out", "ffn_w_gate", "ffn_w_up", "ffn_w_down"]
    big_w = [ret_w_in, ret_w_out, dn_w_in, dn_w_out, ffn_w_gate, ffn_w_up, ffn_w_down]
    big_m = [m_ret_w_in, m_ret_w_out, m_dn_w_in, m_dn_w_out, m_ffn_w_gate, m_ffn_w_up, m_ffn_w_down]
    big_v = [v_ret_w_in, v_ret_w_out, v_dn_w_in, v_dn_w_out, v_ffn_w_gate, v_ffn_w_up, v_ffn_w_down]
    res = {}
    for nm, parts, w_, m_, v_ in zip(big_names, big_parts, big_w, big_m, big_v):
        r2, c2 = parts.shape[1], parts.shape[2]
        outs = _adamw_reduce(parts, w_.reshape(r2, c2), m_.reshape(r2, c2), v_.reshape(r2, c2), f"adamw_{nm}")
        res[nm] = [o.reshape(w_.shape) for o in outs]

    small_names = ["meta_tokens", "mix_norm_w", "ffn_norm_w", "ret_gn_w", "dn_conv_w", "dn_a_log", "dn_dt_bias",
                   "dn_norm_w", "final_norm_w"]
    small_g = [gr["meta"], gr["mix_norm"], gr["ffn_norm"], gr["ret_gn"], gr["conv_w"], gr["a_log"], gr["dt_bias"],
               gr["dn_norm"], gr["final_norm"]]
    small_w = [meta_tokens, mix_norm_w, ffn_norm_w, ret_gn_w, dn_conv_w, dn_a_log, dn_dt_bias, dn_norm_w, final_norm_w]
    small_m = [m_meta_tokens, m_mix_norm_w, m_ffn_norm_w, m_ret_gn_w, m_dn_conv_w, m_dn_a_log, m_dn_dt_bias,
               m_dn_norm_w, m_final_norm_w]
    small_v = [v_meta_tokens, v_mix_norm_w, v_ffn_norm_w, v_ret_gn_w, v_dn_conv_w, v_dn_a_log, v_dn_dt_bias,
               v_dn_norm_w, v_final_norm_w]
    sharded = {"meta_tokens", "dn_conv_w", "dn_norm_w"}
    flat = jnp.concatenate([g.reshape(-1) for g in small_g])
    row = 8 * LANES
    n_flat = flat.shape[0]
    flat = jnp.pad(flat, (0, -n_flat % row)).reshape(-1, row)
    (gathered,) = _exchange([flat], True, "gather_small_grads")
    gathered = gathered.reshape(N_DEV, -1)
    pieces, off = [], 0
    for nm, g, w_ in zip(small_names, small_g, small_w):
        full = gathered[:, off:off + g.size].reshape((N_DEV,) + g.shape)
        off += g.size
        if nm in sharded:
            wloc = w_.shape[-1]
            full = lax.dynamic_slice_in_dim(full, me * wloc, wloc, axis=full.ndim - 1)
        pieces.append(full.reshape(N_DEV, -1))
    sizes = [p.shape[1] for p in pieces]
    n_loc = sum(sizes)
    pad_loc = -n_loc % row

    def pack(vs, lead):
        cat = jnp.concatenate([a.reshape(lead + (-1,)) for a in vs], axis=-1)
        cat = jnp.pad(cat, [(0, 0)] * len(lead) + [(0, pad_loc)])
        return cat.reshape(lead + (-1, row))

    outs = _adamw_reduce(pack(pieces, (N_DEV,)), pack(small_w, ()), pack(small_m, ()), pack(small_v, ()), "adamw_small")
    off = 0
    for nm, sz, w_ in zip(small_names, sizes, small_w):
        res[nm] = [o.reshape(-1)[off:off + sz].reshape(w_.shape) for o in outs]
        off += sz

    order = ["meta_tokens", "mix_norm_w", "ffn_norm_w", "ret_w_in", "ret_gn_w", "ret_w_out", "dn_w_in", "dn_conv_w",
             "dn_a_log", "dn_dt_bias", "dn_norm_w", "dn_w_out", "ffn_w_gate", "ffn_w_up", "ffn_w_down", "final_norm_w"]
    grad_x = grad_x.reshape(x.shape)
    return (loss, grad_x, *[res[nm][0] for nm in order], *[res[nm][1] for nm in order],
            *[res[nm][2] for nm in order], *[res[nm][3] for nm in order])
```

```python
import functools
import math

import jax
import jax.numpy as jnp
import numpy as np
from jax import lax
from jax.experimental import pallas as pl
from jax.experimental.pallas import tpu as pltpu

f32 = jnp.float32
bf16 = jnp.bfloat16

N_META = 16
CHUNK = 64
PAD = CHUNK - N_META
RMS_EPS = 1e-6
RET_HEADS, RET_DK, RET_DV = 4, 256, 512
RET_QK, RET_V = RET_HEADS * RET_DK, RET_HEADS * RET_DV
DN_HEADS, DN_DK, DN_DV = 8, 128, 256
DN_QK, DN_V = DN_HEADS * DN_DK, DN_HEADS * DN_DV
DN_CONV_CH = 2 * DN_QK + DN_V
CONV_K = 4
ROPE_BASE = 10000.0
ADAM_LR, ADAM_B1, ADAM_B2, ADAM_EPS, ADAM_WD, ADAM_STEP = 0.001, 0.9, 0.999, 1e-08, 0.01, 10
N_DEV = 8
AXES = ("x", "y", "c")
LANES = 128
MIB = 1024 * 1024


def _tile(n_rows, cap):
    nch = n_rows // CHUNK
    best = 1
    for d in range(1, nch + 1):
        if nch % d == 0 and d * CHUNK <= cap:
            best = d
    return best * CHUNK


def _div_tile(n, cap, align):
    best = None
    for d in range(align, min(n, cap) + 1, align):
        if n % d == 0:
            best = d
    return best if best is not None else n


def _params(sem, vmem_mb):
    return pltpu.CompilerParams(dimension_semantics=sem, vmem_limit_bytes=int(vmem_mb * MIB))


def _nn(a, b, precision=None):
    return jnp.dot(a, b, preferred_element_type=f32, precision=precision)


def _nt(a, b, precision=None):
    return lax.dot_general(a, b, (((1,), (1,)), ((), ())), preferred_element_type=f32, precision=precision)


def _tn(a, b, precision=None):
    return lax.dot_general(a, b, (((0,), (0,)), ((), ())), preferred_element_type=f32, precision=precision)


def _b(x):
    return x.astype(bf16)


def _sigmoid(x):
    return 0.5 * jnp.tanh(0.5 * x) + 0.5


def _silu(x):
    return x * _sigmoid(x)


def _dsilu(x):
    s = _sigmoid(x)
    return s * (1.0 + x * (1.0 - s))


def _peer(k):
    x, y, c = lax.axis_index("x"), lax.axis_index("y"), lax.axis_index("c")
    px = 1 - x if k & 4 else x
    py = 1 - y if k & 2 else y
    pc = 1 - c if k & 1 else c
    return (px, py, pc), 4 * px + 2 * py + pc


class _Exchange:
    def __init__(self, arrs, gather):
        self.arrs, self.gather, self.n = list(arrs), gather, len(arrs)
        self.out_shapes = [jax.ShapeDtypeStruct(((N_DEV,) + a.shape) if gather else a.shape, a.dtype) for a in arrs]
        self.specs = [pl.BlockSpec(memory_space=pltpu.HBM)] * self.n
        self.scratch = [pltpu.SemaphoreType.DMA((self.n, N_DEV - 1)), pltpu.SemaphoreType.DMA((self.n, N_DEV - 1)),
                        pltpu.SemaphoreType.DMA((self.n,))]

    def _copies(self, ins, outs, sems):
        send_sems, recv_sems, local_sems = sems
        me = 4 * lax.axis_index("x") + 2 * lax.axis_index("y") + lax.axis_index("c")
        src = (lambda a, dest: ins[a]) if self.gather else (lambda a, dest: ins[a].at[dest])
        local = [pltpu.make_async_copy(src(a, me), outs[a].at[me], local_sems.at[a]) for a in range(self.n)]
        sends, lands = [], []
        for k in range(1, N_DEV):
            peer, pidx = _peer(k)
            for a in range(self.n):
                for dst, lst in ((outs[a].at[me], sends), (outs[a].at[pidx], lands)):
                    lst.append(pltpu.make_async_remote_copy(
                        src_ref=src(a, pidx), dst_ref=dst, send_sem=send_sems.at[a, k - 1],
                        recv_sem=recv_sems.at[a, k - 1], device_id=peer, device_id_type=pl.DeviceIdType.MESH))
        return local, sends, lands

    def start(self, ins, outs, sems):
        local, sends, _ = self._copies(ins, outs, sems)
        for cp in local + sends:
            cp.start()

    def wait(self, ins, outs, sems):
        local, sends, lands = self._copies(ins, outs, sems)
        for cp in lands:
            cp.wait_recv()
        for cp in sends:
            cp.wait_send()
        for cp in local:
            cp.wait()


def _call(body, args, *, name, grid, in_specs, out_specs, out_shape, scratch=(), sem, vmem_mb, comm=None,
          aliases=None):
    aliases = aliases or {}
    if comm is None:
        out = pl.pallas_call(body, name=name, grid=grid, in_specs=list(in_specs), out_specs=list(out_specs),
                             out_shape=list(out_shape), scratch_shapes=list(scratch), input_output_aliases=aliases,
                             compiler_params=_params(sem, vmem_mb))(*args)
        return list(out)
    n_in, n_out, n_scr, nc = len(args), len(out_shape), len(scratch), comm.n

    def carried(*refs):
        ins, cin = refs[:n_in], refs[n_in:n_in + nc]
        o0 = n_in + nc
        outs, cout = refs[o0:o0 + n_out], refs[o0 + n_out:o0 + n_out + nc]
        s0 = o0 + n_out + nc
        scr, sems = refs[s0:s0 + n_scr], refs[s0 + n_scr:]
        first = functools.reduce(jnp.logical_and, [pl.program_id(i) == 0 for i in range(len(grid))])
        last = functools.reduce(jnp.logical_and, [pl.program_id(i) == grid[i] - 1 for i in range(len(grid))])

        @pl.when(first)
        def _():
            comm.start(cin, cout, sems)

        body(*ins, *outs, *scr)

        @pl.when(last)
        def _():
            comm.wait(cin, cout, sems)

    out = pl.pallas_call(
        carried, name=name, grid=grid, in_specs=list(in_specs) + comm.specs, out_specs=list(out_specs) + comm.specs,
        out_shape=list(out_shape) + comm.out_shapes, scratch_shapes=list(scratch) + comm.scratch,
        input_output_aliases=aliases,
        compiler_params=_params(("arbitrary",) * len(grid), vmem_mb))(*args, *comm.arrs)
    return list(out)


def _exchange(arrs, gather, name):
    comm = _Exchange(arrs, gather)

    def body(*refs):
        ins, outs, sems = refs[:comm.n], refs[comm.n:2 * comm.n], refs[2 * comm.n:]
        comm.start(ins, outs, sems)
        comm.wait(ins, outs, sems)

    return pl.pallas_call(body, name=name, in_specs=comm.specs, out_specs=comm.specs, out_shape=comm.out_shapes,
                          scratch_shapes=comm.scratch)(*comm.arrs)


def _matmul(a, b, *, mode, tm, tn, tk, name, out_dtype=f32, res=None, vmem_mb=48, comm=None, norm_w=None):
    if mode == "nn":
        (m, k), (k2, n) = a.shape, b.shape
    elif mode == "nt":
        (m, k), (n, k2) = a.shape, b.shape
    else:
        (k, m), (k2, n) = a.shape, b.shape
    assert k == k2 and m % tm == 0 and n % tn == 0 and k % tk == 0, (name, a.shape, b.shape, tm, tn, tk)
    nk = k // tk
    dot = {"nn": _nn, "nt": _nt, "tn": _tn}[mode]
    a_spec = {"nn": pl.BlockSpec((tm, tk), lambda i, j, kk: (i, kk)),
              "nt": pl.BlockSpec((tm, tk), lambda i, j, kk: (i, kk)),
              "tn": pl.BlockSpec((tk, tm), lambda i, j, kk: (kk, i))}[mode]
    b_spec = {"nn": pl.BlockSpec((tk, tn), lambda i, j, kk: (kk, j)),
              "nt": pl.BlockSpec((tn, tk), lambda i, j, kk: (j, kk)),
              "tn": pl.BlockSpec((tk, tn), lambda i, j, kk: (kk, j))}[mode]
    o_spec = pl.BlockSpec((tm, tn), lambda i, j, kk: (i, j))
    has_res = res is not None
    has_norm = norm_w is not None
    assert not has_norm or tn == n
    n_ops = 2 + has_res + has_norm

    def body(*refs):
        a_ref, b_ref = refs[:2]
        r_ref = refs[2] if has_res else None
        nw_ref = refs[2 + has_res] if has_norm else None
        o_ref = refs[n_ops]
        hn_ref = refs[n_ops + 1] if has_norm else None
        rest = refs[n_ops + 1 + has_norm:]

        def finish(tot):
            if has_res:
                tot = tot + r_ref[...]
            o_ref[...] = tot.astype(out_dtype)
            if has_norm:
                r = lax.rsqrt(jnp.mean(tot * tot, axis=-1, keepdims=True) + RMS_EPS)
                hn_ref[...] = _b(tot * r * nw_ref[...])

        if nk == 1:
            finish(dot(_b(a_ref[...]), _b(b_ref[...])))
            return
        acc_ref = rest[0]
        kk = pl.program_id(2)

        @pl.when(kk == 0)
        def _():
            acc_ref[...] = dot(_b(a_ref[...]), _b(b_ref[...]))

        @pl.when(kk > 0)
        def _():
            acc_ref[...] += dot(_b(a_ref[...]), _b(b_ref[...]))

        @pl.when(kk == nk - 1)
        def _():
            finish(acc_ref[...])

    in_specs = [a_spec, b_spec]
    args = (a, b)
    if has_res:
        in_specs.append(o_spec)
        args += (res,)
    out_specs, out_shape = [o_spec], [jax.ShapeDtypeStruct((m, n), out_dtype)]
    if has_norm:
        in_specs.append(pl.BlockSpec((1, tn), lambda i, j, kk: (0, j)))
        args += (norm_w,)
        out_specs.append(o_spec)
        out_shape.append(jax.ShapeDtypeStruct((m, n), bf16))
    out = _call(body, args, name=name, grid=(m // tm, n // tn, nk), in_specs=in_specs, out_specs=out_specs,
                out_shape=out_shape, scratch=[pltpu.VMEM((tm, tn), f32)] if nk > 1 else [],
                sem=("parallel", "parallel", "arbitrary"), vmem_mb=vmem_mb, comm=comm)
    n_own = len(out_shape)
    own = out[0] if n_own == 1 else tuple(out[:n_own])
    return own if comm is None else (own, out[n_own:])


def _rms_fwd(h, w, name):
    l, d = h.shape
    tr = _tile(l, 512)

    def body(h_ref, w_ref, o_ref):
        x = h_ref[...]
        r = lax.rsqrt(jnp.mean(x * x, axis=-1, keepdims=True) + RMS_EPS)
        o_ref[...] = _b(x * r * w_ref[...])

    return pl.pallas_call(
        body, name=name, grid=(l // tr,),
        in_specs=[pl.BlockSpec((tr, d), lambda i: (i, 0)), pl.BlockSpec((1, d), lambda i: (0, 0))],
        out_specs=pl.BlockSpec((tr, d), lambda i: (i, 0)),
        out_shape=jax.ShapeDtypeStruct((l, d), bf16),
        compiler_params=_params(("parallel",), 32),
    )(h, w)


def _dhn_norm_bwd(pairs, h, nw, dres, name, *, tm, tk, init=None, comm=None):
    l, d = h.shape
    nks = [a.shape[1] // tk for a, _ in pairs]
    starts = [sum(nks[:p]) for p in range(len(pairs))]
    nk = sum(nks)
    n_ops = 2 * len(pairs)
    has_init = init is not None

    def body(*refs):
        ops = refs[:n_ops]
        init_ref = refs[n_ops] if has_init else None
        h_ref, w_ref, r_ref, dh_ref, dhb_ref, dw_ref, acc_ref = refs[n_ops + has_init:]
        i, kk = pl.program_id(0), pl.program_id(1)

        @pl.when(kk == 0)
        def _():
            part = _nt(ops[0][...], ops[1][...])
            acc_ref[...] = part + init_ref[...] if has_init else part

        for p in range(len(pairs)):
            lo = max(starts[p], 1)

            @pl.when(jnp.logical_and(kk >= lo, kk < starts[p] + nks[p]))
            def _(a_ref=ops[2 * p], b_ref=ops[2 * p + 1]):
                acc_ref[...] += _nt(a_ref[...], b_ref[...])

        @pl.when(kk == nk - 1)
        def _():
            g = acc_ref[...]
            x = h_ref[...]
            r = lax.rsqrt(jnp.mean(x * x, axis=-1, keepdims=True) + RMS_EPS)
            xh = x * r
            dxh = g * w_ref[...]
            dh = r_ref[...] + r * (dxh - xh * jnp.mean(dxh * xh, axis=-1, keepdims=True))
            dh_ref[...] = dh
            dhb_ref[...] = _b(dh)
            dw = jnp.sum(g * xh, axis=0, keepdims=True)

            @pl.when(i == 0)
            def _():
                dw_ref[...] = dw

            @pl.when(i > 0)
            def _():
                dw_ref[...] += dw

    def k_of(p):
        return lambda kk: jnp.clip(kk - starts[p], 0, nks[p] - 1)

    in_specs, args = [], []
    for p, (a, b) in enumerate(pairs):
        in_specs += [pl.BlockSpec((tm, tk), functools.partial(lambda i, kk, f: (i, f(kk)), f=k_of(p))),
                     pl.BlockSpec((d, tk), functools.partial(lambda i, kk, f: (0, f(kk)), f=k_of(p)))]
        args += [a, b]
    row = pl.BlockSpec((tm, d), lambda i, kk: (i, 0))
    vec = pl.BlockSpec((1, d), lambda i, kk: (0, 0))
    if has_init:
        in_specs.append(row)
        args.append(init)
    return _call(body, tuple(args) + (h, nw, dres), name=name, grid=(l // tm, nk), in_specs=in_specs + [row, vec, row],
                 out_specs=[row, row, vec],
                 out_shape=[jax.ShapeDtypeStruct((l, d), f32), jax.ShapeDtypeStruct((l, d), bf16),
                            jax.ShapeDtypeStruct((1, d), f32)],
                 scratch=[pltpu.VMEM((tm, d), f32)], sem=("arbitrary", "arbitrary"), vmem_mb=48, comm=comm)


def _final_loss(h, w, target, name):
    l, d = h.shape
    nch = l // CHUNK
    cpt = _tile(l, 256) // CHUNK
    nt = nch // cpt

    def body(h_ref, w_ref, *rest):
        t_refs, (dh_ref, dhb_ref, dw_ref, loss_ref) = rest[:cpt], rest[cpt:]
        i = pl.program_id(0)
        wv = w_ref[...]
        dw = jnp.zeros((1, d), f32)
        part = jnp.zeros((1, 1), f32)
        for c in range(cpt):
            rows = slice(c * CHUNK, (c + 1) * CHUNK)
            live = (i * cpt + c > 0).astype(f32)
            x = h_ref[rows, :]
            r = lax.rsqrt(jnp.mean(x * x, axis=-1, keepdims=True) + RMS_EPS)
            xh = x * r
            err = (xh * wv - t_refs[c][...]) * live
            dy = err * (1.0 / d)
            dxh = dy * wv
            dx = r * (dxh - xh * jnp.mean(dxh * xh, axis=-1, keepdims=True))
            dh_ref[rows, :] = dx
            dhb_ref[rows, :] = _b(dx)
            dw = dw + jnp.sum(dy * xh, axis=0, keepdims=True)
            part = part + 0.5 * jnp.sum(jnp.sum(err * err, axis=-1, keepdims=True) * (1.0 / d), axis=0, keepdims=True)
        part = jnp.broadcast_to(part, (1, LANES))

        @pl.when(i == 0)
        def _():
            dw_ref[...] = dw
            loss_ref[...] = part

        @pl.when(i > 0)
        def _():
            dw_ref[...] += dw
            loss_ref[...] += part

    row = pl.BlockSpec((cpt * CHUNK, d), lambda i: (i, 0))
    vec = pl.BlockSpec((1, d), lambda i: (0, 0))
    t_specs = [pl.BlockSpec((CHUNK, d), functools.partial(lambda i, c: (jnp.maximum(i * cpt + c - 1, 0), 0), c=c))
               for c in range(cpt)]
    return pl.pallas_call(
        body, name=name, grid=(nt,),
        in_specs=[row, vec] + t_specs,
        out_specs=[row, row, vec, pl.BlockSpec((1, LANES), lambda i: (0, 0))],
        out_shape=[jax.ShapeDtypeStruct((l, d), f32), jax.ShapeDtypeStruct((l, d), bf16),
                   jax.ShapeDtypeStruct((1, d), f32), jax.ShapeDtypeStruct((1, LANES), f32)],
        compiler_params=_params(("arbitrary",), 32),
    )(h, w, *([target] * cpt))


def _ffn_gu(hn, wg, wu, name, *, tm, tn, comm=None):
    l, d = hn.shape
    fh = wg.shape[1]

    def body(h_ref, g_ref, u_ref, a_ref, b_ref, s_ref):
        hb = h_ref[...]
        a = _nn(hb, g_ref[...])
        bb = _nn(hb, u_ref[...])
        a_ref[...] = _b(a)
        b_ref[...] = _b(bb)
        s_ref[...] = _b(_silu(a) * bb)

    wspec = pl.BlockSpec((d, tn), lambda i, j: (0, j))
    ospec = pl.BlockSpec((tm, tn), lambda i, j: (i, j))
    return _call(body, (hn, wg, wu), name=name, grid=(l // tm, fh // tn),
                 in_specs=[pl.BlockSpec((tm, d), lambda i, j: (i, 0)), wspec, wspec], out_specs=[ospec] * 3,
                 out_shape=[jax.ShapeDtypeStruct((l, fh), bf16)] * 3, sem=("parallel", "parallel"), vmem_mb=48,
                 comm=comm)


def _ffn_ds(dhb, wd, a, b, *, tm, tn, name):
    l, d = dhb.shape
    fh = wd.shape[0]

    def body(g_ref, w_ref, a_ref, b_ref, da_ref, db_ref):
        ds = _nt(g_ref[...], w_ref[...])
        a = a_ref[...].astype(f32)
        da_ref[...] = _b(ds * b_ref[...].astype(f32) * _dsilu(a))
        db_ref[...] = _b(ds * _silu(a))

    ospec = pl.BlockSpec((tm, tn), lambda i, j: (i, j))
    return pl.pallas_call(
        body, name=name, grid=(l // tm, fh // tn),
        in_specs=[pl.BlockSpec((tm, d), lambda i, j: (i, 0)), pl.BlockSpec((tn, d), lambda i, j: (j, 0)), ospec, ospec],
        out_specs=[ospec, ospec], out_shape=[jax.ShapeDtypeStruct((l, fh), bf16)] * 2,
        compiler_params=_params(("parallel", "parallel"), 48),
    )(dhb, wd, a, b)


def _gnorm_fwd(o, proj, nw, heads, dv, gate_blk, name):
    l, hv = o.shape
    tr = _tile(l, 256)

    def body(o_ref, g_ref, w_ref, y_ref):
        wv = w_ref[...]
        for h in range(heads):
            sl = slice(h * dv, (h + 1) * dv)
            oh = o_ref[:, sl]
            r = lax.rsqrt(jnp.mean(oh * oh, axis=-1, keepdims=True) + RMS_EPS)
            y_ref[:, sl] = _b(oh * r * wv * _silu(g_ref[:, sl]))

    return pl.pallas_call(
        body, name=name, grid=(l // tr,),
        in_specs=[pl.BlockSpec((tr, hv), lambda i: (i, 0)), pl.BlockSpec((tr, hv), lambda i: (i, gate_blk)),
                  pl.BlockSpec((1, dv), lambda i: (0, 0))],
        out_specs=pl.BlockSpec((tr, hv), lambda i: (i, 0)),
        out_shape=jax.ShapeDtypeStruct((l, hv), bf16),
        compiler_params=_params(("parallel",), 32),
    )(o, proj, nw)


def _dy_gnorm_bwd(dhb, w_out, o, proj, nw, dv, gate_blk, name, *, tm, tn):
    l, hv = o.shape
    d = dhb.shape[1]
    nj = hv // tn
    heads = tn // dv

    def body(g_ref, w_ref, o_ref, gate_ref, nw_ref, do_ref, dg_ref, dw_ref):
        dy = _nt(g_ref[...], w_ref[...])
        wv = nw_ref[...]
        dw = jnp.zeros((1, dv), f32)
        for h in range(heads):
            sl = slice(h * dv, (h + 1) * dv)
            oh = o_ref[:, sl]
            g = gate_ref[:, sl]
            dyh = dy[:, sl]
            r = lax.rsqrt(jnp.mean(oh * oh, axis=-1, keepdims=True) + RMS_EPS)
            xh = oh * r
            dn = dyh * _silu(g)
            dg_ref[:, sl] = _b(dyh * (xh * wv) * _dsilu(g))
            dxh = dn * wv
            do_ref[:, sl] = r * (dxh - xh * jnp.mean(dxh * xh, axis=-1, keepdims=True))
            dw = dw + jnp.sum(dn * xh, axis=0, keepdims=True)
        first = jnp.logical_and(pl.program_id(0) == 0, pl.program_id(1) == 0)

        @pl.when(first)
        def _():
            dw_ref[...] = dw

        @pl.when(jnp.logical_not(first))
        def _():
            dw_ref[...] += dw

    tile = pl.BlockSpec((tm, tn), lambda i, j: (i, j))
    gate = pl.BlockSpec((tm, tn), lambda i, j: (i, gate_blk * nj + j))
    vec = pl.BlockSpec((1, dv), lambda i, j: (0, 0))
    return pl.pallas_call(
        body, name=name, grid=(l // tm, nj),
        in_specs=[pl.BlockSpec((tm, d), lambda i, j: (i, 0)), pl.BlockSpec((tn, d), lambda i, j: (j, 0)),
                  tile, gate, vec],
        out_specs=[tile, gate, vec],
        out_shape=[jax.ShapeDtypeStruct((l, hv), f32), jax.ShapeDtypeStruct(proj.shape, bf16),
                   jax.ShapeDtypeStruct((1, dv), f32)],
        compiler_params=_params(("arbitrary", "arbitrary"), 48),
    )(dhb, w_out, o, proj, nw)


def _ret_prep(proj, cos, sin, name, comm=None):
    l = proj.shape[0]
    tr = _tile(l, 256)
    half = RET_DK // 2
    scale = RET_DK ** -0.5

    def body(p_ref, c_ref, s_ref, o_ref):
        rows = pl.program_id(0) * tr + lax.broadcasted_iota(jnp.int32, (tr, 1), 0)
        kmul = jnp.where(rows >= PAD, scale, 0.0).astype(f32)
        c, s = c_ref[...], s_ref[...]
        for j in range(2 * RET_HEADS):
            t1 = p_ref[:, j * RET_DK: j * RET_DK + half]
            t2 = p_ref[:, j * RET_DK + half: (j + 1) * RET_DK]
            o1 = t1 * c - t2 * s
            o2 = t1 * s + t2 * c
            if j >= RET_HEADS:
                o1, o2 = o1 * kmul, o2 * kmul
            o_ref[:, j * RET_DK: j * RET_DK + half] = o1
            o_ref[:, j * RET_DK + half: (j + 1) * RET_DK] = o2

    wide = pl.BlockSpec((tr, 2 * RET_QK), lambda i: (i, 0))
    tab = pl.BlockSpec((tr, half), lambda i: (i, 0))
    return _call(body, (proj, cos, sin), name=name, grid=(l // tr,), in_specs=[wide, tab, tab], out_specs=[wide],
                 out_shape=[jax.ShapeDtypeStruct((l, 2 * RET_QK), f32)], sem=("parallel",), vmem_mb=32, comm=comm)


def _ret_prep_bwd(dq, dk, cos, sin, dproj, name):
    l = dq.shape[0]
    tr = _tile(l, 256)
    half = RET_DK // 2
    scale = RET_DK ** -0.5

    def body(dq_ref, dk_ref, c_ref, s_ref, _, o_ref):
        rows = pl.program_id(0) * tr + lax.broadcasted_iota(jnp.int32, (tr, 1), 0)
        kmul = jnp.where(rows >= PAD, scale, 0.0).astype(f32)
        c, s = c_ref[...], s_ref[...]
        for j in range(2 * RET_HEADS):
            d_ref = dq_ref if j < RET_HEADS else dk_ref
            jj = j % RET_HEADS
            d1 = d_ref[:, jj * RET_DK: jj * RET_DK + half]
            d2 = d_ref[:, jj * RET_DK + half: (jj + 1) * RET_DK]
            if j >= RET_HEADS:
                d1, d2 = d1 * kmul, d2 * kmul
            o_ref[:, j * RET_DK: j * RET_DK + half] = _b(d1 * c + d2 * s)
            o_ref[:, j * RET_DK + half: (j + 1) * RET_DK] = _b(d2 * c - d1 * s)

    nar = pl.BlockSpec((tr, RET_QK), lambda i: (i, 0))
    wide = pl.BlockSpec((tr, 2 * RET_QK), lambda i: (i, 0))
    tab = pl.BlockSpec((tr, half), lambda i: (i, 0))
    return pl.pallas_call(
        body, name=name, grid=(l // tr,), in_specs=[nar, nar, tab, tab, pl.BlockSpec(memory_space=pl.ANY)],
        out_specs=wide, out_shape=jax.ShapeDtypeStruct(dproj.shape, dproj.dtype), input_output_aliases={4: 0},
        compiler_params=_params(("parallel",), 32),
    )(dq, dk, cos, sin, dproj)


RET_BLOCK_CHUNKS = 3


def _ret_block(l):
    nch = l // CHUNK
    return RET_BLOCK_CHUNKS * CHUNK if nch % RET_BLOCK_CHUNKS == 0 else CHUNK


def _ret_decay(lg, rb):
    idx = lax.broadcasted_iota(jnp.int32, (rb, 1), 0).astype(f32)
    ri = lax.broadcasted_iota(jnp.int32, (rb, rb), 0)
    ci = lax.broadcasted_iota(jnp.int32, (rb, rb), 1)
    rel = (ri - ci).astype(f32)
    dmask = jnp.where(ri >= ci, jnp.exp(lg * jnp.maximum(rel, 0.0)), 0.0)
    xi = jnp.exp(lg * (idx + 1.0))
    zeta = jnp.exp(lg * (rb - 1.0 - idx))
    return dmask, xi, zeta


def _ret_scan_fwd(qk, proj, gn_w, lgs, gcs, name, comm=None):
    l = qk.shape[0]
    rb = _ret_block(l)
    nb = l // rb

    def body(lg_ref, gc_ref, q_ref, k_ref, v_ref, g_ref, nw_ref, o_ref, st_ref, y_ref, s_ref):
        @pl.when(pl.program_id(0) == 0)
        def _():
            s_ref[...] = jnp.zeros_like(s_ref)

        hs = range(RET_HEADS)
        dec = [_ret_decay(lg_ref[h], rb) for h in hs]
        q = [q_ref[:, h * RET_DK:(h + 1) * RET_DK] for h in hs]
        k = [k_ref[:, h * RET_DK:(h + 1) * RET_DK] for h in hs]
        vb = [_b(v_ref[:, h * RET_DV:(h + 1) * RET_DV]) for h in hs]
        s = [s_ref[h] for h in hs]
        sb = [_b(s[h]) for h in hs]
        scores = [_b(_nt(_b(q[h]), _b(k[h])) * dec[h][0]) for h in hs]
        inter = [_nn(_b(q[h] * dec[h][1]), sb[h]) for h in hs]
        kv = [_tn(_b(k[h] * dec[h][2]), vb[h]) for h in hs]
        nw = nw_ref[...]
        for h in hs:
            cols = slice(h * RET_DV, (h + 1) * RET_DV)
            st_ref[0, h] = sb[h]
            o = _nn(scores[h], vb[h]) + inter[h]
            o_ref[:, cols] = o
            r = lax.rsqrt(jnp.mean(o * o, axis=-1, keepdims=True) + RMS_EPS)
            y_ref[:, cols] = _b(o * r * nw * _silu(g_ref[:, cols]))
            s_ref[h] = gc_ref[h] * s[h] + kv[h]

    smem = pl.BlockSpec(memory_space=pltpu.SMEM)
    wide = pl.BlockSpec((rb, RET_V), lambda n: (n, 0))
    return _call(
        body, (lgs, gcs, qk, qk, proj, proj, gn_w), name=name, grid=(nb,),
        in_specs=[smem, smem,
                  pl.BlockSpec((rb, RET_QK), lambda n: (n, 0)),
                  pl.BlockSpec((rb, RET_QK), lambda n: (n, 1)),
                  pl.BlockSpec((rb, RET_V), lambda n: (n, 1)),
                  pl.BlockSpec((rb, RET_V), lambda n: (n, 2)),
                  pl.BlockSpec((1, RET_DV), lambda n: (0, 0))],
        out_specs=[wide, pl.BlockSpec((1, RET_HEADS, RET_DK, RET_DV), lambda n: (n, 0, 0, 0)), wide],
        out_shape=[jax.ShapeDtypeStruct((l, RET_V), f32),
                   jax.ShapeDtypeStruct((nb, RET_HEADS, RET_DK, RET_DV), bf16),
                   jax.ShapeDtypeStruct((l, RET_V), bf16)],
        scratch=[pltpu.VMEM((RET_HEADS, RET_DK, RET_DV), f32)], sem=("arbitrary",), vmem_mb=40, comm=comm)


def _ret_scan_bwd(qk, proj, states, do, dproj, lgs, gcs, name, comm=None):
    l = qk.shape[0]
    rb = _ret_block(l)
    nb = l // rb

    def body(lg_ref, gc_ref, q_ref, k_ref, v_ref, st_ref, do_ref, _, dq_ref, dk_ref, dv_ref, ds_ref):
        @pl.when(pl.program_id(0) == 0)
        def _():
            ds_ref[...] = jnp.zeros_like(ds_ref)

        hs = range(RET_HEADS)
        dec = [_ret_decay(lg_ref[h], rb) for h in hs]
        q = [q_ref[:, h * RET_DK:(h + 1) * RET_DK] for h in hs]
        k = [k_ref[:, h * RET_DK:(h + 1) * RET_DK] for h in hs]
        qb, kb = [_b(t) for t in q], [_b(t) for t in k]
        vb = [_b(v_ref[:, h * RET_DV:(h + 1) * RET_DV]) for h in hs]
        dob = [_b(do_ref[:, h * RET_DV:(h + 1) * RET_DV]) for h in hs]
        dsp = [ds_ref[h] for h in hs]
        dspb = [_b(t) for t in dsp]
        scores = [_b(_nt(qb[h], kb[h]) * dec[h][0]) for h in hs]
        dscores = [_b(_nt(dob[h], vb[h]) * dec[h][0]) for h in hs]
        for h in hs:
            dq_ref[:, h * RET_DK:(h + 1) * RET_DK] = _nn(dscores[h], kb[h]) + _nt(dob[h], st_ref[0, h]) * dec[h][1]
        for h in hs:
            dk_ref[:, h * RET_DK:(h + 1) * RET_DK] = _tn(dscores[h], qb[h]) + _nt(vb[h], dspb[h]) * dec[h][2]
        for h in hs:
            dv_ref[:, h * RET_DV:(h + 1) * RET_DV] = _b(_tn(scores[h], dob[h]) + _nn(_b(k[h] * dec[h][2]), dspb[h]))
        for h in hs:
            ds_ref[h] = gc_ref[h] * dsp[h] + _tn(_b(q[h] * dec[h][1]), dob[h])

    smem = pl.BlockSpec(memory_space=pltpu.SMEM)
    rev = lambda n: nb - 1 - n
    return _call(
        body, (lgs, gcs, qk, qk, proj, states, do, dproj), name=name, grid=(nb,),
        in_specs=[smem, smem,
                  pl.BlockSpec((rb, RET_QK), lambda n: (rev(n), 0)),
                  pl.BlockSpec((rb, RET_QK), lambda n: (rev(n), 1)),
                  pl.BlockSpec((rb, RET_V), lambda n: (rev(n), 1)),
                  pl.BlockSpec((1, RET_HEADS, RET_DK, RET_DV), lambda n: (rev(n), 0, 0, 0)),
                  pl.BlockSpec((rb, RET_V), lambda n: (rev(n), 0)),
                  pl.BlockSpec(memory_space=pl.ANY)],
        out_specs=[pl.BlockSpec((rb, RET_QK), lambda n: (rev(n), 0)),
                   pl.BlockSpec((rb, RET_QK), lambda n: (rev(n), 0)),
                   pl.BlockSpec((rb, RET_V), lambda n: (rev(n), 1))],
        out_shape=[jax.ShapeDtypeStruct((l, RET_QK), f32), jax.ShapeDtypeStruct((l, RET_QK), f32),
                   jax.ShapeDtypeStruct(dproj.shape, dproj.dtype)],
        scratch=[pltpu.VMEM((RET_HEADS, RET_DK, RET_DV), f32)], sem=("arbitrary",), vmem_mb=40, comm=comm,
        aliases={7: 2})


CONV_BLK = 1024
HALO = 8


def _slab_rows(r):
    return pl.ds(pl.multiple_of(r * HALO, HALO), HALO)


def _conv_slab(x_ref, p_ref, r, i, tr):
    cur = x_ref[_slab_rows(r), :]
    prev = jnp.where(r > 0, x_ref[_slab_rows(jnp.maximum(r - 1, 0)), :], p_ref[...])
    row0 = i * tr + r * HALO
    cur = jnp.where(row0 >= PAD, cur, 0.0)
    prev = jnp.where(row0 - HALO >= PAD, prev, 0.0)
    lrow = lax.broadcasted_iota(jnp.int32, (HALO, 1), 0)
    shifted = [jnp.where(lrow < s, pltpu.roll(prev, s, 0), pltpu.roll(cur, s, 0)) for s in range(1, CONV_K)]
    return [cur] + shifted


def _conv_of(xs, w):
    acc = xs[0] * w[CONV_K - 1:CONV_K, :]
    for s in range(1, CONV_K):
        acc = acc + xs[s] * w[CONV_K - 1 - s:CONV_K - s, :]
    return acc


def _slab_loop(n_slabs, fn, init=None):
    return lax.fori_loop(0, n_slabs, fn, init, unroll=8)


def _dn_conv_fwd(proj, conv_w, name, comm=None):
    l = proj.shape[0]
    tr = _tile(l, 256)
    nblk = DN_CONV_CH // CONV_BLK
    heads = CONV_BLK // DN_DK

    def body(x_ref, p_ref, w_ref, o_ref):
        i, j = pl.program_id(0), pl.program_id(1)
        w = w_ref[...]

        def act(r):
            return _silu(_conv_of(_conv_slab(x_ref, p_ref, r, i, tr), w))

        def normed(scale):
            def slab(r, carry):
                a = act(r)
                outs = []
                for h in range(heads):
                    ah = a[:, h * DN_DK:(h + 1) * DN_DK]
                    outs.append(ah * (lax.rsqrt(jnp.sum(ah * ah, axis=-1, keepdims=True) + RMS_EPS) * scale))
                o_ref[_slab_rows(r), :] = jnp.concatenate(outs, axis=1)
                return carry
            return slab

        def plain(r, carry):
            o_ref[_slab_rows(r), :] = act(r)
            return carry

        @pl.when(j == 0)
        def _():
            _slab_loop(tr // HALO, normed(DN_DK ** -0.5))

        @pl.when(j == 1)
        def _():
            _slab_loop(tr // HALO, normed(1.0))

        @pl.when(j >= 2)
        def _():
            _slab_loop(tr // HALO, plain)

    hb = tr // HALO
    return _call(
        body, (proj, proj, conv_w), name=name, grid=(l // tr, nblk),
        in_specs=[pl.BlockSpec((tr, CONV_BLK), lambda i, j: (i, j)),
                  pl.BlockSpec((HALO, CONV_BLK), lambda i, j: (jnp.maximum(i * hb - 1, 0), j)),
                  pl.BlockSpec((CONV_K, CONV_BLK), lambda i, j: (0, j))],
        out_specs=[pl.BlockSpec((tr, CONV_BLK), lambda i, j: (i, j))],
        out_shape=[jax.ShapeDtypeStruct((l, DN_CONV_CH), f32)],
        scratch=[], sem=("parallel", "parallel"), vmem_mb=32, comm=comm)


def _dn_conv_bwd_a(proj, conv_w, dqkv, name, comm=None):
    l = proj.shape[0]
    tr = _tile(l, 256)
    nblk = DN_CONV_CH // CONV_BLK
    heads = CONV_BLK // DN_DK

    def body(x_ref, p_ref, w_ref, d_ref, dc_ref, dw_ref, acc_ref):
        j, i = pl.program_id(0), pl.program_id(1)
        w = w_ref[...]
        acc_ref[...] = jnp.zeros_like(acc_ref)

        def slab_of(l2_scale):
            def slab(r, carry):
                xs = _conv_slab(x_ref, p_ref, r, i, tr)
                c = _conv_of(xs, w)
                a = _silu(c)
                dy = d_ref[_slab_rows(r), :]
                if l2_scale is None:
                    da = dy
                else:
                    parts = []
                    for h in range(heads):
                        sl = slice(h * DN_DK, (h + 1) * DN_DK)
                        ah, dyh = a[:, sl], dy[:, sl]
                        rn = lax.rsqrt(jnp.sum(ah * ah, axis=-1, keepdims=True) + RMS_EPS)
                        yh = ah * rn
                        parts.append((rn * l2_scale) * (dyh - yh * jnp.sum(dyh * yh, axis=-1, keepdims=True)))
                    da = jnp.concatenate(parts, axis=1)
                dc = da * _dsilu(c)
                dc_ref[_slab_rows(r), :] = dc
                for s in range(CONV_K):
                    acc_ref[CONV_K - 1 - s] += dc * xs[s]
                return carry
            return slab

        @pl.when(j == 0)
        def _():
            _slab_loop(tr // HALO, slab_of(DN_DK ** -0.5))

        @pl.when(j == 1)
        def _():
            _slab_loop(tr // HALO, slab_of(1.0))

        @pl.when(j >= 2)
        def _():
            _slab_loop(tr // HALO, slab_of(None))

        ksel = lax.broadcasted_iota(jnp.int32, (CONV_K, 1), 0)
        dw = jnp.zeros((CONV_K, CONV_BLK), f32)
        for k in range(CONV_K):
            dw = dw + jnp.where(ksel == k, jnp.sum(acc_ref[k], axis=0, keepdims=True), 0.0)

        @pl.when(i == 0)
        def _():
            dw_ref[...] = dw

        @pl.when(i > 0)
        def _():
            dw_ref[...] += dw

    hb = tr // HALO
    blk = pl.BlockSpec((tr, CONV_BLK), lambda j, i: (i, j))
    return _call(
        body, (proj, proj, conv_w, dqkv), name=name, grid=(nblk, l // tr),
        in_specs=[blk, pl.BlockSpec((HALO, CONV_BLK), lambda j, i: (jnp.maximum(i * hb - 1, 0), j)),
                  pl.BlockSpec((CONV_K, CONV_BLK), lambda j, i: (0, j)), blk],
        out_specs=[blk, pl.BlockSpec((CONV_K, CONV_BLK), lambda j, i: (0, j))],
        out_shape=[jax.ShapeDtypeStruct((l, DN_CONV_CH), f32), jax.ShapeDtypeStruct((CONV_K, DN_CONV_CH), f32)],
        scratch=[pltpu.VMEM((CONV_K, HALO, CONV_BLK), f32)], sem=("parallel", "arbitrary"), vmem_mb=40, comm=comm)


def _dn_conv_bwd_b(dc, conv_w, dproj, name):
    l = dc.shape[0]
    tr = _tile(l, 256)
    nblk = DN_CONV_CH // CONV_BLK
    nrow = l // tr

    n_slabs = tr // HALO
    pair = 2 * HALO

    def body(d_ref, n_ref, w_ref, _, o_ref):
        i = pl.program_id(0)
        w = w_ref[...]
        nxt_tile = jnp.where(i < nrow - 1, n_ref[...], 0.0)
        lrow = lax.broadcasted_iota(jnp.int32, (HALO, 1), 0)

        def one(r):
            cur = d_ref[_slab_rows(r), :]
            nxt = jnp.where(r < n_slabs - 1, d_ref[_slab_rows(jnp.minimum(r + 1, n_slabs - 1)), :], nxt_tile)
            acc = cur * w[CONV_K - 1:CONV_K, :]
            for s in range(1, CONV_K):
                up = jnp.where(lrow >= HALO - s, pltpu.roll(nxt, HALO - s, 0), pltpu.roll(cur, HALO - s, 0))
                acc = acc + up * w[CONV_K - 1 - s:CONV_K - s, :]
            return jnp.where(i * tr + r * HALO >= PAD, acc, 0.0)

        def two(q, carry):
            rows = pl.ds(pl.multiple_of(q * pair, pair), pair)
            o_ref[rows, :] = _b(jnp.concatenate([one(2 * q), one(2 * q + 1)], axis=0))
            return carry

        lax.fori_loop(0, n_slabs // 2, two, None, unroll=4)

    hb = tr // HALO
    nh = l // HALO
    return pl.pallas_call(
        body, name=name, grid=(nrow, nblk),
        in_specs=[pl.BlockSpec((tr, CONV_BLK), lambda i, j: (i, j)),
                  pl.BlockSpec((HALO, CONV_BLK), lambda i, j: (jnp.minimum((i + 1) * hb, nh - 1), j)),
                  pl.BlockSpec((CONV_K, CONV_BLK), lambda i, j: (0, j)),
                  pl.BlockSpec(memory_space=pl.ANY)],
        out_specs=pl.BlockSpec((tr, CONV_BLK), lambda i, j: (i, j)),
        out_shape=jax.ShapeDtypeStruct(dproj.shape, dproj.dtype), input_output_aliases={3: 0},
        compiler_params=_params(("parallel", "parallel"), 32),
    )(dc, dc, conv_w, dproj)


BA_W = LANES


def _dn_gates(ba_ref, al_ref, dt_ref, n):
    rows = n * CHUNK + lax.broadcasted_iota(jnp.int32, (CHUNK, 1), 0)
    vm = (rows >= PAD).astype(f32)
    bin_ = ba_ref[:, 0:DN_HEADS]
    z = ba_ref[:, DN_HEADS:2 * DN_HEADS] + dt_ref[...]
    sp = jnp.maximum(z, 0.0) + jnp.log1p(jnp.exp(-jnp.abs(z)))
    ea = jnp.exp(al_ref[...])
    beta = _sigmoid(bin_) * vm
    g = -ea * sp * vm
    return vm, bin_, z, ea, beta, g


def _tri():
    ri = lax.broadcasted_iota(jnp.int32, (CHUNK, CHUNK), 0)
    ci = lax.broadcasted_iota(jnp.int32, (CHUNK, CHUNK), 1)
    return ri, ci


def _split(a):
    hi = _b(a)
    return hi, _b(a - hi.astype(f32))


def _mm3(a, b, dot=_nn):
    (ah, al), (bh, bl) = _split(a), _split(b)
    return dot(ah, bh) + (dot(ah, bl) + dot(al, bh))


def _cumsum_rows(tri, g):
    tb = _b(tri)
    g1 = _b(g)
    r1 = g - g1.astype(f32)
    g2 = _b(r1)
    g3 = _b(r1 - g2.astype(f32))
    return _nn(tb, g1) + (_nn(tb, g2) + _nn(tb, g3))


DN_SCAN_CHUNKS = 3


def _scan_chunks(nch):
    return DN_SCAN_CHUNKS if nch % DN_SCAN_CHUNKS == 0 else 1


def _dn_prep(qkv, ba, a_log, dt_bias, name, comm=None):
    l = qkv.shape[0]
    nch = l // CHUNK
    heads = range(DN_HEADS)

    cb = _scan_chunks(nch)
    items = [(c, h) for c in range(cb) for h in heads]

    def body(q_ref, k_ref, v_ref, ba_ref, al_ref, dt_ref, t_ref, u_ref, wq_ref, pk_ref, eg_ref, kpt_ref, qwt_ref):
        n0 = pl.program_id(0) * cb
        ri, ci = _tri()
        incl, strict = ri >= ci, ri > ci
        eye = (ri == ci).astype(f32)
        rows = [slice(c * CHUNK, (c + 1) * CHUNK) for c in range(cb)]
        gam, gam_t, beta = [], [], []
        for c in range(cb):
            _, _, _, _, beta_c, g_c = _dn_gates(ba_ref[rows[c], :], al_ref, dt_ref, n0 + c)
            gam.append(_cumsum_rows(incl.astype(f32), g_c))
            gam_t.append(gam[c].T)
            beta.append(beta_c)
        gc = {(c, h): gam[c][:, h:h + 1] for c, h in items}
        bh = {(c, h): beta[c][:, h:h + 1] for c, h in items}
        kh = {(c, h): k_ref[rows[c], h * DN_DK:(h + 1) * DN_DK] for c, h in items}
        kb = {i: _b(kh[i]) for i in items}
        decay = {(c, h): jnp.exp(jnp.where(incl, gc[c, h] - gam_t[c][h:h + 1, :], -jnp.inf)) for c, h in items}
        a = {i: jnp.where(strict, bh[i] * _nt(kb[i], kb[i]) * decay[i], 0.0) for i in items}
        t = {i: eye - a[i] for i in items}
        p = a
        for level in range(int(math.log2(CHUNK)) - 1):
            mm = _mm3 if level < 2 else (lambda x, y: _nn(_b(x), _b(y)))
            p = {i: mm(p[i], p[i]) for i in items}
            t = {i: t[i] + mm(t[i], p[i]) for i in items}
        eg = {i: jnp.exp(gc[i]) for i in items}
        for c, h in items:
            i = (c, h)
            t_ref[c, h] = t[i]
            u_ref[rows[c], h * DN_DV:(h + 1) * DN_DV] = _mm3(t[i], v_ref[rows[c], h * DN_DV:(h + 1) * DN_DV] * bh[i])
            w = _mm3(t[i], kh[i] * (bh[i] * eg[i]))
            wq_ref[c, h, 0:CHUNK, :] = _b(w)
            qwt_ref[c, h, DN_DK:2 * DN_DK, :] = _b(w.T)
        for c, h in items:
            i = (c, h)
            qh = q_ref[rows[c], h * DN_DK:(h + 1) * DN_DK]
            gl = gc[i][CHUNK - 1:CHUNK, :]
            qe = qh * eg[i]
            ke = kh[i] * jnp.exp(gl - gc[i])
            pmat = _nt(_b(qh), kb[i]) * decay[i]
            wq_ref[c, h, CHUNK:2 * CHUNK, :] = _b(qe)
            qwt_ref[c, h, 0:DN_DK, :] = _b(qe.T)
            pk_ref[c, h, 0:CHUNK, :] = _b(pmat)
            pk_ref[c, h, CHUNK:CHUNK + DN_DK, :] = _b(ke.T)
            kpt_ref[c, h, :, 0:DN_DK] = _b(ke)
            kpt_ref[c, h, :, DN_DK:DN_DK + CHUNK] = _b(pmat.T)
            eg_ref[c, h] = jnp.broadcast_to(jnp.exp(gl), (8, LANES))

    vec = pl.BlockSpec((1, DN_HEADS), lambda n: (0, 0))
    return _call(
        body, (qkv, qkv, qkv, ba, a_log, dt_bias), name=name, grid=(nch // cb,),
        in_specs=[pl.BlockSpec((cb * CHUNK, DN_QK), lambda n: (n, 0)),
                  pl.BlockSpec((cb * CHUNK, DN_QK), lambda n: (n, 1)),
                  pl.BlockSpec((cb * CHUNK, DN_V), lambda n: (n, 1)),
                  pl.BlockSpec((cb * CHUNK, BA_W), lambda n: (n, 0)), vec, vec],
        out_specs=[pl.BlockSpec((cb, DN_HEADS, CHUNK, CHUNK), lambda n: (n, 0, 0, 0)),
                   pl.BlockSpec((cb * CHUNK, DN_V), lambda n: (n, 0)),
                   pl.BlockSpec((cb, DN_HEADS, 2 * CHUNK, DN_DK), lambda n: (n, 0, 0, 0)),
                   pl.BlockSpec((cb, DN_HEADS, CHUNK + DN_DK, CHUNK), lambda n: (n, 0, 0, 0)),
                   pl.BlockSpec((cb, DN_HEADS, 8, LANES), lambda n: (n, 0, 0, 0)),
                   pl.BlockSpec((cb, DN_HEADS, CHUNK, DN_DK + CHUNK), lambda n: (n, 0, 0, 0)),
                   pl.BlockSpec((cb, DN_HEADS, 2 * DN_DK, CHUNK), lambda n: (n, 0, 0, 0))],
        out_shape=[jax.ShapeDtypeStruct((nch, DN_HEADS, CHUNK, CHUNK), f32),
                   jax.ShapeDtypeStruct((l, DN_V), f32),
                   jax.ShapeDtypeStruct((nch, DN_HEADS, 2 * CHUNK, DN_DK), bf16),
                   jax.ShapeDtypeStruct((nch, DN_HEADS, CHUNK + DN_DK, CHUNK), bf16),
                   jax.ShapeDtypeStruct((nch, DN_HEADS, 8, LANES), f32),
                   jax.ShapeDtypeStruct((nch, DN_HEADS, CHUNK, DN_DK + CHUNK), bf16),
                   jax.ShapeDtypeStruct((nch, DN_HEADS, 2 * DN_DK, CHUNK), bf16)],
        sem=("parallel",), vmem_mb=40, comm=comm)


def _dn_scan_fwd(u, wq, pk, egl, name):
    l = u.shape[0]
    nch = l // CHUNK
    cs = _scan_chunks(nch)

    def body(u_ref, wq_ref, pk_ref, eg_ref, o_ref, st_ref, vn_ref, s_ref):
        @pl.when(pl.program_id(0) == 0)
        def _():
            s_ref[...] = jnp.zeros_like(s_ref)

        hs = range(DN_HEADS)
        cols = [slice(h * DN_DV, (h + 1) * DN_DV) for h in hs]
        s = [s_ref[h] for h in hs]
        for c in range(cs):
            rows = slice(c * CHUNK, (c + 1) * CHUNK)
            sb = [_b(s[h]) for h in hs]
            x = [_nn(wq_ref[c, h], sb[h]) for h in hs]
            vnb = [_b(u_ref[rows, cols[h]] - x[h][0:CHUNK]) for h in hs]
            y = [_nn(pk_ref[c, h], vnb[h]) for h in hs]
            for h in hs:
                st_ref[c, h] = sb[h]
                vn_ref[rows, cols[h]] = vnb[h]
                o_ref[rows, cols[h]] = x[h][CHUNK:2 * CHUNK] + y[h][0:CHUNK]
            s = [eg_ref[c, h][0:1, 0:1] * s[h] + y[h][CHUNK:CHUNK + DN_DK] for h in hs]
        for h in hs:
            s_ref[h] = s[h]

    return pl.pallas_call(
        body, name=name, grid=(nch // cs,),
        in_specs=[pl.BlockSpec((cs * CHUNK, DN_V), lambda n: (n, 0)),
                  pl.BlockSpec((cs, DN_HEADS, 2 * CHUNK, DN_DK), lambda n: (n, 0, 0, 0)),
                  pl.BlockSpec((cs, DN_HEADS, CHUNK + DN_DK, CHUNK), lambda n: (n, 0, 0, 0)),
                  pl.BlockSpec((cs, DN_HEADS, 8, LANES), lambda n: (n, 0, 0, 0))],
        out_specs=[pl.BlockSpec((cs * CHUNK, DN_V), lambda n: (n, 0)),
                   pl.BlockSpec((cs, DN_HEADS, DN_DK, DN_DV), lambda n: (n, 0, 0, 0)),
                   pl.BlockSpec((cs * CHUNK, DN_V), lambda n: (n, 0))],
        out_shape=[jax.ShapeDtypeStruct((l, DN_V), f32),
                   jax.ShapeDtypeStruct((nch, DN_HEADS, DN_DK, DN_DV), bf16),
                   jax.ShapeDtypeStruct((l, DN_V), bf16)],
        scratch_shapes=[pltpu.VMEM((DN_HEADS, DN_DK, DN_DV), f32)],
        compiler_params=_params(("arbitrary",), 40),
    )(u, wq, pk, egl)


def _dn_scan_bwd(do, kpt, qwt, egl, name):
    l = do.shape[0]
    nch = l // CHUNK
    cs = _scan_chunks(nch)
    nblk = nch // cs

    def body(do_ref, kpt_ref, qwt_ref, eg_ref, dvn_ref, dsp_ref, ds_ref):
        @pl.when(pl.program_id(0) == 0)
        def _():
            ds_ref[...] = jnp.zeros_like(ds_ref)

        hs = range(DN_HEADS)
        cols = [slice(h * DN_DV, (h + 1) * DN_DV) for h in hs]
        ds = [ds_ref[h] for h in hs]
        for c in reversed(range(cs)):
            rows = slice(c * CHUNK, (c + 1) * CHUNK)
            dspb = [_b(ds[h]) for h in hs]
            dob = [_b(do_ref[rows, cols[h]]) for h in hs]
            dvn = [_nn(kpt_ref[c, h][:, 0:DN_DK], dspb[h]) + _nn(kpt_ref[c, h][:, DN_DK:DN_DK + CHUNK], dob[h])
                   for h in hs]
            for h in hs:
                dsp_ref[c, h] = dspb[h]
                dvn_ref[rows, cols[h]] = dvn[h]
            ds = [eg_ref[c, h][0:1, 0:1] * ds[h] + _nn(qwt_ref[c, h][0:DN_DK], dob[h])
                  - _nn(qwt_ref[c, h][DN_DK:2 * DN_DK], _b(dvn[h])) for h in hs]
        for h in hs:
            ds_ref[h] = ds[h]

    rev = lambda s: nblk - 1 - s
    return pl.pallas_call(
        body, name=name, grid=(nblk,),
        in_specs=[pl.BlockSpec((cs * CHUNK, DN_V), lambda s: (rev(s), 0)),
                  pl.BlockSpec((cs, DN_HEADS, CHUNK, DN_DK + CHUNK), lambda s: (rev(s), 0, 0, 0)),
                  pl.BlockSpec((cs, DN_HEADS, 2 * DN_DK, CHUNK), lambda s: (rev(s), 0, 0, 0)),
                  pl.BlockSpec((cs, DN_HEADS, 8, LANES), lambda s: (rev(s), 0, 0, 0))],
        out_specs=[pl.BlockSpec((cs * CHUNK, DN_V), lambda s: (rev(s), 0)),
                   pl.BlockSpec((cs, DN_HEADS, DN_DK, DN_DV), lambda s: (rev(s), 0, 0, 0))],
        out_shape=[jax.ShapeDtypeStruct((l, DN_V), f32),
                   jax.ShapeDtypeStruct((nch, DN_HEADS, DN_DK, DN_DV), bf16)],
        scratch_shapes=[pltpu.VMEM((DN_HEADS, DN_DK, DN_DV), f32)],
        compiler_params=_params(("arbitrary",), 40),
    )(do, kpt, qwt, egl)


def _dn_post_bwd(qkv, ba, a_log, dt_bias, states, dsp_all, tinv_all, u_all, wq, vn_all, do, dvn_all, name):
    l = qkv.shape[0]
    nch = l // CHUNK
    cb = _scan_chunks(nch)
    items = [(c, h) for c in range(cb) for h in range(DN_HEADS)]

    def body(q_ref, k_ref, v_ref, ba_ref, al_ref, dt_ref, st_ref, dsp_ref, t_ref, u_ref, wq_ref, vn_ref, do_ref,
             dvn_ref, dqkv_ref, dba_ref, dal_ref, ddt_ref):
        step = pl.program_id(0)
        ri, ci = _tri()
        incl, strict = ri >= ci, ri > ci
        lane8 = lax.broadcasted_iota(jnp.int32, (1, DN_HEADS), 1)
        sub8 = lax.broadcasted_iota(jnp.int32, (DN_HEADS, 1), 0)
        last = (lax.broadcasted_iota(jnp.int32, (CHUNK, 1), 0) == CHUNK - 1).astype(f32)
        rsum = lambda t: jnp.sum(t, axis=-1, keepdims=True)
        rows = [slice(c * CHUNK, (c + 1) * CHUNK) for c in range(cb)]
        gates = [_dn_gates(ba_ref[rows[c], :], al_ref, dt_ref, step * cb + c) for c in range(cb)]
        gam = [_cumsum_rows(incl.astype(f32), gates[c][5]) for c in range(cb)]
        gam_t = [gam[c].T for c in range(cb)]
        each = lambda fn: {(c, h): fn(c, h) for c, h in items}
        dk_cols = lambda h: slice(h * DN_DK, (h + 1) * DN_DK)
        dv_cols = lambda h: slice(h * DN_DV, (h + 1) * DN_DV)
        gc = each(lambda c, h: gam[c][:, h:h + 1])
        bh = each(lambda c, h: gates[c][4][:, h:h + 1])
        qh = each(lambda c, h: q_ref[rows[c], dk_cols(h)])
        kh = each(lambda c, h: k_ref[rows[c], dk_cols(h)])
        doh = each(lambda c, h: _b(do_ref[rows[c], dv_cols(h)]))
        sb = each(lambda c, h: st_ref[c, h])
        dspb = each(lambda c, h: dsp_ref[c, h])
        vnb = each(lambda c, h: vn_ref[rows[c], dv_cols(h)])
        dvn = each(lambda c, h: dvn_ref[rows[c], dv_cols(h)])
        wb = each(lambda c, h: wq_ref[c, h, 0:CHUNK, :])
        decay = each(lambda c, h: jnp.exp(jnp.where(incl, gc[c, h] - gam_t[c][h:h + 1, :], -jnp.inf)))
        qb, kb = each(lambda c, h: _b(qh[c, h])), each(lambda c, h: _b(kh[c, h]))
        eg = each(lambda c, h: jnp.exp(gc[c, h]))
        gl = each(lambda c, h: gc[c, h][CHUNK - 1:CHUNK, :])
        ekd = each(lambda c, h: jnp.exp(gl[c, h] - gc[c, h]))
        dvnb = each(lambda c, h: _b(dvn[c, h]))
        kk = each(lambda c, h: _nt(kb[c, h], kb[c, h]))
        p = each(lambda c, h: _nt(qb[c, h], kb[c, h]) * decay[c, h])
        dpraw = each(lambda c, h: _nt(doh[c, h], vnb[c, h]))
        dqe = each(lambda c, h: _nt(doh[c, h], sb[c, h]))
        dke = each(lambda c, h: _nt(vnb[c, h], dspb[c, h]))
        dw = each(lambda c, h: -_nt(dvnb[c, h], sb[c, h]))
        dru = each(lambda c, h: _mm3(t_ref[c, h], dvn[c, h], _tn))
        drw = each(lambda c, h: _mm3(t_ref[c, h], dw[c, h], _tn))
        dqk = each(lambda c, h: _b(dpraw[c, h] * decay[c, h]))
        for c, h in items:
            i = (c, h)
            dqkv_ref[rows[c], dk_cols(h)] = _nn(dqk[i], kb[i]) + dqe[i] * eg[i]
            dqkv_ref[rows[c], 2 * DN_QK + h * DN_DV:2 * DN_QK + (h + 1) * DN_DV] = bh[i] * dru[i]
        da = each(lambda c, h: jnp.where(strict, -(_nt(_b(dru[c, h]), _b(u_ref[rows[c], dv_cols(h)]))
                                                   + _nt(_b(drw[c, h]), wb[c, h])), 0.0))
        dkk = each(lambda c, h: _b(da[c, h] * bh[c, h] * decay[c, h]))
        for c, h in items:
            i = (c, h)
            dqkv_ref[rows[c], DN_QK + h * DN_DK:DN_QK + (h + 1) * DN_DK] = (
                _tn(dqk[i], qb[i]) + dke[i] * ekd[i] + (bh[i] * eg[i]) * drw[i]
                + _nn(dkk[i], kb[i]) + _tn(dkk[i], kb[i]))
        dal = jnp.zeros((1, DN_HEADS), f32)
        ddt = jnp.zeros((1, DN_HEADS), f32)
        dba_ref[...] = jnp.zeros_like(dba_ref)
        for c in range(cb):
            vm, bin_, z, ea, _, g = gates[c]
            dbeta = jnp.zeros((CHUNK, DN_HEADS), f32)
            dgam = jnp.zeros((CHUNK, DN_HEADS), f32)
            dgam_neg_t = jnp.zeros((DN_HEADS, CHUNK), f32)
            for h in range(DN_HEADS):
                i = (c, h)
                keg = kh[i] * eg[i]
                ke = kh[i] * ekd[i]
                rw = rsum(drw[i] * keg)
                rke = rsum(dke[i] * ke)
                db_h = rsum(dru[i] * v_ref[rows[c], dv_cols(h)]) + rw + rsum(da[i] * kk[i] * decay[i])
                mm = da[i] * (bh[i] * kk[i] * decay[i]) + dpraw[i] * p[i]
                dgl = (jnp.sum(rke, axis=0, keepdims=True)
                       + jnp.exp(gl[i]) * jnp.sum(rsum(dspb[i].astype(f32) * sb[i].astype(f32)), axis=0,
                                                  keepdims=True))
                dg_h = rsum(mm) + rw * bh[i] + rsum(dqe[i] * (qh[i] * eg[i])) - rke + last * dgl
                dbeta = dbeta + jnp.where(lane8 == h, db_h, 0.0)
                dgam = dgam + jnp.where(lane8 == h, dg_h, 0.0)
                dgam_neg_t = dgam_neg_t + jnp.where(sub8 == h, jnp.sum(mm, axis=0, keepdims=True), 0.0)
            dgam = dgam - dgam_neg_t.T
            dg = _cumsum_rows((ri <= ci).astype(f32), dgam)
            sg = _sigmoid(bin_)
            dain = dg * (-ea) * vm * _sigmoid(z)
            dba_ref[rows[c], 0:DN_HEADS] = dbeta * vm * sg * (1.0 - sg)
            dba_ref[rows[c], DN_HEADS:2 * DN_HEADS] = dain
            dal = dal + jnp.sum(dg * g, axis=0, keepdims=True)
            ddt = ddt + jnp.sum(dain, axis=0, keepdims=True)

        @pl.when(step == 0)
        def _():
            dal_ref[...] = dal
            ddt_ref[...] = ddt

        @pl.when(step > 0)
        def _():
            dal_ref[...] += dal
            ddt_ref[...] += ddt

    vec = pl.BlockSpec((1, DN_HEADS), lambda s: (0, 0))
    qs = pl.BlockSpec((cb * CHUNK, DN_QK), lambda s: (s, 0))
    ks = pl.BlockSpec((cb * CHUNK, DN_QK), lambda s: (s, 1))
    vs = pl.BlockSpec((cb * CHUNK, DN_V), lambda s: (s, 1))
    v0 = pl.BlockSpec((cb * CHUNK, DN_V), lambda s: (s, 0))
    st = pl.BlockSpec((cb, DN_HEADS, DN_DK, DN_DV), lambda s: (s, 0, 0, 0))
    return pl.pallas_call(
        body, name=name, grid=(nch // cb,),
        in_specs=[qs, ks, vs, pl.BlockSpec((cb * CHUNK, BA_W), lambda s: (s, 0)), vec, vec, st, st,
                  pl.BlockSpec((cb, DN_HEADS, CHUNK, CHUNK), lambda s: (s, 0, 0, 0)),
                  v0, pl.BlockSpec((cb, DN_HEADS, 2 * CHUNK, DN_DK), lambda s: (s, 0, 0, 0)), v0, v0, v0],
        out_specs=[pl.BlockSpec((cb * CHUNK, DN_CONV_CH), lambda s: (s, 0)),
                   pl.BlockSpec((cb * CHUNK, BA_W), lambda s: (s, 0)), vec, vec],
        out_shape=[jax.ShapeDtypeStruct((l, DN_CONV_CH), f32), jax.ShapeDtypeStruct((l, BA_W), f32),
                   jax.ShapeDtypeStruct((1, DN_HEADS), f32), jax.ShapeDtypeStruct((1, DN_HEADS), f32)],
        compiler_params=_params(("arbitrary",), 48),
    )(qkv, qkv, qkv, ba, a_log, dt_bias, states, dsp_all, tinv_all, u_all, wq, vn_all, do, dvn_all)


def _ffn_fwd(h, hn, wg, wu, wd, tb, th, tag, plan, next_norm_w=None):
    fh, d = wd.shape
    a, b, s = plan.call(f"{tag}_gu", functools.partial(_ffn_gu, tm=th // 2, tn=fh // 2), hn, wg, wu, n_out=3)
    out = plan.matmul(f"{tag}_down", s, wd, mode="nn", tm=th // 2, tn=d, tk=fh, res=h, norm_w=next_norm_w)
    return out, (hn, a, b, s)


def _ffn_bwd(dh, dhb, h, nw, wg, wu, wd, saved, tb, th, tag, plan):
    hn, a, b, s = saved
    d = h.shape[1]
    fh = wd.shape[0]
    layer = tag[-1]
    gr = plan.grads
    da, db = _ffn_ds(dhb, wd, a, b, tm=th // 2, tn=fh // 2, name=f"{tag}_b_ds")
    gr["down" + layer] = _matmul(s, dhb, mode="tn", tm=fh // 2, tn=d, tk=th, out_dtype=bf16, name=f"{tag}_b_dwd")
    dh2, dh2b, dnw = plan.call(f"{tag}_b_dhn", functools.partial(_dhn_norm_bwd, tm=th // 2, tk=fh // 2),
                               [(da, wg), (db, wu)], h, nw, dh, n_out=3)
    gr["gate" + layer] = _matmul(hn, da, mode="tn", tm=d, tn=fh // 2, tk=th, out_dtype=bf16, name=f"{tag}_b_dwg")
    gr["up" + layer] = _matmul(hn, db, mode="tn", tm=d, tn=fh // 2, tk=th, out_dtype=bf16, name=f"{tag}_b_dwu")
    return dh2, dh2b, dnw


class _Plan:
    GATHERS = {"ret_proj": ("ret_out", "gate0"), "ret_prep": ("up0",), "ret_scan": ("down0", "dn_in_top"),
               "ffn0_gu": ("dn_in_bottom",), "dn_proj": ("dn_out",), "dn_conv": ("gate1",),
               "dn_prep": ("up1", "down1")}
    SCATTERS = {"ffn1_b_dhn": ("down1",), "dn_b_conv_a": ("gate1", "up1", "dn_out"), "ffn0_b_dhn": ("dn_in",),
                "ret_b_scan": ("gate0", "up0"), "ret_b_dwin": ("down0", "ret_out"), "ret_b_dhn": ("ret_in",)}

    def __init__(self, shards, wts):
        self.shards, self.wts, self.grads, self.parts = shards, wts, {}, {}

    def _exchange(self, stage):
        if self.shards is None:
            return None
        if stage in self.GATHERS:
            return _Exchange([self.shards[n] for n in self.GATHERS[stage]], True)
        if stage in self.SCATTERS:
            return _Exchange([self._dev_major(n) for n in self.SCATTERS[stage]], False)
        return None

    def _dev_major(self, name):
        g = self.grads
        if name[:-1] in ("gate", "up"):
            return _dev_major_cols(g[name], g[name].shape[1] // N_DEV)
        if name[:-1] == "down":
            dwd = g[name]
            return dwd.reshape(N_DEV, dwd.shape[0] // N_DEV, dwd.shape[1])
        if name in ("ret_out", "dn_out"):
            return g[name].reshape(N_DEV, g[name].shape[0] // N_DEV, g[name].shape[1])
        return _dev_major_cols(g[name], self.shards[name].shape[-1])

    def _landed(self, stage, outs):
        if stage in self.SCATTERS:
            self.parts.update(zip(self.SCATTERS[stage], outs))
            return
        w = self.wts
        cols = lambda t: t.transpose(1, 0, 2).reshape(t.shape[1], N_DEV * t.shape[2])
        rows = lambda t: t.reshape(N_DEV * t.shape[1], t.shape[2])
        for name, t in zip(self.GATHERS[stage], outs):
            if name in ("ret_out", "dn_out") or name.startswith("down"):
                w[name] = rows(t)
            else:
                w[name] = cols(t)
        if "dn_in_top" in w and "dn_in_bottom" in w and "dn_main" not in w:
            full = jnp.concatenate([w["dn_in_top"], w["dn_in_bottom"]], axis=0)
            n_main = DN_CONV_CH + DN_V
            w["dn_main"] = full[:, :n_main]
            w["dn_ba"] = jnp.pad(full[:, n_main:], ((0, 0), (0, BA_W - (full.shape[1] - n_main))))

    def matmul(self, stage, a, b, **kw):
        comm = self._exchange(stage)
        if comm is None:
            return _matmul(a, b, name=stage, **kw)
        out, landed = _matmul(a, b, name=stage, comm=comm, **kw)
        self._landed(stage, landed)
        return out

    def call(self, stage, fn, *args, n_out):
        comm = self._exchange(stage)
        out = fn(*args, stage, comm=comm)
        if comm is not None:
            self._landed(stage, out[n_out:])
        return out[:n_out]


def _local_step(x2, target, wts, shards=None):
    plan = _Plan(shards, wts)
    s_len, d = x2.shape
    l = s_len + CHUNK
    tb = _tile(l, 3072)
    th = tb // 2 if (tb // 2) % 16 == 0 else tb
    half = RET_DK // 2
    inv_freq = (np.float32(ROPE_BASE) ** (-np.arange(half, dtype=np.float32) / np.float32(half))).astype(np.float32)
    ang = (np.arange(l) - PAD).astype(np.float32)[:, None] * inv_freq[None, :]
    cos, sin = jnp.asarray(np.cos(ang), f32), jnp.asarray(np.sin(ang), f32)
    lgs = jnp.log1p(-jnp.exp2(-5.0 - jnp.arange(RET_HEADS, dtype=f32)))
    gcs = jnp.exp(lgs * _ret_block(l))

    h0 = jnp.concatenate([jnp.zeros((PAD, d), f32), wts["meta"], x2], axis=0)
    mixw, ffnw = wts["mix_norm"], wts["ffn_norm"]

    hn0 = _rms_fwd(h0, mixw[0:1], "l0_norm")
    proj0 = plan.matmul("ret_proj", hn0, wts["ret_in"], mode="nn", tm=tb, tn=512, tk=d)
    (qk0,) = plan.call("ret_prep", _ret_prep, proj0, cos, sin, n_out=1)
    o0, st0, y0 = plan.call("ret_scan", _ret_scan_fwd, qk0, proj0, wts["ret_gn"], lgs, gcs, n_out=3)
    h1, hn1 = _matmul(y0, wts["ret_out"], mode="nn", tm=th // 2, tn=d, tk=RET_V, res=h0, norm_w=ffnw[0:1],
                      name="ret_out")
    (h2, hn2), ffn0 = _ffn_fwd(h1, hn1, wts["gate0"], wts["up0"], wts["down0"], tb, th, "ffn0", plan,
                               next_norm_w=mixw[1:2])

    proj1 = plan.matmul("dn_proj", hn2, wts["dn_main"], mode="nn", tm=tb, tn=512, tk=d)
    ba = _matmul(hn2, wts["dn_ba"], mode="nn", tm=tb, tn=BA_W, tk=d, name="dn_proj_ba")
    (qkv1,) = plan.call("dn_conv", _dn_conv_fwd, proj1, wts["conv_w"], n_out=1)
    tinv1, u1, wq1, pk1, egl1, kpt1, qwt1 = plan.call("dn_prep", _dn_prep, qkv1, ba, wts["a_log"], wts["dt_bias"],
                                                      n_out=7)
    o1, st1, vn1 = _dn_scan_fwd(u1, wq1, pk1, egl1, "dn_scan")
    y1 = _gnorm_fwd(o1, proj1, wts["dn_norm"], DN_HEADS, DN_DV, 2, "dn_gnorm")
    h3, hn3 = _matmul(y1, wts["dn_out"], mode="nn", tm=th // 2, tn=d, tk=DN_V, res=h2, norm_w=ffnw[1:2],
                      name="dn_out")
    h4, ffn1 = _ffn_fwd(h3, hn3, wts["gate1"], wts["up1"], wts["down1"], tb, th, "ffn1", plan)

    dh4, dh4b, dfinal, loss = _final_loss(h4, wts["final_norm"], target, "final_loss")
    gr = plan.grads
    dh3, dh3b, dffn1 = _ffn_bwd(dh4, dh4b, h3, ffnw[1:2], wts["gate1"], wts["up1"], wts["down1"], ffn1,
                                tb, th, "ffn1", plan)

    gr["dn_out"] = _matmul(y1, dh3b, mode="tn", tm=1024, tn=d, tk=tb, out_dtype=bf16, name="dn_b_dwout")
    do1, dproj1, ddn_norm = _dy_gnorm_bwd(dh3b, wts["dn_out"], o1, proj1, wts["dn_norm"], DN_DV, 2, "dn_b_gnorm",
                                          tm=th // 2, tn=1024)
    dvn1, dsp1 = _dn_scan_bwd(do1, kpt1, qwt1, egl1, "dn_b_scan")
    dqkv1, dba, dalog, ddt = _dn_post_bwd(qkv1, ba, wts["a_log"], wts["dt_bias"], st1, dsp1, tinv1, u1, wq1, vn1,
                                          do1, dvn1, "dn_b_post")
    dc1, dconv = plan.call("dn_b_conv_a", _dn_conv_bwd_a, proj1, wts["conv_w"], dqkv1, n_out=2)
    dproj1 = _dn_conv_bwd_b(dc1, wts["conv_w"], dproj1, "dn_b_conv_b")
    dbab = dba.astype(bf16)
    n_main = dproj1.shape[1]
    dhn2_ba = _matmul(dbab, wts["dn_ba"], mode="nt", tm=th, tn=d, tk=BA_W, name="dn_b_dhn_ba")
    dh2, dh2b, dmix1 = plan.call("dn_b_dhn", functools.partial(_dhn_norm_bwd, tm=th // 2, tk=n_main // 4,
                                                              init=dhn2_ba),
                                 [(dproj1, wts["dn_main"])], h2, mixw[1:2], dh3, n_out=3)
    dw_main = _matmul(hn2, dproj1, mode="tn", tm=d, tn=512, tk=tb, out_dtype=bf16, name="dn_b_dwin")
    dw_ba = _matmul(hn2, dbab, mode="tn", tm=d, tn=BA_W, tk=tb, out_dtype=bf16, name="dn_b_dwin_ba")
    gr["dn_in"] = jnp.concatenate([dw_main, dw_ba], axis=1)

    dh1, dh1b, dffn0 = _ffn_bwd(dh2, dh2b, h1, ffnw[0:1], wts["gate0"], wts["up0"], wts["down0"], ffn0,
                                tb, th, "ffn0", plan)

    gr["ret_out"] = _matmul(y0, dh1b, mode="tn", tm=1024, tn=d, tk=tb, out_dtype=bf16, name="ret_b_dwout")
    do0, dproj0, dret_gn = _dy_gnorm_bwd(dh1b, wts["ret_out"], o0, proj0, wts["ret_gn"], RET_DV, 2, "ret_b_gnorm",
                                         tm=th // 2, tn=1024)
    dq0, dk0, dproj0 = plan.call("ret_b_scan", _ret_scan_bwd, qk0, proj0, st0, do0, dproj0, lgs, gcs, n_out=3)
    dproj0 = _ret_prep_bwd(dq0, dk0, cos, sin, dproj0, "ret_b_prep")
    n_in = dproj0.shape[1]
    gr["ret_in"] = plan.matmul("ret_b_dwin", hn0, dproj0, mode="tn", tm=d, tn=512, tk=tb, out_dtype=bf16)
    dh0, _, dmix0 = plan.call("ret_b_dhn", functools.partial(_dhn_norm_bwd, tm=th // 2, tk=n_in // 4),
                              [(dproj0, wts["ret_in"])], h0, mixw[0:1], dh1, n_out=3)

    gr.update(meta=dh0[PAD:CHUNK], mix_norm=jnp.concatenate([dmix0, dmix1], axis=0),
              ffn_norm=jnp.concatenate([dffn0, dffn1], axis=0), ret_gn=dret_gn, conv_w=dconv, a_log=dalog,
              dt_bias=ddt, dn_norm=ddn_norm, final_norm=dfinal)
    return loss, dh0[CHUNK:], gr, plan


def _adamw_reduce(parts, w, m, v, name):
    _, r, c = parts.shape
    c_pad = -(-c // LANES) * LANES
    tr = _div_tile(r, max(8, (3 * MIB // 16) // c_pad // 8 * 8), 16)

    def body(p_ref, w_ref, m_ref, v_ref, g_ref, d_ref, nm_ref, nv_ref):
        g = p_ref[0].astype(f32)
        for s in range(1, N_DEV):
            g = g + p_ref[s].astype(f32)
        mm = ADAM_B1 * m_ref[...] + (1.0 - ADAM_B1) * g
        vv = ADAM_B2 * v_ref[...] + (1.0 - ADAM_B2) * (g * g)
        m_hat = mm / (1.0 - ADAM_B1 ** ADAM_STEP)
        v_hat = vv / (1.0 - ADAM_B2 ** ADAM_STEP)
        g_ref[...] = g
        d_ref[...] = -ADAM_LR * (m_hat / (jnp.sqrt(v_hat) + ADAM_EPS) + ADAM_WD * w_ref[...])
        nm_ref[...] = mm
        nv_ref[...] = vv

    blk = pl.BlockSpec((tr, c), lambda i: (i, 0))
    return pl.pallas_call(
        body, name=name, grid=(r // tr,),
        in_specs=[pl.BlockSpec((N_DEV, tr, c), lambda i: (0, i, 0)), blk, blk, blk], out_specs=[blk] * 4,
        out_shape=[jax.ShapeDtypeStruct((r, c), f32)] * 4,
        compiler_params=_params(("parallel",), 48),
    )(parts, w, m, v)


def _dev_major_cols(g, width):
    r = g.shape[0]
    return g[:, :N_DEV * width].reshape(r, N_DEV, width).transpose(1, 0, 2)


def kernel(x, meta_tokens, mix_norm_w, ffn_norm_w, ret_w_in, ret_gn_w, ret_w_out, dn_w_in, dn_conv_w, dn_a_log, dn_dt_bias, dn_norm_w, dn_w_out, ffn_w_gate, ffn_w_up, ffn_w_down, final_norm_w, loss_target, m_meta_tokens, m_mix_norm_w, m_ffn_norm_w, m_ret_w_in, m_ret_gn_w, m_ret_w_out, m_dn_w_in, m_dn_conv_w, m_dn_a_log, m_dn_dt_bias, m_dn_norm_w, m_dn_w_out, m_ffn_w_gate, m_ffn_w_up, m_ffn_w_down, m_final_norm_w, v_meta_tokens, v_mix_norm_w, v_ffn_norm_w, v_ret_w_in, v_ret_gn_w, v_ret_w_out, v_dn_w_in, v_dn_conv_w, v_dn_a_log, v_dn_dt_bias, v_dn_norm_w, v_dn_w_out, v_ffn_w_gate, v_ffn_w_up, v_ffn_w_down, v_final_norm_w):
    d = x.shape[-1]
    me = 4 * lax.axis_index("x") + 2 * lax.axis_index("y") + lax.axis_index("c")

    shards = dict(ret_in=ret_w_in[0].astype(bf16), ret_out=ret_w_out[0].astype(bf16),
                  dn_in=dn_w_in[0].astype(bf16), dn_out=dn_w_out[0].astype(bf16))
    shards["dn_in_top"], shards["dn_in_bottom"] = shards["dn_in"][:d // 2], shards["dn_in"][d // 2:]
    for layer in (0, 1):
        shards[f"gate{layer}"] = ffn_w_gate[layer].astype(bf16)
        shards[f"up{layer}"] = ffn_w_up[layer].astype(bf16)
        shards[f"down{layer}"] = ffn_w_down[layer].astype(bf16)
    g_ret_in, g_meta, g_conv, g_dnn = _exchange([shards["ret_in"], meta_tokens, dn_conv_w[0], dn_norm_w], True,
                                                "gather_first")
    cols = lambda g: g.transpose(1, 0, 2).reshape(g.shape[1], N_DEV * g.shape[2])
    wts = dict(meta=cols(g_meta), mix_norm=mix_norm_w, ffn_norm=ffn_norm_w, ret_in=cols(g_ret_in), ret_gn=ret_gn_w,
               conv_w=cols(g_conv), a_log=dn_a_log, dt_bias=dn_dt_bias, dn_norm=cols(g_dnn),
               final_norm=final_norm_w.reshape(1, d))

    loss_part, grad_x, gr, plan = _local_step(x[0], loss_target[0], wts, shards)
    loss = lax.psum(loss_part[0, 0], AXES)

    pp = plan.parts
    both = lambda name: jnp.concatenate([pp[name + "0"], pp[name + "1"]], axis=1)
    big_parts = [pp["ret_in"], pp["ret_out"], pp["dn_in"], pp["dn_out"], both("gate"), both("up"), both("down")]
    big_names = ["ret_w_in", "ret_w_out", "dn_w_in", "dn_w_out", "ffn_w_gate", "ffn_w_up", "ffn_w_down"]
    big_w = [ret_w_in, ret_w_out, dn_w_in, dn_w_out, ffn_w_gate, ffn_w_up, ffn_w_down]
    big_m = [m_ret_w_in, m_ret_w_out, m_dn_w_in, m_dn_w_out, m_ffn_w_gate, m_ffn_w_up, m_ffn_w_down]
    big_v = [v_ret_w_in, v_ret_w_out, v_dn_w_in, v_dn_w_out, v_ffn_w_gate, v_ffn_w_up, v_ffn_w_down]
    res = {}
    for nm, parts, w_, m_, v_ in zip(big_names, big_parts, big_w, big_m, big_v):
        r2, c2 = parts.shape[1], parts.shape[2]
        outs = _adamw_reduce(parts, w_.reshape(r2, c2), m_.reshape(r2, c2), v_.reshape(r2, c2), f"adamw_{nm}")
        res[nm] = [o.reshape(w_.shape) for o in outs]

    small_names = ["meta_tokens", "mix_norm_w", "ffn_norm_w", "ret_gn_w", "dn_conv_w", "dn_a_log", "dn_dt_bias",
                   "dn_norm_w", "final_norm_w"]
    small_g = [gr["meta"], gr["mix_norm"], gr["ffn_norm"], gr["ret_gn"], gr["conv_w"], gr["a_log"], gr["dt_bias"],
               gr["dn_norm"], gr["final_norm"]]
    small_w = [meta_tokens, mix_norm_w, ffn_norm_w, ret_gn_w, dn_conv_w, dn_a_log, dn_dt_bias, dn_norm_w, final_norm_w]
    small_m = [m_meta_tokens, m_mix_norm_w, m_ffn_norm_w, m_ret_gn_w, m_dn_conv_w, m_dn_a_log, m_dn_dt_bias,
               m_dn_norm_w, m_final_norm_w]
    small_v = [v_meta_tokens, v_mix_norm_w, v_ffn_norm_w, v_ret_gn_w, v_dn_conv_w, v_dn_a_log, v_dn_dt_bias,
               v_dn_norm_w, v_final_norm_w]
    sharded = {"meta_tokens", "dn_conv_w", "dn_norm_w"}
    flat = jnp.concatenate([g.reshape(-1) for g in small_g])
    row = 8 * LANES
    n_flat = flat.shape[0]
    flat = jnp.pad(flat, (0, -n_flat % row)).reshape(-1, row)
    (gathered,) = _exchange([flat], True, "gather_small_grads")
    gathered = gathered.reshape(N_DEV, -1)
    pieces, off = [], 0
    for nm, g, w_ in zip(small_names, small_g, small_w):
        full = gathered[:, off:off + g.size].reshape((N_DEV,) + g.shape)
        off += g.size
        if nm in sharded:
            wloc = w_.shape[-1]
            full = lax.dynamic_slice_in_dim(full, me * wloc, wloc, axis=full.ndim - 1)
        pieces.append(full.reshape(N_DEV, -1))
    sizes = [p.shape[1] for p in pieces]
    n_loc = sum(sizes)
    pad_loc = -n_loc % row

    def pack(vs, lead):
        cat = jnp.concatenate([a.reshape(lead + (-1,)) for a in vs], axis=-1)
        cat = jnp.pad(cat, [(0, 0)] * len(lead) + [(0, pad_loc)])
        return cat.reshape(lead + (-1, row))

    outs = _adamw_reduce(pack(pieces, (N_DEV,)), pack(small_w, ()), pack(small_m, ()), pack(small_v, ()), "adamw_small")
    off = 0
    for nm, sz, w_ in zip(small_names, sizes, small_w):
        res[nm] = [o.reshape(-1)[off:off + sz].reshape(w_.shape) for o in outs]
        off += sz

    order = ["meta_tokens", "mix_norm_w", "ffn_norm_w", "ret_w_in", "ret_gn_w", "ret_w_out", "dn_w_in", "dn_conv_w",
             "dn_a_log", "dn_dt_bias", "dn_norm_w", "dn_w_out", "ffn_w_gate", "ffn_w_up", "ffn_w_down", "final_norm_w"]
    grad_x = grad_x.reshape(x.shape)
    return (loss, grad_x, *[res[nm][0] for nm in order], *[res[nm][1] for nm in order],
            *[res[nm][2] for nm in order], *[res[nm][3] for nm in order])
```

```python
import functools
import math

import jax
import jax.numpy as jnp
import numpy as np
from jax import lax
from jax.experimental import pallas as pl
from jax.experimental.pallas import tpu as pltpu

f32 = jnp.float32
bf16 = jnp.bfloat16

N_META = 16
CHUNK = 64
PAD = CHUNK - N_META
RMS_EPS = 1e-6
RET_HEADS, RET_DK, RET_DV = 4, 256, 512
RET_QK, RET_V = RET_HEADS * RET_DK, RET_HEADS * RET_DV
DN_HEADS, DN_DK, DN_DV = 8, 128, 256
DN_QK, DN_V = DN_HEADS * DN_DK, DN_HEADS * DN_DV
DN_CONV_CH = 2 * DN_QK + DN_V
CONV_K = 4
ROPE_BASE = 10000.0
ADAM_LR, ADAM_B1, ADAM_B2, ADAM_EPS, ADAM_WD, ADAM_STEP = 0.001, 0.9, 0.999, 1e-08, 0.01, 10
N_DEV = 8
AXES = ("x", "y", "c")
LANES = 128
MIB = 1024 * 1024


def _tile(n_rows, cap):
    nch = n_rows // CHUNK
    best = 1
    for d in range(1, nch + 1):
        if nch % d == 0 and d * CHUNK <= cap:
            best = d
    return best * CHUNK


def _div_tile(n, cap, align):
    best = None
    for d in range(align, min(n, cap) + 1, align):
        if n % d == 0:
            best = d
    return best if best is not None else n


def _params(sem, vmem_mb):
    return pltpu.CompilerParams(dimension_semantics=sem, vmem_limit_bytes=int(vmem_mb * MIB))


def _nn(a, b, precision=None):
    return jnp.dot(a, b, preferred_element_type=f32, precision=precision)


def _nt(a, b, precision=None):
    return lax.dot_general(a, b, (((1,), (1,)), ((), ())), preferred_element_type=f32, precision=precision)


def _tn(a, b, precision=None):
    return lax.dot_general(a, b, (((0,), (0,)), ((), ())), preferred_element_type=f32, precision=precision)


def _b(x):
    return x.astype(bf16)


def _sigmoid(x):
    return 0.5 * jnp.tanh(0.5 * x) + 0.5


def _silu(x):
    return x * _sigmoid(x)


def _dsilu(x):
    s = _sigmoid(x)
    return s * (1.0 + x * (1.0 - s))


def _peer(k):
    x, y, c = lax.axis_index("x"), lax.axis_index("y"), lax.axis_index("c")
    px = 1 - x if k & 4 else x
    py = 1 - y if k & 2 else y
    pc = 1 - c if k & 1 else c
    return (px, py, pc), 4 * px + 2 * py + pc


class _Exchange:
    def __init__(self, arrs, gather):
        self.arrs, self.gather, self.n = list(arrs), gather, len(arrs)
        self.out_shapes = [jax.ShapeDtypeStruct(((N_DEV,) + a.shape) if gather else a.shape, a.dtype) for a in arrs]
        self.specs = [pl.BlockSpec(memory_space=pltpu.HBM)] * self.n
        self.scratch = [pltpu.SemaphoreType.DMA((self.n, N_DEV - 1)), pltpu.SemaphoreType.DMA((self.n, N_DEV - 1)),
                        pltpu.SemaphoreType.DMA((self.n,))]

    def _copies(self, ins, outs, sems):
        send_sems, recv_sems, local_sems = sems
        me = 4 * lax.axis_index("x") + 2 * lax.axis_index("y") + lax.axis_index("c")
        src = (lambda a, dest: ins[a]) if self.gather else (lambda a, dest: ins[a].at[dest])
        local = [pltpu.make_async_copy(src(a, me), outs[a].at[me], local_sems.at[a]) for a in range(self.n)]
        sends, lands = [], []
        for k in range(1, N_DEV):
            peer, pidx = _peer(k)
            for a in range(self.n):
                for dst, lst in ((outs[a].at[me], sends), (outs[a].at[pidx], lands)):
                    lst.append(pltpu.make_async_remote_copy(
                        src_ref=src(a, pidx), dst_ref=dst, send_sem=send_sems.at[a, k - 1],
                        recv_sem=recv_sems.at[a, k - 1], device_id=peer, device_id_type=pl.DeviceIdType.MESH))
        return local, sends, lands

    def start(self, ins, outs, sems):
        local, sends, _ = self._copies(ins, outs, sems)
        for cp in local + sends:
            cp.start()

    def wait(self, ins, outs, sems):
        local, sends, lands = self._copies(ins, outs, sems)
        for cp in lands:
            cp.wait_recv()
        for cp in sends:
            cp.wait_send()
        for cp in local:
            cp.wait()


def _call(body, args, *, name, grid, in_specs, out_specs, out_shape, scratch=(), sem, vmem_mb, comm=None,
          aliases=None):
    aliases = aliases or {}
    if comm is None:
        out = pl.pallas_call(body, name=name, grid=grid, in_specs=list(in_specs), out_specs=list(out_specs),
                             out_shape=list(out_shape), scratch_shapes=list(scratch), input_output_aliases=aliases,
                             compiler_params=_params(sem, vmem_mb))(*args)
        return list(out)
    n_in, n_out, n_scr, nc = len(args), len(out_shape), len(scratch), comm.n

    def carried(*refs):
        ins, cin = refs[:n_in], refs[n_in:n_in + nc]
        o0 = n_in + nc
        outs, cout = refs[o0:o0 + n_out], refs[o0 + n_out:o0 + n_out + nc]
        s0 = o0 + n_out + nc
        scr, sems = refs[s0:s0 + n_scr], refs[s0 + n_scr:]
        first = functools.reduce(jnp.logical_and, [pl.program_id(i) == 0 for i in range(len(grid))])
        last = functools.reduce(jnp.logical_and, [pl.program_id(i) == grid[i] - 1 for i in range(len(grid))])

        @pl.when(first)
        def _():
            comm.start(cin, cout, sems)

        body(*ins, *outs, *scr)

        @pl.when(last)
        def _():
            comm.wait(cin, cout, sems)

    out = pl.pallas_call(
        carried, name=name, grid=grid, in_specs=list(in_specs) + comm.specs, out_specs=list(out_specs) + comm.specs,
        out_shape=list(out_shape) + comm.out_shapes, scratch_shapes=list(scratch) + comm.scratch,
        input_output_aliases=aliases,
        compiler_params=_params(("arbitrary",) * len(grid), vmem_mb))(*args, *comm.arrs)
    return list(out)


def _exchange(arrs, gather, name):
    comm = _Exchange(arrs, gather)

    def body(*refs):
        ins, outs, sems = refs[:comm.n], refs[comm.n:2 * comm.n], refs[2 * comm.n:]
        comm.start(ins, outs, sems)
        comm.wait(ins, outs, sems)

    return pl.pallas_call(body, name=name, in_specs=comm.specs, out_specs=comm.specs, out_shape=comm.out_shapes,
                          scratch_shapes=comm.scratch)(*comm.arrs)


def _matmul(a, b, *, mode, tm, tn, tk, name, out_dtype=f32, res=None, vmem_mb=48, comm=None, norm_w=None):
    if mode == "nn":
        (m, k), (k2, n) = a.shape, b.shape
    elif mode == "nt":
        (m, k), (n, k2) = a.shape, b.shape
    else:
        (k, m), (k2, n) = a.shape, b.shape
    assert k == k2 and m % tm == 0 and n % tn == 0 and k % tk == 0, (name, a.shape, b.shape, tm, tn, tk)
    nk = k // tk
    dot = {"nn": _nn, "nt": _nt, "tn": _tn}[mode]
    a_spec = {"nn": pl.BlockSpec((tm, tk), lambda i, j, kk: (i, kk)),
              "nt": pl.BlockSpec((tm, tk), lambda i, j, kk: (i, kk)),
              "tn": pl.BlockSpec((tk, tm), lambda i, j, kk: (kk, i))}[mode]
    b_spec = {"nn": pl.BlockSpec((tk, tn), lambda i, j, kk: (kk, j)),
              "nt": pl.BlockSpec((tn, tk), lambda i, j, kk: (j, kk)),
              "tn": pl.BlockSpec((tk, tn), lambda i, j, kk: (kk, j))}[mode]
    o_spec = pl.BlockSpec((tm, tn), lambda i, j, kk: (i, j))
    has_res = res is not None
    has_norm = norm_w is not None
    assert not has_norm or tn == n
    n_ops = 2 + has_res + has_norm

    def body(*refs):
        a_ref, b_ref = refs[:2]
        r_ref = refs[2] if has_res else None
        nw_ref = refs[2 + has_res] if has_norm else None
        o_ref = refs[n_ops]
        hn_ref = refs[n_ops + 1] if has_norm else None
        rest = refs[n_ops + 1 + has_norm:]

        def finish(tot):
            if has_res:
                tot = tot + r_ref[...]
            o_ref[...] = tot.astype(out_dtype)
            if has_norm:
                r = lax.rsqrt(jnp.mean(tot * tot, axis=-1, keepdims=True) + RMS_EPS)
                hn_ref[...] = _b(tot * r * nw_ref[...])

        if nk == 1:
            finish(dot(_b(a_ref[...]), _b(b_ref[...])))
            return
        acc_ref = rest[0]
        kk = pl.program_id(2)

        @pl.when(kk == 0)
        def _():
            acc_ref[...] = dot(_b(a_ref[...]), _b(b_ref[...]))

        @pl.when(kk > 0)
        def _():
            acc_ref[...] += dot(_b(a_ref[...]), _b(b_ref[...]))

        @pl.when(kk == nk - 1)
        def _():
            finish(acc_ref[...])

    in_specs = [a_spec, b_spec]
    args = (a, b)
    if has_res:
        in_specs.append(o_spec)
        args += (res,)
    out_specs, out_shape = [o_spec], [jax.ShapeDtypeStruct((m, n), out_dtype)]
    if has_norm:
        in_specs.append(pl.BlockSpec((1, tn), lambda i, j, kk: (0, j)))
        args += (norm_w,)
        out_specs.append(o_spec)
        out_shape.append(jax.ShapeDtypeStruct((m, n), bf16))
    out = _call(body, args, name=name, grid=(m // tm, n // tn, nk), in_specs=in_specs, out_specs=out_specs,
                out_shape=out_shape, scratch=[pltpu.VMEM((tm, tn), f32)] if nk > 1 else [],
                sem=("parallel", "parallel", "arbitrary"), vmem_mb=vmem_mb, comm=comm)
    n_own = len(out_shape)
    own = out[0] if n_own == 1 else tuple(out[:n_own])
    return own if comm is None else (own, out[n_own:])


def _rms_fwd(h, w, name):
    l, d = h.shape
    tr = _tile(l, 512)

    def body(h_ref, w_ref, o_ref):
        x = h_ref[...]
        r = lax.rsqrt(jnp.mean(x * x, axis=-1, keepdims=True) + RMS_EPS)
        o_ref[...] = _b(x * r * w_ref[...])

    return pl.pallas_call(
        body, name=name, grid=(l // tr,),
        in_specs=[pl.BlockSpec((tr, d), lambda i: (i, 0)), pl.BlockSpec((1, d), lambda i: (0, 0))],
        out_specs=pl.BlockSpec((tr, d), lambda i: (i, 0)),
        out_shape=jax.ShapeDtypeStruct((l, d), bf16),
        compiler_params=_params(("parallel",), 32),
    )(h, w)


def _dhn_norm_bwd(pairs, h, nw, dres, name, *, tm, tk, init=None, comm=None):
    l, d = h.shape
    nks = [a.shape[1] // tk for a, _ in pairs]
    starts = [sum(nks[:p]) for p in range(len(pairs))]
    nk = sum(nks)
    n_ops = 2 * len(pairs)
    has_init = init is not None

    def body(*refs):
        ops = refs[:n_ops]
        init_ref = refs[n_ops] if has_init else None
        h_ref, w_ref, r_ref, dh_ref, dhb_ref, dw_ref, acc_ref = refs[n_ops + has_init:]
        i, kk = pl.program_id(0), pl.program_id(1)

        @pl.when(kk == 0)
        def _():
            part = _nt(ops[0][...], ops[1][...])
            acc_ref[...] = part + init_ref[...] if has_init else part

        for p in range(len(pairs)):
            lo = max(starts[p], 1)

            @pl.when(jnp.logical_and(kk >= lo, kk < starts[p] + nks[p]))
            def _(a_ref=ops[2 * p], b_ref=ops[2 * p + 1]):
                acc_ref[...] += _nt(a_ref[...], b_ref[...])

        @pl.when(kk == nk - 1)
        def _():
            g = acc_ref[...]
            x = h_ref[...]
            r = lax.rsqrt(jnp.mean(x * x, axis=-1, keepdims=True) + RMS_EPS)
            xh = x * r
            dxh = g * w_ref[...]
            dh = r_ref[...] + r * (dxh - xh * jnp.mean(dxh * xh, axis=-1, keepdims=True))
            dh_ref[...] = dh
            dhb_ref[...] = _b(dh)
            dw = jnp.sum(g * xh, axis=0, keepdims=True)

            @pl.when(i == 0)
            def _():
                dw_ref[...] = dw

            @pl.when(i > 0)
            def _():
                dw_ref[...] += dw

    def k_of(p):
        return lambda kk: jnp.clip(kk - starts[p], 0, nks[p] - 1)

    in_specs, args = [], []
    for p, (a, b) in enumerate(pairs):
        in_specs += [pl.BlockSpec((tm, tk), functools.partial(lambda i, kk, f: (i, f(kk)), f=k_of(p))),
                     pl.BlockSpec((d, tk), functools.partial(lambda i, kk, f: (0, f(kk)), f=k_of(p)))]
        args += [a, b]
    row = pl.BlockSpec((tm, d), lambda i, kk: (i, 0))
    vec = pl.BlockSpec((1, d), lambda i, kk: (0, 0))
    if has_init:
        in_specs.append(row)
        args.append(init)
    return _call(body, tuple(args) + (h, nw, dres), name=name, grid=(l // tm, nk), in_specs=in_specs + [row, vec, row],
                 out_specs=[row, row, vec],
                 out_shape=[jax.ShapeDtypeStruct((l, d), f32), jax.ShapeDtypeStruct((l, d), bf16),
                            jax.ShapeDtypeStruct((1, d), f32)],
                 scratch=[pltpu.VMEM((tm, d), f32)], sem=("arbitrary", "arbitrary"), vmem_mb=48, comm=comm)


def _final_loss(h, w, target, name):
    l, d = h.shape
    nch = l // CHUNK
    cpt = _tile(l, 256) // CHUNK
    nt = nch // cpt

    def body(h_ref, w_ref, *rest):
        t_refs, (dh_ref, dhb_ref, dw_ref, loss_ref) = rest[:cpt], rest[cpt:]
        i = pl.program_id(0)
        wv = w_ref[...]
        dw = jnp.zeros((1, d), f32)
        part = jnp.zeros((1, 1), f32)
        for c in range(cpt):
            rows = slice(c * CHUNK, (c + 1) * CHUNK)
            live = (i * cpt + c > 0).astype(f32)
            x = h_ref[rows, :]
            r = lax.rsqrt(jnp.mean(x * x, axis=-1, keepdims=True) + RMS_EPS)
            xh = x * r
            err = (xh * wv - t_refs[c][...]) * live
            dy = err * (1.0 / d)
            dxh = dy * wv
            dx = r * (dxh - xh * jnp.mean(dxh * xh, axis=-1, keepdims=True))
            dh_ref[rows, :] = dx
            dhb_ref[rows, :] = _b(dx)
            dw = dw + jnp.sum(dy * xh, axis=0, keepdims=True)
            part = part + 0.5 * jnp.sum(jnp.sum(err * err, axis=-1, keepdims=True) * (1.0 / d), axis=0, keepdims=True)
        part = jnp.broadcast_to(part, (1, LANES))

        @pl.when(i == 0)
        def _():
            dw_ref[...] = dw
            loss_ref[...] = part

        @pl.when(i > 0)
        def _():
            dw_ref[...] += dw
            loss_ref[...] += part

    row = pl.BlockSpec((cpt * CHUNK, d), lambda i: (i, 0))
    vec = pl.BlockSpec((1, d), lambda i: (0, 0))
    t_specs = [pl.BlockSpec((CHUNK, d), functools.partial(lambda i, c: (jnp.maximum(i * cpt + c - 1, 0), 0), c=c))
               for c in range(cpt)]
    return pl.pallas_call(
        body, name=name, grid=(nt,),
        in_specs=[row, vec] + t_specs,
        out_specs=[row, row, vec, pl.BlockSpec((1, LANES), lambda i: (0, 0))],
        out_shape=[jax.ShapeDtypeStruct((l, d), f32), jax.ShapeDtypeStruct((l, d), bf16),
                   jax.ShapeDtypeStruct((1, d), f32), jax.ShapeDtypeStruct((1, LANES), f32)],
        compiler_params=_params(("arbitrary",), 32),
    )(h, w, *([target] * cpt))


def _ffn_gu(hn, wg, wu, name, *, tm, tn, comm=None):
    l, d = hn.shape
    fh = wg.shape[1]

    def body(h_ref, g_ref, u_ref, a_ref, b_ref, s_ref):
        hb = h_ref[...]
        a = _nn(hb, g_ref[...])
        bb = _nn(hb, u_ref[...])
        a_ref[...] = _b(a)
        b_ref[...] = _b(bb)
        s_ref[...] = _b(_silu(a) * bb)

    wspec = pl.BlockSpec((d, tn), lambda i, j: (0, j))
    ospec = pl.BlockSpec((tm, tn), lambda i, j: (i, j))
    return _call(body, (hn, wg, wu), name=name, grid=(l // tm, fh // tn),
                 in_specs=[pl.BlockSpec((tm, d), lambda i, j: (i, 0)), wspec, wspec], out_specs=[ospec] * 3,
                 out_shape=[jax.ShapeDtypeStruct((l, fh), bf16)] * 3, sem=("parallel", "parallel"), vmem_mb=48,
                 comm=comm)


def _ffn_ds(dhb, wd, a, b, *, tm, tn, name):
    l, d = dhb.shape
    fh = wd.shape[0]

    def body(g_ref, w_ref, a_ref, b_ref, da_ref, db_ref):
        ds = _nt(g_ref[...], w_ref[...])
        a = a_ref[...].astype(f32)
        da_ref[...] = _b(ds * b_ref[...].astype(f32) * _dsilu(a))
        db_ref[...] = _b(ds * _silu(a))

    ospec = pl.BlockSpec((tm, tn), lambda i, j: (i, j))
    return pl.pallas_call(
        body, name=name, grid=(l // tm, fh // tn),
        in_specs=[pl.BlockSpec((tm, d), lambda i, j: (i, 0)), pl.BlockSpec((tn, d), lambda i, j: (j, 0)), ospec, ospec],
        out_specs=[ospec, ospec], out_shape=[jax.ShapeDtypeStruct((l, fh), bf16)] * 2,
        compiler_params=_params(("parallel", "parallel"), 48),
    )(dhb, wd, a, b)


def _gnorm_fwd(o, proj, nw, heads, dv, gate_blk, name):
    l, hv = o.shape
    tr = _tile(l, 256)

    def body(o_ref, g_ref, w_ref, y_ref):
        wv = w_ref[...]
        for h in range(heads):
            sl = slice(h * dv, (h + 1) * dv)
            oh = o_ref[:, sl]
            r = lax.rsqrt(jnp.mean(oh * oh, axis=-1, keepdims=True) + RMS_EPS)
            y_ref[:, sl] = _b(oh * r * wv * _silu(g_ref[:, sl]))

    return pl.pallas_call(
        body, name=name, grid=(l // tr,),
        in_specs=[pl.BlockSpec((tr, hv), lambda i: (i, 0)), pl.BlockSpec((tr, hv), lambda i: (i, gate_blk)),
                  pl.BlockSpec((1, dv), lambda i: (0, 0))],
        out_specs=pl.BlockSpec((tr, hv), lambda i: (i, 0)),
        out_shape=jax.ShapeDtypeStruct((l, hv), bf16),
        compiler_params=_params(("parallel",), 32),
    )(o, proj, nw)


def _dy_gnorm_bwd(dhb, w_out, o, proj, nw, dv, gate_blk, name, *, tm, tn):
    l, hv = o.shape
    d = dhb.shape[1]
    nj = hv // tn
    heads = tn // dv

    def body(g_ref, w_ref, o_ref, gate_ref, nw_ref, do_ref, dg_ref, dw_ref):
        dy = _nt(g_ref[...], w_ref[...])
        wv = nw_ref[...]
        dw = jnp.zeros((1, dv), f32)
        for h in range(heads):
            sl = slice(h * dv, (h + 1) * dv)
            oh = o_ref[:, sl]
            g = gate_ref[:, sl]
            dyh = dy[:, sl]
            r = lax.rsqrt(jnp.mean(oh * oh, axis=-1, keepdims=True) + RMS_EPS)
            xh = oh * r
            dn = dyh * _silu(g)
            dg_ref[:, sl] = _b(dyh * (xh * wv) * _dsilu(g))
            dxh = dn * wv
            do_ref[:, sl] = r * (dxh - xh * jnp.mean(dxh * xh, axis=-1, keepdims=True))
            dw = dw + jnp.sum(dn * xh, axis=0, keepdims=True)
        first = jnp.logical_and(pl.program_id(0) == 0, pl.program_id(1) == 0)

        @pl.when(first)
        def _():
            dw_ref[...] = dw

        @pl.when(jnp.logical_not(first))
        def _():
            dw_ref[...] += dw

    tile = pl.BlockSpec((tm, tn), lambda i, j: (i, j))
    gate = pl.BlockSpec((tm, tn), lambda i, j: (i, gate_blk * nj + j))
    vec = pl.BlockSpec((1, dv), lambda i, j: (0, 0))
    return pl.pallas_call(
        body, name=name, grid=(l // tm, nj),
        in_specs=[pl.BlockSpec((tm, d), lambda i, j: (i, 0)), pl.BlockSpec((tn, d), lambda i, j: (j, 0)),
                  tile, gate, vec],
        out_specs=[tile, gate, vec],
        out_shape=[jax.ShapeDtypeStruct((l, hv), f32), jax.ShapeDtypeStruct(proj.shape, bf16),
                   jax.ShapeDtypeStruct((1, dv), f32)],
        compiler_params=_params(("arbitrary", "arbitrary"), 48),
    )(dhb, w_out, o, proj, nw)


def _ret_prep(proj, cos, sin, name, comm=None):
    l = proj.shape[0]
    tr = _tile(l, 256)
    half = RET_DK // 2
    scale = RET_DK ** -0.5

    def body(p_ref, c_ref, s_ref, o_ref):
        rows = pl.program_id(0) * tr + lax.broadcasted_iota(jnp.int32, (tr, 1), 0)
        kmul = jnp.where(rows >= PAD, scale, 0.0).astype(f32)
        c, s = c_ref[...], s_ref[...]
        for j in range(2 * RET_HEADS):
            t1 = p_ref[:, j * RET_DK: j * RET_DK + half]
            t2 = p_ref[:, j * RET_DK + half: (j + 1) * RET_DK]
            o1 = t1 * c - t2 * s
            o2 = t1 * s + t2 * c
            if j >= RET_HEADS:
                o1, o2 = o1 * kmul, o2 * kmul
            o_ref[:, j * RET_DK: j * RET_DK + half] = o1
            o_ref[:, j * RET_DK + half: (j + 1) * RET_DK] = o2

    wide = pl.BlockSpec((tr, 2 * RET_QK), lambda i: (i, 0))
    tab = pl.BlockSpec((tr, half), lambda i: (i, 0))
    return _call(body, (proj, cos, sin), name=name, grid=(l // tr,), in_specs=[wide, tab, tab], out_specs=[wide],
                 out_shape=[jax.ShapeDtypeStruct((l, 2 * RET_QK), f32)], sem=("parallel",), vmem_mb=32, comm=comm)


def _ret_prep_bwd(dq, dk, cos, sin, dproj, name):
    l = dq.shape[0]
    tr = _tile(l, 256)
    half = RET_DK // 2
    scale = RET_DK ** -0.5

    def body(dq_ref, dk_ref, c_ref, s_ref, _, o_ref):
        rows = pl.program_id(0) * tr + lax.broadcasted_iota(jnp.int32, (tr, 1), 0)
        kmul = jnp.where(rows >= PAD, scale, 0.0).astype(f32)
        c, s = c_ref[...], s_ref[...]
        for j in range(2 * RET_HEADS):
            d_ref = dq_ref if j < RET_HEADS else dk_ref
            jj = j % RET_HEADS
            d1 = d_ref[:, jj * RET_DK: jj * RET_DK + half]
            d2 = d_ref[:, jj * RET_DK + half: (jj + 1) * RET_DK]
            if j >= RET_HEADS:
                d1, d2 = d1 * kmul, d2 * kmul
            o_ref[:, j * RET_DK: j * RET_DK + half] = _b(d1 * c + d2 * s)
            o_ref[:, j * RET_DK + half: (j + 1) * RET_DK] = _b(d2 * c - d1 * s)

    nar = pl.BlockSpec((tr, RET_QK), lambda i: (i, 0))
    wide = pl.BlockSpec((tr, 2 * RET_QK), lambda i: (i, 0))
    tab = pl.BlockSpec((tr, half), lambda i: (i, 0))
    return pl.pallas_call(
        body, name=name, grid=(l // tr,), in_specs=[nar, nar, tab, tab, pl.BlockSpec(memory_space=pl.ANY)],
        out_specs=wide, out_shape=jax.ShapeDtypeStruct(dproj.shape, dproj.dtype), input_output_aliases={4: 0},
        compiler_params=_params(("parallel",), 32),
    )(dq, dk, cos, sin, dproj)


RET_BLOCK_CHUNKS = 3


def _ret_block(l):
    nch = l // CHUNK
    return RET_BLOCK_CHUNKS * CHUNK if nch % RET_BLOCK_CHUNKS == 0 else CHUNK


def _ret_decay(lg, rb):
    idx = lax.broadcasted_iota(jnp.int32, (rb, 1), 0).astype(f32)
    ri = lax.broadcasted_iota(jnp.int32, (rb, rb), 0)
    ci = lax.broadcasted_iota(jnp.int32, (rb, rb), 1)
    rel = (ri - ci).astype(f32)
    dmask = jnp.where(ri >= ci, jnp.exp(lg * jnp.maximum(rel, 0.0)), 0.0)
    xi = jnp.exp(lg * (idx + 1.0))
    zeta = jnp.exp(lg * (rb - 1.0 - idx))
    return dmask, xi, zeta


def _ret_scan_fwd(qk, proj, gn_w, lgs, gcs, name, comm=None):
    l = qk.shape[0]
    rb = _ret_block(l)
    nb = l // rb

    def body(lg_ref, gc_ref, q_ref, k_ref, v_ref, g_ref, nw_ref, o_ref, st_ref, y_ref, s_ref):
        @pl.when(pl.program_id(0) == 0)
        def _():
            s_ref[...] = jnp.zeros_like(s_ref)

        hs = range(RET_HEADS)
        dec = [_ret_decay(lg_ref[h], rb) for h in hs]
        q = [q_ref[:, h * RET_DK:(h + 1) * RET_DK] for h in hs]
        k = [k_ref[:, h * RET_DK:(h + 1) * RET_DK] for h in hs]
        vb = [_b(v_ref[:, h * RET_DV:(h + 1) * RET_DV]) for h in hs]
        s = [s_ref[h] for h in hs]
        sb = [_b(s[h]) for h in hs]
        scores = [_b(_nt(_b(q[h]), _b(k[h])) * dec[h][0]) for h in hs]
        inter = [_nn(_b(q[h] * dec[h][1]), sb[h]) for h in hs]
        kv = [_tn(_b(k[h] * dec[h][2]), vb[h]) for h in hs]
        nw = nw_ref[...]
        for h in hs:
            cols = slice(h * RET_DV, (h + 1) * RET_DV)
            st_ref[0, h] = sb[h]
            o = _nn(scores[h], vb[h]) + inter[h]
            o_ref[:, cols] = o
            r = lax.rsqrt(jnp.mean(o * o, axis=-1, keepdims=True) + RMS_EPS)
            y_ref[:, cols] = _b(o * r * nw * _silu(g_ref[:, cols]))
            s_ref[h] = gc_ref[h] * s[h] + kv[h]

    smem = pl.BlockSpec(memory_space=pltpu.SMEM)
    wide = pl.BlockSpec((rb, RET_V), lambda n: (n, 0))
    return _call(
        body, (lgs, gcs, qk, qk, proj, proj, gn_w), name=name, grid=(nb,),
        in_specs=[smem, smem,
                  pl.BlockSpec((rb, RET_QK), lambda n: (n, 0)),
                  pl.BlockSpec((rb, RET_QK), lambda n: (n, 1)),
                  pl.BlockSpec((rb, RET_V), lambda n: (n, 1)),
                  pl.BlockSpec((rb, RET_V), lambda n: (n, 2)),
                  pl.BlockSpec((1, RET_DV), lambda n: (0, 0))],
        out_specs=[wide, pl.BlockSpec((1, RET_HEADS, RET_DK, RET_DV), lambda n: (n, 0, 0, 0)), wide],
        out_shape=[jax.ShapeDtypeStruct((l, RET_V), f32),
                   jax.ShapeDtypeStruct((nb, RET_HEADS, RET_DK, RET_DV), bf16),
                   jax.ShapeDtypeStruct((l, RET_V), bf16)],
        scratch=[pltpu.VMEM((RET_HEADS, RET_DK, RET_DV), f32)], sem=("arbitrary",), vmem_mb=40, comm=comm)


def _ret_scan_bwd(qk, proj, states, do, dproj, lgs, gcs, name, comm=None):
    l = qk.shape[0]
    rb = _ret_block(l)
    nb = l // rb

    def body(lg_ref, gc_ref, q_ref, k_ref, v_ref, st_ref, do_ref, _, dq_ref, dk_ref, dv_ref, ds_ref):
        @pl.when(pl.program_id(0) == 0)
        def _():
            ds_ref[...] = jnp.zeros_like(ds_ref)

        hs = range(RET_HEADS)
        dec = [_ret_decay(lg_ref[h], rb) for h in hs]
        q = [q_ref[:, h * RET_DK:(h + 1) * RET_DK] for h in hs]
        k = [k_ref[:, h * RET_DK:(h + 1) * RET_DK] for h in hs]
        qb, kb = [_b(t) for t in q], [_b(t) for t in k]
        vb = [_b(v_ref[:, h * RET_DV:(h + 1) * RET_DV]) for h in hs]
        dob = [_b(do_ref[:, h * RET_DV:(h + 1) * RET_DV]) for h in hs]
        dsp = [ds_ref[h] for h in hs]
        dspb = [_b(t) for t in dsp]
        scores = [_b(_nt(qb[h], kb[h]) * dec[h][0]) for h in hs]
        dscores = [_b(_nt(dob[h], vb[h]) * dec[h][0]) for h in hs]
        for h in hs:
            dq_ref[:, h * RET_DK:(h + 1) * RET_DK] = _nn(dscores[h], kb[h]) + _nt(dob[h], st_ref[0, h]) * dec[h][1]
        for h in hs:
            dk_ref[:, h * RET_DK:(h + 1) * RET_DK] = _tn(dscores[h], qb[h]) + _nt(vb[h], dspb[h]) * dec[h][2]
        for h in hs:
            dv_ref[:, h * RET_DV:(h + 1) * RET_DV] = _b(_tn(scores[h], dob[h]) + _nn(_b(k[h] * dec[h][2]), dspb[h]))
        for h in hs:
            ds_ref[h] = gc_ref[h] * dsp[h] + _tn(_b(q[h] * dec[h][1]), dob[h])

    smem = pl.BlockSpec(memory_space=pltpu.SMEM)
    rev = lambda n: nb - 1 - n
    return _call(
        body, (lgs, gcs, qk, qk, proj, states, do, dproj), name=name, grid=(nb,),
        in_specs=[smem, smem,
                  pl.BlockSpec((rb, RET_QK), lambda n: (rev(n), 0)),
                  pl.BlockSpec((rb, RET_QK), lambda n: (rev(n), 1)),
                  pl.BlockSpec((rb, RET_V), lambda n: (rev(n), 1)),
                  pl.BlockSpec((1, RET_HEADS, RET_DK, RET_DV), lambda n: (rev(n), 0, 0, 0)),
                  pl.BlockSpec((rb, RET_V), lambda n: (rev(n), 0)),
                  pl.BlockSpec(memory_space=pl.ANY)],
        out_specs=[pl.BlockSpec((rb, RET_QK), lambda n: (rev(n), 0)),
                   pl.BlockSpec((rb, RET_QK), lambda n: (rev(n), 0)),
                   pl.BlockSpec((rb, RET_V), lambda n: (rev(n), 1))],
        out_shape=[jax.ShapeDtypeStruct((l, RET_QK), f32), jax.ShapeDtypeStruct((l, RET_QK), f32),
                   jax.ShapeDtypeStruct(dproj.shape, dproj.dtype)],
        scratch=[pltpu.VMEM((RET_HEADS, RET_DK, RET_DV), f32)], sem=("arbitrary",), vmem_mb=40, comm=comm,
        aliases={7: 2})


CONV_BLK = 512
CONV_Q_BLKS = DN_QK // CONV_BLK
HALO = 8


def _conv_tile(l):
    return _tile(l, 3072)


def _slab_rows(r):
    return pl.ds(pl.multiple_of(r * HALO, HALO), HALO)


def _conv_slab(x_ref, p_ref, r, i, tr):
    cur = x_ref[_slab_rows(r), :]
    prev = jnp.where(r > 0, x_ref[_slab_rows(jnp.maximum(r - 1, 0)), :], p_ref[...])
    row0 = i * tr + r * HALO
    cur = jnp.where(row0 >= PAD, cur, 0.0)
    prev = jnp.where(row0 - HALO >= PAD, prev, 0.0)
    lrow = lax.broadcasted_iota(jnp.int32, (HALO, 1), 0)
    shifted = [jnp.where(lrow < s, pltpu.roll(prev, s, 0), pltpu.roll(cur, s, 0)) for s in range(1, CONV_K)]
    return [cur] + shifted


def _conv_of(xs, w):
    acc = xs[0] * w[CONV_K - 1:CONV_K, :]
    for s in range(1, CONV_K):
        acc = acc + xs[s] * w[CONV_K - 1 - s:CONV_K - s, :]
    return acc


def _slab_loop(n_slabs, fn, init=None):
    return lax.fori_loop(0, n_slabs, fn, init, unroll=8)


def _dn_conv_fwd(proj, conv_w, name, comm=None):
    l = proj.shape[0]
    tr = _conv_tile(l)
    nblk = DN_CONV_CH // CONV_BLK
    heads = CONV_BLK // DN_DK

    def body(x_ref, p_ref, w_ref, o_ref):
        i, j = pl.program_id(0), pl.program_id(1)
        w = w_ref[...]

        def act(r):
            return _silu(_conv_of(_conv_slab(x_ref, p_ref, r, i, tr), w))

        def normed(scale):
            def slab(r, carry):
                a = act(r)
                outs = []
                for h in range(heads):
                    ah = a[:, h * DN_DK:(h + 1) * DN_DK]
                    outs.append(ah * (lax.rsqrt(jnp.sum(ah * ah, axis=-1, keepdims=True) + RMS_EPS) * scale))
                o_ref[_slab_rows(r), :] = jnp.concatenate(outs, axis=1)
                return carry
            return slab

        def plain(r, carry):
            o_ref[_slab_rows(r), :] = act(r)
            return carry

        @pl.when(j < CONV_Q_BLKS)
        def _():
            _slab_loop(tr // HALO, normed(DN_DK ** -0.5))

        @pl.when(jnp.logical_and(j >= CONV_Q_BLKS, j < 2 * CONV_Q_BLKS))
        def _():
            _slab_loop(tr // HALO, normed(1.0))

        @pl.when(j >= 2 * CONV_Q_BLKS)
        def _():
            _slab_loop(tr // HALO, plain)

    hb = tr // HALO
    return _call(
        body, (proj, proj, conv_w), name=name, grid=(l // tr, nblk),
        in_specs=[pl.BlockSpec((tr, CONV_BLK), lambda i, j: (i, j)),
                  pl.BlockSpec((HALO, CONV_BLK), lambda i, j: (jnp.maximum(i * hb - 1, 0), j)),
                  pl.BlockSpec((CONV_K, CONV_BLK), lambda i, j: (0, j))],
        out_specs=[pl.BlockSpec((tr, CONV_BLK), lambda i, j: (i, j))],
        out_shape=[jax.ShapeDtypeStruct((l, DN_CONV_CH), f32)],
        scratch=[], sem=("parallel", "parallel"), vmem_mb=32, comm=comm)


def _dn_conv_bwd_a(proj, conv_w, dqkv, name, comm=None):
    l = proj.shape[0]
    tr = _conv_tile(l)
    nblk = DN_CONV_CH // CONV_BLK
    heads = CONV_BLK // DN_DK

    def body(x_ref, p_ref, w_ref, d_ref, dc_ref, dw_ref, acc_ref):
        j, i = pl.program_id(0), pl.program_id(1)
        w = w_ref[...]
        acc_ref[...] = jnp.zeros_like(acc_ref)

        def slab_of(l2_scale):
            def slab(r, carry):
                xs = _conv_slab(x_ref, p_ref, r, i, tr)
                c = _conv_of(xs, w)
                a = _silu(c)
                dy = d_ref[_slab_rows(r), :]
                if l2_scale is None:
                    da = dy
                else:
                    parts = []
                    for h in range(heads):
                        sl = slice(h * DN_DK, (h + 1) * DN_DK)
                        ah, dyh = a[:, sl], dy[:, sl]
                        rn = lax.rsqrt(jnp.sum(ah * ah, axis=-1, keepdims=True) + RMS_EPS)
                        yh = ah * rn
                        parts.append((rn * l2_scale) * (dyh - yh * jnp.sum(dyh * yh, axis=-1, keepdims=True)))
                    da = jnp.concatenate(parts, axis=1)
                dc = da * _dsilu(c)
                dc_ref[_slab_rows(r), :] = dc
                for s in range(CONV_K):
                    acc_ref[CONV_K - 1 - s] += dc * xs[s]
                return carry
            return slab

        @pl.when(j < CONV_Q_BLKS)
        def _():
            _slab_loop(tr // HALO, slab_of(DN_DK ** -0.5))

        @pl.when(jnp.logical_and(j >= CONV_Q_BLKS, j < 2 * CONV_Q_BLKS))
        def _():
            _slab_loop(tr // HALO, slab_of(1.0))

        @pl.when(j >= 2 * CONV_Q_BLKS)
        def _():
            _slab_loop(tr // HALO, slab_of(None))

        ksel = lax.broadcasted_iota(jnp.int32, (CONV_K, 1), 0)
        dw = jnp.zeros((CONV_K, CONV_BLK), f32)
        for k in range(CONV_K):
            dw = dw + jnp.where(ksel == k, jnp.sum(acc_ref[k], axis=0, keepdims=True), 0.0)

        @pl.when(i == 0)
        def _():
            dw_ref[...] = dw

        @pl.when(i > 0)
        def _():
            dw_ref[...] += dw

    hb = tr // HALO
    blk = pl.BlockSpec((tr, CONV_BLK), lambda j, i: (i, j))
    return _call(
        body, (proj, proj, conv_w, dqkv), name=name, grid=(nblk, l // tr),
        in_specs=[blk, pl.BlockSpec((HALO, CONV_BLK), lambda j, i: (jnp.maximum(i * hb - 1, 0), j)),
                  pl.BlockSpec((CONV_K, CONV_BLK), lambda j, i: (0, j)), blk],
        out_specs=[blk, pl.BlockSpec((CONV_K, CONV_BLK), lambda j, i: (0, j))],
        out_shape=[jax.ShapeDtypeStruct((l, DN_CONV_CH), f32), jax.ShapeDtypeStruct((CONV_K, DN_CONV_CH), f32)],
        scratch=[pltpu.VMEM((CONV_K, HALO, CONV_BLK), f32)], sem=("parallel", "arbitrary"), vmem_mb=48, comm=comm)


def _dn_conv_bwd_b(dc, conv_w, dproj, name):
    l = dc.shape[0]
    tr = _conv_tile(l)
    nblk = DN_CONV_CH // CONV_BLK
    nrow = l // tr

    n_slabs = tr // HALO
    pair = 2 * HALO

    def body(d_ref, n_ref, w_ref, _, o_ref):
        i = pl.program_id(0)
        w = w_ref[...]
        nxt_tile = jnp.where(i < nrow - 1, n_ref[...], 0.0)
        lrow = lax.broadcasted_iota(jnp.int32, (HALO, 1), 0)

        def one(r):
            cur = d_ref[_slab_rows(r), :]
            nxt = jnp.where(r < n_slabs - 1, d_ref[_slab_rows(jnp.minimum(r + 1, n_slabs - 1)), :], nxt_tile)
            acc = cur * w[CONV_K - 1:CONV_K, :]
            for s in range(1, CONV_K):
                up = jnp.where(lrow >= HALO - s, pltpu.roll(nxt, HALO - s, 0), pltpu.roll(cur, HALO - s, 0))
                acc = acc + up * w[CONV_K - 1 - s:CONV_K - s, :]
            return jnp.where(i * tr + r * HALO >= PAD, acc, 0.0)

        def two(q, carry):
            rows = pl.ds(pl.multiple_of(q * pair, pair), pair)
            o_ref[rows, :] = _b(jnp.concatenate([one(2 * q), one(2 * q + 1)], axis=0))
            return carry

        lax.fori_loop(0, n_slabs // 2, two, None, unroll=4)

    hb = tr // HALO
    nh = l // HALO
    return pl.pallas_call(
        body, name=name, grid=(nrow, nblk),
        in_specs=[pl.BlockSpec((tr, CONV_BLK), lambda i, j: (i, j)),
                  pl.BlockSpec((HALO, CONV_BLK), lambda i, j: (jnp.minimum((i + 1) * hb, nh - 1), j)),
                  pl.BlockSpec((CONV_K, CONV_BLK), lambda i, j: (0, j)),
                  pl.BlockSpec(memory_space=pl.ANY)],
        out_specs=pl.BlockSpec((tr, CONV_BLK), lambda i, j: (i, j)),
        out_shape=jax.ShapeDtypeStruct(dproj.shape, dproj.dtype), input_output_aliases={3: 0},
        compiler_params=_params(("parallel", "parallel"), 32),
    )(dc, dc, conv_w, dproj)


BA_W = LANES


def _dn_gates(ba_ref, al_ref, dt_ref, n):
    rows = n * CHUNK + lax.broadcasted_iota(jnp.int32, (CHUNK, 1), 0)
    vm = (rows >= PAD).astype(f32)
    bin_ = ba_ref[:, 0:DN_HEADS]
    z = ba_ref[:, DN_HEADS:2 * DN_HEADS] + dt_ref[...]
    sp = jnp.maximum(z, 0.0) + jnp.log1p(jnp.exp(-jnp.abs(z)))
    ea = jnp.exp(al_ref[...])
    beta = _sigmoid(bin_) * vm
    g = -ea * sp * vm
    return vm, bin_, z, ea, beta, g


def _tri():
    ri = lax.broadcasted_iota(jnp.int32, (CHUNK, CHUNK), 0)
    ci = lax.broadcasted_iota(jnp.int32, (CHUNK, CHUNK), 1)
    return ri, ci


def _split(a):
    hi = _b(a)
    return hi, _b(a - hi.astype(f32))


def _mm3(a, b, dot=_nn):
    (ah, al), (bh, bl) = _split(a), _split(b)
    return dot(ah, bh) + (dot(ah, bl) + dot(al, bh))


def _cumsum_rows(tri, g):
    tb = _b(tri)
    g1 = _b(g)
    r1 = g - g1.astype(f32)
    g2 = _b(r1)
    g3 = _b(r1 - g2.astype(f32))
    return _nn(tb, g1) + (_nn(tb, g2) + _nn(tb, g3))


DN_SCAN_CHUNKS = 3


def _scan_chunks(nch):
    return DN_SCAN_CHUNKS if nch % DN_SCAN_CHUNKS == 0 else 1


def _dn_prep(qkv, ba, a_log, dt_bias, name, comm=None):
    l = qkv.shape[0]
    nch = l // CHUNK
    heads = range(DN_HEADS)

    cb = _scan_chunks(nch)
    items = [(c, h) for c in range(cb) for h in heads]

    def body(q_ref, k_ref, v_ref, ba_ref, al_ref, dt_ref, t_ref, u_ref, wq_ref, pk_ref, eg_ref, kpt_ref, qwt_ref):
        n0 = pl.program_id(0) * cb
        ri, ci = _tri()
        incl, strict = ri >= ci, ri > ci
        eye = (ri == ci).astype(f32)
        rows = [slice(c * CHUNK, (c + 1) * CHUNK) for c in range(cb)]
        gam, gam_t, beta = [], [], []
        for c in range(cb):
            _, _, _, _, beta_c, g_c = _dn_gates(ba_ref[rows[c], :], al_ref, dt_ref, n0 + c)
            gam.append(_cumsum_rows(incl.astype(f32), g_c))
            gam_t.append(gam[c].T)
            beta.append(beta_c)
        gc = {(c, h): gam[c][:, h:h + 1] for c, h in items}
        bh = {(c, h): beta[c][:, h:h + 1] for c, h in items}
        kh = {(c, h): k_ref[rows[c], h * DN_DK:(h + 1) * DN_DK] for c, h in items}
        kb = {i: _b(kh[i]) for i in items}
        decay = {(c, h): jnp.exp(jnp.where(incl, gc[c, h] - gam_t[c][h:h + 1, :], -jnp.inf)) for c, h in items}
        a = {i: jnp.where(strict, bh[i] * _nt(kb[i], kb[i]) * decay[i], 0.0) for i in items}
        t = {i: eye - a[i] for i in items}
        p = a
        for level in range(int(math.log2(CHUNK)) - 1):
            mm = _mm3 if level < 2 else (lambda x, y: _nn(_b(x), _b(y)))
            p = {i: mm(p[i], p[i]) for i in items}
            t = {i: t[i] + mm(t[i], p[i]) for i in items}
        eg = {i: jnp.exp(gc[i]) for i in items}
        for c, h in items:
            i = (c, h)
            t_ref[c, h] = t[i]
            u_ref[rows[c], h * DN_DV:(h + 1) * DN_DV] = _mm3(t[i], v_ref[rows[c], h * DN_DV:(h + 1) * DN_DV] * bh[i])
            w = _mm3(t[i], kh[i] * (bh[i] * eg[i]))
            wq_ref[c, h, 0:CHUNK, :] = _b(w)
            qwt_ref[c, h, DN_DK:2 * DN_DK, :] = _b(w.T)
        for c, h in items:
            i = (c, h)
            qh = q_ref[rows[c], h * DN_DK:(h + 1) * DN_DK]
            gl = gc[i][CHUNK - 1:CHUNK, :]
            qe = qh * eg[i]
            ke = kh[i] * jnp.exp(gl - gc[i])
            pmat = _nt(_b(qh), kb[i]) * decay[i]
            wq_ref[c, h, CHUNK:2 * CHUNK, :] = _b(qe)
            qwt_ref[c, h, 0:DN_DK, :] = _b(qe.T)
            pk_ref[c, h, 0:CHUNK, :] = _b(pmat)
            pk_ref[c, h, CHUNK:CHUNK + DN_DK, :] = _b(ke.T)
            kpt_ref[c, h, :, 0:DN_DK] = _b(ke)
            kpt_ref[c, h, :, DN_DK:DN_DK + CHUNK] = _b(pmat.T)
            eg_ref[c, h] = jnp.broadcast_to(jnp.exp(gl), (8, LANES))

    vec = pl.BlockSpec((1, DN_HEADS), lambda n: (0, 0))
    return _call(
        body, (qkv, qkv, qkv, ba, a_log, dt_bias), name=name, grid=(nch // cb,),
        in_specs=[pl.BlockSpec((cb * CHUNK, DN_QK), lambda n: (n, 0)),
                  pl.BlockSpec((cb * CHUNK, DN_QK), lambda n: (n, 1)),
                  pl.BlockSpec((cb * CHUNK, DN_V), lambda n: (n, 1)),
                  pl.BlockSpec((cb * CHUNK, BA_W), lambda n: (n, 0)), vec, vec],
        out_specs=[pl.BlockSpec((cb, DN_HEADS, CHUNK, CHUNK), lambda n: (n, 0, 0, 0)),
                   pl.BlockSpec((cb * CHUNK, DN_V), lambda n: (n, 0)),
                   pl.BlockSpec((cb, DN_HEADS, 2 * CHUNK, DN_DK), lambda n: (n, 0, 0, 0)),
                   pl.BlockSpec((cb, DN_HEADS, CHUNK + DN_DK, CHUNK), lambda n: (n, 0, 0, 0)),
                   pl.BlockSpec((cb, DN_HEADS, 8, LANES), lambda n: (n, 0, 0, 0)),
                   pl.BlockSpec((cb, DN_HEADS, CHUNK, DN_DK + CHUNK), lambda n: (n, 0, 0, 0)),
                   pl.BlockSpec((cb, DN_HEADS, 2 * DN_DK, CHUNK), lambda n: (n, 0, 0, 0))],
        out_shape=[jax.ShapeDtypeStruct((nch, DN_HEADS, CHUNK, CHUNK), f32),
                   jax.ShapeDtypeStruct((l, DN_V), f32),
                   jax.ShapeDtypeStruct((nch, DN_HEADS, 2 * CHUNK, DN_DK), bf16),
                   jax.ShapeDtypeStruct((nch, DN_HEADS, CHUNK + DN_DK, CHUNK), bf16),
                   jax.ShapeDtypeStruct((nch, DN_HEADS, 8, LANES), f32),
                   jax.ShapeDtypeStruct((nch, DN_HEADS, CHUNK, DN_DK + CHUNK), bf16),
                   jax.ShapeDtypeStruct((nch, DN_HEADS, 2 * DN_DK, CHUNK), bf16)],
        sem=("parallel",), vmem_mb=40, comm=comm)


def _dn_scan_fwd(u, wq, pk, egl, name):
    l = u.shape[0]
    nch = l // CHUNK
    cs = _scan_chunks(nch)

    def body(u_ref, wq_ref, pk_ref, eg_ref, o_ref, st_ref, vn_ref, s_ref):
        @pl.when(pl.program_id(0) == 0)
        def _():
            s_ref[...] = jnp.zeros_like(s_ref)

        hs = range(DN_HEADS)
        cols = [slice(h * DN_DV, (h + 1) * DN_DV) for h in hs]
        s = [s_ref[h] for h in hs]
        for c in range(cs):
            rows = slice(c * CHUNK, (c + 1) * CHUNK)
            sb = [_b(s[h]) for h in hs]
            x = [_nn(wq_ref[c, h], sb[h]) for h in hs]
            vnb = [_b(u_ref[rows, cols[h]] - x[h][0:CHUNK]) for h in hs]
            y = [_nn(pk_ref[c, h], vnb[h]) for h in hs]
            for h in hs:
                st_ref[c, h] = sb[h]
                vn_ref[rows, cols[h]] = vnb[h]
                o_ref[rows, cols[h]] = x[h][CHUNK:2 * CHUNK] + y[h][0:CHUNK]
            s = [eg_ref[c, h][0:1, 0:1] * s[h] + y[h][CHUNK:CHUNK + DN_DK] for h in hs]
        for h in hs:
            s_ref[h] = s[h]

    return pl.pallas_call(
        body, name=name, grid=(nch // cs,),
        in_specs=[pl.BlockSpec((cs * CHUNK, DN_V), lambda n: (n, 0)),
                  pl.BlockSpec((cs, DN_HEADS, 2 * CHUNK, DN_DK), lambda n: (n, 0, 0, 0)),
                  pl.BlockSpec((cs, DN_HEADS, CHUNK + DN_DK, CHUNK), lambda n: (n, 0, 0, 0)),
                  pl.BlockSpec((cs, DN_HEADS, 8, LANES), lambda n: (n, 0, 0, 0))],
        out_specs=[pl.BlockSpec((cs * CHUNK, DN_V), lambda n: (n, 0)),
                   pl.BlockSpec((cs, DN_HEADS, DN_DK, DN_DV), lambda n: (n, 0, 0, 0)),
                   pl.BlockSpec((cs * CHUNK, DN_V), lambda n: (n, 0))],
        out_shape=[jax.ShapeDtypeStruct((l, DN_V), f32),
                   jax.ShapeDtypeStruct((nch, DN_HEADS, DN_DK, DN_DV), bf16),
                   jax.ShapeDtypeStruct((l, DN_V), bf16)],
        scratch_shapes=[pltpu.VMEM((DN_HEADS, DN_DK, DN_DV), f32)],
        compiler_params=_params(("arbitrary",), 40),
    )(u, wq, pk, egl)


def _dn_scan_bwd(do, kpt, qwt, egl, name):
    l = do.shape[0]
    nch = l // CHUNK
    cs = _scan_chunks(nch)
    nblk = nch // cs

    def body(do_ref, kpt_ref, qwt_ref, eg_ref, dvn_ref, dsp_ref, ds_ref):
        @pl.when(pl.program_id(0) == 0)
        def _():
            ds_ref[...] = jnp.zeros_like(ds_ref)

        hs = range(DN_HEADS)
        cols = [slice(h * DN_DV, (h + 1) * DN_DV) for h in hs]
        ds = [ds_ref[h] for h in hs]
        for c in reversed(range(cs)):
            rows = slice(c * CHUNK, (c + 1) * CHUNK)
            dspb = [_b(ds[h]) for h in hs]
            dob = [_b(do_ref[rows, cols[h]]) for h in hs]
            dvn = [_nn(kpt_ref[c, h][:, 0:DN_DK], dspb[h]) + _nn(kpt_ref[c, h][:, DN_DK:DN_DK + CHUNK], dob[h])
                   for h in hs]
            for h in hs:
                dsp_ref[c, h] = dspb[h]
                dvn_ref[rows, cols[h]] = dvn[h]
            ds = [eg_ref[c, h][0:1, 0:1] * ds[h] + _nn(qwt_ref[c, h][0:DN_DK], dob[h])
                  - _nn(qwt_ref[c, h][DN_DK:2 * DN_DK], _b(dvn[h])) for h in hs]
        for h in hs:
            ds_ref[h] = ds[h]

    rev = lambda s: nblk - 1 - s
    return pl.pallas_call(
        body, name=name, grid=(nblk,),
        in_specs=[pl.BlockSpec((cs * CHUNK, DN_V), lambda s: (rev(s), 0)),
                  pl.BlockSpec((cs, DN_HEADS, CHUNK, DN_DK + CHUNK), lambda s: (rev(s), 0, 0, 0)),
                  pl.BlockSpec((cs, DN_HEADS, 2 * DN_DK, CHUNK), lambda s: (rev(s), 0, 0, 0)),
                  pl.BlockSpec((cs, DN_HEADS, 8, LANES), lambda s: (rev(s), 0, 0, 0))],
        out_specs=[pl.BlockSpec((cs * CHUNK, DN_V), lambda s: (rev(s), 0)),
                   pl.BlockSpec((cs, DN_HEADS, DN_DK, DN_DV), lambda s: (rev(s), 0, 0, 0))],
        out_shape=[jax.ShapeDtypeStruct((l, DN_V), f32),
                   jax.ShapeDtypeStruct((nch, DN_HEADS, DN_DK, DN_DV), bf16)],
        scratch_shapes=[pltpu.VMEM((DN_HEADS, DN_DK, DN_DV), f32)],
        compiler_params=_params(("arbitrary",), 40),
    )(do, kpt, qwt, egl)


def _dn_post_bwd(qkv, ba, a_log, dt_bias, states, dsp_all, tinv_all, u_all, wq, vn_all, do, dvn_all, name):
    l = qkv.shape[0]
    nch = l // CHUNK
    cb = _scan_chunks(nch)
    items = [(c, h) for c in range(cb) for h in range(DN_HEADS)]

    def body(q_ref, k_ref, v_ref, ba_ref, al_ref, dt_ref, st_ref, dsp_ref, t_ref, u_ref, wq_ref, vn_ref, do_ref,
             dvn_ref, dqkv_ref, dba_ref, dal_ref, ddt_ref):
        step = pl.program_id(0)
        ri, ci = _tri()
        incl, strict = ri >= ci, ri > ci
        lane8 = lax.broadcasted_iota(jnp.int32, (1, DN_HEADS), 1)
        sub8 = lax.broadcasted_iota(jnp.int32, (DN_HEADS, 1), 0)
        last = (lax.broadcasted_iota(jnp.int32, (CHUNK, 1), 0) == CHUNK - 1).astype(f32)
        rsum = lambda t: jnp.sum(t, axis=-1, keepdims=True)
        rows = [slice(c * CHUNK, (c + 1) * CHUNK) for c in range(cb)]
        gates = [_dn_gates(ba_ref[rows[c], :], al_ref, dt_ref, step * cb + c) for c in range(cb)]
        gam = [_cumsum_rows(incl.astype(f32), gates[c][5]) for c in range(cb)]
        gam_t = [gam[c].T for c in range(cb)]
        each = lambda fn: {(c, h): fn(c, h) for c, h in items}
        dk_cols = lambda h: slice(h * DN_DK, (h + 1) * DN_DK)
        dv_cols = lambda h: slice(h * DN_DV, (h + 1) * DN_DV)
        gc = each(lambda c, h: gam[c][:, h:h + 1])
        bh = each(lambda c, h: gates[c][4][:, h:h + 1])
        qh = each(lambda c, h: q_ref[rows[c], dk_cols(h)])
        kh = each(lambda c, h: k_ref[rows[c], dk_cols(h)])
        doh = each(lambda c, h: _b(do_ref[rows[c], dv_cols(h)]))
        sb = each(lambda c, h: st_ref[c, h])
        dspb = each(lambda c, h: dsp_ref[c, h])
        vnb = each(lambda c, h: vn_ref[rows[c], dv_cols(h)])
        dvn = each(lambda c, h: dvn_ref[rows[c], dv_cols(h)])
        wb = each(lambda c, h: wq_ref[c, h, 0:CHUNK, :])
        decay = each(lambda c, h: jnp.exp(jnp.where(incl, gc[c, h] - gam_t[c][h:h + 1, :], -jnp.inf)))
        qb, kb = each(lambda c, h: _b(qh[c, h])), each(lambda c, h: _b(kh[c, h]))
        eg = each(lambda c, h: jnp.exp(gc[c, h]))
        gl = each(lambda c, h: gc[c, h][CHUNK - 1:CHUNK, :])
        ekd = each(lambda c, h: jnp.exp(gl[c, h] - gc[c, h]))
        dvnb = each(lambda c, h: _b(dvn[c, h]))
        kk = each(lambda c, h: _nt(kb[c, h], kb[c, h]))
        p = each(lambda c, h: _nt(qb[c, h], kb[c, h]) * decay[c, h])
        dpraw = each(lambda c, h: _nt(doh[c, h], vnb[c, h]))
        dqe = each(lambda c, h: _nt(doh[c, h], sb[c, h]))
        dke = each(lambda c, h: _nt(vnb[c, h], dspb[c, h]))
        dw = each(lambda c, h: -_nt(dvnb[c, h], sb[c, h]))
        dru = each(lambda c, h: _mm3(t_ref[c, h], dvn[c, h], _tn))
        drw = each(lambda c, h: _mm3(t_ref[c, h], dw[c, h], _tn))
        dqk = each(lambda c, h: _b(dpraw[c, h] * decay[c, h]))
        for c, h in items:
            i = (c, h)
            dqkv_ref[rows[c], dk_cols(h)] = _nn(dqk[i], kb[i]) + dqe[i] * eg[i]
            dqkv_ref[rows[c], 2 * DN_QK + h * DN_DV:2 * DN_QK + (h + 1) * DN_DV] = bh[i] * dru[i]
        da = each(lambda c, h: jnp.where(strict, -(_nt(_b(dru[c, h]), _b(u_ref[rows[c], dv_cols(h)]))
                                                   + _nt(_b(drw[c, h]), wb[c, h])), 0.0))
        dkk = each(lambda c, h: _b(da[c, h] * bh[c, h] * decay[c, h]))
        for c, h in items:
            i = (c, h)
            dqkv_ref[rows[c], DN_QK + h * DN_DK:DN_QK + (h + 1) * DN_DK] = (
                _tn(dqk[i], qb[i]) + dke[i] * ekd[i] + (bh[i] * eg[i]) * drw[i]
                + _nn(dkk[i], kb[i]) + _tn(dkk[i], kb[i]))
        dal = jnp.zeros((1, DN_HEADS), f32)
        ddt = jnp.zeros((1, DN_HEADS), f32)
        dba_ref[...] = jnp.zeros_like(dba_ref)
        for c in range(cb):
            vm, bin_, z, ea, _, g = gates[c]
            dbeta = jnp.zeros((CHUNK, DN_HEADS), f32)
            dgam = jnp.zeros((CHUNK, DN_HEADS), f32)
            dgam_neg_t = jnp.zeros((DN_HEADS, CHUNK), f32)
            for h in range(DN_HEADS):
                i = (c, h)
                keg = kh[i] * eg[i]
                ke = kh[i] * ekd[i]
                rw = rsum(drw[i] * keg)
                rke = rsum(dke[i] * ke)
                db_h = rsum(dru[i] * v_ref[rows[c], dv_cols(h)]) + rw + rsum(da[i] * kk[i] * decay[i])
                mm = da[i] * (bh[i] * kk[i] * decay[i]) + dpraw[i] * p[i]
                dgl = (jnp.sum(rke, axis=0, keepdims=True)
                       + jnp.exp(gl[i]) * jnp.sum(rsum(dspb[i].astype(f32) * sb[i].astype(f32)), axis=0,
                                                  keepdims=True))
                dg_h = rsum(mm) + rw * bh[i] + rsum(dqe[i] * (qh[i] * eg[i])) - rke + last * dgl
                dbeta = dbeta + jnp.where(lane8 == h, db_h, 0.0)
                dgam = dgam + jnp.where(lane8 == h, dg_h, 0.0)
                dgam_neg_t = dgam_neg_t + jnp.where(sub8 == h, jnp.sum(mm, axis=0, keepdims=True), 0.0)
            dgam = dgam - dgam_neg_t.T
            dg = _cumsum_rows((ri <= ci).astype(f32), dgam)
            sg = _sigmoid(bin_)
            dain = dg * (-ea) * vm * _sigmoid(z)
            dba_ref[rows[c], 0:DN_HEADS] = dbeta * vm * sg * (1.0 - sg)
            dba_ref[rows[c], DN_HEADS:2 * DN_HEADS] = dain
            dal = dal + jnp.sum(dg * g, axis=0, keepdims=True)
            ddt = ddt + jnp.sum(dain, axis=0, keepdims=True)

        @pl.when(step == 0)
        def _():
            dal_ref[...] = dal
            ddt_ref[...] = ddt

        @pl.when(step > 0)
        def _():
            dal_ref[...] += dal
            ddt_ref[...] += ddt

    vec = pl.BlockSpec((1, DN_HEADS), lambda s: (0, 0))
    qs = pl.BlockSpec((cb * CHUNK, DN_QK), lambda s: (s, 0))
    ks = pl.BlockSpec((cb * CHUNK, DN_QK), lambda s: (s, 1))
    vs = pl.BlockSpec((cb * CHUNK, DN_V), lambda s: (s, 1))
    v0 = pl.BlockSpec((cb * CHUNK, DN_V), lambda s: (s, 0))
    st = pl.BlockSpec((cb, DN_HEADS, DN_DK, DN_DV), lambda s: (s, 0, 0, 0))
    return pl.pallas_call(
        body, name=name, grid=(nch // cb,),
        in_specs=[qs, ks, vs, pl.BlockSpec((cb * CHUNK, BA_W), lambda s: (s, 0)), vec, vec, st, st,
                  pl.BlockSpec((cb, DN_HEADS, CHUNK, CHUNK), lambda s: (s, 0, 0, 0)),
                  v0, pl.BlockSpec((cb, DN_HEADS, 2 * CHUNK, DN_DK), lambda s: (s, 0, 0, 0)), v0, v0, v0],
        out_specs=[pl.BlockSpec((cb * CHUNK, DN_CONV_CH), lambda s: (s, 0)),
                   pl.BlockSpec((cb * CHUNK, BA_W), lambda s: (s, 0)), vec, vec],
        out_shape=[jax.ShapeDtypeStruct((l, DN_CONV_CH), f32), jax.ShapeDtypeStruct((l, BA_W), f32),
                   jax.ShapeDtypeStruct((1, DN_HEADS), f32), jax.ShapeDtypeStruct((1, DN_HEADS), f32)],
        compiler_params=_params(("arbitrary",), 48),
    )(qkv, qkv, qkv, ba, a_log, dt_bias, states, dsp_all, tinv_all, u_all, wq, vn_all, do, dvn_all)


def _ffn_fwd(h, hn, wg, wu, wd, tb, th, tag, plan, next_norm_w=None):
    fh, d = wd.shape
    a, b, s = plan.call(f"{tag}_gu", functools.partial(_ffn_gu, tm=th // 2, tn=fh // 2), hn, wg, wu, n_out=3)
    out = plan.matmul(f"{tag}_down", s, wd, mode="nn", tm=th // 2, tn=d, tk=fh, res=h, norm_w=next_norm_w)
    return out, (hn, a, b, s)


def _ffn_bwd(dh, dhb, h, nw, wg, wu, wd, saved, tb, th, tag, plan):
    hn, a, b, s = saved
    d = h.shape[1]
    fh = wd.shape[0]
    layer = tag[-1]
    gr = plan.grads
    da, db = _ffn_ds(dhb, wd, a, b, tm=th // 2, tn=fh // 2, name=f"{tag}_b_ds")
    gr["down" + layer] = _matmul(s, dhb, mode="tn", tm=fh // 2, tn=d, tk=th, out_dtype=bf16, name=f"{tag}_b_dwd")
    dh2, dh2b, dnw = plan.call(f"{tag}_b_dhn", functools.partial(_dhn_norm_bwd, tm=th // 2, tk=fh // 2),
                               [(da, wg), (db, wu)], h, nw, dh, n_out=3)
    gr["gate" + layer] = _matmul(hn, da, mode="tn", tm=d, tn=fh // 2, tk=th, out_dtype=bf16, name=f"{tag}_b_dwg")
    gr["up" + layer] = _matmul(hn, db, mode="tn", tm=d, tn=fh // 2, tk=th, out_dtype=bf16, name=f"{tag}_b_dwu")
    return dh2, dh2b, dnw


class _Plan:
    GATHERS = {"ret_proj": ("ret_out", "gate0"), "ret_prep": ("up0",), "ret_scan": ("down0", "dn_in_top"),
               "ffn0_gu": ("dn_in_bottom",), "dn_proj": ("dn_out",), "dn_conv": ("gate1",),
               "dn_prep": ("up1", "down1")}
    SCATTERS = {"ffn1_b_dhn": ("down1",), "dn_b_conv_a": ("gate1", "up1", "dn_out"), "ffn0_b_dhn": ("dn_in",),
                "ret_b_scan": ("gate0", "up0"), "ret_b_dwin": ("down0", "ret_out"), "ret_b_dhn": ("ret_in",)}

    def __init__(self, shards, wts):
        self.shards, self.wts, self.grads, self.parts = shards, wts, {}, {}

    def _exchange(self, stage):
        if self.shards is None:
            return None
        if stage in self.GATHERS:
            return _Exchange([self.shards[n] for n in self.GATHERS[stage]], True)
        if stage in self.SCATTERS:
            return _Exchange([self._dev_major(n) for n in self.SCATTERS[stage]], False)
        return None

    def _dev_major(self, name):
        g = self.grads
        if name[:-1] in ("gate", "up"):
            return _dev_major_cols(g[name], g[name].shape[1] // N_DEV)
        if name[:-1] == "down":
            dwd = g[name]
            return dwd.reshape(N_DEV, dwd.shape[0] // N_DEV, dwd.shape[1])
        if name in ("ret_out", "dn_out"):
            return g[name].reshape(N_DEV, g[name].shape[0] // N_DEV, g[name].shape[1])
        return _dev_major_cols(g[name], self.shards[name].shape[-1])

    def _landed(self, stage, outs):
        if stage in self.SCATTERS:
            self.parts.update(zip(self.SCATTERS[stage], outs))
            return
        w = self.wts
        cols = lambda t: t.transpose(1, 0, 2).reshape(t.shape[1], N_DEV * t.shape[2])
        rows = lambda t: t.reshape(N_DEV * t.shape[1], t.shape[2])
        for name, t in zip(self.GATHERS[stage], outs):
            if name in ("ret_out", "dn_out") or name.startswith("down"):
                w[name] = rows(t)
            else:
                w[name] = cols(t)
        if "dn_in_top" in w and "dn_in_bottom" in w and "dn_main" not in w:
            full = jnp.concatenate([w["dn_in_top"], w["dn_in_bottom"]], axis=0)
            n_main = DN_CONV_CH + DN_V
            w["dn_main"] = full[:, :n_main]
            w["dn_ba"] = jnp.pad(full[:, n_main:], ((0, 0), (0, BA_W - (full.shape[1] - n_main))))

    def matmul(self, stage, a, b, **kw):
        comm = self._exchange(stage)
        if comm is None:
            return _matmul(a, b, name=stage, **kw)
        out, landed = _matmul(a, b, name=stage, comm=comm, **kw)
        self._landed(stage, landed)
        return out

    def call(self, stage, fn, *args, n_out):
        comm = self._exchange(stage)
        out = fn(*args, stage, comm=comm)
        if comm is not None:
            self._landed(stage, out[n_out:])
        return out[:n_out]


def _local_step(x2, target, wts, shards=None):
    plan = _Plan(shards, wts)
    s_len, d = x2.shape
    l = s_len + CHUNK
    tb = _tile(l, 3072)
    th = tb // 2 if (tb // 2) % 16 == 0 else tb
    half = RET_DK // 2
    inv_freq = (np.float32(ROPE_BASE) ** (-np.arange(half, dtype=np.float32) / np.float32(half))).astype(np.float32)
    ang = (np.arange(l) - PAD).astype(np.float32)[:, None] * inv_freq[None, :]
    cos, sin = jnp.asarray(np.cos(ang), f32), jnp.asarray(np.sin(ang), f32)
    lgs = jnp.log1p(-jnp.exp2(-5.0 - jnp.arange(RET_HEADS, dtype=f32)))
    gcs = jnp.exp(lgs * _ret_block(l))

    h0 = jnp.concatenate([jnp.zeros((PAD, d), f32), wts["meta"], x2], axis=0)
    mixw, ffnw = wts["mix_norm"], wts["ffn_norm"]

    hn0 = _rms_fwd(h0, mixw[0:1], "l0_norm")
    proj0 = plan.matmul("ret_proj", hn0, wts["ret_in"], mode="nn", tm=tb, tn=512, tk=d)
    (qk0,) = plan.call("ret_prep", _ret_prep, proj0, cos, sin, n_out=1)
    o0, st0, y0 = plan.call("ret_scan", _ret_scan_fwd, qk0, proj0, wts["ret_gn"], lgs, gcs, n_out=3)
    h1, hn1 = _matmul(y0, wts["ret_out"], mode="nn", tm=th // 2, tn=d, tk=RET_V, res=h0, norm_w=ffnw[0:1],
                      name="ret_out")
    (h2, hn2), ffn0 = _ffn_fwd(h1, hn1, wts["gate0"], wts["up0"], wts["down0"], tb, th, "ffn0", plan,
                               next_norm_w=mixw[1:2])

    proj1 = plan.matmul("dn_proj", hn2, wts["dn_main"], mode="nn", tm=tb, tn=512, tk=d)
    ba = _matmul(hn2, wts["dn_ba"], mode="nn", tm=tb, tn=BA_W, tk=d, name="dn_proj_ba")
    (qkv1,) = plan.call("dn_conv", _dn_conv_fwd, proj1, wts["conv_w"], n_out=1)
    tinv1, u1, wq1, pk1, egl1, kpt1, qwt1 = plan.call("dn_prep", _dn_prep, qkv1, ba, wts["a_log"], wts["dt_bias"],
                                                      n_out=7)
    o1, st1, vn1 = _dn_scan_fwd(u1, wq1, pk1, egl1, "dn_scan")
    y1 = _gnorm_fwd(o1, proj1, wts["dn_norm"], DN_HEADS, DN_DV, 2, "dn_gnorm")
    h3, hn3 = _matmul(y1, wts["dn_out"], mode="nn", tm=th // 2, tn=d, tk=DN_V, res=h2, norm_w=ffnw[1:2],
                      name="dn_out")
    h4, ffn1 = _ffn_fwd(h3, hn3, wts["gate1"], wts["up1"], wts["down1"], tb, th, "ffn1", plan)

    dh4, dh4b, dfinal, loss = _final_loss(h4, wts["final_norm"], target, "final_loss")
    gr = plan.grads
    dh3, dh3b, dffn1 = _ffn_bwd(dh4, dh4b, h3, ffnw[1:2], wts["gate1"], wts["up1"], wts["down1"], ffn1,
                                tb, th, "ffn1", plan)

    gr["dn_out"] = _matmul(y1, dh3b, mode="tn", tm=1024, tn=d, tk=tb, out_dtype=bf16, name="dn_b_dwout")
    do1, dproj1, ddn_norm = _dy_gnorm_bwd(dh3b, wts["dn_out"], o1, proj1, wts["dn_norm"], DN_DV, 2, "dn_b_gnorm",
                                          tm=th // 2, tn=1024)
    dvn1, dsp1 = _dn_scan_bwd(do1, kpt1, qwt1, egl1, "dn_b_scan")
    dqkv1, dba, dalog, ddt = _dn_post_bwd(qkv1, ba, wts["a_log"], wts["dt_bias"], st1, dsp1, tinv1, u1, wq1, vn1,
                                          do1, dvn1, "dn_b_post")
    dc1, dconv = plan.call("dn_b_conv_a", _dn_conv_bwd_a, proj1, wts["conv_w"], dqkv1, n_out=2)
    dproj1 = _dn_conv_bwd_b(dc1, wts["conv_w"], dproj1, "dn_b_conv_b")
    dbab = dba.astype(bf16)
    n_main = dproj1.shape[1]
    dhn2_ba = _matmul(dbab, wts["dn_ba"], mode="nt", tm=th, tn=d, tk=BA_W, name="dn_b_dhn_ba")
    dh2, dh2b, dmix1 = plan.call("dn_b_dhn", functools.partial(_dhn_norm_bwd, tm=th // 2, tk=n_main // 4,
                                                              init=dhn2_ba),
                                 [(dproj1, wts["dn_main"])], h2, mixw[1:2], dh3, n_out=3)
    dw_main = _matmul(hn2, dproj1, mode="tn", tm=d, tn=512, tk=tb, out_dtype=bf16, name="dn_b_dwin")
    dw_ba = _matmul(hn2, dbab, mode="tn", tm=d, tn=BA_W, tk=tb, out_dtype=bf16, name="dn_b_dwin_ba")
    gr["dn_in"] = jnp.concatenate([dw_main, dw_ba], axis=1)

    dh1, dh1b, dffn0 = _ffn_bwd(dh2, dh2b, h1, ffnw[0:1], wts["gate0"], wts["up0"], wts["down0"], ffn0,
                                tb, th, "ffn0", plan)

    gr["ret_out"] = _matmul(y0, dh1b, mode="tn", tm=1024, tn=d, tk=tb, out_dtype=bf16, name="ret_b_dwout")
    do0, dproj0, dret_gn = _dy_gnorm_bwd(dh1b, wts["ret_out"], o0, proj0, wts["ret_gn"], RET_DV, 2, "ret_b_gnorm",
                                         tm=th // 2, tn=1024)
    dq0, dk0, dproj0 = plan.call("ret_b_scan", _ret_scan_bwd, qk0, proj0, st0, do0, dproj0, lgs, gcs, n_out=3)
    dproj0 = _ret_prep_bwd(dq0, dk0, cos, sin, dproj0, "ret_b_prep")
    n_in = dproj0.shape[1]
    gr["ret_in"] = plan.matmul("ret_b_dwin", hn0, dproj0, mode="tn", tm=d, tn=512, tk=tb, out_dtype=bf16)
    dh0, _, dmix0 = plan.call("ret_b_dhn", functools.partial(_dhn_norm_bwd, tm=th // 2, tk=n_in // 4),
                              [(dproj0, wts["ret_in"])], h0, mixw[0:1], dh1, n_out=3)

    gr.update(meta=dh0[PAD:CHUNK], mix_norm=jnp.concatenate([dmix0, dmix1], axis=0),
              ffn_norm=jnp.concatenate([dffn0, dffn1], axis=0), ret_gn=dret_gn, conv_w=dconv, a_log=dalog,
              dt_bias=ddt, dn_norm=ddn_norm, final_norm=dfinal)
    return loss, dh0[CHUNK:], gr, plan


def _adamw_reduce(parts, w, m, v, name):
    _, r, c = parts.shape
    c_pad = -(-c // LANES) * LANES
    tr = _div_tile(r, max(8, (3 * MIB // 16) // c_pad // 8 * 8), 16)

    def body(p_ref, w_ref, m_ref, v_ref, g_ref, d_ref, nm_ref, nv_ref):
        g = p_ref[0].astype(f32)
        for s in range(1, N_DEV):
            g = g + p_ref[s].astype(f32)
        mm = ADAM_B1 * m_ref[...] + (1.0 - ADAM_B1) * g
        vv = ADAM_B2 * v_ref[...] + (1.0 - ADAM_B2) * (g * g)
        m_hat = mm / (1.0 - ADAM_B1 ** ADAM_STEP)
        v_hat = vv / (1.0 - ADAM_B2 ** ADAM_STEP)
        g_ref[...] = g
        d_ref[...] = -ADAM_LR * (m_hat / (jnp.sqrt(v_hat) + ADAM_EPS) + ADAM_WD * w_ref[...])
        nm_ref[...] = mm
        nv_ref[...] = vv

    blk = pl.BlockSpec((tr, c), lambda i: (i, 0))
    return pl.pallas_call(
        body, name=name, grid=(r // tr,),
        in_specs=[pl.BlockSpec((N_DEV, tr, c), lambda i: (0, i, 0)), blk, blk, blk], out_specs=[blk] * 4,
        out_shape=[jax.ShapeDtypeStruct((r, c), f32)] * 4,
        compiler_params=_params(("parallel",), 48),
    )(parts, w, m, v)


def _dev_major_cols(g, width):
    r = g.shape[0]
    return g[:, :N_DEV * width].reshape(r, N_DEV, width).transpose(1, 0, 2)


def kernel(x, meta_tokens, mix_norm_w, ffn_norm_w, ret_w_in, ret_gn_w, ret_w_out, dn_w_in, dn_conv_w, dn_a_log, dn_dt_bias, dn_norm_w, dn_w_out, ffn_w_gate, ffn_w_up, ffn_w_down, final_norm_w, loss_target, m_meta_tokens, m_mix_norm_w, m_ffn_norm_w, m_ret_w_in, m_ret_gn_w, m_ret_w_out, m_dn_w_in, m_dn_conv_w, m_dn_a_log, m_dn_dt_bias, m_dn_norm_w, m_dn_w_out, m_ffn_w_gate, m_ffn_w_up, m_ffn_w_down, m_final_norm_w, v_meta_tokens, v_mix_norm_w, v_ffn_norm_w, v_ret_w_in, v_ret_gn_w, v_ret_w_out, v_dn_w_in, v_dn_conv_w, v_dn_a_log, v_dn_dt_bias, v_dn_norm_w, v_dn_w_out, v_ffn_w_gate, v_ffn_w_up, v_ffn_w_down, v_final_norm_w):
    d = x.shape[-1]
    me = 4 * lax.axis_index("x") + 2 * lax.axis_index("y") + lax.axis_index("c")

    shards = dict(ret_in=ret_w_in[0].astype(bf16), ret_out=ret_w_out[0].astype(bf16),
                  dn_in=dn_w_in[0].astype(bf16), dn_out=dn_w_out[0].astype(bf16))
    shards["dn_in_top"], shards["dn_in_bottom"] = shards["dn_in"][:d // 2], shards["dn_in"][d // 2:]
    for layer in (0, 1):
        shards[f"gate{layer}"] = ffn_w_gate[layer].astype(bf16)
        shards[f"up{layer}"] = ffn_w_up[layer].astype(bf16)
        shards[f"down{layer}"] = ffn_w_down[layer].astype(bf16)
    g_ret_in, g_meta, g_conv, g_dnn = _exchange([shards["ret_in"], meta_tokens, dn_conv_w[0], dn_norm_w], True,
                                                "gather_first")
    cols = lambda g: g.transpose(1, 0, 2).reshape(g.shape[1], N_DEV * g.shape[2])
    wts = dict(meta=cols(g_meta), mix_norm=mix_norm_w, ffn_norm=ffn_norm_w, ret_in=cols(g_ret_in), ret_gn=ret_gn_w,
               conv_w=cols(g_conv), a_log=dn_a_log, dt_bias=dn_dt_bias, dn_norm=cols(g_dnn),
               final_norm=final_norm_w.reshape(1, d))

    loss_part, grad_x, gr, plan = _local_step(x[0], loss_target[0], wts, shards)
    loss = lax.psum(loss_part[0, 0], AXES)

    pp = plan.parts
    both = lambda name: jnp.concatenate([pp[name + "0"], pp[name + "1"]], axis=1)
    big_parts = [pp["ret_in"], pp["ret_out"], pp["dn_in"], pp["dn_out"], both("gate"), both("up"), both("down")]
    big_names = ["ret_w_in", "ret_w_out", "dn_w_in", "dn_w_out", "ffn_w_gate", "ffn_w_up", "ffn_w_down"]
    big_w = [ret_w_in, ret_w_out, dn_w_in, dn_w_out, ffn_w_gate, ffn_w_up, ffn_w_down]
    big_m = [m_ret_w_in, m_ret_w_out, m_dn_w_in, m_dn_w_out, m_ffn_w_gate, m_ffn_w_up, m_ffn_w_down]
    big_v = [v_ret_w_in, v_ret_w_out, v_dn_w_in, v_dn_w_out, v_ffn_w_gate, v_ffn_w_up, v_ffn_w_down]
    res = {}
    for nm, parts, w_, m_, v_ in zip(big_names, big_parts, big_w, big_m, big_v):
        r2, c2 = parts.shape[1], parts.shape[2]
        outs = _adamw_reduce(parts, w_.reshape(r2, c2), m_.reshape(r2, c2), v_.reshape(r2, c2), f"adamw_{nm}")
        res[nm] = [o.reshape(w_.shape) for o in outs]

    small_names = ["meta_tokens", "mix_norm_w", "ffn_norm_w", "ret_gn_w", "dn_conv_w", "dn_a_log", "dn_dt_bias",
                   "dn_norm_w", "final_norm_w"]
    small_g = [gr["meta"], gr["mix_norm"], gr["ffn_norm"], gr["ret_gn"], gr["conv_w"], gr["a_log"], gr["dt_bias"],
               gr["dn_norm"], gr["final_norm"]]
    small_w = [meta_tokens, mix_norm_w, ffn_norm_w, ret_gn_w, dn_conv_w, dn_a_log, dn_dt_bias, dn_norm_w, final_norm_w]
    small_m = [m_meta_tokens, m_mix_norm_w, m_ffn_norm_w, m_ret_gn_w, m_dn_conv_w, m_dn_a_log, m_dn_dt_bias,
               m_dn_norm_w, m_final_norm_w]
    small_v = [v_meta_tokens, v_mix_norm_w, v_ffn_norm_w, v_ret_gn_w, v_dn_conv_w, v_dn_a_log, v_dn_dt_bias,
               v_dn_norm_w, v_final_norm_w]
    sharded = {"meta_tokens", "dn_conv_w", "dn_norm_w"}
    flat = jnp.concatenate([g.reshape(-1) for g in small_g])
    row = 8 * LANES
    n_flat = flat.shape[0]
    flat = jnp.pad(flat, (0, -n_flat % row)).reshape(-1, row)
    (gathered,) = _exchange([flat], True, "gather_small_grads")
    gathered = gathered.reshape(N_DEV, -1)
    pieces, off = [], 0
    for nm, g, w_ in zip(small_names, small_g, small_w):
        full = gathered[:, off:off + g.size].reshape((N_DEV,) + g.shape)
        off += g.size
        if nm in sharded:
            wloc = w_.shape[-1]
            full = lax.dynamic_slice_in_dim(full, me * wloc, wloc, axis=full.ndim - 1)
        pieces.append(full.reshape(N_DEV, -1))
    sizes = [p.shape[1] for p in pieces]
    n_loc = sum(sizes)
    pad_loc = -n_loc % row

    def pack(vs, lead):
        cat = jnp.concatenate([a.reshape(lead + (-1,)) for a in vs], axis=-1)
        cat = jnp.pad(cat, [(0, 0)] * len(lead) + [(0, pad_loc)])
        return cat.reshape(lead + (-1, row))

    outs = _adamw_reduce(pack(pieces, (N_DEV,)), pack(small_w, ()), pack(small_m, ()), pack(small_v, ()), "adamw_small")
    off = 0
    for nm, sz, w_ in zip(small_names, sizes, small_w):
        res[nm] = [o.reshape(-1)[off:off + sz].reshape(w_.shape) for o in outs]
        off += sz

    order = ["meta_tokens", "mix_norm_w", "ffn_norm_w", "ret_w_in", "ret_gn_w", "ret_w_out", "dn_w_in", "dn_conv_w",
             "dn_a_log", "dn_dt_bias", "dn_norm_w", "dn_w_out", "ffn_w_gate", "ffn_w_up", "ffn_w_down", "final_norm_w"]
    grad_x = grad_x.reshape(x.shape)
    return (loss, grad_x, *[res[nm][0] for nm in order], *[res[nm][1] for nm in order],
            *[res[nm][2] for nm in order], *[res[nm][3] for nm in order])
```

```python
import functools
import math

import jax
import jax.numpy as jnp
import numpy as np
from jax import lax
from jax.experimental import pallas as pl
from jax.experimental.pallas import tpu as pltpu

f32 = jnp.float32
bf16 = jnp.bfloat16

N_META = 16
CHUNK = 64
PAD = CHUNK - N_META
RMS_EPS = 1e-6
RET_HEADS, RET_DK, RET_DV = 4, 256, 512
RET_QK, RET_V = RET_HEADS * RET_DK, RET_HEADS * RET_DV
DN_HEADS, DN_DK, DN_DV = 8, 128, 256
DN_QK, DN_V = DN_HEADS * DN_DK, DN_HEADS * DN_DV
DN_CONV_CH = 2 * DN_QK + DN_V
CONV_K = 4
ROPE_BASE = 10000.0
ADAM_LR, ADAM_B1, ADAM_B2, ADAM_EPS, ADAM_WD, ADAM_STEP = 0.001, 0.9, 0.999, 1e-08, 0.01, 10
N_DEV = 8
AXES = ("x", "y", "c")
LANES = 128
MIB = 1024 * 1024


def _tile(n_rows, cap):
    nch = n_rows // CHUNK
    best = 1
    for d in range(1, nch + 1):
        if nch % d == 0 and d * CHUNK <= cap:
            best = d
    return best * CHUNK


def _div_tile(n, cap, align):
    best = None
    for d in range(align, min(n, cap) + 1, align):
        if n % d == 0:
            best = d
    return best if best is not None else n


def _params(sem, vmem_mb):
    return pltpu.CompilerParams(dimension_semantics=sem, vmem_limit_bytes=int(vmem_mb * MIB))


def _nn(a, b, precision=None):
    return jnp.dot(a, b, preferred_element_type=f32, precision=precision)


def _nt(a, b, precision=None):
    return lax.dot_general(a, b, (((1,), (1,)), ((), ())), preferred_element_type=f32, precision=precision)


def _tn(a, b, precision=None):
    return lax.dot_general(a, b, (((0,), (0,)), ((), ())), preferred_element_type=f32, precision=precision)


def _b(x):
    return x.astype(bf16)


def _sigmoid(x):
    return 0.5 * jnp.tanh(0.5 * x) + 0.5


def _silu(x):
    return x * _sigmoid(x)


def _dsilu(x):
    s = _sigmoid(x)
    return s * (1.0 + x * (1.0 - s))


def _peer(k):
    x, y, c = lax.axis_index("x"), lax.axis_index("y"), lax.axis_index("c")
    px = 1 - x if k & 4 else x
    py = 1 - y if k & 2 else y
    pc = 1 - c if k & 1 else c
    return (px, py, pc), 4 * px + 2 * py + pc


class _Exchange:
    def __init__(self, arrs, gather):
        self.arrs, self.gather, self.n = list(arrs), gather, len(arrs)
        self.out_shapes = [jax.ShapeDtypeStruct(((N_DEV,) + a.shape) if gather else a.shape, a.dtype) for a in arrs]
        self.specs = [pl.BlockSpec(memory_space=pltpu.HBM)] * self.n
        self.scratch = [pltpu.SemaphoreType.DMA((self.n, N_DEV - 1)), pltpu.SemaphoreType.DMA((self.n, N_DEV - 1)),
                        pltpu.SemaphoreType.DMA((self.n,))]

    def _copies(self, ins, outs, sems):
        send_sems, recv_sems, local_sems = sems
        me = 4 * lax.axis_index("x") + 2 * lax.axis_index("y") + lax.axis_index("c")
        src = (lambda a, dest: ins[a]) if self.gather else (lambda a, dest: ins[a].at[dest])
        local = [pltpu.make_async_copy(src(a, me), outs[a].at[me], local_sems.at[a]) for a in range(self.n)]
        sends, lands = [], []
        for k in range(1, N_DEV):
            peer, pidx = _peer(k)
            for a in range(self.n):
                for dst, lst in ((outs[a].at[me], sends), (outs[a].at[pidx], lands)):
                    lst.append(pltpu.make_async_remote_copy(
                        src_ref=src(a, pidx), dst_ref=dst, send_sem=send_sems.at[a, k - 1],
                        recv_sem=recv_sems.at[a, k - 1], device_id=peer, device_id_type=pl.DeviceIdType.MESH))
        return local, sends, lands

    def start(self, ins, outs, sems):
        local, sends, _ = self._copies(ins, outs, sems)
        for cp in local + sends:
            cp.start()

    def wait(self, ins, outs, sems):
        local, sends, lands = self._copies(ins, outs, sems)
        for cp in lands:
            cp.wait_recv()
        for cp in sends:
            cp.wait_send()
        for cp in local:
            cp.wait()


def _call(body, args, *, name, grid, in_specs, out_specs, out_shape, scratch=(), sem, vmem_mb, comm=None,
          aliases=None):
    aliases = aliases or {}
    if comm is None:
        out = pl.pallas_call(body, name=name, grid=grid, in_specs=list(in_specs), out_specs=list(out_specs),
                             out_shape=list(out_shape), scratch_shapes=list(scratch), input_output_aliases=aliases,
                             compiler_params=_params(sem, vmem_mb))(*args)
        return list(out)
    n_in, n_out, n_scr, nc = len(args), len(out_shape), len(scratch), comm.n

    def carried(*refs):
        ins, cin = refs[:n_in], refs[n_in:n_in + nc]
        o0 = n_in + nc
        outs, cout = refs[o0:o0 + n_out], refs[o0 + n_out:o0 + n_out + nc]
        s0 = o0 + n_out + nc
        scr, sems = refs[s0:s0 + n_scr], refs[s0 + n_scr:]
        first = functools.reduce(jnp.logical_and, [pl.program_id(i) == 0 for i in range(len(grid))])
        last = functools.reduce(jnp.logical_and, [pl.program_id(i) == grid[i] - 1 for i in range(len(grid))])

        @pl.when(first)
        def _():
            comm.start(cin, cout, sems)

        body(*ins, *outs, *scr)

        @pl.when(last)
        def _():
            comm.wait(cin, cout, sems)

    out = pl.pallas_call(
        carried, name=name, grid=grid, in_specs=list(in_specs) + comm.specs, out_specs=list(out_specs) + comm.specs,
        out_shape=list(out_shape) + comm.out_shapes, scratch_shapes=list(scratch) + comm.scratch,
        input_output_aliases=aliases,
        compiler_params=_params(("arbitrary",) * len(grid), vmem_mb))(*args, *comm.arrs)
    return list(out)


def _exchange(arrs, gather, name):
    comm = _Exchange(arrs, gather)

    def body(*refs):
        ins, outs, sems = refs[:comm.n], refs[comm.n:2 * comm.n], refs[2 * comm.n:]
        comm.start(ins, outs, sems)
        comm.wait(ins, outs, sems)

    return pl.pallas_call(body, name=name, in_specs=comm.specs, out_specs=comm.specs, out_shape=comm.out_shapes,
                          scratch_shapes=comm.scratch)(*comm.arrs)


def _matmul(a, b, *, mode, tm, tn, tk, name, out_dtype=f32, res=None, vmem_mb=48, comm=None, norm_w=None):
    if mode == "nn":
        (m, k), (k2, n) = a.shape, b.shape
    elif mode == "nt":
        (m, k), (n, k2) = a.shape, b.shape
    else:
        (k, m), (k2, n) = a.shape, b.shape
    assert k == k2 and m % tm == 0 and n % tn == 0 and k % tk == 0, (name, a.shape, b.shape, tm, tn, tk)
    nk = k // tk
    dot = {"nn": _nn, "nt": _nt, "tn": _tn}[mode]
    a_spec = {"nn": pl.BlockSpec((tm, tk), lambda i, j, kk: (i, kk)),
              "nt": pl.BlockSpec((tm, tk), lambda i, j, kk: (i, kk)),
              "tn": pl.BlockSpec((tk, tm), lambda i, j, kk: (kk, i))}[mode]
    b_spec = {"nn": pl.BlockSpec((tk, tn), lambda i, j, kk: (kk, j)),
              "nt": pl.BlockSpec((tn, tk), lambda i, j, kk: (j, kk)),
              "tn": pl.BlockSpec((tk, tn), lambda i, j, kk: (kk, j))}[mode]
    o_spec = pl.BlockSpec((tm, tn), lambda i, j, kk: (i, j))
    has_res = res is not None
    has_norm = norm_w is not None
    assert not has_norm or tn == n
    n_ops = 2 + has_res + has_norm

    def body(*refs):
        a_ref, b_ref = refs[:2]
        r_ref = refs[2] if has_res else None
        nw_ref = refs[2 + has_res] if has_norm else None
        o_ref = refs[n_ops]
        hn_ref = refs[n_ops + 1] if has_norm else None
        rest = refs[n_ops + 1 + has_norm:]

        def finish(tot):
            if has_res:
                tot = tot + r_ref[...]
            o_ref[...] = tot.astype(out_dtype)
            if has_norm:
                r = lax.rsqrt(jnp.mean(tot * tot, axis=-1, keepdims=True) + RMS_EPS)
                hn_ref[...] = _b(tot * r * nw_ref[...])

        if nk == 1:
            finish(dot(_b(a_ref[...]), _b(b_ref[...])))
            return
        acc_ref = rest[0]
        kk = pl.program_id(2)

        @pl.when(kk == 0)
        def _():
            acc_ref[...] = dot(_b(a_ref[...]), _b(b_ref[...]))

        @pl.when(kk > 0)
        def _():
            acc_ref[...] += dot(_b(a_ref[...]), _b(b_ref[...]))

        @pl.when(kk == nk - 1)
        def _():
            finish(acc_ref[...])

    in_specs = [a_spec, b_spec]
    args = (a, b)
    if has_res:
        in_specs.append(o_spec)
        args += (res,)
    out_specs, out_shape = [o_spec], [jax.ShapeDtypeStruct((m, n), out_dtype)]
    if has_norm:
        in_specs.append(pl.BlockSpec((1, tn), lambda i, j, kk: (0, j)))
        args += (norm_w,)
        out_specs.append(o_spec)
        out_shape.append(jax.ShapeDtypeStruct((m, n), bf16))
    out = _call(body, args, name=name, grid=(m // tm, n // tn, nk), in_specs=in_specs, out_specs=out_specs,
                out_shape=out_shape, scratch=[pltpu.VMEM((tm, tn), f32)] if nk > 1 else [],
                sem=("parallel", "parallel", "arbitrary"), vmem_mb=vmem_mb, comm=comm)
    n_own = len(out_shape)
    own = out[0] if n_own == 1 else tuple(out[:n_own])
    return own if comm is None else (own, out[n_own:])


def _rms_fwd(h, w, name):
    l, d = h.shape
    tr = _tile(l, 512)

    def body(h_ref, w_ref, o_ref):
        x = h_ref[...]
        r = lax.rsqrt(jnp.mean(x * x, axis=-1, keepdims=True) + RMS_EPS)
        o_ref[...] = _b(x * r * w_ref[...])

    return pl.pallas_call(
        body, name=name, grid=(l // tr,),
        in_specs=[pl.BlockSpec((tr, d), lambda i: (i, 0)), pl.BlockSpec((1, d), lambda i: (0, 0))],
        out_specs=pl.BlockSpec((tr, d), lambda i: (i, 0)),
        out_shape=jax.ShapeDtypeStruct((l, d), bf16),
        compiler_params=_params(("parallel",), 32),
    )(h, w)


def _dhn_norm_bwd(pairs, h, nw, dres, name, *, tm, tk, init=None, comm=None):
    l, d = h.shape
    nks = [a.shape[1] // tk for a, _ in pairs]
    starts = [sum(nks[:p]) for p in range(len(pairs))]
    nk = sum(nks)
    assert nk >= 2
    n_ops = 2 * len(pairs)
    has_init = init is not None

    def body(*refs):
        ops = refs[:n_ops]
        init_ref = refs[n_ops] if has_init else None
        h_ref, w_ref, r_ref, dh_ref, dhb_ref, dw_ref, acc_ref = refs[n_ops + has_init:]
        i, kk = pl.program_id(0), pl.program_id(1)

        @pl.when(kk == 0)
        def _():
            part = _nt(ops[0][...], ops[1][...])
            acc_ref[...] = part + init_ref[...] if has_init else part

        for p in range(len(pairs)):
            lo, hi = max(starts[p], 1), min(starts[p] + nks[p], nk - 1)

            @pl.when(jnp.logical_and(kk >= lo, kk < hi))
            def _(a_ref=ops[2 * p], b_ref=ops[2 * p + 1]):
                acc_ref[...] += _nt(a_ref[...], b_ref[...])

        @pl.when(kk == nk - 1)
        def _():
            g = acc_ref[...] + _nt(ops[-2][...], ops[-1][...])
            x = h_ref[...]
            r = lax.rsqrt(jnp.mean(x * x, axis=-1, keepdims=True) + RMS_EPS)
            xh = x * r
            dxh = g * w_ref[...]
            dh = r_ref[...] + r * (dxh - xh * jnp.mean(dxh * xh, axis=-1, keepdims=True))
            dh_ref[...] = dh
            dhb_ref[...] = _b(dh)
            dw = jnp.sum(g * xh, axis=0, keepdims=True)

            @pl.when(i == 0)
            def _():
                dw_ref[...] = dw

            @pl.when(i > 0)
            def _():
                dw_ref[...] += dw

    def k_of(p):
        return lambda kk: jnp.clip(kk - starts[p], 0, nks[p] - 1)

    in_specs, args = [], []
    for p, (a, b) in enumerate(pairs):
        in_specs += [pl.BlockSpec((tm, tk), functools.partial(lambda i, kk, f: (i, f(kk)), f=k_of(p))),
                     pl.BlockSpec((d, tk), functools.partial(lambda i, kk, f: (0, f(kk)), f=k_of(p)))]
        args += [a, b]
    row = pl.BlockSpec((tm, d), lambda i, kk: (i, 0))
    vec = pl.BlockSpec((1, d), lambda i, kk: (0, 0))
    if has_init:
        in_specs.append(row)
        args.append(init)
    return _call(body, tuple(args) + (h, nw, dres), name=name, grid=(l // tm, nk), in_specs=in_specs + [row, vec, row],
                 out_specs=[row, row, vec],
                 out_shape=[jax.ShapeDtypeStruct((l, d), f32), jax.ShapeDtypeStruct((l, d), bf16),
                            jax.ShapeDtypeStruct((1, d), f32)],
                 scratch=[pltpu.VMEM((tm, d), f32)], sem=("arbitrary", "arbitrary"), vmem_mb=48, comm=comm)


def _final_loss(h, w, target, name):
    l, d = h.shape
    nch = l // CHUNK
    cpt = _tile(l, 256) // CHUNK
    nt = nch // cpt

    def body(h_ref, w_ref, *rest):
        t_refs, (dh_ref, dhb_ref, dw_ref, loss_ref) = rest[:cpt], rest[cpt:]
        i = pl.program_id(0)
        wv = w_ref[...]
        dw = jnp.zeros((1, d), f32)
        part = jnp.zeros((1, 1), f32)
        for c in range(cpt):
            rows = slice(c * CHUNK, (c + 1) * CHUNK)
            live = (i * cpt + c > 0).astype(f32)
            x = h_ref[rows, :]
            r = lax.rsqrt(jnp.mean(x * x, axis=-1, keepdims=True) + RMS_EPS)
            xh = x * r
            err = (xh * wv - t_refs[c][...]) * live
            dy = err * (1.0 / d)
            dxh = dy * wv
            dx = r * (dxh - xh * jnp.mean(dxh * xh, axis=-1, keepdims=True))
            dh_ref[rows, :] = dx
            dhb_ref[rows, :] = _b(dx)
            dw = dw + jnp.sum(dy * xh, axis=0, keepdims=True)
            part = part + 0.5 * jnp.sum(jnp.sum(err * err, axis=-1, keepdims=True) * (1.0 / d), axis=0, keepdims=True)
        part = jnp.broadcast_to(part, (1, LANES))

        @pl.when(i == 0)
        def _():
            dw_ref[...] = dw
            loss_ref[...] = part

        @pl.when(i > 0)
        def _():
            dw_ref[...] += dw
            loss_ref[...] += part

    row = pl.BlockSpec((cpt * CHUNK, d), lambda i: (i, 0))
    vec = pl.BlockSpec((1, d), lambda i: (0, 0))
    t_specs = [pl.BlockSpec((CHUNK, d), functools.partial(lambda i, c: (jnp.maximum(i * cpt + c - 1, 0), 0), c=c))
               for c in range(cpt)]
    return pl.pallas_call(
        body, name=name, grid=(nt,),
        in_specs=[row, vec] + t_specs,
        out_specs=[row, row, vec, pl.BlockSpec((1, LANES), lambda i: (0, 0))],
        out_shape=[jax.ShapeDtypeStruct((l, d), f32), jax.ShapeDtypeStruct((l, d), bf16),
                   jax.ShapeDtypeStruct((1, d), f32), jax.ShapeDtypeStruct((1, LANES), f32)],
        compiler_params=_params(("arbitrary",), 32),
    )(h, w, *([target] * cpt))


def _ffn_gu(hn, wg, wu, name, *, tm, tn, comm=None):
    l, d = hn.shape
    fh = wg.shape[1]

    def body(h_ref, g_ref, u_ref, a_ref, b_ref, s_ref):
        hb = h_ref[...]
        a = _nn(hb, g_ref[...])
        bb = _nn(hb, u_ref[...])
        a_ref[...] = _b(a)
        b_ref[...] = _b(bb)
        s_ref[...] = _b(_silu(a) * bb)

    wspec = pl.BlockSpec((d, tn), lambda i, j: (0, j))
    ospec = pl.BlockSpec((tm, tn), lambda i, j: (i, j))
    return _call(body, (hn, wg, wu), name=name, grid=(l // tm, fh // tn),
                 in_specs=[pl.BlockSpec((tm, d), lambda i, j: (i, 0)), wspec, wspec], out_specs=[ospec] * 3,
                 out_shape=[jax.ShapeDtypeStruct((l, fh), bf16)] * 3, sem=("parallel", "parallel"), vmem_mb=48,
                 comm=comm)


def _ffn_ds(dhb, wd, a, b, *, tm, tn, name):
    l, d = dhb.shape
    fh = wd.shape[0]

    def body(g_ref, w_ref, a_ref, b_ref, da_ref, db_ref):
        ds = _nt(g_ref[...], w_ref[...])
        a = a_ref[...].astype(f32)
        da_ref[...] = _b(ds * b_ref[...].astype(f32) * _dsilu(a))
        db_ref[...] = _b(ds * _silu(a))

    ospec = pl.BlockSpec((tm, tn), lambda i, j: (i, j))
    return pl.pallas_call(
        body, name=name, grid=(l // tm, fh // tn),
        in_specs=[pl.BlockSpec((tm, d), lambda i, j: (i, 0)), pl.BlockSpec((tn, d), lambda i, j: (j, 0)), ospec, ospec],
        out_specs=[ospec, ospec], out_shape=[jax.ShapeDtypeStruct((l, fh), bf16)] * 2,
        compiler_params=_params(("parallel", "parallel"), 48),
    )(dhb, wd, a, b)


def _gnorm_fwd(o, proj, nw, heads, dv, gate_blk, name):
    l, hv = o.shape
    tr = _tile(l, 256)

    def body(o_ref, g_ref, w_ref, y_ref):
        wv = w_ref[...]
        for h in range(heads):
            sl = slice(h * dv, (h + 1) * dv)
            oh = o_ref[:, sl]
            r = lax.rsqrt(jnp.mean(oh * oh, axis=-1, keepdims=True) + RMS_EPS)
            y_ref[:, sl] = _b(oh * r * wv * _silu(g_ref[:, sl]))

    return pl.pallas_call(
        body, name=name, grid=(l // tr,),
        in_specs=[pl.BlockSpec((tr, hv), lambda i: (i, 0)), pl.BlockSpec((tr, hv), lambda i: (i, gate_blk)),
                  pl.BlockSpec((1, dv), lambda i: (0, 0))],
        out_specs=pl.BlockSpec((tr, hv), lambda i: (i, 0)),
        out_shape=jax.ShapeDtypeStruct((l, hv), bf16),
        compiler_params=_params(("parallel",), 32),
    )(o, proj, nw)


def _dy_gnorm_bwd(dhb, w_out, o, proj, nw, dv, gate_blk, name, *, tm, tn):
    l, hv = o.shape
    d = dhb.shape[1]
    nj = hv // tn
    heads = tn // dv

    def body(g_ref, w_ref, o_ref, gate_ref, nw_ref, do_ref, dg_ref, dw_ref):
        dy = _nt(g_ref[...], w_ref[...])
        wv = nw_ref[...]
        dw = jnp.zeros((1, dv), f32)
        for h in range(heads):
            sl = slice(h * dv, (h + 1) * dv)
            oh = o_ref[:, sl]
            g = gate_ref[:, sl]
            dyh = dy[:, sl]
            r = lax.rsqrt(jnp.mean(oh * oh, axis=-1, keepdims=True) + RMS_EPS)
            xh = oh * r
            dn = dyh * _silu(g)
            dg_ref[:, sl] = _b(dyh * (xh * wv) * _dsilu(g))
            dxh = dn * wv
            do_ref[:, sl] = r * (dxh - xh * jnp.mean(dxh * xh, axis=-1, keepdims=True))
            dw = dw + jnp.sum(dn * xh, axis=0, keepdims=True)
        first = jnp.logical_and(pl.program_id(0) == 0, pl.program_id(1) == 0)

        @pl.when(first)
        def _():
            dw_ref[...] = dw

        @pl.when(jnp.logical_not(first))
        def _():
            dw_ref[...] += dw

    tile = pl.BlockSpec((tm, tn), lambda i, j: (i, j))
    gate = pl.BlockSpec((tm, tn), lambda i, j: (i, gate_blk * nj + j))
    vec = pl.BlockSpec((1, dv), lambda i, j: (0, 0))
    return pl.pallas_call(
        body, name=name, grid=(l // tm, nj),
        in_specs=[pl.BlockSpec((tm, d), lambda i, j: (i, 0)), pl.BlockSpec((tn, d), lambda i, j: (j, 0)),
                  tile, gate, vec],
        out_specs=[tile, gate, vec],
        out_shape=[jax.ShapeDtypeStruct((l, hv), f32), jax.ShapeDtypeStruct(proj.shape, bf16),
                   jax.ShapeDtypeStruct((1, dv), f32)],
        compiler_params=_params(("arbitrary", "arbitrary"), 48),
    )(dhb, w_out, o, proj, nw)


def _ret_prep(proj, cos, sin, name, comm=None):
    l = proj.shape[0]
    tr = _tile(l, 256)
    half = RET_DK // 2
    scale = RET_DK ** -0.5

    def body(p_ref, c_ref, s_ref, o_ref):
        rows = pl.program_id(0) * tr + lax.broadcasted_iota(jnp.int32, (tr, 1), 0)
        kmul = jnp.where(rows >= PAD, scale, 0.0).astype(f32)
        c, s = c_ref[...], s_ref[...]
        for j in range(2 * RET_HEADS):
            t1 = p_ref[:, j * RET_DK: j * RET_DK + half]
            t2 = p_ref[:, j * RET_DK + half: (j + 1) * RET_DK]
            o1 = t1 * c - t2 * s
            o2 = t1 * s + t2 * c
            if j >= RET_HEADS:
                o1, o2 = o1 * kmul, o2 * kmul
            o_ref[:, j * RET_DK: j * RET_DK + half] = o1
            o_ref[:, j * RET_DK + half: (j + 1) * RET_DK] = o2

    wide = pl.BlockSpec((tr, 2 * RET_QK), lambda i: (i, 0))
    tab = pl.BlockSpec((tr, half), lambda i: (i, 0))
    return _call(body, (proj, cos, sin), name=name, grid=(l // tr,), in_specs=[wide, tab, tab], out_specs=[wide],
                 out_shape=[jax.ShapeDtypeStruct((l, 2 * RET_QK), f32)], sem=("parallel",), vmem_mb=32, comm=comm)


def _ret_prep_bwd(dq, dk, cos, sin, dproj, name):
    l = dq.shape[0]
    tr = _tile(l, 256)
    half = RET_DK // 2
    scale = RET_DK ** -0.5

    def body(dq_ref, dk_ref, c_ref, s_ref, _, o_ref):
        rows = pl.program_id(0) * tr + lax.broadcasted_iota(jnp.int32, (tr, 1), 0)
        kmul = jnp.where(rows >= PAD, scale, 0.0).astype(f32)
        c, s = c_ref[...], s_ref[...]
        for j in range(2 * RET_HEADS):
            d_ref = dq_ref if j < RET_HEADS else dk_ref
            jj = j % RET_HEADS
            d1 = d_ref[:, jj * RET_DK: jj * RET_DK + half]
            d2 = d_ref[:, jj * RET_DK + half: (jj + 1) * RET_DK]
            if j >= RET_HEADS:
                d1, d2 = d1 * kmul, d2 * kmul
            o_ref[:, j * RET_DK: j * RET_DK + half] = _b(d1 * c + d2 * s)
            o_ref[:, j * RET_DK + half: (j + 1) * RET_DK] = _b(d2 * c - d1 * s)

    nar = pl.BlockSpec((tr, RET_QK), lambda i: (i, 0))
    wide = pl.BlockSpec((tr, 2 * RET_QK), lambda i: (i, 0))
    tab = pl.BlockSpec((tr, half), lambda i: (i, 0))
    return pl.pallas_call(
        body, name=name, grid=(l // tr,), in_specs=[nar, nar, tab, tab, pl.BlockSpec(memory_space=pl.ANY)],
        out_specs=wide, out_shape=jax.ShapeDtypeStruct(dproj.shape, dproj.dtype), input_output_aliases={4: 0},
        compiler_params=_params(("parallel",), 32),
    )(dq, dk, cos, sin, dproj)


RET_BLOCK_CHUNKS = 3


def _ret_block(l):
    nch = l // CHUNK
    return RET_BLOCK_CHUNKS * CHUNK if nch % RET_BLOCK_CHUNKS == 0 else CHUNK


def _ret_decay(lg, rb):
    idx = lax.broadcasted_iota(jnp.int32, (rb, 1), 0).astype(f32)
    ri = lax.broadcasted_iota(jnp.int32, (rb, rb), 0)
    ci = lax.broadcasted_iota(jnp.int32, (rb, rb), 1)
    rel = (ri - ci).astype(f32)
    dmask = jnp.where(ri >= ci, jnp.exp(lg * jnp.maximum(rel, 0.0)), 0.0)
    xi = jnp.exp(lg * (idx + 1.0))
    zeta = jnp.exp(lg * (rb - 1.0 - idx))
    return dmask, xi, zeta


def _ret_scan_fwd(qk, proj, gn_w, lgs, gcs, name, comm=None):
    l = qk.shape[0]
    rb = _ret_block(l)
    nb = l // rb

    def body(lg_ref, gc_ref, q_ref, k_ref, v_ref, g_ref, nw_ref, o_ref, st_ref, y_ref, s_ref):
        @pl.when(pl.program_id(0) == 0)
        def _():
            s_ref[...] = jnp.zeros_like(s_ref)

        hs = range(RET_HEADS)
        dec = [_ret_decay(lg_ref[h], rb) for h in hs]
        q = [q_ref[:, h * RET_DK:(h + 1) * RET_DK] for h in hs]
        k = [k_ref[:, h * RET_DK:(h + 1) * RET_DK] for h in hs]
        vb = [_b(v_ref[:, h * RET_DV:(h + 1) * RET_DV]) for h in hs]
        s = [s_ref[h] for h in hs]
        sb = [_b(s[h]) for h in hs]
        scores = [_b(_nt(_b(q[h]), _b(k[h])) * dec[h][0]) for h in hs]
        inter = [_nn(_b(q[h] * dec[h][1]), sb[h]) for h in hs]
        kv = [_tn(_b(k[h] * dec[h][2]), vb[h]) for h in hs]
        nw = nw_ref[...]
        for h in hs:
            cols = slice(h * RET_DV, (h + 1) * RET_DV)
            st_ref[0, h] = sb[h]
            o = _nn(scores[h], vb[h]) + inter[h]
            o_ref[:, cols] = o
            r = lax.rsqrt(jnp.mean(o * o, axis=-1, keepdims=True) + RMS_EPS)
            y_ref[:, cols] = _b(o * r * nw * _silu(g_ref[:, cols]))
            s_ref[h] = gc_ref[h] * s[h] + kv[h]

    smem = pl.BlockSpec(memory_space=pltpu.SMEM)
    wide = pl.BlockSpec((rb, RET_V), lambda n: (n, 0))
    return _call(
        body, (lgs, gcs, qk, qk, proj, proj, gn_w), name=name, grid=(nb,),
        in_specs=[smem, smem,
                  pl.BlockSpec((rb, RET_QK), lambda n: (n, 0)),
                  pl.BlockSpec((rb, RET_QK), lambda n: (n, 1)),
                  pl.BlockSpec((rb, RET_V), lambda n: (n, 1)),
                  pl.BlockSpec((rb, RET_V), lambda n: (n, 2)),
                  pl.BlockSpec((1, RET_DV), lambda n: (0, 0))],
        out_specs=[wide, pl.BlockSpec((1, RET_HEADS, RET_DK, RET_DV), lambda n: (n, 0, 0, 0)), wide],
        out_shape=[jax.ShapeDtypeStruct((l, RET_V), f32),
                   jax.ShapeDtypeStruct((nb, RET_HEADS, RET_DK, RET_DV), bf16),
                   jax.ShapeDtypeStruct((l, RET_V), bf16)],
        scratch=[pltpu.VMEM((RET_HEADS, RET_DK, RET_DV), f32)], sem=("arbitrary",), vmem_mb=40, comm=comm)


def _ret_scan_bwd(qk, proj, states, do, dproj, lgs, gcs, name, comm=None):
    l = qk.shape[0]
    rb = _ret_block(l)
    nb = l // rb

    def body(lg_ref, gc_ref, q_ref, k_ref, v_ref, st_ref, do_ref, _, dq_ref, dk_ref, dv_ref, ds_ref):
        @pl.when(pl.program_id(0) == 0)
        def _():
            ds_ref[...] = jnp.zeros_like(ds_ref)

        hs = range(RET_HEADS)
        dec = [_ret_decay(lg_ref[h], rb) for h in hs]
        q = [q_ref[:, h * RET_DK:(h + 1) * RET_DK] for h in hs]
        k = [k_ref[:, h * RET_DK:(h + 1) * RET_DK] for h in hs]
        qb, kb = [_b(t) for t in q], [_b(t) for t in k]
        vb = [_b(v_ref[:, h * RET_DV:(h + 1) * RET_DV]) for h in hs]
        dob = [_b(do_ref[:, h * RET_DV:(h + 1) * RET_DV]) for h in hs]
        dsp = [ds_ref[h] for h in hs]
        dspb = [_b(t) for t in dsp]
        scores = [_b(_nt(qb[h], kb[h]) * dec[h][0]) for h in hs]
        dscores = [_b(_nt(dob[h], vb[h]) * dec[h][0]) for h in hs]
        for h in hs:
            dq_ref[:, h * RET_DK:(h + 1) * RET_DK] = _nn(dscores[h], kb[h]) + _nt(dob[h], st_ref[0, h]) * dec[h][1]
        for h in hs:
            dk_ref[:, h * RET_DK:(h + 1) * RET_DK] = _tn(dscores[h], qb[h]) + _nt(vb[h], dspb[h]) * dec[h][2]
        for h in hs:
            dv_ref[:, h * RET_DV:(h + 1) * RET_DV] = _b(_tn(scores[h], dob[h]) + _nn(_b(k[h] * dec[h][2]), dspb[h]))
        for h in hs:
            ds_ref[h] = gc_ref[h] * dsp[h] + _tn(_b(q[h] * dec[h][1]), dob[h])

    smem = pl.BlockSpec(memory_space=pltpu.SMEM)
    rev = lambda n: nb - 1 - n
    return _call(
        body, (lgs, gcs, qk, qk, proj, states, do, dproj), name=name, grid=(nb,),
        in_specs=[smem, smem,
                  pl.BlockSpec((rb, RET_QK), lambda n: (rev(n), 0)),
                  pl.BlockSpec((rb, RET_QK), lambda n: (rev(n), 1)),
                  pl.BlockSpec((rb, RET_V), lambda n: (rev(n), 1)),
                  pl.BlockSpec((1, RET_HEADS, RET_DK, RET_DV), lambda n: (rev(n), 0, 0, 0)),
                  pl.BlockSpec((rb, RET_V), lambda n: (rev(n), 0)),
                  pl.BlockSpec(memory_space=pl.ANY)],
        out_specs=[pl.BlockSpec((rb, RET_QK), lambda n: (rev(n), 0)),
                   pl.BlockSpec((rb, RET_QK), lambda n: (rev(n), 0)),
                   pl.BlockSpec((rb, RET_V), lambda n: (rev(n), 1))],
        out_shape=[jax.ShapeDtypeStruct((l, RET_QK), f32), jax.ShapeDtypeStruct((l, RET_QK), f32),
                   jax.ShapeDtypeStruct(dproj.shape, dproj.dtype)],
        scratch=[pltpu.VMEM((RET_HEADS, RET_DK, RET_DV), f32)], sem=("arbitrary",), vmem_mb=40, comm=comm,
        aliases={7: 2})


CONV_BLK = 512
CONV_Q_BLKS = DN_QK // CONV_BLK
HALO = 8


def _conv_tile(l):
    return _tile(l, 3072)


def _slab_rows(r):
    return pl.ds(pl.multiple_of(r * HALO, HALO), HALO)


def _conv_slab(x_ref, p_ref, r, i, tr):
    cur = x_ref[_slab_rows(r), :]
    prev = jnp.where(r > 0, x_ref[_slab_rows(jnp.maximum(r - 1, 0)), :], p_ref[...])
    row0 = i * tr + r * HALO
    cur = jnp.where(row0 >= PAD, cur, 0.0)
    prev = jnp.where(row0 - HALO >= PAD, prev, 0.0)
    lrow = lax.broadcasted_iota(jnp.int32, (HALO, 1), 0)
    shifted = [jnp.where(lrow < s, pltpu.roll(prev, s, 0), pltpu.roll(cur, s, 0)) for s in range(1, CONV_K)]
    return [cur] + shifted


def _conv_of(xs, w):
    acc = xs[0] * w[CONV_K - 1:CONV_K, :]
    for s in range(1, CONV_K):
        acc = acc + xs[s] * w[CONV_K - 1 - s:CONV_K - s, :]
    return acc


def _slab_loop(n_slabs, fn, init=None):
    return lax.fori_loop(0, n_slabs, fn, init, unroll=8)


def _dn_conv_fwd(proj, conv_w, name, comm=None):
    l = proj.shape[0]
    tr = _conv_tile(l)
    nblk = DN_CONV_CH // CONV_BLK
    heads = CONV_BLK // DN_DK

    def body(x_ref, p_ref, w_ref, o_ref):
        i, j = pl.program_id(0), pl.program_id(1)
        w = w_ref[...]

        def act(r):
            return _silu(_conv_of(_conv_slab(x_ref, p_ref, r, i, tr), w))

        def normed(scale):
            def slab(r, carry):
                a = act(r)
                outs = []
                for h in range(heads):
                    ah = a[:, h * DN_DK:(h + 1) * DN_DK]
                    outs.append(ah * (lax.rsqrt(jnp.sum(ah * ah, axis=-1, keepdims=True) + RMS_EPS) * scale))
                o_ref[_slab_rows(r), :] = jnp.concatenate(outs, axis=1)
                return carry
            return slab

        def plain(r, carry):
            o_ref[_slab_rows(r), :] = act(r)
            return carry

        @pl.when(j < CONV_Q_BLKS)
        def _():
            _slab_loop(tr // HALO, normed(DN_DK ** -0.5))

        @pl.when(jnp.logical_and(j >= CONV_Q_BLKS, j < 2 * CONV_Q_BLKS))
        def _():
            _slab_loop(tr // HALO, normed(1.0))

        @pl.when(j >= 2 * CONV_Q_BLKS)
        def _():
            _slab_loop(tr // HALO, plain)

    hb = tr // HALO
    return _call(
        body, (proj, proj, conv_w), name=name, grid=(l // tr, nblk),
        in_specs=[pl.BlockSpec((tr, CONV_BLK), lambda i, j: (i, j)),
                  pl.BlockSpec((HALO, CONV_BLK), lambda i, j: (jnp.maximum(i * hb - 1, 0), j)),
                  pl.BlockSpec((CONV_K, CONV_BLK), lambda i, j: (0, j))],
        out_specs=[pl.BlockSpec((tr, CONV_BLK), lambda i, j: (i, j))],
        out_shape=[jax.ShapeDtypeStruct((l, DN_CONV_CH), f32)],
        scratch=[], sem=("parallel", "parallel"), vmem_mb=32, comm=comm)


def _dn_conv_bwd_a(proj, conv_w, dqkv, name, comm=None):
    l = proj.shape[0]
    tr = _conv_tile(l)
    nblk = DN_CONV_CH // CONV_BLK
    heads = CONV_BLK // DN_DK

    def body(x_ref, p_ref, w_ref, d_ref, dc_ref, dw_ref, acc_ref):
        j, i = pl.program_id(0), pl.program_id(1)
        w = w_ref[...]

        def run(l2_scale):
            zero = jnp.zeros((HALO, CONV_BLK), f32)
            acc = _slab_loop(tr // HALO, slab_of(l2_scale), (zero,) * CONV_K)
            for k in range(CONV_K):
                acc_ref[k] = acc[k]

        def slab_of(l2_scale):
            def slab(r, acc):
                xs = _conv_slab(x_ref, p_ref, r, i, tr)
                c = _conv_of(xs, w)
                a = _silu(c)
                dy = d_ref[_slab_rows(r), :]
                if l2_scale is None:
                    da = dy
                else:
                    parts = []
                    for h in range(heads):
                        sl = slice(h * DN_DK, (h + 1) * DN_DK)
                        ah, dyh = a[:, sl], dy[:, sl]
                        rn = lax.rsqrt(jnp.sum(ah * ah, axis=-1, keepdims=True) + RMS_EPS)
                        yh = ah * rn
                        parts.append((rn * l2_scale) * (dyh - yh * jnp.sum(dyh * yh, axis=-1, keepdims=True)))
                    da = jnp.concatenate(parts, axis=1)
                dc = da * _dsilu(c)
                dc_ref[_slab_rows(r), :] = dc
                return tuple(acc[k] + dc * xs[CONV_K - 1 - k] for k in range(CONV_K))
            return slab

        @pl.when(j < CONV_Q_BLKS)
        def _():
            run(DN_DK ** -0.5)

        @pl.when(jnp.logical_and(j >= CONV_Q_BLKS, j < 2 * CONV_Q_BLKS))
        def _():
            run(1.0)

        @pl.when(j >= 2 * CONV_Q_BLKS)
        def _():
            run(None)

        ksel = lax.broadcasted_iota(jnp.int32, (CONV_K, 1), 0)
        dw = jnp.zeros((CONV_K, CONV_BLK), f32)
        for k in range(CONV_K):
            dw = dw + jnp.where(ksel == k, jnp.sum(acc_ref[k], axis=0, keepdims=True), 0.0)

        @pl.when(i == 0)
        def _():
            dw_ref[...] = dw

        @pl.when(i > 0)
        def _():
            dw_ref[...] += dw

    hb = tr // HALO
    blk = pl.BlockSpec((tr, CONV_BLK), lambda j, i: (i, j))
    return _call(
        body, (proj, proj, conv_w, dqkv), name=name, grid=(nblk, l // tr),
        in_specs=[blk, pl.BlockSpec((HALO, CONV_BLK), lambda j, i: (jnp.maximum(i * hb - 1, 0), j)),
                  pl.BlockSpec((CONV_K, CONV_BLK), lambda j, i: (0, j)), blk],
        out_specs=[blk, pl.BlockSpec((CONV_K, CONV_BLK), lambda j, i: (0, j))],
        out_shape=[jax.ShapeDtypeStruct((l, DN_CONV_CH), f32), jax.ShapeDtypeStruct((CONV_K, DN_CONV_CH), f32)],
        scratch=[pltpu.VMEM((CONV_K, HALO, CONV_BLK), f32)], sem=("parallel", "arbitrary"), vmem_mb=48, comm=comm)


def _dn_conv_bwd_b(dc, conv_w, dproj, name):
    l = dc.shape[0]
    tr = _conv_tile(l)
    nblk = DN_CONV_CH // CONV_BLK
    nrow = l // tr

    n_slabs = tr // HALO
    pair = 2 * HALO

    def body(d_ref, n_ref, w_ref, _, o_ref):
        i = pl.program_id(0)
        w = w_ref[...]
        nxt_tile = jnp.where(i < nrow - 1, n_ref[...], 0.0)
        lrow = lax.broadcasted_iota(jnp.int32, (HALO, 1), 0)

        def one(r):
            cur = d_ref[_slab_rows(r), :]
            nxt = jnp.where(r < n_slabs - 1, d_ref[_slab_rows(jnp.minimum(r + 1, n_slabs - 1)), :], nxt_tile)
            acc = cur * w[CONV_K - 1:CONV_K, :]
            for s in range(1, CONV_K):
                up = jnp.where(lrow >= HALO - s, pltpu.roll(nxt, HALO - s, 0), pltpu.roll(cur, HALO - s, 0))
                acc = acc + up * w[CONV_K - 1 - s:CONV_K - s, :]
            return jnp.where(i * tr + r * HALO >= PAD, acc, 0.0)

        def two(q, carry):
            rows = pl.ds(pl.multiple_of(q * pair, pair), pair)
            o_ref[rows, :] = _b(jnp.concatenate([one(2 * q), one(2 * q + 1)], axis=0))
            return carry

        lax.fori_loop(0, n_slabs // 2, two, None, unroll=4)

    hb = tr // HALO
    nh = l // HALO
    return pl.pallas_call(
        body, name=name, grid=(nrow, nblk),
        in_specs=[pl.BlockSpec((tr, CONV_BLK), lambda i, j: (i, j)),
                  pl.BlockSpec((HALO, CONV_BLK), lambda i, j: (jnp.minimum((i + 1) * hb, nh - 1), j)),
                  pl.BlockSpec((CONV_K, CONV_BLK), lambda i, j: (0, j)),
                  pl.BlockSpec(memory_space=pl.ANY)],
        out_specs=pl.BlockSpec((tr, CONV_BLK), lambda i, j: (i, j)),
        out_shape=jax.ShapeDtypeStruct(dproj.shape, dproj.dtype), input_output_aliases={3: 0},
        compiler_params=_params(("parallel", "parallel"), 32),
    )(dc, dc, conv_w, dproj)


BA_W = LANES


def _dn_gates(ba_ref, al_ref, dt_ref, n):
    rows = n * CHUNK + lax.broadcasted_iota(jnp.int32, (CHUNK, 1), 0)
    vm = (rows >= PAD).astype(f32)
    bin_ = ba_ref[:, 0:DN_HEADS]
    z = ba_ref[:, DN_HEADS:2 * DN_HEADS] + dt_ref[...]
    sp = jnp.maximum(z, 0.0) + jnp.log1p(jnp.exp(-jnp.abs(z)))
    ea = jnp.exp(al_ref[...])
    beta = _sigmoid(bin_) * vm
    g = -ea * sp * vm
    return vm, bin_, z, ea, beta, g


def _tri():
    ri = lax.broadcasted_iota(jnp.int32, (CHUNK, CHUNK), 0)
    ci = lax.broadcasted_iota(jnp.int32, (CHUNK, CHUNK), 1)
    return ri, ci


def _split(a):
    hi = _b(a)
    return hi, _b(a - hi.astype(f32))


def _mm3(a, b, dot=_nn):
    (ah, al), (bh, bl) = _split(a), _split(b)
    return dot(ah, bh) + (dot(ah, bl) + dot(al, bh))


def _cumsum_rows(tri, g):
    tb = _b(tri)
    g1 = _b(g)
    r1 = g - g1.astype(f32)
    g2 = _b(r1)
    g3 = _b(r1 - g2.astype(f32))
    return _nn(tb, g1) + (_nn(tb, g2) + _nn(tb, g3))


DN_SCAN_CHUNKS = 3


def _scan_chunks(nch):
    return DN_SCAN_CHUNKS if nch % DN_SCAN_CHUNKS == 0 else 1


def _dn_prep(qkv, ba, a_log, dt_bias, name, comm=None):
    l = qkv.shape[0]
    nch = l // CHUNK
    heads = range(DN_HEADS)

    cb = _scan_chunks(nch)
    items = [(c, h) for c in range(cb) for h in heads]

    def body(q_ref, k_ref, v_ref, ba_ref, al_ref, dt_ref, t_ref, u_ref, wq_ref, pk_ref, eg_ref, kpt_ref, qwt_ref):
        n0 = pl.program_id(0) * cb
        ri, ci = _tri()
        incl, strict = ri >= ci, ri > ci
        eye = (ri == ci).astype(f32)
        rows = [slice(c * CHUNK, (c + 1) * CHUNK) for c in range(cb)]
        gam, gam_t, beta = [], [], []
        for c in range(cb):
            _, _, _, _, beta_c, g_c = _dn_gates(ba_ref[rows[c], :], al_ref, dt_ref, n0 + c)
            gam.append(_cumsum_rows(incl.astype(f32), g_c))
            gam_t.append(gam[c].T)
            beta.append(beta_c)
        gc = {(c, h): gam[c][:, h:h + 1] for c, h in items}
        bh = {(c, h): beta[c][:, h:h + 1] for c, h in items}
        kh = {(c, h): k_ref[rows[c], h * DN_DK:(h + 1) * DN_DK] for c, h in items}
        kb = {i: _b(kh[i]) for i in items}
        decay = {(c, h): jnp.exp(jnp.where(incl, gc[c, h] - gam_t[c][h:h + 1, :], -jnp.inf)) for c, h in items}
        a = {i: jnp.where(strict, bh[i] * _nt(kb[i], kb[i]) * decay[i], 0.0) for i in items}
        t = {i: eye - a[i] for i in items}
        p = a
        for level in range(int(math.log2(CHUNK)) - 1):
            mm = _mm3 if level < 2 else (lambda x, y: _nn(_b(x), _b(y)))
            p = {i: mm(p[i], p[i]) for i in items}
            t = {i: t[i] + mm(t[i], p[i]) for i in items}
        eg = {i: jnp.exp(gc[i]) for i in items}
        for c, h in items:
            i = (c, h)
            t_ref[c, h] = t[i]
            u_ref[rows[c], h * DN_DV:(h + 1) * DN_DV] = _mm3(t[i], v_ref[rows[c], h * DN_DV:(h + 1) * DN_DV] * bh[i])
            w = _mm3(t[i], kh[i] * (bh[i] * eg[i]))
            wq_ref[c, h, 0:CHUNK, :] = _b(w)
            qwt_ref[c, h, DN_DK:2 * DN_DK, :] = _b(w.T)
        for c, h in items:
            i = (c, h)
            qh = q_ref[rows[c], h * DN_DK:(h + 1) * DN_DK]
            gl = gc[i][CHUNK - 1:CHUNK, :]
            qe = qh * eg[i]
            ke = kh[i] * jnp.exp(gl - gc[i])
            pmat = _nt(_b(qh), kb[i]) * decay[i]
            wq_ref[c, h, CHUNK:2 * CHUNK, :] = _b(qe)
            qwt_ref[c, h, 0:DN_DK, :] = _b(qe.T)
            pk_ref[c, h, 0:CHUNK, :] = _b(pmat)
            pk_ref[c, h, CHUNK:CHUNK + DN_DK, :] = _b(ke.T)
            kpt_ref[c, h, :, 0:DN_DK] = _b(ke)
            kpt_ref[c, h, :, DN_DK:DN_DK + CHUNK] = _b(pmat.T)
            eg_ref[c, h] = jnp.broadcast_to(jnp.exp(gl), (8, LANES))

    vec = pl.BlockSpec((1, DN_HEADS), lambda n: (0, 0))
    return _call(
        body, (qkv, qkv, qkv, ba, a_log, dt_bias), name=name, grid=(nch // cb,),
        in_specs=[pl.BlockSpec((cb * CHUNK, DN_QK), lambda n: (n, 0)),
                  pl.BlockSpec((cb * CHUNK, DN_QK), lambda n: (n, 1)),
                  pl.BlockSpec((cb * CHUNK, DN_V), lambda n: (n, 1)),
                  pl.BlockSpec((cb * CHUNK, BA_W), lambda n: (n, 0)), vec, vec],
        out_specs=[pl.BlockSpec((cb, DN_HEADS, CHUNK, CHUNK), lambda n: (n, 0, 0, 0)),
                   pl.BlockSpec((cb * CHUNK, DN_V), lambda n: (n, 0)),
                   pl.BlockSpec((cb, DN_HEADS, 2 * CHUNK, DN_DK), lambda n: (n, 0, 0, 0)),
                   pl.BlockSpec((cb, DN_HEADS, CHUNK + DN_DK, CHUNK), lambda n: (n, 0, 0, 0)),
                   pl.BlockSpec((cb, DN_HEADS, 8, LANES), lambda n: (n, 0, 0, 0)),
                   pl.BlockSpec((cb, DN_HEADS, CHUNK, DN_DK + CHUNK), lambda n: (n, 0, 0, 0)),
                   pl.BlockSpec((cb, DN_HEADS, 2 * DN_DK, CHUNK), lambda n: (n, 0, 0, 0))],
        out_shape=[jax.ShapeDtypeStruct((nch, DN_HEADS, CHUNK, CHUNK), f32),
                   jax.ShapeDtypeStruct((l, DN_V), f32),
                   jax.ShapeDtypeStruct((nch, DN_HEADS, 2 * CHUNK, DN_DK), bf16),
                   jax.ShapeDtypeStruct((nch, DN_HEADS, CHUNK + DN_DK, CHUNK), bf16),
                   jax.ShapeDtypeStruct((nch, DN_HEADS, 8, LANES), f32),
                   jax.ShapeDtypeStruct((nch, DN_HEADS, CHUNK, DN_DK + CHUNK), bf16),
                   jax.ShapeDtypeStruct((nch, DN_HEADS, 2 * DN_DK, CHUNK), bf16)],
        sem=("parallel",), vmem_mb=40, comm=comm)


def _dn_scan_fwd(u, wq, pk, egl, name):
    l = u.shape[0]
    nch = l // CHUNK
    cs = _scan_chunks(nch)

    def body(u_ref, wq_ref, pk_ref, eg_ref, o_ref, st_ref, vn_ref, s_ref):
        @pl.when(pl.program_id(0) == 0)
        def _():
            s_ref[...] = jnp.zeros_like(s_ref)

        hs = range(DN_HEADS)
        cols = [slice(h * DN_DV, (h + 1) * DN_DV) for h in hs]
        s = [s_ref[h] for h in hs]
        for c in range(cs):
            rows = slice(c * CHUNK, (c + 1) * CHUNK)
            sb = [_b(s[h]) for h in hs]
            x = [_nn(wq_ref[c, h], sb[h]) for h in hs]
            vnb = [_b(u_ref[rows, cols[h]] - x[h][0:CHUNK]) for h in hs]
            y = [_nn(pk_ref[c, h], vnb[h]) for h in hs]
            for h in hs:
                st_ref[c, h] = sb[h]
                vn_ref[rows, cols[h]] = vnb[h]
                o_ref[rows, cols[h]] = x[h][CHUNK:2 * CHUNK] + y[h][0:CHUNK]
            s = [eg_ref[c, h][0:1, 0:1] * s[h] + y[h][CHUNK:CHUNK + DN_DK] for h in hs]
        for h in hs:
            s_ref[h] = s[h]

    return pl.pallas_call(
        body, name=name, grid=(nch // cs,),
        in_specs=[pl.BlockSpec((cs * CHUNK, DN_V), lambda n: (n, 0)),
                  pl.BlockSpec((cs, DN_HEADS, 2 * CHUNK, DN_DK), lambda n: (n, 0, 0, 0)),
                  pl.BlockSpec((cs, DN_HEADS, CHUNK + DN_DK, CHUNK), lambda n: (n, 0, 0, 0)),
                  pl.BlockSpec((cs, DN_HEADS, 8, LANES), lambda n: (n, 0, 0, 0))],
        out_specs=[pl.BlockSpec((cs * CHUNK, DN_V), lambda n: (n, 0)),
                   pl.BlockSpec((cs, DN_HEADS, DN_DK, DN_DV), lambda n: (n, 0, 0, 0)),
                   pl.BlockSpec((cs * CHUNK, DN_V), lambda n: (n, 0))],
        out_shape=[jax.ShapeDtypeStruct((l, DN_V), f32),
                   jax.ShapeDtypeStruct((nch, DN_HEADS, DN_DK, DN_DV), bf16),
                   jax.ShapeDtypeStruct((l, DN_V), bf16)],
        scratch_shapes=[pltpu.VMEM((DN_HEADS, DN_DK, DN_DV), f32)],
        compiler_params=_params(("arbitrary",), 40),
    )(u, wq, pk, egl)


def _dn_scan_bwd(do, kpt, qwt, egl, name):
    l = do.shape[0]
    nch = l // CHUNK
    cs = _scan_chunks(nch)
    nblk = nch // cs

    def body(do_ref, kpt_ref, qwt_ref, eg_ref, dvn_ref, dsp_ref, ds_ref):
        @pl.when(pl.program_id(0) == 0)
        def _():
            ds_ref[...] = jnp.zeros_like(ds_ref)

        hs = range(DN_HEADS)
        cols = [slice(h * DN_DV, (h + 1) * DN_DV) for h in hs]
        ds = [ds_ref[h] for h in hs]
        for c in reversed(range(cs)):
            rows = slice(c * CHUNK, (c + 1) * CHUNK)
            dspb = [_b(ds[h]) for h in hs]
            dob = [_b(do_ref[rows, cols[h]]) for h in hs]
            dvn = [_nn(kpt_ref[c, h][:, 0:DN_DK], dspb[h]) + _nn(kpt_ref[c, h][:, DN_DK:DN_DK + CHUNK], dob[h])
                   for h in hs]
            for h in hs:
                dsp_ref[c, h] = dspb[h]
                dvn_ref[rows, cols[h]] = dvn[h]
            ds = [eg_ref[c, h][0:1, 0:1] * ds[h] + _nn(qwt_ref[c, h][0:DN_DK], dob[h])
                  - _nn(qwt_ref[c, h][DN_DK:2 * DN_DK], _b(dvn[h])) for h in hs]
        for h in hs:
            ds_ref[h] = ds[h]

    rev = lambda s: nblk - 1 - s
    return pl.pallas_call(
        body, name=name, grid=(nblk,),
        in_specs=[pl.BlockSpec((cs * CHUNK, DN_V), lambda s: (rev(s), 0)),
                  pl.BlockSpec((cs, DN_HEADS, CHUNK, DN_DK + CHUNK), lambda s: (rev(s), 0, 0, 0)),
                  pl.BlockSpec((cs, DN_HEADS, 2 * DN_DK, CHUNK), lambda s: (rev(s), 0, 0, 0)),
                  pl.BlockSpec((cs, DN_HEADS, 8, LANES), lambda s: (rev(s), 0, 0, 0))],
        out_specs=[pl.BlockSpec((cs * CHUNK, DN_V), lambda s: (rev(s), 0)),
                   pl.BlockSpec((cs, DN_HEADS, DN_DK, DN_DV), lambda s: (rev(s), 0, 0, 0))],
        out_shape=[jax.ShapeDtypeStruct((l, DN_V), f32),
                   jax.ShapeDtypeStruct((nch, DN_HEADS, DN_DK, DN_DV), bf16)],
        scratch_shapes=[pltpu.VMEM((DN_HEADS, DN_DK, DN_DV), f32)],
        compiler_params=_params(("arbitrary",), 40),
    )(do, kpt, qwt, egl)


def _dn_post_bwd(qkv, ba, a_log, dt_bias, states, dsp_all, tinv_all, u_all, wq, vn_all, do, dvn_all, name):
    l = qkv.shape[0]
    nch = l // CHUNK
    cb = _scan_chunks(nch)
    items = [(c, h) for c in range(cb) for h in range(DN_HEADS)]

    def body(q_ref, k_ref, v_ref, ba_ref, al_ref, dt_ref, st_ref, dsp_ref, t_ref, u_ref, wq_ref, vn_ref, do_ref,
             dvn_ref, dqkv_ref, dba_ref, dal_ref, ddt_ref):
        step = pl.program_id(0)
        ri, ci = _tri()
        incl, strict = ri >= ci, ri > ci
        lane8 = lax.broadcasted_iota(jnp.int32, (1, DN_HEADS), 1)
        sub8 = lax.broadcasted_iota(jnp.int32, (DN_HEADS, 1), 0)
        last = (lax.broadcasted_iota(jnp.int32, (CHUNK, 1), 0) == CHUNK - 1).astype(f32)
        rsum = lambda t: jnp.sum(t, axis=-1, keepdims=True)
        rows = [slice(c * CHUNK, (c + 1) * CHUNK) for c in range(cb)]
        gates = [_dn_gates(ba_ref[rows[c], :], al_ref, dt_ref, step * cb + c) for c in range(cb)]
        gam = [_cumsum_rows(incl.astype(f32), gates[c][5]) for c in range(cb)]
        gam_t = [gam[c].T for c in range(cb)]
        each = lambda fn: {(c, h): fn(c, h) for c, h in items}
        dk_cols = lambda h: slice(h * DN_DK, (h + 1) * DN_DK)
        dv_cols = lambda h: slice(h * DN_DV, (h + 1) * DN_DV)
        gc = each(lambda c, h: gam[c][:, h:h + 1])
        bh = each(lambda c, h: gates[c][4][:, h:h + 1])
        qh = each(lambda c, h: q_ref[rows[c], dk_cols(h)])
        kh = each(lambda c, h: k_ref[rows[c], dk_cols(h)])
        doh = each(lambda c, h: _b(do_ref[rows[c], dv_cols(h)]))
        sb = each(lambda c, h: st_ref[c, h])
        dspb = each(lambda c, h: dsp_ref[c, h])
        vnb = each(lambda c, h: vn_ref[rows[c], dv_cols(h)])
        dvn = each(lambda c, h: dvn_ref[rows[c], dv_cols(h)])
        wb = each(lambda c, h: wq_ref[c, h, 0:CHUNK, :])
        decay = each(lambda c, h: jnp.exp(jnp.where(incl, gc[c, h] - gam_t[c][h:h + 1, :], -jnp.inf)))
        qb, kb = each(lambda c, h: _b(qh[c, h])), each(lambda c, h: _b(kh[c, h]))
        eg = each(lambda c, h: jnp.exp(gc[c, h]))
        gl = each(lambda c, h: gc[c, h][CHUNK - 1:CHUNK, :])
        ekd = each(lambda c, h: jnp.exp(gl[c, h] - gc[c, h]))
        dvnb = each(lambda c, h: _b(dvn[c, h]))
        kk = each(lambda c, h: _nt(kb[c, h], kb[c, h]))
        p = each(lambda c, h: _nt(qb[c, h], kb[c, h]) * decay[c, h])
        dpraw = each(lambda c, h: _nt(doh[c, h], vnb[c, h]))
        dqe = each(lambda c, h: _nt(doh[c, h], sb[c, h]))
        dke = each(lambda c, h: _nt(vnb[c, h], dspb[c, h]))
        dw = each(lambda c, h: -_nt(dvnb[c, h], sb[c, h]))
        dru = each(lambda c, h: _mm3(t_ref[c, h], dvn[c, h], _tn))
        drw = each(lambda c, h: _mm3(t_ref[c, h], dw[c, h], _tn))
        dqk = each(lambda c, h: _b(dpraw[c, h] * decay[c, h]))
        for c, h in items:
            i = (c, h)
            dqkv_ref[rows[c], dk_cols(h)] = _nn(dqk[i], kb[i]) + dqe[i] * eg[i]
            dqkv_ref[rows[c], 2 * DN_QK + h * DN_DV:2 * DN_QK + (h + 1) * DN_DV] = bh[i] * dru[i]
        da = each(lambda c, h: jnp.where(strict, -(_nt(_b(dru[c, h]), _b(u_ref[rows[c], dv_cols(h)]))
                                                   + _nt(_b(drw[c, h]), wb[c, h])), 0.0))
        dkk = each(lambda c, h: _b(da[c, h] * bh[c, h] * decay[c, h]))
        for c, h in items:
            i = (c, h)
            dqkv_ref[rows[c], DN_QK + h * DN_DK:DN_QK + (h + 1) * DN_DK] = (
                _tn(dqk[i], qb[i]) + dke[i] * ekd[i] + (bh[i] * eg[i]) * drw[i]
                + _nn(dkk[i], kb[i]) + _tn(dkk[i], kb[i]))
        dal = jnp.zeros((1, DN_HEADS), f32)
        ddt = jnp.zeros((1, DN_HEADS), f32)
        dba_ref[...] = jnp.zeros_like(dba_ref)
        for c in range(cb):
            vm, bin_, z, ea, _, g = gates[c]
            dbeta = jnp.zeros((CHUNK, DN_HEADS), f32)
            dgam = jnp.zeros((CHUNK, DN_HEADS), f32)
            dgam_neg_t = jnp.zeros((DN_HEADS, CHUNK), f32)
            for h in range(DN_HEADS):
                i = (c, h)
                keg = kh[i] * eg[i]
                ke = kh[i] * ekd[i]
                rw = rsum(drw[i] * keg)
                rke = rsum(dke[i] * ke)
                db_h = rsum(dru[i] * v_ref[rows[c], dv_cols(h)]) + rw + rsum(da[i] * kk[i] * decay[i])
                mm = da[i] * (bh[i] * kk[i] * decay[i]) + dpraw[i] * p[i]
                dgl = (jnp.sum(rke, axis=0, keepdims=True)
                       + jnp.exp(gl[i]) * jnp.sum(rsum(dspb[i].astype(f32) * sb[i].astype(f32)), axis=0,
                                                  keepdims=True))
                dg_h = rsum(mm) + rw * bh[i] + rsum(dqe[i] * (qh[i] * eg[i])) - rke + last * dgl
                dbeta = dbeta + jnp.where(lane8 == h, db_h, 0.0)
                dgam = dgam + jnp.where(lane8 == h, dg_h, 0.0)
                dgam_neg_t = dgam_neg_t + jnp.where(sub8 == h, jnp.sum(mm, axis=0, keepdims=True), 0.0)
            dgam = dgam - dgam_neg_t.T
            dg = _cumsum_rows((ri <= ci).astype(f32), dgam)
            sg = _sigmoid(bin_)
            dain = dg * (-ea) * vm * _sigmoid(z)
            dba_ref[rows[c], 0:DN_HEADS] = dbeta * vm * sg * (1.0 - sg)
            dba_ref[rows[c], DN_HEADS:2 * DN_HEADS] = dain
            dal = dal + jnp.sum(dg * g, axis=0, keepdims=True)
            ddt = ddt + jnp.sum(dain, axis=0, keepdims=True)

        @pl.when(step == 0)
        def _():
            dal_ref[...] = dal
            ddt_ref[...] = ddt

        @pl.when(step > 0)
        def _():
            dal_ref[...] += dal
            ddt_ref[...] += ddt

    vec = pl.BlockSpec((1, DN_HEADS), lambda s: (0, 0))
    qs = pl.BlockSpec((cb * CHUNK, DN_QK), lambda s: (s, 0))
    ks = pl.BlockSpec((cb * CHUNK, DN_QK), lambda s: (s, 1))
    vs = pl.BlockSpec((cb * CHUNK, DN_V), lambda s: (s, 1))
    v0 = pl.BlockSpec((cb * CHUNK, DN_V), lambda s: (s, 0))
    st = pl.BlockSpec((cb, DN_HEADS, DN_DK, DN_DV), lambda s: (s, 0, 0, 0))
    return pl.pallas_call(
        body, name=name, grid=(nch // cb,),
        in_specs=[qs, ks, vs, pl.BlockSpec((cb * CHUNK, BA_W), lambda s: (s, 0)), vec, vec, st, st,
                  pl.BlockSpec((cb, DN_HEADS, CHUNK, CHUNK), lambda s: (s, 0, 0, 0)),
                  v0, pl.BlockSpec((cb, DN_HEADS, 2 * CHUNK, DN_DK), lambda s: (s, 0, 0, 0)), v0, v0, v0],
        out_specs=[pl.BlockSpec((cb * CHUNK, DN_CONV_CH), lambda s: (s, 0)),
                   pl.BlockSpec((cb * CHUNK, BA_W), lambda s: (s, 0)), vec, vec],
        out_shape=[jax.ShapeDtypeStruct((l, DN_CONV_CH), f32), jax.ShapeDtypeStruct((l, BA_W), f32),
                   jax.ShapeDtypeStruct((1, DN_HEADS), f32), jax.ShapeDtypeStruct((1, DN_HEADS), f32)],
        compiler_params=_params(("arbitrary",), 48),
    )(qkv, qkv, qkv, ba, a_log, dt_bias, states, dsp_all, tinv_all, u_all, wq, vn_all, do, dvn_all)


def _ffn_fwd(h, hn, wg, wu, wd, tb, th, tag, plan, next_norm_w=None):
    fh, d = wd.shape
    a, b, s = plan.call(f"{tag}_gu", functools.partial(_ffn_gu, tm=th // 2, tn=fh // 2), hn, wg, wu, n_out=3)
    out = plan.matmul(f"{tag}_down", s, wd, mode="nn", tm=th // 2, tn=d, tk=fh, res=h, norm_w=next_norm_w)
    return out, (hn, a, b, s)


def _ffn_bwd(dh, dhb, h, nw, wg, wu, wd, saved, tb, th, tag, plan):
    hn, a, b, s = saved
    d = h.shape[1]
    fh = wd.shape[0]
    layer = tag[-1]
    gr = plan.grads
    da, db = _ffn_ds(dhb, wd, a, b, tm=th // 2, tn=fh // 2, name=f"{tag}_b_ds")
    gr["down" + layer] = _matmul(s, dhb, mode="tn", tm=fh // 2, tn=d, tk=th, out_dtype=bf16, name=f"{tag}_b_dwd")
    dh2, dh2b, dnw = plan.call(f"{tag}_b_dhn", functools.partial(_dhn_norm_bwd, tm=th // 2, tk=fh // 2),
                               [(da, wg), (db, wu)], h, nw, dh, n_out=3)
    gr["gate" + layer] = _matmul(hn, da, mode="tn", tm=d, tn=fh // 2, tk=th, out_dtype=bf16, name=f"{tag}_b_dwg")
    gr["up" + layer] = _matmul(hn, db, mode="tn", tm=d, tn=fh // 2, tk=th, out_dtype=bf16, name=f"{tag}_b_dwu")
    return dh2, dh2b, dnw


class _Plan:
    GATHERS = {"ret_proj": ("ret_out", "gate0"), "ret_prep": ("up0",), "ret_scan": ("down0", "dn_in_top"),
               "ffn0_gu": ("dn_in_bottom",), "dn_proj": ("dn_out",), "dn_conv": ("gate1",),
               "dn_prep": ("up1", "down1")}
    SCATTERS = {"ffn1_b_dhn": ("down1",), "dn_b_conv_a": ("gate1", "up1", "dn_out"), "ffn0_b_dhn": ("dn_in",),
                "ret_b_scan": ("gate0", "up0"), "ret_b_dwin": ("down0", "ret_out"), "ret_b_dhn": ("ret_in",)}

    def __init__(self, shards, wts):
        self.shards, self.wts, self.grads, self.parts = shards, wts, {}, {}

    def _exchange(self, stage):
        if self.shards is None:
            return None
        if stage in self.GATHERS:
            return _Exchange([self.shards[n] for n in self.GATHERS[stage]], True)
        if stage in self.SCATTERS:
            return _Exchange([self._dev_major(n) for n in self.SCATTERS[stage]], False)
        return None

    def _dev_major(self, name):
        g = self.grads
        if name[:-1] in ("gate", "up"):
            return _dev_major_cols(g[name], g[name].shape[1] // N_DEV)
        if name[:-1] == "down":
            dwd = g[name]
            return dwd.reshape(N_DEV, dwd.shape[0] // N_DEV, dwd.shape[1])
        if name in ("ret_out", "dn_out"):
            return g[name].reshape(N_DEV, g[name].shape[0] // N_DEV, g[name].shape[1])
        return _dev_major_cols(g[name], self.shards[name].shape[-1])

    def _landed(self, stage, outs):
        if stage in self.SCATTERS:
            self.parts.update(zip(self.SCATTERS[stage], outs))
            return
        w = self.wts
        cols = lambda t: t.transpose(1, 0, 2).reshape(t.shape[1], N_DEV * t.shape[2])
        rows = lambda t: t.reshape(N_DEV * t.shape[1], t.shape[2])
        for name, t in zip(self.GATHERS[stage], outs):
            if name in ("ret_out", "dn_out") or name.startswith("down"):
                w[name] = rows(t)
            else:
                w[name] = cols(t)
        if "dn_in_top" in w and "dn_in_bottom" in w and "dn_main" not in w:
            full = jnp.concatenate([w["dn_in_top"], w["dn_in_bottom"]], axis=0)
            n_main = DN_CONV_CH + DN_V
            w["dn_main"] = full[:, :n_main]
            w["dn_ba"] = jnp.pad(full[:, n_main:], ((0, 0), (0, BA_W - (full.shape[1] - n_main))))

    def matmul(self, stage, a, b, **kw):
        comm = self._exchange(stage)
        if comm is None:
            return _matmul(a, b, name=stage, **kw)
        out, landed = _matmul(a, b, name=stage, comm=comm, **kw)
        self._landed(stage, landed)
        return out

    def call(self, stage, fn, *args, n_out):
        comm = self._exchange(stage)
        out = fn(*args, stage, comm=comm)
        if comm is not None:
            self._landed(stage, out[n_out:])
        return out[:n_out]


def _local_step(x2, target, wts, shards=None):
    plan = _Plan(shards, wts)
    s_len, d = x2.shape
    l = s_len + CHUNK
    tb = _tile(l, 3072)
    th = tb // 2 if (tb // 2) % 16 == 0 else tb
    half = RET_DK // 2
    inv_freq = (np.float32(ROPE_BASE) ** (-np.arange(half, dtype=np.float32) / np.float32(half))).astype(np.float32)
    ang = (np.arange(l) - PAD).astype(np.float32)[:, None] * inv_freq[None, :]
    cos, sin = jnp.asarray(np.cos(ang), f32), jnp.asarray(np.sin(ang), f32)
    lgs = jnp.log1p(-jnp.exp2(-5.0 - jnp.arange(RET_HEADS, dtype=f32)))
    gcs = jnp.exp(lgs * _ret_block(l))

    h0 = jnp.concatenate([jnp.zeros((PAD, d), f32), wts["meta"], x2], axis=0)
    mixw, ffnw = wts["mix_norm"], wts["ffn_norm"]

    hn0 = _rms_fwd(h0, mixw[0:1], "l0_norm")
    proj0 = plan.matmul("ret_proj", hn0, wts["ret_in"], mode="nn", tm=tb, tn=512, tk=d)
    (qk0,) = plan.call("ret_prep", _ret_prep, proj0, cos, sin, n_out=1)
    o0, st0, y0 = plan.call("ret_scan", _ret_scan_fwd, qk0, proj0, wts["ret_gn"], lgs, gcs, n_out=3)
    h1, hn1 = _matmul(y0, wts["ret_out"], mode="nn", tm=th // 2, tn=d, tk=RET_V, res=h0, norm_w=ffnw[0:1],
                      name="ret_out")
    (h2, hn2), ffn0 = _ffn_fwd(h1, hn1, wts["gate0"], wts["up0"], wts["down0"], tb, th, "ffn0", plan,
                               next_norm_w=mixw[1:2])

    proj1 = plan.matmul("dn_proj", hn2, wts["dn_main"], mode="nn", tm=tb, tn=512, tk=d)
    ba = _matmul(hn2, wts["dn_ba"], mode="nn", tm=tb, tn=BA_W, tk=d, name="dn_proj_ba")
    (qkv1,) = plan.call("dn_conv", _dn_conv_fwd, proj1, wts["conv_w"], n_out=1)
    tinv1, u1, wq1, pk1, egl1, kpt1, qwt1 = plan.call("dn_prep", _dn_prep, qkv1, ba, wts["a_log"], wts["dt_bias"],
                                                      n_out=7)
    o1, st1, vn1 = _dn_scan_fwd(u1, wq1, pk1, egl1, "dn_scan")
    y1 = _gnorm_fwd(o1, proj1, wts["dn_norm"], DN_HEADS, DN_DV, 2, "dn_gnorm")
    h3, hn3 = _matmul(y1, wts["dn_out"], mode="nn", tm=th // 2, tn=d, tk=DN_V, res=h2, norm_w=ffnw[1:2],
                      name="dn_out")
    h4, ffn1 = _ffn_fwd(h3, hn3, wts["gate1"], wts["up1"], wts["down1"], tb, th, "ffn1", plan)

    dh4, dh4b, dfinal, loss = _final_loss(h4, wts["final_norm"], target, "final_loss")
    gr = plan.grads
    dh3, dh3b, dffn1 = _ffn_bwd(dh4, dh4b, h3, ffnw[1:2], wts["gate1"], wts["up1"], wts["down1"], ffn1,
                                tb, th, "ffn1", plan)

    gr["dn_out"] = _matmul(y1, dh3b, mode="tn", tm=1024, tn=d, tk=tb, out_dtype=bf16, name="dn_b_dwout")
    do1, dproj1, ddn_norm = _dy_gnorm_bwd(dh3b, wts["dn_out"], o1, proj1, wts["dn_norm"], DN_DV, 2, "dn_b_gnorm",
                                          tm=th // 2, tn=1024)
    dvn1, dsp1 = _dn_scan_bwd(do1, kpt1, qwt1, egl1, "dn_b_scan")
    dqkv1, dba, dalog, ddt = _dn_post_bwd(qkv1, ba, wts["a_log"], wts["dt_bias"], st1, dsp1, tinv1, u1, wq1, vn1,
                                          do1, dvn1, "dn_b_post")
    dc1, dconv = plan.call("dn_b_conv_a", _dn_conv_bwd_a, proj1, wts["conv_w"], dqkv1, n_out=2)
    dproj1 = _dn_conv_bwd_b(dc1, wts["conv_w"], dproj1, "dn_b_conv_b")
    dbab = dba.astype(bf16)
    n_main = dproj1.shape[1]
    dhn2_ba = _matmul(dbab, wts["dn_ba"], mode="nt", tm=th, tn=d, tk=BA_W, name="dn_b_dhn_ba")
    dh2, dh2b, dmix1 = plan.call("dn_b_dhn", functools.partial(_dhn_norm_bwd, tm=th // 2, tk=n_main // 4,
                                                              init=dhn2_ba),
                                 [(dproj1, wts["dn_main"])], h2, mixw[1:2], dh3, n_out=3)
    dw_main = _matmul(hn2, dproj1, mode="tn", tm=d, tn=512, tk=tb, out_dtype=bf16, name="dn_b_dwin")
    dw_ba = _matmul(hn2, dbab, mode="tn", tm=d, tn=BA_W, tk=tb, out_dtype=bf16, name="dn_b_dwin_ba")
    gr["dn_in"] = jnp.concatenate([dw_main, dw_ba], axis=1)

    dh1, dh1b, dffn0 = _ffn_bwd(dh2, dh2b, h1, ffnw[0:1], wts["gate0"], wts["up0"], wts["down0"], ffn0,
                                tb, th, "ffn0", plan)

    gr["ret_out"] = _matmul(y0, dh1b, mode="tn", tm=1024, tn=d, tk=tb, out_dtype=bf16, name="ret_b_dwout")
    do0, dproj0, dret_gn = _dy_gnorm_bwd(dh1b, wts["ret_out"], o0, proj0, wts["ret_gn"], RET_DV, 2, "ret_b_gnorm",
                                         tm=th // 2, tn=1024)
    dq0, dk0, dproj0 = plan.call("ret_b_scan", _ret_scan_bwd, qk0, proj0, st0, do0, dproj0, lgs, gcs, n_out=3)
    dproj0 = _ret_prep_bwd(dq0, dk0, cos, sin, dproj0, "ret_b_prep")
    n_in = dproj0.shape[1]
    gr["ret_in"] = plan.matmul("ret_b_dwin", hn0, dproj0, mode="tn", tm=d, tn=512, tk=tb, out_dtype=bf16)
    dh0, _, dmix0 = plan.call("ret_b_dhn", functools.partial(_dhn_norm_bwd, tm=th // 2, tk=n_in // 4),
                              [(dproj0, wts["ret_in"])], h0, mixw[0:1], dh1, n_out=3)

    gr.update(meta=dh0[PAD:CHUNK], mix_norm=jnp.concatenate([dmix0, dmix1], axis=0),
              ffn_norm=jnp.concatenate([dffn0, dffn1], axis=0), ret_gn=dret_gn, conv_w=dconv, a_log=dalog,
              dt_bias=ddt, dn_norm=ddn_norm, final_norm=dfinal)
    return loss, dh0[CHUNK:], gr, plan


def _adamw_reduce(parts, w, m, v, name):
    _, r, c = parts.shape
    c_pad = -(-c // LANES) * LANES
    tr = _div_tile(r, max(8, (3 * MIB // 16) // c_pad // 8 * 8), 16)

    def body(p_ref, w_ref, m_ref, v_ref, g_ref, d_ref, nm_ref, nv_ref):
        g = p_ref[0].astype(f32)
        for s in range(1, N_DEV):
            g = g + p_ref[s].astype(f32)
        mm = ADAM_B1 * m_ref[...] + (1.0 - ADAM_B1) * g
        vv = ADAM_B2 * v_ref[...] + (1.0 - ADAM_B2) * (g * g)
        m_hat = mm / (1.0 - ADAM_B1 ** ADAM_STEP)
        v_hat = vv / (1.0 - ADAM_B2 ** ADAM_STEP)
        g_ref[...] = g
        d_ref[...] = -ADAM_LR * (m_hat / (jnp.sqrt(v_hat) + ADAM_EPS) + ADAM_WD * w_ref[...])
        nm_ref[...] = mm
        nv_ref[...] = vv

    blk = pl.BlockSpec((tr, c), lambda i: (i, 0))
    return pl.pallas_call(
        body, name=name, grid=(r // tr,),
        in_specs=[pl.BlockSpec((N_DEV, tr, c), lambda i: (0, i, 0)), blk, blk, blk], out_specs=[blk] * 4,
        out_shape=[jax.ShapeDtypeStruct((r, c), f32)] * 4,
        compiler_params=_params(("parallel",), 48),
    )(parts, w, m, v)


def _dev_major_cols(g, width):
    r = g.shape[0]
    return g[:, :N_DEV * width].reshape(r, N_DEV, width).transpose(1, 0, 2)


def kernel(x, meta_tokens, mix_norm_w, ffn_norm_w, ret_w_in, ret_gn_w, ret_w_out, dn_w_in, dn_conv_w, dn_a_log, dn_dt_bias, dn_norm_w, dn_w_out, ffn_w_gate, ffn_w_up, ffn_w_down, final_norm_w, loss_target, m_meta_tokens, m_mix_norm_w, m_ffn_norm_w, m_ret_w_in, m_ret_gn_w, m_ret_w_out, m_dn_w_in, m_dn_conv_w, m_dn_a_log, m_dn_dt_bias, m_dn_norm_w, m_dn_w_out, m_ffn_w_gate, m_ffn_w_up, m_ffn_w_down, m_final_norm_w, v_meta_tokens, v_mix_norm_w, v_ffn_norm_w, v_ret_w_in, v_ret_gn_w, v_ret_w_out, v_dn_w_in, v_dn_conv_w, v_dn_a_log, v_dn_dt_bias, v_dn_norm_w, v_dn_w_out, v_ffn_w_gate, v_ffn_w_up, v_ffn_w_down, v_final_norm_w):
    d = x.shape[-1]
    me = 4 * lax.axis_index("x") + 2 * lax.axis_index("y") + lax.axis_index("c")

    shards = dict(ret_in=ret_w_in[0].astype(bf16), ret_out=ret_w_out[0].astype(bf16),
                  dn_in=dn_w_in[0].astype(bf16), dn_out=dn_w_out[0].astype(bf16))
    shards["dn_in_top"], shards["dn_in_bottom"] = shards["dn_in"][:d // 2], shards["dn_in"][d // 2:]
    for layer in (0, 1):
        shards[f"gate{layer}"] = ffn_w_gate[layer].astype(bf16)
        shards[f"up{layer}"] = ffn_w_up[layer].astype(bf16)
        shards[f"down{layer}"] = ffn_w_down[layer].astype(bf16)
    g_ret_in, g_meta, g_conv, g_dnn = _exchange([shards["ret_in"], meta_tokens, dn_conv_w[0], dn_norm_w], True,
                                                "gather_first")
    cols = lambda g: g.transpose(1, 0, 2).reshape(g.shape[1], N_DEV * g.shape[2])
    wts = dict(meta=cols(g_meta), mix_norm=mix_norm_w, ffn_norm=ffn_norm_w, ret_in=cols(g_ret_in), ret_gn=ret_gn_w,
               conv_w=cols(g_conv), a_log=dn_a_log, dt_bias=dn_dt_bias, dn_norm=cols(g_dnn),
               final_norm=final_norm_w.reshape(1, d))

    loss_part, grad_x, gr, plan = _local_step(x[0], loss_target[0], wts, shards)
    loss = lax.psum(loss_part[0, 0], AXES)

    pp = plan.parts
    both = lambda name: jnp.concatenate([pp[name + "0"], pp[name + "1"]], axis=1)
    big_parts = [pp["ret_in"], pp["ret_out"], pp["dn_in"], pp["dn_out"], both("gate"), both("up"), both("down")]
    big_names = ["ret_w_in", "ret_w_out", "dn_w_in", "dn_w_out", "ffn_w_gate", "ffn_w_up", "ffn_w_down"]
    big_w = [ret_w_in, ret_w_out, dn_w_in, dn_w_out, ffn_w_gate, ffn_w_up, ffn_w_down]
    big_m = [m_ret_w_in, m_ret_w_out, m_dn_w_in, m_dn_w_out, m_ffn_w_gate, m_ffn_w_up, m_ffn_w_down]
    big_v = [v_ret_w_in, v_ret_w_out, v_dn_w_in, v_dn_w_out, v_ffn_w_gate, v_ffn_w_up, v_ffn_w_down]
    res = {}
    for nm, parts, w_, m_, v_ in zip(big_names, big_parts, big_w, big_m, big_v):
        r2, c2 = parts.shape[1], parts.shape[2]
        outs = _adamw_reduce(parts, w_.reshape(r2, c2), m_.reshape(r2, c2), v_.reshape(r2, c2), f"adamw_{nm}")
        res[nm] = [o.reshape(w_.shape) for o in outs]

    small_names = ["meta_tokens", "mix_norm_w", "ffn_norm_w", "ret_gn_w", "dn_conv_w", "dn_a_log", "dn_dt_bias",
                   "dn_norm_w", "final_norm_w"]
    small_g = [gr["meta"], gr["mix_norm"], gr["ffn_norm"], gr["ret_gn"], gr["conv_w"], gr["a_log"], gr["dt_bias"],
               gr["dn_norm"], gr["final_norm"]]
    small_w = [meta_tokens, mix_norm_w, ffn_norm_w, ret_gn_w, dn_conv_w, dn_a_log, dn_dt_bias, dn_norm_w, final_norm_w]
    small_m = [m_meta_tokens, m_mix_norm_w, m_ffn_norm_w, m_ret_gn_w, m_dn_conv_w, m_dn_a_log, m_dn_dt_bias,
               m_dn_norm_w, m_final_norm_w]
    small_v = [v_meta_tokens, v_mix_norm_w, v_ffn_norm_w, v_ret_gn_w, v_dn_conv_w, v_dn_a_log, v_dn_dt_bias,
               v_dn_norm_w, v_final_norm_w]
    sharded = {"meta_tokens", "dn_conv_w", "dn_norm_w"}
    flat = jnp.concatenate([g.reshape(-1) for g in small_g])
    row = 8 * LANES
    n_flat = flat.shape[0]
    flat = jnp.pad(flat, (0, -n_flat % row)).reshape(-1, row)
    (gathered,) = _exchange([flat], True, "gather_small_grads")
    gathered = gathered.reshape(N_DEV, -1)
    pieces, off = [], 0
    for nm, g, w_ in zip(small_names, small_g, small_w):
        full = gathered[:, off:off + g.size].reshape((N_DEV,) + g.shape)
        off += g.size
        if nm in sharded:
            wloc = w_.shape[-1]
            full = lax.dynamic_slice_in_dim(full, me * wloc, wloc, axis=full.ndim - 1)
        pieces.append(full.reshape(N_DEV, -1))
    sizes = [p.shape[1] for p in pieces]
    n_loc = sum(sizes)
    pad_loc = -n_loc % row

    def pack(vs, lead):
        cat = jnp.concatenate([a.reshape(lead + (-1,)) for a in vs], axis=-1)
        cat = jnp.pad(cat, [(0, 0)] * len(lead) + [(0, pad_loc)])
        return cat.reshape(lead + (-1, row))

    outs = _adamw_reduce(pack(pieces, (N_DEV,)), pack(small_w, ()), pack(small_m, ()), pack(small_v, ()), "adamw_small")
    off = 0
    for nm, sz, w_ in zip(small_names, sizes, small_w):
        res[nm] = [o.reshape(-1)[off:off + sz].reshape(w_.shape) for o in outs]
        off += sz

    order = ["meta_tokens", "mix_norm_w", "ffn_norm_w", "ret_w_in", "ret_gn_w", "ret_w_out", "dn_w_in", "dn_conv_w",
             "dn_a_log", "dn_dt_bias", "dn_norm_w", "dn_w_out", "ffn_w_gate", "ffn_w_up", "ffn_w_down", "final_norm_w"]
    grad_x = grad_x.reshape(x.shape)
    return (loss, grad_x, *[res[nm][0] for nm in order], *[res[nm][1] for nm in order],
            *[res[nm][2] for nm in order], *[res[nm][3] for nm in order])
```

```python
import functools
import math

import jax
import jax.numpy as jnp
import numpy as np
from jax import lax
from jax.experimental import pallas as pl
from jax.experimental.pallas import tpu as pltpu

f32 = jnp.float32
bf16 = jnp.bfloat16

N_META = 16
CHUNK = 64
PAD = CHUNK - N_META
RMS_EPS = 1e-6
RET_HEADS, RET_DK, RET_DV = 4, 256, 512
RET_QK, RET_V = RET_HEADS * RET_DK, RET_HEADS * RET_DV
DN_HEADS, DN_DK, DN_DV = 8, 128, 256
DN_QK, DN_V = DN_HEADS * DN_DK, DN_HEADS * DN_DV
DN_CONV_CH = 2 * DN_QK + DN_V
CONV_K = 4
ROPE_BASE = 10000.0
ADAM_LR, ADAM_B1, ADAM_B2, ADAM_EPS, ADAM_WD, ADAM_STEP = 0.001, 0.9, 0.999, 1e-08, 0.01, 10
N_DEV = 8
AXES = ("x", "y", "c")
LANES = 128
MIB = 1024 * 1024


def _tile(n_rows, cap):
    nch = n_rows // CHUNK
    best = 1
    for d in range(1, nch + 1):
        if nch % d == 0 and d * CHUNK <= cap:
            best = d
    return best * CHUNK


def _div_tile(n, cap, align):
    best = None
    for d in range(align, min(n, cap) + 1, align):
        if n % d == 0:
            best = d
    return best if best is not None else n


def _params(sem, vmem_mb):
    return pltpu.CompilerParams(dimension_semantics=sem, vmem_limit_bytes=int(vmem_mb * MIB))


def _nn(a, b, precision=None):
    return jnp.dot(a, b, preferred_element_type=f32, precision=precision)


def _nt(a, b, precision=None):
    return lax.dot_general(a, b, (((1,), (1,)), ((), ())), preferred_element_type=f32, precision=precision)


def _tn(a, b, precision=None):
    return lax.dot_general(a, b, (((0,), (0,)), ((), ())), preferred_element_type=f32, precision=precision)


def _b(x):
    return x.astype(bf16)


def _sigmoid(x):
    return 0.5 * jnp.tanh(0.5 * x) + 0.5


def _silu(x):
    return x * _sigmoid(x)


def _dsilu(x):
    s = _sigmoid(x)
    return s * (1.0 + x * (1.0 - s))


def _peer(k):
    x, y, c = lax.axis_index("x"), lax.axis_index("y"), lax.axis_index("c")
    px = 1 - x if k & 4 else x
    py = 1 - y if k & 2 else y
    pc = 1 - c if k & 1 else c
    return (px, py, pc), 4 * px + 2 * py + pc


class _Exchange:
    def __init__(self, arrs, gather):
        self.arrs, self.gather, self.n = list(arrs), gather, len(arrs)
        self.out_shapes = [jax.ShapeDtypeStruct(((N_DEV,) + a.shape) if gather else a.shape, a.dtype) for a in arrs]
        self.specs = [pl.BlockSpec(memory_space=pltpu.HBM)] * self.n
        self.scratch = [pltpu.SemaphoreType.DMA((self.n, N_DEV - 1)), pltpu.SemaphoreType.DMA((self.n, N_DEV - 1)),
                        pltpu.SemaphoreType.DMA((self.n,))]

    def _copies(self, ins, outs, sems):
        send_sems, recv_sems, local_sems = sems
        x, y, c = lax.axis_index("x"), lax.axis_index("y"), lax.axis_index("c")
        me = 4 * x + 2 * y + c

        def copy(a, k, src, slot, to):
            return pltpu.make_async_remote_copy(
                src_ref=src, dst_ref=outs[a].at[slot], send_sem=send_sems.at[a, k], recv_sem=recv_sems.at[a, k],
                device_id=to, device_id_type=pl.DeviceIdType.MESH)

        first, passed, lands_first, lands_rest = [], [], [], []
        if not self.gather:
            local = [pltpu.make_async_copy(ins[a].at[me], outs[a].at[me], local_sems.at[a]) for a in range(self.n)]
            for k in range(1, N_DEV):
                peer, pidx = _peer(k)
                for a in range(self.n):
                    first.append(copy(a, k - 1, ins[a].at[pidx], me, peer))
                    lands_rest.append(copy(a, k - 1, ins[a].at[pidx], pidx, peer))
            return local, first, passed, lands_first, lands_rest
        local = [pltpu.make_async_copy(ins[a], outs[a].at[me], local_sems.at[a]) for a in range(self.n)]
        sibling, sibling_slot = (x, y, 1 - c), 4 * x + 2 * y + (1 - c)
        chips = [(1 - x, y), (x, 1 - y), (1 - x, 1 - y)]
        for a in range(self.n):
            first.append(copy(a, 0, ins[a], me, sibling))
            lands_rest.append(copy(a, 0, ins[a], sibling_slot, sibling))
            for j, (px, py) in enumerate(chips):
                slot, slot_other = 4 * px + 2 * py + c, 4 * px + 2 * py + (1 - c)
                first.append(copy(a, 1 + j, ins[a], me, (px, py, c)))
                lands_first.append(copy(a, 1 + j, ins[a], slot, (px, py, c)))
                passed.append(copy(a, 4 + j, outs[a].at[slot], slot, sibling))
                lands_rest.append(copy(a, 4 + j, outs[a].at[slot_other], slot_other, sibling))
        return local, first, passed, lands_first, lands_rest

    def start(self, ins, outs, sems):
        local, first, _, _, _ = self._copies(ins, outs, sems)
        for cp in local + first:
            cp.start()

    def wait(self, ins, outs, sems):
        local, first, passed, lands_first, lands_rest = self._copies(ins, outs, sems)
        for landed, onward in zip(lands_first, passed):
            landed.wait_recv()
            onward.start()
        for cp in lands_rest:
            cp.wait_recv()
        for cp in first + passed:
            cp.wait_send()
        for cp in local:
            cp.wait()


def _call(body, args, *, name, grid, in_specs, out_specs, out_shape, scratch=(), sem, vmem_mb, comm=None,
          aliases=None):
    aliases = aliases or {}
    if comm is None:
        out = pl.pallas_call(body, name=name, grid=grid, in_specs=list(in_specs), out_specs=list(out_specs),
                             out_shape=list(out_shape), scratch_shapes=list(scratch), input_output_aliases=aliases,
                             compiler_params=_params(sem, vmem_mb))(*args)
        return list(out)
    n_in, n_out, n_scr, nc = len(args), len(out_shape), len(scratch), comm.n

    def carried(*refs):
        ins, cin = refs[:n_in], refs[n_in:n_in + nc]
        o0 = n_in + nc
        outs, cout = refs[o0:o0 + n_out], refs[o0 + n_out:o0 + n_out + nc]
        s0 = o0 + n_out + nc
        scr, sems = refs[s0:s0 + n_scr], refs[s0 + n_scr:]
        first = functools.reduce(jnp.logical_and, [pl.program_id(i) == 0 for i in range(len(grid))])
        last = functools.reduce(jnp.logical_and, [pl.program_id(i) == grid[i] - 1 for i in range(len(grid))])

        @pl.when(first)
        def _():
            comm.start(cin, cout, sems)

        body(*ins, *outs, *scr)

        @pl.when(last)
        def _():
            comm.wait(cin, cout, sems)

    out = pl.pallas_call(
        carried, name=name, grid=grid, in_specs=list(in_specs) + comm.specs, out_specs=list(out_specs) + comm.specs,
        out_shape=list(out_shape) + comm.out_shapes, scratch_shapes=list(scratch) + comm.scratch,
        input_output_aliases=aliases,
        compiler_params=_params(("arbitrary",) * len(grid), vmem_mb))(*args, *comm.arrs)
    return list(out)


def _exchange(arrs, gather, name):
    comm = _Exchange(arrs, gather)

    def body(*refs):
        ins, outs, sems = refs[:comm.n], refs[comm.n:2 * comm.n], refs[2 * comm.n:]
        comm.start(ins, outs, sems)
        comm.wait(ins, outs, sems)

    return pl.pallas_call(body, name=name, in_specs=comm.specs, out_specs=comm.specs, out_shape=comm.out_shapes,
                          scratch_shapes=comm.scratch)(*comm.arrs)


def _matmul(a, b, *, mode, tm, tn, tk, name, out_dtype=f32, res=None, vmem_mb=48, comm=None, norm_w=None):
    if mode == "nn":
        (m, k), (k2, n) = a.shape, b.shape
    elif mode == "nt":
        (m, k), (n, k2) = a.shape, b.shape
    else:
        (k, m), (k2, n) = a.shape, b.shape
    assert k == k2 and m % tm == 0 and n % tn == 0 and k % tk == 0, (name, a.shape, b.shape, tm, tn, tk)
    nk = k // tk
    dot = {"nn": _nn, "nt": _nt, "tn": _tn}[mode]
    a_spec = {"nn": pl.BlockSpec((tm, tk), lambda i, j, kk: (i, kk)),
              "nt": pl.BlockSpec((tm, tk), lambda i, j, kk: (i, kk)),
              "tn": pl.BlockSpec((tk, tm), lambda i, j, kk: (kk, i))}[mode]
    b_spec = {"nn": pl.BlockSpec((tk, tn), lambda i, j, kk: (kk, j)),
              "nt": pl.BlockSpec((tn, tk), lambda i, j, kk: (j, kk)),
              "tn": pl.BlockSpec((tk, tn), lambda i, j, kk: (kk, j))}[mode]
    o_spec = pl.BlockSpec((tm, tn), lambda i, j, kk: (i, j))
    has_res = res is not None
    has_norm = norm_w is not None
    assert not has_norm or tn == n
    n_ops = 2 + has_res + has_norm

    def body(*refs):
        a_ref, b_ref = refs[:2]
        r_ref = refs[2] if has_res else None
        nw_ref = refs[2 + has_res] if has_norm else None
        o_ref = refs[n_ops]
        hn_ref = refs[n_ops + 1] if has_norm else None
        rest = refs[n_ops + 1 + has_norm:]

        def finish(tot):
            if has_res:
                tot = tot + r_ref[...]
            o_ref[...] = tot.astype(out_dtype)
            if has_norm:
                r = lax.rsqrt(jnp.mean(tot * tot, axis=-1, keepdims=True) + RMS_EPS)
                hn_ref[...] = _b(tot * r * nw_ref[...])

        if nk == 1:
            finish(dot(_b(a_ref[...]), _b(b_ref[...])))
            return
        acc_ref = rest[0]
        kk = pl.program_id(2)

        @pl.when(kk == 0)
        def _():
            acc_ref[...] = dot(_b(a_ref[...]), _b(b_ref[...]))

        @pl.when(kk > 0)
        def _():
            acc_ref[...] += dot(_b(a_ref[...]), _b(b_ref[...]))

        @pl.when(kk == nk - 1)
        def _():
            finish(acc_ref[...])

    in_specs = [a_spec, b_spec]
    args = (a, b)
    if has_res:
        in_specs.append(o_spec)
        args += (res,)
    out_specs, out_shape = [o_spec], [jax.ShapeDtypeStruct((m, n), out_dtype)]
    if has_norm:
        in_specs.append(pl.BlockSpec((1, tn), lambda i, j, kk: (0, j)))
        args += (norm_w,)
        out_specs.append(o_spec)
        out_shape.append(jax.ShapeDtypeStruct((m, n), bf16))
    out = _call(body, args, name=name, grid=(m // tm, n // tn, nk), in_specs=in_specs, out_specs=out_specs,
                out_shape=out_shape, scratch=[pltpu.VMEM((tm, tn), f32)] if nk > 1 else [],
                sem=("parallel", "parallel", "arbitrary"), vmem_mb=vmem_mb, comm=comm)
    n_own = len(out_shape)
    own = out[0] if n_own == 1 else tuple(out[:n_own])
    return own if comm is None else (own, out[n_own:])


def _rms_fwd(h, w, name):
    l, d = h.shape
    tr = _tile(l, 512)

    def body(h_ref, w_ref, o_ref):
        x = h_ref[...]
        r = lax.rsqrt(jnp.mean(x * x, axis=-1, keepdims=True) + RMS_EPS)
        o_ref[...] = _b(x * r * w_ref[...])

    return pl.pallas_call(
        body, name=name, grid=(l // tr,),
        in_specs=[pl.BlockSpec((tr, d), lambda i: (i, 0)), pl.BlockSpec((1, d), lambda i: (0, 0))],
        out_specs=pl.BlockSpec((tr, d), lambda i: (i, 0)),
        out_shape=jax.ShapeDtypeStruct((l, d), bf16),
        compiler_params=_params(("parallel",), 32),
    )(h, w)


def _dhn_norm_bwd(pairs, h, nw, dres, name, *, tm, tk, init=None, comm=None):
    l, d = h.shape
    nks = [a.shape[1] // tk for a, _ in pairs]
    starts = [sum(nks[:p]) for p in range(len(pairs))]
    nk = sum(nks)
    assert nk >= 2
    n_ops = 2 * len(pairs)
    has_init = init is not None

    def body(*refs):
        ops = refs[:n_ops]
        init_ref = refs[n_ops] if has_init else None
        h_ref, w_ref, r_ref, dh_ref, dhb_ref, dw_ref, acc_ref = refs[n_ops + has_init:]
        i, kk = pl.program_id(0), pl.program_id(1)

        @pl.when(kk == 0)
        def _():
            part = _nt(ops[0][...], ops[1][...])
            acc_ref[...] = part + init_ref[...] if has_init else part

        for p in range(len(pairs)):
            lo, hi = max(starts[p], 1), min(starts[p] + nks[p], nk - 1)

            @pl.when(jnp.logical_and(kk >= lo, kk < hi))
            def _(a_ref=ops[2 * p], b_ref=ops[2 * p + 1]):
                acc_ref[...] += _nt(a_ref[...], b_ref[...])

        @pl.when(kk == nk - 1)
        def _():
            g = acc_ref[...] + _nt(ops[-2][...], ops[-1][...])
            x = h_ref[...]
            r = lax.rsqrt(jnp.mean(x * x, axis=-1, keepdims=True) + RMS_EPS)
            xh = x * r
            dxh = g * w_ref[...]
            dh = r_ref[...] + r * (dxh - xh * jnp.mean(dxh * xh, axis=-1, keepdims=True))
            dh_ref[...] = dh
            dhb_ref[...] = _b(dh)
            dw = jnp.sum(g * xh, axis=0, keepdims=True)

            @pl.when(i == 0)
            def _():
                dw_ref[...] = dw

            @pl.when(i > 0)
            def _():
                dw_ref[...] += dw

    def k_of(p):
        return lambda kk: jnp.clip(kk - starts[p], 0, nks[p] - 1)

    in_specs, args = [], []
    for p, (a, b) in enumerate(pairs):
        in_specs += [pl.BlockSpec((tm, tk), functools.partial(lambda i, kk, f: (i, f(kk)), f=k_of(p))),
                     pl.BlockSpec((d, tk), functools.partial(lambda i, kk, f: (0, f(kk)), f=k_of(p)))]
        args += [a, b]
    row = pl.BlockSpec((tm, d), lambda i, kk: (i, 0))
    vec = pl.BlockSpec((1, d), lambda i, kk: (0, 0))
    if has_init:
        in_specs.append(row)
        args.append(init)
    return _call(body, tuple(args) + (h, nw, dres), name=name, grid=(l // tm, nk), in_specs=in_specs + [row, vec, row],
                 out_specs=[row, row, vec],
                 out_shape=[jax.ShapeDtypeStruct((l, d), f32), jax.ShapeDtypeStruct((l, d), bf16),
                            jax.ShapeDtypeStruct((1, d), f32)],
                 scratch=[pltpu.VMEM((tm, d), f32)], sem=("arbitrary", "arbitrary"), vmem_mb=48, comm=comm)


def _final_loss(h, w, target, name):
    l, d = h.shape
    nch = l // CHUNK
    cpt = _tile(l, 256) // CHUNK
    nt = nch // cpt

    def body(h_ref, w_ref, *rest):
        t_refs, (dh_ref, dhb_ref, dw_ref, loss_ref) = rest[:cpt], rest[cpt:]
        i = pl.program_id(0)
        wv = w_ref[...]
        dw = jnp.zeros((1, d), f32)
        part = jnp.zeros((1, 1), f32)
        for c in range(cpt):
            rows = slice(c * CHUNK, (c + 1) * CHUNK)
            live = (i * cpt + c > 0).astype(f32)
            x = h_ref[rows, :]
            r = lax.rsqrt(jnp.mean(x * x, axis=-1, keepdims=True) + RMS_EPS)
            xh = x * r
            err = (xh * wv - t_refs[c][...]) * live
            dy = err * (1.0 / d)
            dxh = dy * wv
            dx = r * (dxh - xh * jnp.mean(dxh * xh, axis=-1, keepdims=True))
            dh_ref[rows, :] = dx
            dhb_ref[rows, :] = _b(dx)
            dw = dw + jnp.sum(dy * xh, axis=0, keepdims=True)
            part = part + 0.5 * jnp.sum(jnp.sum(err * err, axis=-1, keepdims=True) * (1.0 / d), axis=0, keepdims=True)
        part = jnp.broadcast_to(part, (1, LANES))

        @pl.when(i == 0)
        def _():
            dw_ref[...] = dw
            loss_ref[...] = part

        @pl.when(i > 0)
        def _():
            dw_ref[...] += dw
            loss_ref[...] += part

    row = pl.BlockSpec((cpt * CHUNK, d), lambda i: (i, 0))
    vec = pl.BlockSpec((1, d), lambda i: (0, 0))
    t_specs = [pl.BlockSpec((CHUNK, d), functools.partial(lambda i, c: (jnp.maximum(i * cpt + c - 1, 0), 0), c=c))
               for c in range(cpt)]
    return pl.pallas_call(
        body, name=name, grid=(nt,),
        in_specs=[row, vec] + t_specs,
        out_specs=[row, row, vec, pl.BlockSpec((1, LANES), lambda i: (0, 0))],
        out_shape=[jax.ShapeDtypeStruct((l, d), f32), jax.ShapeDtypeStruct((l, d), bf16),
                   jax.ShapeDtypeStruct((1, d), f32), jax.ShapeDtypeStruct((1, LANES), f32)],
        compiler_params=_params(("arbitrary",), 32),
    )(h, w, *([target] * cpt))


def _ffn_gu(hn, wg, wu, name, *, tm, tn, comm=None):
    l, d = hn.shape
    fh = wg.shape[1]

    def body(h_ref, g_ref, u_ref, a_ref, b_ref, s_ref):
        hb = h_ref[...]
        a = _nn(hb, g_ref[...])
        bb = _nn(hb, u_ref[...])
        a_ref[...] = _b(a)
        b_ref[...] = _b(bb)
        s_ref[...] = _b(_silu(a) * bb)

    wspec = pl.BlockSpec((d, tn), lambda i, j: (0, j))
    ospec = pl.BlockSpec((tm, tn), lambda i, j: (i, j))
    return _call(body, (hn, wg, wu), name=name, grid=(l // tm, fh // tn),
                 in_specs=[pl.BlockSpec((tm, d), lambda i, j: (i, 0)), wspec, wspec], out_specs=[ospec] * 3,
                 out_shape=[jax.ShapeDtypeStruct((l, fh), bf16)] * 3, sem=("parallel", "parallel"), vmem_mb=48,
                 comm=comm)


def _ffn_ds(dhb, wd, a, b, *, tm, tn, name):
    l, d = dhb.shape
    fh = wd.shape[0]

    def body(g_ref, w_ref, a_ref, b_ref, da_ref, db_ref):
        ds = _nt(g_ref[...], w_ref[...])
        a = a_ref[...].astype(f32)
        da_ref[...] = _b(ds * b_ref[...].astype(f32) * _dsilu(a))
        db_ref[...] = _b(ds * _silu(a))

    ospec = pl.BlockSpec((tm, tn), lambda i, j: (i, j))
    return pl.pallas_call(
        body, name=name, grid=(l // tm, fh // tn),
        in_specs=[pl.BlockSpec((tm, d), lambda i, j: (i, 0)), pl.BlockSpec((tn, d), lambda i, j: (j, 0)), ospec, ospec],
        out_specs=[ospec, ospec], out_shape=[jax.ShapeDtypeStruct((l, fh), bf16)] * 2,
        compiler_params=_params(("parallel", "parallel"), 48),
    )(dhb, wd, a, b)


def _gnorm_fwd(o, proj, nw, heads, dv, gate_blk, name):
    l, hv = o.shape
    tr = _tile(l, 256)

    def body(o_ref, g_ref, w_ref, y_ref):
        wv = w_ref[...]
        for h in range(heads):
            sl = slice(h * dv, (h + 1) * dv)
            oh = o_ref[:, sl]
            r = lax.rsqrt(jnp.mean(oh * oh, axis=-1, keepdims=True) + RMS_EPS)
            y_ref[:, sl] = _b(oh * r * wv * _silu(g_ref[:, sl]))

    return pl.pallas_call(
        body, name=name, grid=(l // tr,),
        in_specs=[pl.BlockSpec((tr, hv), lambda i: (i, 0)), pl.BlockSpec((tr, hv), lambda i: (i, gate_blk)),
                  pl.BlockSpec((1, dv), lambda i: (0, 0))],
        out_specs=pl.BlockSpec((tr, hv), lambda i: (i, 0)),
        out_shape=jax.ShapeDtypeStruct((l, hv), bf16),
        compiler_params=_params(("parallel",), 32),
    )(o, proj, nw)


def _dy_gnorm_bwd(dhb, w_out, o, proj, nw, dv, gate_blk, name, *, tm, tn):
    l, hv = o.shape
    d = dhb.shape[1]
    nj = hv // tn
    heads = tn // dv

    def body(g_ref, w_ref, o_ref, gate_ref, nw_ref, do_ref, dg_ref, dw_ref):
        dy = _nt(g_ref[...], w_ref[...])
        wv = nw_ref[...]
        dw = jnp.zeros((1, dv), f32)
        for h in range(heads):
            sl = slice(h * dv, (h + 1) * dv)
            oh = o_ref[:, sl]
            g = gate_ref[:, sl]
            dyh = dy[:, sl]
            r = lax.rsqrt(jnp.mean(oh * oh, axis=-1, keepdims=True) + RMS_EPS)
            xh = oh * r
            dn = dyh * _silu(g)
            dg_ref[:, sl] = _b(dyh * (xh * wv) * _dsilu(g))
            dxh = dn * wv
            do_ref[:, sl] = r * (dxh - xh * jnp.mean(dxh * xh, axis=-1, keepdims=True))
            dw = dw + jnp.sum(dn * xh, axis=0, keepdims=True)
        first = jnp.logical_and(pl.program_id(0) == 0, pl.program_id(1) == 0)

        @pl.when(first)
        def _():
            dw_ref[...] = dw

        @pl.when(jnp.logical_not(first))
        def _():
            dw_ref[...] += dw

    tile = pl.BlockSpec((tm, tn), lambda i, j: (i, j))
    gate = pl.BlockSpec((tm, tn), lambda i, j: (i, gate_blk * nj + j))
    vec = pl.BlockSpec((1, dv), lambda i, j: (0, 0))
    return pl.pallas_call(
        body, name=name, grid=(l // tm, nj),
        in_specs=[pl.BlockSpec((tm, d), lambda i, j: (i, 0)), pl.BlockSpec((tn, d), lambda i, j: (j, 0)),
                  tile, gate, vec],
        out_specs=[tile, gate, vec],
        out_shape=[jax.ShapeDtypeStruct((l, hv), f32), jax.ShapeDtypeStruct(proj.shape, bf16),
                   jax.ShapeDtypeStruct((1, dv), f32)],
        compiler_params=_params(("arbitrary", "arbitrary"), 48),
    )(dhb, w_out, o, proj, nw)


def _ret_prep(proj, cos, sin, name, comm=None):
    l = proj.shape[0]
    tr = _tile(l, 256)
    half = RET_DK // 2
    scale = RET_DK ** -0.5

    def body(p_ref, c_ref, s_ref, o_ref):
        rows = pl.program_id(0) * tr + lax.broadcasted_iota(jnp.int32, (tr, 1), 0)
        kmul = jnp.where(rows >= PAD, scale, 0.0).astype(f32)
        c, s = c_ref[...], s_ref[...]
        for j in range(2 * RET_HEADS):
            t1 = p_ref[:, j * RET_DK: j * RET_DK + half]
            t2 = p_ref[:, j * RET_DK + half: (j + 1) * RET_DK]
            o1 = t1 * c - t2 * s
            o2 = t1 * s + t2 * c
            if j >= RET_HEADS:
                o1, o2 = o1 * kmul, o2 * kmul
            o_ref[:, j * RET_DK: j * RET_DK + half] = o1
            o_ref[:, j * RET_DK + half: (j + 1) * RET_DK] = o2

    wide = pl.BlockSpec((tr, 2 * RET_QK), lambda i: (i, 0))
    tab = pl.BlockSpec((tr, half), lambda i: (i, 0))
    return _call(body, (proj, cos, sin), name=name, grid=(l // tr,), in_specs=[wide, tab, tab], out_specs=[wide],
                 out_shape=[jax.ShapeDtypeStruct((l, 2 * RET_QK), f32)], sem=("parallel",), vmem_mb=32, comm=comm)


def _ret_prep_bwd(dq, dk, cos, sin, dproj, name):
    l = dq.shape[0]
    tr = _tile(l, 256)
    half = RET_DK // 2
    scale = RET_DK ** -0.5

    def body(dq_ref, dk_ref, c_ref, s_ref, _, o_ref):
        rows = pl.program_id(0) * tr + lax.broadcasted_iota(jnp.int32, (tr, 1), 0)
        kmul = jnp.where(rows >= PAD, scale, 0.0).astype(f32)
        c, s = c_ref[...], s_ref[...]
        for j in range(2 * RET_HEADS):
            d_ref = dq_ref if j < RET_HEADS else dk_ref
            jj = j % RET_HEADS
            d1 = d_ref[:, jj * RET_DK: jj * RET_DK + half]
            d2 = d_ref[:, jj * RET_DK + half: (jj + 1) * RET_DK]
            if j >= RET_HEADS:
                d1, d2 = d1 * kmul, d2 * kmul
            o_ref[:, j * RET_DK: j * RET_DK + half] = _b(d1 * c + d2 * s)
            o_ref[:, j * RET_DK + half: (j + 1) * RET_DK] = _b(d2 * c - d1 * s)

    nar = pl.BlockSpec((tr, RET_QK), lambda i: (i, 0))
    wide = pl.BlockSpec((tr, 2 * RET_QK), lambda i: (i, 0))
    tab = pl.BlockSpec((tr, half), lambda i: (i, 0))
    return pl.pallas_call(
        body, name=name, grid=(l // tr,), in_specs=[nar, nar, tab, tab, pl.BlockSpec(memory_space=pl.ANY)],
        out_specs=wide, out_shape=jax.ShapeDtypeStruct(dproj.shape, dproj.dtype), input_output_aliases={4: 0},
        compiler_params=_params(("parallel",), 32),
    )(dq, dk, cos, sin, dproj)


RET_BLOCK_CHUNKS = 3


def _ret_block(l):
    nch = l // CHUNK
    return RET_BLOCK_CHUNKS * CHUNK if nch % RET_BLOCK_CHUNKS == 0 else CHUNK


def _ret_decay(lg, rb):
    idx = lax.broadcasted_iota(jnp.int32, (rb, 1), 0).astype(f32)
    ri = lax.broadcasted_iota(jnp.int32, (rb, rb), 0)
    ci = lax.broadcasted_iota(jnp.int32, (rb, rb), 1)
    rel = (ri - ci).astype(f32)
    dmask = jnp.where(ri >= ci, jnp.exp(lg * jnp.maximum(rel, 0.0)), 0.0)
    xi = jnp.exp(lg * (idx + 1.0))
    zeta = jnp.exp(lg * (rb - 1.0 - idx))
    return dmask, xi, zeta


def _ret_scan_fwd(qk, proj, gn_w, lgs, gcs, name, comm=None):
    l = qk.shape[0]
    rb = _ret_block(l)
    nb = l // rb

    def body(lg_ref, gc_ref, q_ref, k_ref, v_ref, g_ref, nw_ref, o_ref, st_ref, y_ref, s_ref):
        @pl.when(pl.program_id(0) == 0)
        def _():
            s_ref[...] = jnp.zeros_like(s_ref)

        hs = range(RET_HEADS)
        dec = [_ret_decay(lg_ref[h], rb) for h in hs]
        q = [q_ref[:, h * RET_DK:(h + 1) * RET_DK] for h in hs]
        k = [k_ref[:, h * RET_DK:(h + 1) * RET_DK] for h in hs]
        vb = [_b(v_ref[:, h * RET_DV:(h + 1) * RET_DV]) for h in hs]
        s = [s_ref[h] for h in hs]
        sb = [_b(s[h]) for h in hs]
        scores = [_b(_nt(_b(q[h]), _b(k[h])) * dec[h][0]) for h in hs]
        inter = [_nn(_b(q[h] * dec[h][1]), sb[h]) for h in hs]
        kv = [_tn(_b(k[h] * dec[h][2]), vb[h]) for h in hs]
        nw = nw_ref[...]
        for h in hs:
            cols = slice(h * RET_DV, (h + 1) * RET_DV)
            st_ref[0, h] = sb[h]
            o = _nn(scores[h], vb[h]) + inter[h]
            o_ref[:, cols] = o
            r = lax.rsqrt(jnp.mean(o * o, axis=-1, keepdims=True) + RMS_EPS)
            y_ref[:, cols] = _b(o * r * nw * _silu(g_ref[:, cols]))
            s_ref[h] = gc_ref[h] * s[h] + kv[h]

    smem = pl.BlockSpec(memory_space=pltpu.SMEM)
    wide = pl.BlockSpec((rb, RET_V), lambda n: (n, 0))
    return _call(
        body, (lgs, gcs, qk, qk, proj, proj, gn_w), name=name, grid=(nb,),
        in_specs=[smem, smem,
                  pl.BlockSpec((rb, RET_QK), lambda n: (n, 0)),
                  pl.BlockSpec((rb, RET_QK), lambda n: (n, 1)),
                  pl.BlockSpec((rb, RET_V), lambda n: (n, 1)),
                  pl.BlockSpec((rb, RET_V), lambda n: (n, 2)),
                  pl.BlockSpec((1, RET_DV), lambda n: (0, 0))],
        out_specs=[wide, pl.BlockSpec((1, RET_HEADS, RET_DK, RET_DV), lambda n: (n, 0, 0, 0)), wide],
        out_shape=[jax.ShapeDtypeStruct((l, RET_V), f32),
                   jax.ShapeDtypeStruct((nb, RET_HEADS, RET_DK, RET_DV), bf16),
                   jax.ShapeDtypeStruct((l, RET_V), bf16)],
        scratch=[pltpu.VMEM((RET_HEADS, RET_DK, RET_DV), f32)], sem=("arbitrary",), vmem_mb=40, comm=comm)


def _ret_scan_bwd(qk, proj, states, do, dproj, lgs, gcs, name, comm=None):
    l = qk.shape[0]
    rb = _ret_block(l)
    nb = l // rb

    def body(lg_ref, gc_ref, q_ref, k_ref, v_ref, st_ref, do_ref, _, dq_ref, dk_ref, dv_ref, ds_ref):
        @pl.when(pl.program_id(0) == 0)
        def _():
            ds_ref[...] = jnp.zeros_like(ds_ref)

        hs = range(RET_HEADS)
        dec = [_ret_decay(lg_ref[h], rb) for h in hs]
        q = [q_ref[:, h * RET_DK:(h + 1) * RET_DK] for h in hs]
        k = [k_ref[:, h * RET_DK:(h + 1) * RET_DK] for h in hs]
        qb, kb = [_b(t) for t in q], [_b(t) for t in k]
        vb = [_b(v_ref[:, h * RET_DV:(h + 1) * RET_DV]) for h in hs]
        dob = [_b(do_ref[:, h * RET_DV:(h + 1) * RET_DV]) for h in hs]
        dsp = [ds_ref[h] for h in hs]
        dspb = [_b(t) for t in dsp]
        scores = [_b(_nt(qb[h], kb[h]) * dec[h][0]) for h in hs]
        dscores = [_b(_nt(dob[h], vb[h]) * dec[h][0]) for h in hs]
        for h in hs:
            dq_ref[:, h * RET_DK:(h + 1) * RET_DK] = _nn(dscores[h], kb[h]) + _nt(dob[h], st_ref[0, h]) * dec[h][1]
        for h in hs:
            dk_ref[:, h * RET_DK:(h + 1) * RET_DK] = _tn(dscores[h], qb[h]) + _nt(vb[h], dspb[h]) * dec[h][2]
        for h in hs:
            dv_ref[:, h * RET_DV:(h + 1) * RET_DV] = _b(_tn(scores[h], dob[h]) + _nn(_b(k[h] * dec[h][2]), dspb[h]))
        for h in hs:
            ds_ref[h] = gc_ref[h] * dsp[h] + _tn(_b(q[h] * dec[h][1]), dob[h])

    smem = pl.BlockSpec(memory_space=pltpu.SMEM)
    rev = lambda n: nb - 1 - n
    return _call(
        body, (lgs, gcs, qk, qk, proj, states, do, dproj), name=name, grid=(nb,),
        in_specs=[smem, smem,
                  pl.BlockSpec((rb, RET_QK), lambda n: (rev(n), 0)),
                  pl.BlockSpec((rb, RET_QK), lambda n: (rev(n), 1)),
                  pl.BlockSpec((rb, RET_V), lambda n: (rev(n), 1)),
                  pl.BlockSpec((1, RET_HEADS, RET_DK, RET_DV), lambda n: (rev(n), 0, 0, 0)),
                  pl.BlockSpec((rb, RET_V), lambda n: (rev(n), 0)),
                  pl.BlockSpec(memory_space=pl.ANY)],
        out_specs=[pl.BlockSpec((rb, RET_QK), lambda n: (rev(n), 0)),
                   pl.BlockSpec((rb, RET_QK), lambda n: (rev(n), 0)),
                   pl.BlockSpec((rb, RET_V), lambda n: (rev(n), 1))],
        out_shape=[jax.ShapeDtypeStruct((l, RET_QK), f32), jax.ShapeDtypeStruct((l, RET_QK), f32),
                   jax.ShapeDtypeStruct(dproj.shape, dproj.dtype)],
        scratch=[pltpu.VMEM((RET_HEADS, RET_DK, RET_DV), f32)], sem=("arbitrary",), vmem_mb=40, comm=comm,
        aliases={7: 2})


CONV_BLK = 512
CONV_Q_BLKS = DN_QK // CONV_BLK
HALO = 8


def _conv_tile(l):
    return _tile(l, 3072)


def _slab_rows(r):
    return pl.ds(pl.multiple_of(r * HALO, HALO), HALO)


def _conv_slab(x_ref, p_ref, r, i, tr):
    cur = x_ref[_slab_rows(r), :]
    prev = jnp.where(r > 0, x_ref[_slab_rows(jnp.maximum(r - 1, 0)), :], p_ref[...])
    row0 = i * tr + r * HALO
    cur = jnp.where(row0 >= PAD, cur, 0.0)
    prev = jnp.where(row0 - HALO >= PAD, prev, 0.0)
    lrow = lax.broadcasted_iota(jnp.int32, (HALO, 1), 0)
    shifted = [jnp.where(lrow < s, pltpu.roll(prev, s, 0), pltpu.roll(cur, s, 0)) for s in range(1, CONV_K)]
    return [cur] + shifted


def _conv_of(xs, w):
    acc = xs[0] * w[CONV_K - 1:CONV_K, :]
    for s in range(1, CONV_K):
        acc = acc + xs[s] * w[CONV_K - 1 - s:CONV_K - s, :]
    return acc


def _slab_loop(n_slabs, fn, init=None):
    return lax.fori_loop(0, n_slabs, fn, init, unroll=8)


def _dn_conv_fwd(proj, conv_w, name, comm=None):
    l = proj.shape[0]
    tr = _conv_tile(l)
    nblk = DN_CONV_CH // CONV_BLK
    heads = CONV_BLK // DN_DK

    def body(x_ref, p_ref, w_ref, o_ref):
        i, j = pl.program_id(0), pl.program_id(1)
        w = w_ref[...]

        def act(r):
            return _silu(_conv_of(_conv_slab(x_ref, p_ref, r, i, tr), w))

        def normed(scale):
            def slab(r, carry):
                a = act(r)
                outs = []
                for h in range(heads):
                    ah = a[:, h * DN_DK:(h + 1) * DN_DK]
                    outs.append(ah * (lax.rsqrt(jnp.sum(ah * ah, axis=-1, keepdims=True) + RMS_EPS) * scale))
                o_ref[_slab_rows(r), :] = jnp.concatenate(outs, axis=1)
                return carry
            return slab

        def plain(r, carry):
            o_ref[_slab_rows(r), :] = act(r)
            return carry

        @pl.when(j < CONV_Q_BLKS)
        def _():
            _slab_loop(tr // HALO, normed(DN_DK ** -0.5))

        @pl.when(jnp.logical_and(j >= CONV_Q_BLKS, j < 2 * CONV_Q_BLKS))
        def _():
            _slab_loop(tr // HALO, normed(1.0))

        @pl.when(j >= 2 * CONV_Q_BLKS)
        def _():
            _slab_loop(tr // HALO, plain)

    hb = tr // HALO
    return _call(
        body, (proj, proj, conv_w), name=name, grid=(l // tr, nblk),
        in_specs=[pl.BlockSpec((tr, CONV_BLK), lambda i, j: (i, j)),
                  pl.BlockSpec((HALO, CONV_BLK), lambda i, j: (jnp.maximum(i * hb - 1, 0), j)),
                  pl.BlockSpec((CONV_K, CONV_BLK), lambda i, j: (0, j))],
        out_specs=[pl.BlockSpec((tr, CONV_BLK), lambda i, j: (i, j))],
        out_shape=[jax.ShapeDtypeStruct((l, DN_CONV_CH), f32)],
        scratch=[], sem=("parallel", "parallel"), vmem_mb=32, comm=comm)


def _dn_conv_bwd_a(proj, conv_w, dqkv, name, comm=None):
    l = proj.shape[0]
    tr = _conv_tile(l)
    nblk = DN_CONV_CH // CONV_BLK
    heads = CONV_BLK // DN_DK

    def body(x_ref, p_ref, w_ref, d_ref, dc_ref, dw_ref, acc_ref):
        j, i = pl.program_id(0), pl.program_id(1)
        w = w_ref[...]
        acc_ref[...] = jnp.zeros_like(acc_ref)

        def run(l2_scale):
            _slab_loop(tr // HALO, slab_of(l2_scale))

        def slab_of(l2_scale):
            def slab(r, carry):
                xs = _conv_slab(x_ref, p_ref, r, i, tr)
                c = _conv_of(xs, w)
                a = _silu(c)
                dy = d_ref[_slab_rows(r), :]
                if l2_scale is None:
                    da = dy
                else:
                    parts = []
                    for h in range(heads):
                        sl = slice(h * DN_DK, (h + 1) * DN_DK)
                        ah, dyh = a[:, sl], dy[:, sl]
                        rn = lax.rsqrt(jnp.sum(ah * ah, axis=-1, keepdims=True) + RMS_EPS)
                        yh = ah * rn
                        parts.append((rn * l2_scale) * (dyh - yh * jnp.sum(dyh * yh, axis=-1, keepdims=True)))
                    da = jnp.concatenate(parts, axis=1)
                dc = da * _dsilu(c)
                dc_ref[_slab_rows(r), :] = dc
                for k in range(CONV_K):
                    acc_ref[k] += dc * xs[CONV_K - 1 - k]
                return carry
            return slab

        @pl.when(j < CONV_Q_BLKS)
        def _():
            run(DN_DK ** -0.5)

        @pl.when(jnp.logical_and(j >= CONV_Q_BLKS, j < 2 * CONV_Q_BLKS))
        def _():
            run(1.0)

        @pl.when(j >= 2 * CONV_Q_BLKS)
        def _():
            run(None)

        ksel = lax.broadcasted_iota(jnp.int32, (CONV_K, 1), 0)
        dw = jnp.zeros((CONV_K, CONV_BLK), f32)
        for k in range(CONV_K):
            dw = dw + jnp.where(ksel == k, jnp.sum(acc_ref[k], axis=0, keepdims=True), 0.0)

        @pl.when(i == 0)
        def _():
            dw_ref[...] = dw

        @pl.when(i > 0)
        def _():
            dw_ref[...] += dw

    hb = tr // HALO
    blk = pl.BlockSpec((tr, CONV_BLK), lambda j, i: (i, j))
    return _call(
        body, (proj, proj, conv_w, dqkv), name=name, grid=(nblk, l // tr),
        in_specs=[blk, pl.BlockSpec((HALO, CONV_BLK), lambda j, i: (jnp.maximum(i * hb - 1, 0), j)),
                  pl.BlockSpec((CONV_K, CONV_BLK), lambda j, i: (0, j)), blk],
        out_specs=[blk, pl.BlockSpec((CONV_K, CONV_BLK), lambda j, i: (0, j))],
        out_shape=[jax.ShapeDtypeStruct((l, DN_CONV_CH), f32), jax.ShapeDtypeStruct((CONV_K, DN_CONV_CH), f32)],
        scratch=[pltpu.VMEM((CONV_K, HALO, CONV_BLK), f32)], sem=("parallel", "arbitrary"), vmem_mb=48, comm=comm)


def _dn_conv_bwd_b(dc, conv_w, dproj, name):
    l = dc.shape[0]
    tr = _conv_tile(l)
    nblk = DN_CONV_CH // CONV_BLK
    nrow = l // tr

    n_slabs = tr // HALO
    pair = 2 * HALO

    def body(d_ref, n_ref, w_ref, _, o_ref):
        i = pl.program_id(0)
        w = w_ref[...]
        nxt_tile = jnp.where(i < nrow - 1, n_ref[...], 0.0)
        lrow = lax.broadcasted_iota(jnp.int32, (HALO, 1), 0)

        def one(r):
            cur = d_ref[_slab_rows(r), :]
            nxt = jnp.where(r < n_slabs - 1, d_ref[_slab_rows(jnp.minimum(r + 1, n_slabs - 1)), :], nxt_tile)
            acc = cur * w[CONV_K - 1:CONV_K, :]
            for s in range(1, CONV_K):
                up = jnp.where(lrow >= HALO - s, pltpu.roll(nxt, HALO - s, 0), pltpu.roll(cur, HALO - s, 0))
                acc = acc + up * w[CONV_K - 1 - s:CONV_K - s, :]
            return jnp.where(i * tr + r * HALO >= PAD, acc, 0.0)

        def two(q, carry):
            rows = pl.ds(pl.multiple_of(q * pair, pair), pair)
            o_ref[rows, :] = _b(jnp.concatenate([one(2 * q), one(2 * q + 1)], axis=0))
            return carry

        lax.fori_loop(0, n_slabs // 2, two, None, unroll=4)

    hb = tr // HALO
    nh = l // HALO
    return pl.pallas_call(
        body, name=name, grid=(nrow, nblk),
        in_specs=[pl.BlockSpec((tr, CONV_BLK), lambda i, j: (i, j)),
                  pl.BlockSpec((HALO, CONV_BLK), lambda i, j: (jnp.minimum((i + 1) * hb, nh - 1), j)),
                  pl.BlockSpec((CONV_K, CONV_BLK), lambda i, j: (0, j)),
                  pl.BlockSpec(memory_space=pl.ANY)],
        out_specs=pl.BlockSpec((tr, CONV_BLK), lambda i, j: (i, j)),
        out_shape=jax.ShapeDtypeStruct(dproj.shape, dproj.dtype), input_output_aliases={3: 0},
        compiler_params=_params(("parallel", "parallel"), 32),
    )(dc, dc, conv_w, dproj)


BA_W = LANES


def _dn_gates(ba_ref, al_ref, dt_ref, n):
    rows = n * CHUNK + lax.broadcasted_iota(jnp.int32, (CHUNK, 1), 0)
    vm = (rows >= PAD).astype(f32)
    bin_ = ba_ref[:, 0:DN_HEADS]
    z = ba_ref[:, DN_HEADS:2 * DN_HEADS] + dt_ref[...]
    sp = jnp.maximum(z, 0.0) + jnp.log1p(jnp.exp(-jnp.abs(z)))
    ea = jnp.exp(al_ref[...])
    beta = _sigmoid(bin_) * vm
    g = -ea * sp * vm
    return vm, bin_, z, ea, beta, g


def _tri():
    ri = lax.broadcasted_iota(jnp.int32, (CHUNK, CHUNK), 0)
    ci = lax.broadcasted_iota(jnp.int32, (CHUNK, CHUNK), 1)
    return ri, ci


def _split(a):
    hi = _b(a)
    return hi, _b(a - hi.astype(f32))


def _mm3(a, b, dot=_nn):
    (ah, al), (bh, bl) = _split(a), _split(b)
    return dot(ah, bh) + (dot(ah, bl) + dot(al, bh))


def _cumsum_rows(tri, g):
    tb = _b(tri)
    g1 = _b(g)
    r1 = g - g1.astype(f32)
    g2 = _b(r1)
    g3 = _b(r1 - g2.astype(f32))
    return _nn(tb, g1) + (_nn(tb, g2) + _nn(tb, g3))


DN_SCAN_CHUNKS = 3


def _scan_chunks(nch):
    return DN_SCAN_CHUNKS if nch % DN_SCAN_CHUNKS == 0 else 1


def _dn_prep(qkv, ba, a_log, dt_bias, name, comm=None):
    l = qkv.shape[0]
    nch = l // CHUNK
    heads = range(DN_HEADS)

    cb = _scan_chunks(nch)
    items = [(c, h) for c in range(cb) for h in heads]

    def body(q_ref, k_ref, v_ref, ba_ref, al_ref, dt_ref, t_ref, u_ref, wq_ref, pk_ref, eg_ref, kpt_ref, qwt_ref):
        n0 = pl.program_id(0) * cb
        ri, ci = _tri()
        incl, strict = ri >= ci, ri > ci
        eye = (ri == ci).astype(f32)
        rows = [slice(c * CHUNK, (c + 1) * CHUNK) for c in range(cb)]
        gam, gam_t, beta = [], [], []
        for c in range(cb):
            _, _, _, _, beta_c, g_c = _dn_gates(ba_ref[rows[c], :], al_ref, dt_ref, n0 + c)
            gam.append(_cumsum_rows(incl.astype(f32), g_c))
            gam_t.append(gam[c].T)
            beta.append(beta_c)
        gc = {(c, h): gam[c][:, h:h + 1] for c, h in items}
        bh = {(c, h): beta[c][:, h:h + 1] for c, h in items}
        kh = {(c, h): k_ref[rows[c], h * DN_DK:(h + 1) * DN_DK] for c, h in items}
        kb = {i: _b(kh[i]) for i in items}
        decay = {(c, h): jnp.exp(jnp.where(incl, gc[c, h] - gam_t[c][h:h + 1, :], -jnp.inf)) for c, h in items}
        a = {i: jnp.where(strict, bh[i] * _nt(kb[i], kb[i]) * decay[i], 0.0) for i in items}
        t = {i: eye - a[i] for i in items}
        p = a
        for level in range(int(math.log2(CHUNK)) - 1):
            mm = _mm3 if level < 2 else (lambda x, y: _nn(_b(x), _b(y)))
            p = {i: mm(p[i], p[i]) for i in items}
            t = {i: t[i] + mm(t[i], p[i]) for i in items}
        eg = {i: jnp.exp(gc[i]) for i in items}
        for c, h in items:
            i = (c, h)
            t_ref[c, h] = t[i]
            u_ref[rows[c], h * DN_DV:(h + 1) * DN_DV] = _mm3(t[i], v_ref[rows[c], h * DN_DV:(h + 1) * DN_DV] * bh[i])
            w = _mm3(t[i], kh[i] * (bh[i] * eg[i]))
            wq_ref[c, h, 0:CHUNK, :] = _b(w)
            qwt_ref[c, h, DN_DK:2 * DN_DK, :] = _b(w.T)
        for c, h in items:
            i = (c, h)
            qh = q_ref[rows[c], h * DN_DK:(h + 1) * DN_DK]
            gl = gc[i][CHUNK - 1:CHUNK, :]
            qe = qh * eg[i]
            ke = kh[i] * jnp.exp(gl - gc[i])
            pmat = _nt(_b(qh), kb[i]) * decay[i]
            wq_ref[c, h, CHUNK:2 * CHUNK, :] = _b(qe)
            qwt_ref[c, h, 0:DN_DK, :] = _b(qe.T)
            pk_ref[c, h, 0:CHUNK, :] = _b(pmat)
            pk_ref[c, h, CHUNK:CHUNK + DN_DK, :] = _b(ke.T)
            kpt_ref[c, h, :, 0:DN_DK] = _b(ke)
            kpt_ref[c, h, :, DN_DK:DN_DK + CHUNK] = _b(pmat.T)
            eg_ref[c, h] = jnp.broadcast_to(jnp.exp(gl), (8, LANES))

    vec = pl.BlockSpec((1, DN_HEADS), lambda n: (0, 0))
    return _call(
        body, (qkv, qkv, qkv, ba, a_log, dt_bias), name=name, grid=(nch // cb,),
        in_specs=[pl.BlockSpec((cb * CHUNK, DN_QK), lambda n: (n, 0)),
                  pl.BlockSpec((cb * CHUNK, DN_QK), lambda n: (n, 1)),
                  pl.BlockSpec((cb * CHUNK, DN_V), lambda n: (n, 1)),
                  pl.BlockSpec((cb * CHUNK, BA_W), lambda n: (n, 0)), vec, vec],
        out_specs=[pl.BlockSpec((cb, DN_HEADS, CHUNK, CHUNK), lambda n: (n, 0, 0, 0)),
                   pl.BlockSpec((cb * CHUNK, DN_V), lambda n: (n, 0)),
                   pl.BlockSpec((cb, DN_HEADS, 2 * CHUNK, DN_DK), lambda n: (n, 0, 0, 0)),
                   pl.BlockSpec((cb, DN_HEADS, CHUNK + DN_DK, CHUNK), lambda n: (n, 0, 0, 0)),
                   pl.BlockSpec((cb, DN_HEADS, 8, LANES), lambda n: (n, 0, 0, 0)),
                   pl.BlockSpec((cb, DN_HEADS, CHUNK, DN_DK + CHUNK), lambda n: (n, 0, 0, 0)),
                   pl.BlockSpec((cb, DN_HEADS, 2 * DN_DK, CHUNK), lambda n: (n, 0, 0, 0))],
        out_shape=[jax.ShapeDtypeStruct((nch, DN_HEADS, CHUNK, CHUNK), f32),
                   jax.ShapeDtypeStruct((l, DN_V), f32),
                   jax.ShapeDtypeStruct((nch, DN_HEADS, 2 * CHUNK, DN_DK), bf16),
                   jax.ShapeDtypeStruct((nch, DN_HEADS, CHUNK + DN_DK, CHUNK), bf16),
                   jax.ShapeDtypeStruct((nch, DN_HEADS, 8, LANES), f32),
                   jax.ShapeDtypeStruct((nch, DN_HEADS, CHUNK, DN_DK + CHUNK), bf16),
                   jax.ShapeDtypeStruct((nch, DN_HEADS, 2 * DN_DK, CHUNK), bf16)],
        sem=("parallel",), vmem_mb=40, comm=comm)


def _dn_scan_fwd(u, wq, pk, egl, name):
    l = u.shape[0]
    nch = l // CHUNK
    cs = _scan_chunks(nch)

    def body(u_ref, wq_ref, pk_ref, eg_ref, o_ref, st_ref, vn_ref, s_ref):
        @pl.when(pl.program_id(0) == 0)
        def _():
            s_ref[...] = jnp.zeros_like(s_ref)

        hs = range(DN_HEADS)
        cols = [slice(h * DN_DV, (h + 1) * DN_DV) for h in hs]
        s = [s_ref[h] for h in hs]
        for c in range(cs):
            rows = slice(c * CHUNK, (c + 1) * CHUNK)
            sb = [_b(s[h]) for h in hs]
            x = [_nn(wq_ref[c, h], sb[h]) for h in hs]
            vnb = [_b(u_ref[rows, cols[h]] - x[h][0:CHUNK]) for h in hs]
            y = [_nn(pk_ref[c, h], vnb[h]) for h in hs]
            for h in hs:
                st_ref[c, h] = sb[h]
                vn_ref[rows, cols[h]] = vnb[h]
                o_ref[rows, cols[h]] = x[h][CHUNK:2 * CHUNK] + y[h][0:CHUNK]
            s = [eg_ref[c, h][0:1, 0:1] * s[h] + y[h][CHUNK:CHUNK + DN_DK] for h in hs]
        for h in hs:
            s_ref[h] = s[h]

    return pl.pallas_call(
        body, name=name, grid=(nch // cs,),
        in_specs=[pl.BlockSpec((cs * CHUNK, DN_V), lambda n: (n, 0)),
                  pl.BlockSpec((cs, DN_HEADS, 2 * CHUNK, DN_DK), lambda n: (n, 0, 0, 0)),
                  pl.BlockSpec((cs, DN_HEADS, CHUNK + DN_DK, CHUNK), lambda n: (n, 0, 0, 0)),
                  pl.BlockSpec((cs, DN_HEADS, 8, LANES), lambda n: (n, 0, 0, 0))],
        out_specs=[pl.BlockSpec((cs * CHUNK, DN_V), lambda n: (n, 0)),
                   pl.BlockSpec((cs, DN_HEADS, DN_DK, DN_DV), lambda n: (n, 0, 0, 0)),
                   pl.BlockSpec((cs * CHUNK, DN_V), lambda n: (n, 0))],
        out_shape=[jax.ShapeDtypeStruct((l, DN_V), f32),
                   jax.ShapeDtypeStruct((nch, DN_HEADS, DN_DK, DN_DV), bf16),
                   jax.ShapeDtypeStruct((l, DN_V), bf16)],
        scratch_shapes=[pltpu.VMEM((DN_HEADS, DN_DK, DN_DV), f32)],
        compiler_params=_params(("arbitrary",), 40),
    )(u, wq, pk, egl)


def _dn_scan_bwd(do, kpt, qwt, egl, name):
    l = do.shape[0]
    nch = l // CHUNK
    cs = _scan_chunks(nch)
    nblk = nch // cs

    def body(do_ref, kpt_ref, qwt_ref, eg_ref, dvn_ref, dsp_ref, ds_ref):
        @pl.when(pl.program_id(0) == 0)
        def _():
            ds_ref[...] = jnp.zeros_like(ds_ref)

        hs = range(DN_HEADS)
        cols = [slice(h * DN_DV, (h + 1) * DN_DV) for h in hs]
        ds = [ds_ref[h] for h in hs]
        for c in reversed(range(cs)):
            rows = slice(c * CHUNK, (c + 1) * CHUNK)
            dspb = [_b(ds[h]) for h in hs]
            dob = [_b(do_ref[rows, cols[h]]) for h in hs]
            dvn = [_nn(kpt_ref[c, h][:, 0:DN_DK], dspb[h]) + _nn(kpt_ref[c, h][:, DN_DK:DN_DK + CHUNK], dob[h])
                   for h in hs]
            for h in hs:
                dsp_ref[c, h] = dspb[h]
                dvn_ref[rows, cols[h]] = dvn[h]
            ds = [eg_ref[c, h][0:1, 0:1] * ds[h] + _nn(qwt_ref[c, h][0:DN_DK], dob[h])
                  - _nn(qwt_ref[c, h][DN_DK:2 * DN_DK], _b(dvn[h])) for h in hs]
        for h in hs:
            ds_ref[h] = ds[h]

    rev = lambda s: nblk - 1 - s
    return pl.pallas_call(
        body, name=name, grid=(nblk,),
        in_specs=[pl.BlockSpec((cs * CHUNK, DN_V), lambda s: (rev(s), 0)),
                  pl.BlockSpec((cs, DN_HEADS, CHUNK, DN_DK + CHUNK), lambda s: (rev(s), 0, 0, 0)),
                  pl.BlockSpec((cs, DN_HEADS, 2 * DN_DK, CHUNK), lambda s: (rev(s), 0, 0, 0)),
                  pl.BlockSpec((cs, DN_HEADS, 8, LANES), lambda s: (rev(s), 0, 0, 0))],
        out_specs=[pl.BlockSpec((cs * CHUNK, DN_V), lambda s: (rev(s), 0)),
                   pl.BlockSpec((cs, DN_HEADS, DN_DK, DN_DV), lambda s: (rev(s), 0, 0, 0))],
        out_shape=[jax.ShapeDtypeStruct((l, DN_V), f32),
                   jax.ShapeDtypeStruct((nch, DN_HEADS, DN_DK, DN_DV), bf16)],
        scratch_shapes=[pltpu.VMEM((DN_HEADS, DN_DK, DN_DV), f32)],
        compiler_params=_params(("arbitrary",), 40),
    )(do, kpt, qwt, egl)


def _dn_post_bwd(qkv, ba, a_log, dt_bias, states, dsp_all, tinv_all, u_all, wq, vn_all, do, dvn_all, name):
    l = qkv.shape[0]
    nch = l // CHUNK
    cb = _scan_chunks(nch)
    items = [(c, h) for c in range(cb) for h in range(DN_HEADS)]

    def body(q_ref, k_ref, v_ref, ba_ref, al_ref, dt_ref, st_ref, dsp_ref, t_ref, u_ref, wq_ref, vn_ref, do_ref,
             dvn_ref, dqkv_ref, dba_ref, dal_ref, ddt_ref):
        step = pl.program_id(0)
        ri, ci = _tri()
        incl, strict = ri >= ci, ri > ci
        lane8 = lax.broadcasted_iota(jnp.int32, (1, DN_HEADS), 1)
        sub8 = lax.broadcasted_iota(jnp.int32, (DN_HEADS, 1), 0)
        last = (lax.broadcasted_iota(jnp.int32, (CHUNK, 1), 0) == CHUNK - 1).astype(f32)
        rsum = lambda t: jnp.sum(t, axis=-1, keepdims=True)
        rows = [slice(c * CHUNK, (c + 1) * CHUNK) for c in range(cb)]
        gates = [_dn_gates(ba_ref[rows[c], :], al_ref, dt_ref, step * cb + c) for c in range(cb)]
        gam = [_cumsum_rows(incl.astype(f32), gates[c][5]) for c in range(cb)]
        gam_t = [gam[c].T for c in range(cb)]
        each = lambda fn: {(c, h): fn(c, h) for c, h in items}
        dk_cols = lambda h: slice(h * DN_DK, (h + 1) * DN_DK)
        dv_cols = lambda h: slice(h * DN_DV, (h + 1) * DN_DV)
        gc = each(lambda c, h: gam[c][:, h:h + 1])
        bh = each(lambda c, h: gates[c][4][:, h:h + 1])
        qh = each(lambda c, h: q_ref[rows[c], dk_cols(h)])
        kh = each(lambda c, h: k_ref[rows[c], dk_cols(h)])
        doh = each(lambda c, h: _b(do_ref[rows[c], dv_cols(h)]))
        sb = each(lambda c, h: st_ref[c, h])
        dspb = each(lambda c, h: dsp_ref[c, h])
        vnb = each(lambda c, h: vn_ref[rows[c], dv_cols(h)])
        dvn = each(lambda c, h: dvn_ref[rows[c], dv_cols(h)])
        wb = each(lambda c, h: wq_ref[c, h, 0:CHUNK, :])
        decay = each(lambda c, h: jnp.exp(jnp.where(incl, gc[c, h] - gam_t[c][h:h + 1, :], -jnp.inf)))
        qb, kb = each(lambda c, h: _b(qh[c, h])), each(lambda c, h: _b(kh[c, h]))
        eg = each(lambda c, h: jnp.exp(gc[c, h]))
        gl = each(lambda c, h: gc[c, h][CHUNK - 1:CHUNK, :])
        ekd = each(lambda c, h: jnp.exp(gl[c, h] - gc[c, h]))
        dvnb = each(lambda c, h: _b(dvn[c, h]))
        kk = each(lambda c, h: _nt(kb[c, h], kb[c, h]))
        p = each(lambda c, h: _nt(qb[c, h], kb[c, h]) * decay[c, h])
        dpraw = each(lambda c, h: _nt(doh[c, h], vnb[c, h]))
        dqe = each(lambda c, h: _nt(doh[c, h], sb[c, h]))
        dke = each(lambda c, h: _nt(vnb[c, h], dspb[c, h]))
        dw = each(lambda c, h: -_nt(dvnb[c, h], sb[c, h]))
        dru = each(lambda c, h: _mm3(t_ref[c, h], dvn[c, h], _tn))
        drw = each(lambda c, h: _mm3(t_ref[c, h], dw[c, h], _tn))
        dqk = each(lambda c, h: _b(dpraw[c, h] * decay[c, h]))
        for c, h in items:
            i = (c, h)
            dqkv_ref[rows[c], dk_cols(h)] = _nn(dqk[i], kb[i]) + dqe[i] * eg[i]
            dqkv_ref[rows[c], 2 * DN_QK + h * DN_DV:2 * DN_QK + (h + 1) * DN_DV] = bh[i] * dru[i]
        da = each(lambda c, h: jnp.where(strict, -(_nt(_b(dru[c, h]), _b(u_ref[rows[c], dv_cols(h)]))
                                                   + _nt(_b(drw[c, h]), wb[c, h])), 0.0))
        dkk = each(lambda c, h: _b(da[c, h] * bh[c, h] * decay[c, h]))
        for c, h in items:
            i = (c, h)
            dqkv_ref[rows[c], DN_QK + h * DN_DK:DN_QK + (h + 1) * DN_DK] = (
                _tn(dqk[i], qb[i]) + dke[i] * ekd[i] + (bh[i] * eg[i]) * drw[i]
                + _nn(dkk[i], kb[i]) + _tn(dkk[i], kb[i]))
        dal = jnp.zeros((1, DN_HEADS), f32)
        ddt = jnp.zeros((1, DN_HEADS), f32)
        dba_ref[...] = jnp.zeros_like(dba_ref)
        for c in range(cb):
            vm, bin_, z, ea, _, g = gates[c]
            dbeta = jnp.zeros((CHUNK, DN_HEADS), f32)
            dgam = jnp.zeros((CHUNK, DN_HEADS), f32)
            dgam_neg_t = jnp.zeros((DN_HEADS, CHUNK), f32)
            for h in range(DN_HEADS):
                i = (c, h)
                keg = kh[i] * eg[i]
                ke = kh[i] * ekd[i]
                rw = rsum(drw[i] * keg)
                rke = rsum(dke[i] * ke)
                db_h = rsum(dru[i] * v_ref[rows[c], dv_cols(h)]) + rw + rsum(da[i] * kk[i] * decay[i])
                mm = da[i] * (bh[i] * kk[i] * decay[i]) + dpraw[i] * p[i]
                dgl = (jnp.sum(rke, axis=0, keepdims=True)
                       + jnp.exp(gl[i]) * jnp.sum(rsum(dspb[i].astype(f32) * sb[i].astype(f32)), axis=0,
                                                  keepdims=True))
                dg_h = rsum(mm) + rw * bh[i] + rsum(dqe[i] * (qh[i] * eg[i])) - rke + last * dgl
                dbeta = dbeta + jnp.where(lane8 == h, db_h, 0.0)
                dgam = dgam + jnp.where(lane8 == h, dg_h, 0.0)
                dgam_neg_t = dgam_neg_t + jnp.where(sub8 == h, jnp.sum(mm, axis=0, keepdims=True), 0.0)
            dgam = dgam - dgam_neg_t.T
            dg = _cumsum_rows((ri <= ci).astype(f32), dgam)
            sg = _sigmoid(bin_)
            dain = dg * (-ea) * vm * _sigmoid(z)
            dba_ref[rows[c], 0:DN_HEADS] = dbeta * vm * sg * (1.0 - sg)
            dba_ref[rows[c], DN_HEADS:2 * DN_HEADS] = dain
            dal = dal + jnp.sum(dg * g, axis=0, keepdims=True)
            ddt = ddt + jnp.sum(dain, axis=0, keepdims=True)

        @pl.when(step == 0)
        def _():
            dal_ref[...] = dal
            ddt_ref[...] = ddt

        @pl.when(step > 0)
        def _():
            dal_ref[...] += dal
            ddt_ref[...] += ddt

    vec = pl.BlockSpec((1, DN_HEADS), lambda s: (0, 0))
    qs = pl.BlockSpec((cb * CHUNK, DN_QK), lambda s: (s, 0))
    ks = pl.BlockSpec((cb * CHUNK, DN_QK), lambda s: (s, 1))
    vs = pl.BlockSpec((cb * CHUNK, DN_V), lambda s: (s, 1))
    v0 = pl.BlockSpec((cb * CHUNK, DN_V), lambda s: (s, 0))
    st = pl.BlockSpec((cb, DN_HEADS, DN_DK, DN_DV), lambda s: (s, 0, 0, 0))
    return pl.pallas_call(
        body, name=name, grid=(nch // cb,),
        in_specs=[qs, ks, vs, pl.BlockSpec((cb * CHUNK, BA_W), lambda s: (s, 0)), vec, vec, st, st,
                  pl.BlockSpec((cb, DN_HEADS, CHUNK, CHUNK), lambda s: (s, 0, 0, 0)),
                  v0, pl.BlockSpec((cb, DN_HEADS, 2 * CHUNK, DN_DK), lambda s: (s, 0, 0, 0)), v0, v0, v0],
        out_specs=[pl.BlockSpec((cb * CHUNK, DN_CONV_CH), lambda s: (s, 0)),
                   pl.BlockSpec((cb * CHUNK, BA_W), lambda s: (s, 0)), vec, vec],
        out_shape=[jax.ShapeDtypeStruct((l, DN_CONV_CH), f32), jax.ShapeDtypeStruct((l, BA_W), f32),
                   jax.ShapeDtypeStruct((1, DN_HEADS), f32), jax.ShapeDtypeStruct((1, DN_HEADS), f32)],
        compiler_params=_params(("arbitrary",), 48),
    )(qkv, qkv, qkv, ba, a_log, dt_bias, states, dsp_all, tinv_all, u_all, wq, vn_all, do, dvn_all)


def _ffn_fwd(h, hn, wg, wu, wd, tb, th, tag, plan, next_norm_w=None):
    fh, d = wd.shape
    a, b, s = plan.call(f"{tag}_gu", functools.partial(_ffn_gu, tm=th // 2, tn=fh // 2), hn, wg, wu, n_out=3)
    out = plan.matmul(f"{tag}_down", s, wd, mode="nn", tm=th // 2, tn=d, tk=fh, res=h, norm_w=next_norm_w)
    return out, (hn, a, b, s)


def _ffn_bwd(dh, dhb, h, nw, wg, wu, wd, saved, tb, th, tag, plan):
    hn, a, b, s = saved
    d = h.shape[1]
    fh = wd.shape[0]
    layer = tag[-1]
    gr = plan.grads
    da, db = _ffn_ds(dhb, wd, a, b, tm=th // 2, tn=fh // 2, name=f"{tag}_b_ds")
    gr["down" + layer] = _matmul(s, dhb, mode="tn", tm=fh // 2, tn=d, tk=th, out_dtype=bf16, name=f"{tag}_b_dwd")
    dh2, dh2b, dnw = plan.call(f"{tag}_b_dhn", functools.partial(_dhn_norm_bwd, tm=th // 2, tk=fh // 2),
                               [(da, wg), (db, wu)], h, nw, dh, n_out=3)
    gr["gate" + layer] = _matmul(hn, da, mode="tn", tm=d, tn=fh // 2, tk=th, out_dtype=bf16, name=f"{tag}_b_dwg")
    gr["up" + layer] = _matmul(hn, db, mode="tn", tm=d, tn=fh // 2, tk=th, out_dtype=bf16, name=f"{tag}_b_dwu")
    return dh2, dh2b, dnw


class _Plan:
    GATHERS = {"ret_proj": ("ret_out", "gate0"), "ret_prep": ("up0",), "ret_scan": ("down0", "dn_in_top"),
               "ffn0_gu": ("dn_in_bottom",), "dn_proj": ("dn_out",), "dn_conv": ("gate1",),
               "dn_prep": ("up1", "down1")}
    SCATTERS = {"ffn1_b_dhn": ("down1",), "dn_b_conv_a": ("gate1", "up1", "dn_out"), "ffn0_b_dhn": ("dn_in",),
                "ret_b_scan": ("gate0", "up0"), "ret_b_dwin": ("down0", "ret_out"), "ret_b_dhn": ("ret_in",)}

    def __init__(self, shards, wts):
        self.shards, self.wts, self.grads, self.parts = shards, wts, {}, {}

    def _exchange(self, stage):
        if self.shards is None:
            return None
        if stage in self.GATHERS:
            return _Exchange([self.shards[n] for n in self.GATHERS[stage]], True)
        if stage in self.SCATTERS:
            return _Exchange([self._dev_major(n) for n in self.SCATTERS[stage]], False)
        return None

    def _dev_major(self, name):
        g = self.grads
        if name[:-1] in ("gate", "up"):
            return _dev_major_cols(g[name], g[name].shape[1] // N_DEV)
        if name[:-1] == "down":
            dwd = g[name]
            return dwd.reshape(N_DEV, dwd.shape[0] // N_DEV, dwd.shape[1])
        if name in ("ret_out", "dn_out"):
            return g[name].reshape(N_DEV, g[name].shape[0] // N_DEV, g[name].shape[1])
        return _dev_major_cols(g[name], self.shards[name].shape[-1])

    def _landed(self, stage, outs):
        if stage in self.SCATTERS:
            self.parts.update(zip(self.SCATTERS[stage], outs))
            return
        w = self.wts
        cols = lambda t: t.transpose(1, 0, 2).reshape(t.shape[1], N_DEV * t.shape[2])
        rows = lambda t: t.reshape(N_DEV * t.shape[1], t.shape[2])
        for name, t in zip(self.GATHERS[stage], outs):
            if name in ("ret_out", "dn_out") or name.startswith("down"):
                w[name] = rows(t)
            else:
                w[name] = cols(t)
        if "dn_in_top" in w and "dn_in_bottom" in w and "dn_main" not in w:
            full = jnp.concatenate([w["dn_in_top"], w["dn_in_bottom"]], axis=0)
            n_main = DN_CONV_CH + DN_V
            w["dn_main"] = full[:, :n_main]
            w["dn_ba"] = jnp.pad(full[:, n_main:], ((0, 0), (0, BA_W - (full.shape[1] - n_main))))

    def matmul(self, stage, a, b, **kw):
        comm = self._exchange(stage)
        if comm is None:
            return _matmul(a, b, name=stage, **kw)
        out, landed = _matmul(a, b, name=stage, comm=comm, **kw)
        self._landed(stage, landed)
        return out

    def call(self, stage, fn, *args, n_out):
        comm = self._exchange(stage)
        out = fn(*args, stage, comm=comm)
        if comm is not None:
            self._landed(stage, out[n_out:])
        return out[:n_out]


def _local_step(x2, target, wts, shards=None):
    plan = _Plan(shards, wts)
    s_len, d = x2.shape
    l = s_len + CHUNK
    tb = _tile(l, 3072)
    th = tb // 2 if (tb // 2) % 16 == 0 else tb
    half = RET_DK // 2
    inv_freq = (np.float32(ROPE_BASE) ** (-np.arange(half, dtype=np.float32) / np.float32(half))).astype(np.float32)
    ang = (np.arange(l) - PAD).astype(np.float32)[:, None] * inv_freq[None, :]
    cos, sin = jnp.asarray(np.cos(ang), f32), jnp.asarray(np.sin(ang), f32)
    lgs = jnp.log1p(-jnp.exp2(-5.0 - jnp.arange(RET_HEADS, dtype=f32)))
    gcs = jnp.exp(lgs * _ret_block(l))

    h0 = jnp.concatenate([jnp.zeros((PAD, d), f32), wts["meta"], x2], axis=0)
    mixw, ffnw = wts["mix_norm"], wts["ffn_norm"]

    hn0 = _rms_fwd(h0, mixw[0:1], "l0_norm")
    proj0 = plan.matmul("ret_proj", hn0, wts["ret_in"], mode="nn", tm=tb, tn=512, tk=d)
    (qk0,) = plan.call("ret_prep", _ret_prep, proj0, cos, sin, n_out=1)
    o0, st0, y0 = plan.call("ret_scan", _ret_scan_fwd, qk0, proj0, wts["ret_gn"], lgs, gcs, n_out=3)
    h1, hn1 = _matmul(y0, wts["ret_out"], mode="nn", tm=th // 2, tn=d, tk=RET_V, res=h0, norm_w=ffnw[0:1],
                      name="ret_out")
    (h2, hn2), ffn0 = _ffn_fwd(h1, hn1, wts["gate0"], wts["up0"], wts["down0"], tb, th, "ffn0", plan,
                               next_norm_w=mixw[1:2])

    proj1 = plan.matmul("dn_proj", hn2, wts["dn_main"], mode="nn", tm=tb, tn=512, tk=d)
    ba = _matmul(hn2, wts["dn_ba"], mode="nn", tm=tb, tn=BA_W, tk=d, name="dn_proj_ba")
    (qkv1,) = plan.call("dn_conv", _dn_conv_fwd, proj1, wts["conv_w"], n_out=1)
    tinv1, u1, wq1, pk1, egl1, kpt1, qwt1 = plan.call("dn_prep", _dn_prep, qkv1, ba, wts["a_log"], wts["dt_bias"],
                                                      n_out=7)
    o1, st1, vn1 = _dn_scan_fwd(u1, wq1, pk1, egl1, "dn_scan")
    y1 = _gnorm_fwd(o1, proj1, wts["dn_norm"], DN_HEADS, DN_DV, 2, "dn_gnorm")
    h3, hn3 = _matmul(y1, wts["dn_out"], mode="nn", tm=th // 2, tn=d, tk=DN_V, res=h2, norm_w=ffnw[1:2],
                      name="dn_out")
    h4, ffn1 = _ffn_fwd(h3, hn3, wts["gate1"], wts["up1"], wts["down1"], tb, th, "ffn1", plan)

    dh4, dh4b, dfinal, loss = _final_loss(h4, wts["final_norm"], target, "final_loss")
    gr = plan.grads
    dh3, dh3b, dffn1 = _ffn_bwd(dh4, dh4b, h3, ffnw[1:2], wts["gate1"], wts["up1"], wts["down1"], ffn1,
                                tb, th, "ffn1", plan)

    gr["dn_out"] = _matmul(y1, dh3b, mode="tn", tm=1024, tn=d, tk=tb, out_dtype=bf16, name="dn_b_dwout")
    do1, dproj1, ddn_norm = _dy_gnorm_bwd(dh3b, wts["dn_out"], o1, proj1, wts["dn_norm"], DN_DV, 2, "dn_b_gnorm",
                                          tm=th // 2, tn=1024)
    dvn1, dsp1 = _dn_scan_bwd(do1, kpt1, qwt1, egl1, "dn_b_scan")
    dqkv1, dba, dalog, ddt = _dn_post_bwd(qkv1, ba, wts["a_log"], wts["dt_bias"], st1, dsp1, tinv1, u1, wq1, vn1,
                                          do1, dvn1, "dn_b_post")
    dc1, dconv = plan.call("dn_b_conv_a", _dn_conv_bwd_a, proj1, wts["conv_w"], dqkv1, n_out=2)
    dproj1 = _dn_conv_bwd_b(dc1, wts["conv_w"], dproj1, "dn_b_conv_b")
    dbab = dba.astype(bf16)
    n_main = dproj1.shape[1]
    dhn2_ba = _matmul(dbab, wts["dn_ba"], mode="nt", tm=th, tn=d, tk=BA_W, name="dn_b_dhn_ba")
    dh2, dh2b, dmix1 = plan.call("dn_b_dhn", functools.partial(_dhn_norm_bwd, tm=th // 2, tk=n_main // 4,
                                                              init=dhn2_ba),
                                 [(dproj1, wts["dn_main"])], h2, mixw[1:2], dh3, n_out=3)
    dw_main = _matmul(hn2, dproj1, mode="tn", tm=d, tn=512, tk=tb, out_dtype=bf16, name="dn_b_dwin")
    dw_ba = _matmul(hn2, dbab, mode="tn", tm=d, tn=BA_W, tk=tb, out_dtype=bf16, name="dn_b_dwin_ba")
    gr["dn_in"] = jnp.concatenate([dw_main, dw_ba], axis=1)

    dh1, dh1b, dffn0 = _ffn_bwd(dh2, dh2b, h1, ffnw[0:1], wts["gate0"], wts["up0"], wts["down0"], ffn0,
                                tb, th, "ffn0", plan)

    gr["ret_out"] = _matmul(y0, dh1b, mode="tn", tm=1024, tn=d, tk=tb, out_dtype=bf16, name="ret_b_dwout")
    do0, dproj0, dret_gn = _dy_gnorm_bwd(dh1b, wts["ret_out"], o0, proj0, wts["ret_gn"], RET_DV, 2, "ret_b_gnorm",
                                         tm=th // 2, tn=1024)
    dq0, dk0, dproj0 = plan.call("ret_b_scan", _ret_scan_bwd, qk0, proj0, st0, do0, dproj0, lgs, gcs, n_out=3)
    dproj0 = _ret_prep_bwd(dq0, dk0, cos, sin, dproj0, "ret_b_prep")
    n_in = dproj0.shape[1]
    gr["ret_in"] = plan.matmul("ret_b_dwin", hn0, dproj0, mode="tn", tm=d, tn=512, tk=tb, out_dtype=bf16)
    dh0, _, dmix0 = plan.call("ret_b_dhn", functools.partial(_dhn_norm_bwd, tm=th // 2, tk=n_in // 4),
                              [(dproj0, wts["ret_in"])], h0, mixw[0:1], dh1, n_out=3)

    gr.update(meta=dh0[PAD:CHUNK], mix_norm=jnp.concatenate([dmix0, dmix1], axis=0),
              ffn_norm=jnp.concatenate([dffn0, dffn1], axis=0), ret_gn=dret_gn, conv_w=dconv, a_log=dalog,
              dt_bias=ddt, dn_norm=ddn_norm, final_norm=dfinal)
    return loss, dh0[CHUNK:], gr, plan


def _adamw_reduce(parts, w, m, v, name):
    _, r, c = parts.shape
    c_pad = -(-c // LANES) * LANES
    tr = _div_tile(r, max(8, (3 * MIB // 16) // c_pad // 8 * 8), 16)

    def body(p_ref, w_ref, m_ref, v_ref, g_ref, d_ref, nm_ref, nv_ref):
        g = p_ref[0].astype(f32)
        for s in range(1, N_DEV):
            g = g + p_ref[s].astype(f32)
        mm = ADAM_B1 * m_ref[...] + (1.0 - ADAM_B1) * g
        vv = ADAM_B2 * v_ref[...] + (1.0 - ADAM_B2) * (g * g)
        m_hat = mm / (1.0 - ADAM_B1 ** ADAM_STEP)
        v_hat = vv / (1.0 - ADAM_B2 ** ADAM_STEP)
        g_ref[...] = g
        d_ref[...] = -ADAM_LR * (m_hat / (jnp.sqrt(v_hat) + ADAM_EPS) + ADAM_WD * w_ref[...])
        nm_ref[...] = mm
        nv_ref[...] = vv

    blk = pl.BlockSpec((tr, c), lambda i: (i, 0))
    return pl.pallas_call(
        body, name=name, grid=(r // tr,),
        in_specs=[pl.BlockSpec((N_DEV, tr, c), lambda i: (0, i, 0)), blk, blk, blk], out_specs=[blk] * 4,
        out_shape=[jax.ShapeDtypeStruct((r, c), f32)] * 4,
        compiler_params=_params(("parallel",), 48),
    )(parts, w, m, v)


def _dev_major_cols(g, width):
    r = g.shape[0]
    return g[:, :N_DEV * width].reshape(r, N_DEV, width).transpose(1, 0, 2)


def kernel(x, meta_tokens, mix_norm_w, ffn_norm_w, ret_w_in, ret_gn_w, ret_w_out, dn_w_in, dn_conv_w, dn_a_log, dn_dt_bias, dn_norm_w, dn_w_out, ffn_w_gate, ffn_w_up, ffn_w_down, final_norm_w, loss_target, m_meta_tokens, m_mix_norm_w, m_ffn_norm_w, m_ret_w_in, m_ret_gn_w, m_ret_w_out, m_dn_w_in, m_dn_conv_w, m_dn_a_log, m_dn_dt_bias, m_dn_norm_w, m_dn_w_out, m_ffn_w_gate, m_ffn_w_up, m_ffn_w_down, m_final_norm_w, v_meta_tokens, v_mix_norm_w, v_ffn_norm_w, v_ret_w_in, v_ret_gn_w, v_ret_w_out, v_dn_w_in, v_dn_conv_w, v_dn_a_log, v_dn_dt_bias, v_dn_norm_w, v_dn_w_out, v_ffn_w_gate, v_ffn_w_up, v_ffn_w_down, v_final_norm_w):
    d = x.shape[-1]
    me = 4 * lax.axis_index("x") + 2 * lax.axis_index("y") + lax.axis_index("c")

    shards = dict(ret_in=ret_w_in[0].astype(bf16), ret_out=ret_w_out[0].astype(bf16),
                  dn_in=dn_w_in[0].astype(bf16), dn_out=dn_w_out[0].astype(bf16))
    shards["dn_in_top"], shards["dn_in_bottom"] = shards["dn_in"][:d // 2], shards["dn_in"][d // 2:]
    for layer in (0, 1):
        shards[f"gate{layer}"] = ffn_w_gate[layer].astype(bf16)
        shards[f"up{layer}"] = ffn_w_up[layer].astype(bf16)
        shards[f"down{layer}"] = ffn_w_down[layer].astype(bf16)
    g_ret_in, g_meta, g_conv, g_dnn = _exchange([shards["ret_in"], meta_tokens, dn_conv_w[0], dn_norm_w], True,
                                                "gather_first")
    cols = lambda g: g.transpose(1, 0, 2).reshape(g.shape[1], N_DEV * g.shape[2])
    wts = dict(meta=cols(g_meta), mix_norm=mix_norm_w, ffn_norm=ffn_norm_w, ret_in=cols(g_ret_in), ret_gn=ret_gn_w,
               conv_w=cols(g_conv), a_log=dn_a_log, dt_bias=dn_dt_bias, dn_norm=cols(g_dnn),
               final_norm=final_norm_w.reshape(1, d))

    loss_part, grad_x, gr, plan = _local_step(x[0], loss_target[0], wts, shards)
    loss = lax.psum(loss_part[0, 0], AXES)

    pp = plan.parts
    both = lambda name: jnp.concatenate([pp[name + "0"], pp[name + "1"]], axis=1)
    big_parts = [pp["ret_in"], pp["ret_out"], pp["dn_in"], pp["dn_out"], both("gate"), both("up"), both("down")]
    big_names = ["ret_w_in", "ret_w_out", "dn_w_in", "dn_w_out", "ffn_w_gate", "ffn_w_up", "ffn_w_down"]
    big_w = [ret_w_in, ret_w_out, dn_w_in, dn_w_out, ffn_w_gate, ffn_w_up, ffn_w_down]
    big_m = [m_ret_w_in, m_ret_w_out, m_dn_w_in, m_dn_w_out, m_ffn_w_gate, m_ffn_w_up, m_ffn_w_down]
    big_v = [v_ret_w_in, v_ret_w_out, v_dn_w_in, v_dn_w_out, v_ffn_w_gate, v_ffn_w_up, v_ffn_w_down]
    res = {}
    for nm, parts, w_, m_, v_ in zip(big_names, big_parts, big_w, big_m, big_v):
        r2, c2 = parts.shape[1], parts.shape[2]
        outs = _adamw_reduce(parts, w_.reshape(r2, c2), m_.reshape(r2, c2), v_.reshape(r2, c2), f"adamw_{nm}")
        res[nm] = [o.reshape(w_.shape) for o in outs]

    small_names = ["meta_tokens", "mix_norm_w", "ffn_norm_w", "ret_gn_w", "dn_conv_w", "dn_a_log", "dn_dt_bias",
                   "dn_norm_w", "final_norm_w"]
    small_g = [gr["meta"], gr["mix_norm"], gr["ffn_norm"], gr["ret_gn"], gr["conv_w"], gr["a_log"], gr["dt_bias"],
               gr["dn_norm"], gr["final_norm"]]
    small_w = [meta_tokens, mix_norm_w, ffn_norm_w, ret_gn_w, dn_conv_w, dn_a_log, dn_dt_bias, dn_norm_w, final_norm_w]
    small_m = [m_meta_tokens, m_mix_norm_w, m_ffn_norm_w, m_ret_gn_w, m_dn_conv_w, m_dn_a_log, m_dn_dt_bias,
               m_dn_norm_w, m_final_norm_w]
    small_v = [v_meta_tokens, v_mix_norm_w, v_ffn_norm_w, v_ret_gn_w, v_dn_conv_w, v_dn_a_log, v_dn_dt_bias,
               v_dn_norm_w, v_final_norm_w]
    sharded = {"meta_tokens", "dn_conv_w", "dn_norm_w"}
    flat = jnp.concatenate([g.reshape(-1) for g in small_g])
    row = 8 * LANES
    n_flat = flat.shape[0]
    flat = jnp.pad(flat, (0, -n_flat % row)).reshape(-1, row)
    (gathered,) = _exchange([flat], True, "gather_small_grads")
    gathered = gathered.reshape(N_DEV, -1)
    pieces, off = [], 0
    for nm, g, w_ in zip(small_names, small_g, small_w):
        full = gathered[:, off:off + g.size].reshape((N_DEV,) + g.shape)
        off += g.size
        if nm in sharded:
            wloc = w_.shape[-1]
            full = lax.dynamic_slice_in_dim(full, me * wloc, wloc, axis=full.ndim - 1)
        pieces.append(full.reshape(N_DEV, -1))
    sizes = [p.shape[1] for p in pieces]
    n_loc = sum(sizes)
    pad_loc = -n_loc % row

    def pack(vs, lead):
        cat = jnp.concatenate([a.reshape(lead + (-1,)) for a in vs], axis=-1)
        cat = jnp.pad(cat, [(0, 0)] * len(lead) + [(0, pad_loc)])
        return cat.reshape(lead + (-1, row))

    outs = _adamw_reduce(pack(pieces, (N_DEV,)), pack(small_w, ()), pack(small_m, ()), pack(small_v, ()), "adamw_small")
    off = 0
    for nm, sz, w_ in zip(small_names, sizes, small_w):
        res[nm] = [o.reshape(-1)[off:off + sz].reshape(w_.shape) for o in outs]
        off += sz

    order = ["meta_tokens", "mix_norm_w", "ffn_norm_w", "ret_w_in", "ret_gn_w", "ret_w_out", "dn_w_in", "dn_conv_w",
             "dn_a_log", "dn_dt_bias", "dn_norm_w", "dn_w_out", "ffn_w_gate", "ffn_w_up", "ffn_w_down", "final_norm_w"]
    grad_x = grad_x.reshape(x.shape)
    return (loss, grad_x, *[res[nm][0] for nm in order], *[res[nm][1] for nm in order],
            *[res[nm][2] for nm in order], *[res[nm][3] for nm in order])
```

```python
import functools
import math

import jax
import jax.numpy as jnp
import numpy as np
from jax import lax
from jax.experimental import pallas as pl
from jax.experimental.pallas import tpu as pltpu

f32 = jnp.float32
bf16 = jnp.bfloat16

N_META = 16
CHUNK = 64
PAD = CHUNK - N_META
RMS_EPS = 1e-6
RET_HEADS, RET_DK, RET_DV = 4, 256, 512
RET_QK, RET_V = RET_HEADS * RET_DK, RET_HEADS * RET_DV
DN_HEADS, DN_DK, DN_DV = 8, 128, 256
DN_QK, DN_V = DN_HEADS * DN_DK, DN_HEADS * DN_DV
DN_CONV_CH = 2 * DN_QK + DN_V
CONV_K = 4
ROPE_BASE = 10000.0
ADAM_LR, ADAM_B1, ADAM_B2, ADAM_EPS, ADAM_WD, ADAM_STEP = 0.001, 0.9, 0.999, 1e-08, 0.01, 10
N_DEV = 8
AXES = ("x", "y", "c")
LANES = 128
MIB = 1024 * 1024


def _tile(n_rows, cap):
    nch = n_rows // CHUNK
    best = 1
    for d in range(1, nch + 1):
        if nch % d == 0 and d * CHUNK <= cap:
            best = d
    return best * CHUNK


def _div_tile(n, cap, align):
    best = None
    for d in range(align, min(n, cap) + 1, align):
        if n % d == 0:
            best = d
    return best if best is not None else n


def _params(sem, vmem_mb):
    return pltpu.CompilerParams(dimension_semantics=sem, vmem_limit_bytes=int(vmem_mb * MIB))


def _nn(a, b, precision=None):
    return jnp.dot(a, b, preferred_element_type=f32, precision=precision)


def _nt(a, b, precision=None):
    return lax.dot_general(a, b, (((1,), (1,)), ((), ())), preferred_element_type=f32, precision=precision)


def _tn(a, b, precision=None):
    return lax.dot_general(a, b, (((0,), (0,)), ((), ())), preferred_element_type=f32, precision=precision)


def _b(x):
    return x.astype(bf16)


def _sigmoid(x):
    return 0.5 * jnp.tanh(0.5 * x) + 0.5


def _silu(x):
    return x * _sigmoid(x)


def _dsilu(x):
    s = _sigmoid(x)
    return s * (1.0 + x * (1.0 - s))


def _peer(k):
    x, y, c = lax.axis_index("x"), lax.axis_index("y"), lax.axis_index("c")
    px = 1 - x if k & 4 else x
    py = 1 - y if k & 2 else y
    pc = 1 - c if k & 1 else c
    return (px, py, pc), 4 * px + 2 * py + pc


class _Exchange:
    def __init__(self, arrs, gather):
        self.arrs, self.gather, self.n = list(arrs), gather, len(arrs)
        self.out_shapes = [jax.ShapeDtypeStruct(((N_DEV,) + a.shape) if gather else a.shape, a.dtype) for a in arrs]
        self.specs = [pl.BlockSpec(memory_space=pltpu.HBM)] * self.n
        self.scratch = [pltpu.SemaphoreType.DMA((self.n, N_DEV - 1)), pltpu.SemaphoreType.DMA((self.n, N_DEV - 1)),
                        pltpu.SemaphoreType.DMA((self.n,))]

    def _copies(self, ins, outs, sems):
        send_sems, recv_sems, local_sems = sems
        x, y, c = lax.axis_index("x"), lax.axis_index("y"), lax.axis_index("c")
        me = 4 * x + 2 * y + c

        def copy(a, k, src, slot, to):
            return pltpu.make_async_remote_copy(
                src_ref=src, dst_ref=outs[a].at[slot], send_sem=send_sems.at[a, k], recv_sem=recv_sems.at[a, k],
                device_id=to, device_id_type=pl.DeviceIdType.MESH)

        first, passed, lands_first, lands_rest = [], [], [], []
        if not self.gather:
            local = [pltpu.make_async_copy(ins[a].at[me], outs[a].at[me], local_sems.at[a]) for a in range(self.n)]
            for k in range(1, N_DEV):
                peer, pidx = _peer(k)
                for a in range(self.n):
                    first.append(copy(a, k - 1, ins[a].at[pidx], me, peer))
                    lands_rest.append(copy(a, k - 1, ins[a].at[pidx], pidx, peer))
            return local, first, passed, lands_first, lands_rest
        local = [pltpu.make_async_copy(ins[a], outs[a].at[me], local_sems.at[a]) for a in range(self.n)]
        sibling, sibling_slot = (x, y, 1 - c), 4 * x + 2 * y + (1 - c)
        chips = [(1 - x, y), (x, 1 - y), (1 - x, 1 - y)]
        for a in range(self.n):
            first.append(copy(a, 0, ins[a], me, sibling))
            lands_rest.append(copy(a, 0, ins[a], sibling_slot, sibling))
            for j, (px, py) in enumerate(chips):
                slot, slot_other = 4 * px + 2 * py + c, 4 * px + 2 * py + (1 - c)
                first.append(copy(a, 1 + j, ins[a], me, (px, py, c)))
                lands_first.append(copy(a, 1 + j, ins[a], slot, (px, py, c)))
                passed.append(copy(a, 4 + j, outs[a].at[slot], slot, sibling))
                lands_rest.append(copy(a, 4 + j, outs[a].at[slot_other], slot_other, sibling))
        return local, first, passed, lands_first, lands_rest

    def start(self, ins, outs, sems):
        local, first, _, _, _ = self._copies(ins, outs, sems)
        for cp in local + first:
            cp.start()

    def wait(self, ins, outs, sems):
        local, first, passed, lands_first, lands_rest = self._copies(ins, outs, sems)
        for landed, onward in zip(lands_first, passed):
            landed.wait_recv()
            onward.start()
        for cp in lands_rest:
            cp.wait_recv()
        for cp in first + passed:
            cp.wait_send()
        for cp in local:
            cp.wait()


def _call(body, args, *, name, grid, in_specs, out_specs, out_shape, scratch=(), sem, vmem_mb, comm=None,
          aliases=None):
    aliases = aliases or {}
    if comm is None:
        out = pl.pallas_call(body, name=name, grid=grid, in_specs=list(in_specs), out_specs=list(out_specs),
                             out_shape=list(out_shape), scratch_shapes=list(scratch), input_output_aliases=aliases,
                             compiler_params=_params(sem, vmem_mb))(*args)
        return list(out)
    n_in, n_out, n_scr, nc = len(args), len(out_shape), len(scratch), comm.n

    def carried(*refs):
        ins, cin = refs[:n_in], refs[n_in:n_in + nc]
        o0 = n_in + nc
        outs, cout = refs[o0:o0 + n_out], refs[o0 + n_out:o0 + n_out + nc]
        s0 = o0 + n_out + nc
        scr, sems = refs[s0:s0 + n_scr], refs[s0 + n_scr:]
        first = functools.reduce(jnp.logical_and, [pl.program_id(i) == 0 for i in range(len(grid))])
        last = functools.reduce(jnp.logical_and, [pl.program_id(i) == grid[i] - 1 for i in range(len(grid))])

        @pl.when(first)
        def _():
            comm.start(cin, cout, sems)

        body(*ins, *outs, *scr)

        @pl.when(last)
        def _():
            comm.wait(cin, cout, sems)

    out = pl.pallas_call(
        carried, name=name, grid=grid, in_specs=list(in_specs) + comm.specs, out_specs=list(out_specs) + comm.specs,
        out_shape=list(out_shape) + comm.out_shapes, scratch_shapes=list(scratch) + comm.scratch,
        input_output_aliases=aliases,
        compiler_params=_params(("arbitrary",) * len(grid), vmem_mb))(*args, *comm.arrs)
    return list(out)


def _exchange(arrs, gather, name):
    comm = _Exchange(arrs, gather)

    def body(*refs):
        ins, outs, sems = refs[:comm.n], refs[comm.n:2 * comm.n], refs[2 * comm.n:]
        comm.start(ins, outs, sems)
        comm.wait(ins, outs, sems)

    return pl.pallas_call(body, name=name, in_specs=comm.specs, out_specs=comm.specs, out_shape=comm.out_shapes,
                          scratch_shapes=comm.scratch)(*comm.arrs)


def _matmul(a, b, *, mode, tm, tn, tk, name, out_dtype=f32, res=None, vmem_mb=48, comm=None, norm_w=None):
    if mode == "nn":
        (m, k), (k2, n) = a.shape, b.shape
    elif mode == "nt":
        (m, k), (n, k2) = a.shape, b.shape
    else:
        (k, m), (k2, n) = a.shape, b.shape
    assert k == k2 and m % tm == 0 and n % tn == 0 and k % tk == 0, (name, a.shape, b.shape, tm, tn, tk)
    nk = k // tk
    dot = {"nn": _nn, "nt": _nt, "tn": _tn}[mode]
    a_spec = {"nn": pl.BlockSpec((tm, tk), lambda i, j, kk: (i, kk)),
              "nt": pl.BlockSpec((tm, tk), lambda i, j, kk: (i, kk)),
              "tn": pl.BlockSpec((tk, tm), lambda i, j, kk: (kk, i))}[mode]
    b_spec = {"nn": pl.BlockSpec((tk, tn), lambda i, j, kk: (kk, j)),
              "nt": pl.BlockSpec((tn, tk), lambda i, j, kk: (j, kk)),
              "tn": pl.BlockSpec((tk, tn), lambda i, j, kk: (kk, j))}[mode]
    o_spec = pl.BlockSpec((tm, tn), lambda i, j, kk: (i, j))
    has_res = res is not None
    has_norm = norm_w is not None
    assert not has_norm or tn == n
    n_ops = 2 + has_res + has_norm

    def body(*refs):
        a_ref, b_ref = refs[:2]
        r_ref = refs[2] if has_res else None
        nw_ref = refs[2 + has_res] if has_norm else None
        o_ref = refs[n_ops]
        hn_ref = refs[n_ops + 1] if has_norm else None
        rest = refs[n_ops + 1 + has_norm:]

        def finish(tot):
            if has_res:
                tot = tot + r_ref[...]
            o_ref[...] = tot.astype(out_dtype)
            if has_norm:
                r = lax.rsqrt(jnp.mean(tot * tot, axis=-1, keepdims=True) + RMS_EPS)
                hn_ref[...] = _b(tot * r * nw_ref[...])

        if nk == 1:
            finish(dot(_b(a_ref[...]), _b(b_ref[...])))
            return
        acc_ref = rest[0]
        kk = pl.program_id(2)

        @pl.when(kk == 0)
        def _():
            acc_ref[...] = dot(_b(a_ref[...]), _b(b_ref[...]))

        @pl.when(kk > 0)
        def _():
            acc_ref[...] += dot(_b(a_ref[...]), _b(b_ref[...]))

        @pl.when(kk == nk - 1)
        def _():
            finish(acc_ref[...])

    in_specs = [a_spec, b_spec]
    args = (a, b)
    if has_res:
        in_specs.append(o_spec)
        args += (res,)
    out_specs, out_shape = [o_spec], [jax.ShapeDtypeStruct((m, n), out_dtype)]
    if has_norm:
        in_specs.append(pl.BlockSpec((1, tn), lambda i, j, kk: (0, j)))
        args += (norm_w,)
        out_specs.append(o_spec)
        out_shape.append(jax.ShapeDtypeStruct((m, n), bf16))
    out = _call(body, args, name=name, grid=(m // tm, n // tn, nk), in_specs=in_specs, out_specs=out_specs,
                out_shape=out_shape, scratch=[pltpu.VMEM((tm, tn), f32)] if nk > 1 else [],
                sem=("parallel", "parallel", "arbitrary"), vmem_mb=vmem_mb, comm=comm)
    n_own = len(out_shape)
    own = out[0] if n_own == 1 else tuple(out[:n_own])
    return own if comm is None else (own, out[n_own:])


def _rms_fwd(h, w, name):
    l, d = h.shape
    tr = _tile(l, 512)

    def body(h_ref, w_ref, o_ref):
        x = h_ref[...]
        r = lax.rsqrt(jnp.mean(x * x, axis=-1, keepdims=True) + RMS_EPS)
        o_ref[...] = _b(x * r * w_ref[...])

    return pl.pallas_call(
        body, name=name, grid=(l // tr,),
        in_specs=[pl.BlockSpec((tr, d), lambda i: (i, 0)), pl.BlockSpec((1, d), lambda i: (0, 0))],
        out_specs=pl.BlockSpec((tr, d), lambda i: (i, 0)),
        out_shape=jax.ShapeDtypeStruct((l, d), bf16),
        compiler_params=_params(("parallel",), 32),
    )(h, w)


def _dhn_norm_bwd(pairs, h, nw, dres, name, *, tm, tk, init=None, comm=None):
    l, d = h.shape
    nks = [a.shape[1] // tk for a, _ in pairs]
    starts = [sum(nks[:p]) for p in range(len(pairs))]
    nk = sum(nks)
    assert nk >= 2
    n_ops = 2 * len(pairs)
    has_init = init is not None

    def body(*refs):
        ops = refs[:n_ops]
        init_ref = refs[n_ops] if has_init else None
        h_ref, w_ref, r_ref, dh_ref, dhb_ref, dw_ref, acc_ref = refs[n_ops + has_init:]
        i, kk = pl.program_id(0), pl.program_id(1)

        @pl.when(kk == 0)
        def _():
            part = _nt(ops[0][...], ops[1][...])
            acc_ref[...] = part + init_ref[...] if has_init else part

        for p in range(len(pairs)):
            lo, hi = max(starts[p], 1), min(starts[p] + nks[p], nk - 1)

            @pl.when(jnp.logical_and(kk >= lo, kk < hi))
            def _(a_ref=ops[2 * p], b_ref=ops[2 * p + 1]):
                acc_ref[...] += _nt(a_ref[...], b_ref[...])

        @pl.when(kk == nk - 1)
        def _():
            g = acc_ref[...] + _nt(ops[-2][...], ops[-1][...])
            x = h_ref[...]
            r = lax.rsqrt(jnp.mean(x * x, axis=-1, keepdims=True) + RMS_EPS)
            xh = x * r
            dxh = g * w_ref[...]
            dh = r_ref[...] + r * (dxh - xh * jnp.mean(dxh * xh, axis=-1, keepdims=True))
            dh_ref[...] = dh
            dhb_ref[...] = _b(dh)
            dw = jnp.sum(g * xh, axis=0, keepdims=True)

            @pl.when(i == 0)
            def _():
                dw_ref[...] = dw

            @pl.when(i > 0)
            def _():
                dw_ref[...] += dw

    def k_of(p):
        return lambda kk: jnp.clip(kk - starts[p], 0, nks[p] - 1)

    in_specs, args = [], []
    for p, (a, b) in enumerate(pairs):
        in_specs += [pl.BlockSpec((tm, tk), functools.partial(lambda i, kk, f: (i, f(kk)), f=k_of(p))),
                     pl.BlockSpec((d, tk), functools.partial(lambda i, kk, f: (0, f(kk)), f=k_of(p)))]
        args += [a, b]
    row = pl.BlockSpec((tm, d), lambda i, kk: (i, 0))
    vec = pl.BlockSpec((1, d), lambda i, kk: (0, 0))
    if has_init:
        in_specs.append(row)
        args.append(init)
    return _call(body, tuple(args) + (h, nw, dres), name=name, grid=(l // tm, nk), in_specs=in_specs + [row, vec, row],
                 out_specs=[row, row, vec],
                 out_shape=[jax.ShapeDtypeStruct((l, d), f32), jax.ShapeDtypeStruct((l, d), bf16),
                            jax.ShapeDtypeStruct((1, d), f32)],
                 scratch=[pltpu.VMEM((tm, d), f32)], sem=("arbitrary", "arbitrary"), vmem_mb=48, comm=comm)


def _final_loss(h, w, target, name):
    l, d = h.shape
    nch = l // CHUNK
    cpt = _tile(l, 256) // CHUNK
    nt = nch // cpt

    def body(h_ref, w_ref, *rest):
        t_refs, (dh_ref, dhb_ref, dw_ref, loss_ref) = rest[:cpt], rest[cpt:]
        i = pl.program_id(0)
        wv = w_ref[...]
        dw = jnp.zeros((1, d), f32)
        part = jnp.zeros((1, 1), f32)
        for c in range(cpt):
            rows = slice(c * CHUNK, (c + 1) * CHUNK)
            live = (i * cpt + c > 0).astype(f32)
            x = h_ref[rows, :]
            r = lax.rsqrt(jnp.mean(x * x, axis=-1, keepdims=True) + RMS_EPS)
            xh = x * r
            err = (xh * wv - t_refs[c][...]) * live
            dy = err * (1.0 / d)
            dxh = dy * wv
            dx = r * (dxh - xh * jnp.mean(dxh * xh, axis=-1, keepdims=True))
            dh_ref[rows, :] = dx
            dhb_ref[rows, :] = _b(dx)
            dw = dw + jnp.sum(dy * xh, axis=0, keepdims=True)
            part = part + 0.5 * jnp.sum(jnp.sum(err * err, axis=-1, keepdims=True) * (1.0 / d), axis=0, keepdims=True)
        part = jnp.broadcast_to(part, (1, LANES))

        @pl.when(i == 0)
        def _():
            dw_ref[...] = dw
            loss_ref[...] = part

        @pl.when(i > 0)
        def _():
            dw_ref[...] += dw
            loss_ref[...] += part

    row = pl.BlockSpec((cpt * CHUNK, d), lambda i: (i, 0))
    vec = pl.BlockSpec((1, d), lambda i: (0, 0))
    t_specs = [pl.BlockSpec((CHUNK, d), functools.partial(lambda i, c: (jnp.maximum(i * cpt + c - 1, 0), 0), c=c))
               for c in range(cpt)]
    return pl.pallas_call(
        body, name=name, grid=(nt,),
        in_specs=[row, vec] + t_specs,
        out_specs=[row, row, vec, pl.BlockSpec((1, LANES), lambda i: (0, 0))],
        out_shape=[jax.ShapeDtypeStruct((l, d), f32), jax.ShapeDtypeStruct((l, d), bf16),
                   jax.ShapeDtypeStruct((1, d), f32), jax.ShapeDtypeStruct((1, LANES), f32)],
        compiler_params=_params(("arbitrary",), 32),
    )(h, w, *([target] * cpt))


def _ffn_gu(hn, wg, wu, name, *, tm, tn, comm=None):
    l, d = hn.shape
    fh = wg.shape[1]

    def body(h_ref, g_ref, u_ref, a_ref, b_ref, s_ref):
        hb = h_ref[...]
        a = _nn(hb, g_ref[...])
        bb = _nn(hb, u_ref[...])
        a_ref[...] = _b(a)
        b_ref[...] = _b(bb)
        s_ref[...] = _b(_silu(a) * bb)

    wspec = pl.BlockSpec((d, tn), lambda i, j: (0, j))
    ospec = pl.BlockSpec((tm, tn), lambda i, j: (i, j))
    return _call(body, (hn, wg, wu), name=name, grid=(l // tm, fh // tn),
                 in_specs=[pl.BlockSpec((tm, d), lambda i, j: (i, 0)), wspec, wspec], out_specs=[ospec] * 3,
                 out_shape=[jax.ShapeDtypeStruct((l, fh), bf16)] * 3, sem=("parallel", "parallel"), vmem_mb=48,
                 comm=comm)


def _ffn_ds(dhb, wd, a, b, *, tm, tn, name):
    l, d = dhb.shape
    fh = wd.shape[0]

    def body(g_ref, w_ref, a_ref, b_ref, da_ref, db_ref):
        ds = _nt(g_ref[...], w_ref[...])
        a = a_ref[...].astype(f32)
        da_ref[...] = _b(ds * b_ref[...].astype(f32) * _dsilu(a))
        db_ref[...] = _b(ds * _silu(a))

    ospec = pl.BlockSpec((tm, tn), lambda i, j: (i, j))
    return pl.pallas_call(
        body, name=name, grid=(l // tm, fh // tn),
        in_specs=[pl.BlockSpec((tm, d), lambda i, j: (i, 0)), pl.BlockSpec((tn, d), lambda i, j: (j, 0)), ospec, ospec],
        out_specs=[ospec, ospec], out_shape=[jax.ShapeDtypeStruct((l, fh), bf16)] * 2,
        compiler_params=_params(("parallel", "parallel"), 48),
    )(dhb, wd, a, b)


def _gnorm_fwd(o, proj, nw, heads, dv, gate_blk, name):
    l, hv = o.shape
    tr = _tile(l, 256)

    def body(o_ref, g_ref, w_ref, y_ref):
        wv = w_ref[...]
        for h in range(heads):
            sl = slice(h * dv, (h + 1) * dv)
            oh = o_ref[:, sl]
            r = lax.rsqrt(jnp.mean(oh * oh, axis=-1, keepdims=True) + RMS_EPS)
            y_ref[:, sl] = _b(oh * r * wv * _silu(g_ref[:, sl]))

    return pl.pallas_call(
        body, name=name, grid=(l // tr,),
        in_specs=[pl.BlockSpec((tr, hv), lambda i: (i, 0)), pl.BlockSpec((tr, hv), lambda i: (i, gate_blk)),
                  pl.BlockSpec((1, dv), lambda i: (0, 0))],
        out_specs=pl.BlockSpec((tr, hv), lambda i: (i, 0)),
        out_shape=jax.ShapeDtypeStruct((l, hv), bf16),
        compiler_params=_params(("parallel",), 32),
    )(o, proj, nw)


def _dy_gnorm_bwd(dhb, w_out, o, proj, nw, dv, gate_blk, name, *, tm, tn):
    l, hv = o.shape
    d = dhb.shape[1]
    nj = hv // tn
    heads = tn // dv

    def body(g_ref, w_ref, o_ref, gate_ref, nw_ref, do_ref, dg_ref, dw_ref):
        dy = _nt(g_ref[...], w_ref[...])
        wv = nw_ref[...]
        dw = jnp.zeros((1, dv), f32)
        for h in range(heads):
            sl = slice(h * dv, (h + 1) * dv)
            oh = o_ref[:, sl]
            g = gate_ref[:, sl]
            dyh = dy[:, sl]
            r = lax.rsqrt(jnp.mean(oh * oh, axis=-1, keepdims=True) + RMS_EPS)
            xh = oh * r
            dn = dyh * _silu(g)
            dg_ref[:, sl] = _b(dyh * (xh * wv) * _dsilu(g))
            dxh = dn * wv
            do_ref[:, sl] = r * (dxh - xh * jnp.mean(dxh * xh, axis=-1, keepdims=True))
            dw = dw + jnp.sum(dn * xh, axis=0, keepdims=True)
        first = jnp.logical_and(pl.program_id(0) == 0, pl.program_id(1) == 0)

        @pl.when(first)
        def _():
            dw_ref[...] = dw

        @pl.when(jnp.logical_not(first))
        def _():
            dw_ref[...] += dw

    tile = pl.BlockSpec((tm, tn), lambda i, j: (i, j))
    gate = pl.BlockSpec((tm, tn), lambda i, j: (i, gate_blk * nj + j))
    vec = pl.BlockSpec((1, dv), lambda i, j: (0, 0))
    return pl.pallas_call(
        body, name=name, grid=(l // tm, nj),
        in_specs=[pl.BlockSpec((tm, d), lambda i, j: (i, 0)), pl.BlockSpec((tn, d), lambda i, j: (j, 0)),
                  tile, gate, vec],
        out_specs=[tile, gate, vec],
        out_shape=[jax.ShapeDtypeStruct((l, hv), f32), jax.ShapeDtypeStruct(proj.shape, bf16),
                   jax.ShapeDtypeStruct((1, dv), f32)],
        compiler_params=_params(("arbitrary", "arbitrary"), 48),
    )(dhb, w_out, o, proj, nw)


def _ret_prep(proj, cos, sin, name, comm=None):
    l = proj.shape[0]
    tr = _tile(l, 256)
    half = RET_DK // 2
    scale = RET_DK ** -0.5

    def body(p_ref, c_ref, s_ref, o_ref):
        rows = pl.program_id(0) * tr + lax.broadcasted_iota(jnp.int32, (tr, 1), 0)
        kmul = jnp.where(rows >= PAD, scale, 0.0).astype(f32)
        c, s = c_ref[...], s_ref[...]
        for j in range(2 * RET_HEADS):
            t1 = p_ref[:, j * RET_DK: j * RET_DK + half]
            t2 = p_ref[:, j * RET_DK + half: (j + 1) * RET_DK]
            o1 = t1 * c - t2 * s
            o2 = t1 * s + t2 * c
            if j >= RET_HEADS:
                o1, o2 = o1 * kmul, o2 * kmul
            o_ref[:, j * RET_DK: j * RET_DK + half] = o1
            o_ref[:, j * RET_DK + half: (j + 1) * RET_DK] = o2

    wide = pl.BlockSpec((tr, 2 * RET_QK), lambda i: (i, 0))
    tab = pl.BlockSpec((tr, half), lambda i: (i, 0))
    return _call(body, (proj, cos, sin), name=name, grid=(l // tr,), in_specs=[wide, tab, tab], out_specs=[wide],
                 out_shape=[jax.ShapeDtypeStruct((l, 2 * RET_QK), f32)], sem=("parallel",), vmem_mb=32, comm=comm)


def _ret_prep_bwd(dq, dk, cos, sin, dproj, name):
    l = dq.shape[0]
    tr = _tile(l, 256)
    half = RET_DK // 2
    scale = RET_DK ** -0.5

    def body(dq_ref, dk_ref, c_ref, s_ref, _, o_ref):
        rows = pl.program_id(0) * tr + lax.broadcasted_iota(jnp.int32, (tr, 1), 0)
        kmul = jnp.where(rows >= PAD, scale, 0.0).astype(f32)
        c, s = c_ref[...], s_ref[...]
        for j in range(2 * RET_HEADS):
            d_ref = dq_ref if j < RET_HEADS else dk_ref
            jj = j % RET_HEADS
            d1 = d_ref[:, jj * RET_DK: jj * RET_DK + half]
            d2 = d_ref[:, jj * RET_DK + half: (jj + 1) * RET_DK]
            if j >= RET_HEADS:
                d1, d2 = d1 * kmul, d2 * kmul
            o_ref[:, j * RET_DK: j * RET_DK + half] = _b(d1 * c + d2 * s)
            o_ref[:, j * RET_DK + half: (j + 1) * RET_DK] = _b(d2 * c - d1 * s)

    nar = pl.BlockSpec((tr, RET_QK), lambda i: (i, 0))
    wide = pl.BlockSpec((tr, 2 * RET_QK), lambda i: (i, 0))
    tab = pl.BlockSpec((tr, half), lambda i: (i, 0))
    return pl.pallas_call(
        body, name=name, grid=(l // tr,), in_specs=[nar, nar, tab, tab, pl.BlockSpec(memory_space=pl.ANY)],
        out_specs=wide, out_shape=jax.ShapeDtypeStruct(dproj.shape, dproj.dtype), input_output_aliases={4: 0},
        compiler_params=_params(("parallel",), 32),
    )(dq, dk, cos, sin, dproj)


RET_BLOCK_CHUNKS = 3


def _ret_block(l):
    nch = l // CHUNK
    return RET_BLOCK_CHUNKS * CHUNK if nch % RET_BLOCK_CHUNKS == 0 else CHUNK


def _ret_decay(lg, rb):
    idx = lax.broadcasted_iota(jnp.int32, (rb, 1), 0).astype(f32)
    ri = lax.broadcasted_iota(jnp.int32, (rb, rb), 0)
    ci = lax.broadcasted_iota(jnp.int32, (rb, rb), 1)
    rel = (ri - ci).astype(f32)
    dmask = jnp.where(ri >= ci, jnp.exp(lg * jnp.maximum(rel, 0.0)), 0.0)
    xi = jnp.exp(lg * (idx + 1.0))
    zeta = jnp.exp(lg * (rb - 1.0 - idx))
    return dmask, xi, zeta


def _ret_scan_fwd(qk, proj, gn_w, lgs, gcs, name, comm=None):
    l = qk.shape[0]
    rb = _ret_block(l)
    nb = l // rb

    def body(lg_ref, gc_ref, q_ref, k_ref, v_ref, g_ref, nw_ref, o_ref, st_ref, y_ref, s_ref):
        @pl.when(pl.program_id(0) == 0)
        def _():
            s_ref[...] = jnp.zeros_like(s_ref)

        hs = range(RET_HEADS)
        dec = [_ret_decay(lg_ref[h], rb) for h in hs]
        q = [q_ref[:, h * RET_DK:(h + 1) * RET_DK] for h in hs]
        k = [k_ref[:, h * RET_DK:(h + 1) * RET_DK] for h in hs]
        vb = [_b(v_ref[:, h * RET_DV:(h + 1) * RET_DV]) for h in hs]
        s = [s_ref[h] for h in hs]
        sb = [_b(s[h]) for h in hs]
        scores = [_b(_nt(_b(q[h]), _b(k[h])) * dec[h][0]) for h in hs]
        inter = [_nn(_b(q[h] * dec[h][1]), sb[h]) for h in hs]
        kv = [_tn(_b(k[h] * dec[h][2]), vb[h]) for h in hs]
        nw = nw_ref[...]
        for h in hs:
            cols = slice(h * RET_DV, (h + 1) * RET_DV)
            st_ref[0, h] = sb[h]
            o = _nn(scores[h], vb[h]) + inter[h]
            o_ref[:, cols] = o
            r = lax.rsqrt(jnp.mean(o * o, axis=-1, keepdims=True) + RMS_EPS)
            y_ref[:, cols] = _b(o * r * nw * _silu(g_ref[:, cols]))
            s_ref[h] = gc_ref[h] * s[h] + kv[h]

    smem = pl.BlockSpec(memory_space=pltpu.SMEM)
    wide = pl.BlockSpec((rb, RET_V), lambda n: (n, 0))
    return _call(
        body, (lgs, gcs, qk, qk, proj, proj, gn_w), name=name, grid=(nb,),
        in_specs=[smem, smem,
                  pl.BlockSpec((rb, RET_QK), lambda n: (n, 0)),
                  pl.BlockSpec((rb, RET_QK), lambda n: (n, 1)),
                  pl.BlockSpec((rb, RET_V), lambda n: (n, 1)),
                  pl.BlockSpec((rb, RET_V), lambda n: (n, 2)),
                  pl.BlockSpec((1, RET_DV), lambda n: (0, 0))],
        out_specs=[wide, pl.BlockSpec((1, RET_HEADS, RET_DK, RET_DV), lambda n: (n, 0, 0, 0)), wide],
        out_shape=[jax.ShapeDtypeStruct((l, RET_V), f32),
                   jax.ShapeDtypeStruct((nb, RET_HEADS, RET_DK, RET_DV), bf16),
                   jax.ShapeDtypeStruct((l, RET_V), bf16)],
        scratch=[pltpu.VMEM((RET_HEADS, RET_DK, RET_DV), f32)], sem=("arbitrary",), vmem_mb=40, comm=comm)


def _ret_scan_bwd(qk, proj, states, do, dproj, lgs, gcs, name, comm=None):
    l = qk.shape[0]
    rb = _ret_block(l)
    nb = l // rb

    def body(lg_ref, gc_ref, q_ref, k_ref, v_ref, st_ref, do_ref, _, dq_ref, dk_ref, dv_ref, ds_ref):
        @pl.when(pl.program_id(0) == 0)
        def _():
            ds_ref[...] = jnp.zeros_like(ds_ref)

        hs = range(RET_HEADS)
        dec = [_ret_decay(lg_ref[h], rb) for h in hs]
        q = [q_ref[:, h * RET_DK:(h + 1) * RET_DK] for h in hs]
        k = [k_ref[:, h * RET_DK:(h + 1) * RET_DK] for h in hs]
        qb, kb = [_b(t) for t in q], [_b(t) for t in k]
        vb = [_b(v_ref[:, h * RET_DV:(h + 1) * RET_DV]) for h in hs]
        dob = [_b(do_ref[:, h * RET_DV:(h + 1) * RET_DV]) for h in hs]
        dsp = [ds_ref[h] for h in hs]
        dspb = [_b(t) for t in dsp]
        scores = [_b(_nt(qb[h], kb[h]) * dec[h][0]) for h in hs]
        dscores = [_b(_nt(dob[h], vb[h]) * dec[h][0]) for h in hs]
        for h in hs:
            dq_ref[:, h * RET_DK:(h + 1) * RET_DK] = _nn(dscores[h], kb[h]) + _nt(dob[h], st_ref[0, h]) * dec[h][1]
        for h in hs:
            dk_ref[:, h * RET_DK:(h + 1) * RET_DK] = _tn(dscores[h], qb[h]) + _nt(vb[h], dspb[h]) * dec[h][2]
        for h in hs:
            dv_ref[:, h * RET_DV:(h + 1) * RET_DV] = _b(_tn(scores[h], dob[h]) + _nn(_b(k[h] * dec[h][2]), dspb[h]))
        for h in hs:
            ds_ref[h] = gc_ref[h] * dsp[h] + _tn(_b(q[h] * dec[h][1]), dob[h])

    smem = pl.BlockSpec(memory_space=pltpu.SMEM)
    rev = lambda n: nb - 1 - n
    return _call(
        body, (lgs, gcs, qk, qk, proj, states, do, dproj), name=name, grid=(nb,),
        in_specs=[smem, smem,
                  pl.BlockSpec((rb, RET_QK), lambda n: (rev(n), 0)),
                  pl.BlockSpec((rb, RET_QK), lambda n: (rev(n), 1)),
                  pl.BlockSpec((rb, RET_V), lambda n: (rev(n), 1)),
                  pl.BlockSpec((1, RET_HEADS, RET_DK, RET_DV), lambda n: (rev(n), 0, 0, 0)),
                  pl.BlockSpec((rb, RET_V), lambda n: (rev(n), 0)),
                  pl.BlockSpec(memory_space=pl.ANY)],
        out_specs=[pl.BlockSpec((rb, RET_QK), lambda n: (rev(n), 0)),
                   pl.BlockSpec((rb, RET_QK), lambda n: (rev(n), 0)),
                   pl.BlockSpec((rb, RET_V), lambda n: (rev(n), 1))],
        out_shape=[jax.ShapeDtypeStruct((l, RET_QK), f32), jax.ShapeDtypeStruct((l, RET_QK), f32),
                   jax.ShapeDtypeStruct(dproj.shape, dproj.dtype)],
        scratch=[pltpu.VMEM((RET_HEADS, RET_DK, RET_DV), f32)], sem=("arbitrary",), vmem_mb=40, comm=comm,
        aliases={7: 2})


CONV_BLK = 512
CONV_Q_BLKS = DN_QK // CONV_BLK
HALO = 8


def _conv_tile(l):
    return _tile(l, 3072)


def _slab_rows(r):
    return pl.ds(pl.multiple_of(r * HALO, HALO), HALO)


def _conv_slab(x_ref, p_ref, r, i, tr):
    cur = x_ref[_slab_rows(r), :]
    prev = jnp.where(r > 0, x_ref[_slab_rows(jnp.maximum(r - 1, 0)), :], p_ref[...])
    row0 = i * tr + r * HALO
    cur = jnp.where(row0 >= PAD, cur, 0.0)
    prev = jnp.where(row0 - HALO >= PAD, prev, 0.0)
    lrow = lax.broadcasted_iota(jnp.int32, (HALO, 1), 0)
    shifted = [jnp.where(lrow < s, pltpu.roll(prev, s, 0), pltpu.roll(cur, s, 0)) for s in range(1, CONV_K)]
    return [cur] + shifted


def _conv_of(xs, w):
    acc = xs[0] * w[CONV_K - 1:CONV_K, :]
    for s in range(1, CONV_K):
        acc = acc + xs[s] * w[CONV_K - 1 - s:CONV_K - s, :]
    return acc


def _slab_loop(n_slabs, fn, init=None):
    return lax.fori_loop(0, n_slabs, fn, init, unroll=8)


def _dn_conv_fwd(proj, conv_w, name, comm=None):
    l = proj.shape[0]
    tr = _conv_tile(l)
    nblk = DN_CONV_CH // CONV_BLK
    heads = CONV_BLK // DN_DK

    def body(x_ref, p_ref, w_ref, o_ref):
        i, j = pl.program_id(0), pl.program_id(1)
        w = w_ref[...]

        def act(r):
            return _silu(_conv_of(_conv_slab(x_ref, p_ref, r, i, tr), w))

        def normed(scale):
            def slab(r, carry):
                a = act(r)
                outs = []
                for h in range(heads):
                    ah = a[:, h * DN_DK:(h + 1) * DN_DK]
                    outs.append(ah * (lax.rsqrt(jnp.sum(ah * ah, axis=-1, keepdims=True) + RMS_EPS) * scale))
                o_ref[_slab_rows(r), :] = jnp.concatenate(outs, axis=1)
                return carry
            return slab

        def plain(r, carry):
            o_ref[_slab_rows(r), :] = act(r)
            return carry

        @pl.when(j < CONV_Q_BLKS)
        def _():
            _slab_loop(tr // HALO, normed(DN_DK ** -0.5))

        @pl.when(jnp.logical_and(j >= CONV_Q_BLKS, j < 2 * CONV_Q_BLKS))
        def _():
            _slab_loop(tr // HALO, normed(1.0))

        @pl.when(j >= 2 * CONV_Q_BLKS)
        def _():
            _slab_loop(tr // HALO, plain)

    hb = tr // HALO
    return _call(
        body, (proj, proj, conv_w), name=name, grid=(l // tr, nblk),
        in_specs=[pl.BlockSpec((tr, CONV_BLK), lambda i, j: (i, j)),
                  pl.BlockSpec((HALO, CONV_BLK), lambda i, j: (jnp.maximum(i * hb - 1, 0), j)),
                  pl.BlockSpec((CONV_K, CONV_BLK), lambda i, j: (0, j))],
        out_specs=[pl.BlockSpec((tr, CONV_BLK), lambda i, j: (i, j))],
        out_shape=[jax.ShapeDtypeStruct((l, DN_CONV_CH), f32)],
        scratch=[], sem=("parallel", "parallel"), vmem_mb=32, comm=comm)


def _dn_conv_bwd_a(proj, conv_w, dqkv, name, comm=None):
    l = proj.shape[0]
    tr = _conv_tile(l)
    nblk = DN_CONV_CH // CONV_BLK
    heads = CONV_BLK // DN_DK

    def body(x_ref, p_ref, w_ref, d_ref, dc_ref, dw_ref, acc_ref):
        j, i = pl.program_id(0), pl.program_id(1)
        w = w_ref[...]
        acc_ref[...] = jnp.zeros_like(acc_ref)

        def run(l2_scale):
            _slab_loop(tr // HALO, slab_of(l2_scale))

        def slab_of(l2_scale):
            def slab(r, carry):
                xs = _conv_slab(x_ref, p_ref, r, i, tr)
                c = _conv_of(xs, w)
                a = _silu(c)
                dy = d_ref[_slab_rows(r), :]
                if l2_scale is None:
                    da = dy
                else:
                    parts = []
                    for h in range(heads):
                        sl = slice(h * DN_DK, (h + 1) * DN_DK)
                        ah, dyh = a[:, sl], dy[:, sl]
                        rn = lax.rsqrt(jnp.sum(ah * ah, axis=-1, keepdims=True) + RMS_EPS)
                        yh = ah * rn
                        parts.append((rn * l2_scale) * (dyh - yh * jnp.sum(dyh * yh, axis=-1, keepdims=True)))
                    da = jnp.concatenate(parts, axis=1)
                dc = da * _dsilu(c)
                dc_ref[_slab_rows(r), :] = dc
                for k in range(CONV_K):
                    acc_ref[k] += dc * xs[CONV_K - 1 - k]
                return carry
            return slab

        @pl.when(j < CONV_Q_BLKS)
        def _():
            run(DN_DK ** -0.5)

        @pl.when(jnp.logical_and(j >= CONV_Q_BLKS, j < 2 * CONV_Q_BLKS))
        def _():
            run(1.0)

        @pl.when(j >= 2 * CONV_Q_BLKS)
        def _():
            run(None)

        ksel = lax.broadcasted_iota(jnp.int32, (CONV_K, 1), 0)
        dw = jnp.zeros((CONV_K, CONV_BLK), f32)
        for k in range(CONV_K):
            dw = dw + jnp.where(ksel == k, jnp.sum(acc_ref[k], axis=0, keepdims=True), 0.0)

        @pl.when(i == 0)
        def _():
            dw_ref[...] = dw

        @pl.when(i > 0)
        def _():
            dw_ref[...] += dw

    hb = tr // HALO
    blk = pl.BlockSpec((tr, CONV_BLK), lambda j, i: (i, j))
    return _call(
        body, (proj, proj, conv_w, dqkv), name=name, grid=(nblk, l // tr),
        in_specs=[blk, pl.BlockSpec((HALO, CONV_BLK), lambda j, i: (jnp.maximum(i * hb - 1, 0), j)),
                  pl.BlockSpec((CONV_K, CONV_BLK), lambda j, i: (0, j)), blk],
        out_specs=[blk, pl.BlockSpec((CONV_K, CONV_BLK), lambda j, i: (0, j))],
        out_shape=[jax.ShapeDtypeStruct((l, DN_CONV_CH), f32), jax.ShapeDtypeStruct((CONV_K, DN_CONV_CH), f32)],
        scratch=[pltpu.VMEM((CONV_K, HALO, CONV_BLK), f32)], sem=("parallel", "arbitrary"), vmem_mb=48, comm=comm)


def _dn_conv_bwd_b(dc, conv_w, dproj, name):
    l = dc.shape[0]
    tr = _conv_tile(l)
    nblk = DN_CONV_CH // CONV_BLK
    nrow = l // tr

    n_slabs = tr // HALO
    pair = 2 * HALO

    def body(d_ref, n_ref, w_ref, _, o_ref):
        i = pl.program_id(0)
        w = w_ref[...]
        nxt_tile = jnp.where(i < nrow - 1, n_ref[...], 0.0)
        lrow = lax.broadcasted_iota(jnp.int32, (HALO, 1), 0)

        def one(r):
            cur = d_ref[_slab_rows(r), :]
            nxt = jnp.where(r < n_slabs - 1, d_ref[_slab_rows(jnp.minimum(r + 1, n_slabs - 1)), :], nxt_tile)
            acc = cur * w[CONV_K - 1:CONV_K, :]
            for s in range(1, CONV_K):
                up = jnp.where(lrow >= HALO - s, pltpu.roll(nxt, HALO - s, 0), pltpu.roll(cur, HALO - s, 0))
                acc = acc + up * w[CONV_K - 1 - s:CONV_K - s, :]
            return jnp.where(i * tr + r * HALO >= PAD, acc, 0.0)

        def two(q, carry):
            rows = pl.ds(pl.multiple_of(q * pair, pair), pair)
            o_ref[rows, :] = _b(jnp.concatenate([one(2 * q), one(2 * q + 1)], axis=0))
            return carry

        lax.fori_loop(0, n_slabs // 2, two, None, unroll=4)

    hb = tr // HALO
    nh = l // HALO
    return pl.pallas_call(
        body, name=name, grid=(nrow, nblk),
        in_specs=[pl.BlockSpec((tr, CONV_BLK), lambda i, j: (i, j)),
                  pl.BlockSpec((HALO, CONV_BLK), lambda i, j: (jnp.minimum((i + 1) * hb, nh - 1), j)),
                  pl.BlockSpec((CONV_K, CONV_BLK), lambda i, j: (0, j)),
                  pl.BlockSpec(memory_space=pl.ANY)],
        out_specs=pl.BlockSpec((tr, CONV_BLK), lambda i, j: (i, j)),
        out_shape=jax.ShapeDtypeStruct(dproj.shape, dproj.dtype), input_output_aliases={3: 0},
        compiler_params=_params(("parallel", "parallel"), 32),
    )(dc, dc, conv_w, dproj)


BA_W = LANES


def _dn_gates(ba_ref, al_ref, dt_ref, n):
    rows = n * CHUNK + lax.broadcasted_iota(jnp.int32, (CHUNK, 1), 0)
    vm = (rows >= PAD).astype(f32)
    bin_ = ba_ref[:, 0:DN_HEADS]
    z = ba_ref[:, DN_HEADS:2 * DN_HEADS] + dt_ref[...]
    sp = jnp.maximum(z, 0.0) + jnp.log1p(jnp.exp(-jnp.abs(z)))
    ea = jnp.exp(al_ref[...])
    beta = _sigmoid(bin_) * vm
    g = -ea * sp * vm
    return vm, bin_, z, ea, beta, g


def _tri():
    ri = lax.broadcasted_iota(jnp.int32, (CHUNK, CHUNK), 0)
    ci = lax.broadcasted_iota(jnp.int32, (CHUNK, CHUNK), 1)
    return ri, ci


def _split(a):
    hi = _b(a)
    return hi, _b(a - hi.astype(f32))


def _mm3(a, b, dot=_nn):
    (ah, al), (bh, bl) = _split(a), _split(b)
    return dot(ah, bh) + (dot(ah, bl) + dot(al, bh))


def _cumsum_rows(tri, g):
    tb = _b(tri)
    g1 = _b(g)
    r1 = g - g1.astype(f32)
    g2 = _b(r1)
    g3 = _b(r1 - g2.astype(f32))
    return _nn(tb, g1) + (_nn(tb, g2) + _nn(tb, g3))


DN_SCAN_CHUNKS = 3


def _scan_chunks(nch):
    return DN_SCAN_CHUNKS if nch % DN_SCAN_CHUNKS == 0 else 1


def _dn_prep(qkv, ba, a_log, dt_bias, name, comm=None):
    l = qkv.shape[0]
    nch = l // CHUNK
    heads = range(DN_HEADS)

    cb = _scan_chunks(nch)
    items = [(c, h) for c in range(cb) for h in heads]

    def body(q_ref, k_ref, v_ref, ba_ref, al_ref, dt_ref, t_ref, u_ref, wq_ref, pk_ref, eg_ref, kpt_ref, qwt_ref):
        n0 = pl.program_id(0) * cb
        ri, ci = _tri()
        incl, strict = ri >= ci, ri > ci
        eye = (ri == ci).astype(f32)
        rows = [slice(c * CHUNK, (c + 1) * CHUNK) for c in range(cb)]
        gam, gam_t, beta = [], [], []
        for c in range(cb):
            _, _, _, _, beta_c, g_c = _dn_gates(ba_ref[rows[c], :], al_ref, dt_ref, n0 + c)
            gam.append(_cumsum_rows(incl.astype(f32), g_c))
            gam_t.append(gam[c].T)
            beta.append(beta_c)
        gc = {(c, h): gam[c][:, h:h + 1] for c, h in items}
        bh = {(c, h): beta[c][:, h:h + 1] for c, h in items}
        kh = {(c, h): k_ref[rows[c], h * DN_DK:(h + 1) * DN_DK] for c, h in items}
        kb = {i: _b(kh[i]) for i in items}
        decay = {(c, h): jnp.exp(jnp.where(incl, gc[c, h] - gam_t[c][h:h + 1, :], -jnp.inf)) for c, h in items}
        a = {i: jnp.where(strict, bh[i] * _nt(kb[i], kb[i]) * decay[i], 0.0) for i in items}
        t = {i: eye - a[i] for i in items}
        p = a
        for level in range(int(math.log2(CHUNK)) - 1):
            mm = _mm3 if level < 2 else (lambda x, y: _nn(_b(x), _b(y)))
            p = {i: mm(p[i], p[i]) for i in items}
            t = {i: t[i] + mm(t[i], p[i]) for i in items}
        eg = {i: jnp.exp(gc[i]) for i in items}
        for c, h in items:
            i = (c, h)
            t_ref[c, h] = t[i]
            u_ref[rows[c], h * DN_DV:(h + 1) * DN_DV] = _mm3(t[i], v_ref[rows[c], h * DN_DV:(h + 1) * DN_DV] * bh[i])
            w = _mm3(t[i], kh[i] * (bh[i] * eg[i]))
            wq_ref[c, h, 0:CHUNK, :] = _b(w)
            qwt_ref[c, h, DN_DK:2 * DN_DK, :] = _b(w.T)
        for c, h in items:
            i = (c, h)
            qh = q_ref[rows[c], h * DN_DK:(h + 1) * DN_DK]
            gl = gc[i][CHUNK - 1:CHUNK, :]
            qe = qh * eg[i]
            ke = kh[i] * jnp.exp(gl - gc[i])
            pmat = _nt(_b(qh), kb[i]) * decay[i]
            wq_ref[c, h, CHUNK:2 * CHUNK, :] = _b(qe)
            qwt_ref[c, h, 0:DN_DK, :] = _b(qe.T)
            pk_ref[c, h, 0:CHUNK, :] = _b(pmat)
            pk_ref[c, h, CHUNK:CHUNK + DN_DK, :] = _b(ke.T)
            kpt_ref[c, h, :, 0:DN_DK] = _b(ke)
            kpt_ref[c, h, :, DN_DK:DN_DK + CHUNK] = _b(pmat.T)
            eg_ref[c, h] = jnp.broadcast_to(jnp.exp(gl), (8, LANES))

    vec = pl.BlockSpec((1, DN_HEADS), lambda n: (0, 0))
    return _call(
        body, (qkv, qkv, qkv, ba, a_log, dt_bias), name=name, grid=(nch // cb,),
        in_specs=[pl.BlockSpec((cb * CHUNK, DN_QK), lambda n: (n, 0)),
                  pl.BlockSpec((cb * CHUNK, DN_QK), lambda n: (n, 1)),
                  pl.BlockSpec((cb * CHUNK, DN_V), lambda n: (n, 1)),
                  pl.BlockSpec((cb * CHUNK, BA_W), lambda n: (n, 0)), vec, vec],
        out_specs=[pl.BlockSpec((cb, DN_HEADS, CHUNK, CHUNK), lambda n: (n, 0, 0, 0)),
                   pl.BlockSpec((cb * CHUNK, DN_V), lambda n: (n, 0)),
                   pl.BlockSpec((cb, DN_HEADS, 2 * CHUNK, DN_DK), lambda n: (n, 0, 0, 0)),
                   pl.BlockSpec((cb, DN_HEADS, CHUNK + DN_DK, CHUNK), lambda n: (n, 0, 0, 0)),
                   pl.BlockSpec((cb, DN_HEADS, 8, LANES), lambda n: (n, 0, 0, 0)),
                   pl.BlockSpec((cb, DN_HEADS, CHUNK, DN_DK + CHUNK), lambda n: (n, 0, 0, 0)),
                   pl.BlockSpec((cb, DN_HEADS, 2 * DN_DK, CHUNK), lambda n: (n, 0, 0, 0))],
        out_shape=[jax.ShapeDtypeStruct((nch, DN_HEADS, CHUNK, CHUNK), f32),
                   jax.ShapeDtypeStruct((l, DN_V), f32),
                   jax.ShapeDtypeStruct((nch, DN_HEADS, 2 * CHUNK, DN_DK), bf16),
                   jax.ShapeDtypeStruct((nch, DN_HEADS, CHUNK + DN_DK, CHUNK), bf16),
                   jax.ShapeDtypeStruct((nch, DN_HEADS, 8, LANES), f32),
                   jax.ShapeDtypeStruct((nch, DN_HEADS, CHUNK, DN_DK + CHUNK), bf16),
                   jax.ShapeDtypeStruct((nch, DN_HEADS, 2 * DN_DK, CHUNK), bf16)],
        sem=("parallel",), vmem_mb=40, comm=comm)


def _dn_scan_fwd(u, wq, pk, egl, name):
    l = u.shape[0]
    nch = l // CHUNK
    cs = _scan_chunks(nch)

    def body(u_ref, wq_ref, pk_ref, eg_ref, o_ref, st_ref, vn_ref, s_ref):
        @pl.when(pl.program_id(0) == 0)
        def _():
            s_ref[...] = jnp.zeros_like(s_ref)

        hs = range(DN_HEADS)
        cols = [slice(h * DN_DV, (h + 1) * DN_DV) for h in hs]
        s = [s_ref[h] for h in hs]
        for c in range(cs):
            rows = slice(c * CHUNK, (c + 1) * CHUNK)
            sb = [_b(s[h]) for h in hs]
            x = [_nn(wq_ref[c, h], sb[h]) for h in hs]
            vnb = [_b(u_ref[rows, cols[h]] - x[h][0:CHUNK]) for h in hs]
            y = [_nn(pk_ref[c, h], vnb[h]) for h in hs]
            for h in hs:
                st_ref[c, h] = sb[h]
                vn_ref[rows, cols[h]] = vnb[h]
                o_ref[rows, cols[h]] = x[h][CHUNK:2 * CHUNK] + y[h][0:CHUNK]
            s = [eg_ref[c, h][0:1, 0:1] * s[h] + y[h][CHUNK:CHUNK + DN_DK] for h in hs]
        for h in hs:
            s_ref[h] = s[h]

    return pl.pallas_call(
        body, name=name, grid=(nch // cs,),
        in_specs=[pl.BlockSpec((cs * CHUNK, DN_V), lambda n: (n, 0)),
                  pl.BlockSpec((cs, DN_HEADS, 2 * CHUNK, DN_DK), lambda n: (n, 0, 0, 0)),
                  pl.BlockSpec((cs, DN_HEADS, CHUNK + DN_DK, CHUNK), lambda n: (n, 0, 0, 0)),
                  pl.BlockSpec((cs, DN_HEADS, 8, LANES), lambda n: (n, 0, 0, 0))],
        out_specs=[pl.BlockSpec((cs * CHUNK, DN_V), lambda n: (n, 0)),
                   pl.BlockSpec((cs, DN_HEADS, DN_DK, DN_DV), lambda n: (n, 0, 0, 0)),
                   pl.BlockSpec((cs * CHUNK, DN_V), lambda n: (n, 0))],
        out_shape=[jax.ShapeDtypeStruct((l, DN_V), f32),
                   jax.ShapeDtypeStruct((nch, DN_HEADS, DN_DK, DN_DV), bf16),
                   jax.ShapeDtypeStruct((l, DN_V), bf16)],
        scratch_shapes=[pltpu.VMEM((DN_HEADS, DN_DK, DN_DV), f32)],
        compiler_params=_params(("arbitrary",), 40),
    )(u, wq, pk, egl)


def _dn_scan_bwd(do, kpt, qwt, egl, name):
    l = do.shape[0]
    nch = l // CHUNK
    cs = _scan_chunks(nch)
    nblk = nch // cs

    def body(do_ref, kpt_ref, qwt_ref, eg_ref, dvn_ref, dsp_ref, ds_ref):
        @pl.when(pl.program_id(0) == 0)
        def _():
            ds_ref[...] = jnp.zeros_like(ds_ref)

        hs = range(DN_HEADS)
        cols = [slice(h * DN_DV, (h + 1) * DN_DV) for h in hs]
        ds = [ds_ref[h] for h in hs]
        for c in reversed(range(cs)):
            rows = slice(c * CHUNK, (c + 1) * CHUNK)
            dspb = [_b(ds[h]) for h in hs]
            dob = [_b(do_ref[rows, cols[h]]) for h in hs]
            dvn = [_nn(kpt_ref[c, h][:, 0:DN_DK], dspb[h]) + _nn(kpt_ref[c, h][:, DN_DK:DN_DK + CHUNK], dob[h])
                   for h in hs]
            for h in hs:
                dsp_ref[c, h] = dspb[h]
                dvn_ref[rows, cols[h]] = dvn[h]
            ds = [eg_ref[c, h][0:1, 0:1] * ds[h] + _nn(qwt_ref[c, h][0:DN_DK], dob[h])
                  - _nn(qwt_ref[c, h][DN_DK:2 * DN_DK], _b(dvn[h])) for h in hs]
        for h in hs:
            ds_ref[h] = ds[h]

    rev = lambda s: nblk - 1 - s
    return pl.pallas_call(
        body, name=name, grid=(nblk,),
        in_specs=[pl.BlockSpec((cs * CHUNK, DN_V), lambda s: (rev(s), 0)),
                  pl.BlockSpec((cs, DN_HEADS, CHUNK, DN_DK + CHUNK), lambda s: (rev(s), 0, 0, 0)),
                  pl.BlockSpec((cs, DN_HEADS, 2 * DN_DK, CHUNK), lambda s: (rev(s), 0, 0, 0)),
                  pl.BlockSpec((cs, DN_HEADS, 8, LANES), lambda s: (rev(s), 0, 0, 0))],
        out_specs=[pl.BlockSpec((cs * CHUNK, DN_V), lambda s: (rev(s), 0)),
                   pl.BlockSpec((cs, DN_HEADS, DN_DK, DN_DV), lambda s: (rev(s), 0, 0, 0))],
        out_shape=[jax.ShapeDtypeStruct((l, DN_V), f32),
                   jax.ShapeDtypeStruct((nch, DN_HEADS, DN_DK, DN_DV), bf16)],
        scratch_shapes=[pltpu.VMEM((DN_HEADS, DN_DK, DN_DV), f32)],
        compiler_params=_params(("arbitrary",), 40),
    )(do, kpt, qwt, egl)


def _dn_post_bwd(qkv, ba, a_log, dt_bias, states, dsp_all, tinv_all, u_all, wq, vn_all, do, dvn_all, name):
    l = qkv.shape[0]
    nch = l // CHUNK
    cb = _scan_chunks(nch)
    items = [(c, h) for c in range(cb) for h in range(DN_HEADS)]

    def body(q_ref, k_ref, v_ref, ba_ref, al_ref, dt_ref, st_ref, dsp_ref, t_ref, u_ref, wq_ref, vn_ref, do_ref,
             dvn_ref, dqkv_ref, dba_ref, dal_ref, ddt_ref):
        step = pl.program_id(0)
        ri, ci = _tri()
        incl, strict = ri >= ci, ri > ci
        lane8 = lax.broadcasted_iota(jnp.int32, (1, DN_HEADS), 1)
        sub8 = lax.broadcasted_iota(jnp.int32, (DN_HEADS, 1), 0)
        last = (lax.broadcasted_iota(jnp.int32, (CHUNK, 1), 0) == CHUNK - 1).astype(f32)
        rsum = lambda t: jnp.sum(t, axis=-1, keepdims=True)
        rows = [slice(c * CHUNK, (c + 1) * CHUNK) for c in range(cb)]
        gates = [_dn_gates(ba_ref[rows[c], :], al_ref, dt_ref, step * cb + c) for c in range(cb)]
        gam = [_cumsum_rows(incl.astype(f32), gates[c][5]) for c in range(cb)]
        gam_t = [gam[c].T for c in range(cb)]
        each = lambda fn: {(c, h): fn(c, h) for c, h in items}
        dk_cols = lambda h: slice(h * DN_DK, (h + 1) * DN_DK)
        dv_cols = lambda h: slice(h * DN_DV, (h + 1) * DN_DV)
        gc = each(lambda c, h: gam[c][:, h:h + 1])
        bh = each(lambda c, h: gates[c][4][:, h:h + 1])
        qh = each(lambda c, h: q_ref[rows[c], dk_cols(h)])
        kh = each(lambda c, h: k_ref[rows[c], dk_cols(h)])
        doh = each(lambda c, h: _b(do_ref[rows[c], dv_cols(h)]))
        sb = each(lambda c, h: st_ref[c, h])
        dspb = each(lambda c, h: dsp_ref[c, h])
        vnb = each(lambda c, h: vn_ref[rows[c], dv_cols(h)])
        dvn = each(lambda c, h: dvn_ref[rows[c], dv_cols(h)])
        wb = each(lambda c, h: wq_ref[c, h, 0:CHUNK, :])
        decay = each(lambda c, h: jnp.exp(jnp.where(incl, gc[c, h] - gam_t[c][h:h + 1, :], -jnp.inf)))
        qb, kb = each(lambda c, h: _b(qh[c, h])), each(lambda c, h: _b(kh[c, h]))
        eg = each(lambda c, h: jnp.exp(gc[c, h]))
        gl = each(lambda c, h: gc[c, h][CHUNK - 1:CHUNK, :])
        ekd = each(lambda c, h: jnp.exp(gl[c, h] - gc[c, h]))
        dvnb = each(lambda c, h: _b(dvn[c, h]))
        kk = each(lambda c, h: _nt(kb[c, h], kb[c, h]))
        p = each(lambda c, h: _nt(qb[c, h], kb[c, h]) * decay[c, h])
        dpraw = each(lambda c, h: _nt(doh[c, h], vnb[c, h]))
        dqe = each(lambda c, h: _nt(doh[c, h], sb[c, h]))
        dke = each(lambda c, h: _nt(vnb[c, h], dspb[c, h]))
        dw = each(lambda c, h: -_nt(dvnb[c, h], sb[c, h]))
        dru = each(lambda c, h: _mm3(t_ref[c, h], dvn[c, h], _tn))
        drw = each(lambda c, h: _mm3(t_ref[c, h], dw[c, h], _tn))
        dqk = each(lambda c, h: _b(dpraw[c, h] * decay[c, h]))
        for c, h in items:
            i = (c, h)
            dqkv_ref[rows[c], dk_cols(h)] = _nn(dqk[i], kb[i]) + dqe[i] * eg[i]
            dqkv_ref[rows[c], 2 * DN_QK + h * DN_DV:2 * DN_QK + (h + 1) * DN_DV] = bh[i] * dru[i]
        da = each(lambda c, h: jnp.where(strict, -(_nt(_b(dru[c, h]), _b(u_ref[rows[c], dv_cols(h)]))
                                                   + _nt(_b(drw[c, h]), wb[c, h])), 0.0))
        dkk = each(lambda c, h: _b(da[c, h] * bh[c, h] * decay[c, h]))
        for c, h in items:
            i = (c, h)
            dqkv_ref[rows[c], DN_QK + h * DN_DK:DN_QK + (h + 1) * DN_DK] = (
                _tn(dqk[i], qb[i]) + dke[i] * ekd[i] + (bh[i] * eg[i]) * drw[i]
                + _nn(dkk[i], kb[i]) + _tn(dkk[i], kb[i]))
        dal = jnp.zeros((1, DN_HEADS), f32)
        ddt = jnp.zeros((1, DN_HEADS), f32)
        dba_ref[...] = jnp.zeros_like(dba_ref)
        for c in range(cb):
            vm, bin_, z, ea, _, g = gates[c]
            dbeta = jnp.zeros((CHUNK, DN_HEADS), f32)
            dgam = jnp.zeros((CHUNK, DN_HEADS), f32)
            dgam_neg_t = jnp.zeros((DN_HEADS, CHUNK), f32)
            for h in range(DN_HEADS):
                i = (c, h)
                keg = kh[i] * eg[i]
                ke = kh[i] * ekd[i]
                rw = rsum(drw[i] * keg)
                rke = rsum(dke[i] * ke)
                db_h = rsum(dru[i] * v_ref[rows[c], dv_cols(h)]) + rw + rsum(da[i] * kk[i] * decay[i])
                mm = da[i] * (bh[i] * kk[i] * decay[i]) + dpraw[i] * p[i]
                dgl = (jnp.sum(rke, axis=0, keepdims=True)
                       + jnp.exp(gl[i]) * jnp.sum(rsum(dspb[i].astype(f32) * sb[i].astype(f32)), axis=0,
                                                  keepdims=True))
                dg_h = rsum(mm) + rw * bh[i] + rsum(dqe[i] * (qh[i] * eg[i])) - rke + last * dgl
                dbeta = dbeta + jnp.where(lane8 == h, db_h, 0.0)
                dgam = dgam + jnp.where(lane8 == h, dg_h, 0.0)
                dgam_neg_t = dgam_neg_t + jnp.where(sub8 == h, jnp.sum(mm, axis=0, keepdims=True), 0.0)
            dgam = dgam - dgam_neg_t.T
            dg = _cumsum_rows((ri <= ci).astype(f32), dgam)
            sg = _sigmoid(bin_)
            dain = dg * (-ea) * vm * _sigmoid(z)
            dba_ref[rows[c], 0:DN_HEADS] = dbeta * vm * sg * (1.0 - sg)
            dba_ref[rows[c], DN_HEADS:2 * DN_HEADS] = dain
            dal = dal + jnp.sum(dg * g, axis=0, keepdims=True)
            ddt = ddt + jnp.sum(dain, axis=0, keepdims=True)

        @pl.when(step == 0)
        def _():
            dal_ref[...] = dal
            ddt_ref[...] = ddt

        @pl.when(step > 0)
        def _():
            dal_ref[...] += dal
            ddt_ref[...] += ddt

    vec = pl.BlockSpec((1, DN_HEADS), lambda s: (0, 0))
    qs = pl.BlockSpec((cb * CHUNK, DN_QK), lambda s: (s, 0))
    ks = pl.BlockSpec((cb * CHUNK, DN_QK), lambda s: (s, 1))
    vs = pl.BlockSpec((cb * CHUNK, DN_V), lambda s: (s, 1))
    v0 = pl.BlockSpec((cb * CHUNK, DN_V), lambda s: (s, 0))
    st = pl.BlockSpec((cb, DN_HEADS, DN_DK, DN_DV), lambda s: (s, 0, 0, 0))
    return pl.pallas_call(
        body, name=name, grid=(nch // cb,),
        in_specs=[qs, ks, vs, pl.BlockSpec((cb * CHUNK, BA_W), lambda s: (s, 0)), vec, vec, st, st,
                  pl.BlockSpec((cb, DN_HEADS, CHUNK, CHUNK), lambda s: (s, 0, 0, 0)),
                  v0, pl.BlockSpec((cb, DN_HEADS, 2 * CHUNK, DN_DK), lambda s: (s, 0, 0, 0)), v0, v0, v0],
        out_specs=[pl.BlockSpec((cb * CHUNK, DN_CONV_CH), lambda s: (s, 0)),
                   pl.BlockSpec((cb * CHUNK, BA_W), lambda s: (s, 0)), vec, vec],
        out_shape=[jax.ShapeDtypeStruct((l, DN_CONV_CH), f32), jax.ShapeDtypeStruct((l, BA_W), f32),
                   jax.ShapeDtypeStruct((1, DN_HEADS), f32), jax.ShapeDtypeStruct((1, DN_HEADS), f32)],
        compiler_params=_params(("arbitrary",), 48),
    )(qkv, qkv, qkv, ba, a_log, dt_bias, states, dsp_all, tinv_all, u_all, wq, vn_all, do, dvn_all)


def _ffn_fwd(h, hn, wg, wu, wd, tb, th, tag, plan, next_norm_w=None):
    fh, d = wd.shape
    a, b, s = plan.call(f"{tag}_gu", functools.partial(_ffn_gu, tm=th // 2, tn=fh // 2), hn, wg, wu, n_out=3)
    out = plan.matmul(f"{tag}_down", s, wd, mode="nn", tm=th // 2, tn=d, tk=fh, res=h, norm_w=next_norm_w)
    return out, (hn, a, b, s)


def _ffn_bwd(dh, dhb, h, nw, wg, wu, wd, saved, tb, th, tag, plan):
    hn, a, b, s = saved
    d = h.shape[1]
    fh = wd.shape[0]
    layer = tag[-1]
    gr = plan.grads
    da, db = _ffn_ds(dhb, wd, a, b, tm=th // 2, tn=fh // 2, name=f"{tag}_b_ds")
    gr["down" + layer] = _matmul(s, dhb, mode="tn", tm=fh // 2, tn=d, tk=th, out_dtype=bf16, name=f"{tag}_b_dwd")
    dh2, dh2b, dnw = plan.call(f"{tag}_b_dhn", functools.partial(_dhn_norm_bwd, tm=th // 2, tk=fh // 2),
                               [(da, wg), (db, wu)], h, nw, dh, n_out=3)
    gr["gate" + layer] = _matmul(hn, da, mode="tn", tm=d, tn=fh // 2, tk=th, out_dtype=bf16, name=f"{tag}_b_dwg")
    gr["up" + layer] = _matmul(hn, db, mode="tn", tm=d, tn=fh // 2, tk=th, out_dtype=bf16, name=f"{tag}_b_dwu")
    return dh2, dh2b, dnw


class _Plan:
    GATHERS = {"ret_proj": ("ret_out", "gate0", "up0"), "ret_scan": ("down0", "dn_in"),
               "dn_conv": ("dn_out", "gate1", "up1", "down1")}
    SCATTERS = {"ffn1_b_dhn": ("down1",), "dn_b_conv_a": ("gate1", "up1", "dn_out"), "ffn0_b_dhn": ("dn_in",),
                "ret_b_scan": ("gate0", "up0"), "ret_b_dwin": ("down0", "ret_out"), "ret_b_dhn": ("ret_in",)}

    def __init__(self, shards, wts):
        self.shards, self.wts, self.grads, self.parts = shards, wts, {}, {}

    def _exchange(self, stage):
        if self.shards is None:
            return None
        if stage in self.GATHERS:
            return _Exchange([self.shards[n] for n in self.GATHERS[stage]], True)
        if stage in self.SCATTERS:
            return _Exchange([self._dev_major(n) for n in self.SCATTERS[stage]], False)
        return None

    def _dev_major(self, name):
        g = self.grads
        if name[:-1] in ("gate", "up"):
            return _dev_major_cols(g[name], g[name].shape[1] // N_DEV)
        if name[:-1] == "down":
            dwd = g[name]
            return dwd.reshape(N_DEV, dwd.shape[0] // N_DEV, dwd.shape[1])
        if name in ("ret_out", "dn_out"):
            return g[name].reshape(N_DEV, g[name].shape[0] // N_DEV, g[name].shape[1])
        return _dev_major_cols(g[name], self.shards[name].shape[-1])

    def _landed(self, stage, outs):
        if stage in self.SCATTERS:
            self.parts.update(zip(self.SCATTERS[stage], outs))
            return
        w = self.wts
        cols = lambda t: t.transpose(1, 0, 2).reshape(t.shape[1], N_DEV * t.shape[2])
        rows = lambda t: t.reshape(N_DEV * t.shape[1], t.shape[2])
        for name, t in zip(self.GATHERS[stage], outs):
            if name in ("ret_out", "dn_out") or name.startswith("down"):
                w[name] = rows(t)
            elif name == "dn_in":
                full = cols(t)
                n_main = DN_CONV_CH + DN_V
                w["dn_main"] = full[:, :n_main]
                w["dn_ba"] = jnp.pad(full[:, n_main:], ((0, 0), (0, BA_W - (full.shape[1] - n_main))))
            else:
                w[name] = cols(t)

    def matmul(self, stage, a, b, **kw):
        comm = self._exchange(stage)
        if comm is None:
            return _matmul(a, b, name=stage, **kw)
        out, landed = _matmul(a, b, name=stage, comm=comm, **kw)
        self._landed(stage, landed)
        return out

    def call(self, stage, fn, *args, n_out):
        comm = self._exchange(stage)
        out = fn(*args, stage, comm=comm)
        if comm is not None:
            self._landed(stage, out[n_out:])
        return out[:n_out]


def _local_step(x2, target, wts, shards=None):
    plan = _Plan(shards, wts)
    s_len, d = x2.shape
    l = s_len + CHUNK
    tb = _tile(l, 3072)
    th = tb // 2 if (tb // 2) % 16 == 0 else tb
    half = RET_DK // 2
    inv_freq = (np.float32(ROPE_BASE) ** (-np.arange(half, dtype=np.float32) / np.float32(half))).astype(np.float32)
    ang = (np.arange(l) - PAD).astype(np.float32)[:, None] * inv_freq[None, :]
    cos, sin = jnp.asarray(np.cos(ang), f32), jnp.asarray(np.sin(ang), f32)
    lgs = jnp.log1p(-jnp.exp2(-5.0 - jnp.arange(RET_HEADS, dtype=f32)))
    gcs = jnp.exp(lgs * _ret_block(l))

    h0 = jnp.concatenate([jnp.zeros((PAD, d), f32), wts["meta"], x2], axis=0)
    mixw, ffnw = wts["mix_norm"], wts["ffn_norm"]

    hn0 = _rms_fwd(h0, mixw[0:1], "l0_norm")
    proj0 = plan.matmul("ret_proj", hn0, wts["ret_in"], mode="nn", tm=tb, tn=512, tk=d)
    (qk0,) = plan.call("ret_prep", _ret_prep, proj0, cos, sin, n_out=1)
    o0, st0, y0 = plan.call("ret_scan", _ret_scan_fwd, qk0, proj0, wts["ret_gn"], lgs, gcs, n_out=3)
    h1, hn1 = _matmul(y0, wts["ret_out"], mode="nn", tm=th // 2, tn=d, tk=RET_V, res=h0, norm_w=ffnw[0:1],
                      name="ret_out")
    (h2, hn2), ffn0 = _ffn_fwd(h1, hn1, wts["gate0"], wts["up0"], wts["down0"], tb, th, "ffn0", plan,
                               next_norm_w=mixw[1:2])

    proj1 = plan.matmul("dn_proj", hn2, wts["dn_main"], mode="nn", tm=tb, tn=512, tk=d)
    ba = _matmul(hn2, wts["dn_ba"], mode="nn", tm=tb, tn=BA_W, tk=d, name="dn_proj_ba")
    (qkv1,) = plan.call("dn_conv", _dn_conv_fwd, proj1, wts["conv_w"], n_out=1)
    tinv1, u1, wq1, pk1, egl1, kpt1, qwt1 = plan.call("dn_prep", _dn_prep, qkv1, ba, wts["a_log"], wts["dt_bias"],
                                                      n_out=7)
    o1, st1, vn1 = _dn_scan_fwd(u1, wq1, pk1, egl1, "dn_scan")
    y1 = _gnorm_fwd(o1, proj1, wts["dn_norm"], DN_HEADS, DN_DV, 2, "dn_gnorm")
    h3, hn3 = _matmul(y1, wts["dn_out"], mode="nn", tm=th // 2, tn=d, tk=DN_V, res=h2, norm_w=ffnw[1:2],
                      name="dn_out")
    h4, ffn1 = _ffn_fwd(h3, hn3, wts["gate1"], wts["up1"], wts["down1"], tb, th, "ffn1", plan)

    dh4, dh4b, dfinal, loss = _final_loss(h4, wts["final_norm"], target, "final_loss")
    gr = plan.grads
    dh3, dh3b, dffn1 = _ffn_bwd(dh4, dh4b, h3, ffnw[1:2], wts["gate1"], wts["up1"], wts["down1"], ffn1,
                                tb, th, "ffn1", plan)

    gr["dn_out"] = _matmul(y1, dh3b, mode="tn", tm=1024, tn=d, tk=tb, out_dtype=bf16, name="dn_b_dwout")
    do1, dproj1, ddn_norm = _dy_gnorm_bwd(dh3b, wts["dn_out"], o1, proj1, wts["dn_norm"], DN_DV, 2, "dn_b_gnorm",
                                          tm=th // 2, tn=1024)
    dvn1, dsp1 = _dn_scan_bwd(do1, kpt1, qwt1, egl1, "dn_b_scan")
    dqkv1, dba, dalog, ddt = _dn_post_bwd(qkv1, ba, wts["a_log"], wts["dt_bias"], st1, dsp1, tinv1, u1, wq1, vn1,
                                          do1, dvn1, "dn_b_post")
    dc1, dconv = plan.call("dn_b_conv_a", _dn_conv_bwd_a, proj1, wts["conv_w"], dqkv1, n_out=2)
    dproj1 = _dn_conv_bwd_b(dc1, wts["conv_w"], dproj1, "dn_b_conv_b")
    dbab = dba.astype(bf16)
    n_main = dproj1.shape[1]
    dhn2_ba = _matmul(dbab, wts["dn_ba"], mode="nt", tm=th, tn=d, tk=BA_W, name="dn_b_dhn_ba")
    dh2, dh2b, dmix1 = plan.call("dn_b_dhn", functools.partial(_dhn_norm_bwd, tm=th // 2, tk=n_main // 4,
                                                              init=dhn2_ba),
                                 [(dproj1, wts["dn_main"])], h2, mixw[1:2], dh3, n_out=3)
    dw_main = _matmul(hn2, dproj1, mode="tn", tm=d, tn=512, tk=tb, out_dtype=bf16, name="dn_b_dwin")
    dw_ba = _matmul(hn2, dbab, mode="tn", tm=d, tn=BA_W, tk=tb, out_dtype=bf16, name="dn_b_dwin_ba")
    gr["dn_in"] = jnp.concatenate([dw_main, dw_ba], axis=1)

    dh1, dh1b, dffn0 = _ffn_bwd(dh2, dh2b, h1, ffnw[0:1], wts["gate0"], wts["up0"], wts["down0"], ffn0,
                                tb, th, "ffn0", plan)

    gr["ret_out"] = _matmul(y0, dh1b, mode="tn", tm=1024, tn=d, tk=tb, out_dtype=bf16, name="ret_b_dwout")
    do0, dproj0, dret_gn = _dy_gnorm_bwd(dh1b, wts["ret_out"], o0, proj0, wts["ret_gn"], RET_DV, 2, "ret_b_gnorm",
                                         tm=th // 2, tn=1024)
    dq0, dk0, dproj0 = plan.call("ret_b_scan", _ret_scan_bwd, qk0, proj0, st0, do0, dproj0, lgs, gcs, n_out=3)
    dproj0 = _ret_prep_bwd(dq0, dk0, cos, sin, dproj0, "ret_b_prep")
    n_in = dproj0.shape[1]
    gr["ret_in"] = plan.matmul("ret_b_dwin", hn0, dproj0, mode="tn", tm=d, tn=512, tk=tb, out_dtype=bf16)
    dh0, _, dmix0 = plan.call("ret_b_dhn", functools.partial(_dhn_norm_bwd, tm=th // 2, tk=n_in // 4),
                              [(dproj0, wts["ret_in"])], h0, mixw[0:1], dh1, n_out=3)

    gr.update(meta=dh0[PAD:CHUNK], mix_norm=jnp.concatenate([dmix0, dmix1], axis=0),
              ffn_norm=jnp.concatenate([dffn0, dffn1], axis=0), ret_gn=dret_gn, conv_w=dconv, a_log=dalog,
              dt_bias=ddt, dn_norm=ddn_norm, final_norm=dfinal)
    return loss, dh0[CHUNK:], gr, plan


def _adamw_reduce(parts, w, m, v, name):
    _, r, c = parts.shape
    c_pad = -(-c // LANES) * LANES
    tr = _div_tile(r, max(8, (3 * MIB // 16) // c_pad // 8 * 8), 16)

    def body(p_ref, w_ref, m_ref, v_ref, g_ref, d_ref, nm_ref, nv_ref):
        g = p_ref[0].astype(f32)
        for s in range(1, N_DEV):
            g = g + p_ref[s].astype(f32)
        mm = ADAM_B1 * m_ref[...] + (1.0 - ADAM_B1) * g
        vv = ADAM_B2 * v_ref[...] + (1.0 - ADAM_B2) * (g * g)
        m_hat = mm / (1.0 - ADAM_B1 ** ADAM_STEP)
        v_hat = vv / (1.0 - ADAM_B2 ** ADAM_STEP)
        g_ref[...] = g
        d_ref[...] = -ADAM_LR * (m_hat / (jnp.sqrt(v_hat) + ADAM_EPS) + ADAM_WD * w_ref[...])
        nm_ref[...] = mm
        nv_ref[...] = vv

    blk = pl.BlockSpec((tr, c), lambda i: (i, 0))
    return pl.pallas_call(
        body, name=name, grid=(r // tr,),
        in_specs=[pl.BlockSpec((N_DEV, tr, c), lambda i: (0, i, 0)), blk, blk, blk], out_specs=[blk] * 4,
        out_shape=[jax.ShapeDtypeStruct((r, c), f32)] * 4,
        compiler_params=_params(("parallel",), 48),
    )(parts, w, m, v)


def _dev_major_cols(g, width):
    r = g.shape[0]
    return g[:, :N_DEV * width].reshape(r, N_DEV, width).transpose(1, 0, 2)


def kernel(x, meta_tokens, mix_norm_w, ffn_norm_w, ret_w_in, ret_gn_w, ret_w_out, dn_w_in, dn_conv_w, dn_a_log, dn_dt_bias, dn_norm_w, dn_w_out, ffn_w_gate, ffn_w_up, ffn_w_down, final_norm_w, loss_target, m_meta_tokens, m_mix_norm_w, m_ffn_norm_w, m_ret_w_in, m_ret_gn_w, m_ret_w_out, m_dn_w_in, m_dn_conv_w, m_dn_a_log, m_dn_dt_bias, m_dn_norm_w, m_dn_w_out, m_ffn_w_gate, m_ffn_w_up, m_ffn_w_down, m_final_norm_w, v_meta_tokens, v_mix_norm_w, v_ffn_norm_w, v_ret_w_in, v_ret_gn_w, v_ret_w_out, v_dn_w_in, v_dn_conv_w, v_dn_a_log, v_dn_dt_bias, v_dn_norm_w, v_dn_w_out, v_ffn_w_gate, v_ffn_w_up, v_ffn_w_down, v_final_norm_w):
    d = x.shape[-1]
    me = 4 * lax.axis_index("x") + 2 * lax.axis_index("y") + lax.axis_index("c")

    shards = dict(ret_in=ret_w_in[0].astype(bf16), ret_out=ret_w_out[0].astype(bf16),
                  dn_in=dn_w_in[0].astype(bf16), dn_out=dn_w_out[0].astype(bf16))
    for layer in (0, 1):
        shards[f"gate{layer}"] = ffn_w_gate[layer].astype(bf16)
        shards[f"up{layer}"] = ffn_w_up[layer].astype(bf16)
        shards[f"down{layer}"] = ffn_w_down[layer].astype(bf16)
    g_ret_in, g_meta, g_conv, g_dnn = _exchange([shards["ret_in"], meta_tokens, dn_conv_w[0], dn_norm_w], True,
                                                "gather_first")
    cols = lambda g: g.transpose(1, 0, 2).reshape(g.shape[1], N_DEV * g.shape[2])
    wts = dict(meta=cols(g_meta), mix_norm=mix_norm_w, ffn_norm=ffn_norm_w, ret_in=cols(g_ret_in), ret_gn=ret_gn_w,
               conv_w=cols(g_conv), a_log=dn_a_log, dt_bias=dn_dt_bias, dn_norm=cols(g_dnn),
               final_norm=final_norm_w.reshape(1, d))

    loss_part, grad_x, gr, plan = _local_step(x[0], loss_target[0], wts, shards)
    loss = lax.psum(loss_part[0, 0], AXES)

    pp = plan.parts
    both = lambda name: jnp.concatenate([pp[name + "0"], pp[name + "1"]], axis=1)
    big_parts = [pp["ret_in"], pp["ret_out"], pp["dn_in"], pp["dn_out"], both("gate"), both("up"), both("down")]
    big_names = ["ret_w_in", "ret_w_out", "dn_w_in", "dn_w_out", "ffn_w_gate", "ffn_w_up", "ffn_w_down"]
    big_w = [ret_w_in, ret_w_out, dn_w_in, dn_w_out, ffn_w_gate, ffn_w_up, ffn_w_down]
    big_m = [m_ret_w_in, m_ret_w_out, m_dn_w_in, m_dn_w_out, m_ffn_w_gate, m_ffn_w_up, m_ffn_w_down]
    big_v = [v_ret_w_in, v_ret_w_out, v_dn_w_in, v_dn_w_out, v_ffn_w_gate, v_ffn_w_up, v_ffn_w_down]
    res = {}
    for nm, parts, w_, m_, v_ in zip(big_names, big_parts, big_w, big_m, big_v):
        r2, c2 = parts.shape[1], parts.shape[2]
        outs = _adamw_reduce(parts, w_.reshape(r2, c2), m_.reshape(r2, c2), v_.reshape(r2, c2), f"adamw_{nm}")
        res[nm] = [o.reshape(w_.shape) for o in outs]

    small_names = ["meta_tokens", "mix_norm_w", "ffn_norm_w", "ret_gn_w", "dn_conv_w", "dn_a_log", "dn_dt_bias",
                   "dn_norm_w", "final_norm_w"]
    small_g = [gr["meta"], gr["mix_norm"], gr["ffn_norm"], gr["ret_gn"], gr["conv_w"], gr["a_log"], gr["dt_bias"],
               gr["dn_norm"], gr["final_norm"]]
    small_w = [meta_tokens, mix_norm_w, ffn_norm_w, ret_gn_w, dn_conv_w, dn_a_log, dn_dt_bias, dn_norm_w, final_norm_w]
    small_m = [m_meta_tokens, m_mix_norm_w, m_ffn_norm_w, m_ret_gn_w, m_dn_conv_w, m_dn_a_log, m_dn_dt_bias,
               m_dn_norm_w, m_final_norm_w]
    small_v = [v_meta_tokens, v_mix_norm_w, v_ffn_norm_w, v_ret_gn_w, v_dn_conv_w, v_dn_a_log, v_dn_dt_bias,
               v_dn_norm_w, v_final_norm_w]
    sharded = {"meta_tokens", "dn_conv_w", "dn_norm_w"}
    flat = jnp.concatenate([g.reshape(-1) for g in small_g])
    row = 8 * LANES
    n_flat = flat.shape[0]
    flat = jnp.pad(flat, (0, -n_flat % row)).reshape(-1, row)
    (gathered,) = _exchange([flat], True, "gather_small_grads")
    gathered = gathered.reshape(N_DEV, -1)
    pieces, off = [], 0
    for nm, g, w_ in zip(small_names, small_g, small_w):
        full = gathered[:, off:off + g.size].reshape((N_DEV,) + g.shape)
        off += g.size
        if nm in sharded:
            wloc = w_.shape[-1]
            full = lax.dynamic_slice_in_dim(full, me * wloc, wloc, axis=full.ndim - 1)
        pieces.append(full.reshape(N_DEV, -1))
    sizes = [p.shape[1] for p in pieces]
    n_loc = sum(sizes)
    pad_loc = -n_loc % row

    def pack(vs, lead):
        cat = jnp.concatenate([a.reshape(lead + (-1,)) for a in vs], axis=-1)
        cat = jnp.pad(cat, [(0, 0)] * len(lead) + [(0, pad_loc)])
        return cat.reshape(lead + (-1, row))

    outs = _adamw_reduce(pack(pieces, (N_DEV,)), pack(small_w, ()), pack(small_m, ()), pack(small_v, ()), "adamw_small")
    off = 0
    for nm, sz, w_ in zip(small_names, sizes, small_w):
        res[nm] = [o.reshape(-1)[off:off + sz].reshape(w_.shape) for o in outs]
        off += sz

    order = ["meta_tokens", "mix_norm_w", "ffn_norm_w", "ret_w_in", "ret_gn_w", "ret_w_out", "dn_w_in", "dn_conv_w",
             "dn_a_log", "dn_dt_bias", "dn_norm_w", "dn_w_out", "ffn_w_gate", "ffn_w_up", "ffn_w_down", "final_norm_w"]
    grad_x = grad_x.reshape(x.shape)
    return (loss, grad_x, *[res[nm][0] for nm in order], *[res[nm][1] for nm in order],
            *[res[nm][2] for nm in order], *[res[nm][3] for nm in order])
```

```python
import functools
import math

import jax
import jax.numpy as jnp
import numpy as np
from jax import lax
from jax.experimental import pallas as pl
from jax.experimental.pallas import tpu as pltpu

f32 = jnp.float32
bf16 = jnp.bfloat16

N_META = 16
CHUNK = 64
PAD = CHUNK - N_META
RMS_EPS = 1e-6
RET_HEADS, RET_DK, RET_DV = 4, 256, 512
RET_QK, RET_V = RET_HEADS * RET_DK, RET_HEADS * RET_DV
DN_HEADS, DN_DK, DN_DV = 8, 128, 256
DN_QK, DN_V = DN_HEADS * DN_DK, DN_HEADS * DN_DV
DN_CONV_CH = 2 * DN_QK + DN_V
CONV_K = 4
ROPE_BASE = 10000.0
ADAM_LR, ADAM_B1, ADAM_B2, ADAM_EPS, ADAM_WD, ADAM_STEP = 0.001, 0.9, 0.999, 1e-08, 0.01, 10
N_DEV = 8
AXES = ("x", "y", "c")
LANES = 128
MIB = 1024 * 1024


def _tile(n_rows, cap):
    nch = n_rows // CHUNK
    best = 1
    for d in range(1, nch + 1):
        if nch % d == 0 and d * CHUNK <= cap:
            best = d
    return best * CHUNK


def _div_tile(n, cap, align):
    best = None
    for d in range(align, min(n, cap) + 1, align):
        if n % d == 0:
            best = d
    return best if best is not None else n


def _params(sem, vmem_mb):
    return pltpu.CompilerParams(dimension_semantics=sem, vmem_limit_bytes=int(vmem_mb * MIB))


def _nn(a, b, precision=None):
    return jnp.dot(a, b, preferred_element_type=f32, precision=precision)


def _nt(a, b, precision=None):
    return lax.dot_general(a, b, (((1,), (1,)), ((), ())), preferred_element_type=f32, precision=precision)


def _tn(a, b, precision=None):
    return lax.dot_general(a, b, (((0,), (0,)), ((), ())), preferred_element_type=f32, precision=precision)


def _b(x):
    return x.astype(bf16)


def _sigmoid(x):
    return 0.5 * jnp.tanh(0.5 * x) + 0.5


def _silu(x):
    return x * _sigmoid(x)


def _dsilu(x):
    s = _sigmoid(x)
    return s * (1.0 + x * (1.0 - s))


def _peer(k):
    x, y, c = lax.axis_index("x"), lax.axis_index("y"), lax.axis_index("c")
    px = 1 - x if k & 4 else x
    py = 1 - y if k & 2 else y
    pc = 1 - c if k & 1 else c
    return (px, py, pc), 4 * px + 2 * py + pc


class _Exchange:
    def __init__(self, arrs, gather):
        self.arrs, self.gather, self.n = list(arrs), gather, len(arrs)
        self.out_shapes = [jax.ShapeDtypeStruct(((N_DEV,) + a.shape) if gather else a.shape, a.dtype) for a in arrs]
        self.specs = [pl.BlockSpec(memory_space=pltpu.HBM)] * self.n
        self.scratch = [pltpu.SemaphoreType.DMA((self.n, N_DEV - 1)), pltpu.SemaphoreType.DMA((self.n, N_DEV - 1)),
                        pltpu.SemaphoreType.DMA((self.n,))]

    def _copies(self, ins, outs, sems):
        send_sems, recv_sems, local_sems = sems
        x, y, c = lax.axis_index("x"), lax.axis_index("y"), lax.axis_index("c")
        me = 4 * x + 2 * y + c

        def copy(a, k, src, slot, to):
            return pltpu.make_async_remote_copy(
                src_ref=src, dst_ref=outs[a].at[slot], send_sem=send_sems.at[a, k], recv_sem=recv_sems.at[a, k],
                device_id=to, device_id_type=pl.DeviceIdType.MESH)

        first, passed, lands_first, lands_rest = [], [], [], []
        if not self.gather:
            local = [pltpu.make_async_copy(ins[a].at[me], outs[a].at[me], local_sems.at[a]) for a in range(self.n)]
            for k in range(1, N_DEV):
                peer, pidx = _peer(k)
                for a in range(self.n):
                    first.append(copy(a, k - 1, ins[a].at[pidx], me, peer))
                    lands_rest.append(copy(a, k - 1, ins[a].at[pidx], pidx, peer))
            return local, first, passed, lands_first, lands_rest
        local = [pltpu.make_async_copy(ins[a], outs[a].at[me], local_sems.at[a]) for a in range(self.n)]
        sibling, sibling_slot = (x, y, 1 - c), 4 * x + 2 * y + (1 - c)
        chips = [(1 - x, y), (x, 1 - y), (1 - x, 1 - y)]
        for a in range(self.n):
            first.append(copy(a, 0, ins[a], me, sibling))
            lands_rest.append(copy(a, 0, ins[a], sibling_slot, sibling))
            for j, (px, py) in enumerate(chips):
                slot, slot_other = 4 * px + 2 * py + c, 4 * px + 2 * py + (1 - c)
                first.append(copy(a, 1 + j, ins[a], me, (px, py, c)))
                lands_first.append(copy(a, 1 + j, ins[a], slot, (px, py, c)))
                passed.append(copy(a, 4 + j, outs[a].at[slot], slot, sibling))
                lands_rest.append(copy(a, 4 + j, outs[a].at[slot_other], slot_other, sibling))
        return local, first, passed, lands_first, lands_rest

    def start(self, ins, outs, sems):
        local, first, _, _, _ = self._copies(ins, outs, sems)
        for cp in local + first:
            cp.start()

    def wait(self, ins, outs, sems):
        local, first, passed, lands_first, lands_rest = self._copies(ins, outs, sems)
        for landed, onward in zip(lands_first, passed):
            landed.wait_recv()
            onward.start()
        for cp in lands_rest:
            cp.wait_recv()
        for cp in first + passed:
            cp.wait_send()
        for cp in local:
            cp.wait()


def _call(body, args, *, name, grid, in_specs, out_specs, out_shape, scratch=(), sem, vmem_mb, comm=None,
          aliases=None):
    aliases = aliases or {}
    if comm is None:
        out = pl.pallas_call(body, name=name, grid=grid, in_specs=list(in_specs), out_specs=list(out_specs),
                             out_shape=list(out_shape), scratch_shapes=list(scratch), input_output_aliases=aliases,
                             compiler_params=_params(sem, vmem_mb))(*args)
        return list(out)
    n_in, n_out, n_scr, nc = len(args), len(out_shape), len(scratch), comm.n

    def carried(*refs):
        ins, cin = refs[:n_in], refs[n_in:n_in + nc]
        o0 = n_in + nc
        outs, cout = refs[o0:o0 + n_out], refs[o0 + n_out:o0 + n_out + nc]
        s0 = o0 + n_out + nc
        scr, sems = refs[s0:s0 + n_scr], refs[s0 + n_scr:]
        first = functools.reduce(jnp.logical_and, [pl.program_id(i) == 0 for i in range(len(grid))])
        last = functools.reduce(jnp.logical_and, [pl.program_id(i) == grid[i] - 1 for i in range(len(grid))])

        @pl.when(first)
        def _():
            comm.start(cin, cout, sems)

        body(*ins, *outs, *scr)

        @pl.when(last)
        def _():
            comm.wait(cin, cout, sems)

    out = pl.pallas_call(
        carried, name=name, grid=grid, in_specs=list(in_specs) + comm.specs, out_specs=list(out_specs) + comm.specs,
        out_shape=list(out_shape) + comm.out_shapes, scratch_shapes=list(scratch) + comm.scratch,
        input_output_aliases=aliases,
        compiler_params=_params(("arbitrary",) * len(grid), vmem_mb))(*args, *comm.arrs)
    return list(out)


def _exchange(arrs, gather, name):
    comm = _Exchange(arrs, gather)

    def body(*refs):
        ins, outs, sems = refs[:comm.n], refs[comm.n:2 * comm.n], refs[2 * comm.n:]
        comm.start(ins, outs, sems)
        comm.wait(ins, outs, sems)

    return pl.pallas_call(body, name=name, in_specs=comm.specs, out_specs=comm.specs, out_shape=comm.out_shapes,
                          scratch_shapes=comm.scratch)(*comm.arrs)


def _matmul(a, b, *, mode, tm, tn, tk, name, out_dtype=f32, res=None, vmem_mb=48, comm=None, norm_w=None):
    if mode == "nn":
        (m, k), (k2, n) = a.shape, b.shape
    elif mode == "nt":
        (m, k), (n, k2) = a.shape, b.shape
    else:
        (k, m), (k2, n) = a.shape, b.shape
    assert k == k2 and m % tm == 0 and n % tn == 0 and k % tk == 0, (name, a.shape, b.shape, tm, tn, tk)
    nk = k // tk
    dot = {"nn": _nn, "nt": _nt, "tn": _tn}[mode]
    a_spec = {"nn": pl.BlockSpec((tm, tk), lambda i, j, kk: (i, kk)),
              "nt": pl.BlockSpec((tm, tk), lambda i, j, kk: (i, kk)),
              "tn": pl.BlockSpec((tk, tm), lambda i, j, kk: (kk, i))}[mode]
    b_spec = {"nn": pl.BlockSpec((tk, tn), lambda i, j, kk: (kk, j)),
              "nt": pl.BlockSpec((tn, tk), lambda i, j, kk: (j, kk)),
              "tn": pl.BlockSpec((tk, tn), lambda i, j, kk: (kk, j))}[mode]
    o_spec = pl.BlockSpec((tm, tn), lambda i, j, kk: (i, j))
    has_res = res is not None
    has_norm = norm_w is not None
    assert not has_norm or tn == n
    n_ops = 2 + has_res + has_norm

    def body(*refs):
        a_ref, b_ref = refs[:2]
        r_ref = refs[2] if has_res else None
        nw_ref = refs[2 + has_res] if has_norm else None
        o_ref = refs[n_ops]
        hn_ref = refs[n_ops + 1] if has_norm else None
        rest = refs[n_ops + 1 + has_norm:]

        def finish(tot):
            if has_res:
                tot = tot + r_ref[...]
            o_ref[...] = tot.astype(out_dtype)
            if has_norm:
                r = lax.rsqrt(jnp.mean(tot * tot, axis=-1, keepdims=True) + RMS_EPS)
                hn_ref[...] = _b(tot * r * nw_ref[...])

        if nk == 1:
            finish(dot(_b(a_ref[...]), _b(b_ref[...])))
            return
        acc_ref = rest[0]
        kk = pl.program_id(2)

        @pl.when(kk == 0)
        def _():
            acc_ref[...] = dot(_b(a_ref[...]), _b(b_ref[...]))

        @pl.when(kk > 0)
        def _():
            acc_ref[...] += dot(_b(a_ref[...]), _b(b_ref[...]))

        @pl.when(kk == nk - 1)
        def _():
            finish(acc_ref[...])

    in_specs = [a_spec, b_spec]
    args = (a, b)
    if has_res:
        in_specs.append(o_spec)
        args += (res,)
    out_specs, out_shape = [o_spec], [jax.ShapeDtypeStruct((m, n), out_dtype)]
    if has_norm:
        in_specs.append(pl.BlockSpec((1, tn), lambda i, j, kk: (0, j)))
        args += (norm_w,)
        out_specs.append(o_spec)
        out_shape.append(jax.ShapeDtypeStruct((m, n), bf16))
    out = _call(body, args, name=name, grid=(m // tm, n // tn, nk), in_specs=in_specs, out_specs=out_specs,
                out_shape=out_shape, scratch=[pltpu.VMEM((tm, tn), f32)] if nk > 1 else [],
                sem=("parallel", "parallel", "arbitrary"), vmem_mb=vmem_mb, comm=comm)
    n_own = len(out_shape)
    own = out[0] if n_own == 1 else tuple(out[:n_own])
    return own if comm is None else (own, out[n_own:])


def _embed_norm(x2, meta, w, name):
    s_len, d = x2.shape
    l = s_len + CHUNK
    cpt = _tile(l, 256) // CHUNK

    def body(m_ref, w_ref, *rest):
        x_refs, (h_ref, hn_ref) = rest[:cpt], rest[cpt:]
        i = pl.program_id(0)
        prefix = jnp.concatenate([jnp.zeros((PAD, d), f32), m_ref[...]], axis=0)
        wv = w_ref[...]
        for c in range(cpt):
            rows = slice(c * CHUNK, (c + 1) * CHUNK)
            x = jnp.where(i * cpt + c > 0, x_refs[c][...], prefix)
            h_ref[rows, :] = x
            r = lax.rsqrt(jnp.mean(x * x, axis=-1, keepdims=True) + RMS_EPS)
            hn_ref[rows, :] = _b(x * r * wv)

    row = pl.BlockSpec((cpt * CHUNK, d), lambda i: (i, 0))
    x_specs = [pl.BlockSpec((CHUNK, d), functools.partial(lambda i, c: (jnp.maximum(i * cpt + c - 1, 0), 0), c=c))
               for c in range(cpt)]
    return pl.pallas_call(
        body, name=name, grid=(l // (cpt * CHUNK),),
        in_specs=[pl.BlockSpec((N_META, d), lambda i: (0, 0)), pl.BlockSpec((1, d), lambda i: (0, 0))] + x_specs,
        out_specs=[row, row],
        out_shape=[jax.ShapeDtypeStruct((l, d), f32), jax.ShapeDtypeStruct((l, d), bf16)],
        compiler_params=_params(("parallel",), 32),
    )(meta, w, *([x2] * cpt))


def _dhn_norm_bwd(pairs, h, nw, dres, name, *, tm, tk, init=None, comm=None):
    l, d = h.shape
    nks = [a.shape[1] // tk for a, _ in pairs]
    starts = [sum(nks[:p]) for p in range(len(pairs))]
    nk = sum(nks)
    assert nk >= 2
    n_ops = 2 * len(pairs)
    has_init = init is not None

    def body(*refs):
        ops = refs[:n_ops]
        init_ref = refs[n_ops] if has_init else None
        h_ref, w_ref, r_ref, dh_ref, dhb_ref, dw_ref, acc_ref = refs[n_ops + has_init:]
        i, kk = pl.program_id(0), pl.program_id(1)

        @pl.when(kk == 0)
        def _():
            part = _nt(ops[0][...], ops[1][...])
            acc_ref[...] = part + init_ref[...] if has_init else part

        for p in range(len(pairs)):
            lo, hi = max(starts[p], 1), min(starts[p] + nks[p], nk - 1)

            @pl.when(jnp.logical_and(kk >= lo, kk < hi))
            def _(a_ref=ops[2 * p], b_ref=ops[2 * p + 1]):
                acc_ref[...] += _nt(a_ref[...], b_ref[...])

        @pl.when(kk == nk - 1)
        def _():
            g = acc_ref[...] + _nt(ops[-2][...], ops[-1][...])
            x = h_ref[...]
            r = lax.rsqrt(jnp.mean(x * x, axis=-1, keepdims=True) + RMS_EPS)
            xh = x * r
            dxh = g * w_ref[...]
            dh = r_ref[...] + r * (dxh - xh * jnp.mean(dxh * xh, axis=-1, keepdims=True))
            dh_ref[...] = dh
            dhb_ref[...] = _b(dh)
            dw = jnp.sum(g * xh, axis=0, keepdims=True)

            @pl.when(i == 0)
            def _():
                dw_ref[...] = dw

            @pl.when(i > 0)
            def _():
                dw_ref[...] += dw

    def k_of(p):
        return lambda kk: jnp.clip(kk - starts[p], 0, nks[p] - 1)

    in_specs, args = [], []
    for p, (a, b) in enumerate(pairs):
        in_specs += [pl.BlockSpec((tm, tk), functools.partial(lambda i, kk, f: (i, f(kk)), f=k_of(p))),
                     pl.BlockSpec((d, tk), functools.partial(lambda i, kk, f: (0, f(kk)), f=k_of(p)))]
        args += [a, b]
    row = pl.BlockSpec((tm, d), lambda i, kk: (i, 0))
    vec = pl.BlockSpec((1, d), lambda i, kk: (0, 0))
    if has_init:
        in_specs.append(row)
        args.append(init)
    return _call(body, tuple(args) + (h, nw, dres), name=name, grid=(l // tm, nk), in_specs=in_specs + [row, vec, row],
                 out_specs=[row, row, vec],
                 out_shape=[jax.ShapeDtypeStruct((l, d), f32), jax.ShapeDtypeStruct((l, d), bf16),
                            jax.ShapeDtypeStruct((1, d), f32)],
                 scratch=[pltpu.VMEM((tm, d), f32)], sem=("arbitrary", "arbitrary"), vmem_mb=48, comm=comm)


def _final_loss(h, w, target, name):
    l, d = h.shape
    nch = l // CHUNK
    cpt = _tile(l, 256) // CHUNK
    nt = nch // cpt

    def body(h_ref, w_ref, *rest):
        t_refs, (dh_ref, dhb_ref, dw_ref, loss_ref) = rest[:cpt], rest[cpt:]
        i = pl.program_id(0)
        wv = w_ref[...]
        dw = jnp.zeros((1, d), f32)
        part = jnp.zeros((1, 1), f32)
        for c in range(cpt):
            rows = slice(c * CHUNK, (c + 1) * CHUNK)
            live = (i * cpt + c > 0).astype(f32)
            x = h_ref[rows, :]
            r = lax.rsqrt(jnp.mean(x * x, axis=-1, keepdims=True) + RMS_EPS)
            xh = x * r
            err = (xh * wv - t_refs[c][...]) * live
            dy = err * (1.0 / d)
            dxh = dy * wv
            dx = r * (dxh - xh * jnp.mean(dxh * xh, axis=-1, keepdims=True))
            dh_ref[rows, :] = dx
            dhb_ref[rows, :] = _b(dx)
            dw = dw + jnp.sum(dy * xh, axis=0, keepdims=True)
            part = part + 0.5 * jnp.sum(jnp.sum(err * err, axis=-1, keepdims=True) * (1.0 / d), axis=0, keepdims=True)
        part = jnp.broadcast_to(part, (1, LANES))

        @pl.when(i == 0)
        def _():
            dw_ref[...] = dw
            loss_ref[...] = part

        @pl.when(i > 0)
        def _():
            dw_ref[...] += dw
            loss_ref[...] += part

    row = pl.BlockSpec((cpt * CHUNK, d), lambda i: (i, 0))
    vec = pl.BlockSpec((1, d), lambda i: (0, 0))
    t_specs = [pl.BlockSpec((CHUNK, d), functools.partial(lambda i, c: (jnp.maximum(i * cpt + c - 1, 0), 0), c=c))
               for c in range(cpt)]
    return pl.pallas_call(
        body, name=name, grid=(nt,),
        in_specs=[row, vec] + t_specs,
        out_specs=[row, row, vec, pl.BlockSpec((1, LANES), lambda i: (0, 0))],
        out_shape=[jax.ShapeDtypeStruct((l, d), f32), jax.ShapeDtypeStruct((l, d), bf16),
                   jax.ShapeDtypeStruct((1, d), f32), jax.ShapeDtypeStruct((1, LANES), f32)],
        compiler_params=_params(("arbitrary",), 32),
    )(h, w, *([target] * cpt))


def _ffn_gu(hn, wg, wu, name, *, tm, tn, comm=None):
    l, d = hn.shape
    fh = wg.shape[1]

    def body(h_ref, g_ref, u_ref, a_ref, b_ref, s_ref):
        hb = h_ref[...]
        a = _nn(hb, g_ref[...])
        bb = _nn(hb, u_ref[...])
        a_ref[...] = _b(a)
        b_ref[...] = _b(bb)
        s_ref[...] = _b(_silu(a) * bb)

    wspec = pl.BlockSpec((d, tn), lambda i, j: (0, j))
    ospec = pl.BlockSpec((tm, tn), lambda i, j: (i, j))
    return _call(body, (hn, wg, wu), name=name, grid=(l // tm, fh // tn),
                 in_specs=[pl.BlockSpec((tm, d), lambda i, j: (i, 0)), wspec, wspec], out_specs=[ospec] * 3,
                 out_shape=[jax.ShapeDtypeStruct((l, fh), bf16)] * 3, sem=("parallel", "parallel"), vmem_mb=48,
                 comm=comm)


def _ffn_ds(dhb, wd, a, b, *, tm, tn, name):
    l, d = dhb.shape
    fh = wd.shape[0]

    def body(g_ref, w_ref, a_ref, b_ref, da_ref, db_ref):
        ds = _nt(g_ref[...], w_ref[...])
        a = a_ref[...].astype(f32)
        da_ref[...] = _b(ds * b_ref[...].astype(f32) * _dsilu(a))
        db_ref[...] = _b(ds * _silu(a))

    ospec = pl.BlockSpec((tm, tn), lambda i, j: (i, j))
    return pl.pallas_call(
        body, name=name, grid=(l // tm, fh // tn),
        in_specs=[pl.BlockSpec((tm, d), lambda i, j: (i, 0)), pl.BlockSpec((tn, d), lambda i, j: (j, 0)), ospec, ospec],
        out_specs=[ospec, ospec], out_shape=[jax.ShapeDtypeStruct((l, fh), bf16)] * 2,
        compiler_params=_params(("parallel", "parallel"), 48),
    )(dhb, wd, a, b)


def _gnorm_fwd(o, proj, nw, heads, dv, gate_blk, name):
    l, hv = o.shape
    tr = _tile(l, 256)

    def body(o_ref, g_ref, w_ref, y_ref):
        wv = w_ref[...]
        for h in range(heads):
            sl = slice(h * dv, (h + 1) * dv)
            oh = o_ref[:, sl]
            r = lax.rsqrt(jnp.mean(oh * oh, axis=-1, keepdims=True) + RMS_EPS)
            y_ref[:, sl] = _b(oh * r * wv * _silu(g_ref[:, sl]))

    return pl.pallas_call(
        body, name=name, grid=(l // tr,),
        in_specs=[pl.BlockSpec((tr, hv), lambda i: (i, 0)), pl.BlockSpec((tr, hv), lambda i: (i, gate_blk)),
                  pl.BlockSpec((1, dv), lambda i: (0, 0))],
        out_specs=pl.BlockSpec((tr, hv), lambda i: (i, 0)),
        out_shape=jax.ShapeDtypeStruct((l, hv), bf16),
        compiler_params=_params(("parallel",), 32),
    )(o, proj, nw)


def _dy_gnorm_bwd(dhb, w_out, o, proj, nw, dv, gate_blk, name, *, tm, tn):
    l, hv = o.shape
    d = dhb.shape[1]
    nj = hv // tn
    heads = tn // dv

    def body(g_ref, w_ref, o_ref, gate_ref, nw_ref, do_ref, dg_ref, dw_ref):
        dy = _nt(g_ref[...], w_ref[...])
        wv = nw_ref[...]
        dw = jnp.zeros((1, dv), f32)
        for h in range(heads):
            sl = slice(h * dv, (h + 1) * dv)
            oh = o_ref[:, sl]
            g = gate_ref[:, sl]
            dyh = dy[:, sl]
            r = lax.rsqrt(jnp.mean(oh * oh, axis=-1, keepdims=True) + RMS_EPS)
            xh = oh * r
            dn = dyh * _silu(g)
            dg_ref[:, sl] = _b(dyh * (xh * wv) * _dsilu(g))
            dxh = dn * wv
            do_ref[:, sl] = r * (dxh - xh * jnp.mean(dxh * xh, axis=-1, keepdims=True))
            dw = dw + jnp.sum(dn * xh, axis=0, keepdims=True)
        first = jnp.logical_and(pl.program_id(0) == 0, pl.program_id(1) == 0)

        @pl.when(first)
        def _():
            dw_ref[...] = dw

        @pl.when(jnp.logical_not(first))
        def _():
            dw_ref[...] += dw

    tile = pl.BlockSpec((tm, tn), lambda i, j: (i, j))
    gate = pl.BlockSpec((tm, tn), lambda i, j: (i, gate_blk * nj + j))
    vec = pl.BlockSpec((1, dv), lambda i, j: (0, 0))
    return pl.pallas_call(
        body, name=name, grid=(l // tm, nj),
        in_specs=[pl.BlockSpec((tm, d), lambda i, j: (i, 0)), pl.BlockSpec((tn, d), lambda i, j: (j, 0)),
                  tile, gate, vec],
        out_specs=[tile, gate, vec],
        out_shape=[jax.ShapeDtypeStruct((l, hv), f32), jax.ShapeDtypeStruct(proj.shape, bf16),
                   jax.ShapeDtypeStruct((1, dv), f32)],
        compiler_params=_params(("arbitrary", "arbitrary"), 48),
    )(dhb, w_out, o, proj, nw)


def _ret_prep(proj, cos, sin, name, comm=None):
    l = proj.shape[0]
    tr = _tile(l, 256)
    half = RET_DK // 2
    scale = RET_DK ** -0.5

    def body(p_ref, c_ref, s_ref, o_ref):
        rows = pl.program_id(0) * tr + lax.broadcasted_iota(jnp.int32, (tr, 1), 0)
        kmul = jnp.where(rows >= PAD, scale, 0.0).astype(f32)
        c, s = c_ref[...], s_ref[...]
        for j in range(2 * RET_HEADS):
            t1 = p_ref[:, j * RET_DK: j * RET_DK + half]
            t2 = p_ref[:, j * RET_DK + half: (j + 1) * RET_DK]
            o1 = t1 * c - t2 * s
            o2 = t1 * s + t2 * c
            if j >= RET_HEADS:
                o1, o2 = o1 * kmul, o2 * kmul
            o_ref[:, j * RET_DK: j * RET_DK + half] = o1
            o_ref[:, j * RET_DK + half: (j + 1) * RET_DK] = o2

    wide = pl.BlockSpec((tr, 2 * RET_QK), lambda i: (i, 0))
    tab = pl.BlockSpec((tr, half), lambda i: (i, 0))
    return _call(body, (proj, cos, sin), name=name, grid=(l // tr,), in_specs=[wide, tab, tab], out_specs=[wide],
                 out_shape=[jax.ShapeDtypeStruct((l, 2 * RET_QK), f32)], sem=("parallel",), vmem_mb=32, comm=comm)


def _ret_prep_bwd(dq, dk, cos, sin, dproj, name):
    l = dq.shape[0]
    tr = _tile(l, 256)
    half = RET_DK // 2
    scale = RET_DK ** -0.5

    def body(dq_ref, dk_ref, c_ref, s_ref, _, o_ref):
        rows = pl.program_id(0) * tr + lax.broadcasted_iota(jnp.int32, (tr, 1), 0)
        kmul = jnp.where(rows >= PAD, scale, 0.0).astype(f32)
        c, s = c_ref[...], s_ref[...]
        for j in range(2 * RET_HEADS):
            d_ref = dq_ref if j < RET_HEADS else dk_ref
            jj = j % RET_HEADS
            d1 = d_ref[:, jj * RET_DK: jj * RET_DK + half]
            d2 = d_ref[:, jj * RET_DK + half: (jj + 1) * RET_DK]
            if j >= RET_HEADS:
                d1, d2 = d1 * kmul, d2 * kmul
            o_ref[:, j * RET_DK: j * RET_DK + half] = _b(d1 * c + d2 * s)
            o_ref[:, j * RET_DK + half: (j + 1) * RET_DK] = _b(d2 * c - d1 * s)

    nar = pl.BlockSpec((tr, RET_QK), lambda i: (i, 0))
    wide = pl.BlockSpec((tr, 2 * RET_QK), lambda i: (i, 0))
    tab = pl.BlockSpec((tr, half), lambda i: (i, 0))
    return pl.pallas_call(
        body, name=name, grid=(l // tr,), in_specs=[nar, nar, tab, tab, pl.BlockSpec(memory_space=pl.ANY)],
        out_specs=wide, out_shape=jax.ShapeDtypeStruct(dproj.shape, dproj.dtype), input_output_aliases={4: 0},
        compiler_params=_params(("parallel",), 32),
    )(dq, dk, cos, sin, dproj)


RET_BLOCK_CHUNKS = 3


def _ret_block(l):
    nch = l // CHUNK
    return RET_BLOCK_CHUNKS * CHUNK if nch % RET_BLOCK_CHUNKS == 0 else CHUNK


def _ret_decay(lg, rb):
    idx = lax.broadcasted_iota(jnp.int32, (rb, 1), 0).astype(f32)
    ri = lax.broadcasted_iota(jnp.int32, (rb, rb), 0)
    ci = lax.broadcasted_iota(jnp.int32, (rb, rb), 1)
    rel = (ri - ci).astype(f32)
    dmask = jnp.where(ri >= ci, jnp.exp(lg * jnp.maximum(rel, 0.0)), 0.0)
    xi = jnp.exp(lg * (idx + 1.0))
    zeta = jnp.exp(lg * (rb - 1.0 - idx))
    return dmask, xi, zeta


def _ret_scan_fwd(qk, proj, gn_w, lgs, gcs, name, comm=None):
    l = qk.shape[0]
    rb = _ret_block(l)
    nb = l // rb

    def body(lg_ref, gc_ref, q_ref, k_ref, v_ref, g_ref, nw_ref, o_ref, st_ref, y_ref, s_ref):
        @pl.when(pl.program_id(0) == 0)
        def _():
            s_ref[...] = jnp.zeros_like(s_ref)

        hs = range(RET_HEADS)
        dec = [_ret_decay(lg_ref[h], rb) for h in hs]
        q = [q_ref[:, h * RET_DK:(h + 1) * RET_DK] for h in hs]
        k = [k_ref[:, h * RET_DK:(h + 1) * RET_DK] for h in hs]
        vb = [_b(v_ref[:, h * RET_DV:(h + 1) * RET_DV]) for h in hs]
        s = [s_ref[h] for h in hs]
        sb = [_b(s[h]) for h in hs]
        scores = [_b(_nt(_b(q[h]), _b(k[h])) * dec[h][0]) for h in hs]
        inter = [_nn(_b(q[h] * dec[h][1]), sb[h]) for h in hs]
        kv = [_tn(_b(k[h] * dec[h][2]), vb[h]) for h in hs]
        nw = nw_ref[...]
        for h in hs:
            cols = slice(h * RET_DV, (h + 1) * RET_DV)
            st_ref[0, h] = sb[h]
            o = _nn(scores[h], vb[h]) + inter[h]
            o_ref[:, cols] = o
            r = lax.rsqrt(jnp.mean(o * o, axis=-1, keepdims=True) + RMS_EPS)
            y_ref[:, cols] = _b(o * r * nw * _silu(g_ref[:, cols]))
            s_ref[h] = gc_ref[h] * s[h] + kv[h]

    smem = pl.BlockSpec(memory_space=pltpu.SMEM)
    wide = pl.BlockSpec((rb, RET_V), lambda n: (n, 0))
    return _call(
        body, (lgs, gcs, qk, qk, proj, proj, gn_w), name=name, grid=(nb,),
        in_specs=[smem, smem,
                  pl.BlockSpec((rb, RET_QK), lambda n: (n, 0)),
                  pl.BlockSpec((rb, RET_QK), lambda n: (n, 1)),
                  pl.BlockSpec((rb, RET_V), lambda n: (n, 1)),
                  pl.BlockSpec((rb, RET_V), lambda n: (n, 2)),
                  pl.BlockSpec((1, RET_DV), lambda n: (0, 0))],
        out_specs=[wide, pl.BlockSpec((1, RET_HEADS, RET_DK, RET_DV), lambda n: (n, 0, 0, 0)), wide],
        out_shape=[jax.ShapeDtypeStruct((l, RET_V), f32),
                   jax.ShapeDtypeStruct((nb, RET_HEADS, RET_DK, RET_DV), bf16),
                   jax.ShapeDtypeStruct((l, RET_V), bf16)],
        scratch=[pltpu.VMEM((RET_HEADS, RET_DK, RET_DV), f32)], sem=("arbitrary",), vmem_mb=40, comm=comm)


def _ret_scan_bwd(qk, proj, states, do, dproj, lgs, gcs, name, comm=None):
    l = qk.shape[0]
    rb = _ret_block(l)
    nb = l // rb

    def body(lg_ref, gc_ref, q_ref, k_ref, v_ref, st_ref, do_ref, _, dq_ref, dk_ref, dv_ref, ds_ref):
        @pl.when(pl.program_id(0) == 0)
        def _():
            ds_ref[...] = jnp.zeros_like(ds_ref)

        hs = range(RET_HEADS)
        dec = [_ret_decay(lg_ref[h], rb) for h in hs]
        q = [q_ref[:, h * RET_DK:(h + 1) * RET_DK] for h in hs]
        k = [k_ref[:, h * RET_DK:(h + 1) * RET_DK] for h in hs]
        qb, kb = [_b(t) for t in q], [_b(t) for t in k]
        vb = [_b(v_ref[:, h * RET_DV:(h + 1) * RET_DV]) for h in hs]
        dob = [_b(do_ref[:, h * RET_DV:(h + 1) * RET_DV]) for h in hs]
        dsp = [ds_ref[h] for h in hs]
        dspb = [_b(t) for t in dsp]
        scores = [_b(_nt(qb[h], kb[h]) * dec[h][0]) for h in hs]
        dscores = [_b(_nt(dob[h], vb[h]) * dec[h][0]) for h in hs]
        for h in hs:
            dq_ref[:, h * RET_DK:(h + 1) * RET_DK] = _nn(dscores[h], kb[h]) + _nt(dob[h], st_ref[0, h]) * dec[h][1]
        for h in hs:
            dk_ref[:, h * RET_DK:(h + 1) * RET_DK] = _tn(dscores[h], qb[h]) + _nt(vb[h], dspb[h]) * dec[h][2]
        for h in hs:
            dv_ref[:, h * RET_DV:(h + 1) * RET_DV] = _b(_tn(scores[h], dob[h]) + _nn(_b(k[h] * dec[h][2]), dspb[h]))
        for h in hs:
            ds_ref[h] = gc_ref[h] * dsp[h] + _tn(_b(q[h] * dec[h][1]), dob[h])

    smem = pl.BlockSpec(memory_space=pltpu.SMEM)
    rev = lambda n: nb - 1 - n
    return _call(
        body, (lgs, gcs, qk, qk, proj, states, do, dproj), name=name, grid=(nb,),
        in_specs=[smem, smem,
                  pl.BlockSpec((rb, RET_QK), lambda n: (rev(n), 0)),
                  pl.BlockSpec((rb, RET_QK), lambda n: (rev(n), 1)),
                  pl.BlockSpec((rb, RET_V), lambda n: (rev(n), 1)),
                  pl.BlockSpec((1, RET_HEADS, RET_DK, RET_DV), lambda n: (rev(n), 0, 0, 0)),
                  pl.BlockSpec((rb, RET_V), lambda n: (rev(n), 0)),
                  pl.BlockSpec(memory_space=pl.ANY)],
        out_specs=[pl.BlockSpec((rb, RET_QK), lambda n: (rev(n), 0)),
                   pl.BlockSpec((rb, RET_QK), lambda n: (rev(n), 0)),
                   pl.BlockSpec((rb, RET_V), lambda n: (rev(n), 1))],
        out_shape=[jax.ShapeDtypeStruct((l, RET_QK), f32), jax.ShapeDtypeStruct((l, RET_QK), f32),
                   jax.ShapeDtypeStruct(dproj.shape, dproj.dtype)],
        scratch=[pltpu.VMEM((RET_HEADS, RET_DK, RET_DV), f32)], sem=("arbitrary",), vmem_mb=40, comm=comm,
        aliases={7: 2})


CONV_BLK = 512
CONV_Q_BLKS = DN_QK // CONV_BLK
HALO = 8


def _conv_tile(l):
    return _tile(l, 3072)


def _slab_rows(r):
    return pl.ds(pl.multiple_of(r * HALO, HALO), HALO)


def _conv_slab(x_ref, p_ref, r, i, tr):
    cur = x_ref[_slab_rows(r), :]
    prev = jnp.where(r > 0, x_ref[_slab_rows(jnp.maximum(r - 1, 0)), :], p_ref[...])
    row0 = i * tr + r * HALO
    cur = jnp.where(row0 >= PAD, cur, 0.0)
    prev = jnp.where(row0 - HALO >= PAD, prev, 0.0)
    lrow = lax.broadcasted_iota(jnp.int32, (HALO, 1), 0)
    shifted = [jnp.where(lrow < s, pltpu.roll(prev, s, 0), pltpu.roll(cur, s, 0)) for s in range(1, CONV_K)]
    return [cur] + shifted


def _conv_of(xs, w):
    acc = xs[0] * w[CONV_K - 1:CONV_K, :]
    for s in range(1, CONV_K):
        acc = acc + xs[s] * w[CONV_K - 1 - s:CONV_K - s, :]
    return acc


def _slab_loop(n_slabs, fn, init=None):
    return lax.fori_loop(0, n_slabs, fn, init, unroll=8)


def _dn_conv_fwd(proj, conv_w, name, comm=None):
    l = proj.shape[0]
    tr = _conv_tile(l)
    nblk = DN_CONV_CH // CONV_BLK
    heads = CONV_BLK // DN_DK

    def body(x_ref, p_ref, w_ref, o_ref):
        i, j = pl.program_id(0), pl.program_id(1)
        w = w_ref[...]

        def act(r):
            return _silu(_conv_of(_conv_slab(x_ref, p_ref, r, i, tr), w))

        def normed(scale):
            def slab(r, carry):
                a = act(r)
                outs = []
                for h in range(heads):
                    ah = a[:, h * DN_DK:(h + 1) * DN_DK]
                    outs.append(ah * (lax.rsqrt(jnp.sum(ah * ah, axis=-1, keepdims=True) + RMS_EPS) * scale))
                o_ref[_slab_rows(r), :] = jnp.concatenate(outs, axis=1)
                return carry
            return slab

        def plain(r, carry):
            o_ref[_slab_rows(r), :] = act(r)
            return carry

        @pl.when(j < CONV_Q_BLKS)
        def _():
            _slab_loop(tr // HALO, normed(DN_DK ** -0.5))

        @pl.when(jnp.logical_and(j >= CONV_Q_BLKS, j < 2 * CONV_Q_BLKS))
        def _():
            _slab_loop(tr // HALO, normed(1.0))

        @pl.when(j >= 2 * CONV_Q_BLKS)
        def _():
            _slab_loop(tr // HALO, plain)

    hb = tr // HALO
    return _call(
        body, (proj, proj, conv_w), name=name, grid=(l // tr, nblk),
        in_specs=[pl.BlockSpec((tr, CONV_BLK), lambda i, j: (i, j)),
                  pl.BlockSpec((HALO, CONV_BLK), lambda i, j: (jnp.maximum(i * hb - 1, 0), j)),
                  pl.BlockSpec((CONV_K, CONV_BLK), lambda i, j: (0, j))],
        out_specs=[pl.BlockSpec((tr, CONV_BLK), lambda i, j: (i, j))],
        out_shape=[jax.ShapeDtypeStruct((l, DN_CONV_CH), f32)],
        scratch=[], sem=("parallel", "parallel"), vmem_mb=32, comm=comm)


def _dn_conv_bwd_a(proj, conv_w, dqkv, name, comm=None):
    l = proj.shape[0]
    tr = _conv_tile(l)
    nblk = DN_CONV_CH // CONV_BLK
    heads = CONV_BLK // DN_DK

    def body(x_ref, p_ref, w_ref, d_ref, dc_ref, dw_ref, acc_ref):
        j, i = pl.program_id(0), pl.program_id(1)
        w = w_ref[...]
        acc_ref[...] = jnp.zeros_like(acc_ref)

        def run(l2_scale):
            _slab_loop(tr // HALO, slab_of(l2_scale))

        def slab_of(l2_scale):
            def slab(r, carry):
                xs = _conv_slab(x_ref, p_ref, r, i, tr)
                c = _conv_of(xs, w)
                a = _silu(c)
                dy = d_ref[_slab_rows(r), :]
                if l2_scale is None:
                    da = dy
                else:
                    parts = []
                    for h in range(heads):
                        sl = slice(h * DN_DK, (h + 1) * DN_DK)
                        ah, dyh = a[:, sl], dy[:, sl]
                        rn = lax.rsqrt(jnp.sum(ah * ah, axis=-1, keepdims=True) + RMS_EPS)
                        yh = ah * rn
                        parts.append((rn * l2_scale) * (dyh - yh * jnp.sum(dyh * yh, axis=-1, keepdims=True)))
                    da = jnp.concatenate(parts, axis=1)
                dc = da * _dsilu(c)
                dc_ref[_slab_rows(r), :] = dc
                for k in range(CONV_K):
                    acc_ref[k] += dc * xs[CONV_K - 1 - k]
                return carry
            return slab

        @pl.when(j < CONV_Q_BLKS)
        def _():
            run(DN_DK ** -0.5)

        @pl.when(jnp.logical_and(j >= CONV_Q_BLKS, j < 2 * CONV_Q_BLKS))
        def _():
            run(1.0)

        @pl.when(j >= 2 * CONV_Q_BLKS)
        def _():
            run(None)

        ksel = lax.broadcasted_iota(jnp.int32, (CONV_K, 1), 0)
        dw = jnp.zeros((CONV_K, CONV_BLK), f32)
        for k in range(CONV_K):
            dw = dw + jnp.where(ksel == k, jnp.sum(acc_ref[k], axis=0, keepdims=True), 0.0)

        @pl.when(i == 0)
        def _():
            dw_ref[...] = dw

        @pl.when(i > 0)
        def _():
            dw_ref[...] += dw

    hb = tr // HALO
    blk = pl.BlockSpec((tr, CONV_BLK), lambda j, i: (i, j))
    return _call(
        body, (proj, proj, conv_w, dqkv), name=name, grid=(nblk, l // tr),
        in_specs=[blk, pl.BlockSpec((HALO, CONV_BLK), lambda j, i: (jnp.maximum(i * hb - 1, 0), j)),
                  pl.BlockSpec((CONV_K, CONV_BLK), lambda j, i: (0, j)), blk],
        out_specs=[blk, pl.BlockSpec((CONV_K, CONV_BLK), lambda j, i: (0, j))],
        out_shape=[jax.ShapeDtypeStruct((l, DN_CONV_CH), f32), jax.ShapeDtypeStruct((CONV_K, DN_CONV_CH), f32)],
        scratch=[pltpu.VMEM((CONV_K, HALO, CONV_BLK), f32)], sem=("parallel", "arbitrary"), vmem_mb=48, comm=comm)


def _dn_conv_bwd_b(dc, conv_w, dproj, name):
    l = dc.shape[0]
    tr = _conv_tile(l)
    nblk = DN_CONV_CH // CONV_BLK
    nrow = l // tr

    n_slabs = tr // HALO
    pair = 2 * HALO

    def body(d_ref, n_ref, w_ref, _, o_ref):
        i = pl.program_id(0)
        w = w_ref[...]
        nxt_tile = jnp.where(i < nrow - 1, n_ref[...], 0.0)
        lrow = lax.broadcasted_iota(jnp.int32, (HALO, 1), 0)

        def one(r):
            cur = d_ref[_slab_rows(r), :]
            nxt = jnp.where(r < n_slabs - 1, d_ref[_slab_rows(jnp.minimum(r + 1, n_slabs - 1)), :], nxt_tile)
            acc = cur * w[CONV_K - 1:CONV_K, :]
            for s in range(1, CONV_K):
                up = jnp.where(lrow >= HALO - s, pltpu.roll(nxt, HALO - s, 0), pltpu.roll(cur, HALO - s, 0))
                acc = acc + up * w[CONV_K - 1 - s:CONV_K - s, :]
            return jnp.where(i * tr + r * HALO >= PAD, acc, 0.0)

        def two(q, carry):
            rows = pl.ds(pl.multiple_of(q * pair, pair), pair)
            o_ref[rows, :] = _b(jnp.concatenate([one(2 * q), one(2 * q + 1)], axis=0))
            return carry

        lax.fori_loop(0, n_slabs // 2, two, None, unroll=4)

    hb = tr // HALO
    nh = l // HALO
    return pl.pallas_call(
        body, name=name, grid=(nrow, nblk),
        in_specs=[pl.BlockSpec((tr, CONV_BLK), lambda i, j: (i, j)),
                  pl.BlockSpec((HALO, CONV_BLK), lambda i, j: (jnp.minimum((i + 1) * hb, nh - 1), j)),
                  pl.BlockSpec((CONV_K, CONV_BLK), lambda i, j: (0, j)),
                  pl.BlockSpec(memory_space=pl.ANY)],
        out_specs=pl.BlockSpec((tr, CONV_BLK), lambda i, j: (i, j)),
        out_shape=jax.ShapeDtypeStruct(dproj.shape, dproj.dtype), input_output_aliases={3: 0},
        compiler_params=_params(("parallel", "parallel"), 32),
    )(dc, dc, conv_w, dproj)


BA_W = LANES


def _dn_gates(ba_ref, al_ref, dt_ref, n):
    rows = n * CHUNK + lax.broadcasted_iota(jnp.int32, (CHUNK, 1), 0)
    vm = (rows >= PAD).astype(f32)
    bin_ = ba_ref[:, 0:DN_HEADS]
    z = ba_ref[:, DN_HEADS:2 * DN_HEADS] + dt_ref[...]
    sp = jnp.maximum(z, 0.0) + jnp.log1p(jnp.exp(-jnp.abs(z)))
    ea = jnp.exp(al_ref[...])
    beta = _sigmoid(bin_) * vm
    g = -ea * sp * vm
    return vm, bin_, z, ea, beta, g


def _tri():
    ri = lax.broadcasted_iota(jnp.int32, (CHUNK, CHUNK), 0)
    ci = lax.broadcasted_iota(jnp.int32, (CHUNK, CHUNK), 1)
    return ri, ci


def _split(a):
    hi = _b(a)
    return hi, _b(a - hi.astype(f32))


def _mm3(a, b, dot=_nn):
    (ah, al), (bh, bl) = _split(a), _split(b)
    return dot(ah, bh) + (dot(ah, bl) + dot(al, bh))


def _cumsum_rows(tri, g):
    tb = _b(tri)
    g1 = _b(g)
    r1 = g - g1.astype(f32)
    g2 = _b(r1)
    g3 = _b(r1 - g2.astype(f32))
    return _nn(tb, g1) + (_nn(tb, g2) + _nn(tb, g3))


DN_SCAN_CHUNKS = 3


def _scan_chunks(nch):
    return DN_SCAN_CHUNKS if nch % DN_SCAN_CHUNKS == 0 else 1


def _dn_prep(qkv, ba, a_log, dt_bias, name, comm=None):
    l = qkv.shape[0]
    nch = l // CHUNK
    heads = range(DN_HEADS)

    cb = _scan_chunks(nch)
    items = [(c, h) for c in range(cb) for h in heads]

    def body(q_ref, k_ref, v_ref, ba_ref, al_ref, dt_ref, t_ref, u_ref, wq_ref, pk_ref, eg_ref, kpt_ref, qwt_ref):
        n0 = pl.program_id(0) * cb
        ri, ci = _tri()
        incl, strict = ri >= ci, ri > ci
        eye = (ri == ci).astype(f32)
        rows = [slice(c * CHUNK, (c + 1) * CHUNK) for c in range(cb)]
        gam, gam_t, beta = [], [], []
        for c in range(cb):
            _, _, _, _, beta_c, g_c = _dn_gates(ba_ref[rows[c], :], al_ref, dt_ref, n0 + c)
            gam.append(_cumsum_rows(incl.astype(f32), g_c))
            gam_t.append(gam[c].T)
            beta.append(beta_c)
        gc = {(c, h): gam[c][:, h:h + 1] for c, h in items}
        bh = {(c, h): beta[c][:, h:h + 1] for c, h in items}
        kh = {(c, h): k_ref[rows[c], h * DN_DK:(h + 1) * DN_DK] for c, h in items}
        kb = {i: _b(kh[i]) for i in items}
        decay = {(c, h): jnp.exp(jnp.where(incl, gc[c, h] - gam_t[c][h:h + 1, :], -jnp.inf)) for c, h in items}
        a = {i: jnp.where(strict, bh[i] * _nt(kb[i], kb[i]) * decay[i], 0.0) for i in items}
        t = {i: eye - a[i] for i in items}
        p = a
        for level in range(int(math.log2(CHUNK)) - 1):
            mm = _mm3 if level < 2 else (lambda x, y: _nn(_b(x), _b(y)))
            p = {i: mm(p[i], p[i]) for i in items}
            t = {i: t[i] + mm(t[i], p[i]) for i in items}
        eg = {i: jnp.exp(gc[i]) for i in items}
        for c, h in items:
            i = (c, h)
            t_ref[c, h] = t[i]
            u_ref[rows[c], h * DN_DV:(h + 1) * DN_DV] = _mm3(t[i], v_ref[rows[c], h * DN_DV:(h + 1) * DN_DV] * bh[i])
            w = _mm3(t[i], kh[i] * (bh[i] * eg[i]))
            wq_ref[c, h, 0:CHUNK, :] = _b(w)
            qwt_ref[c, h, DN_DK:2 * DN_DK, :] = _b(w.T)
        for c, h in items:
            i = (c, h)
            qh = q_ref[rows[c], h * DN_DK:(h + 1) * DN_DK]
            gl = gc[i][CHUNK - 1:CHUNK, :]
            qe = qh * eg[i]
            ke = kh[i] * jnp.exp(gl - gc[i])
            pmat = _nt(_b(qh), kb[i]) * decay[i]
            wq_ref[c, h, CHUNK:2 * CHUNK, :] = _b(qe)
            qwt_ref[c, h, 0:DN_DK, :] = _b(qe.T)
            pk_ref[c, h, 0:CHUNK, :] = _b(pmat)
            pk_ref[c, h, CHUNK:CHUNK + DN_DK, :] = _b(ke.T)
            kpt_ref[c, h, :, 0:DN_DK] = _b(ke)
            kpt_ref[c, h, :, DN_DK:DN_DK + CHUNK] = _b(pmat.T)
            eg_ref[c, h] = jnp.broadcast_to(jnp.exp(gl), (8, LANES))

    vec = pl.BlockSpec((1, DN_HEADS), lambda n: (0, 0))
    return _call(
        body, (qkv, qkv, qkv, ba, a_log, dt_bias), name=name, grid=(nch // cb,),
        in_specs=[pl.BlockSpec((cb * CHUNK, DN_QK), lambda n: (n, 0)),
                  pl.BlockSpec((cb * CHUNK, DN_QK), lambda n: (n, 1)),
                  pl.BlockSpec((cb * CHUNK, DN_V), lambda n: (n, 1)),
                  pl.BlockSpec((cb * CHUNK, BA_W), lambda n: (n, 0)), vec, vec],
        out_specs=[pl.BlockSpec((cb, DN_HEADS, CHUNK, CHUNK), lambda n: (n, 0, 0, 0)),
                   pl.BlockSpec((cb * CHUNK, DN_V), lambda n: (n, 0)),
                   pl.BlockSpec((cb, DN_HEADS, 2 * CHUNK, DN_DK), lambda n: (n, 0, 0, 0)),
                   pl.BlockSpec((cb, DN_HEADS, CHUNK + DN_DK, CHUNK), lambda n: (n, 0, 0, 0)),
                   pl.BlockSpec((cb, DN_HEADS, 8, LANES), lambda n: (n, 0, 0, 0)),
                   pl.BlockSpec((cb, DN_HEADS, CHUNK, DN_DK + CHUNK), lambda n: (n, 0, 0, 0)),
                   pl.BlockSpec((cb, DN_HEADS, 2 * DN_DK, CHUNK), lambda n: (n, 0, 0, 0))],
        out_shape=[jax.ShapeDtypeStruct((nch, DN_HEADS, CHUNK, CHUNK), f32),
                   jax.ShapeDtypeStruct((l, DN_V), f32),
                   jax.ShapeDtypeStruct((nch, DN_HEADS, 2 * CHUNK, DN_DK), bf16),
                   jax.ShapeDtypeStruct((nch, DN_HEADS, CHUNK + DN_DK, CHUNK), bf16),
                   jax.ShapeDtypeStruct((nch, DN_HEADS, 8, LANES), f32),
                   jax.ShapeDtypeStruct((nch, DN_HEADS, CHUNK, DN_DK + CHUNK), bf16),
                   jax.ShapeDtypeStruct((nch, DN_HEADS, 2 * DN_DK, CHUNK), bf16)],
        sem=("parallel",), vmem_mb=40, comm=comm)


def _dn_scan_fwd(u, wq, pk, egl, name):
    l = u.shape[0]
    nch = l // CHUNK
    cs = _scan_chunks(nch)

    def body(u_ref, wq_ref, pk_ref, eg_ref, o_ref, st_ref, vn_ref, s_ref):
        @pl.when(pl.program_id(0) == 0)
        def _():
            s_ref[...] = jnp.zeros_like(s_ref)

        hs = range(DN_HEADS)
        cols = [slice(h * DN_DV, (h + 1) * DN_DV) for h in hs]
        s = [s_ref[h] for h in hs]
        for c in range(cs):
            rows = slice(c * CHUNK, (c + 1) * CHUNK)
            sb = [_b(s[h]) for h in hs]
            x = [_nn(wq_ref[c, h], sb[h]) for h in hs]
            vnb = [_b(u_ref[rows, cols[h]] - x[h][0:CHUNK]) for h in hs]
            y = [_nn(pk_ref[c, h], vnb[h]) for h in hs]
            for h in hs:
                st_ref[c, h] = sb[h]
                vn_ref[rows, cols[h]] = vnb[h]
                o_ref[rows, cols[h]] = x[h][CHUNK:2 * CHUNK] + y[h][0:CHUNK]
            s = [eg_ref[c, h][0:1, 0:1] * s[h] + y[h][CHUNK:CHUNK + DN_DK] for h in hs]
        for h in hs:
            s_ref[h] = s[h]

    return pl.pallas_call(
        body, name=name, grid=(nch // cs,),
        in_specs=[pl.BlockSpec((cs * CHUNK, DN_V), lambda n: (n, 0)),
                  pl.BlockSpec((cs, DN_HEADS, 2 * CHUNK, DN_DK), lambda n: (n, 0, 0, 0)),
                  pl.BlockSpec((cs, DN_HEADS, CHUNK + DN_DK, CHUNK), lambda n: (n, 0, 0, 0)),
                  pl.BlockSpec((cs, DN_HEADS, 8, LANES), lambda n: (n, 0, 0, 0))],
        out_specs=[pl.BlockSpec((cs * CHUNK, DN_V), lambda n: (n, 0)),
                   pl.BlockSpec((cs, DN_HEADS, DN_DK, DN_DV), lambda n: (n, 0, 0, 0)),
                   pl.BlockSpec((cs * CHUNK, DN_V), lambda n: (n, 0))],
        out_shape=[jax.ShapeDtypeStruct((l, DN_V), f32),
                   jax.ShapeDtypeStruct((nch, DN_HEADS, DN_DK, DN_DV), bf16),
                   jax.ShapeDtypeStruct((l, DN_V), bf16)],
        scratch_shapes=[pltpu.VMEM((DN_HEADS, DN_DK, DN_DV), f32)],
        compiler_params=_params(("arbitrary",), 40),
    )(u, wq, pk, egl)


def _dn_scan_bwd(do, kpt, qwt, egl, name):
    l = do.shape[0]
    nch = l // CHUNK
    cs = _scan_chunks(nch)
    nblk = nch // cs

    def body(do_ref, kpt_ref, qwt_ref, eg_ref, dvn_ref, dsp_ref, ds_ref):
        @pl.when(pl.program_id(0) == 0)
        def _():
            ds_ref[...] = jnp.zeros_like(ds_ref)

        hs = range(DN_HEADS)
        cols = [slice(h * DN_DV, (h + 1) * DN_DV) for h in hs]
        ds = [ds_ref[h] for h in hs]
        for c in reversed(range(cs)):
            rows = slice(c * CHUNK, (c + 1) * CHUNK)
            dspb = [_b(ds[h]) for h in hs]
            dob = [_b(do_ref[rows, cols[h]]) for h in hs]
            dvn = [_nn(kpt_ref[c, h][:, 0:DN_DK], dspb[h]) + _nn(kpt_ref[c, h][:, DN_DK:DN_DK + CHUNK], dob[h])
                   for h in hs]
            for h in hs:
                dsp_ref[c, h] = dspb[h]
                dvn_ref[rows, cols[h]] = dvn[h]
            ds = [eg_ref[c, h][0:1, 0:1] * ds[h] + _nn(qwt_ref[c, h][0:DN_DK], dob[h])
                  - _nn(qwt_ref[c, h][DN_DK:2 * DN_DK], _b(dvn[h])) for h in hs]
        for h in hs:
            ds_ref[h] = ds[h]

    rev = lambda s: nblk - 1 - s
    return pl.pallas_call(
        body, name=name, grid=(nblk,),
        in_specs=[pl.BlockSpec((cs * CHUNK, DN_V), lambda s: (rev(s), 0)),
                  pl.BlockSpec((cs, DN_HEADS, CHUNK, DN_DK + CHUNK), lambda s: (rev(s), 0, 0, 0)),
                  pl.BlockSpec((cs, DN_HEADS, 2 * DN_DK, CHUNK), lambda s: (rev(s), 0, 0, 0)),
                  pl.BlockSpec((cs, DN_HEADS, 8, LANES), lambda s: (rev(s), 0, 0, 0))],
        out_specs=[pl.BlockSpec((cs * CHUNK, DN_V), lambda s: (rev(s), 0)),
                   pl.BlockSpec((cs, DN_HEADS, DN_DK, DN_DV), lambda s: (rev(s), 0, 0, 0))],
        out_shape=[jax.ShapeDtypeStruct((l, DN_V), f32),
                   jax.ShapeDtypeStruct((nch, DN_HEADS, DN_DK, DN_DV), bf16)],
        scratch_shapes=[pltpu.VMEM((DN_HEADS, DN_DK, DN_DV), f32)],
        compiler_params=_params(("arbitrary",), 40),
    )(do, kpt, qwt, egl)


def _dn_post_bwd(qkv, ba, a_log, dt_bias, states, dsp_all, tinv_all, u_all, wq, vn_all, do, dvn_all, name):
    l = qkv.shape[0]
    nch = l // CHUNK
    cb = _scan_chunks(nch)
    items = [(c, h) for c in range(cb) for h in range(DN_HEADS)]

    def body(q_ref, k_ref, v_ref, ba_ref, al_ref, dt_ref, st_ref, dsp_ref, t_ref, u_ref, wq_ref, vn_ref, do_ref,
             dvn_ref, dqkv_ref, dba_ref, dal_ref, ddt_ref):
        step = pl.program_id(0)
        ri, ci = _tri()
        incl, strict = ri >= ci, ri > ci
        lane8 = lax.broadcasted_iota(jnp.int32, (1, DN_HEADS), 1)
        sub8 = lax.broadcasted_iota(jnp.int32, (DN_HEADS, 1), 0)
        last = (lax.broadcasted_iota(jnp.int32, (CHUNK, 1), 0) == CHUNK - 1).astype(f32)
        rsum = lambda t: jnp.sum(t, axis=-1, keepdims=True)
        rows = [slice(c * CHUNK, (c + 1) * CHUNK) for c in range(cb)]
        gates = [_dn_gates(ba_ref[rows[c], :], al_ref, dt_ref, step * cb + c) for c in range(cb)]
        gam = [_cumsum_rows(incl.astype(f32), gates[c][5]) for c in range(cb)]
        gam_t = [gam[c].T for c in range(cb)]
        each = lambda fn: {(c, h): fn(c, h) for c, h in items}
        dk_cols = lambda h: slice(h * DN_DK, (h + 1) * DN_DK)
        dv_cols = lambda h: slice(h * DN_DV, (h + 1) * DN_DV)
        gc = each(lambda c, h: gam[c][:, h:h + 1])
        bh = each(lambda c, h: gates[c][4][:, h:h + 1])
        qh = each(lambda c, h: q_ref[rows[c], dk_cols(h)])
        kh = each(lambda c, h: k_ref[rows[c], dk_cols(h)])
        doh = each(lambda c, h: _b(do_ref[rows[c], dv_cols(h)]))
        sb = each(lambda c, h: st_ref[c, h])
        dspb = each(lambda c, h: dsp_ref[c, h])
        vnb = each(lambda c, h: vn_ref[rows[c], dv_cols(h)])
        dvn = each(lambda c, h: dvn_ref[rows[c], dv_cols(h)])
        wb = each(lambda c, h: wq_ref[c, h, 0:CHUNK, :])
        decay = each(lambda c, h: jnp.exp(jnp.where(incl, gc[c, h] - gam_t[c][h:h + 1, :], -jnp.inf)))
        qb, kb = each(lambda c, h: _b(qh[c, h])), each(lambda c, h: _b(kh[c, h]))
        eg = each(lambda c, h: jnp.exp(gc[c, h]))
        gl = each(lambda c, h: gc[c, h][CHUNK - 1:CHUNK, :])
        ekd = each(lambda c, h: jnp.exp(gl[c, h] - gc[c, h]))
        dvnb = each(lambda c, h: _b(dvn[c, h]))
        kk = each(lambda c, h: _nt(kb[c, h], kb[c, h]))
        p = each(lambda c, h: _nt(qb[c, h], kb[c, h]) * decay[c, h])
        dpraw = each(lambda c, h: _nt(doh[c, h], vnb[c, h]))
        dqe = each(lambda c, h: _nt(doh[c, h], sb[c, h]))
        dke = each(lambda c, h: _nt(vnb[c, h], dspb[c, h]))
        dw = each(lambda c, h: -_nt(dvnb[c, h], sb[c, h]))
        dru = each(lambda c, h: _mm3(t_ref[c, h], dvn[c, h], _tn))
        drw = each(lambda c, h: _mm3(t_ref[c, h], dw[c, h], _tn))
        dqk = each(lambda c, h: _b(dpraw[c, h] * decay[c, h]))
        for c, h in items:
            i = (c, h)
            dqkv_ref[rows[c], dk_cols(h)] = _nn(dqk[i], kb[i]) + dqe[i] * eg[i]
            dqkv_ref[rows[c], 2 * DN_QK + h * DN_DV:2 * DN_QK + (h + 1) * DN_DV] = bh[i] * dru[i]
        da = each(lambda c, h: jnp.where(strict, -(_nt(_b(dru[c, h]), _b(u_ref[rows[c], dv_cols(h)]))
                                                   + _nt(_b(drw[c, h]), wb[c, h])), 0.0))
        dkk = each(lambda c, h: _b(da[c, h] * bh[c, h] * decay[c, h]))
        for c, h in items:
            i = (c, h)
            dqkv_ref[rows[c], DN_QK + h * DN_DK:DN_QK + (h + 1) * DN_DK] = (
                _tn(dqk[i], qb[i]) + dke[i] * ekd[i] + (bh[i] * eg[i]) * drw[i]
                + _nn(dkk[i], kb[i]) + _tn(dkk[i], kb[i]))
        dal = jnp.zeros((1, DN_HEADS), f32)
        ddt = jnp.zeros((1, DN_HEADS), f32)
        dba_ref[...] = jnp.zeros_like(dba_ref)
        for c in range(cb):
            vm, bin_, z, ea, _, g = gates[c]
            dbeta = jnp.zeros((CHUNK, DN_HEADS), f32)
            dgam = jnp.zeros((CHUNK, DN_HEADS), f32)
            dgam_neg_t = jnp.zeros((DN_HEADS, CHUNK), f32)
            for h in range(DN_HEADS):
                i = (c, h)
                keg = kh[i] * eg[i]
                ke = kh[i] * ekd[i]
                rw = rsum(drw[i] * keg)
                rke = rsum(dke[i] * ke)
                db_h = rsum(dru[i] * v_ref[rows[c], dv_cols(h)]) + rw + rsum(da[i] * kk[i] * decay[i])
                mm = da[i] * (bh[i] * kk[i] * decay[i]) + dpraw[i] * p[i]
                dgl = (jnp.sum(rke, axis=0, keepdims=True)
                       + jnp.exp(gl[i]) * jnp.sum(rsum(dspb[i].astype(f32) * sb[i].astype(f32)), axis=0,
                                                  keepdims=True))
                dg_h = rsum(mm) + rw * bh[i] + rsum(dqe[i] * (qh[i] * eg[i])) - rke + last * dgl
                dbeta = dbeta + jnp.where(lane8 == h, db_h, 0.0)
                dgam = dgam + jnp.where(lane8 == h, dg_h, 0.0)
                dgam_neg_t = dgam_neg_t + jnp.where(sub8 == h, jnp.sum(mm, axis=0, keepdims=True), 0.0)
            dgam = dgam - dgam_neg_t.T
            dg = _cumsum_rows((ri <= ci).astype(f32), dgam)
            sg = _sigmoid(bin_)
            dain = dg * (-ea) * vm * _sigmoid(z)
            dba_ref[rows[c], 0:DN_HEADS] = dbeta * vm * sg * (1.0 - sg)
            dba_ref[rows[c], DN_HEADS:2 * DN_HEADS] = dain
            dal = dal + jnp.sum(dg * g, axis=0, keepdims=True)
            ddt = ddt + jnp.sum(dain, axis=0, keepdims=True)

        @pl.when(step == 0)
        def _():
            dal_ref[...] = dal
            ddt_ref[...] = ddt

        @pl.when(step > 0)
        def _():
            dal_ref[...] += dal
            ddt_ref[...] += ddt

    vec = pl.BlockSpec((1, DN_HEADS), lambda s: (0, 0))
    qs = pl.BlockSpec((cb * CHUNK, DN_QK), lambda s: (s, 0))
    ks = pl.BlockSpec((cb * CHUNK, DN_QK), lambda s: (s, 1))
    vs = pl.BlockSpec((cb * CHUNK, DN_V), lambda s: (s, 1))
    v0 = pl.BlockSpec((cb * CHUNK, DN_V), lambda s: (s, 0))
    st = pl.BlockSpec((cb, DN_HEADS, DN_DK, DN_DV), lambda s: (s, 0, 0, 0))
    return pl.pallas_call(
        body, name=name, grid=(nch // cb,),
        in_specs=[qs, ks, vs, pl.BlockSpec((cb * CHUNK, BA_W), lambda s: (s, 0)), vec, vec, st, st,
                  pl.BlockSpec((cb, DN_HEADS, CHUNK, CHUNK), lambda s: (s, 0, 0, 0)),
                  v0, pl.BlockSpec((cb, DN_HEADS, 2 * CHUNK, DN_DK), lambda s: (s, 0, 0, 0)), v0, v0, v0],
        out_specs=[pl.BlockSpec((cb * CHUNK, DN_CONV_CH), lambda s: (s, 0)),
                   pl.BlockSpec((cb * CHUNK, BA_W), lambda s: (s, 0)), vec, vec],
        out_shape=[jax.ShapeDtypeStruct((l, DN_CONV_CH), f32), jax.ShapeDtypeStruct((l, BA_W), f32),
                   jax.ShapeDtypeStruct((1, DN_HEADS), f32), jax.ShapeDtypeStruct((1, DN_HEADS), f32)],
        compiler_params=_params(("arbitrary",), 48),
    )(qkv, qkv, qkv, ba, a_log, dt_bias, states, dsp_all, tinv_all, u_all, wq, vn_all, do, dvn_all)


def _ffn_fwd(h, hn, wg, wu, wd, tb, th, tag, plan, next_norm_w=None):
    fh, d = wd.shape
    a, b, s = plan.call(f"{tag}_gu", functools.partial(_ffn_gu, tm=th // 2, tn=fh // 2), hn, wg, wu, n_out=3)
    out = plan.matmul(f"{tag}_down", s, wd, mode="nn", tm=th // 2, tn=d, tk=fh, res=h, norm_w=next_norm_w)
    return out, (hn, a, b, s)


def _ffn_bwd(dh, dhb, h, nw, wg, wu, wd, saved, tb, th, tag, plan):
    hn, a, b, s = saved
    d = h.shape[1]
    fh = wd.shape[0]
    layer = tag[-1]
    gr = plan.grads
    da, db = _ffn_ds(dhb, wd, a, b, tm=th // 2, tn=fh // 2, name=f"{tag}_b_ds")
    gr["down" + layer] = _matmul(s, dhb, mode="tn", tm=fh // 2, tn=d, tk=th, out_dtype=bf16, name=f"{tag}_b_dwd")
    dh2, dh2b, dnw = plan.call(f"{tag}_b_dhn", functools.partial(_dhn_norm_bwd, tm=th // 2, tk=fh // 2),
                               [(da, wg), (db, wu)], h, nw, dh, n_out=3)
    gr["gate" + layer] = _matmul(hn, da, mode="tn", tm=d, tn=fh // 2, tk=th, out_dtype=bf16, name=f"{tag}_b_dwg")
    gr["up" + layer] = _matmul(hn, db, mode="tn", tm=d, tn=fh // 2, tk=th, out_dtype=bf16, name=f"{tag}_b_dwu")
    return dh2, dh2b, dnw


class _Plan:
    GATHERS = {"ret_proj": ("ret_out", "gate0", "up0"), "ret_scan": ("down0", "dn_in"),
               "dn_conv": ("dn_out", "gate1", "up1", "down1")}
    SCATTERS = {"dn_b_conv_a": ("down1", "gate1", "up1", "dn_out"), "ffn0_b_dhn": ("dn_in",),
                "ret_b_scan": ("gate0", "up0"), "ret_b_dwin": ("down0", "ret_out"), "ret_b_dhn": ("ret_in",)}

    def __init__(self, shards, wts):
        self.shards, self.wts, self.grads, self.parts = shards, wts, {}, {}

    def _exchange(self, stage):
        if self.shards is None:
            return None
        if stage in self.GATHERS:
            return _Exchange([self.shards[n] for n in self.GATHERS[stage]], True)
        if stage in self.SCATTERS:
            return _Exchange([self._dev_major(n) for n in self.SCATTERS[stage]], False)
        return None

    def _dev_major(self, name):
        g = self.grads
        if name[:-1] in ("gate", "up"):
            return _dev_major_cols(g[name], g[name].shape[1] // N_DEV)
        if name[:-1] == "down":
            dwd = g[name]
            return dwd.reshape(N_DEV, dwd.shape[0] // N_DEV, dwd.shape[1])
        if name in ("ret_out", "dn_out"):
            return g[name].reshape(N_DEV, g[name].shape[0] // N_DEV, g[name].shape[1])
        return _dev_major_cols(g[name], self.shards[name].shape[-1])

    def _landed(self, stage, outs):
        if stage in self.SCATTERS:
            self.parts.update(zip(self.SCATTERS[stage], outs))
            return
        w = self.wts
        cols = lambda t: t.transpose(1, 0, 2).reshape(t.shape[1], N_DEV * t.shape[2])
        rows = lambda t: t.reshape(N_DEV * t.shape[1], t.shape[2])
        for name, t in zip(self.GATHERS[stage], outs):
            if name in ("ret_out", "dn_out") or name.startswith("down"):
                w[name] = rows(t)
            elif name == "dn_in":
                full = cols(t)
                n_main = DN_CONV_CH + DN_V
                w["dn_main"] = full[:, :n_main]
                w["dn_ba"] = jnp.pad(full[:, n_main:], ((0, 0), (0, BA_W - (full.shape[1] - n_main))))
            else:
                w[name] = cols(t)

    def matmul(self, stage, a, b, **kw):
        comm = self._exchange(stage)
        if comm is None:
            return _matmul(a, b, name=stage, **kw)
        out, landed = _matmul(a, b, name=stage, comm=comm, **kw)
        self._landed(stage, landed)
        return out

    def call(self, stage, fn, *args, n_out):
        comm = self._exchange(stage)
        out = fn(*args, stage, comm=comm)
        if comm is not None:
            self._landed(stage, out[n_out:])
        return out[:n_out]


def _local_step(x2, target, wts, shards=None):
    plan = _Plan(shards, wts)
    s_len, d = x2.shape
    l = s_len + CHUNK
    tb = _tile(l, 3072)
    th = tb // 2 if (tb // 2) % 16 == 0 else tb
    half = RET_DK // 2
    inv_freq = (np.float32(ROPE_BASE) ** (-np.arange(half, dtype=np.float32) / np.float32(half))).astype(np.float32)
    ang = (np.arange(l) - PAD).astype(np.float32)[:, None] * inv_freq[None, :]
    cos, sin = jnp.asarray(np.cos(ang), f32), jnp.asarray(np.sin(ang), f32)
    lgs = jnp.log1p(-jnp.exp2(-5.0 - jnp.arange(RET_HEADS, dtype=f32)))
    gcs = jnp.exp(lgs * _ret_block(l))

    mixw, ffnw = wts["mix_norm"], wts["ffn_norm"]

    h0, hn0 = _embed_norm(x2, wts["meta"], mixw[0:1], "l0_norm")
    proj0 = plan.matmul("ret_proj", hn0, wts["ret_in"], mode="nn", tm=tb, tn=512, tk=d)
    (qk0,) = plan.call("ret_prep", _ret_prep, proj0, cos, sin, n_out=1)
    o0, st0, y0 = plan.call("ret_scan", _ret_scan_fwd, qk0, proj0, wts["ret_gn"], lgs, gcs, n_out=3)
    h1, hn1 = _matmul(y0, wts["ret_out"], mode="nn", tm=th // 2, tn=d, tk=RET_V, res=h0, norm_w=ffnw[0:1],
                      name="ret_out")
    (h2, hn2), ffn0 = _ffn_fwd(h1, hn1, wts["gate0"], wts["up0"], wts["down0"], tb, th, "ffn0", plan,
                               next_norm_w=mixw[1:2])

    proj1 = plan.matmul("dn_proj", hn2, wts["dn_main"], mode="nn", tm=tb, tn=512, tk=d)
    ba = _matmul(hn2, wts["dn_ba"], mode="nn", tm=tb, tn=BA_W, tk=d, name="dn_proj_ba")
    (qkv1,) = plan.call("dn_conv", _dn_conv_fwd, proj1, wts["conv_w"], n_out=1)
    tinv1, u1, wq1, pk1, egl1, kpt1, qwt1 = plan.call("dn_prep", _dn_prep, qkv1, ba, wts["a_log"], wts["dt_bias"],
                                                      n_out=7)
    o1, st1, vn1 = _dn_scan_fwd(u1, wq1, pk1, egl1, "dn_scan")
    y1 = _gnorm_fwd(o1, proj1, wts["dn_norm"], DN_HEADS, DN_DV, 2, "dn_gnorm")
    h3, hn3 = _matmul(y1, wts["dn_out"], mode="nn", tm=th // 2, tn=d, tk=DN_V, res=h2, norm_w=ffnw[1:2],
                      name="dn_out")
    h4, ffn1 = _ffn_fwd(h3, hn3, wts["gate1"], wts["up1"], wts["down1"], tb, th, "ffn1", plan)

    dh4, dh4b, dfinal, loss = _final_loss(h4, wts["final_norm"], target, "final_loss")
    gr = plan.grads
    dh3, dh3b, dffn1 = _ffn_bwd(dh4, dh4b, h3, ffnw[1:2], wts["gate1"], wts["up1"], wts["down1"], ffn1,
                                tb, th, "ffn1", plan)

    gr["dn_out"] = _matmul(y1, dh3b, mode="tn", tm=1024, tn=d, tk=tb, out_dtype=bf16, name="dn_b_dwout")
    do1, dproj1, ddn_norm = _dy_gnorm_bwd(dh3b, wts["dn_out"], o1, proj1, wts["dn_norm"], DN_DV, 2, "dn_b_gnorm",
                                          tm=th // 2, tn=1024)
    dvn1, dsp1 = _dn_scan_bwd(do1, kpt1, qwt1, egl1, "dn_b_scan")
    dqkv1, dba, dalog, ddt = _dn_post_bwd(qkv1, ba, wts["a_log"], wts["dt_bias"], st1, dsp1, tinv1, u1, wq1, vn1,
                                          do1, dvn1, "dn_b_post")
    dc1, dconv = plan.call("dn_b_conv_a", _dn_conv_bwd_a, proj1, wts["conv_w"], dqkv1, n_out=2)
    dproj1 = _dn_conv_bwd_b(dc1, wts["conv_w"], dproj1, "dn_b_conv_b")
    dbab = dba.astype(bf16)
    n_main = dproj1.shape[1]
    dhn2_ba = _matmul(dbab, wts["dn_ba"], mode="nt", tm=th, tn=d, tk=BA_W, name="dn_b_dhn_ba")
    dh2, dh2b, dmix1 = plan.call("dn_b_dhn", functools.partial(_dhn_norm_bwd, tm=th // 2, tk=n_main // 4,
                                                              init=dhn2_ba),
                                 [(dproj1, wts["dn_main"])], h2, mixw[1:2], dh3, n_out=3)
    dw_main = _matmul(hn2, dproj1, mode="tn", tm=d, tn=512, tk=tb, out_dtype=bf16, name="dn_b_dwin")
    dw_ba = _matmul(hn2, dbab, mode="tn", tm=d, tn=BA_W, tk=tb, out_dtype=bf16, name="dn_b_dwin_ba")
    gr["dn_in"] = jnp.concatenate([dw_main, dw_ba], axis=1)

    dh1, dh1b, dffn0 = _ffn_bwd(dh2, dh2b, h1, ffnw[0:1], wts["gate0"], wts["up0"], wts["down0"], ffn0,
                                tb, th, "ffn0", plan)

    gr["ret_out"] = _matmul(y0, dh1b, mode="tn", tm=1024, tn=d, tk=tb, out_dtype=bf16, name="ret_b_dwout")
    do0, dproj0, dret_gn = _dy_gnorm_bwd(dh1b, wts["ret_out"], o0, proj0, wts["ret_gn"], RET_DV, 2, "ret_b_gnorm",
                                         tm=th // 2, tn=1024)
    dq0, dk0, dproj0 = plan.call("ret_b_scan", _ret_scan_bwd, qk0, proj0, st0, do0, dproj0, lgs, gcs, n_out=3)
    dproj0 = _ret_prep_bwd(dq0, dk0, cos, sin, dproj0, "ret_b_prep")
    n_in = dproj0.shape[1]
    gr["ret_in"] = plan.matmul("ret_b_dwin", hn0, dproj0, mode="tn", tm=d, tn=512, tk=tb, out_dtype=bf16)
    dh0, _, dmix0 = plan.call("ret_b_dhn", functools.partial(_dhn_norm_bwd, tm=th // 2, tk=n_in // 4),
                              [(dproj0, wts["ret_in"])], h0, mixw[0:1], dh1, n_out=3)

    gr.update(meta=dh0[PAD:CHUNK], mix_norm=jnp.concatenate([dmix0, dmix1], axis=0),
              ffn_norm=jnp.concatenate([dffn0, dffn1], axis=0), ret_gn=dret_gn, conv_w=dconv, a_log=dalog,
              dt_bias=ddt, dn_norm=ddn_norm, final_norm=dfinal)
    return loss, dh0[CHUNK:], gr, plan


def _adamw_reduce(parts, w, m, v, name):
    _, r, c = parts.shape
    c_pad = -(-c // LANES) * LANES
    tr = _div_tile(r, max(8, (3 * MIB // 16) // c_pad // 8 * 8), 16)

    def body(p_ref, w_ref, m_ref, v_ref, g_ref, d_ref, nm_ref, nv_ref):
        g = p_ref[0].astype(f32)
        for s in range(1, N_DEV):
            g = g + p_ref[s].astype(f32)
        mm = ADAM_B1 * m_ref[...] + (1.0 - ADAM_B1) * g
        vv = ADAM_B2 * v_ref[...] + (1.0 - ADAM_B2) * (g * g)
        m_hat = mm / (1.0 - ADAM_B1 ** ADAM_STEP)
        v_hat = vv / (1.0 - ADAM_B2 ** ADAM_STEP)
        g_ref[...] = g
        d_ref[...] = -ADAM_LR * (m_hat / (jnp.sqrt(v_hat) + ADAM_EPS) + ADAM_WD * w_ref[...])
        nm_ref[...] = mm
        nv_ref[...] = vv

    blk = pl.BlockSpec((tr, c), lambda i: (i, 0))
    return pl.pallas_call(
        body, name=name, grid=(r // tr,),
        in_specs=[pl.BlockSpec((N_DEV, tr, c), lambda i: (0, i, 0)), blk, blk, blk], out_specs=[blk] * 4,
        out_shape=[jax.ShapeDtypeStruct((r, c), f32)] * 4,
        compiler_params=_params(("parallel",), 48),
    )(parts, w, m, v)


def _dev_major_cols(g, width):
    r = g.shape[0]
    return g[:, :N_DEV * width].reshape(r, N_DEV, width).transpose(1, 0, 2)


def kernel(x, meta_tokens, mix_norm_w, ffn_norm_w, ret_w_in, ret_gn_w, ret_w_out, dn_w_in, dn_conv_w, dn_a_log, dn_dt_bias, dn_norm_w, dn_w_out, ffn_w_gate, ffn_w_up, ffn_w_down, final_norm_w, loss_target, m_meta_tokens, m_mix_norm_w, m_ffn_norm_w, m_ret_w_in, m_ret_gn_w, m_ret_w_out, m_dn_w_in, m_dn_conv_w, m_dn_a_log, m_dn_dt_bias, m_dn_norm_w, m_dn_w_out, m_ffn_w_gate, m_ffn_w_up, m_ffn_w_down, m_final_norm_w, v_meta_tokens, v_mix_norm_w, v_ffn_norm_w, v_ret_w_in, v_ret_gn_w, v_ret_w_out, v_dn_w_in, v_dn_conv_w, v_dn_a_log, v_dn_dt_bias, v_dn_norm_w, v_dn_w_out, v_ffn_w_gate, v_ffn_w_up, v_ffn_w_down, v_final_norm_w):
    d = x.shape[-1]
    me = 4 * lax.axis_index("x") + 2 * lax.axis_index("y") + lax.axis_index("c")

    shards = dict(ret_in=ret_w_in[0].astype(bf16), ret_out=ret_w_out[0].astype(bf16),
                  dn_in=dn_w_in[0].astype(bf16), dn_out=dn_w_out[0].astype(bf16))
    for layer in (0, 1):
        shards[f"gate{layer}"] = ffn_w_gate[layer].astype(bf16)
        shards[f"up{layer}"] = ffn_w_up[layer].astype(bf16)
        shards[f"down{layer}"] = ffn_w_down[layer].astype(bf16)
    g_ret_in, g_meta, g_conv, g_dnn = _exchange([shards["ret_in"], meta_tokens, dn_conv_w[0], dn_norm_w], True,
                                                "gather_first")
    cols = lambda g: g.transpose(1, 0, 2).reshape(g.shape[1], N_DEV * g.shape[2])
    wts = dict(meta=cols(g_meta), mix_norm=mix_norm_w, ffn_norm=ffn_norm_w, ret_in=cols(g_ret_in), ret_gn=ret_gn_w,
               conv_w=cols(g_conv), a_log=dn_a_log, dt_bias=dn_dt_bias, dn_norm=cols(g_dnn),
               final_norm=final_norm_w.reshape(1, d))

    loss_part, grad_x, gr, plan = _local_step(x[0], loss_target[0], wts, shards)
    loss = lax.psum(loss_part[0, 0], AXES)

    pp = plan.parts
    both = lambda name: jnp.concatenate([pp[name + "0"], pp[name + "1"]], axis=1)
    big_parts = [pp["ret_in"], pp["ret_out"], pp["dn_in"], pp["dn_out"], both("gate"), both("up"), both("down")]
    big_names = ["ret_w_in", "ret_w_out", "dn_w_in", "dn_w_out", "ffn_w_gate", "ffn_w_up", "ffn_w_down"]
    big_w = [ret_w_in, ret_w_out, dn_w_in, dn_w_out, ffn_w_gate, ffn_w_up, ffn_w_down]
    big_m = [m_ret_w_in, m_ret_w_out, m_dn_w_in, m_dn_w_out, m_ffn_w_gate, m_ffn_w_up, m_ffn_w_down]
    big_v = [v_ret_w_in, v_ret_w_out, v_dn_w_in, v_dn_w_out, v_ffn_w_gate, v_ffn_w_up, v_ffn_w_down]
    res = {}
    for nm, parts, w_, m_, v_ in zip(big_names, big_parts, big_w, big_m, big_v):
        r2, c2 = parts.shape[1], parts.shape[2]
        outs = _adamw_reduce(parts, w_.reshape(r2, c2), m_.reshape(r2, c2), v_.reshape(r2, c2), f"adamw_{nm}")
        res[nm] = [o.reshape(w_.shape) for o in outs]

    small_names = ["meta_tokens", "mix_norm_w", "ffn_norm_w", "ret_gn_w", "dn_conv_w", "dn_a_log", "dn_dt_bias",
                   "dn_norm_w", "final_norm_w"]
    small_g = [gr["meta"], gr["mix_norm"], gr["ffn_norm"], gr["ret_gn"], gr["conv_w"], gr["a_log"], gr["dt_bias"],
               gr["dn_norm"], gr["final_norm"]]
    small_w = [meta_tokens, mix_norm_w, ffn_norm_w, ret_gn_w, dn_conv_w, dn_a_log, dn_dt_bias, dn_norm_w, final_norm_w]
    small_m = [m_meta_tokens, m_mix_norm_w, m_ffn_norm_w, m_ret_gn_w, m_dn_conv_w, m_dn_a_log, m_dn_dt_bias,
               m_dn_norm_w, m_final_norm_w]
    small_v = [v_meta_tokens, v_mix_norm_w, v_ffn_norm_w, v_ret_gn_w, v_dn_conv_w, v_dn_a_log, v_dn_dt_bias,
               v_dn_norm_w, v_final_norm_w]
    sharded = {"meta_tokens", "dn_conv_w", "dn_norm_w"}
    flat = jnp.concatenate([g.reshape(-1) for g in small_g])
    row = 8 * LANES
    n_flat = flat.shape[0]
    flat = jnp.pad(flat, (0, -n_flat % row)).reshape(-1, row)
    (gathered,) = _exchange([flat], True, "gather_small_grads")
    gathered = gathered.reshape(N_DEV, -1)
    pieces, off = [], 0
    for nm, g, w_ in zip(small_names, small_g, small_w):
        full = gathered[:, off:off + g.size].reshape((N_DEV,) + g.shape)
        off += g.size
        if nm in sharded:
            wloc = w_.shape[-1]
            full = lax.dynamic_slice_in_dim(full, me * wloc, wloc, axis=full.ndim - 1)
        pieces.append(full.reshape(N_DEV, -1))
    sizes = [p.shape[1] for p in pieces]
    n_loc = sum(sizes)
    pad_loc = -n_loc % row

    def pack(vs, lead):
        cat = jnp.concatenate([a.reshape(lead + (-1,)) for a in vs], axis=-1)
        cat = jnp.pad(cat, [(0, 0)] * len(lead) + [(0, pad_loc)])
        return cat.reshape(lead + (-1, row))

    outs = _adamw_reduce(pack(pieces, (N_DEV,)), pack(small_w, ()), pack(small_m, ()), pack(small_v, ()), "adamw_small")
    off = 0
    for nm, sz, w_ in zip(small_names, sizes, small_w):
        res[nm] = [o.reshape(-1)[off:off + sz].reshape(w_.shape) for o in outs]
        off += sz

    order = ["meta_tokens", "mix_norm_w", "ffn_norm_w", "ret_w_in", "ret_gn_w", "ret_w_out", "dn_w_in", "dn_conv_w",
             "dn_a_log", "dn_dt_bias", "dn_norm_w", "dn_w_out", "ffn_w_gate", "ffn_w_up", "ffn_w_down", "final_norm_w"]
    grad_x = grad_x.reshape(x.shape)
    return (loss, grad_x, *[res[nm][0] for nm in order], *[res[nm][1] for nm in order],
            *[res[nm][2] for nm in order], *[res[nm][3] for nm in order])
```

```python
import functools
import math

import jax
import jax.numpy as jnp
import numpy as np
from jax import lax
from jax.experimental import pallas as pl
from jax.experimental.pallas import tpu as pltpu

f32 = jnp.float32
bf16 = jnp.bfloat16

N_META = 16
CHUNK = 64
PAD = CHUNK - N_META
RMS_EPS = 1e-6
RET_HEADS, RET_DK, RET_DV = 4, 256, 512
RET_QK, RET_V = RET_HEADS * RET_DK, RET_HEADS * RET_DV
DN_HEADS, DN_DK, DN_DV = 8, 128, 256
DN_QK, DN_V = DN_HEADS * DN_DK, DN_HEADS * DN_DV
DN_CONV_CH = 2 * DN_QK + DN_V
CONV_K = 4
ROPE_BASE = 10000.0
ADAM_LR, ADAM_B1, ADAM_B2, ADAM_EPS, ADAM_WD, ADAM_STEP = 0.001, 0.9, 0.999, 1e-08, 0.01, 10
N_DEV = 8
AXES = ("x", "y", "c")
LANES = 128
MIB = 1024 * 1024


def _tile(n_rows, cap):
    nch = n_rows // CHUNK
    best = 1
    for d in range(1, nch + 1):
        if nch % d == 0 and d * CHUNK <= cap:
            best = d
    return best * CHUNK


def _div_tile(n, cap, align):
    best = None
    for d in range(align, min(n, cap) + 1, align):
        if n % d == 0:
            best = d
    return best if best is not None else n


def _params(sem, vmem_mb):
    return pltpu.CompilerParams(dimension_semantics=sem, vmem_limit_bytes=int(vmem_mb * MIB))


def _nn(a, b, precision=None):
    return jnp.dot(a, b, preferred_element_type=f32, precision=precision)


def _nt(a, b, precision=None):
    return lax.dot_general(a, b, (((1,), (1,)), ((), ())), preferred_element_type=f32, precision=precision)


def _tn(a, b, precision=None):
    return lax.dot_general(a, b, (((0,), (0,)), ((), ())), preferred_element_type=f32, precision=precision)


def _b(x):
    return x.astype(bf16)


def _sigmoid(x):
    return 0.5 * jnp.tanh(0.5 * x) + 0.5


def _silu(x):
    return x * _sigmoid(x)


def _dsilu(x):
    s = _sigmoid(x)
    return s * (1.0 + x * (1.0 - s))


def _peer(k):
    x, y, c = lax.axis_index("x"), lax.axis_index("y"), lax.axis_index("c")
    px = 1 - x if k & 4 else x
    py = 1 - y if k & 2 else y
    pc = 1 - c if k & 1 else c
    return (px, py, pc), 4 * px + 2 * py + pc


class _Exchange:
    def __init__(self, arrs, gather):
        self.arrs, self.gather, self.n = list(arrs), gather, len(arrs)
        self.out_shapes = [jax.ShapeDtypeStruct(((N_DEV,) + a.shape) if gather else a.shape, a.dtype) for a in arrs]
        self.specs = [pl.BlockSpec(memory_space=pltpu.HBM)] * self.n
        self.scratch = [pltpu.SemaphoreType.DMA((self.n, N_DEV - 1)), pltpu.SemaphoreType.DMA((self.n, N_DEV - 1)),
                        pltpu.SemaphoreType.DMA((self.n,))]

    def _copies(self, ins, outs, sems):
        send_sems, recv_sems, local_sems = sems
        x, y, c = lax.axis_index("x"), lax.axis_index("y"), lax.axis_index("c")
        me = 4 * x + 2 * y + c

        def copy(a, k, src, slot, to):
            return pltpu.make_async_remote_copy(
                src_ref=src, dst_ref=outs[a].at[slot], send_sem=send_sems.at[a, k], recv_sem=recv_sems.at[a, k],
                device_id=to, device_id_type=pl.DeviceIdType.MESH)

        first, passed, lands_first, lands_rest = [], [], [], []
        if not self.gather:
            local = [pltpu.make_async_copy(ins[a].at[me], outs[a].at[me], local_sems.at[a]) for a in range(self.n)]
            for k in range(1, N_DEV):
                peer, pidx = _peer(k)
                for a in range(self.n):
                    first.append(copy(a, k - 1, ins[a].at[pidx], me, peer))
                    lands_rest.append(copy(a, k - 1, ins[a].at[pidx], pidx, peer))
            return local, first, passed, lands_first, lands_rest
        local = [pltpu.make_async_copy(ins[a], outs[a].at[me], local_sems.at[a]) for a in range(self.n)]
        sibling, sibling_slot = (x, y, 1 - c), 4 * x + 2 * y + (1 - c)
        chips = [(1 - x, y), (x, 1 - y), (1 - x, 1 - y)]
        for a in range(self.n):
            first.append(copy(a, 0, ins[a], me, sibling))
            lands_rest.append(copy(a, 0, ins[a], sibling_slot, sibling))
            for j, (px, py) in enumerate(chips):
                slot, slot_other = 4 * px + 2 * py + c, 4 * px + 2 * py + (1 - c)
                first.append(copy(a, 1 + j, ins[a], me, (px, py, c)))
                lands_first.append(copy(a, 1 + j, ins[a], slot, (px, py, c)))
                passed.append(copy(a, 4 + j, outs[a].at[slot], slot, sibling))
                lands_rest.append(copy(a, 4 + j, outs[a].at[slot_other], slot_other, sibling))
        return local, first, passed, lands_first, lands_rest

    def start(self, ins, outs, sems):
        local, first, _, _, _ = self._copies(ins, outs, sems)
        for cp in local + first:
            cp.start()

    def wait(self, ins, outs, sems):
        local, first, passed, lands_first, lands_rest = self._copies(ins, outs, sems)
        for landed, onward in zip(lands_first, passed):
            landed.wait_recv()
            onward.start()
        for cp in lands_rest:
            cp.wait_recv()
        for cp in first + passed:
            cp.wait_send()
        for cp in local:
            cp.wait()


def _call(body, args, *, name, grid, in_specs, out_specs, out_shape, scratch=(), sem, vmem_mb, comm=None,
          aliases=None):
    aliases = aliases or {}
    if comm is None:
        out = pl.pallas_call(body, name=name, grid=grid, in_specs=list(in_specs), out_specs=list(out_specs),
                             out_shape=list(out_shape), scratch_shapes=list(scratch), input_output_aliases=aliases,
                             compiler_params=_params(sem, vmem_mb))(*args)
        return list(out)
    n_in, n_out, n_scr, nc = len(args), len(out_shape), len(scratch), comm.n

    def carried(*refs):
        ins, cin = refs[:n_in], refs[n_in:n_in + nc]
        o0 = n_in + nc
        outs, cout = refs[o0:o0 + n_out], refs[o0 + n_out:o0 + n_out + nc]
        s0 = o0 + n_out + nc
        scr, sems = refs[s0:s0 + n_scr], refs[s0 + n_scr:]
        first = functools.reduce(jnp.logical_and, [pl.program_id(i) == 0 for i in range(len(grid))])
        last = functools.reduce(jnp.logical_and, [pl.program_id(i) == grid[i] - 1 for i in range(len(grid))])

        @pl.when(first)
        def _():
            comm.start(cin, cout, sems)

        body(*ins, *outs, *scr)

        @pl.when(last)
        def _():
            comm.wait(cin, cout, sems)

    out = pl.pallas_call(
        carried, name=name, grid=grid, in_specs=list(in_specs) + comm.specs, out_specs=list(out_specs) + comm.specs,
        out_shape=list(out_shape) + comm.out_shapes, scratch_shapes=list(scratch) + comm.scratch,
        input_output_aliases=aliases,
        compiler_params=_params(("arbitrary",) * len(grid), vmem_mb))(*args, *comm.arrs)
    return list(out)


def _exchange(arrs, gather, name):
    comm = _Exchange(arrs, gather)

    def body(*refs):
        ins, outs, sems = refs[:comm.n], refs[comm.n:2 * comm.n], refs[2 * comm.n:]
        comm.start(ins, outs, sems)
        comm.wait(ins, outs, sems)

    return pl.pallas_call(body, name=name, in_specs=comm.specs, out_specs=comm.specs, out_shape=comm.out_shapes,
                          scratch_shapes=comm.scratch)(*comm.arrs)


def _matmul(a, b, *, mode, tm, tn, tk, name, out_dtype=f32, res=None, vmem_mb=48, comm=None, norm_w=None):
    if mode == "nn":
        (m, k), (k2, n) = a.shape, b.shape
    elif mode == "nt":
        (m, k), (n, k2) = a.shape, b.shape
    else:
        (k, m), (k2, n) = a.shape, b.shape
    assert k == k2 and m % tm == 0 and n % tn == 0 and k % tk == 0, (name, a.shape, b.shape, tm, tn, tk)
    nk = k // tk
    dot = {"nn": _nn, "nt": _nt, "tn": _tn}[mode]
    a_spec = {"nn": pl.BlockSpec((tm, tk), lambda i, j, kk: (i, kk)),
              "nt": pl.BlockSpec((tm, tk), lambda i, j, kk: (i, kk)),
              "tn": pl.BlockSpec((tk, tm), lambda i, j, kk: (kk, i))}[mode]
    b_spec = {"nn": pl.BlockSpec((tk, tn), lambda i, j, kk: (kk, j)),
              "nt": pl.BlockSpec((tn, tk), lambda i, j, kk: (j, kk)),
              "tn": pl.BlockSpec((tk, tn), lambda i, j, kk: (kk, j))}[mode]
    o_spec = pl.BlockSpec((tm, tn), lambda i, j, kk: (i, j))
    has_res = res is not None
    has_norm = norm_w is not None
    assert not has_norm or tn == n
    n_ops = 2 + has_res + has_norm

    def body(*refs):
        a_ref, b_ref = refs[:2]
        r_ref = refs[2] if has_res else None
        nw_ref = refs[2 + has_res] if has_norm else None
        o_ref = refs[n_ops]
        hn_ref = refs[n_ops + 1] if has_norm else None
        rest = refs[n_ops + 1 + has_norm:]

        def finish(tot):
            if has_res:
                tot = tot + r_ref[...]
            o_ref[...] = tot.astype(out_dtype)
            if has_norm:
                r = lax.rsqrt(jnp.mean(tot * tot, axis=-1, keepdims=True) + RMS_EPS)
                hn_ref[...] = _b(tot * r * nw_ref[...])

        if nk == 1:
            finish(dot(_b(a_ref[...]), _b(b_ref[...])))
            return
        acc_ref = rest[0]
        kk = pl.program_id(2)

        @pl.when(kk == 0)
        def _():
            acc_ref[...] = dot(_b(a_ref[...]), _b(b_ref[...]))

        @pl.when(kk > 0)
        def _():
            acc_ref[...] += dot(_b(a_ref[...]), _b(b_ref[...]))

        @pl.when(kk == nk - 1)
        def _():
            finish(acc_ref[...])

    in_specs = [a_spec, b_spec]
    args = (a, b)
    if has_res:
        in_specs.append(o_spec)
        args += (res,)
    out_specs, out_shape = [o_spec], [jax.ShapeDtypeStruct((m, n), out_dtype)]
    if has_norm:
        in_specs.append(pl.BlockSpec((1, tn), lambda i, j, kk: (0, j)))
        args += (norm_w,)
        out_specs.append(o_spec)
        out_shape.append(jax.ShapeDtypeStruct((m, n), bf16))
    out = _call(body, args, name=name, grid=(m // tm, n // tn, nk), in_specs=in_specs, out_specs=out_specs,
                out_shape=out_shape, scratch=[pltpu.VMEM((tm, tn), f32)] if nk > 1 else [],
                sem=("parallel", "parallel", "arbitrary"), vmem_mb=vmem_mb, comm=comm)
    n_own = len(out_shape)
    own = out[0] if n_own == 1 else tuple(out[:n_own])
    return own if comm is None else (own, out[n_own:])


def _embed_norm(x2, meta, w, name):
    s_len, d = x2.shape
    l = s_len + CHUNK
    cpt = _tile(l, 256) // CHUNK

    def body(m_ref, w_ref, *rest):
        x_refs, (h_ref, hn_ref) = rest[:cpt], rest[cpt:]
        i = pl.program_id(0)
        prefix = jnp.concatenate([jnp.zeros((PAD, d), f32), m_ref[...]], axis=0)
        wv = w_ref[...]
        for c in range(cpt):
            rows = slice(c * CHUNK, (c + 1) * CHUNK)
            x = jnp.where(i * cpt + c > 0, x_refs[c][...], prefix)
            h_ref[rows, :] = x
            r = lax.rsqrt(jnp.mean(x * x, axis=-1, keepdims=True) + RMS_EPS)
            hn_ref[rows, :] = _b(x * r * wv)

    row = pl.BlockSpec((cpt * CHUNK, d), lambda i: (i, 0))
    x_specs = [pl.BlockSpec((CHUNK, d), functools.partial(lambda i, c: (jnp.maximum(i * cpt + c - 1, 0), 0), c=c))
               for c in range(cpt)]
    return pl.pallas_call(
        body, name=name, grid=(l // (cpt * CHUNK),),
        in_specs=[pl.BlockSpec((N_META, d), lambda i: (0, 0)), pl.BlockSpec((1, d), lambda i: (0, 0))] + x_specs,
        out_specs=[row, row],
        out_shape=[jax.ShapeDtypeStruct((l, d), f32), jax.ShapeDtypeStruct((l, d), bf16)],
        compiler_params=_params(("parallel",), 32),
    )(meta, w, *([x2] * cpt))


def _dhn_norm_bwd(pairs, h, nw, dres, name, *, tm, tk, init=None, comm=None):
    l, d = h.shape
    nks = [a.shape[1] // tk for a, _ in pairs]
    starts = [sum(nks[:p]) for p in range(len(pairs))]
    nk = sum(nks)
    assert nk >= 2
    n_ops = 2 * len(pairs)
    has_init = init is not None

    def body(*refs):
        ops = refs[:n_ops]
        init_ref = refs[n_ops] if has_init else None
        h_ref, w_ref, r_ref, dh_ref, dhb_ref, dw_ref, acc_ref = refs[n_ops + has_init:]
        i, kk = pl.program_id(0), pl.program_id(1)

        @pl.when(kk == 0)
        def _():
            part = _nt(ops[0][...], ops[1][...])
            acc_ref[...] = part + init_ref[...] if has_init else part

        for p in range(len(pairs)):
            lo, hi = max(starts[p], 1), min(starts[p] + nks[p], nk - 1)

            @pl.when(jnp.logical_and(kk >= lo, kk < hi))
            def _(a_ref=ops[2 * p], b_ref=ops[2 * p + 1]):
                acc_ref[...] += _nt(a_ref[...], b_ref[...])

        @pl.when(kk == nk - 1)
        def _():
            g = acc_ref[...] + _nt(ops[-2][...], ops[-1][...])
            x = h_ref[...]
            r = lax.rsqrt(jnp.mean(x * x, axis=-1, keepdims=True) + RMS_EPS)
            xh = x * r
            dxh = g * w_ref[...]
            dh = r_ref[...] + r * (dxh - xh * jnp.mean(dxh * xh, axis=-1, keepdims=True))
            dh_ref[...] = dh
            dhb_ref[...] = _b(dh)
            dw = jnp.sum(g * xh, axis=0, keepdims=True)

            @pl.when(i == 0)
            def _():
                dw_ref[...] = dw

            @pl.when(i > 0)
            def _():
                dw_ref[...] += dw

    def k_of(p):
        return lambda kk: jnp.clip(kk - starts[p], 0, nks[p] - 1)

    in_specs, args = [], []
    for p, (a, b) in enumerate(pairs):
        in_specs += [pl.BlockSpec((tm, tk), functools.partial(lambda i, kk, f: (i, f(kk)), f=k_of(p))),
                     pl.BlockSpec((d, tk), functools.partial(lambda i, kk, f: (0, f(kk)), f=k_of(p)))]
        args += [a, b]
    row = pl.BlockSpec((tm, d), lambda i, kk: (i, 0))
    vec = pl.BlockSpec((1, d), lambda i, kk: (0, 0))
    if has_init:
        in_specs.append(row)
        args.append(init)
    return _call(body, tuple(args) + (h, nw, dres), name=name, grid=(l // tm, nk), in_specs=in_specs + [row, vec, row],
                 out_specs=[row, row, vec],
                 out_shape=[jax.ShapeDtypeStruct((l, d), f32), jax.ShapeDtypeStruct((l, d), bf16),
                            jax.ShapeDtypeStruct((1, d), f32)],
                 scratch=[pltpu.VMEM((tm, d), f32)], sem=("arbitrary", "arbitrary"), vmem_mb=48, comm=comm)


def _final_loss(h, w, target, name):
    l, d = h.shape
    nch = l // CHUNK
    cpt = _tile(l, 256) // CHUNK
    nt = nch // cpt

    def body(h_ref, w_ref, *rest):
        t_refs, (dh_ref, dhb_ref, dw_ref, loss_ref) = rest[:cpt], rest[cpt:]
        i = pl.program_id(0)
        wv = w_ref[...]
        dw = jnp.zeros((1, d), f32)
        part = jnp.zeros((1, 1), f32)
        for c in range(cpt):
            rows = slice(c * CHUNK, (c + 1) * CHUNK)
            live = (i * cpt + c > 0).astype(f32)
            x = h_ref[rows, :]
            r = lax.rsqrt(jnp.mean(x * x, axis=-1, keepdims=True) + RMS_EPS)
            xh = x * r
            err = (xh * wv - t_refs[c][...]) * live
            dy = err * (1.0 / d)
            dxh = dy * wv
            dx = r * (dxh - xh * jnp.mean(dxh * xh, axis=-1, keepdims=True))
            dh_ref[rows, :] = dx
            dhb_ref[rows, :] = _b(dx)
            dw = dw + jnp.sum(dy * xh, axis=0, keepdims=True)
            part = part + 0.5 * jnp.sum(jnp.sum(err * err, axis=-1, keepdims=True) * (1.0 / d), axis=0, keepdims=True)
        part = jnp.broadcast_to(part, (1, LANES))

        @pl.when(i == 0)
        def _():
            dw_ref[...] = dw
            loss_ref[...] = part

        @pl.when(i > 0)
        def _():
            dw_ref[...] += dw
            loss_ref[...] += part

    row = pl.BlockSpec((cpt * CHUNK, d), lambda i: (i, 0))
    vec = pl.BlockSpec((1, d), lambda i: (0, 0))
    t_specs = [pl.BlockSpec((CHUNK, d), functools.partial(lambda i, c: (jnp.maximum(i * cpt + c - 1, 0), 0), c=c))
               for c in range(cpt)]
    return pl.pallas_call(
        body, name=name, grid=(nt,),
        in_specs=[row, vec] + t_specs,
        out_specs=[row, row, vec, pl.BlockSpec((1, LANES), lambda i: (0, 0))],
        out_shape=[jax.ShapeDtypeStruct((l, d), f32), jax.ShapeDtypeStruct((l, d), bf16),
                   jax.ShapeDtypeStruct((1, d), f32), jax.ShapeDtypeStruct((1, LANES), f32)],
        compiler_params=_params(("arbitrary",), 32),
    )(h, w, *([target] * cpt))


def _ffn_gu(hn, wg, wu, name, *, tm, tn, comm=None):
    l, d = hn.shape
    fh = wg.shape[1]

    def body(h_ref, g_ref, u_ref, a_ref, b_ref, s_ref):
        hb = h_ref[...]
        a = _nn(hb, g_ref[...])
        bb = _nn(hb, u_ref[...])
        a_ref[...] = _b(a)
        b_ref[...] = _b(bb)
        s_ref[...] = _b(_silu(a) * bb)

    wspec = pl.BlockSpec((d, tn), lambda i, j: (0, j))
    ospec = pl.BlockSpec((tm, tn), lambda i, j: (i, j))
    return _call(body, (hn, wg, wu), name=name, grid=(l // tm, fh // tn),
                 in_specs=[pl.BlockSpec((tm, d), lambda i, j: (i, 0)), wspec, wspec], out_specs=[ospec] * 3,
                 out_shape=[jax.ShapeDtypeStruct((l, fh), bf16)] * 3, sem=("parallel", "parallel"), vmem_mb=48,
                 comm=comm)


def _ffn_ds(dhb, wd, a, b, *, tm, tn, name):
    l, d = dhb.shape
    fh = wd.shape[0]

    def body(g_ref, w_ref, a_ref, b_ref, da_ref, db_ref):
        ds = _nt(g_ref[...], w_ref[...])
        a = a_ref[...].astype(f32)
        da_ref[...] = _b(ds * b_ref[...].astype(f32) * _dsilu(a))
        db_ref[...] = _b(ds * _silu(a))

    ospec = pl.BlockSpec((tm, tn), lambda i, j: (i, j))
    return pl.pallas_call(
        body, name=name, grid=(l // tm, fh // tn),
        in_specs=[pl.BlockSpec((tm, d), lambda i, j: (i, 0)), pl.BlockSpec((tn, d), lambda i, j: (j, 0)), ospec, ospec],
        out_specs=[ospec, ospec], out_shape=[jax.ShapeDtypeStruct((l, fh), bf16)] * 2,
        compiler_params=_params(("parallel", "parallel"), 48),
    )(dhb, wd, a, b)


def _gnorm_fwd(o, proj, nw, heads, dv, gate_blk, name):
    l, hv = o.shape
    tr = _tile(l, 256)

    def body(o_ref, g_ref, w_ref, y_ref):
        wv = w_ref[...]
        for h in range(heads):
            sl = slice(h * dv, (h + 1) * dv)
            oh = o_ref[:, sl]
            r = lax.rsqrt(jnp.mean(oh * oh, axis=-1, keepdims=True) + RMS_EPS)
            y_ref[:, sl] = _b(oh * r * wv * _silu(g_ref[:, sl]))

    return pl.pallas_call(
        body, name=name, grid=(l // tr,),
        in_specs=[pl.BlockSpec((tr, hv), lambda i: (i, 0)), pl.BlockSpec((tr, hv), lambda i: (i, gate_blk)),
                  pl.BlockSpec((1, dv), lambda i: (0, 0))],
        out_specs=pl.BlockSpec((tr, hv), lambda i: (i, 0)),
        out_shape=jax.ShapeDtypeStruct((l, hv), bf16),
        compiler_params=_params(("parallel",), 32),
    )(o, proj, nw)


def _dy_gnorm_bwd(dhb, w_out, o, proj, nw, name, *, dv, gate_blk, tm, tn, comm=None):
    l, hv = o.shape
    d = dhb.shape[1]
    nj = hv // tn
    heads = tn // dv

    def body(g_ref, w_ref, o_ref, gate_ref, nw_ref, do_ref, dg_ref, dw_ref):
        dy = _nt(g_ref[...], w_ref[...])
        wv = nw_ref[...]
        dw = jnp.zeros((1, dv), f32)
        for h in range(heads):
            sl = slice(h * dv, (h + 1) * dv)
            oh = o_ref[:, sl]
            g = gate_ref[:, sl]
            dyh = dy[:, sl]
            r = lax.rsqrt(jnp.mean(oh * oh, axis=-1, keepdims=True) + RMS_EPS)
            xh = oh * r
            dn = dyh * _silu(g)
            dg_ref[:, sl] = _b(dyh * (xh * wv) * _dsilu(g))
            dxh = dn * wv
            do_ref[:, sl] = r * (dxh - xh * jnp.mean(dxh * xh, axis=-1, keepdims=True))
            dw = dw + jnp.sum(dn * xh, axis=0, keepdims=True)
        first = jnp.logical_and(pl.program_id(0) == 0, pl.program_id(1) == 0)

        @pl.when(first)
        def _():
            dw_ref[...] = dw

        @pl.when(jnp.logical_not(first))
        def _():
            dw_ref[...] += dw

    tile = pl.BlockSpec((tm, tn), lambda i, j: (i, j))
    gate = pl.BlockSpec((tm, tn), lambda i, j: (i, gate_blk * nj + j))
    vec = pl.BlockSpec((1, dv), lambda i, j: (0, 0))
    return _call(
        body, (dhb, w_out, o, proj, nw), name=name, grid=(l // tm, nj),
        in_specs=[pl.BlockSpec((tm, d), lambda i, j: (i, 0)), pl.BlockSpec((tn, d), lambda i, j: (j, 0)),
                  tile, gate, vec],
        out_specs=[tile, gate, vec],
        out_shape=[jax.ShapeDtypeStruct((l, hv), f32), jax.ShapeDtypeStruct(proj.shape, bf16),
                   jax.ShapeDtypeStruct((1, dv), f32)],
        sem=("arbitrary", "arbitrary"), vmem_mb=48, comm=comm)


def _ret_prep(proj, cos, sin, name, comm=None):
    l = proj.shape[0]
    tr = _tile(l, 256)
    half = RET_DK // 2
    scale = RET_DK ** -0.5

    def body(p_ref, c_ref, s_ref, o_ref):
        rows = pl.program_id(0) * tr + lax.broadcasted_iota(jnp.int32, (tr, 1), 0)
        kmul = jnp.where(rows >= PAD, scale, 0.0).astype(f32)
        c, s = c_ref[...], s_ref[...]
        for j in range(2 * RET_HEADS):
            t1 = p_ref[:, j * RET_DK: j * RET_DK + half]
            t2 = p_ref[:, j * RET_DK + half: (j + 1) * RET_DK]
            o1 = t1 * c - t2 * s
            o2 = t1 * s + t2 * c
            if j >= RET_HEADS:
                o1, o2 = o1 * kmul, o2 * kmul
            o_ref[:, j * RET_DK: j * RET_DK + half] = o1
            o_ref[:, j * RET_DK + half: (j + 1) * RET_DK] = o2

    wide = pl.BlockSpec((tr, 2 * RET_QK), lambda i: (i, 0))
    tab = pl.BlockSpec((tr, half), lambda i: (i, 0))
    return _call(body, (proj, cos, sin), name=name, grid=(l // tr,), in_specs=[wide, tab, tab], out_specs=[wide],
                 out_shape=[jax.ShapeDtypeStruct((l, 2 * RET_QK), f32)], sem=("parallel",), vmem_mb=32, comm=comm)


def _ret_prep_bwd(dq, dk, cos, sin, dproj, name):
    l = dq.shape[0]
    tr = _tile(l, 256)
    half = RET_DK // 2
    scale = RET_DK ** -0.5

    def body(dq_ref, dk_ref, c_ref, s_ref, _, o_ref):
        rows = pl.program_id(0) * tr + lax.broadcasted_iota(jnp.int32, (tr, 1), 0)
        kmul = jnp.where(rows >= PAD, scale, 0.0).astype(f32)
        c, s = c_ref[...], s_ref[...]
        for j in range(2 * RET_HEADS):
            d_ref = dq_ref if j < RET_HEADS else dk_ref
            jj = j % RET_HEADS
            d1 = d_ref[:, jj * RET_DK: jj * RET_DK + half]
            d2 = d_ref[:, jj * RET_DK + half: (jj + 1) * RET_DK]
            if j >= RET_HEADS:
                d1, d2 = d1 * kmul, d2 * kmul
            o_ref[:, j * RET_DK: j * RET_DK + half] = _b(d1 * c + d2 * s)
            o_ref[:, j * RET_DK + half: (j + 1) * RET_DK] = _b(d2 * c - d1 * s)

    nar = pl.BlockSpec((tr, RET_QK), lambda i: (i, 0))
    wide = pl.BlockSpec((tr, 2 * RET_QK), lambda i: (i, 0))
    tab = pl.BlockSpec((tr, half), lambda i: (i, 0))
    return pl.pallas_call(
        body, name=name, grid=(l // tr,), in_specs=[nar, nar, tab, tab, pl.BlockSpec(memory_space=pl.ANY)],
        out_specs=wide, out_shape=jax.ShapeDtypeStruct(dproj.shape, dproj.dtype), input_output_aliases={4: 0},
        compiler_params=_params(("parallel",), 32),
    )(dq, dk, cos, sin, dproj)


RET_BLOCK_CHUNKS = 3


def _ret_block(l):
    nch = l // CHUNK
    return RET_BLOCK_CHUNKS * CHUNK if nch % RET_BLOCK_CHUNKS == 0 else CHUNK


def _ret_decay(lg, rb):
    idx = lax.broadcasted_iota(jnp.int32, (rb, 1), 0).astype(f32)
    ri = lax.broadcasted_iota(jnp.int32, (rb, rb), 0)
    ci = lax.broadcasted_iota(jnp.int32, (rb, rb), 1)
    rel = (ri - ci).astype(f32)
    dmask = jnp.where(ri >= ci, jnp.exp(lg * jnp.maximum(rel, 0.0)), 0.0)
    xi = jnp.exp(lg * (idx + 1.0))
    zeta = jnp.exp(lg * (rb - 1.0 - idx))
    return dmask, xi, zeta


def _ret_scan_fwd(qk, proj, gn_w, lgs, gcs, name, comm=None):
    l = qk.shape[0]
    rb = _ret_block(l)
    nb = l // rb

    def body(lg_ref, gc_ref, q_ref, k_ref, v_ref, g_ref, nw_ref, o_ref, st_ref, y_ref, s_ref):
        @pl.when(pl.program_id(0) == 0)
        def _():
            s_ref[...] = jnp.zeros_like(s_ref)

        hs = range(RET_HEADS)
        dec = [_ret_decay(lg_ref[h], rb) for h in hs]
        q = [q_ref[:, h * RET_DK:(h + 1) * RET_DK] for h in hs]
        k = [k_ref[:, h * RET_DK:(h + 1) * RET_DK] for h in hs]
        vb = [_b(v_ref[:, h * RET_DV:(h + 1) * RET_DV]) for h in hs]
        s = [s_ref[h] for h in hs]
        sb = [_b(s[h]) for h in hs]
        scores = [_b(_nt(_b(q[h]), _b(k[h])) * dec[h][0]) for h in hs]
        inter = [_nn(_b(q[h] * dec[h][1]), sb[h]) for h in hs]
        kv = [_tn(_b(k[h] * dec[h][2]), vb[h]) for h in hs]
        nw = nw_ref[...]
        for h in hs:
            cols = slice(h * RET_DV, (h + 1) * RET_DV)
            st_ref[0, h] = sb[h]
            o = _nn(scores[h], vb[h]) + inter[h]
            o_ref[:, cols] = o
            r = lax.rsqrt(jnp.mean(o * o, axis=-1, keepdims=True) + RMS_EPS)
            y_ref[:, cols] = _b(o * r * nw * _silu(g_ref[:, cols]))
            s_ref[h] = gc_ref[h] * s[h] + kv[h]

    smem = pl.BlockSpec(memory_space=pltpu.SMEM)
    wide = pl.BlockSpec((rb, RET_V), lambda n: (n, 0))
    return _call(
        body, (lgs, gcs, qk, qk, proj, proj, gn_w), name=name, grid=(nb,),
        in_specs=[smem, smem,
                  pl.BlockSpec((rb, RET_QK), lambda n: (n, 0)),
                  pl.BlockSpec((rb, RET_QK), lambda n: (n, 1)),
                  pl.BlockSpec((rb, RET_V), lambda n: (n, 1)),
                  pl.BlockSpec((rb, RET_V), lambda n: (n, 2)),
                  pl.BlockSpec((1, RET_DV), lambda n: (0, 0))],
        out_specs=[wide, pl.BlockSpec((1, RET_HEADS, RET_DK, RET_DV), lambda n: (n, 0, 0, 0)), wide],
        out_shape=[jax.ShapeDtypeStruct((l, RET_V), f32),
                   jax.ShapeDtypeStruct((nb, RET_HEADS, RET_DK, RET_DV), bf16),
                   jax.ShapeDtypeStruct((l, RET_V), bf16)],
        scratch=[pltpu.VMEM((RET_HEADS, RET_DK, RET_DV), f32)], sem=("arbitrary",), vmem_mb=40, comm=comm)


def _ret_scan_bwd(qk, proj, states, do, dproj, lgs, gcs, name, comm=None):
    l = qk.shape[0]
    rb = _ret_block(l)
    nb = l // rb

    def body(lg_ref, gc_ref, q_ref, k_ref, v_ref, st_ref, do_ref, _, dq_ref, dk_ref, dv_ref, ds_ref):
        @pl.when(pl.program_id(0) == 0)
        def _():
            ds_ref[...] = jnp.zeros_like(ds_ref)

        hs = range(RET_HEADS)
        dec = [_ret_decay(lg_ref[h], rb) for h in hs]
        q = [q_ref[:, h * RET_DK:(h + 1) * RET_DK] for h in hs]
        k = [k_ref[:, h * RET_DK:(h + 1) * RET_DK] for h in hs]
        qb, kb = [_b(t) for t in q], [_b(t) for t in k]
        vb = [_b(v_ref[:, h * RET_DV:(h + 1) * RET_DV]) for h in hs]
        dob = [_b(do_ref[:, h * RET_DV:(h + 1) * RET_DV]) for h in hs]
        dsp = [ds_ref[h] for h in hs]
        dspb = [_b(t) for t in dsp]
        scores = [_b(_nt(qb[h], kb[h]) * dec[h][0]) for h in hs]
        dscores = [_b(_nt(dob[h], vb[h]) * dec[h][0]) for h in hs]
        for h in hs:
            dq_ref[:, h * RET_DK:(h + 1) * RET_DK] = _nn(dscores[h], kb[h]) + _nt(dob[h], st_ref[0, h]) * dec[h][1]
        for h in hs:
            dk_ref[:, h * RET_DK:(h + 1) * RET_DK] = _tn(dscores[h], qb[h]) + _nt(vb[h], dspb[h]) * dec[h][2]
        for h in hs:
            dv_ref[:, h * RET_DV:(h + 1) * RET_DV] = _b(_tn(scores[h], dob[h]) + _nn(_b(k[h] * dec[h][2]), dspb[h]))
        for h in hs:
            ds_ref[h] = gc_ref[h] * dsp[h] + _tn(_b(q[h] * dec[h][1]), dob[h])

    smem = pl.BlockSpec(memory_space=pltpu.SMEM)
    rev = lambda n: nb - 1 - n
    return _call(
        body, (lgs, gcs, qk, qk, proj, states, do, dproj), name=name, grid=(nb,),
        in_specs=[smem, smem,
                  pl.BlockSpec((rb, RET_QK), lambda n: (rev(n), 0)),
                  pl.BlockSpec((rb, RET_QK), lambda n: (rev(n), 1)),
                  pl.BlockSpec((rb, RET_V), lambda n: (rev(n), 1)),
                  pl.BlockSpec((1, RET_HEADS, RET_DK, RET_DV), lambda n: (rev(n), 0, 0, 0)),
                  pl.BlockSpec((rb, RET_V), lambda n: (rev(n), 0)),
                  pl.BlockSpec(memory_space=pl.ANY)],
        out_specs=[pl.BlockSpec((rb, RET_QK), lambda n: (rev(n), 0)),
                   pl.BlockSpec((rb, RET_QK), lambda n: (rev(n), 0)),
                   pl.BlockSpec((rb, RET_V), lambda n: (rev(n), 1))],
        out_shape=[jax.ShapeDtypeStruct((l, RET_QK), f32), jax.ShapeDtypeStruct((l, RET_QK), f32),
                   jax.ShapeDtypeStruct(dproj.shape, dproj.dtype)],
        scratch=[pltpu.VMEM((RET_HEADS, RET_DK, RET_DV), f32)], sem=("arbitrary",), vmem_mb=40, comm=comm,
        aliases={7: 2})


CONV_BLK = 512
CONV_Q_BLKS = DN_QK // CONV_BLK
HALO = 8


def _conv_tile(l):
    return _tile(l, 3072)


def _slab_rows(r):
    return pl.ds(pl.multiple_of(r * HALO, HALO), HALO)


def _conv_slab(x_ref, p_ref, r, i, tr):
    cur = x_ref[_slab_rows(r), :]
    prev = jnp.where(r > 0, x_ref[_slab_rows(jnp.maximum(r - 1, 0)), :], p_ref[...])
    row0 = i * tr + r * HALO
    cur = jnp.where(row0 >= PAD, cur, 0.0)
    prev = jnp.where(row0 - HALO >= PAD, prev, 0.0)
    lrow = lax.broadcasted_iota(jnp.int32, (HALO, 1), 0)
    shifted = [jnp.where(lrow < s, pltpu.roll(prev, s, 0), pltpu.roll(cur, s, 0)) for s in range(1, CONV_K)]
    return [cur] + shifted


def _conv_of(xs, w):
    acc = xs[0] * w[CONV_K - 1:CONV_K, :]
    for s in range(1, CONV_K):
        acc = acc + xs[s] * w[CONV_K - 1 - s:CONV_K - s, :]
    return acc


def _slab_loop(n_slabs, fn, init=None):
    return lax.fori_loop(0, n_slabs, fn, init, unroll=8)


def _dn_conv_fwd(proj, conv_w, name, comm=None):
    l = proj.shape[0]
    tr = _conv_tile(l)
    nblk = DN_CONV_CH // CONV_BLK
    heads = CONV_BLK // DN_DK

    def body(x_ref, p_ref, w_ref, o_ref):
        i, j = pl.program_id(0), pl.program_id(1)
        w = w_ref[...]

        def act(r):
            return _silu(_conv_of(_conv_slab(x_ref, p_ref, r, i, tr), w))

        def normed(scale):
            def slab(r, carry):
                a = act(r)
                outs = []
                for h in range(heads):
                    ah = a[:, h * DN_DK:(h + 1) * DN_DK]
                    outs.append(ah * (lax.rsqrt(jnp.sum(ah * ah, axis=-1, keepdims=True) + RMS_EPS) * scale))
                o_ref[_slab_rows(r), :] = jnp.concatenate(outs, axis=1)
                return carry
            return slab

        def plain(r, carry):
            o_ref[_slab_rows(r), :] = act(r)
            return carry

        @pl.when(j < CONV_Q_BLKS)
        def _():
            _slab_loop(tr // HALO, normed(DN_DK ** -0.5))

        @pl.when(jnp.logical_and(j >= CONV_Q_BLKS, j < 2 * CONV_Q_BLKS))
        def _():
            _slab_loop(tr // HALO, normed(1.0))

        @pl.when(j >= 2 * CONV_Q_BLKS)
        def _():
            _slab_loop(tr // HALO, plain)

    hb = tr // HALO
    return _call(
        body, (proj, proj, conv_w), name=name, grid=(l // tr, nblk),
        in_specs=[pl.BlockSpec((tr, CONV_BLK), lambda i, j: (i, j)),
                  pl.BlockSpec((HALO, CONV_BLK), lambda i, j: (jnp.maximum(i * hb - 1, 0), j)),
                  pl.BlockSpec((CONV_K, CONV_BLK), lambda i, j: (0, j))],
        out_specs=[pl.BlockSpec((tr, CONV_BLK), lambda i, j: (i, j))],
        out_shape=[jax.ShapeDtypeStruct((l, DN_CONV_CH), f32)],
        scratch=[], sem=("parallel", "parallel"), vmem_mb=32, comm=comm)


def _dn_conv_bwd_a(proj, conv_w, dqkv, name, comm=None):
    l = proj.shape[0]
    tr = _conv_tile(l)
    nblk = DN_CONV_CH // CONV_BLK
    heads = CONV_BLK // DN_DK

    def body(x_ref, p_ref, w_ref, d_ref, dc_ref, dw_ref, acc_ref):
        j, i = pl.program_id(0), pl.program_id(1)
        w = w_ref[...]
        acc_ref[...] = jnp.zeros_like(acc_ref)

        def run(l2_scale):
            _slab_loop(tr // HALO, slab_of(l2_scale))

        def slab_of(l2_scale):
            def slab(r, carry):
                xs = _conv_slab(x_ref, p_ref, r, i, tr)
                c = _conv_of(xs, w)
                a = _silu(c)
                dy = d_ref[_slab_rows(r), :]
                if l2_scale is None:
                    da = dy
                else:
                    parts = []
                    for h in range(heads):
                        sl = slice(h * DN_DK, (h + 1) * DN_DK)
                        ah, dyh = a[:, sl], dy[:, sl]
                        rn = lax.rsqrt(jnp.sum(ah * ah, axis=-1, keepdims=True) + RMS_EPS)
                        yh = ah * rn
                        parts.append((rn * l2_scale) * (dyh - yh * jnp.sum(dyh * yh, axis=-1, keepdims=True)))
                    da = jnp.concatenate(parts, axis=1)
                dc = da * _dsilu(c)
                dc_ref[_slab_rows(r), :] = dc
                for k in range(CONV_K):
                    acc_ref[k] += dc * xs[CONV_K - 1 - k]
                return carry
            return slab

        @pl.when(j < CONV_Q_BLKS)
        def _():
            run(DN_DK ** -0.5)

        @pl.when(jnp.logical_and(j >= CONV_Q_BLKS, j < 2 * CONV_Q_BLKS))
        def _():
            run(1.0)

        @pl.when(j >= 2 * CONV_Q_BLKS)
        def _():
            run(None)

        ksel = lax.broadcasted_iota(jnp.int32, (CONV_K, 1), 0)
        dw = jnp.zeros((CONV_K, CONV_BLK), f32)
        for k in range(CONV_K):
            dw = dw + jnp.where(ksel == k, jnp.sum(acc_ref[k], axis=0, keepdims=True), 0.0)

        @pl.when(i == 0)
        def _():
            dw_ref[...] = dw

        @pl.when(i > 0)
        def _():
            dw_ref[...] += dw

    hb = tr // HALO
    blk = pl.BlockSpec((tr, CONV_BLK), lambda j, i: (i, j))
    return _call(
        body, (proj, proj, conv_w, dqkv), name=name, grid=(nblk, l // tr),
        in_specs=[blk, pl.BlockSpec((HALO, CONV_BLK), lambda j, i: (jnp.maximum(i * hb - 1, 0), j)),
                  pl.BlockSpec((CONV_K, CONV_BLK), lambda j, i: (0, j)), blk],
        out_specs=[blk, pl.BlockSpec((CONV_K, CONV_BLK), lambda j, i: (0, j))],
        out_shape=[jax.ShapeDtypeStruct((l, DN_CONV_CH), f32), jax.ShapeDtypeStruct((CONV_K, DN_CONV_CH), f32)],
        scratch=[pltpu.VMEM((CONV_K, HALO, CONV_BLK), f32)], sem=("parallel", "arbitrary"), vmem_mb=48, comm=comm)


def _dn_conv_bwd_b(dc, conv_w, dproj, name):
    l = dc.shape[0]
    tr = _conv_tile(l)
    nblk = DN_CONV_CH // CONV_BLK
    nrow = l // tr

    n_slabs = tr // HALO
    pair = 2 * HALO

    def body(d_ref, n_ref, w_ref, _, o_ref):
        i = pl.program_id(0)
        w = w_ref[...]
        nxt_tile = jnp.where(i < nrow - 1, n_ref[...], 0.0)
        lrow = lax.broadcasted_iota(jnp.int32, (HALO, 1), 0)

        def one(r):
            cur = d_ref[_slab_rows(r), :]
            nxt = jnp.where(r < n_slabs - 1, d_ref[_slab_rows(jnp.minimum(r + 1, n_slabs - 1)), :], nxt_tile)
            acc = cur * w[CONV_K - 1:CONV_K, :]
            for s in range(1, CONV_K):
                up = jnp.where(lrow >= HALO - s, pltpu.roll(nxt, HALO - s, 0), pltpu.roll(cur, HALO - s, 0))
                acc = acc + up * w[CONV_K - 1 - s:CONV_K - s, :]
            return jnp.where(i * tr + r * HALO >= PAD, acc, 0.0)

        def two(q, carry):
            rows = pl.ds(pl.multiple_of(q * pair, pair), pair)
            o_ref[rows, :] = _b(jnp.concatenate([one(2 * q), one(2 * q + 1)], axis=0))
            return carry

        lax.fori_loop(0, n_slabs // 2, two, None, unroll=4)

    hb = tr // HALO
    nh = l // HALO
    return pl.pallas_call(
        body, name=name, grid=(nrow, nblk),
        in_specs=[pl.BlockSpec((tr, CONV_BLK), lambda i, j: (i, j)),
                  pl.BlockSpec((HALO, CONV_BLK), lambda i, j: (jnp.minimum((i + 1) * hb, nh - 1), j)),
                  pl.BlockSpec((CONV_K, CONV_BLK), lambda i, j: (0, j)),
                  pl.BlockSpec(memory_space=pl.ANY)],
        out_specs=pl.BlockSpec((tr, CONV_BLK), lambda i, j: (i, j)),
        out_shape=jax.ShapeDtypeStruct(dproj.shape, dproj.dtype), input_output_aliases={3: 0},
        compiler_params=_params(("parallel", "parallel"), 32),
    )(dc, dc, conv_w, dproj)


BA_W = LANES


def _dn_gates(ba_ref, al_ref, dt_ref, n):
    rows = n * CHUNK + lax.broadcasted_iota(jnp.int32, (CHUNK, 1), 0)
    vm = (rows >= PAD).astype(f32)
    bin_ = ba_ref[:, 0:DN_HEADS]
    z = ba_ref[:, DN_HEADS:2 * DN_HEADS] + dt_ref[...]
    sp = jnp.maximum(z, 0.0) + jnp.log1p(jnp.exp(-jnp.abs(z)))
    ea = jnp.exp(al_ref[...])
    beta = _sigmoid(bin_) * vm
    g = -ea * sp * vm
    return vm, bin_, z, ea, beta, g


def _tri():
    ri = lax.broadcasted_iota(jnp.int32, (CHUNK, CHUNK), 0)
    ci = lax.broadcasted_iota(jnp.int32, (CHUNK, CHUNK), 1)
    return ri, ci


def _split(a):
    hi = _b(a)
    return hi, _b(a - hi.astype(f32))


def _mm3(a, b, dot=_nn):
    (ah, al), (bh, bl) = _split(a), _split(b)
    return dot(ah, bh) + (dot(ah, bl) + dot(al, bh))


def _cumsum_rows(tri, g):
    tb = _b(tri)
    g1 = _b(g)
    r1 = g - g1.astype(f32)
    g2 = _b(r1)
    g3 = _b(r1 - g2.astype(f32))
    return _nn(tb, g1) + (_nn(tb, g2) + _nn(tb, g3))


DN_SCAN_CHUNKS = 3


def _scan_chunks(nch):
    return DN_SCAN_CHUNKS if nch % DN_SCAN_CHUNKS == 0 else 1


def _dn_prep(qkv, ba, a_log, dt_bias, name, comm=None):
    l = qkv.shape[0]
    nch = l // CHUNK
    heads = range(DN_HEADS)

    cb = _scan_chunks(nch)
    items = [(c, h) for c in range(cb) for h in heads]

    def body(q_ref, k_ref, v_ref, ba_ref, al_ref, dt_ref, t_ref, u_ref, wq_ref, pk_ref, eg_ref, kpt_ref, qwt_ref):
        n0 = pl.program_id(0) * cb
        ri, ci = _tri()
        incl, strict = ri >= ci, ri > ci
        eye = (ri == ci).astype(f32)
        rows = [slice(c * CHUNK, (c + 1) * CHUNK) for c in range(cb)]
        gam, gam_t, beta = [], [], []
        for c in range(cb):
            _, _, _, _, beta_c, g_c = _dn_gates(ba_ref[rows[c], :], al_ref, dt_ref, n0 + c)
            gam.append(_cumsum_rows(incl.astype(f32), g_c))
            gam_t.append(gam[c].T)
            beta.append(beta_c)
        gc = {(c, h): gam[c][:, h:h + 1] for c, h in items}
        bh = {(c, h): beta[c][:, h:h + 1] for c, h in items}
        kh = {(c, h): k_ref[rows[c], h * DN_DK:(h + 1) * DN_DK] for c, h in items}
        kb = {i: _b(kh[i]) for i in items}
        decay = {(c, h): jnp.exp(jnp.where(incl, gc[c, h] - gam_t[c][h:h + 1, :], -jnp.inf)) for c, h in items}
        a = {i: jnp.where(strict, bh[i] * _nt(kb[i], kb[i]) * decay[i], 0.0) for i in items}
        t = {i: eye - a[i] for i in items}
        p = a
        for level in range(int(math.log2(CHUNK)) - 1):
            mm = _mm3 if level < 2 else (lambda x, y: _nn(_b(x), _b(y)))
            p = {i: mm(p[i], p[i]) for i in items}
            t = {i: t[i] + mm(t[i], p[i]) for i in items}
        eg = {i: jnp.exp(gc[i]) for i in items}
        for c, h in items:
            i = (c, h)
            t_ref[c, h] = t[i]
            u_ref[rows[c], h * DN_DV:(h + 1) * DN_DV] = _mm3(t[i], v_ref[rows[c], h * DN_DV:(h + 1) * DN_DV] * bh[i])
            w = _mm3(t[i], kh[i] * (bh[i] * eg[i]))
            wq_ref[c, h, 0:CHUNK, :] = _b(w)
            qwt_ref[c, h, DN_DK:2 * DN_DK, :] = _b(w.T)
        for c, h in items:
            i = (c, h)
            qh = q_ref[rows[c], h * DN_DK:(h + 1) * DN_DK]
            gl = gc[i][CHUNK - 1:CHUNK, :]
            qe = qh * eg[i]
            ke = kh[i] * jnp.exp(gl - gc[i])
            pmat = _nt(_b(qh), kb[i]) * decay[i]
            wq_ref[c, h, CHUNK:2 * CHUNK, :] = _b(qe)
            qwt_ref[c, h, 0:DN_DK, :] = _b(qe.T)
            pk_ref[c, h, 0:CHUNK, :] = _b(pmat)
            pk_ref[c, h, CHUNK:CHUNK + DN_DK, :] = _b(ke.T)
            kpt_ref[c, h, :, 0:DN_DK] = _b(ke)
            kpt_ref[c, h, :, DN_DK:DN_DK + CHUNK] = _b(pmat.T)
            eg_ref[c, h] = jnp.broadcast_to(jnp.exp(gl), (8, LANES))

    vec = pl.BlockSpec((1, DN_HEADS), lambda n: (0, 0))
    return _call(
        body, (qkv, qkv, qkv, ba, a_log, dt_bias), name=name, grid=(nch // cb,),
        in_specs=[pl.BlockSpec((cb * CHUNK, DN_QK), lambda n: (n, 0)),
                  pl.BlockSpec((cb * CHUNK, DN_QK), lambda n: (n, 1)),
                  pl.BlockSpec((cb * CHUNK, DN_V), lambda n: (n, 1)),
                  pl.BlockSpec((cb * CHUNK, BA_W), lambda n: (n, 0)), vec, vec],
        out_specs=[pl.BlockSpec((cb, DN_HEADS, CHUNK, CHUNK), lambda n: (n, 0, 0, 0)),
                   pl.BlockSpec((cb * CHUNK, DN_V), lambda n: (n, 0)),
                   pl.BlockSpec((cb, DN_HEADS, 2 * CHUNK, DN_DK), lambda n: (n, 0, 0, 0)),
                   pl.BlockSpec((cb, DN_HEADS, CHUNK + DN_DK, CHUNK), lambda n: (n, 0, 0, 0)),
                   pl.BlockSpec((cb, DN_HEADS, 8, LANES), lambda n: (n, 0, 0, 0)),
                   pl.BlockSpec((cb, DN_HEADS, CHUNK, DN_DK + CHUNK), lambda n: (n, 0, 0, 0)),
                   pl.BlockSpec((cb, DN_HEADS, 2 * DN_DK, CHUNK), lambda n: (n, 0, 0, 0))],
        out_shape=[jax.ShapeDtypeStruct((nch, DN_HEADS, CHUNK, CHUNK), f32),
                   jax.ShapeDtypeStruct((l, DN_V), f32),
                   jax.ShapeDtypeStruct((nch, DN_HEADS, 2 * CHUNK, DN_DK), bf16),
                   jax.ShapeDtypeStruct((nch, DN_HEADS, CHUNK + DN_DK, CHUNK), bf16),
                   jax.ShapeDtypeStruct((nch, DN_HEADS, 8, LANES), f32),
                   jax.ShapeDtypeStruct((nch, DN_HEADS, CHUNK, DN_DK + CHUNK), bf16),
                   jax.ShapeDtypeStruct((nch, DN_HEADS, 2 * DN_DK, CHUNK), bf16)],
        sem=("parallel",), vmem_mb=40, comm=comm)


def _dn_scan_fwd(u, wq, pk, egl, name):
    l = u.shape[0]
    nch = l // CHUNK
    cs = _scan_chunks(nch)

    def body(u_ref, wq_ref, pk_ref, eg_ref, o_ref, st_ref, vn_ref, s_ref):
        @pl.when(pl.program_id(0) == 0)
        def _():
            s_ref[...] = jnp.zeros_like(s_ref)

        hs = range(DN_HEADS)
        cols = [slice(h * DN_DV, (h + 1) * DN_DV) for h in hs]
        s = [s_ref[h] for h in hs]
        for c in range(cs):
            rows = slice(c * CHUNK, (c + 1) * CHUNK)
            sb = [_b(s[h]) for h in hs]
            x = [_nn(wq_ref[c, h], sb[h]) for h in hs]
            vnb = [_b(u_ref[rows, cols[h]] - x[h][0:CHUNK]) for h in hs]
            y = [_nn(pk_ref[c, h], vnb[h]) for h in hs]
            for h in hs:
                st_ref[c, h] = sb[h]
                vn_ref[rows, cols[h]] = vnb[h]
                o_ref[rows, cols[h]] = x[h][CHUNK:2 * CHUNK] + y[h][0:CHUNK]
            s = [eg_ref[c, h][0:1, 0:1] * s[h] + y[h][CHUNK:CHUNK + DN_DK] for h in hs]
        for h in hs:
            s_ref[h] = s[h]

    return pl.pallas_call(
        body, name=name, grid=(nch // cs,),
        in_specs=[pl.BlockSpec((cs * CHUNK, DN_V), lambda n: (n, 0)),
                  pl.BlockSpec((cs, DN_HEADS, 2 * CHUNK, DN_DK), lambda n: (n, 0, 0, 0)),
                  pl.BlockSpec((cs, DN_HEADS, CHUNK + DN_DK, CHUNK), lambda n: (n, 0, 0, 0)),
                  pl.BlockSpec((cs, DN_HEADS, 8, LANES), lambda n: (n, 0, 0, 0))],
        out_specs=[pl.BlockSpec((cs * CHUNK, DN_V), lambda n: (n, 0)),
                   pl.BlockSpec((cs, DN_HEADS, DN_DK, DN_DV), lambda n: (n, 0, 0, 0)),
                   pl.BlockSpec((cs * CHUNK, DN_V), lambda n: (n, 0))],
        out_shape=[jax.ShapeDtypeStruct((l, DN_V), f32),
                   jax.ShapeDtypeStruct((nch, DN_HEADS, DN_DK, DN_DV), bf16),
                   jax.ShapeDtypeStruct((l, DN_V), bf16)],
        scratch_shapes=[pltpu.VMEM((DN_HEADS, DN_DK, DN_DV), f32)],
        compiler_params=_params(("arbitrary",), 40),
    )(u, wq, pk, egl)


def _dn_scan_bwd(do, kpt, qwt, egl, name):
    l = do.shape[0]
    nch = l // CHUNK
    cs = _scan_chunks(nch)
    nblk = nch // cs

    def body(do_ref, kpt_ref, qwt_ref, eg_ref, dvn_ref, dsp_ref, ds_ref):
        @pl.when(pl.program_id(0) == 0)
        def _():
            ds_ref[...] = jnp.zeros_like(ds_ref)

        hs = range(DN_HEADS)
        cols = [slice(h * DN_DV, (h + 1) * DN_DV) for h in hs]
        ds = [ds_ref[h] for h in hs]
        for c in reversed(range(cs)):
            rows = slice(c * CHUNK, (c + 1) * CHUNK)
            dspb = [_b(ds[h]) for h in hs]
            dob = [_b(do_ref[rows, cols[h]]) for h in hs]
            dvn = [_nn(kpt_ref[c, h][:, 0:DN_DK], dspb[h]) + _nn(kpt_ref[c, h][:, DN_DK:DN_DK + CHUNK], dob[h])
                   for h in hs]
            for h in hs:
                dsp_ref[c, h] = dspb[h]
                dvn_ref[rows, cols[h]] = dvn[h]
            ds = [eg_ref[c, h][0:1, 0:1] * ds[h] + _nn(qwt_ref[c, h][0:DN_DK], dob[h])
                  - _nn(qwt_ref[c, h][DN_DK:2 * DN_DK], _b(dvn[h])) for h in hs]
        for h in hs:
            ds_ref[h] = ds[h]

    rev = lambda s: nblk - 1 - s
    return pl.pallas_call(
        body, name=name, grid=(nblk,),
        in_specs=[pl.BlockSpec((cs * CHUNK, DN_V), lambda s: (rev(s), 0)),
                  pl.BlockSpec((cs, DN_HEADS, CHUNK, DN_DK + CHUNK), lambda s: (rev(s), 0, 0, 0)),
                  pl.BlockSpec((cs, DN_HEADS, 2 * DN_DK, CHUNK), lambda s: (rev(s), 0, 0, 0)),
                  pl.BlockSpec((cs, DN_HEADS, 8, LANES), lambda s: (rev(s), 0, 0, 0))],
        out_specs=[pl.BlockSpec((cs * CHUNK, DN_V), lambda s: (rev(s), 0)),
                   pl.BlockSpec((cs, DN_HEADS, DN_DK, DN_DV), lambda s: (rev(s), 0, 0, 0))],
        out_shape=[jax.ShapeDtypeStruct((l, DN_V), f32),
                   jax.ShapeDtypeStruct((nch, DN_HEADS, DN_DK, DN_DV), bf16)],
        scratch_shapes=[pltpu.VMEM((DN_HEADS, DN_DK, DN_DV), f32)],
        compiler_params=_params(("arbitrary",), 40),
    )(do, kpt, qwt, egl)


def _dn_post_bwd(qkv, ba, a_log, dt_bias, states, dsp_all, tinv_all, u_all, wq, vn_all, do, dvn_all, name):
    l = qkv.shape[0]
    nch = l // CHUNK
    cb = _scan_chunks(nch)
    items = [(c, h) for c in range(cb) for h in range(DN_HEADS)]

    def body(q_ref, k_ref, v_ref, ba_ref, al_ref, dt_ref, st_ref, dsp_ref, t_ref, u_ref, wq_ref, vn_ref, do_ref,
             dvn_ref, dqkv_ref, dba_ref, dal_ref, ddt_ref):
        step = pl.program_id(0)
        ri, ci = _tri()
        incl, strict = ri >= ci, ri > ci
        lane8 = lax.broadcasted_iota(jnp.int32, (1, DN_HEADS), 1)
        sub8 = lax.broadcasted_iota(jnp.int32, (DN_HEADS, 1), 0)
        last = (lax.broadcasted_iota(jnp.int32, (CHUNK, 1), 0) == CHUNK - 1).astype(f32)
        rsum = lambda t: jnp.sum(t, axis=-1, keepdims=True)
        rows = [slice(c * CHUNK, (c + 1) * CHUNK) for c in range(cb)]
        gates = [_dn_gates(ba_ref[rows[c], :], al_ref, dt_ref, step * cb + c) for c in range(cb)]
        gam = [_cumsum_rows(incl.astype(f32), gates[c][5]) for c in range(cb)]
        gam_t = [gam[c].T for c in range(cb)]
        each = lambda fn: {(c, h): fn(c, h) for c, h in items}
        dk_cols = lambda h: slice(h * DN_DK, (h + 1) * DN_DK)
        dv_cols = lambda h: slice(h * DN_DV, (h + 1) * DN_DV)
        gc = each(lambda c, h: gam[c][:, h:h + 1])
        bh = each(lambda c, h: gates[c][4][:, h:h + 1])
        qh = each(lambda c, h: q_ref[rows[c], dk_cols(h)])
        kh = each(lambda c, h: k_ref[rows[c], dk_cols(h)])
        doh = each(lambda c, h: _b(do_ref[rows[c], dv_cols(h)]))
        sb = each(lambda c, h: st_ref[c, h])
        dspb = each(lambda c, h: dsp_ref[c, h])
        vnb = each(lambda c, h: vn_ref[rows[c], dv_cols(h)])
        dvn = each(lambda c, h: dvn_ref[rows[c], dv_cols(h)])
        wb = each(lambda c, h: wq_ref[c, h, 0:CHUNK, :])
        decay = each(lambda c, h: jnp.exp(jnp.where(incl, gc[c, h] - gam_t[c][h:h + 1, :], -jnp.inf)))
        qb, kb = each(lambda c, h: _b(qh[c, h])), each(lambda c, h: _b(kh[c, h]))
        eg = each(lambda c, h: jnp.exp(gc[c, h]))
        gl = each(lambda c, h: gc[c, h][CHUNK - 1:CHUNK, :])
        ekd = each(lambda c, h: jnp.exp(gl[c, h] - gc[c, h]))
        dvnb = each(lambda c, h: _b(dvn[c, h]))
        kk = each(lambda c, h: _nt(kb[c, h], kb[c, h]))
        p = each(lambda c, h: _nt(qb[c, h], kb[c, h]) * decay[c, h])
        dpraw = each(lambda c, h: _nt(doh[c, h], vnb[c, h]))
        dqe = each(lambda c, h: _nt(doh[c, h], sb[c, h]))
        dke = each(lambda c, h: _nt(vnb[c, h], dspb[c, h]))
        dw = each(lambda c, h: -_nt(dvnb[c, h], sb[c, h]))
        dru = each(lambda c, h: _mm3(t_ref[c, h], dvn[c, h], _tn))
        drw = each(lambda c, h: _mm3(t_ref[c, h], dw[c, h], _tn))
        dqk = each(lambda c, h: _b(dpraw[c, h] * decay[c, h]))
        for c, h in items:
            i = (c, h)
            dqkv_ref[rows[c], dk_cols(h)] = _nn(dqk[i], kb[i]) + dqe[i] * eg[i]
            dqkv_ref[rows[c], 2 * DN_QK + h * DN_DV:2 * DN_QK + (h + 1) * DN_DV] = bh[i] * dru[i]
        da = each(lambda c, h: jnp.where(strict, -(_nt(_b(dru[c, h]), _b(u_ref[rows[c], dv_cols(h)]))
                                                   + _nt(_b(drw[c, h]), wb[c, h])), 0.0))
        dkk = each(lambda c, h: _b(da[c, h] * bh[c, h] * decay[c, h]))
        for c, h in items:
            i = (c, h)
            dqkv_ref[rows[c], DN_QK + h * DN_DK:DN_QK + (h + 1) * DN_DK] = (
                _tn(dqk[i], qb[i]) + dke[i] * ekd[i] + (bh[i] * eg[i]) * drw[i]
                + _nn(dkk[i], kb[i]) + _tn(dkk[i], kb[i]))
        dal = jnp.zeros((1, DN_HEADS), f32)
        ddt = jnp.zeros((1, DN_HEADS), f32)
        dba_ref[...] = jnp.zeros_like(dba_ref)
        for c in range(cb):
            vm, bin_, z, ea, _, g = gates[c]
            dbeta = jnp.zeros((CHUNK, DN_HEADS), f32)
            dgam = jnp.zeros((CHUNK, DN_HEADS), f32)
            dgam_neg_t = jnp.zeros((DN_HEADS, CHUNK), f32)
            for h in range(DN_HEADS):
                i = (c, h)
                keg = kh[i] * eg[i]
                ke = kh[i] * ekd[i]
                rw = rsum(drw[i] * keg)
                rke = rsum(dke[i] * ke)
                db_h = rsum(dru[i] * v_ref[rows[c], dv_cols(h)]) + rw + rsum(da[i] * kk[i] * decay[i])
                mm = da[i] * (bh[i] * kk[i] * decay[i]) + dpraw[i] * p[i]
                dgl = (jnp.sum(rke, axis=0, keepdims=True)
                       + jnp.exp(gl[i]) * jnp.sum(rsum(dspb[i].astype(f32) * sb[i].astype(f32)), axis=0,
                                                  keepdims=True))
                dg_h = rsum(mm) + rw * bh[i] + rsum(dqe[i] * (qh[i] * eg[i])) - rke + last * dgl
                dbeta = dbeta + jnp.where(lane8 == h, db_h, 0.0)
                dgam = dgam + jnp.where(lane8 == h, dg_h, 0.0)
                dgam_neg_t = dgam_neg_t + jnp.where(sub8 == h, jnp.sum(mm, axis=0, keepdims=True), 0.0)
            dgam = dgam - dgam_neg_t.T
            dg = _cumsum_rows((ri <= ci).astype(f32), dgam)
            sg = _sigmoid(bin_)
            dain = dg * (-ea) * vm * _sigmoid(z)
            dba_ref[rows[c], 0:DN_HEADS] = dbeta * vm * sg * (1.0 - sg)
            dba_ref[rows[c], DN_HEADS:2 * DN_HEADS] = dain
            dal = dal + jnp.sum(dg * g, axis=0, keepdims=True)
            ddt = ddt + jnp.sum(dain, axis=0, keepdims=True)

        @pl.when(step == 0)
        def _():
            dal_ref[...] = dal
            ddt_ref[...] = ddt

        @pl.when(step > 0)
        def _():
            dal_ref[...] += dal
            ddt_ref[...] += ddt

    vec = pl.BlockSpec((1, DN_HEADS), lambda s: (0, 0))
    qs = pl.BlockSpec((cb * CHUNK, DN_QK), lambda s: (s, 0))
    ks = pl.BlockSpec((cb * CHUNK, DN_QK), lambda s: (s, 1))
    vs = pl.BlockSpec((cb * CHUNK, DN_V), lambda s: (s, 1))
    v0 = pl.BlockSpec((cb * CHUNK, DN_V), lambda s: (s, 0))
    st = pl.BlockSpec((cb, DN_HEADS, DN_DK, DN_DV), lambda s: (s, 0, 0, 0))
    return pl.pallas_call(
        body, name=name, grid=(nch // cb,),
        in_specs=[qs, ks, vs, pl.BlockSpec((cb * CHUNK, BA_W), lambda s: (s, 0)), vec, vec, st, st,
                  pl.BlockSpec((cb, DN_HEADS, CHUNK, CHUNK), lambda s: (s, 0, 0, 0)),
                  v0, pl.BlockSpec((cb, DN_HEADS, 2 * CHUNK, DN_DK), lambda s: (s, 0, 0, 0)), v0, v0, v0],
        out_specs=[pl.BlockSpec((cb * CHUNK, DN_CONV_CH), lambda s: (s, 0)),
                   pl.BlockSpec((cb * CHUNK, BA_W), lambda s: (s, 0)), vec, vec],
        out_shape=[jax.ShapeDtypeStruct((l, DN_CONV_CH), f32), jax.ShapeDtypeStruct((l, BA_W), f32),
                   jax.ShapeDtypeStruct((1, DN_HEADS), f32), jax.ShapeDtypeStruct((1, DN_HEADS), f32)],
        compiler_params=_params(("arbitrary",), 48),
    )(qkv, qkv, qkv, ba, a_log, dt_bias, states, dsp_all, tinv_all, u_all, wq, vn_all, do, dvn_all)


def _ffn_fwd(h, hn, wg, wu, wd, tb, th, tag, plan, next_norm_w=None):
    fh, d = wd.shape
    a, b, s = plan.call(f"{tag}_gu", functools.partial(_ffn_gu, tm=th // 2, tn=fh // 2), hn, wg, wu, n_out=3)
    out = plan.matmul(f"{tag}_down", s, wd, mode="nn", tm=th // 2, tn=d, tk=fh, res=h, norm_w=next_norm_w)
    return out, (hn, a, b, s)


def _ffn_bwd(dh, dhb, h, nw, wg, wu, wd, saved, tb, th, tag, plan):
    hn, a, b, s = saved
    d = h.shape[1]
    fh = wd.shape[0]
    layer = tag[-1]
    gr = plan.grads
    da, db = _ffn_ds(dhb, wd, a, b, tm=th // 2, tn=fh // 2, name=f"{tag}_b_ds")
    gr["down" + layer] = _matmul(s, dhb, mode="tn", tm=fh // 2, tn=d, tk=th, out_dtype=bf16, name=f"{tag}_b_dwd")
    dh2, dh2b, dnw = plan.call(f"{tag}_b_dhn", functools.partial(_dhn_norm_bwd, tm=th // 2, tk=fh // 2),
                               [(da, wg), (db, wu)], h, nw, dh, n_out=3)
    gr["gate" + layer] = _matmul(hn, da, mode="tn", tm=d, tn=fh // 2, tk=th, out_dtype=bf16, name=f"{tag}_b_dwg")
    gr["up" + layer] = _matmul(hn, db, mode="tn", tm=d, tn=fh // 2, tk=th, out_dtype=bf16, name=f"{tag}_b_dwu")
    return dh2, dh2b, dnw


class _Plan:
    GATHERS = {"ret_proj": ("ret_out", "gate0", "up0"), "ret_scan": ("down0", "dn_in"),
               "dn_conv": ("dn_out", "gate1", "up1", "down1")}
    SCATTERS = {"dn_b_conv_a": ("down1", "gate1", "up1", "dn_out"), "ffn0_b_dhn": ("dn_in",),
                "ret_b_gnorm": ("ret_out", "gate0"), "ret_b_scan": ("up0",), "ret_b_dwin": ("down0",),
                "ret_b_dhn": ("ret_in",)}

    def __init__(self, shards, wts):
        self.shards, self.wts, self.grads, self.parts = shards, wts, {}, {}

    def _exchange(self, stage):
        if self.shards is None:
            return None
        if stage in self.GATHERS:
            return _Exchange([self.shards[n] for n in self.GATHERS[stage]], True)
        if stage in self.SCATTERS:
            return _Exchange([self._dev_major(n) for n in self.SCATTERS[stage]], False)
        return None

    def _dev_major(self, name):
        g = self.grads
        if name[:-1] in ("gate", "up"):
            return _dev_major_cols(g[name], g[name].shape[1] // N_DEV)
        if name[:-1] == "down":
            dwd = g[name]
            return dwd.reshape(N_DEV, dwd.shape[0] // N_DEV, dwd.shape[1])
        if name in ("ret_out", "dn_out"):
            return g[name].reshape(N_DEV, g[name].shape[0] // N_DEV, g[name].shape[1])
        return _dev_major_cols(g[name], self.shards[name].shape[-1])

    def _landed(self, stage, outs):
        if stage in self.SCATTERS:
            self.parts.update(zip(self.SCATTERS[stage], outs))
            return
        w = self.wts
        cols = lambda t: t.transpose(1, 0, 2).reshape(t.shape[1], N_DEV * t.shape[2])
        rows = lambda t: t.reshape(N_DEV * t.shape[1], t.shape[2])
        for name, t in zip(self.GATHERS[stage], outs):
            if name in ("ret_out", "dn_out") or name.startswith("down"):
                w[name] = rows(t)
            elif name == "dn_in":
                full = cols(t)
                n_main = DN_CONV_CH + DN_V
                w["dn_main"] = full[:, :n_main]
                w["dn_ba"] = jnp.pad(full[:, n_main:], ((0, 0), (0, BA_W - (full.shape[1] - n_main))))
            else:
                w[name] = cols(t)

    def matmul(self, stage, a, b, **kw):
        comm = self._exchange(stage)
        if comm is None:
            return _matmul(a, b, name=stage, **kw)
        out, landed = _matmul(a, b, name=stage, comm=comm, **kw)
        self._landed(stage, landed)
        return out

    def call(self, stage, fn, *args, n_out):
        comm = self._exchange(stage)
        out = fn(*args, stage, comm=comm)
        if comm is not None:
            self._landed(stage, out[n_out:])
        return out[:n_out]


def _local_step(x2, target, wts, shards=None):
    plan = _Plan(shards, wts)
    s_len, d = x2.shape
    l = s_len + CHUNK
    tb = _tile(l, 3072)
    th = tb // 2 if (tb // 2) % 16 == 0 else tb
    half = RET_DK // 2
    inv_freq = (np.float32(ROPE_BASE) ** (-np.arange(half, dtype=np.float32) / np.float32(half))).astype(np.float32)
    ang = (np.arange(l) - PAD).astype(np.float32)[:, None] * inv_freq[None, :]
    cos, sin = jnp.asarray(np.cos(ang), f32), jnp.asarray(np.sin(ang), f32)
    lgs = jnp.log1p(-jnp.exp2(-5.0 - jnp.arange(RET_HEADS, dtype=f32)))
    gcs = jnp.exp(lgs * _ret_block(l))

    mixw, ffnw = wts["mix_norm"], wts["ffn_norm"]

    h0, hn0 = _embed_norm(x2, wts["meta"], mixw[0:1], "l0_norm")
    proj0 = plan.matmul("ret_proj", hn0, wts["ret_in"], mode="nn", tm=tb, tn=512, tk=d)
    (qk0,) = plan.call("ret_prep", _ret_prep, proj0, cos, sin, n_out=1)
    o0, st0, y0 = plan.call("ret_scan", _ret_scan_fwd, qk0, proj0, wts["ret_gn"], lgs, gcs, n_out=3)
    h1, hn1 = _matmul(y0, wts["ret_out"], mode="nn", tm=th // 2, tn=d, tk=RET_V, res=h0, norm_w=ffnw[0:1],
                      name="ret_out")
    (h2, hn2), ffn0 = _ffn_fwd(h1, hn1, wts["gate0"], wts["up0"], wts["down0"], tb, th, "ffn0", plan,
                               next_norm_w=mixw[1:2])

    proj1 = plan.matmul("dn_proj", hn2, wts["dn_main"], mode="nn", tm=tb, tn=512, tk=d)
    ba = _matmul(hn2, wts["dn_ba"], mode="nn", tm=tb, tn=BA_W, tk=d, name="dn_proj_ba")
    (qkv1,) = plan.call("dn_conv", _dn_conv_fwd, proj1, wts["conv_w"], n_out=1)
    tinv1, u1, wq1, pk1, egl1, kpt1, qwt1 = plan.call("dn_prep", _dn_prep, qkv1, ba, wts["a_log"], wts["dt_bias"],
                                                      n_out=7)
    o1, st1, vn1 = _dn_scan_fwd(u1, wq1, pk1, egl1, "dn_scan")
    y1 = _gnorm_fwd(o1, proj1, wts["dn_norm"], DN_HEADS, DN_DV, 2, "dn_gnorm")
    h3, hn3 = _matmul(y1, wts["dn_out"], mode="nn", tm=th // 2, tn=d, tk=DN_V, res=h2, norm_w=ffnw[1:2],
                      name="dn_out")
    h4, ffn1 = _ffn_fwd(h3, hn3, wts["gate1"], wts["up1"], wts["down1"], tb, th, "ffn1", plan)

    dh4, dh4b, dfinal, loss = _final_loss(h4, wts["final_norm"], target, "final_loss")
    gr = plan.grads
    dh3, dh3b, dffn1 = _ffn_bwd(dh4, dh4b, h3, ffnw[1:2], wts["gate1"], wts["up1"], wts["down1"], ffn1,
                                tb, th, "ffn1", plan)

    gr["dn_out"] = _matmul(y1, dh3b, mode="tn", tm=1024, tn=d, tk=tb, out_dtype=bf16, name="dn_b_dwout")
    do1, dproj1, ddn_norm = _dy_gnorm_bwd(dh3b, wts["dn_out"], o1, proj1, wts["dn_norm"], "dn_b_gnorm",
                                          dv=DN_DV, gate_blk=2, tm=th // 2, tn=1024)
    dvn1, dsp1 = _dn_scan_bwd(do1, kpt1, qwt1, egl1, "dn_b_scan")
    dqkv1, dba, dalog, ddt = _dn_post_bwd(qkv1, ba, wts["a_log"], wts["dt_bias"], st1, dsp1, tinv1, u1, wq1, vn1,
                                          do1, dvn1, "dn_b_post")
    dc1, dconv = plan.call("dn_b_conv_a", _dn_conv_bwd_a, proj1, wts["conv_w"], dqkv1, n_out=2)
    dproj1 = _dn_conv_bwd_b(dc1, wts["conv_w"], dproj1, "dn_b_conv_b")
    dbab = dba.astype(bf16)
    n_main = dproj1.shape[1]
    dhn2_ba = _matmul(dbab, wts["dn_ba"], mode="nt", tm=th, tn=d, tk=BA_W, name="dn_b_dhn_ba")
    dh2, dh2b, dmix1 = plan.call("dn_b_dhn", functools.partial(_dhn_norm_bwd, tm=th // 2, tk=n_main // 4,
                                                              init=dhn2_ba),
                                 [(dproj1, wts["dn_main"])], h2, mixw[1:2], dh3, n_out=3)
    dw_main = _matmul(hn2, dproj1, mode="tn", tm=d, tn=512, tk=tb, out_dtype=bf16, name="dn_b_dwin")
    dw_ba = _matmul(hn2, dbab, mode="tn", tm=d, tn=BA_W, tk=tb, out_dtype=bf16, name="dn_b_dwin_ba")
    gr["dn_in"] = jnp.concatenate([dw_main, dw_ba], axis=1)

    dh1, dh1b, dffn0 = _ffn_bwd(dh2, dh2b, h1, ffnw[0:1], wts["gate0"], wts["up0"], wts["down0"], ffn0,
                                tb, th, "ffn0", plan)

    gr["ret_out"] = _matmul(y0, dh1b, mode="tn", tm=1024, tn=d, tk=tb, out_dtype=bf16, name="ret_b_dwout")
    do0, dproj0, dret_gn = plan.call(
        "ret_b_gnorm", functools.partial(_dy_gnorm_bwd, dv=RET_DV, gate_blk=2, tm=th // 2, tn=1024),
        dh1b, wts["ret_out"], o0, proj0, wts["ret_gn"], n_out=3)
    dq0, dk0, dproj0 = plan.call("ret_b_scan", _ret_scan_bwd, qk0, proj0, st0, do0, dproj0, lgs, gcs, n_out=3)
    dproj0 = _ret_prep_bwd(dq0, dk0, cos, sin, dproj0, "ret_b_prep")
    n_in = dproj0.shape[1]
    gr["ret_in"] = plan.matmul("ret_b_dwin", hn0, dproj0, mode="tn", tm=d, tn=512, tk=tb, out_dtype=bf16)
    dh0, _, dmix0 = plan.call("ret_b_dhn", functools.partial(_dhn_norm_bwd, tm=th // 2, tk=n_in // 4),
                              [(dproj0, wts["ret_in"])], h0, mixw[0:1], dh1, n_out=3)

    gr.update(meta=dh0[PAD:CHUNK], mix_norm=jnp.concatenate([dmix0, dmix1], axis=0),
              ffn_norm=jnp.concatenate([dffn0, dffn1], axis=0), ret_gn=dret_gn, conv_w=dconv, a_log=dalog,
              dt_bias=ddt, dn_norm=ddn_norm, final_norm=dfinal)
    return loss, dh0[CHUNK:], gr, plan


def _adamw_reduce(parts, w, m, v, name):
    _, r, c = parts.shape
    c_pad = -(-c // LANES) * LANES
    tr = _div_tile(r, max(8, (3 * MIB // 16) // c_pad // 8 * 8), 16)

    def body(p_ref, w_ref, m_ref, v_ref, g_ref, d_ref, nm_ref, nv_ref):
        g = p_ref[0].astype(f32)
        for s in range(1, N_DEV):
            g = g + p_ref[s].astype(f32)
        mm = ADAM_B1 * m_ref[...] + (1.0 - ADAM_B1) * g
        vv = ADAM_B2 * v_ref[...] + (1.0 - ADAM_B2) * (g * g)
        m_hat = mm / (1.0 - ADAM_B1 ** ADAM_STEP)
        v_hat = vv / (1.0 - ADAM_B2 ** ADAM_STEP)
        g_ref[...] = g
        d_ref[...] = -ADAM_LR * (m_hat / (jnp.sqrt(v_hat) + ADAM_EPS) + ADAM_WD * w_ref[...])
        nm_ref[...] = mm
        nv_ref[...] = vv

    blk = pl.BlockSpec((tr, c), lambda i: (i, 0))
    return pl.pallas_call(
        body, name=name, grid=(r // tr,),
        in_specs=[pl.BlockSpec((N_DEV, tr, c), lambda i: (0, i, 0)), blk, blk, blk], out_specs=[blk] * 4,
        out_shape=[jax.ShapeDtypeStruct((r, c), f32)] * 4,
        compiler_params=_params(("parallel",), 48),
    )(parts, w, m, v)


def _dev_major_cols(g, width):
    r = g.shape[0]
    return g[:, :N_DEV * width].reshape(r, N_DEV, width).transpose(1, 0, 2)


def kernel(x, meta_tokens, mix_norm_w, ffn_norm_w, ret_w_in, ret_gn_w, ret_w_out, dn_w_in, dn_conv_w, dn_a_log, dn_dt_bias, dn_norm_w, dn_w_out, ffn_w_gate, ffn_w_up, ffn_w_down, final_norm_w, loss_target, m_meta_tokens, m_mix_norm_w, m_ffn_norm_w, m_ret_w_in, m_ret_gn_w, m_ret_w_out, m_dn_w_in, m_dn_conv_w, m_dn_a_log, m_dn_dt_bias, m_dn_norm_w, m_dn_w_out, m_ffn_w_gate, m_ffn_w_up, m_ffn_w_down, m_final_norm_w, v_meta_tokens, v_mix_norm_w, v_ffn_norm_w, v_ret_w_in, v_ret_gn_w, v_ret_w_out, v_dn_w_in, v_dn_conv_w, v_dn_a_log, v_dn_dt_bias, v_dn_norm_w, v_dn_w_out, v_ffn_w_gate, v_ffn_w_up, v_ffn_w_down, v_final_norm_w):
    d = x.shape[-1]
    me = 4 * lax.axis_index("x") + 2 * lax.axis_index("y") + lax.axis_index("c")

    shards = dict(ret_in=ret_w_in[0].astype(bf16), ret_out=ret_w_out[0].astype(bf16),
                  dn_in=dn_w_in[0].astype(bf16), dn_out=dn_w_out[0].astype(bf16))
    for layer in (0, 1):
        shards[f"gate{layer}"] = ffn_w_gate[layer].astype(bf16)
        shards[f"up{layer}"] = ffn_w_up[layer].astype(bf16)
        shards[f"down{layer}"] = ffn_w_down[layer].astype(bf16)
    g_ret_in, g_meta, g_conv, g_dnn = _exchange([shards["ret_in"], meta_tokens, dn_conv_w[0], dn_norm_w], True,
                                                "gather_first")
    cols = lambda g: g.transpose(1, 0, 2).reshape(g.shape[1], N_DEV * g.shape[2])
    wts = dict(meta=cols(g_meta), mix_norm=mix_norm_w, ffn_norm=ffn_norm_w, ret_in=cols(g_ret_in), ret_gn=ret_gn_w,
               conv_w=cols(g_conv), a_log=dn_a_log, dt_bias=dn_dt_bias, dn_norm=cols(g_dnn),
               final_norm=final_norm_w.reshape(1, d))

    loss_part, grad_x, gr, plan = _local_step(x[0], loss_target[0], wts, shards)
    loss = lax.psum(loss_part[0, 0], AXES)

    pp = plan.parts
    both = lambda name: jnp.concatenate([pp[name + "0"], pp[name + "1"]], axis=1)
    big_parts = [pp["ret_in"], pp["ret_out"], pp["dn_in"], pp["dn_out"], both("gate"), both("up"), both("down")]
    big_names = ["ret_w_in", "ret_w_out", "dn_w_in", "dn_w_out", "ffn_w_gate", "ffn_w_up", "ffn_w_down"]
    big_w = [ret_w_in, ret_w_out, dn_w_in, dn_w_out, ffn_w_gate, ffn_w_up, ffn_w_down]
    big_m = [m_ret_w_in, m_ret_w_out, m_dn_w_in, m_dn_w_out, m_ffn_w_gate, m_ffn_w_up, m_ffn_w_down]
    big_v = [v_ret_w_in, v_ret_w_out, v_dn_w_in, v_dn_w_out, v_ffn_w_gate, v_ffn_w_up, v_ffn_w_down]
    res = {}
    for nm, parts, w_, m_, v_ in zip(big_names, big_parts, big_w, big_m, big_v):
        r2, c2 = parts.shape[1], parts.shape[2]
        outs = _adamw_reduce(parts, w_.reshape(r2, c2), m_.reshape(r2, c2), v_.reshape(r2, c2), f"adamw_{nm}")
        res[nm] = [o.reshape(w_.shape) for o in outs]

    small_names = ["meta_tokens", "mix_norm_w", "ffn_norm_w", "ret_gn_w", "dn_conv_w", "dn_a_log", "dn_dt_bias",
                   "dn_norm_w", "final_norm_w"]
    small_g = [gr["meta"], gr["mix_norm"], gr["ffn_norm"], gr["ret_gn"], gr["conv_w"], gr["a_log"], gr["dt_bias"],
               gr["dn_norm"], gr["final_norm"]]
    small_w = [meta_tokens, mix_norm_w, ffn_norm_w, ret_gn_w, dn_conv_w, dn_a_log, dn_dt_bias, dn_norm_w, final_norm_w]
    small_m = [m_meta_tokens, m_mix_norm_w, m_ffn_norm_w, m_ret_gn_w, m_dn_conv_w, m_dn_a_log, m_dn_dt_bias,
               m_dn_norm_w, m_final_norm_w]
    small_v = [v_meta_tokens, v_mix_norm_w, v_ffn_norm_w, v_ret_gn_w, v_dn_conv_w, v_dn_a_log, v_dn_dt_bias,
               v_dn_norm_w, v_final_norm_w]
    sharded = {"meta_tokens", "dn_conv_w", "dn_norm_w"}
    flat = jnp.concatenate([g.reshape(-1) for g in small_g])
    row = 8 * LANES
    n_flat = flat.shape[0]
    flat = jnp.pad(flat, (0, -n_flat % row)).reshape(-1, row)
    (gathered,) = _exchange([flat], True, "gather_small_grads")
    gathered = gathered.reshape(N_DEV, -1)
    pieces, off = [], 0
    for nm, g, w_ in zip(small_names, small_g, small_w):
        full = gathered[:, off:off + g.size].reshape((N_DEV,) + g.shape)
        off += g.size
        if nm in sharded:
            wloc = w_.shape[-1]
            full = lax.dynamic_slice_in_dim(full, me * wloc, wloc, axis=full.ndim - 1)
        pieces.append(full.reshape(N_DEV, -1))
    sizes = [p.shape[1] for p in pieces]
    n_loc = sum(sizes)
    pad_loc = -n_loc % row

    def pack(vs, lead):
        cat = jnp.concatenate([a.reshape(lead + (-1,)) for a in vs], axis=-1)
        cat = jnp.pad(cat, [(0, 0)] * len(lead) + [(0, pad_loc)])
        return cat.reshape(lead + (-1, row))

    outs = _adamw_reduce(pack(pieces, (N_DEV,)), pack(small_w, ()), pack(small_m, ()), pack(small_v, ()), "adamw_small")
    off = 0
    for nm, sz, w_ in zip(small_names, sizes, small_w):
        res[nm] = [o.reshape(-1)[off:off + sz].reshape(w_.shape) for o in outs]
        off += sz

    order = ["meta_tokens", "mix_norm_w", "ffn_norm_w", "ret_w_in", "ret_gn_w", "ret_w_out", "dn_w_in", "dn_conv_w",
             "dn_a_log", "dn_dt_bias", "dn_norm_w", "dn_w_out", "ffn_w_gate", "ffn_w_up", "ffn_w_down", "final_norm_w"]
    grad_x = grad_x.reshape(x.shape)
    return (loss, grad_x, *[res[nm][0] for nm in order], *[res[nm][1] for nm in order],
            *[res[nm][2] for nm in order], *[res[nm][3] for nm in order])
```

```python
import functools
import math

import jax
import jax.numpy as jnp
import numpy as np
from jax import lax
from jax.experimental import pallas as pl
from jax.experimental.pallas import tpu as pltpu

f32 = jnp.float32
bf16 = jnp.bfloat16

N_META = 16
CHUNK = 64
PAD = CHUNK - N_META
RMS_EPS = 1e-6
RET_HEADS, RET_DK, RET_DV = 4, 256, 512
RET_QK, RET_V = RET_HEADS * RET_DK, RET_HEADS * RET_DV
DN_HEADS, DN_DK, DN_DV = 8, 128, 256
DN_QK, DN_V = DN_HEADS * DN_DK, DN_HEADS * DN_DV
DN_CONV_CH = 2 * DN_QK + DN_V
CONV_K = 4
ROPE_BASE = 10000.0
ADAM_LR, ADAM_B1, ADAM_B2, ADAM_EPS, ADAM_WD, ADAM_STEP = 0.001, 0.9, 0.999, 1e-08, 0.01, 10
N_DEV = 8
AXES = ("x", "y", "c")
LANES = 128
MIB = 1024 * 1024


def _tile(n_rows, cap):
    nch = n_rows // CHUNK
    best = 1
    for d in range(1, nch + 1):
        if nch % d == 0 and d * CHUNK <= cap:
            best = d
    return best * CHUNK


def _div_tile(n, cap, align):
    best = None
    for d in range(align, min(n, cap) + 1, align):
        if n % d == 0:
            best = d
    return best if best is not None else n


def _params(sem, vmem_mb):
    return pltpu.CompilerParams(dimension_semantics=sem, vmem_limit_bytes=int(vmem_mb * MIB))


def _nn(a, b, precision=None):
    return jnp.dot(a, b, preferred_element_type=f32, precision=precision)


def _nt(a, b, precision=None):
    return lax.dot_general(a, b, (((1,), (1,)), ((), ())), preferred_element_type=f32, precision=precision)


def _tn(a, b, precision=None):
    return lax.dot_general(a, b, (((0,), (0,)), ((), ())), preferred_element_type=f32, precision=precision)


def _b(x):
    return x.astype(bf16)


def _sigmoid(x):
    return 0.5 * jnp.tanh(0.5 * x) + 0.5


def _silu(x):
    return x * _sigmoid(x)


def _dsilu(x):
    s = _sigmoid(x)
    return s * (1.0 + x * (1.0 - s))


def _peer(k):
    x, y, c = lax.axis_index("x"), lax.axis_index("y"), lax.axis_index("c")
    px = 1 - x if k & 4 else x
    py = 1 - y if k & 2 else y
    pc = 1 - c if k & 1 else c
    return (px, py, pc), 4 * px + 2 * py + pc


class _Exchange:
    def __init__(self, arrs, gather):
        self.arrs, self.gather, self.n = list(arrs), gather, len(arrs)
        self.out_shapes = [jax.ShapeDtypeStruct(((N_DEV,) + a.shape) if gather else a.shape, a.dtype) for a in arrs]
        self.specs = [pl.BlockSpec(memory_space=pltpu.HBM)] * self.n
        self.scratch = [pltpu.SemaphoreType.DMA((self.n, N_DEV - 1)), pltpu.SemaphoreType.DMA((self.n, N_DEV - 1)),
                        pltpu.SemaphoreType.DMA((self.n,))]

    def _copies(self, ins, outs, sems):
        send_sems, recv_sems, local_sems = sems
        x, y, c = lax.axis_index("x"), lax.axis_index("y"), lax.axis_index("c")
        me = 4 * x + 2 * y + c

        def copy(a, k, src, slot, to):
            return pltpu.make_async_remote_copy(
                src_ref=src, dst_ref=outs[a].at[slot], send_sem=send_sems.at[a, k], recv_sem=recv_sems.at[a, k],
                device_id=to, device_id_type=pl.DeviceIdType.MESH)

        first, passed, lands_first, lands_rest = [], [], [], []
        if not self.gather:
            local = [pltpu.make_async_copy(ins[a].at[me], outs[a].at[me], local_sems.at[a]) for a in range(self.n)]
            for k in range(1, N_DEV):
                peer, pidx = _peer(k)
                for a in range(self.n):
                    first.append(copy(a, k - 1, ins[a].at[pidx], me, peer))
                    lands_rest.append(copy(a, k - 1, ins[a].at[pidx], pidx, peer))
            return local, first, passed, lands_first, lands_rest
        local = [pltpu.make_async_copy(ins[a], outs[a].at[me], local_sems.at[a]) for a in range(self.n)]
        sibling, sibling_slot = (x, y, 1 - c), 4 * x + 2 * y + (1 - c)
        chips = [(1 - x, y), (x, 1 - y), (1 - x, 1 - y)]
        for a in range(self.n):
            first.append(copy(a, 0, ins[a], me, sibling))
            lands_rest.append(copy(a, 0, ins[a], sibling_slot, sibling))
            for j, (px, py) in enumerate(chips):
                slot, slot_other = 4 * px + 2 * py + c, 4 * px + 2 * py + (1 - c)
                first.append(copy(a, 1 + j, ins[a], me, (px, py, c)))
                lands_first.append(copy(a, 1 + j, ins[a], slot, (px, py, c)))
                passed.append(copy(a, 4 + j, outs[a].at[slot], slot, sibling))
                lands_rest.append(copy(a, 4 + j, outs[a].at[slot_other], slot_other, sibling))
        return local, first, passed, lands_first, lands_rest

    def start(self, ins, outs, sems):
        local, first, _, _, _ = self._copies(ins, outs, sems)
        for cp in local + first:
            cp.start()

    def wait(self, ins, outs, sems):
        local, first, passed, lands_first, lands_rest = self._copies(ins, outs, sems)
        for landed, onward in zip(lands_first, passed):
            landed.wait_recv()
            onward.start()
        for cp in lands_rest:
            cp.wait_recv()
        for cp in first + passed:
            cp.wait_send()
        for cp in local:
            cp.wait()


def _call(body, args, *, name, grid, in_specs, out_specs, out_shape, scratch=(), sem, vmem_mb, comm=None,
          aliases=None):
    aliases = aliases or {}
    if comm is None:
        out = pl.pallas_call(body, name=name, grid=grid, in_specs=list(in_specs), out_specs=list(out_specs),
                             out_shape=list(out_shape), scratch_shapes=list(scratch), input_output_aliases=aliases,
                             compiler_params=_params(sem, vmem_mb))(*args)
        return list(out)
    n_in, n_out, n_scr, nc = len(args), len(out_shape), len(scratch), comm.n

    def carried(*refs):
        ins, cin = refs[:n_in], refs[n_in:n_in + nc]
        o0 = n_in + nc
        outs, cout = refs[o0:o0 + n_out], refs[o0 + n_out:o0 + n_out + nc]
        s0 = o0 + n_out + nc
        scr, sems = refs[s0:s0 + n_scr], refs[s0 + n_scr:]
        first = functools.reduce(jnp.logical_and, [pl.program_id(i) == 0 for i in range(len(grid))])
        last = functools.reduce(jnp.logical_and, [pl.program_id(i) == grid[i] - 1 for i in range(len(grid))])

        @pl.when(first)
        def _():
            comm.start(cin, cout, sems)

        body(*ins, *outs, *scr)

        @pl.when(last)
        def _():
            comm.wait(cin, cout, sems)

    out = pl.pallas_call(
        carried, name=name, grid=grid, in_specs=list(in_specs) + comm.specs, out_specs=list(out_specs) + comm.specs,
        out_shape=list(out_shape) + comm.out_shapes, scratch_shapes=list(scratch) + comm.scratch,
        input_output_aliases=aliases,
        compiler_params=_params(("arbitrary",) * len(grid), vmem_mb))(*args, *comm.arrs)
    return list(out)


def _exchange(arrs, gather, name):
    comm = _Exchange(arrs, gather)

    def body(*refs):
        ins, outs, sems = refs[:comm.n], refs[comm.n:2 * comm.n], refs[2 * comm.n:]
        comm.start(ins, outs, sems)
        comm.wait(ins, outs, sems)

    return pl.pallas_call(body, name=name, in_specs=comm.specs, out_specs=comm.specs, out_shape=comm.out_shapes,
                          scratch_shapes=comm.scratch)(*comm.arrs)


def _matmul(a, b, *, mode, tm, tn, tk, name, out_dtype=f32, res=None, vmem_mb=48, comm=None, norm_w=None,
            resident_a=False):
    if mode == "nn":
        (m, k), (k2, n) = a.shape, b.shape
    elif mode == "nt":
        (m, k), (n, k2) = a.shape, b.shape
    else:
        (k, m), (k2, n) = a.shape, b.shape
    assert k == k2 and m % tm == 0 and n % tn == 0 and k % tk == 0, (name, a.shape, b.shape, tm, tn, tk)
    nk = k // tk
    dot = {"nn": _nn, "nt": _nt, "tn": _tn}[mode]
    a_spec = {"nn": pl.BlockSpec((tm, tk), lambda i, j, kk: (i, kk)),
              "nt": pl.BlockSpec((tm, tk), lambda i, j, kk: (i, kk)),
              "tn": pl.BlockSpec((tk, tm), lambda i, j, kk: (kk, i))}[mode]
    if resident_a:
        assert mode == "tn" and nk == 1
        a_spec = pl.BlockSpec((tk, tm), lambda i, j, kk: (0, i), pipeline_mode=pl.Buffered(1))
    b_spec = {"nn": pl.BlockSpec((tk, tn), lambda i, j, kk: (kk, j)),
              "nt": pl.BlockSpec((tn, tk), lambda i, j, kk: (j, kk)),
              "tn": pl.BlockSpec((tk, tn), lambda i, j, kk: (kk, j))}[mode]
    o_spec = pl.BlockSpec((tm, tn), lambda i, j, kk: (i, j))
    has_res = res is not None
    has_norm = norm_w is not None
    assert not has_norm or tn == n
    n_ops = 2 + has_res + has_norm

    def body(*refs):
        a_ref, b_ref = refs[:2]
        r_ref = refs[2] if has_res else None
        nw_ref = refs[2 + has_res] if has_norm else None
        o_ref = refs[n_ops]
        hn_ref = refs[n_ops + 1] if has_norm else None
        rest = refs[n_ops + 1 + has_norm:]

        def finish(tot):
            if has_res:
                tot = tot + r_ref[...]
            o_ref[...] = tot.astype(out_dtype)
            if has_norm:
                r = lax.rsqrt(jnp.mean(tot * tot, axis=-1, keepdims=True) + RMS_EPS)
                hn_ref[...] = _b(tot * r * nw_ref[...])

        if nk == 1:
            finish(dot(_b(a_ref[...]), _b(b_ref[...])))
            return
        acc_ref = rest[0]
        kk = pl.program_id(2)

        @pl.when(kk == 0)
        def _():
            acc_ref[...] = dot(_b(a_ref[...]), _b(b_ref[...]))

        @pl.when(kk > 0)
        def _():
            acc_ref[...] += dot(_b(a_ref[...]), _b(b_ref[...]))

        @pl.when(kk == nk - 1)
        def _():
            finish(acc_ref[...])

    in_specs = [a_spec, b_spec]
    args = (a, b)
    if has_res:
        in_specs.append(o_spec)
        args += (res,)
    out_specs, out_shape = [o_spec], [jax.ShapeDtypeStruct((m, n), out_dtype)]
    if has_norm:
        in_specs.append(pl.BlockSpec((1, tn), lambda i, j, kk: (0, j)))
        args += (norm_w,)
        out_specs.append(o_spec)
        out_shape.append(jax.ShapeDtypeStruct((m, n), bf16))
    out = _call(body, args, name=name, grid=(m // tm, n // tn, nk), in_specs=in_specs, out_specs=out_specs,
                out_shape=out_shape, scratch=[pltpu.VMEM((tm, tn), f32)] if nk > 1 else [],
                sem=("parallel", "parallel", "arbitrary"), vmem_mb=vmem_mb, comm=comm)
    n_own = len(out_shape)
    own = out[0] if n_own == 1 else tuple(out[:n_own])
    return own if comm is None else (own, out[n_own:])


def _embed_norm(x2, meta, w, name):
    s_len, d = x2.shape
    l = s_len + CHUNK
    cpt = _tile(l, 256) // CHUNK

    def body(m_ref, w_ref, *rest):
        x_refs, (h_ref, hn_ref) = rest[:cpt], rest[cpt:]
        i = pl.program_id(0)
        prefix = jnp.concatenate([jnp.zeros((PAD, d), f32), m_ref[...]], axis=0)
        wv = w_ref[...]
        for c in range(cpt):
            rows = slice(c * CHUNK, (c + 1) * CHUNK)
            x = jnp.where(i * cpt + c > 0, x_refs[c][...], prefix)
            h_ref[rows, :] = x
            r = lax.rsqrt(jnp.mean(x * x, axis=-1, keepdims=True) + RMS_EPS)
            hn_ref[rows, :] = _b(x * r * wv)

    row = pl.BlockSpec((cpt * CHUNK, d), lambda i: (i, 0))
    x_specs = [pl.BlockSpec((CHUNK, d), functools.partial(lambda i, c: (jnp.maximum(i * cpt + c - 1, 0), 0), c=c))
               for c in range(cpt)]
    return pl.pallas_call(
        body, name=name, grid=(l // (cpt * CHUNK),),
        in_specs=[pl.BlockSpec((N_META, d), lambda i: (0, 0)), pl.BlockSpec((1, d), lambda i: (0, 0))] + x_specs,
        out_specs=[row, row],
        out_shape=[jax.ShapeDtypeStruct((l, d), f32), jax.ShapeDtypeStruct((l, d), bf16)],
        compiler_params=_params(("parallel",), 32),
    )(meta, w, *([x2] * cpt))


def _dhn_norm_bwd(pairs, h, nw, dres, name, *, tm, tk, init=None, comm=None):
    l, d = h.shape
    nks = [a.shape[1] // tk for a, _ in pairs]
    starts = [sum(nks[:p]) for p in range(len(pairs))]
    nk = sum(nks)
    assert nk >= 2
    n_ops = 2 * len(pairs)
    has_init = init is not None

    def body(*refs):
        ops = refs[:n_ops]
        init_ref = refs[n_ops] if has_init else None
        h_ref, w_ref, r_ref, dh_ref, dhb_ref, dw_ref, acc_ref = refs[n_ops + has_init:]
        i, kk = pl.program_id(0), pl.program_id(1)

        @pl.when(kk == 0)
        def _():
            part = _nt(ops[0][...], ops[1][...])
            acc_ref[...] = part + init_ref[...] if has_init else part

        for p in range(len(pairs)):
            lo, hi = max(starts[p], 1), min(starts[p] + nks[p], nk - 1)

            @pl.when(jnp.logical_and(kk >= lo, kk < hi))
            def _(a_ref=ops[2 * p], b_ref=ops[2 * p + 1]):
                acc_ref[...] += _nt(a_ref[...], b_ref[...])

        @pl.when(kk == nk - 1)
        def _():
            g = acc_ref[...] + _nt(ops[-2][...], ops[-1][...])
            x = h_ref[...]
            r = lax.rsqrt(jnp.mean(x * x, axis=-1, keepdims=True) + RMS_EPS)
            xh = x * r
            dxh = g * w_ref[...]
            dh = r_ref[...] + r * (dxh - xh * jnp.mean(dxh * xh, axis=-1, keepdims=True))
            dh_ref[...] = dh
            dhb_ref[...] = _b(dh)
            dw = jnp.sum(g * xh, axis=0, keepdims=True)

            @pl.when(i == 0)
            def _():
                dw_ref[...] = dw

            @pl.when(i > 0)
            def _():
                dw_ref[...] += dw

    def k_of(p):
        return lambda kk: jnp.clip(kk - starts[p], 0, nks[p] - 1)

    in_specs, args = [], []
    for p, (a, b) in enumerate(pairs):
        in_specs += [pl.BlockSpec((tm, tk), functools.partial(lambda i, kk, f: (i, f(kk)), f=k_of(p))),
                     pl.BlockSpec((d, tk), functools.partial(lambda i, kk, f: (0, f(kk)), f=k_of(p)))]
        args += [a, b]
    row = pl.BlockSpec((tm, d), lambda i, kk: (i, 0))
    vec = pl.BlockSpec((1, d), lambda i, kk: (0, 0))
    if has_init:
        in_specs.append(row)
        args.append(init)
    return _call(body, tuple(args) + (h, nw, dres), name=name, grid=(l // tm, nk), in_specs=in_specs + [row, vec, row],
                 out_specs=[row, row, vec],
                 out_shape=[jax.ShapeDtypeStruct((l, d), f32), jax.ShapeDtypeStruct((l, d), bf16),
                            jax.ShapeDtypeStruct((1, d), f32)],
                 scratch=[pltpu.VMEM((tm, d), f32)], sem=("arbitrary", "arbitrary"), vmem_mb=48, comm=comm)


def _final_loss(h, w, target, name):
    l, d = h.shape
    nch = l // CHUNK
    cpt = _tile(l, 256) // CHUNK
    nt = nch // cpt

    def body(h_ref, w_ref, *rest):
        t_refs, (dh_ref, dhb_ref, dw_ref, loss_ref) = rest[:cpt], rest[cpt:]
        i = pl.program_id(0)
        wv = w_ref[...]
        dw = jnp.zeros((1, d), f32)
        part = jnp.zeros((1, 1), f32)
        for c in range(cpt):
            rows = slice(c * CHUNK, (c + 1) * CHUNK)
            live = (i * cpt + c > 0).astype(f32)
            x = h_ref[rows, :]
            r = lax.rsqrt(jnp.mean(x * x, axis=-1, keepdims=True) + RMS_EPS)
            xh = x * r
            err = (xh * wv - t_refs[c][...]) * live
            dy = err * (1.0 / d)
            dxh = dy * wv
            dx = r * (dxh - xh * jnp.mean(dxh * xh, axis=-1, keepdims=True))
            dh_ref[rows, :] = dx
            dhb_ref[rows, :] = _b(dx)
            dw = dw + jnp.sum(dy * xh, axis=0, keepdims=True)
            part = part + 0.5 * jnp.sum(jnp.sum(err * err, axis=-1, keepdims=True) * (1.0 / d), axis=0, keepdims=True)
        part = jnp.broadcast_to(part, (1, LANES))

        @pl.when(i == 0)
        def _():
            dw_ref[...] = dw
            loss_ref[...] = part

        @pl.when(i > 0)
        def _():
            dw_ref[...] += dw
            loss_ref[...] += part

    row = pl.BlockSpec((cpt * CHUNK, d), lambda i: (i, 0))
    vec = pl.BlockSpec((1, d), lambda i: (0, 0))
    t_specs = [pl.BlockSpec((CHUNK, d), functools.partial(lambda i, c: (jnp.maximum(i * cpt + c - 1, 0), 0), c=c))
               for c in range(cpt)]
    return pl.pallas_call(
        body, name=name, grid=(nt,),
        in_specs=[row, vec] + t_specs,
        out_specs=[row, row, vec, pl.BlockSpec((1, LANES), lambda i: (0, 0))],
        out_shape=[jax.ShapeDtypeStruct((l, d), f32), jax.ShapeDtypeStruct((l, d), bf16),
                   jax.ShapeDtypeStruct((1, d), f32), jax.ShapeDtypeStruct((1, LANES), f32)],
        compiler_params=_params(("arbitrary",), 32),
    )(h, w, *([target] * cpt))


def _ffn_gu(hn, wg, wu, name, *, tm, tn, comm=None):
    l, d = hn.shape
    fh = wg.shape[1]

    def body(h_ref, g_ref, u_ref, a_ref, b_ref, s_ref):
        hb = h_ref[...]
        a = _nn(hb, g_ref[...])
        bb = _nn(hb, u_ref[...])
        a_ref[...] = _b(a)
        b_ref[...] = _b(bb)
        s_ref[...] = _b(_silu(a) * bb)

    wspec = pl.BlockSpec((d, tn), lambda i, j: (0, j))
    ospec = pl.BlockSpec((tm, tn), lambda i, j: (i, j))
    return _call(body, (hn, wg, wu), name=name, grid=(l // tm, fh // tn),
                 in_specs=[pl.BlockSpec((tm, d), lambda i, j: (i, 0)), wspec, wspec], out_specs=[ospec] * 3,
                 out_shape=[jax.ShapeDtypeStruct((l, fh), bf16)] * 3, sem=("parallel", "parallel"), vmem_mb=48,
                 comm=comm)


def _ffn_ds(dhb, wd, a, b, *, tm, tn, name):
    l, d = dhb.shape
    fh = wd.shape[0]

    def body(g_ref, w_ref, a_ref, b_ref, da_ref, db_ref):
        ds = _nt(g_ref[...], w_ref[...])
        a = a_ref[...].astype(f32)
        da_ref[...] = _b(ds * b_ref[...].astype(f32) * _dsilu(a))
        db_ref[...] = _b(ds * _silu(a))

    ospec = pl.BlockSpec((tm, tn), lambda i, j: (i, j))
    return pl.pallas_call(
        body, name=name, grid=(l // tm, fh // tn),
        in_specs=[pl.BlockSpec((tm, d), lambda i, j: (i, 0)), pl.BlockSpec((tn, d), lambda i, j: (j, 0)), ospec, ospec],
        out_specs=[ospec, ospec], out_shape=[jax.ShapeDtypeStruct((l, fh), bf16)] * 2,
        compiler_params=_params(("parallel", "parallel"), 48),
    )(dhb, wd, a, b)


def _gnorm_fwd(o, proj, nw, heads, dv, gate_blk, name):
    l, hv = o.shape
    tr = _tile(l, 256)

    def body(o_ref, g_ref, w_ref, y_ref):
        wv = w_ref[...]
        for h in range(heads):
            sl = slice(h * dv, (h + 1) * dv)
            oh = o_ref[:, sl]
            r = lax.rsqrt(jnp.mean(oh * oh, axis=-1, keepdims=True) + RMS_EPS)
            y_ref[:, sl] = _b(oh * r * wv * _silu(g_ref[:, sl]))

    return pl.pallas_call(
        body, name=name, grid=(l // tr,),
        in_specs=[pl.BlockSpec((tr, hv), lambda i: (i, 0)), pl.BlockSpec((tr, hv), lambda i: (i, gate_blk)),
                  pl.BlockSpec((1, dv), lambda i: (0, 0))],
        out_specs=pl.BlockSpec((tr, hv), lambda i: (i, 0)),
        out_shape=jax.ShapeDtypeStruct((l, hv), bf16),
        compiler_params=_params(("parallel",), 32),
    )(o, proj, nw)


def _dy_gnorm_bwd(dhb, w_out, o, proj, nw, dv, gate_blk, name, *, tm, tn):
    l, hv = o.shape
    d = dhb.shape[1]
    nj = hv // tn
    heads = tn // dv

    def body(g_ref, w_ref, o_ref, gate_ref, nw_ref, do_ref, dg_ref, dw_ref):
        dy = _nt(g_ref[...], w_ref[...])
        wv = nw_ref[...]
        dw = jnp.zeros((1, dv), f32)
        for h in range(heads):
            sl = slice(h * dv, (h + 1) * dv)
            oh = o_ref[:, sl]
            g = gate_ref[:, sl]
            dyh = dy[:, sl]
            r = lax.rsqrt(jnp.mean(oh * oh, axis=-1, keepdims=True) + RMS_EPS)
            xh = oh * r
            dn = dyh * _silu(g)
            dg_ref[:, sl] = _b(dyh * (xh * wv) * _dsilu(g))
            dxh = dn * wv
            do_ref[:, sl] = r * (dxh - xh * jnp.mean(dxh * xh, axis=-1, keepdims=True))
            dw = dw + jnp.sum(dn * xh, axis=0, keepdims=True)
        first = jnp.logical_and(pl.program_id(0) == 0, pl.program_id(1) == 0)

        @pl.when(first)
        def _():
            dw_ref[...] = dw

        @pl.when(jnp.logical_not(first))
        def _():
            dw_ref[...] += dw

    tile = pl.BlockSpec((tm, tn), lambda i, j: (i, j))
    gate = pl.BlockSpec((tm, tn), lambda i, j: (i, gate_blk * nj + j))
    vec = pl.BlockSpec((1, dv), lambda i, j: (0, 0))
    return pl.pallas_call(
        body, name=name, grid=(l // tm, nj),
        in_specs=[pl.BlockSpec((tm, d), lambda i, j: (i, 0)), pl.BlockSpec((tn, d), lambda i, j: (j, 0)),
                  tile, gate, vec],
        out_specs=[tile, gate, vec],
        out_shape=[jax.ShapeDtypeStruct((l, hv), f32), jax.ShapeDtypeStruct(proj.shape, bf16),
                   jax.ShapeDtypeStruct((1, dv), f32)],
        compiler_params=_params(("arbitrary", "arbitrary"), 48),
    )(dhb, w_out, o, proj, nw)


def _ret_prep(proj, cos, sin, name, comm=None):
    l = proj.shape[0]
    tr = _tile(l, 256)
    half = RET_DK // 2
    scale = RET_DK ** -0.5

    def body(p_ref, c_ref, s_ref, o_ref):
        rows = pl.program_id(0) * tr + lax.broadcasted_iota(jnp.int32, (tr, 1), 0)
        kmul = jnp.where(rows >= PAD, scale, 0.0).astype(f32)
        c, s = c_ref[...], s_ref[...]
        for j in range(2 * RET_HEADS):
            t1 = p_ref[:, j * RET_DK: j * RET_DK + half]
            t2 = p_ref[:, j * RET_DK + half: (j + 1) * RET_DK]
            o1 = t1 * c - t2 * s
            o2 = t1 * s + t2 * c
            if j >= RET_HEADS:
                o1, o2 = o1 * kmul, o2 * kmul
            o_ref[:, j * RET_DK: j * RET_DK + half] = o1
            o_ref[:, j * RET_DK + half: (j + 1) * RET_DK] = o2

    wide = pl.BlockSpec((tr, 2 * RET_QK), lambda i: (i, 0))
    tab = pl.BlockSpec((tr, half), lambda i: (i, 0))
    return _call(body, (proj, cos, sin), name=name, grid=(l // tr,), in_specs=[wide, tab, tab], out_specs=[wide],
                 out_shape=[jax.ShapeDtypeStruct((l, 2 * RET_QK), f32)], sem=("parallel",), vmem_mb=32, comm=comm)


def _ret_prep_bwd(dq, dk, cos, sin, dproj, name):
    l = dq.shape[0]
    tr = _tile(l, 256)
    half = RET_DK // 2
    scale = RET_DK ** -0.5

    def body(dq_ref, dk_ref, c_ref, s_ref, _, o_ref):
        rows = pl.program_id(0) * tr + lax.broadcasted_iota(jnp.int32, (tr, 1), 0)
        kmul = jnp.where(rows >= PAD, scale, 0.0).astype(f32)
        c, s = c_ref[...], s_ref[...]
        for j in range(2 * RET_HEADS):
            d_ref = dq_ref if j < RET_HEADS else dk_ref
            jj = j % RET_HEADS
            d1 = d_ref[:, jj * RET_DK: jj * RET_DK + half]
            d2 = d_ref[:, jj * RET_DK + half: (jj + 1) * RET_DK]
            if j >= RET_HEADS:
                d1, d2 = d1 * kmul, d2 * kmul
            o_ref[:, j * RET_DK: j * RET_DK + half] = _b(d1 * c + d2 * s)
            o_ref[:, j * RET_DK + half: (j + 1) * RET_DK] = _b(d2 * c - d1 * s)

    nar = pl.BlockSpec((tr, RET_QK), lambda i: (i, 0))
    wide = pl.BlockSpec((tr, 2 * RET_QK), lambda i: (i, 0))
    tab = pl.BlockSpec((tr, half), lambda i: (i, 0))
    return pl.pallas_call(
        body, name=name, grid=(l // tr,), in_specs=[nar, nar, tab, tab, pl.BlockSpec(memory_space=pl.ANY)],
        out_specs=wide, out_shape=jax.ShapeDtypeStruct(dproj.shape, dproj.dtype), input_output_aliases={4: 0},
        compiler_params=_params(("parallel",), 32),
    )(dq, dk, cos, sin, dproj)


RET_BLOCK_CHUNKS = 3


def _ret_block(l):
    nch = l // CHUNK
    return RET_BLOCK_CHUNKS * CHUNK if nch % RET_BLOCK_CHUNKS == 0 else CHUNK


def _ret_decay(lg, rb):
    idx = lax.broadcasted_iota(jnp.int32, (rb, 1), 0).astype(f32)
    ri = lax.broadcasted_iota(jnp.int32, (rb, rb), 0)
    ci = lax.broadcasted_iota(jnp.int32, (rb, rb), 1)
    rel = (ri - ci).astype(f32)
    dmask = jnp.where(ri >= ci, jnp.exp(lg * jnp.maximum(rel, 0.0)), 0.0)
    xi = jnp.exp(lg * (idx + 1.0))
    zeta = jnp.exp(lg * (rb - 1.0 - idx))
    return dmask, xi, zeta


def _ret_scan_fwd(qk, proj, gn_w, lgs, gcs, name, comm=None):
    l = qk.shape[0]
    rb = _ret_block(l)
    nb = l // rb

    def body(lg_ref, gc_ref, q_ref, k_ref, v_ref, g_ref, nw_ref, o_ref, st_ref, y_ref, s_ref):
        @pl.when(pl.program_id(0) == 0)
        def _():
            s_ref[...] = jnp.zeros_like(s_ref)

        hs = range(RET_HEADS)
        dec = [_ret_decay(lg_ref[h], rb) for h in hs]
        q = [q_ref[:, h * RET_DK:(h + 1) * RET_DK] for h in hs]
        k = [k_ref[:, h * RET_DK:(h + 1) * RET_DK] for h in hs]
        vb = [_b(v_ref[:, h * RET_DV:(h + 1) * RET_DV]) for h in hs]
        s = [s_ref[h] for h in hs]
        sb = [_b(s[h]) for h in hs]
        scores = [_b(_nt(_b(q[h]), _b(k[h])) * dec[h][0]) for h in hs]
        inter = [_nn(_b(q[h] * dec[h][1]), sb[h]) for h in hs]
        kv = [_tn(_b(k[h] * dec[h][2]), vb[h]) for h in hs]
        nw = nw_ref[...]
        for h in hs:
            cols = slice(h * RET_DV, (h + 1) * RET_DV)
            st_ref[0, h] = sb[h]
            o = _nn(scores[h], vb[h]) + inter[h]
            o_ref[:, cols] = o
            r = lax.rsqrt(jnp.mean(o * o, axis=-1, keepdims=True) + RMS_EPS)
            y_ref[:, cols] = _b(o * r * nw * _silu(g_ref[:, cols]))
            s_ref[h] = gc_ref[h] * s[h] + kv[h]

    smem = pl.BlockSpec(memory_space=pltpu.SMEM)
    wide = pl.BlockSpec((rb, RET_V), lambda n: (n, 0))
    return _call(
        body, (lgs, gcs, qk, qk, proj, proj, gn_w), name=name, grid=(nb,),
        in_specs=[smem, smem,
                  pl.BlockSpec((rb, RET_QK), lambda n: (n, 0)),
                  pl.BlockSpec((rb, RET_QK), lambda n: (n, 1)),
                  pl.BlockSpec((rb, RET_V), lambda n: (n, 1)),
                  pl.BlockSpec((rb, RET_V), lambda n: (n, 2)),
                  pl.BlockSpec((1, RET_DV), lambda n: (0, 0))],
        out_specs=[wide, pl.BlockSpec((1, RET_HEADS, RET_DK, RET_DV), lambda n: (n, 0, 0, 0)), wide],
        out_shape=[jax.ShapeDtypeStruct((l, RET_V), f32),
                   jax.ShapeDtypeStruct((nb, RET_HEADS, RET_DK, RET_DV), bf16),
                   jax.ShapeDtypeStruct((l, RET_V), bf16)],
        scratch=[pltpu.VMEM((RET_HEADS, RET_DK, RET_DV), f32)], sem=("arbitrary",), vmem_mb=40, comm=comm)


def _ret_scan_bwd(qk, proj, states, do, dproj, lgs, gcs, name, comm=None):
    l = qk.shape[0]
    rb = _ret_block(l)
    nb = l // rb

    def body(lg_ref, gc_ref, q_ref, k_ref, v_ref, st_ref, do_ref, _, dq_ref, dk_ref, dv_ref, ds_ref):
        @pl.when(pl.program_id(0) == 0)
        def _():
            ds_ref[...] = jnp.zeros_like(ds_ref)

        hs = range(RET_HEADS)
        dec = [_ret_decay(lg_ref[h], rb) for h in hs]
        q = [q_ref[:, h * RET_DK:(h + 1) * RET_DK] for h in hs]
        k = [k_ref[:, h * RET_DK:(h + 1) * RET_DK] for h in hs]
        qb, kb = [_b(t) for t in q], [_b(t) for t in k]
        vb = [_b(v_ref[:, h * RET_DV:(h + 1) * RET_DV]) for h in hs]
        dob = [_b(do_ref[:, h * RET_DV:(h + 1) * RET_DV]) for h in hs]
        dsp = [ds_ref[h] for h in hs]
        dspb = [_b(t) for t in dsp]
        scores = [_b(_nt(qb[h], kb[h]) * dec[h][0]) for h in hs]
        dscores = [_b(_nt(dob[h], vb[h]) * dec[h][0]) for h in hs]
        for h in hs:
            dq_ref[:, h * RET_DK:(h + 1) * RET_DK] = _nn(dscores[h], kb[h]) + _nt(dob[h], st_ref[0, h]) * dec[h][1]
        for h in hs:
            dk_ref[:, h * RET_DK:(h + 1) * RET_DK] = _tn(dscores[h], qb[h]) + _nt(vb[h], dspb[h]) * dec[h][2]
        for h in hs:
            dv_ref[:, h * RET_DV:(h + 1) * RET_DV] = _b(_tn(scores[h], dob[h]) + _nn(_b(k[h] * dec[h][2]), dspb[h]))
        for h in hs:
            ds_ref[h] = gc_ref[h] * dsp[h] + _tn(_b(q[h] * dec[h][1]), dob[h])

    smem = pl.BlockSpec(memory_space=pltpu.SMEM)
    rev = lambda n: nb - 1 - n
    return _call(
        body, (lgs, gcs, qk, qk, proj, states, do, dproj), name=name, grid=(nb,),
        in_specs=[smem, smem,
                  pl.BlockSpec((rb, RET_QK), lambda n: (rev(n), 0)),
                  pl.BlockSpec((rb, RET_QK), lambda n: (rev(n), 1)),
                  pl.BlockSpec((rb, RET_V), lambda n: (rev(n), 1)),
                  pl.BlockSpec((1, RET_HEADS, RET_DK, RET_DV), lambda n: (rev(n), 0, 0, 0)),
                  pl.BlockSpec((rb, RET_V), lambda n: (rev(n), 0)),
                  pl.BlockSpec(memory_space=pl.ANY)],
        out_specs=[pl.BlockSpec((rb, RET_QK), lambda n: (rev(n), 0)),
                   pl.BlockSpec((rb, RET_QK), lambda n: (rev(n), 0)),
                   pl.BlockSpec((rb, RET_V), lambda n: (rev(n), 1))],
        out_shape=[jax.ShapeDtypeStruct((l, RET_QK), f32), jax.ShapeDtypeStruct((l, RET_QK), f32),
                   jax.ShapeDtypeStruct(dproj.shape, dproj.dtype)],
        scratch=[pltpu.VMEM((RET_HEADS, RET_DK, RET_DV), f32)], sem=("arbitrary",), vmem_mb=40, comm=comm,
        aliases={7: 2})


CONV_BLK = 512
CONV_Q_BLKS = DN_QK // CONV_BLK
HALO = 8


def _conv_tile(l):
    return _tile(l, 3072)


def _slab_rows(r):
    return pl.ds(pl.multiple_of(r * HALO, HALO), HALO)


def _conv_slab(x_ref, p_ref, r, i, tr):
    cur = x_ref[_slab_rows(r), :]
    prev = jnp.where(r > 0, x_ref[_slab_rows(jnp.maximum(r - 1, 0)), :], p_ref[...])
    row0 = i * tr + r * HALO
    cur = jnp.where(row0 >= PAD, cur, 0.0)
    prev = jnp.where(row0 - HALO >= PAD, prev, 0.0)
    lrow = lax.broadcasted_iota(jnp.int32, (HALO, 1), 0)
    shifted = [jnp.where(lrow < s, pltpu.roll(prev, s, 0), pltpu.roll(cur, s, 0)) for s in range(1, CONV_K)]
    return [cur] + shifted


def _conv_of(xs, w):
    acc = xs[0] * w[CONV_K - 1:CONV_K, :]
    for s in range(1, CONV_K):
        acc = acc + xs[s] * w[CONV_K - 1 - s:CONV_K - s, :]
    return acc


def _slab_loop(n_slabs, fn, init=None):
    return lax.fori_loop(0, n_slabs, fn, init, unroll=8)


def _dn_conv_fwd(proj, conv_w, name, comm=None):
    l = proj.shape[0]
    tr = _conv_tile(l)
    nblk = DN_CONV_CH // CONV_BLK
    heads = CONV_BLK // DN_DK

    def body(x_ref, p_ref, w_ref, o_ref):
        i, j = pl.program_id(0), pl.program_id(1)
        w = w_ref[...]

        def act(r):
            return _silu(_conv_of(_conv_slab(x_ref, p_ref, r, i, tr), w))

        def normed(scale):
            def slab(r, carry):
                a = act(r)
                outs = []
                for h in range(heads):
                    ah = a[:, h * DN_DK:(h + 1) * DN_DK]
                    outs.append(ah * (lax.rsqrt(jnp.sum(ah * ah, axis=-1, keepdims=True) + RMS_EPS) * scale))
                o_ref[_slab_rows(r), :] = jnp.concatenate(outs, axis=1)
                return carry
            return slab

        def plain(r, carry):
            o_ref[_slab_rows(r), :] = act(r)
            return carry

        @pl.when(j < CONV_Q_BLKS)
        def _():
            _slab_loop(tr // HALO, normed(DN_DK ** -0.5))

        @pl.when(jnp.logical_and(j >= CONV_Q_BLKS, j < 2 * CONV_Q_BLKS))
        def _():
            _slab_loop(tr // HALO, normed(1.0))

        @pl.when(j >= 2 * CONV_Q_BLKS)
        def _():
            _slab_loop(tr // HALO, plain)

    hb = tr // HALO
    return _call(
        body, (proj, proj, conv_w), name=name, grid=(l // tr, nblk),
        in_specs=[pl.BlockSpec((tr, CONV_BLK), lambda i, j: (i, j)),
                  pl.BlockSpec((HALO, CONV_BLK), lambda i, j: (jnp.maximum(i * hb - 1, 0), j)),
                  pl.BlockSpec((CONV_K, CONV_BLK), lambda i, j: (0, j))],
        out_specs=[pl.BlockSpec((tr, CONV_BLK), lambda i, j: (i, j))],
        out_shape=[jax.ShapeDtypeStruct((l, DN_CONV_CH), f32)],
        scratch=[], sem=("parallel", "parallel"), vmem_mb=32, comm=comm)


def _dn_conv_bwd_a(proj, conv_w, dqkv, name, comm=None):
    l = proj.shape[0]
    tr = _conv_tile(l)
    nblk = DN_CONV_CH // CONV_BLK
    heads = CONV_BLK // DN_DK

    def body(x_ref, p_ref, w_ref, d_ref, dc_ref, dw_ref, acc_ref):
        j, i = pl.program_id(0), pl.program_id(1)
        w = w_ref[...]
        acc_ref[...] = jnp.zeros_like(acc_ref)

        def run(l2_scale):
            _slab_loop(tr // HALO, slab_of(l2_scale))

        def slab_of(l2_scale):
            def slab(r, carry):
                xs = _conv_slab(x_ref, p_ref, r, i, tr)
                c = _conv_of(xs, w)
                a = _silu(c)
                dy = d_ref[_slab_rows(r), :]
                if l2_scale is None:
                    da = dy
                else:
                    parts = []
                    for h in range(heads):
                        sl = slice(h * DN_DK, (h + 1) * DN_DK)
                        ah, dyh = a[:, sl], dy[:, sl]
                        rn = lax.rsqrt(jnp.sum(ah * ah, axis=-1, keepdims=True) + RMS_EPS)
                        yh = ah * rn
                        parts.append((rn * l2_scale) * (dyh - yh * jnp.sum(dyh * yh, axis=-1, keepdims=True)))
                    da = jnp.concatenate(parts, axis=1)
                dc = da * _dsilu(c)
                dc_ref[_slab_rows(r), :] = dc
                for k in range(CONV_K):
                    acc_ref[k] += dc * xs[CONV_K - 1 - k]
                return carry
            return slab

        @pl.when(j < CONV_Q_BLKS)
        def _():
            run(DN_DK ** -0.5)

        @pl.when(jnp.logical_and(j >= CONV_Q_BLKS, j < 2 * CONV_Q_BLKS))
        def _():
            run(1.0)

        @pl.when(j >= 2 * CONV_Q_BLKS)
        def _():
            run(None)

        ksel = lax.broadcasted_iota(jnp.int32, (CONV_K, 1), 0)
        dw = jnp.zeros((CONV_K, CONV_BLK), f32)
        for k in range(CONV_K):
            dw = dw + jnp.where(ksel == k, jnp.sum(acc_ref[k], axis=0, keepdims=True), 0.0)

        @pl.when(i == 0)
        def _():
            dw_ref[...] = dw

        @pl.when(i > 0)
        def _():
            dw_ref[...] += dw

    hb = tr // HALO
    blk = pl.BlockSpec((tr, CONV_BLK), lambda j, i: (i, j))
    return _call(
        body, (proj, proj, conv_w, dqkv), name=name, grid=(nblk, l // tr),
        in_specs=[blk, pl.BlockSpec((HALO, CONV_BLK), lambda j, i: (jnp.maximum(i * hb - 1, 0), j)),
                  pl.BlockSpec((CONV_K, CONV_BLK), lambda j, i: (0, j)), blk],
        out_specs=[blk, pl.BlockSpec((CONV_K, CONV_BLK), lambda j, i: (0, j))],
        out_shape=[jax.ShapeDtypeStruct((l, DN_CONV_CH), f32), jax.ShapeDtypeStruct((CONV_K, DN_CONV_CH), f32)],
        scratch=[pltpu.VMEM((CONV_K, HALO, CONV_BLK), f32)], sem=("parallel", "arbitrary"), vmem_mb=48, comm=comm)


def _dn_conv_bwd_b(dc, conv_w, dproj, name):
    l = dc.shape[0]
    tr = _conv_tile(l)
    nblk = DN_CONV_CH // CONV_BLK
    nrow = l // tr

    n_slabs = tr // HALO
    pair = 2 * HALO

    def body(d_ref, n_ref, w_ref, _, o_ref):
        i = pl.program_id(0)
        w = w_ref[...]
        nxt_tile = jnp.where(i < nrow - 1, n_ref[...], 0.0)
        lrow = lax.broadcasted_iota(jnp.int32, (HALO, 1), 0)

        def one(r):
            cur = d_ref[_slab_rows(r), :]
            nxt = jnp.where(r < n_slabs - 1, d_ref[_slab_rows(jnp.minimum(r + 1, n_slabs - 1)), :], nxt_tile)
            acc = cur * w[CONV_K - 1:CONV_K, :]
            for s in range(1, CONV_K):
                up = jnp.where(lrow >= HALO - s, pltpu.roll(nxt, HALO - s, 0), pltpu.roll(cur, HALO - s, 0))
                acc = acc + up * w[CONV_K - 1 - s:CONV_K - s, :]
            return jnp.where(i * tr + r * HALO >= PAD, acc, 0.0)

        def two(q, carry):
            rows = pl.ds(pl.multiple_of(q * pair, pair), pair)
            o_ref[rows, :] = _b(jnp.concatenate([one(2 * q), one(2 * q + 1)], axis=0))
            return carry

        lax.fori_loop(0, n_slabs // 2, two, None, unroll=4)

    hb = tr // HALO
    nh = l // HALO
    return pl.pallas_call(
        body, name=name, grid=(nrow, nblk),
        in_specs=[pl.BlockSpec((tr, CONV_BLK), lambda i, j: (i, j)),
                  pl.BlockSpec((HALO, CONV_BLK), lambda i, j: (jnp.minimum((i + 1) * hb, nh - 1), j)),
                  pl.BlockSpec((CONV_K, CONV_BLK), lambda i, j: (0, j)),
                  pl.BlockSpec(memory_space=pl.ANY)],
        out_specs=pl.BlockSpec((tr, CONV_BLK), lambda i, j: (i, j)),
        out_shape=jax.ShapeDtypeStruct(dproj.shape, dproj.dtype), input_output_aliases={3: 0},
        compiler_params=_params(("parallel", "parallel"), 32),
    )(dc, dc, conv_w, dproj)


BA_W = LANES


def _dn_gates(ba_ref, al_ref, dt_ref, n):
    rows = n * CHUNK + lax.broadcasted_iota(jnp.int32, (CHUNK, 1), 0)
    vm = (rows >= PAD).astype(f32)
    bin_ = ba_ref[:, 0:DN_HEADS]
    z = ba_ref[:, DN_HEADS:2 * DN_HEADS] + dt_ref[...]
    sp = jnp.maximum(z, 0.0) + jnp.log1p(jnp.exp(-jnp.abs(z)))
    ea = jnp.exp(al_ref[...])
    beta = _sigmoid(bin_) * vm
    g = -ea * sp * vm
    return vm, bin_, z, ea, beta, g


def _tri():
    ri = lax.broadcasted_iota(jnp.int32, (CHUNK, CHUNK), 0)
    ci = lax.broadcasted_iota(jnp.int32, (CHUNK, CHUNK), 1)
    return ri, ci


def _split(a):
    hi = _b(a)
    return hi, _b(a - hi.astype(f32))


def _mm3(a, b, dot=_nn):
    (ah, al), (bh, bl) = _split(a), _split(b)
    return dot(ah, bh) + (dot(ah, bl) + dot(al, bh))


def _cumsum_rows(tri, g):
    tb = _b(tri)
    g1 = _b(g)
    r1 = g - g1.astype(f32)
    g2 = _b(r1)
    g3 = _b(r1 - g2.astype(f32))
    return _nn(tb, g1) + (_nn(tb, g2) + _nn(tb, g3))


DN_SCAN_CHUNKS = 3


def _scan_chunks(nch):
    return DN_SCAN_CHUNKS if nch % DN_SCAN_CHUNKS == 0 else 1


def _dn_prep(qkv, ba, a_log, dt_bias, name, comm=None):
    l = qkv.shape[0]
    nch = l // CHUNK
    heads = range(DN_HEADS)

    cb = _scan_chunks(nch)
    items = [(c, h) for c in range(cb) for h in heads]

    def body(q_ref, k_ref, v_ref, ba_ref, al_ref, dt_ref, t_ref, u_ref, wq_ref, pk_ref, eg_ref, kpt_ref, qwt_ref):
        n0 = pl.program_id(0) * cb
        ri, ci = _tri()
        incl, strict = ri >= ci, ri > ci
        eye = (ri == ci).astype(f32)
        rows = [slice(c * CHUNK, (c + 1) * CHUNK) for c in range(cb)]
        gam, gam_t, beta = [], [], []
        for c in range(cb):
            _, _, _, _, beta_c, g_c = _dn_gates(ba_ref[rows[c], :], al_ref, dt_ref, n0 + c)
            gam.append(_cumsum_rows(incl.astype(f32), g_c))
            gam_t.append(gam[c].T)
            beta.append(beta_c)
        gc = {(c, h): gam[c][:, h:h + 1] for c, h in items}
        bh = {(c, h): beta[c][:, h:h + 1] for c, h in items}
        kh = {(c, h): k_ref[rows[c], h * DN_DK:(h + 1) * DN_DK] for c, h in items}
        kb = {i: _b(kh[i]) for i in items}
        decay = {(c, h): jnp.exp(jnp.where(incl, gc[c, h] - gam_t[c][h:h + 1, :], -jnp.inf)) for c, h in items}
        a = {i: jnp.where(strict, bh[i] * _nt(kb[i], kb[i]) * decay[i], 0.0) for i in items}
        t = {i: eye - a[i] for i in items}
        p = a
        for level in range(int(math.log2(CHUNK)) - 1):
            mm = _mm3 if level < 2 else (lambda x, y: _nn(_b(x), _b(y)))
            p = {i: mm(p[i], p[i]) for i in items}
            t = {i: t[i] + mm(t[i], p[i]) for i in items}
        eg = {i: jnp.exp(gc[i]) for i in items}
        for c, h in items:
            i = (c, h)
            t_ref[c, h] = t[i]
            u_ref[rows[c], h * DN_DV:(h + 1) * DN_DV] = _mm3(t[i], v_ref[rows[c], h * DN_DV:(h + 1) * DN_DV] * bh[i])
            w = _mm3(t[i], kh[i] * (bh[i] * eg[i]))
            wq_ref[c, h, 0:CHUNK, :] = _b(w)
            qwt_ref[c, h, DN_DK:2 * DN_DK, :] = _b(w.T)
        for c, h in items:
            i = (c, h)
            qh = q_ref[rows[c], h * DN_DK:(h + 1) * DN_DK]
            gl = gc[i][CHUNK - 1:CHUNK, :]
            qe = qh * eg[i]
            ke = kh[i] * jnp.exp(gl - gc[i])
            pmat = _nt(_b(qh), kb[i]) * decay[i]
            wq_ref[c, h, CHUNK:2 * CHUNK, :] = _b(qe)
            qwt_ref[c, h, 0:DN_DK, :] = _b(qe.T)
            pk_ref[c, h, 0:CHUNK, :] = _b(pmat)
            pk_ref[c, h, CHUNK:CHUNK + DN_DK, :] = _b(ke.T)
            kpt_ref[c, h, :, 0:DN_DK] = _b(ke)
            kpt_ref[c, h, :, DN_DK:DN_DK + CHUNK] = _b(pmat.T)
            eg_ref[c, h] = jnp.broadcast_to(jnp.exp(gl), (8, LANES))

    vec = pl.BlockSpec((1, DN_HEADS), lambda n: (0, 0))
    return _call(
        body, (qkv, qkv, qkv, ba, a_log, dt_bias), name=name, grid=(nch // cb,),
        in_specs=[pl.BlockSpec((cb * CHUNK, DN_QK), lambda n: (n, 0)),
                  pl.BlockSpec((cb * CHUNK, DN_QK), lambda n: (n, 1)),
                  pl.BlockSpec((cb * CHUNK, DN_V), lambda n: (n, 1)),
                  pl.BlockSpec((cb * CHUNK, BA_W), lambda n: (n, 0)), vec, vec],
        out_specs=[pl.BlockSpec((cb, DN_HEADS, CHUNK, CHUNK), lambda n: (n, 0, 0, 0)),
                   pl.BlockSpec((cb * CHUNK, DN_V), lambda n: (n, 0)),
                   pl.BlockSpec((cb, DN_HEADS, 2 * CHUNK, DN_DK), lambda n: (n, 0, 0, 0)),
                   pl.BlockSpec((cb, DN_HEADS, CHUNK + DN_DK, CHUNK), lambda n: (n, 0, 0, 0)),
                   pl.BlockSpec((cb, DN_HEADS, 8, LANES), lambda n: (n, 0, 0, 0)),
                   pl.BlockSpec((cb, DN_HEADS, CHUNK, DN_DK + CHUNK), lambda n: (n, 0, 0, 0)),
                   pl.BlockSpec((cb, DN_HEADS, 2 * DN_DK, CHUNK), lambda n: (n, 0, 0, 0))],
        out_shape=[jax.ShapeDtypeStruct((nch, DN_HEADS, CHUNK, CHUNK), f32),
                   jax.ShapeDtypeStruct((l, DN_V), f32),
                   jax.ShapeDtypeStruct((nch, DN_HEADS, 2 * CHUNK, DN_DK), bf16),
                   jax.ShapeDtypeStruct((nch, DN_HEADS, CHUNK + DN_DK, CHUNK), bf16),
                   jax.ShapeDtypeStruct((nch, DN_HEADS, 8, LANES), f32),
                   jax.ShapeDtypeStruct((nch, DN_HEADS, CHUNK, DN_DK + CHUNK), bf16),
                   jax.ShapeDtypeStruct((nch, DN_HEADS, 2 * DN_DK, CHUNK), bf16)],
        sem=("parallel",), vmem_mb=40, comm=comm)


def _dn_scan_fwd(u, wq, pk, egl, name):
    l = u.shape[0]
    nch = l // CHUNK
    cs = _scan_chunks(nch)

    def body(u_ref, wq_ref, pk_ref, eg_ref, o_ref, st_ref, vn_ref, s_ref):
        @pl.when(pl.program_id(0) == 0)
        def _():
            s_ref[...] = jnp.zeros_like(s_ref)

        hs = range(DN_HEADS)
        cols = [slice(h * DN_DV, (h + 1) * DN_DV) for h in hs]
        s = [s_ref[h] for h in hs]
        for c in range(cs):
            rows = slice(c * CHUNK, (c + 1) * CHUNK)
            sb = [_b(s[h]) for h in hs]
            x = [_nn(wq_ref[c, h], sb[h]) for h in hs]
            vnb = [_b(u_ref[rows, cols[h]] - x[h][0:CHUNK]) for h in hs]
            y = [_nn(pk_ref[c, h], vnb[h]) for h in hs]
            for h in hs:
                st_ref[c, h] = sb[h]
                vn_ref[rows, cols[h]] = vnb[h]
                o_ref[rows, cols[h]] = x[h][CHUNK:2 * CHUNK] + y[h][0:CHUNK]
            s = [eg_ref[c, h][0:1, 0:1] * s[h] + y[h][CHUNK:CHUNK + DN_DK] for h in hs]
        for h in hs:
            s_ref[h] = s[h]

    return pl.pallas_call(
        body, name=name, grid=(nch // cs,),
        in_specs=[pl.BlockSpec((cs * CHUNK, DN_V), lambda n: (n, 0)),
                  pl.BlockSpec((cs, DN_HEADS, 2 * CHUNK, DN_DK), lambda n: (n, 0, 0, 0)),
                  pl.BlockSpec((cs, DN_HEADS, CHUNK + DN_DK, CHUNK), lambda n: (n, 0, 0, 0)),
                  pl.BlockSpec((cs, DN_HEADS, 8, LANES), lambda n: (n, 0, 0, 0))],
        out_specs=[pl.BlockSpec((cs * CHUNK, DN_V), lambda n: (n, 0)),
                   pl.BlockSpec((cs, DN_HEADS, DN_DK, DN_DV), lambda n: (n, 0, 0, 0)),
                   pl.BlockSpec((cs * CHUNK, DN_V), lambda n: (n, 0))],
        out_shape=[jax.ShapeDtypeStruct((l, DN_V), f32),
                   jax.ShapeDtypeStruct((nch, DN_HEADS, DN_DK, DN_DV), bf16),
                   jax.ShapeDtypeStruct((l, DN_V), bf16)],
        scratch_shapes=[pltpu.VMEM((DN_HEADS, DN_DK, DN_DV), f32)],
        compiler_params=_params(("arbitrary",), 40),
    )(u, wq, pk, egl)


def _dn_scan_bwd(do, kpt, qwt, egl, name):
    l = do.shape[0]
    nch = l // CHUNK
    cs = _scan_chunks(nch)
    nblk = nch // cs

    def body(do_ref, kpt_ref, qwt_ref, eg_ref, dvn_ref, dsp_ref, ds_ref):
        @pl.when(pl.program_id(0) == 0)
        def _():
            ds_ref[...] = jnp.zeros_like(ds_ref)

        hs = range(DN_HEADS)
        cols = [slice(h * DN_DV, (h + 1) * DN_DV) for h in hs]
        ds = [ds_ref[h] for h in hs]
        for c in reversed(range(cs)):
            rows = slice(c * CHUNK, (c + 1) * CHUNK)
            dspb = [_b(ds[h]) for h in hs]
            dob = [_b(do_ref[rows, cols[h]]) for h in hs]
            dvn = [_nn(kpt_ref[c, h][:, 0:DN_DK], dspb[h]) + _nn(kpt_ref[c, h][:, DN_DK:DN_DK + CHUNK], dob[h])
                   for h in hs]
            for h in hs:
                dsp_ref[c, h] = dspb[h]
                dvn_ref[rows, cols[h]] = dvn[h]
            ds = [eg_ref[c, h][0:1, 0:1] * ds[h] + _nn(qwt_ref[c, h][0:DN_DK], dob[h])
                  - _nn(qwt_ref[c, h][DN_DK:2 * DN_DK], _b(dvn[h])) for h in hs]
        for h in hs:
            ds_ref[h] = ds[h]

    rev = lambda s: nblk - 1 - s
    return pl.pallas_call(
        body, name=name, grid=(nblk,),
        in_specs=[pl.BlockSpec((cs * CHUNK, DN_V), lambda s: (rev(s), 0)),
                  pl.BlockSpec((cs, DN_HEADS, CHUNK, DN_DK + CHUNK), lambda s: (rev(s), 0, 0, 0)),
                  pl.BlockSpec((cs, DN_HEADS, 2 * DN_DK, CHUNK), lambda s: (rev(s), 0, 0, 0)),
                  pl.BlockSpec((cs, DN_HEADS, 8, LANES), lambda s: (rev(s), 0, 0, 0))],
        out_specs=[pl.BlockSpec((cs * CHUNK, DN_V), lambda s: (rev(s), 0)),
                   pl.BlockSpec((cs, DN_HEADS, DN_DK, DN_DV), lambda s: (rev(s), 0, 0, 0))],
        out_shape=[jax.ShapeDtypeStruct((l, DN_V), f32),
                   jax.ShapeDtypeStruct((nch, DN_HEADS, DN_DK, DN_DV), bf16)],
        scratch_shapes=[pltpu.VMEM((DN_HEADS, DN_DK, DN_DV), f32)],
        compiler_params=_params(("arbitrary",), 40),
    )(do, kpt, qwt, egl)


def _dn_post_bwd(qkv, ba, a_log, dt_bias, states, dsp_all, tinv_all, u_all, wq, vn_all, do, dvn_all, name):
    l = qkv.shape[0]
    nch = l // CHUNK
    cb = _scan_chunks(nch)
    items = [(c, h) for c in range(cb) for h in range(DN_HEADS)]

    def body(q_ref, k_ref, v_ref, ba_ref, al_ref, dt_ref, st_ref, dsp_ref, t_ref, u_ref, wq_ref, vn_ref, do_ref,
             dvn_ref, dqkv_ref, dba_ref, dal_ref, ddt_ref):
        step = pl.program_id(0)
        ri, ci = _tri()
        incl, strict = ri >= ci, ri > ci
        lane8 = lax.broadcasted_iota(jnp.int32, (1, DN_HEADS), 1)
        sub8 = lax.broadcasted_iota(jnp.int32, (DN_HEADS, 1), 0)
        last = (lax.broadcasted_iota(jnp.int32, (CHUNK, 1), 0) == CHUNK - 1).astype(f32)
        rsum = lambda t: jnp.sum(t, axis=-1, keepdims=True)
        rows = [slice(c * CHUNK, (c + 1) * CHUNK) for c in range(cb)]
        gates = [_dn_gates(ba_ref[rows[c], :], al_ref, dt_ref, step * cb + c) for c in range(cb)]
        gam = [_cumsum_rows(incl.astype(f32), gates[c][5]) for c in range(cb)]
        gam_t = [gam[c].T for c in range(cb)]
        each = lambda fn: {(c, h): fn(c, h) for c, h in items}
        dk_cols = lambda h: slice(h * DN_DK, (h + 1) * DN_DK)
        dv_cols = lambda h: slice(h * DN_DV, (h + 1) * DN_DV)
        gc = each(lambda c, h: gam[c][:, h:h + 1])
        bh = each(lambda c, h: gates[c][4][:, h:h + 1])
        qh = each(lambda c, h: q_ref[rows[c], dk_cols(h)])
        kh = each(lambda c, h: k_ref[rows[c], dk_cols(h)])
        doh = each(lambda c, h: _b(do_ref[rows[c], dv_cols(h)]))
        sb = each(lambda c, h: st_ref[c, h])
        dspb = each(lambda c, h: dsp_ref[c, h])
        vnb = each(lambda c, h: vn_ref[rows[c], dv_cols(h)])
        dvn = each(lambda c, h: dvn_ref[rows[c], dv_cols(h)])
        wb = each(lambda c, h: wq_ref[c, h, 0:CHUNK, :])
        decay = each(lambda c, h: jnp.exp(jnp.where(incl, gc[c, h] - gam_t[c][h:h + 1, :], -jnp.inf)))
        qb, kb = each(lambda c, h: _b(qh[c, h])), each(lambda c, h: _b(kh[c, h]))
        eg = each(lambda c, h: jnp.exp(gc[c, h]))
        gl = each(lambda c, h: gc[c, h][CHUNK - 1:CHUNK, :])
        ekd = each(lambda c, h: jnp.exp(gl[c, h] - gc[c, h]))
        dvnb = each(lambda c, h: _b(dvn[c, h]))
        kk = each(lambda c, h: _nt(kb[c, h], kb[c, h]))
        p = each(lambda c, h: _nt(qb[c, h], kb[c, h]) * decay[c, h])
        dpraw = each(lambda c, h: _nt(doh[c, h], vnb[c, h]))
        dqe = each(lambda c, h: _nt(doh[c, h], sb[c, h]))
        dke = each(lambda c, h: _nt(vnb[c, h], dspb[c, h]))
        dw = each(lambda c, h: -_nt(dvnb[c, h], sb[c, h]))
        dru = each(lambda c, h: _mm3(t_ref[c, h], dvn[c, h], _tn))
        drw = each(lambda c, h: _mm3(t_ref[c, h], dw[c, h], _tn))
        dqk = each(lambda c, h: _b(dpraw[c, h] * decay[c, h]))
        for c, h in items:
            i = (c, h)
            dqkv_ref[rows[c], dk_cols(h)] = _nn(dqk[i], kb[i]) + dqe[i] * eg[i]
            dqkv_ref[rows[c], 2 * DN_QK + h * DN_DV:2 * DN_QK + (h + 1) * DN_DV] = bh[i] * dru[i]
        da = each(lambda c, h: jnp.where(strict, -(_nt(_b(dru[c, h]), _b(u_ref[rows[c], dv_cols(h)]))
                                                   + _nt(_b(drw[c, h]), wb[c, h])), 0.0))
        dkk = each(lambda c, h: _b(da[c, h] * bh[c, h] * decay[c, h]))
        for c, h in items:
            i = (c, h)
            dqkv_ref[rows[c], DN_QK + h * DN_DK:DN_QK + (h + 1) * DN_DK] = (
                _tn(dqk[i], qb[i]) + dke[i] * ekd[i] + (bh[i] * eg[i]) * drw[i]
                + _nn(dkk[i], kb[i]) + _tn(dkk[i], kb[i]))
        dal = jnp.zeros((1, DN_HEADS), f32)
        ddt = jnp.zeros((1, DN_HEADS), f32)
        dba_ref[...] = jnp.zeros_like(dba_ref)
        for c in range(cb):
            vm, bin_, z, ea, _, g = gates[c]
            dbeta = jnp.zeros((CHUNK, DN_HEADS), f32)
            dgam = jnp.zeros((CHUNK, DN_HEADS), f32)
            dgam_neg_t = jnp.zeros((DN_HEADS, CHUNK), f32)
            for h in range(DN_HEADS):
                i = (c, h)
                keg = kh[i] * eg[i]
                ke = kh[i] * ekd[i]
                rw = rsum(drw[i] * keg)
                rke = rsum(dke[i] * ke)
                db_h = rsum(dru[i] * v_ref[rows[c], dv_cols(h)]) + rw + rsum(da[i] * kk[i] * decay[i])
                mm = da[i] * (bh[i] * kk[i] * decay[i]) + dpraw[i] * p[i]
                dgl = (jnp.sum(rke, axis=0, keepdims=True)
                       + jnp.exp(gl[i]) * jnp.sum(rsum(dspb[i].astype(f32) * sb[i].astype(f32)), axis=0,
                                                  keepdims=True))
                dg_h = rsum(mm) + rw * bh[i] + rsum(dqe[i] * (qh[i] * eg[i])) - rke + last * dgl
                dbeta = dbeta + jnp.where(lane8 == h, db_h, 0.0)
                dgam = dgam + jnp.where(lane8 == h, dg_h, 0.0)
                dgam_neg_t = dgam_neg_t + jnp.where(sub8 == h, jnp.sum(mm, axis=0, keepdims=True), 0.0)
            dgam = dgam - dgam_neg_t.T
            dg = _cumsum_rows((ri <= ci).astype(f32), dgam)
            sg = _sigmoid(bin_)
            dain = dg * (-ea) * vm * _sigmoid(z)
            dba_ref[rows[c], 0:DN_HEADS] = dbeta * vm * sg * (1.0 - sg)
            dba_ref[rows[c], DN_HEADS:2 * DN_HEADS] = dain
            dal = dal + jnp.sum(dg * g, axis=0, keepdims=True)
            ddt = ddt + jnp.sum(dain, axis=0, keepdims=True)

        @pl.when(step == 0)
        def _():
            dal_ref[...] = dal
            ddt_ref[...] = ddt

        @pl.when(step > 0)
        def _():
            dal_ref[...] += dal
            ddt_ref[...] += ddt

    vec = pl.BlockSpec((1, DN_HEADS), lambda s: (0, 0))
    qs = pl.BlockSpec((cb * CHUNK, DN_QK), lambda s: (s, 0))
    ks = pl.BlockSpec((cb * CHUNK, DN_QK), lambda s: (s, 1))
    vs = pl.BlockSpec((cb * CHUNK, DN_V), lambda s: (s, 1))
    v0 = pl.BlockSpec((cb * CHUNK, DN_V), lambda s: (s, 0))
    st = pl.BlockSpec((cb, DN_HEADS, DN_DK, DN_DV), lambda s: (s, 0, 0, 0))
    return pl.pallas_call(
        body, name=name, grid=(nch // cb,),
        in_specs=[qs, ks, vs, pl.BlockSpec((cb * CHUNK, BA_W), lambda s: (s, 0)), vec, vec, st, st,
                  pl.BlockSpec((cb, DN_HEADS, CHUNK, CHUNK), lambda s: (s, 0, 0, 0)),
                  v0, pl.BlockSpec((cb, DN_HEADS, 2 * CHUNK, DN_DK), lambda s: (s, 0, 0, 0)), v0, v0, v0],
        out_specs=[pl.BlockSpec((cb * CHUNK, DN_CONV_CH), lambda s: (s, 0)),
                   pl.BlockSpec((cb * CHUNK, BA_W), lambda s: (s, 0)), vec, vec],
        out_shape=[jax.ShapeDtypeStruct((l, DN_CONV_CH), f32), jax.ShapeDtypeStruct((l, BA_W), f32),
                   jax.ShapeDtypeStruct((1, DN_HEADS), f32), jax.ShapeDtypeStruct((1, DN_HEADS), f32)],
        compiler_params=_params(("arbitrary",), 48),
    )(qkv, qkv, qkv, ba, a_log, dt_bias, states, dsp_all, tinv_all, u_all, wq, vn_all, do, dvn_all)


def _ffn_fwd(h, hn, wg, wu, wd, tb, th, tag, plan, next_norm_w=None):
    fh, d = wd.shape
    a, b, s = plan.call(f"{tag}_gu", functools.partial(_ffn_gu, tm=th // 2, tn=fh // 2), hn, wg, wu, n_out=3)
    out = plan.matmul(f"{tag}_down", s, wd, mode="nn", tm=th // 2, tn=d, tk=fh, res=h, norm_w=next_norm_w)
    return out, (hn, a, b, s)


def _ffn_bwd(dh, dhb, h, nw, wg, wu, wd, saved, tb, th, tag, plan):
    hn, a, b, s = saved
    d = h.shape[1]
    fh = wd.shape[0]
    layer = tag[-1]
    gr = plan.grads
    da, db = _ffn_ds(dhb, wd, a, b, tm=th // 2, tn=fh // 2, name=f"{tag}_b_ds")
    gr["down" + layer] = _matmul(s, dhb, mode="tn", tm=fh // 2, tn=d, tk=th, out_dtype=bf16, name=f"{tag}_b_dwd")
    dh2, dh2b, dnw = plan.call(f"{tag}_b_dhn", functools.partial(_dhn_norm_bwd, tm=th // 2, tk=fh // 2),
                               [(da, wg), (db, wu)], h, nw, dh, n_out=3)
    resident = dict(mode="tn", tm=d, tn=256, tk=h.shape[0], resident_a=True, vmem_mb=52, out_dtype=bf16)
    gr["gate" + layer] = _matmul(hn, da, name=f"{tag}_b_dwg", **resident)
    gr["up" + layer] = _matmul(hn, db, name=f"{tag}_b_dwu", **resident)
    return dh2, dh2b, dnw


class _Plan:
    GATHERS = {"ret_proj": ("ret_out", "gate0", "up0"), "ret_scan": ("down0", "dn_in"),
               "dn_conv": ("dn_out", "gate1", "up1", "down1")}
    SCATTERS = {"dn_b_conv_a": ("down1", "gate1", "up1", "dn_out"), "ffn0_b_dhn": ("dn_in",),
                "ret_b_scan": ("gate0", "up0"), "ret_b_dwin": ("down0", "ret_out"), "ret_b_dhn": ("ret_in",)}

    def __init__(self, shards, wts):
        self.shards, self.wts, self.grads, self.parts = shards, wts, {}, {}

    def _exchange(self, stage):
        if self.shards is None:
            return None
        if stage in self.GATHERS:
            return _Exchange([self.shards[n] for n in self.GATHERS[stage]], True)
        if stage in self.SCATTERS:
            return _Exchange([self._dev_major(n) for n in self.SCATTERS[stage]], False)
        return None

    def _dev_major(self, name):
        g = self.grads
        if name[:-1] in ("gate", "up"):
            return _dev_major_cols(g[name], g[name].shape[1] // N_DEV)
        if name[:-1] == "down":
            dwd = g[name]
            return dwd.reshape(N_DEV, dwd.shape[0] // N_DEV, dwd.shape[1])
        if name in ("ret_out", "dn_out"):
            return g[name].reshape(N_DEV, g[name].shape[0] // N_DEV, g[name].shape[1])
        return _dev_major_cols(g[name], self.shards[name].shape[-1])

    def _landed(self, stage, outs):
        if stage in self.SCATTERS:
            self.parts.update(zip(self.SCATTERS[stage], outs))
            return
        w = self.wts
        cols = lambda t: t.transpose(1, 0, 2).reshape(t.shape[1], N_DEV * t.shape[2])
        rows = lambda t: t.reshape(N_DEV * t.shape[1], t.shape[2])
        for name, t in zip(self.GATHERS[stage], outs):
            if name in ("ret_out", "dn_out") or name.startswith("down"):
                w[name] = rows(t)
            elif name == "dn_in":
                full = cols(t)
                n_main = DN_CONV_CH + DN_V
                w["dn_main"] = full[:, :n_main]
                w["dn_ba"] = jnp.pad(full[:, n_main:], ((0, 0), (0, BA_W - (full.shape[1] - n_main))))
            else:
                w[name] = cols(t)

    def matmul(self, stage, a, b, **kw):
        comm = self._exchange(stage)
        if comm is None:
            return _matmul(a, b, name=stage, **kw)
        out, landed = _matmul(a, b, name=stage, comm=comm, **kw)
        self._landed(stage, landed)
        return out

    def call(self, stage, fn, *args, n_out):
        comm = self._exchange(stage)
        out = fn(*args, stage, comm=comm)
        if comm is not None:
            self._landed(stage, out[n_out:])
        return out[:n_out]


def _local_step(x2, target, wts, shards=None):
    plan = _Plan(shards, wts)
    s_len, d = x2.shape
    l = s_len + CHUNK
    tb = _tile(l, 3072)
    th = tb // 2 if (tb // 2) % 16 == 0 else tb
    half = RET_DK // 2
    inv_freq = (np.float32(ROPE_BASE) ** (-np.arange(half, dtype=np.float32) / np.float32(half))).astype(np.float32)
    ang = (np.arange(l) - PAD).astype(np.float32)[:, None] * inv_freq[None, :]
    cos, sin = jnp.asarray(np.cos(ang), f32), jnp.asarray(np.sin(ang), f32)
    lgs = jnp.log1p(-jnp.exp2(-5.0 - jnp.arange(RET_HEADS, dtype=f32)))
    gcs = jnp.exp(lgs * _ret_block(l))

    mixw, ffnw = wts["mix_norm"], wts["ffn_norm"]

    h0, hn0 = _embed_norm(x2, wts["meta"], mixw[0:1], "l0_norm")
    proj0 = plan.matmul("ret_proj", hn0, wts["ret_in"], mode="nn", tm=tb, tn=512, tk=d)
    (qk0,) = plan.call("ret_prep", _ret_prep, proj0, cos, sin, n_out=1)
    o0, st0, y0 = plan.call("ret_scan", _ret_scan_fwd, qk0, proj0, wts["ret_gn"], lgs, gcs, n_out=3)
    h1, hn1 = _matmul(y0, wts["ret_out"], mode="nn", tm=th // 2, tn=d, tk=RET_V, res=h0, norm_w=ffnw[0:1],
                      name="ret_out")
    (h2, hn2), ffn0 = _ffn_fwd(h1, hn1, wts["gate0"], wts["up0"], wts["down0"], tb, th, "ffn0", plan,
                               next_norm_w=mixw[1:2])

    proj1 = plan.matmul("dn_proj", hn2, wts["dn_main"], mode="nn", tm=tb, tn=512, tk=d)
    ba = _matmul(hn2, wts["dn_ba"], mode="nn", tm=tb, tn=BA_W, tk=d, name="dn_proj_ba")
    (qkv1,) = plan.call("dn_conv", _dn_conv_fwd, proj1, wts["conv_w"], n_out=1)
    tinv1, u1, wq1, pk1, egl1, kpt1, qwt1 = plan.call("dn_prep", _dn_prep, qkv1, ba, wts["a_log"], wts["dt_bias"],
                                                      n_out=7)
    o1, st1, vn1 = _dn_scan_fwd(u1, wq1, pk1, egl1, "dn_scan")
    y1 = _gnorm_fwd(o1, proj1, wts["dn_norm"], DN_HEADS, DN_DV, 2, "dn_gnorm")
    h3, hn3 = _matmul(y1, wts["dn_out"], mode="nn", tm=th // 2, tn=d, tk=DN_V, res=h2, norm_w=ffnw[1:2],
                      name="dn_out")
    h4, ffn1 = _ffn_fwd(h3, hn3, wts["gate1"], wts["up1"], wts["down1"], tb, th, "ffn1", plan)

    dh4, dh4b, dfinal, loss = _final_loss(h4, wts["final_norm"], target, "final_loss")
    gr = plan.grads
    dh3, dh3b, dffn1 = _ffn_bwd(dh4, dh4b, h3, ffnw[1:2], wts["gate1"], wts["up1"], wts["down1"], ffn1,
                                tb, th, "ffn1", plan)

    gr["dn_out"] = _matmul(y1, dh3b, mode="tn", tm=1024, tn=d, tk=tb, out_dtype=bf16, name="dn_b_dwout")
    do1, dproj1, ddn_norm = _dy_gnorm_bwd(dh3b, wts["dn_out"], o1, proj1, wts["dn_norm"], DN_DV, 2, "dn_b_gnorm",
                                          tm=th // 2, tn=1024)
    dvn1, dsp1 = _dn_scan_bwd(do1, kpt1, qwt1, egl1, "dn_b_scan")
    dqkv1, dba, dalog, ddt = _dn_post_bwd(qkv1, ba, wts["a_log"], wts["dt_bias"], st1, dsp1, tinv1, u1, wq1, vn1,
                                          do1, dvn1, "dn_b_post")
    dc1, dconv = plan.call("dn_b_conv_a", _dn_conv_bwd_a, proj1, wts["conv_w"], dqkv1, n_out=2)
    dproj1 = _dn_conv_bwd_b(dc1, wts["conv_w"], dproj1, "dn_b_conv_b")
    dbab = dba.astype(bf16)
    n_main = dproj1.shape[1]
    dhn2_ba = _matmul(dbab, wts["dn_ba"], mode="nt", tm=th, tn=d, tk=BA_W, name="dn_b_dhn_ba")
    dh2, dh2b, dmix1 = plan.call("dn_b_dhn", functools.partial(_dhn_norm_bwd, tm=th // 2, tk=n_main // 4,
                                                              init=dhn2_ba),
                                 [(dproj1, wts["dn_main"])], h2, mixw[1:2], dh3, n_out=3)
    dw_main = _matmul(hn2, dproj1, mode="tn", tm=d, tn=512, tk=l, resident_a=True, vmem_mb=52, out_dtype=bf16,
                      name="dn_b_dwin")
    dw_ba = _matmul(hn2, dbab, mode="tn", tm=d, tn=BA_W, tk=tb, out_dtype=bf16, name="dn_b_dwin_ba")
    gr["dn_in"] = jnp.concatenate([dw_main, dw_ba], axis=1)

    dh1, dh1b, dffn0 = _ffn_bwd(dh2, dh2b, h1, ffnw[0:1], wts["gate0"], wts["up0"], wts["down0"], ffn0,
                                tb, th, "ffn0", plan)

    gr["ret_out"] = _matmul(y0, dh1b, mode="tn", tm=1024, tn=d, tk=tb, out_dtype=bf16, name="ret_b_dwout")
    do0, dproj0, dret_gn = _dy_gnorm_bwd(dh1b, wts["ret_out"], o0, proj0, wts["ret_gn"], RET_DV, 2, "ret_b_gnorm",
                                         tm=th // 2, tn=1024)
    dq0, dk0, dproj0 = plan.call("ret_b_scan", _ret_scan_bwd, qk0, proj0, st0, do0, dproj0, lgs, gcs, n_out=3)
    dproj0 = _ret_prep_bwd(dq0, dk0, cos, sin, dproj0, "ret_b_prep")
    n_in = dproj0.shape[1]
    gr["ret_in"] = plan.matmul("ret_b_dwin", hn0, dproj0, mode="tn", tm=d, tn=512, tk=l, resident_a=True, vmem_mb=52,
                               out_dtype=bf16)
    dh0, _, dmix0 = plan.call("ret_b_dhn", functools.partial(_dhn_norm_bwd, tm=th // 2, tk=n_in // 4),
                              [(dproj0, wts["ret_in"])], h0, mixw[0:1], dh1, n_out=3)

    gr.update(meta=dh0[PAD:CHUNK], mix_norm=jnp.concatenate([dmix0, dmix1], axis=0),
              ffn_norm=jnp.concatenate([dffn0, dffn1], axis=0), ret_gn=dret_gn, conv_w=dconv, a_log=dalog,
              dt_bias=ddt, dn_norm=ddn_norm, final_norm=dfinal)
    return loss, dh0[CHUNK:], gr, plan


def _adamw_reduce(parts, w, m, v, name):
    _, r, c = parts.shape
    c_pad = -(-c // LANES) * LANES
    tr = _div_tile(r, max(8, (3 * MIB // 16) // c_pad // 8 * 8), 16)

    def body(p_ref, w_ref, m_ref, v_ref, g_ref, d_ref, nm_ref, nv_ref):
        g = p_ref[0].astype(f32)
        for s in range(1, N_DEV):
            g = g + p_ref[s].astype(f32)
        mm = ADAM_B1 * m_ref[...] + (1.0 - ADAM_B1) * g
        vv = ADAM_B2 * v_ref[...] + (1.0 - ADAM_B2) * (g * g)
        m_hat = mm / (1.0 - ADAM_B1 ** ADAM_STEP)
        v_hat = vv / (1.0 - ADAM_B2 ** ADAM_STEP)
        g_ref[...] = g
        d_ref[...] = -ADAM_LR * (m_hat / (jnp.sqrt(v_hat) + ADAM_EPS) + ADAM_WD * w_ref[...])
        nm_ref[...] = mm
        nv_ref[...] = vv

    blk = pl.BlockSpec((tr, c), lambda i: (i, 0))
    return pl.pallas_call(
        body, name=name, grid=(r // tr,),
        in_specs=[pl.BlockSpec((N_DEV, tr, c), lambda i: (0, i, 0)), blk, blk, blk], out_specs=[blk] * 4,
        out_shape=[jax.ShapeDtypeStruct((r, c), f32)] * 4,
        compiler_params=_params(("parallel",), 48),
    )(parts, w, m, v)


def _dev_major_cols(g, width):
    r = g.shape[0]
    return g[:, :N_DEV * width].reshape(r, N_DEV, width).transpose(1, 0, 2)


def kernel(x, meta_tokens, mix_norm_w, ffn_norm_w, ret_w_in, ret_gn_w, ret_w_out, dn_w_in, dn_conv_w, dn_a_log, dn_dt_bias, dn_norm_w, dn_w_out, ffn_w_gate, ffn_w_up, ffn_w_down, final_norm_w, loss_target, m_meta_tokens, m_mix_norm_w, m_ffn_norm_w, m_ret_w_in, m_ret_gn_w, m_ret_w_out, m_dn_w_in, m_dn_conv_w, m_dn_a_log, m_dn_dt_bias, m_dn_norm_w, m_dn_w_out, m_ffn_w_gate, m_ffn_w_up, m_ffn_w_down, m_final_norm_w, v_meta_tokens, v_mix_norm_w, v_ffn_norm_w, v_ret_w_in, v_ret_gn_w, v_ret_w_out, v_dn_w_in, v_dn_conv_w, v_dn_a_log, v_dn_dt_bias, v_dn_norm_w, v_dn_w_out, v_ffn_w_gate, v_ffn_w_up, v_ffn_w_down, v_final_norm_w):
    d = x.shape[-1]
    me = 4 * lax.axis_index("x") + 2 * lax.axis_index("y") + lax.axis_index("c")

    shards = dict(ret_in=ret_w_in[0].astype(bf16), ret_out=ret_w_out[0].astype(bf16),
                  dn_in=dn_w_in[0].astype(bf16), dn_out=dn_w_out[0].astype(bf16))
    for layer in (0, 1):
        shards[f"gate{layer}"] = ffn_w_gate[layer].astype(bf16)
        shards[f"up{layer}"] = ffn_w_up[layer].astype(bf16)
        shards[f"down{layer}"] = ffn_w_down[layer].astype(bf16)
    g_ret_in, g_meta, g_conv, g_dnn = _exchange([shards["ret_in"], meta_tokens, dn_conv_w[0], dn_norm_w], True,
                                                "gather_first")
    cols = lambda g: g.transpose(1, 0, 2).reshape(g.shape[1], N_DEV * g.shape[2])
    wts = dict(meta=cols(g_meta), mix_norm=mix_norm_w, ffn_norm=ffn_norm_w, ret_in=cols(g_ret_in), ret_gn=ret_gn_w,
               conv_w=cols(g_conv), a_log=dn_a_log, dt_bias=dn_dt_bias, dn_norm=cols(g_dnn),
               final_norm=final_norm_w.reshape(1, d))

    loss_part, grad_x, gr, plan = _local_step(x[0], loss_target[0], wts, shards)
    loss = lax.psum(loss_part[0, 0], AXES)

    pp = plan.parts
    both = lambda name: jnp.concatenate([pp[name + "0"], pp[name + "1"]], axis=1)
    big_parts = [pp["ret_in"], pp["ret_out"], pp["dn_in"], pp["dn_out"], both("gate"), both("up"), both("down")]
    big_names = ["ret_w_in", "ret_w_out", "dn_w_in", "dn_w_out", "ffn_w_gate", "ffn_w_up", "ffn_w_down"]
    big_w = [ret_w_in, ret_w_out, dn_w_in, dn_w_out, ffn_w_gate, ffn_w_up, ffn_w_down]
    big_m = [m_ret_w_in, m_ret_w_out, m_dn_w_in, m_dn_w_out, m_ffn_w_gate, m_ffn_w_up, m_ffn_w_down]
    big_v = [v_ret_w_in, v_ret_w_out, v_dn_w_in, v_dn_w_out, v_ffn_w_gate, v_ffn_w_up, v_ffn_w_down]
    res = {}
    for nm, parts, w_, m_, v_ in zip(big_names, big_parts, big_w, big_m, big_v):
        r2, c2 = parts.shape[1], parts.shape[2]
        outs = _adamw_reduce(parts, w_.reshape(r2, c2), m_.reshape(r2, c2), v_.reshape(r2, c2), f"adamw_{nm}")
        res[nm] = [o.reshape(w_.shape) for o in outs]

    small_names = ["meta_tokens", "mix_norm_w", "ffn_norm_w", "ret_gn_w", "dn_conv_w", "dn_a_log", "dn_dt_bias",
                   "dn_norm_w", "final_norm_w"]
    small_g = [gr["meta"], gr["mix_norm"], gr["ffn_norm"], gr["ret_gn"], gr["conv_w"], gr["a_log"], gr["dt_bias"],
               gr["dn_norm"], gr["final_norm"]]
    small_w = [meta_tokens, mix_norm_w, ffn_norm_w, ret_gn_w, dn_conv_w, dn_a_log, dn_dt_bias, dn_norm_w, final_norm_w]
    small_m = [m_meta_tokens, m_mix_norm_w, m_ffn_norm_w, m_ret_gn_w, m_dn_conv_w, m_dn_a_log, m_dn_dt_bias,
               m_dn_norm_w, m_final_norm_w]
    small_v = [v_meta_tokens, v_mix_norm_w, v_ffn_norm_w, v_ret_gn_w, v_dn_conv_w, v_dn_a_log, v_dn_dt_bias,
               v_dn_norm_w, v_final_norm_w]
    sharded = {"meta_tokens", "dn_conv_w", "dn_norm_w"}
    flat = jnp.concatenate([g.reshape(-1) for g in small_g])
    row = 8 * LANES
    n_flat = flat.shape[0]
    flat = jnp.pad(flat, (0, -n_flat % row)).reshape(-1, row)
    (gathered,) = _exchange([flat], True, "gather_small_grads")
    gathered = gathered.reshape(N_DEV, -1)
    pieces, off = [], 0
    for nm, g, w_ in zip(small_names, small_g, small_w):
        full = gathered[:, off:off + g.size].reshape((N_DEV,) + g.shape)
        off += g.size
        if nm in sharded:
            wloc = w_.shape[-1]
            full = lax.dynamic_slice_in_dim(full, me * wloc, wloc, axis=full.ndim - 1)
        pieces.append(full.reshape(N_DEV, -1))
    sizes = [p.shape[1] for p in pieces]
    n_loc = sum(sizes)
    pad_loc = -n_loc % row

    def pack(vs, lead):
        cat = jnp.concatenate([a.reshape(lead + (-1,)) for a in vs], axis=-1)
        cat = jnp.pad(cat, [(0, 0)] * len(lead) + [(0, pad_loc)])
        return cat.reshape(lead + (-1, row))

    outs = _adamw_reduce(pack(pieces, (N_DEV,)), pack(small_w, ()), pack(small_m, ()), pack(small_v, ()), "adamw_small")
    off = 0
    for nm, sz, w_ in zip(small_names, sizes, small_w):
        res[nm] = [o.reshape(-1)[off:off + sz].reshape(w_.shape) for o in outs]
        off += sz

    order = ["meta_tokens", "mix_norm_w", "ffn_norm_w", "ret_w_in", "ret_gn_w", "ret_w_out", "dn_w_in", "dn_conv_w",
             "dn_a_log", "dn_dt_bias", "dn_norm_w", "dn_w_out", "ffn_w_gate", "ffn_w_up", "ffn_w_down", "final_norm_w"]
    grad_x = grad_x.reshape(x.shape)
    return (loss, grad_x, *[res[nm][0] for nm in order], *[res[nm][1] for nm in order],
            *[res[nm][2] for nm in order], *[res[nm][3] for nm in order])
```

```python
import functools
import math

import jax
import jax.numpy as jnp
import numpy as np
from jax import lax
from jax.experimental import pallas as pl
from jax.experimental.pallas import tpu as pltpu

f32 = jnp.float32
bf16 = jnp.bfloat16

N_META = 16
CHUNK = 64
PAD = CHUNK - N_META
RMS_EPS = 1e-6
RET_HEADS, RET_DK, RET_DV = 4, 256, 512
RET_QK, RET_V = RET_HEADS * RET_DK, RET_HEADS * RET_DV
DN_HEADS, DN_DK, DN_DV = 8, 128, 256
DN_QK, DN_V = DN_HEADS * DN_DK, DN_HEADS * DN_DV
DN_CONV_CH = 2 * DN_QK + DN_V
CONV_K = 4
ROPE_BASE = 10000.0
ADAM_LR, ADAM_B1, ADAM_B2, ADAM_EPS, ADAM_WD, ADAM_STEP = 0.001, 0.9, 0.999, 1e-08, 0.01, 10
N_DEV = 8
AXES = ("x", "y", "c")
LANES = 128
MIB = 1024 * 1024


def _tile(n_rows, cap):
    nch = n_rows // CHUNK
    best = 1
    for d in range(1, nch + 1):
        if nch % d == 0 and d * CHUNK <= cap:
            best = d
    return best * CHUNK


def _div_tile(n, cap, align):
    best = None
    for d in range(align, min(n, cap) + 1, align):
        if n % d == 0:
            best = d
    return best if best is not None else n


def _params(sem, vmem_mb):
    return pltpu.CompilerParams(dimension_semantics=sem, vmem_limit_bytes=int(vmem_mb * MIB))


def _nn(a, b, precision=None):
    return jnp.dot(a, b, preferred_element_type=f32, precision=precision)


def _nt(a, b, precision=None):
    return lax.dot_general(a, b, (((1,), (1,)), ((), ())), preferred_element_type=f32, precision=precision)


def _tn(a, b, precision=None):
    return lax.dot_general(a, b, (((0,), (0,)), ((), ())), preferred_element_type=f32, precision=precision)


def _b(x):
    return x.astype(bf16)


def _sigmoid(x):
    return 0.5 * jnp.tanh(0.5 * x) + 0.5


def _silu(x):
    return x * _sigmoid(x)


def _dsilu(x):
    s = _sigmoid(x)
    return s * (1.0 + x * (1.0 - s))


def _peer(k):
    x, y, c = lax.axis_index("x"), lax.axis_index("y"), lax.axis_index("c")
    px = 1 - x if k & 4 else x
    py = 1 - y if k & 2 else y
    pc = 1 - c if k & 1 else c
    return (px, py, pc), 4 * px + 2 * py + pc


class _Exchange:
    def __init__(self, arrs, gather):
        self.arrs, self.gather, self.n = list(arrs), gather, len(arrs)
        self.out_shapes = [jax.ShapeDtypeStruct(((N_DEV,) + a.shape) if gather else a.shape, a.dtype) for a in arrs]
        self.specs = [pl.BlockSpec(memory_space=pltpu.HBM)] * self.n
        self.scratch = [pltpu.SemaphoreType.DMA((self.n, N_DEV - 1)), pltpu.SemaphoreType.DMA((self.n, N_DEV - 1)),
                        pltpu.SemaphoreType.DMA((self.n,))]

    def _copies(self, ins, outs, sems):
        send_sems, recv_sems, local_sems = sems
        x, y, c = lax.axis_index("x"), lax.axis_index("y"), lax.axis_index("c")
        me = 4 * x + 2 * y + c

        def copy(a, k, src, slot, to):
            return pltpu.make_async_remote_copy(
                src_ref=src, dst_ref=outs[a].at[slot], send_sem=send_sems.at[a, k], recv_sem=recv_sems.at[a, k],
                device_id=to, device_id_type=pl.DeviceIdType.MESH)

        first, passed, lands_first, lands_rest = [], [], [], []
        if not self.gather:
            local = [pltpu.make_async_copy(ins[a].at[me], outs[a].at[me], local_sems.at[a]) for a in range(self.n)]
            for k in range(1, N_DEV):
                peer, pidx = _peer(k)
                for a in range(self.n):
                    first.append(copy(a, k - 1, ins[a].at[pidx], me, peer))
                    lands_rest.append(copy(a, k - 1, ins[a].at[pidx], pidx, peer))
            return local, first, passed, lands_first, lands_rest
        local = [pltpu.make_async_copy(ins[a], outs[a].at[me], local_sems.at[a]) for a in range(self.n)]
        sibling, sibling_slot = (x, y, 1 - c), 4 * x + 2 * y + (1 - c)
        chips = [(1 - x, y), (x, 1 - y), (1 - x, 1 - y)]
        for a in range(self.n):
            first.append(copy(a, 0, ins[a], me, sibling))
            lands_rest.append(copy(a, 0, ins[a], sibling_slot, sibling))
            for j, (px, py) in enumerate(chips):
                slot, slot_other = 4 * px + 2 * py + c, 4 * px + 2 * py + (1 - c)
                first.append(copy(a, 1 + j, ins[a], me, (px, py, c)))
                lands_first.append(copy(a, 1 + j, ins[a], slot, (px, py, c)))
                passed.append(copy(a, 4 + j, outs[a].at[slot], slot, sibling))
                lands_rest.append(copy(a, 4 + j, outs[a].at[slot_other], slot_other, sibling))
        return local, first, passed, lands_first, lands_rest

    def start(self, ins, outs, sems):
        local, first, _, _, _ = self._copies(ins, outs, sems)
        for cp in local + first:
            cp.start()

    def wait(self, ins, outs, sems):
        local, first, passed, lands_first, lands_rest = self._copies(ins, outs, sems)
        for landed, onward in zip(lands_first, passed):
            landed.wait_recv()
            onward.start()
        for cp in lands_rest:
            cp.wait_recv()
        for cp in first + passed:
            cp.wait_send()
        for cp in local:
            cp.wait()


def _call(body, args, *, name, grid, in_specs, out_specs, out_shape, scratch=(), sem, vmem_mb, comm=None,
          aliases=None):
    aliases = aliases or {}
    if comm is None:
        out = pl.pallas_call(body, name=name, grid=grid, in_specs=list(in_specs), out_specs=list(out_specs),
                             out_shape=list(out_shape), scratch_shapes=list(scratch), input_output_aliases=aliases,
                             compiler_params=_params(sem, vmem_mb))(*args)
        return list(out)
    n_in, n_out, n_scr, nc = len(args), len(out_shape), len(scratch), comm.n

    def carried(*refs):
        ins, cin = refs[:n_in], refs[n_in:n_in + nc]
        o0 = n_in + nc
        outs, cout = refs[o0:o0 + n_out], refs[o0 + n_out:o0 + n_out + nc]
        s0 = o0 + n_out + nc
        scr, sems = refs[s0:s0 + n_scr], refs[s0 + n_scr:]
        first = functools.reduce(jnp.logical_and, [pl.program_id(i) == 0 for i in range(len(grid))])
        last = functools.reduce(jnp.logical_and, [pl.program_id(i) == grid[i] - 1 for i in range(len(grid))])

        @pl.when(first)
        def _():
            comm.start(cin, cout, sems)

        body(*ins, *outs, *scr)

        @pl.when(last)
        def _():
            comm.wait(cin, cout, sems)

    out = pl.pallas_call(
        carried, name=name, grid=grid, in_specs=list(in_specs) + comm.specs, out_specs=list(out_specs) + comm.specs,
        out_shape=list(out_shape) + comm.out_shapes, scratch_shapes=list(scratch) + comm.scratch,
        input_output_aliases=aliases,
        compiler_params=_params(("arbitrary",) * len(grid), vmem_mb))(*args, *comm.arrs)
    return list(out)


def _exchange(arrs, gather, name):
    comm = _Exchange(arrs, gather)

    def body(*refs):
        ins, outs, sems = refs[:comm.n], refs[comm.n:2 * comm.n], refs[2 * comm.n:]
        comm.start(ins, outs, sems)
        comm.wait(ins, outs, sems)

    return pl.pallas_call(body, name=name, in_specs=comm.specs, out_specs=comm.specs, out_shape=comm.out_shapes,
                          scratch_shapes=comm.scratch)(*comm.arrs)


def _matmul(a, b, *, mode, tm, tn, tk, name, out_dtype=f32, res=None, vmem_mb=48, comm=None, norm_w=None,
            resident_a=False):
    if mode == "nn":
        (m, k), (k2, n) = a.shape, b.shape
    elif mode == "nt":
        (m, k), (n, k2) = a.shape, b.shape
    else:
        (k, m), (k2, n) = a.shape, b.shape
    assert k == k2 and m % tm == 0 and n % tn == 0 and k % tk == 0, (name, a.shape, b.shape, tm, tn, tk)
    nk = k // tk
    dot = {"nn": _nn, "nt": _nt, "tn": _tn}[mode]
    a_spec = {"nn": pl.BlockSpec((tm, tk), lambda i, j, kk: (i, kk)),
              "nt": pl.BlockSpec((tm, tk), lambda i, j, kk: (i, kk)),
              "tn": pl.BlockSpec((tk, tm), lambda i, j, kk: (kk, i))}[mode]
    if resident_a:
        assert mode == "tn" and nk == 1
        a_spec = pl.BlockSpec((tk, tm), lambda i, j, kk: (0, i), pipeline_mode=pl.Buffered(1))
    b_spec = {"nn": pl.BlockSpec((tk, tn), lambda i, j, kk: (kk, j)),
              "nt": pl.BlockSpec((tn, tk), lambda i, j, kk: (j, kk)),
              "tn": pl.BlockSpec((tk, tn), lambda i, j, kk: (kk, j))}[mode]
    o_spec = pl.BlockSpec((tm, tn), lambda i, j, kk: (i, j))
    has_res = res is not None
    has_norm = norm_w is not None
    assert not has_norm or tn == n
    n_ops = 2 + has_res + has_norm

    def body(*refs):
        a_ref, b_ref = refs[:2]
        r_ref = refs[2] if has_res else None
        nw_ref = refs[2 + has_res] if has_norm else None
        o_ref = refs[n_ops]
        hn_ref = refs[n_ops + 1] if has_norm else None
        rest = refs[n_ops + 1 + has_norm:]

        def finish(tot):
            if has_res:
                tot = tot + r_ref[...]
            o_ref[...] = tot.astype(out_dtype)
            if has_norm:
                r = lax.rsqrt(jnp.mean(tot * tot, axis=-1, keepdims=True) + RMS_EPS)
                hn_ref[...] = _b(tot * r * nw_ref[...])

        if resident_a:
            at_ref = rest[0]

            @pl.when(pl.program_id(1) == 0)
            def _():
                at_ref[...] = _b(a_ref[...]).T

            finish(_nn(at_ref[...], _b(b_ref[...])))
            return
        if nk == 1:
            finish(dot(_b(a_ref[...]), _b(b_ref[...])))
            return
        acc_ref = rest[0]
        kk = pl.program_id(2)

        @pl.when(kk == 0)
        def _():
            acc_ref[...] = dot(_b(a_ref[...]), _b(b_ref[...]))

        @pl.when(kk > 0)
        def _():
            acc_ref[...] += dot(_b(a_ref[...]), _b(b_ref[...]))

        @pl.when(kk == nk - 1)
        def _():
            finish(acc_ref[...])

    in_specs = [a_spec, b_spec]
    args = (a, b)
    if has_res:
        in_specs.append(o_spec)
        args += (res,)
    out_specs, out_shape = [o_spec], [jax.ShapeDtypeStruct((m, n), out_dtype)]
    if has_norm:
        in_specs.append(pl.BlockSpec((1, tn), lambda i, j, kk: (0, j)))
        args += (norm_w,)
        out_specs.append(o_spec)
        out_shape.append(jax.ShapeDtypeStruct((m, n), bf16))
    out = _call(body, args, name=name, grid=(m // tm, n // tn, nk), in_specs=in_specs, out_specs=out_specs,
                out_shape=out_shape,
                scratch=[pltpu.VMEM((tm, k), bf16)] if resident_a else [pltpu.VMEM((tm, tn), f32)] if nk > 1 else [],
                sem=("parallel", "arbitrary", "arbitrary"), vmem_mb=vmem_mb, comm=comm)
    n_own = len(out_shape)
    own = out[0] if n_own == 1 else tuple(out[:n_own])
    return own if comm is None else (own, out[n_own:])


def _embed_norm(x2, meta, w, name):
    s_len, d = x2.shape
    l = s_len + CHUNK
    cpt = _tile(l, 256) // CHUNK

    def body(m_ref, w_ref, *rest):
        x_refs, (h_ref, hn_ref) = rest[:cpt], rest[cpt:]
        i = pl.program_id(0)
        prefix = jnp.concatenate([jnp.zeros((PAD, d), f32), m_ref[...]], axis=0)
        wv = w_ref[...]
        for c in range(cpt):
            rows = slice(c * CHUNK, (c + 1) * CHUNK)
            x = jnp.where(i * cpt + c > 0, x_refs[c][...], prefix)
            h_ref[rows, :] = x
            r = lax.rsqrt(jnp.mean(x * x, axis=-1, keepdims=True) + RMS_EPS)
            hn_ref[rows, :] = _b(x * r * wv)

    row = pl.BlockSpec((cpt * CHUNK, d), lambda i: (i, 0))
    x_specs = [pl.BlockSpec((CHUNK, d), functools.partial(lambda i, c: (jnp.maximum(i * cpt + c - 1, 0), 0), c=c))
               for c in range(cpt)]
    return pl.pallas_call(
        body, name=name, grid=(l // (cpt * CHUNK),),
        in_specs=[pl.BlockSpec((N_META, d), lambda i: (0, 0)), pl.BlockSpec((1, d), lambda i: (0, 0))] + x_specs,
        out_specs=[row, row],
        out_shape=[jax.ShapeDtypeStruct((l, d), f32), jax.ShapeDtypeStruct((l, d), bf16)],
        compiler_params=_params(("parallel",), 32),
    )(meta, w, *([x2] * cpt))


def _dhn_norm_bwd(pairs, h, nw, dres, name, *, tm, tk, init=None, comm=None):
    l, d = h.shape
    nks = [a.shape[1] // tk for a, _ in pairs]
    starts = [sum(nks[:p]) for p in range(len(pairs))]
    nk = sum(nks)
    assert nk >= 2
    n_ops = 2 * len(pairs)
    has_init = init is not None

    def body(*refs):
        ops = refs[:n_ops]
        init_ref = refs[n_ops] if has_init else None
        h_ref, w_ref, r_ref, dh_ref, dhb_ref, dw_ref, acc_ref = refs[n_ops + has_init:]
        i, kk = pl.program_id(0), pl.program_id(1)

        @pl.when(kk == 0)
        def _():
            part = _nt(ops[0][...], ops[1][...])
            acc_ref[...] = part + init_ref[...] if has_init else part

        for p in range(len(pairs)):
            lo, hi = max(starts[p], 1), min(starts[p] + nks[p], nk - 1)

            @pl.when(jnp.logical_and(kk >= lo, kk < hi))
            def _(a_ref=ops[2 * p], b_ref=ops[2 * p + 1]):
                acc_ref[...] += _nt(a_ref[...], b_ref[...])

        @pl.when(kk == nk - 1)
        def _():
            g = acc_ref[...] + _nt(ops[-2][...], ops[-1][...])
            x = h_ref[...]
            r = lax.rsqrt(jnp.mean(x * x, axis=-1, keepdims=True) + RMS_EPS)
            xh = x * r
            dxh = g * w_ref[...]
            dh = r_ref[...] + r * (dxh - xh * jnp.mean(dxh * xh, axis=-1, keepdims=True))
            dh_ref[...] = dh
            dhb_ref[...] = _b(dh)
            dw = jnp.sum(g * xh, axis=0, keepdims=True)

            @pl.when(i == 0)
            def _():
                dw_ref[...] = dw

            @pl.when(i > 0)
            def _():
                dw_ref[...] += dw

    def k_of(p):
        return lambda kk: jnp.clip(kk - starts[p], 0, nks[p] - 1)

    in_specs, args = [], []
    for p, (a, b) in enumerate(pairs):
        in_specs += [pl.BlockSpec((tm, tk), functools.partial(lambda i, kk, f: (i, f(kk)), f=k_of(p))),
                     pl.BlockSpec((d, tk), functools.partial(lambda i, kk, f: (0, f(kk)), f=k_of(p)))]
        args += [a, b]
    row = pl.BlockSpec((tm, d), lambda i, kk: (i, 0))
    vec = pl.BlockSpec((1, d), lambda i, kk: (0, 0))
    if has_init:
        in_specs.append(row)
        args.append(init)
    return _call(body, tuple(args) + (h, nw, dres), name=name, grid=(l // tm, nk), in_specs=in_specs + [row, vec, row],
                 out_specs=[row, row, vec],
                 out_shape=[jax.ShapeDtypeStruct((l, d), f32), jax.ShapeDtypeStruct((l, d), bf16),
                            jax.ShapeDtypeStruct((1, d), f32)],
                 scratch=[pltpu.VMEM((tm, d), f32)], sem=("arbitrary", "arbitrary"), vmem_mb=48, comm=comm)


def _final_loss(h, w, target, name):
    l, d = h.shape
    nch = l // CHUNK
    cpt = _tile(l, 256) // CHUNK
    nt = nch // cpt

    def body(h_ref, w_ref, *rest):
        t_refs, (dh_ref, dhb_ref, dw_ref, loss_ref) = rest[:cpt], rest[cpt:]
        i = pl.program_id(0)
        wv = w_ref[...]
        dw = jnp.zeros((1, d), f32)
        part = jnp.zeros((1, 1), f32)
        for c in range(cpt):
            rows = slice(c * CHUNK, (c + 1) * CHUNK)
            live = (i * cpt + c > 0).astype(f32)
            x = h_ref[rows, :]
            r = lax.rsqrt(jnp.mean(x * x, axis=-1, keepdims=True) + RMS_EPS)
            xh = x * r
            err = (xh * wv - t_refs[c][...]) * live
            dy = err * (1.0 / d)
            dxh = dy * wv
            dx = r * (dxh - xh * jnp.mean(dxh * xh, axis=-1, keepdims=True))
            dh_ref[rows, :] = dx
            dhb_ref[rows, :] = _b(dx)
            dw = dw + jnp.sum(dy * xh, axis=0, keepdims=True)
            part = part + 0.5 * jnp.sum(jnp.sum(err * err, axis=-1, keepdims=True) * (1.0 / d), axis=0, keepdims=True)
        part = jnp.broadcast_to(part, (1, LANES))

        @pl.when(i == 0)
        def _():
            dw_ref[...] = dw
            loss_ref[...] = part

        @pl.when(i > 0)
        def _():
            dw_ref[...] += dw
            loss_ref[...] += part

    row = pl.BlockSpec((cpt * CHUNK, d), lambda i: (i, 0))
    vec = pl.BlockSpec((1, d), lambda i: (0, 0))
    t_specs = [pl.BlockSpec((CHUNK, d), functools.partial(lambda i, c: (jnp.maximum(i * cpt + c - 1, 0), 0), c=c))
               for c in range(cpt)]
    return pl.pallas_call(
        body, name=name, grid=(nt,),
        in_specs=[row, vec] + t_specs,
        out_specs=[row, row, vec, pl.BlockSpec((1, LANES), lambda i: (0, 0))],
        out_shape=[jax.ShapeDtypeStruct((l, d), f32), jax.ShapeDtypeStruct((l, d), bf16),
                   jax.ShapeDtypeStruct((1, d), f32), jax.ShapeDtypeStruct((1, LANES), f32)],
        compiler_params=_params(("arbitrary",), 32),
    )(h, w, *([target] * cpt))


def _ffn_gu(hn, wg, wu, name, *, tm, tn, comm=None):
    l, d = hn.shape
    fh = wg.shape[1]

    def body(h_ref, g_ref, u_ref, a_ref, b_ref, s_ref):
        hb = h_ref[...]
        a = _nn(hb, g_ref[...])
        bb = _nn(hb, u_ref[...])
        a_ref[...] = _b(a)
        b_ref[...] = _b(bb)
        s_ref[...] = _b(_silu(a) * bb)

    wspec = pl.BlockSpec((d, tn), lambda i, j: (0, j))
    ospec = pl.BlockSpec((tm, tn), lambda i, j: (i, j))
    return _call(body, (hn, wg, wu), name=name, grid=(l // tm, fh // tn),
                 in_specs=[pl.BlockSpec((tm, d), lambda i, j: (i, 0)), wspec, wspec], out_specs=[ospec] * 3,
                 out_shape=[jax.ShapeDtypeStruct((l, fh), bf16)] * 3, sem=("parallel", "parallel"), vmem_mb=48,
                 comm=comm)


def _ffn_ds(dhb, wd, a, b, *, tm, tn, name):
    l, d = dhb.shape
    fh = wd.shape[0]

    def body(g_ref, w_ref, a_ref, b_ref, da_ref, db_ref):
        ds = _nt(g_ref[...], w_ref[...])
        a = a_ref[...].astype(f32)
        da_ref[...] = _b(ds * b_ref[...].astype(f32) * _dsilu(a))
        db_ref[...] = _b(ds * _silu(a))

    ospec = pl.BlockSpec((tm, tn), lambda i, j: (i, j))
    return pl.pallas_call(
        body, name=name, grid=(l // tm, fh // tn),
        in_specs=[pl.BlockSpec((tm, d), lambda i, j: (i, 0)), pl.BlockSpec((tn, d), lambda i, j: (j, 0)), ospec, ospec],
        out_specs=[ospec, ospec], out_shape=[jax.ShapeDtypeStruct((l, fh), bf16)] * 2,
        compiler_params=_params(("parallel", "parallel"), 48),
    )(dhb, wd, a, b)


def _gnorm_fwd(o, proj, nw, heads, dv, gate_blk, name):
    l, hv = o.shape
    tr = _tile(l, 256)

    def body(o_ref, g_ref, w_ref, y_ref):
        wv = w_ref[...]
        for h in range(heads):
            sl = slice(h * dv, (h + 1) * dv)
            oh = o_ref[:, sl]
            r = lax.rsqrt(jnp.mean(oh * oh, axis=-1, keepdims=True) + RMS_EPS)
            y_ref[:, sl] = _b(oh * r * wv * _silu(g_ref[:, sl]))

    return pl.pallas_call(
        body, name=name, grid=(l // tr,),
        in_specs=[pl.BlockSpec((tr, hv), lambda i: (i, 0)), pl.BlockSpec((tr, hv), lambda i: (i, gate_blk)),
                  pl.BlockSpec((1, dv), lambda i: (0, 0))],
        out_specs=pl.BlockSpec((tr, hv), lambda i: (i, 0)),
        out_shape=jax.ShapeDtypeStruct((l, hv), bf16),
        compiler_params=_params(("parallel",), 32),
    )(o, proj, nw)


def _dy_gnorm_bwd(dhb, w_out, o, proj, nw, dv, gate_blk, name, *, tm, tn):
    l, hv = o.shape
    d = dhb.shape[1]
    nj = hv // tn
    heads = tn // dv

    def body(g_ref, w_ref, o_ref, gate_ref, nw_ref, do_ref, dg_ref, dw_ref):
        dy = _nt(g_ref[...], w_ref[...])
        wv = nw_ref[...]
        dw = jnp.zeros((1, dv), f32)
        for h in range(heads):
            sl = slice(h * dv, (h + 1) * dv)
            oh = o_ref[:, sl]
            g = gate_ref[:, sl]
            dyh = dy[:, sl]
            r = lax.rsqrt(jnp.mean(oh * oh, axis=-1, keepdims=True) + RMS_EPS)
            xh = oh * r
            dn = dyh * _silu(g)
            dg_ref[:, sl] = _b(dyh * (xh * wv) * _dsilu(g))
            dxh = dn * wv
            do_ref[:, sl] = r * (dxh - xh * jnp.mean(dxh * xh, axis=-1, keepdims=True))
            dw = dw + jnp.sum(dn * xh, axis=0, keepdims=True)
        first = jnp.logical_and(pl.program_id(0) == 0, pl.program_id(1) == 0)

        @pl.when(first)
        def _():
            dw_ref[...] = dw

        @pl.when(jnp.logical_not(first))
        def _():
            dw_ref[...] += dw

    tile = pl.BlockSpec((tm, tn), lambda i, j: (i, j))
    gate = pl.BlockSpec((tm, tn), lambda i, j: (i, gate_blk * nj + j))
    vec = pl.BlockSpec((1, dv), lambda i, j: (0, 0))
    return pl.pallas_call(
        body, name=name, grid=(l // tm, nj),
        in_specs=[pl.BlockSpec((tm, d), lambda i, j: (i, 0)), pl.BlockSpec((tn, d), lambda i, j: (j, 0)),
                  tile, gate, vec],
        out_specs=[tile, gate, vec],
        out_shape=[jax.ShapeDtypeStruct((l, hv), f32), jax.ShapeDtypeStruct(proj.shape, bf16),
                   jax.ShapeDtypeStruct((1, dv), f32)],
        compiler_params=_params(("arbitrary", "arbitrary"), 48),
    )(dhb, w_out, o, proj, nw)


def _ret_prep(proj, cos, sin, name, comm=None):
    l = proj.shape[0]
    tr = _tile(l, 256)
    half = RET_DK // 2
    scale = RET_DK ** -0.5

    def body(p_ref, c_ref, s_ref, o_ref):
        rows = pl.program_id(0) * tr + lax.broadcasted_iota(jnp.int32, (tr, 1), 0)
        kmul = jnp.where(rows >= PAD, scale, 0.0).astype(f32)
        c, s = c_ref[...], s_ref[...]
        for j in range(2 * RET_HEADS):
            t1 = p_ref[:, j * RET_DK: j * RET_DK + half]
            t2 = p_ref[:, j * RET_DK + half: (j + 1) * RET_DK]
            o1 = t1 * c - t2 * s
            o2 = t1 * s + t2 * c
            if j >= RET_HEADS:
                o1, o2 = o1 * kmul, o2 * kmul
            o_ref[:, j * RET_DK: j * RET_DK + half] = o1
            o_ref[:, j * RET_DK + half: (j + 1) * RET_DK] = o2

    wide = pl.BlockSpec((tr, 2 * RET_QK), lambda i: (i, 0))
    tab = pl.BlockSpec((tr, half), lambda i: (i, 0))
    return _call(body, (proj, cos, sin), name=name, grid=(l // tr,), in_specs=[wide, tab, tab], out_specs=[wide],
                 out_shape=[jax.ShapeDtypeStruct((l, 2 * RET_QK), f32)], sem=("parallel",), vmem_mb=32, comm=comm)


def _ret_prep_bwd(dq, dk, cos, sin, dproj, name):
    l = dq.shape[0]
    tr = _tile(l, 256)
    half = RET_DK // 2
    scale = RET_DK ** -0.5

    def body(dq_ref, dk_ref, c_ref, s_ref, _, o_ref):
        rows = pl.program_id(0) * tr + lax.broadcasted_iota(jnp.int32, (tr, 1), 0)
        kmul = jnp.where(rows >= PAD, scale, 0.0).astype(f32)
        c, s = c_ref[...], s_ref[...]
        for j in range(2 * RET_HEADS):
            d_ref = dq_ref if j < RET_HEADS else dk_ref
            jj = j % RET_HEADS
            d1 = d_ref[:, jj * RET_DK: jj * RET_DK + half]
            d2 = d_ref[:, jj * RET_DK + half: (jj + 1) * RET_DK]
            if j >= RET_HEADS:
                d1, d2 = d1 * kmul, d2 * kmul
            o_ref[:, j * RET_DK: j * RET_DK + half] = _b(d1 * c + d2 * s)
            o_ref[:, j * RET_DK + half: (j + 1) * RET_DK] = _b(d2 * c - d1 * s)

    nar = pl.BlockSpec((tr, RET_QK), lambda i: (i, 0))
    wide = pl.BlockSpec((tr, 2 * RET_QK), lambda i: (i, 0))
    tab = pl.BlockSpec((tr, half), lambda i: (i, 0))
    return pl.pallas_call(
        body, name=name, grid=(l // tr,), in_specs=[nar, nar, tab, tab, pl.BlockSpec(memory_space=pl.ANY)],
        out_specs=wide, out_shape=jax.ShapeDtypeStruct(dproj.shape, dproj.dtype), input_output_aliases={4: 0},
        compiler_params=_params(("parallel",), 32),
    )(dq, dk, cos, sin, dproj)


RET_BLOCK_CHUNKS = 3


def _ret_block(l):
    nch = l // CHUNK
    return RET_BLOCK_CHUNKS * CHUNK if nch % RET_BLOCK_CHUNKS == 0 else CHUNK


def _ret_decay(lg, rb):
    idx = lax.broadcasted_iota(jnp.int32, (rb, 1), 0).astype(f32)
    ri = lax.broadcasted_iota(jnp.int32, (rb, rb), 0)
    ci = lax.broadcasted_iota(jnp.int32, (rb, rb), 1)
    rel = (ri - ci).astype(f32)
    dmask = jnp.where(ri >= ci, jnp.exp(lg * jnp.maximum(rel, 0.0)), 0.0)
    xi = jnp.exp(lg * (idx + 1.0))
    zeta = jnp.exp(lg * (rb - 1.0 - idx))
    return dmask, xi, zeta


def _ret_scan_fwd(qk, proj, gn_w, lgs, gcs, name, comm=None):
    l = qk.shape[0]
    rb = _ret_block(l)
    nb = l // rb

    def body(lg_ref, gc_ref, q_ref, k_ref, v_ref, g_ref, nw_ref, o_ref, st_ref, y_ref, s_ref):
        @pl.when(pl.program_id(0) == 0)
        def _():
            s_ref[...] = jnp.zeros_like(s_ref)

        hs = range(RET_HEADS)
        dec = [_ret_decay(lg_ref[h], rb) for h in hs]
        q = [q_ref[:, h * RET_DK:(h + 1) * RET_DK] for h in hs]
        k = [k_ref[:, h * RET_DK:(h + 1) * RET_DK] for h in hs]
        vb = [_b(v_ref[:, h * RET_DV:(h + 1) * RET_DV]) for h in hs]
        s = [s_ref[h] for h in hs]
        sb = [_b(s[h]) for h in hs]
        scores = [_b(_nt(_b(q[h]), _b(k[h])) * dec[h][0]) for h in hs]
        inter = [_nn(_b(q[h] * dec[h][1]), sb[h]) for h in hs]
        kv = [_tn(_b(k[h] * dec[h][2]), vb[h]) for h in hs]
        nw = nw_ref[...]
        for h in hs:
            cols = slice(h * RET_DV, (h + 1) * RET_DV)
            st_ref[0, h] = sb[h]
            o = _nn(scores[h], vb[h]) + inter[h]
            o_ref[:, cols] = o
            r = lax.rsqrt(jnp.mean(o * o, axis=-1, keepdims=True) + RMS_EPS)
            y_ref[:, cols] = _b(o * r * nw * _silu(g_ref[:, cols]))
            s_ref[h] = gc_ref[h] * s[h] + kv[h]

    smem = pl.BlockSpec(memory_space=pltpu.SMEM)
    wide = pl.BlockSpec((rb, RET_V), lambda n: (n, 0))
    return _call(
        body, (lgs, gcs, qk, qk, proj, proj, gn_w), name=name, grid=(nb,),
        in_specs=[smem, smem,
                  pl.BlockSpec((rb, RET_QK), lambda n: (n, 0)),
                  pl.BlockSpec((rb, RET_QK), lambda n: (n, 1)),
                  pl.BlockSpec((rb, RET_V), lambda n: (n, 1)),
                  pl.BlockSpec((rb, RET_V), lambda n: (n, 2)),
                  pl.BlockSpec((1, RET_DV), lambda n: (0, 0))],
        out_specs=[wide, pl.BlockSpec((1, RET_HEADS, RET_DK, RET_DV), lambda n: (n, 0, 0, 0)), wide],
        out_shape=[jax.ShapeDtypeStruct((l, RET_V), f32),
                   jax.ShapeDtypeStruct((nb, RET_HEADS, RET_DK, RET_DV), bf16),
                   jax.ShapeDtypeStruct((l, RET_V), bf16)],
        scratch=[pltpu.VMEM((RET_HEADS, RET_DK, RET_DV), f32)], sem=("arbitrary",), vmem_mb=40, comm=comm)


def _ret_scan_bwd(qk, proj, states, do, dproj, lgs, gcs, name, comm=None):
    l = qk.shape[0]
    rb = _ret_block(l)
    nb = l // rb

    def body(lg_ref, gc_ref, q_ref, k_ref, v_ref, st_ref, do_ref, _, dq_ref, dk_ref, dv_ref, ds_ref):
        @pl.when(pl.program_id(0) == 0)
        def _():
            ds_ref[...] = jnp.zeros_like(ds_ref)

        hs = range(RET_HEADS)
        dec = [_ret_decay(lg_ref[h], rb) for h in hs]
        q = [q_ref[:, h * RET_DK:(h + 1) * RET_DK] for h in hs]
        k = [k_ref[:, h * RET_DK:(h + 1) * RET_DK] for h in hs]
        qb, kb = [_b(t) for t in q], [_b(t) for t in k]
        vb = [_b(v_ref[:, h * RET_DV:(h + 1) * RET_DV]) for h in hs]
        dob = [_b(do_ref[:, h * RET_DV:(h + 1) * RET_DV]) for h in hs]
        dsp = [ds_ref[h] for h in hs]
        dspb = [_b(t) for t in dsp]
        scores = [_b(_nt(qb[h], kb[h]) * dec[h][0]) for h in hs]
        dscores = [_b(_nt(dob[h], vb[h]) * dec[h][0]) for h in hs]
        for h in hs:
            dq_ref[:, h * RET_DK:(h + 1) * RET_DK] = _nn(dscores[h], kb[h]) + _nt(dob[h], st_ref[0, h]) * dec[h][1]
        for h in hs:
            dk_ref[:, h * RET_DK:(h + 1) * RET_DK] = _tn(dscores[h], qb[h]) + _nt(vb[h], dspb[h]) * dec[h][2]
        for h in hs:
            dv_ref[:, h * RET_DV:(h + 1) * RET_DV] = _b(_tn(scores[h], dob[h]) + _nn(_b(k[h] * dec[h][2]), dspb[h]))
        for h in hs:
            ds_ref[h] = gc_ref[h] * dsp[h] + _tn(_b(q[h] * dec[h][1]), dob[h])

    smem = pl.BlockSpec(memory_space=pltpu.SMEM)
    rev = lambda n: nb - 1 - n
    return _call(
        body, (lgs, gcs, qk, qk, proj, states, do, dproj), name=name, grid=(nb,),
        in_specs=[smem, smem,
                  pl.BlockSpec((rb, RET_QK), lambda n: (rev(n), 0)),
                  pl.BlockSpec((rb, RET_QK), lambda n: (rev(n), 1)),
                  pl.BlockSpec((rb, RET_V), lambda n: (rev(n), 1)),
                  pl.BlockSpec((1, RET_HEADS, RET_DK, RET_DV), lambda n: (rev(n), 0, 0, 0)),
                  pl.BlockSpec((rb, RET_V), lambda n: (rev(n), 0)),
                  pl.BlockSpec(memory_space=pl.ANY)],
        out_specs=[pl.BlockSpec((rb, RET_QK), lambda n: (rev(n), 0)),
                   pl.BlockSpec((rb, RET_QK), lambda n: (rev(n), 0)),
                   pl.BlockSpec((rb, RET_V), lambda n: (rev(n), 1))],
        out_shape=[jax.ShapeDtypeStruct((l, RET_QK), f32), jax.ShapeDtypeStruct((l, RET_QK), f32),
                   jax.ShapeDtypeStruct(dproj.shape, dproj.dtype)],
        scratch=[pltpu.VMEM((RET_HEADS, RET_DK, RET_DV), f32)], sem=("arbitrary",), vmem_mb=40, comm=comm,
        aliases={7: 2})


CONV_BLK = 512
CONV_Q_BLKS = DN_QK // CONV_BLK
HALO = 8


def _conv_tile(l):
    return _tile(l, 3072)


def _slab_rows(r):
    return pl.ds(pl.multiple_of(r * HALO, HALO), HALO)


def _conv_slab(x_ref, p_ref, r, i, tr):
    cur = x_ref[_slab_rows(r), :]
    prev = jnp.where(r > 0, x_ref[_slab_rows(jnp.maximum(r - 1, 0)), :], p_ref[...])
    row0 = i * tr + r * HALO
    cur = jnp.where(row0 >= PAD, cur, 0.0)
    prev = jnp.where(row0 - HALO >= PAD, prev, 0.0)
    lrow = lax.broadcasted_iota(jnp.int32, (HALO, 1), 0)
    shifted = [jnp.where(lrow < s, pltpu.roll(prev, s, 0), pltpu.roll(cur, s, 0)) for s in range(1, CONV_K)]
    return [cur] + shifted


def _conv_of(xs, w):
    acc = xs[0] * w[CONV_K - 1:CONV_K, :]
    for s in range(1, CONV_K):
        acc = acc + xs[s] * w[CONV_K - 1 - s:CONV_K - s, :]
    return acc


def _slab_loop(n_slabs, fn, init=None):
    return lax.fori_loop(0, n_slabs, fn, init, unroll=8)


def _dn_conv_fwd(proj, conv_w, name, comm=None):
    l = proj.shape[0]
    tr = _conv_tile(l)
    nblk = DN_CONV_CH // CONV_BLK
    heads = CONV_BLK // DN_DK

    def body(x_ref, p_ref, w_ref, o_ref):
        i, j = pl.program_id(0), pl.program_id(1)
        w = w_ref[...]

        def act(r):
            return _silu(_conv_of(_conv_slab(x_ref, p_ref, r, i, tr), w))

        def normed(scale):
            def slab(r, carry):
                a = act(r)
                outs = []
                for h in range(heads):
                    ah = a[:, h * DN_DK:(h + 1) * DN_DK]
                    outs.append(ah * (lax.rsqrt(jnp.sum(ah * ah, axis=-1, keepdims=True) + RMS_EPS) * scale))
                o_ref[_slab_rows(r), :] = jnp.concatenate(outs, axis=1)
                return carry
            return slab

        def plain(r, carry):
            o_ref[_slab_rows(r), :] = act(r)
            return carry

        @pl.when(j < CONV_Q_BLKS)
        def _():
            _slab_loop(tr // HALO, normed(DN_DK ** -0.5))

        @pl.when(jnp.logical_and(j >= CONV_Q_BLKS, j < 2 * CONV_Q_BLKS))
        def _():
            _slab_loop(tr // HALO, normed(1.0))

        @pl.when(j >= 2 * CONV_Q_BLKS)
        def _():
            _slab_loop(tr // HALO, plain)

    hb = tr // HALO
    return _call(
        body, (proj, proj, conv_w), name=name, grid=(l // tr, nblk),
        in_specs=[pl.BlockSpec((tr, CONV_BLK), lambda i, j: (i, j)),
                  pl.BlockSpec((HALO, CONV_BLK), lambda i, j: (jnp.maximum(i * hb - 1, 0), j)),
                  pl.BlockSpec((CONV_K, CONV_BLK), lambda i, j: (0, j))],
        out_specs=[pl.BlockSpec((tr, CONV_BLK), lambda i, j: (i, j))],
        out_shape=[jax.ShapeDtypeStruct((l, DN_CONV_CH), f32)],
        scratch=[], sem=("parallel", "parallel"), vmem_mb=32, comm=comm)


def _dn_conv_bwd_a(proj, conv_w, dqkv, name, comm=None):
    l = proj.shape[0]
    tr = _conv_tile(l)
    nblk = DN_CONV_CH // CONV_BLK
    heads = CONV_BLK // DN_DK

    def body(x_ref, p_ref, w_ref, d_ref, dc_ref, dw_ref, acc_ref):
        j, i = pl.program_id(0), pl.program_id(1)
        w = w_ref[...]
        acc_ref[...] = jnp.zeros_like(acc_ref)

        def run(l2_scale):
            _slab_loop(tr // HALO, slab_of(l2_scale))

        def slab_of(l2_scale):
            def slab(r, carry):
                xs = _conv_slab(x_ref, p_ref, r, i, tr)
                c = _conv_of(xs, w)
                a = _silu(c)
                dy = d_ref[_slab_rows(r), :]
                if l2_scale is None:
                    da = dy
                else:
                    parts = []
                    for h in range(heads):
                        sl = slice(h * DN_DK, (h + 1) * DN_DK)
                        ah, dyh = a[:, sl], dy[:, sl]
                        rn = lax.rsqrt(jnp.sum(ah * ah, axis=-1, keepdims=True) + RMS_EPS)
                        yh = ah * rn
                        parts.append((rn * l2_scale) * (dyh - yh * jnp.sum(dyh * yh, axis=-1, keepdims=True)))
                    da = jnp.concatenate(parts, axis=1)
                dc = da * _dsilu(c)
                dc_ref[_slab_rows(r), :] = dc
                for k in range(CONV_K):
                    acc_ref[k] += dc * xs[CONV_K - 1 - k]
                return carry
            return slab

        @pl.when(j < CONV_Q_BLKS)
        def _():
            run(DN_DK ** -0.5)

        @pl.when(jnp.logical_and(j >= CONV_Q_BLKS, j < 2 * CONV_Q_BLKS))
        def _():
            run(1.0)

        @pl.when(j >= 2 * CONV_Q_BLKS)
        def _():
            run(None)

        ksel = lax.broadcasted_iota(jnp.int32, (CONV_K, 1), 0)
        dw = jnp.zeros((CONV_K, CONV_BLK), f32)
        for k in range(CONV_K):
            dw = dw + jnp.where(ksel == k, jnp.sum(acc_ref[k], axis=0, keepdims=True), 0.0)

        @pl.when(i == 0)
        def _():
            dw_ref[...] = dw

        @pl.when(i > 0)
        def _():
            dw_ref[...] += dw

    hb = tr // HALO
    blk = pl.BlockSpec((tr, CONV_BLK), lambda j, i: (i, j))
    return _call(
        body, (proj, proj, conv_w, dqkv), name=name, grid=(nblk, l // tr),
        in_specs=[blk, pl.BlockSpec((HALO, CONV_BLK), lambda j, i: (jnp.maximum(i * hb - 1, 0), j)),
                  pl.BlockSpec((CONV_K, CONV_BLK), lambda j, i: (0, j)), blk],
        out_specs=[blk, pl.BlockSpec((CONV_K, CONV_BLK), lambda j, i: (0, j))],
        out_shape=[jax.ShapeDtypeStruct((l, DN_CONV_CH), f32), jax.ShapeDtypeStruct((CONV_K, DN_CONV_CH), f32)],
        scratch=[pltpu.VMEM((CONV_K, HALO, CONV_BLK), f32)], sem=("parallel", "arbitrary"), vmem_mb=48, comm=comm)


def _dn_conv_bwd_b(dc, conv_w, dproj, name):
    l = dc.shape[0]
    tr = _conv_tile(l)
    nblk = DN_CONV_CH // CONV_BLK
    nrow = l // tr

    n_slabs = tr // HALO
    pair = 2 * HALO

    def body(d_ref, n_ref, w_ref, _, o_ref):
        i = pl.program_id(0)
        w = w_ref[...]
        nxt_tile = jnp.where(i < nrow - 1, n_ref[...], 0.0)
        lrow = lax.broadcasted_iota(jnp.int32, (HALO, 1), 0)

        def one(r):
            cur = d_ref[_slab_rows(r), :]
            nxt = jnp.where(r < n_slabs - 1, d_ref[_slab_rows(jnp.minimum(r + 1, n_slabs - 1)), :], nxt_tile)
            acc = cur * w[CONV_K - 1:CONV_K, :]
            for s in range(1, CONV_K):
                up = jnp.where(lrow >= HALO - s, pltpu.roll(nxt, HALO - s, 0), pltpu.roll(cur, HALO - s, 0))
                acc = acc + up * w[CONV_K - 1 - s:CONV_K - s, :]
            return jnp.where(i * tr + r * HALO >= PAD, acc, 0.0)

        def two(q, carry):
            rows = pl.ds(pl.multiple_of(q * pair, pair), pair)
            o_ref[rows, :] = _b(jnp.concatenate([one(2 * q), one(2 * q + 1)], axis=0))
            return carry

        lax.fori_loop(0, n_slabs // 2, two, None, unroll=4)

    hb = tr // HALO
    nh = l // HALO
    return pl.pallas_call(
        body, name=name, grid=(nrow, nblk),
        in_specs=[pl.BlockSpec((tr, CONV_BLK), lambda i, j: (i, j)),
                  pl.BlockSpec((HALO, CONV_BLK), lambda i, j: (jnp.minimum((i + 1) * hb, nh - 1), j)),
                  pl.BlockSpec((CONV_K, CONV_BLK), lambda i, j: (0, j)),
                  pl.BlockSpec(memory_space=pl.ANY)],
        out_specs=pl.BlockSpec((tr, CONV_BLK), lambda i, j: (i, j)),
        out_shape=jax.ShapeDtypeStruct(dproj.shape, dproj.dtype), input_output_aliases={3: 0},
        compiler_params=_params(("parallel", "parallel"), 32),
    )(dc, dc, conv_w, dproj)


BA_W = LANES


def _dn_gates(ba_ref, al_ref, dt_ref, n):
    rows = n * CHUNK + lax.broadcasted_iota(jnp.int32, (CHUNK, 1), 0)
    vm = (rows >= PAD).astype(f32)
    bin_ = ba_ref[:, 0:DN_HEADS]
    z = ba_ref[:, DN_HEADS:2 * DN_HEADS] + dt_ref[...]
    sp = jnp.maximum(z, 0.0) + jnp.log1p(jnp.exp(-jnp.abs(z)))
    ea = jnp.exp(al_ref[...])
    beta = _sigmoid(bin_) * vm
    g = -ea * sp * vm
    return vm, bin_, z, ea, beta, g


def _tri():
    ri = lax.broadcasted_iota(jnp.int32, (CHUNK, CHUNK), 0)
    ci = lax.broadcasted_iota(jnp.int32, (CHUNK, CHUNK), 1)
    return ri, ci


def _split(a):
    hi = _b(a)
    return hi, _b(a - hi.astype(f32))


def _mm3(a, b, dot=_nn):
    (ah, al), (bh, bl) = _split(a), _split(b)
    return dot(ah, bh) + (dot(ah, bl) + dot(al, bh))


def _cumsum_rows(tri, g):
    tb = _b(tri)
    g1 = _b(g)
    r1 = g - g1.astype(f32)
    g2 = _b(r1)
    g3 = _b(r1 - g2.astype(f32))
    return _nn(tb, g1) + (_nn(tb, g2) + _nn(tb, g3))


DN_SCAN_CHUNKS = 3


def _scan_chunks(nch):
    return DN_SCAN_CHUNKS if nch % DN_SCAN_CHUNKS == 0 else 1


def _dn_prep(qkv, ba, a_log, dt_bias, name, comm=None):
    l = qkv.shape[0]
    nch = l // CHUNK
    heads = range(DN_HEADS)

    cb = _scan_chunks(nch)
    items = [(c, h) for c in range(cb) for h in heads]

    def body(q_ref, k_ref, v_ref, ba_ref, al_ref, dt_ref, t_ref, u_ref, wq_ref, pk_ref, eg_ref, kpt_ref, qwt_ref):
        n0 = pl.program_id(0) * cb
        ri, ci = _tri()
        incl, strict = ri >= ci, ri > ci
        eye = (ri == ci).astype(f32)
        rows = [slice(c * CHUNK, (c + 1) * CHUNK) for c in range(cb)]
        gam, gam_t, beta = [], [], []
        for c in range(cb):
            _, _, _, _, beta_c, g_c = _dn_gates(ba_ref[rows[c], :], al_ref, dt_ref, n0 + c)
            gam.append(_cumsum_rows(incl.astype(f32), g_c))
            gam_t.append(gam[c].T)
            beta.append(beta_c)
        gc = {(c, h): gam[c][:, h:h + 1] for c, h in items}
        bh = {(c, h): beta[c][:, h:h + 1] for c, h in items}
        kh = {(c, h): k_ref[rows[c], h * DN_DK:(h + 1) * DN_DK] for c, h in items}
        kb = {i: _b(kh[i]) for i in items}
        decay = {(c, h): jnp.exp(jnp.where(incl, gc[c, h] - gam_t[c][h:h + 1, :], -jnp.inf)) for c, h in items}
        a = {i: jnp.where(strict, bh[i] * _nt(kb[i], kb[i]) * decay[i], 0.0) for i in items}
        t = {i: eye - a[i] for i in items}
        p = a
        for level in range(int(math.log2(CHUNK)) - 1):
            mm = _mm3 if level < 2 else (lambda x, y: _nn(_b(x), _b(y)))
            p = {i: mm(p[i], p[i]) for i in items}
            t = {i: t[i] + mm(t[i], p[i]) for i in items}
        eg = {i: jnp.exp(gc[i]) for i in items}
        for c, h in items:
            i = (c, h)
            t_ref[c, h] = t[i]
            u_ref[rows[c], h * DN_DV:(h + 1) * DN_DV] = _mm3(t[i], v_ref[rows[c], h * DN_DV:(h + 1) * DN_DV] * bh[i])
            w = _mm3(t[i], kh[i] * (bh[i] * eg[i]))
            wq_ref[c, h, 0:CHUNK, :] = _b(w)
            qwt_ref[c, h, DN_DK:2 * DN_DK, :] = _b(w.T)
        for c, h in items:
            i = (c, h)
            qh = q_ref[rows[c], h * DN_DK:(h + 1) * DN_DK]
            gl = gc[i][CHUNK - 1:CHUNK, :]
            qe = qh * eg[i]
            ke = kh[i] * jnp.exp(gl - gc[i])
            pmat = _nt(_b(qh), kb[i]) * decay[i]
            wq_ref[c, h, CHUNK:2 * CHUNK, :] = _b(qe)
            qwt_ref[c, h, 0:DN_DK, :] = _b(qe.T)
            pk_ref[c, h, 0:CHUNK, :] = _b(pmat)
            pk_ref[c, h, CHUNK:CHUNK + DN_DK, :] = _b(ke.T)
            kpt_ref[c, h, :, 0:DN_DK] = _b(ke)
            kpt_ref[c, h, :, DN_DK:DN_DK + CHUNK] = _b(pmat.T)
            eg_ref[c, h] = jnp.broadcast_to(jnp.exp(gl), (8, LANES))

    vec = pl.BlockSpec((1, DN_HEADS), lambda n: (0, 0))
    return _call(
        body, (qkv, qkv, qkv, ba, a_log, dt_bias), name=name, grid=(nch // cb,),
        in_specs=[pl.BlockSpec((cb * CHUNK, DN_QK), lambda n: (n, 0)),
                  pl.BlockSpec((cb * CHUNK, DN_QK), lambda n: (n, 1)),
                  pl.BlockSpec((cb * CHUNK, DN_V), lambda n: (n, 1)),
                  pl.BlockSpec((cb * CHUNK, BA_W), lambda n: (n, 0)), vec, vec],
        out_specs=[pl.BlockSpec((cb, DN_HEADS, CHUNK, CHUNK), lambda n: (n, 0, 0, 0)),
                   pl.BlockSpec((cb * CHUNK, DN_V), lambda n: (n, 0)),
                   pl.BlockSpec((cb, DN_HEADS, 2 * CHUNK, DN_DK), lambda n: (n, 0, 0, 0)),
                   pl.BlockSpec((cb, DN_HEADS, CHUNK + DN_DK, CHUNK), lambda n: (n, 0, 0, 0)),
                   pl.BlockSpec((cb, DN_HEADS, 8, LANES), lambda n: (n, 0, 0, 0)),
                   pl.BlockSpec((cb, DN_HEADS, CHUNK, DN_DK + CHUNK), lambda n: (n, 0, 0, 0)),
                   pl.BlockSpec((cb, DN_HEADS, 2 * DN_DK, CHUNK), lambda n: (n, 0, 0, 0))],
        out_shape=[jax.ShapeDtypeStruct((nch, DN_HEADS, CHUNK, CHUNK), f32),
                   jax.ShapeDtypeStruct((l, DN_V), f32),
                   jax.ShapeDtypeStruct((nch, DN_HEADS, 2 * CHUNK, DN_DK), bf16),
                   jax.ShapeDtypeStruct((nch, DN_HEADS, CHUNK + DN_DK, CHUNK), bf16),
                   jax.ShapeDtypeStruct((nch, DN_HEADS, 8, LANES), f32),
                   jax.ShapeDtypeStruct((nch, DN_HEADS, CHUNK, DN_DK + CHUNK), bf16),
                   jax.ShapeDtypeStruct((nch, DN_HEADS, 2 * DN_DK, CHUNK), bf16)],
        sem=("parallel",), vmem_mb=40, comm=comm)


def _dn_scan_fwd(u, wq, pk, egl, name):
    l = u.shape[0]
    nch = l // CHUNK
    cs = _scan_chunks(nch)

    def body(u_ref, wq_ref, pk_ref, eg_ref, o_ref, st_ref, vn_ref, s_ref):
        @pl.when(pl.program_id(0) == 0)
        def _():
            s_ref[...] = jnp.zeros_like(s_ref)

        hs = range(DN_HEADS)
        cols = [slice(h * DN_DV, (h + 1) * DN_DV) for h in hs]
        s = [s_ref[h] for h in hs]
        for c in range(cs):
            rows = slice(c * CHUNK, (c + 1) * CHUNK)
            sb = [_b(s[h]) for h in hs]
            x = [_nn(wq_ref[c, h], sb[h]) for h in hs]
            vnb = [_b(u_ref[rows, cols[h]] - x[h][0:CHUNK]) for h in hs]
            y = [_nn(pk_ref[c, h], vnb[h]) for h in hs]
            for h in hs:
                st_ref[c, h] = sb[h]
                vn_ref[rows, cols[h]] = vnb[h]
                o_ref[rows, cols[h]] = x[h][CHUNK:2 * CHUNK] + y[h][0:CHUNK]
            s = [eg_ref[c, h][0:1, 0:1] * s[h] + y[h][CHUNK:CHUNK + DN_DK] for h in hs]
        for h in hs:
            s_ref[h] = s[h]

    return pl.pallas_call(
        body, name=name, grid=(nch // cs,),
        in_specs=[pl.BlockSpec((cs * CHUNK, DN_V), lambda n: (n, 0)),
                  pl.BlockSpec((cs, DN_HEADS, 2 * CHUNK, DN_DK), lambda n: (n, 0, 0, 0)),
                  pl.BlockSpec((cs, DN_HEADS, CHUNK + DN_DK, CHUNK), lambda n: (n, 0, 0, 0)),
                  pl.BlockSpec((cs, DN_HEADS, 8, LANES), lambda n: (n, 0, 0, 0))],
        out_specs=[pl.BlockSpec((cs * CHUNK, DN_V), lambda n: (n, 0)),
                   pl.BlockSpec((cs, DN_HEADS, DN_DK, DN_DV), lambda n: (n, 0, 0, 0)),
                   pl.BlockSpec((cs * CHUNK, DN_V), lambda n: (n, 0))],
        out_shape=[jax.ShapeDtypeStruct((l, DN_V), f32),
                   jax.ShapeDtypeStruct((nch, DN_HEADS, DN_DK, DN_DV), bf16),
                   jax.ShapeDtypeStruct((l, DN_V), bf16)],
        scratch_shapes=[pltpu.VMEM((DN_HEADS, DN_DK, DN_DV), f32)],
        compiler_params=_params(("arbitrary",), 40),
    )(u, wq, pk, egl)


def _dn_scan_bwd(do, kpt, qwt, egl, name):
    l = do.shape[0]
    nch = l // CHUNK
    cs = _scan_chunks(nch)
    nblk = nch // cs

    def body(do_ref, kpt_ref, qwt_ref, eg_ref, dvn_ref, dsp_ref, ds_ref):
        @pl.when(pl.program_id(0) == 0)
        def _():
            ds_ref[...] = jnp.zeros_like(ds_ref)

        hs = range(DN_HEADS)
        cols = [slice(h * DN_DV, (h + 1) * DN_DV) for h in hs]
        ds = [ds_ref[h] for h in hs]
        for c in reversed(range(cs)):
            rows = slice(c * CHUNK, (c + 1) * CHUNK)
            dspb = [_b(ds[h]) for h in hs]
            dob = [_b(do_ref[rows, cols[h]]) for h in hs]
            dvn = [_nn(kpt_ref[c, h][:, 0:DN_DK], dspb[h]) + _nn(kpt_ref[c, h][:, DN_DK:DN_DK + CHUNK], dob[h])
                   for h in hs]
            for h in hs:
                dsp_ref[c, h] = dspb[h]
                dvn_ref[rows, cols[h]] = dvn[h]
            ds = [eg_ref[c, h][0:1, 0:1] * ds[h] + _nn(qwt_ref[c, h][0:DN_DK], dob[h])
                  - _nn(qwt_ref[c, h][DN_DK:2 * DN_DK], _b(dvn[h])) for h in hs]
        for h in hs:
            ds_ref[h] = ds[h]

    rev = lambda s: nblk - 1 - s
    return pl.pallas_call(
        body, name=name, grid=(nblk,),
        in_specs=[pl.BlockSpec((cs * CHUNK, DN_V), lambda s: (rev(s), 0)),
                  pl.BlockSpec((cs, DN_HEADS, CHUNK, DN_DK + CHUNK), lambda s: (rev(s), 0, 0, 0)),
                  pl.BlockSpec((cs, DN_HEADS, 2 * DN_DK, CHUNK), lambda s: (rev(s), 0, 0, 0)),
                  pl.BlockSpec((cs, DN_HEADS, 8, LANES), lambda s: (rev(s), 0, 0, 0))],
        out_specs=[pl.BlockSpec((cs * CHUNK, DN_V), lambda s: (rev(s), 0)),
                   pl.BlockSpec((cs, DN_HEADS, DN_DK, DN_DV), lambda s: (rev(s), 0, 0, 0))],
        out_shape=[jax.ShapeDtypeStruct((l, DN_V), f32),
                   jax.ShapeDtypeStruct((nch, DN_HEADS, DN_DK, DN_DV), bf16)],
        scratch_shapes=[pltpu.VMEM((DN_HEADS, DN_DK, DN_DV), f32)],
        compiler_params=_params(("arbitrary",), 40),
    )(do, kpt, qwt, egl)


def _dn_post_bwd(qkv, ba, a_log, dt_bias, states, dsp_all, tinv_all, u_all, wq, vn_all, do, dvn_all, name):
    l = qkv.shape[0]
    nch = l // CHUNK
    cb = _scan_chunks(nch)
    items = [(c, h) for c in range(cb) for h in range(DN_HEADS)]

    def body(q_ref, k_ref, v_ref, ba_ref, al_ref, dt_ref, st_ref, dsp_ref, t_ref, u_ref, wq_ref, vn_ref, do_ref,
             dvn_ref, dqkv_ref, dba_ref, dal_ref, ddt_ref):
        step = pl.program_id(0)
        ri, ci = _tri()
        incl, strict = ri >= ci, ri > ci
        lane8 = lax.broadcasted_iota(jnp.int32, (1, DN_HEADS), 1)
        sub8 = lax.broadcasted_iota(jnp.int32, (DN_HEADS, 1), 0)
        last = (lax.broadcasted_iota(jnp.int32, (CHUNK, 1), 0) == CHUNK - 1).astype(f32)
        rsum = lambda t: jnp.sum(t, axis=-1, keepdims=True)
        rows = [slice(c * CHUNK, (c + 1) * CHUNK) for c in range(cb)]
        gates = [_dn_gates(ba_ref[rows[c], :], al_ref, dt_ref, step * cb + c) for c in range(cb)]
        gam = [_cumsum_rows(incl.astype(f32), gates[c][5]) for c in range(cb)]
        gam_t = [gam[c].T for c in range(cb)]
        each = lambda fn: {(c, h): fn(c, h) for c, h in items}
        dk_cols = lambda h: slice(h * DN_DK, (h + 1) * DN_DK)
        dv_cols = lambda h: slice(h * DN_DV, (h + 1) * DN_DV)
        gc = each(lambda c, h: gam[c][:, h:h + 1])
        bh = each(lambda c, h: gates[c][4][:, h:h + 1])
        qh = each(lambda c, h: q_ref[rows[c], dk_cols(h)])
        kh = each(lambda c, h: k_ref[rows[c], dk_cols(h)])
        doh = each(lambda c, h: _b(do_ref[rows[c], dv_cols(h)]))
        sb = each(lambda c, h: st_ref[c, h])
        dspb = each(lambda c, h: dsp_ref[c, h])
        vnb = each(lambda c, h: vn_ref[rows[c], dv_cols(h)])
        dvn = each(lambda c, h: dvn_ref[rows[c], dv_cols(h)])
        wb = each(lambda c, h: wq_ref[c, h, 0:CHUNK, :])
        decay = each(lambda c, h: jnp.exp(jnp.where(incl, gc[c, h] - gam_t[c][h:h + 1, :], -jnp.inf)))
        qb, kb = each(lambda c, h: _b(qh[c, h])), each(lambda c, h: _b(kh[c, h]))
        eg = each(lambda c, h: jnp.exp(gc[c, h]))
        gl = each(lambda c, h: gc[c, h][CHUNK - 1:CHUNK, :])
        ekd = each(lambda c, h: jnp.exp(gl[c, h] - gc[c, h]))
        dvnb = each(lambda c, h: _b(dvn[c, h]))
        kk = each(lambda c, h: _nt(kb[c, h], kb[c, h]))
        p = each(lambda c, h: _nt(qb[c, h], kb[c, h]) * decay[c, h])
        dpraw = each(lambda c, h: _nt(doh[c, h], vnb[c, h]))
        dqe = each(lambda c, h: _nt(doh[c, h], sb[c, h]))
        dke = each(lambda c, h: _nt(vnb[c, h], dspb[c, h]))
        dw = each(lambda c, h: -_nt(dvnb[c, h], sb[c, h]))
        dru = each(lambda c, h: _mm3(t_ref[c, h], dvn[c, h], _tn))
        drw = each(lambda c, h: _mm3(t_ref[c, h], dw[c, h], _tn))
        dqk = each(lambda c, h: _b(dpraw[c, h] * decay[c, h]))
        for c, h in items:
            i = (c, h)
            dqkv_ref[rows[c], dk_cols(h)] = _nn(dqk[i], kb[i]) + dqe[i] * eg[i]
            dqkv_ref[rows[c], 2 * DN_QK + h * DN_DV:2 * DN_QK + (h + 1) * DN_DV] = bh[i] * dru[i]
        da = each(lambda c, h: jnp.where(strict, -(_nt(_b(dru[c, h]), _b(u_ref[rows[c], dv_cols(h)]))
                                                   + _nt(_b(drw[c, h]), wb[c, h])), 0.0))
        dkk = each(lambda c, h: _b(da[c, h] * bh[c, h] * decay[c, h]))
        for c, h in items:
            i = (c, h)
            dqkv_ref[rows[c], DN_QK + h * DN_DK:DN_QK + (h + 1) * DN_DK] = (
                _tn(dqk[i], qb[i]) + dke[i] * ekd[i] + (bh[i] * eg[i]) * drw[i]
                + _nn(dkk[i], kb[i]) + _tn(dkk[i], kb[i]))
        dal = jnp.zeros((1, DN_HEADS), f32)
        ddt = jnp.zeros((1, DN_HEADS), f32)
        dba_ref[...] = jnp.zeros_like(dba_ref)
        for c in range(cb):
            vm, bin_, z, ea, _, g = gates[c]
            dbeta = jnp.zeros((CHUNK, DN_HEADS), f32)
            dgam = jnp.zeros((CHUNK, DN_HEADS), f32)
            dgam_neg_t = jnp.zeros((DN_HEADS, CHUNK), f32)
            for h in range(DN_HEADS):
                i = (c, h)
                keg = kh[i] * eg[i]
                ke = kh[i] * ekd[i]
                rw = rsum(drw[i] * keg)
                rke = rsum(dke[i] * ke)
                db_h = rsum(dru[i] * v_ref[rows[c], dv_cols(h)]) + rw + rsum(da[i] * kk[i] * decay[i])
                mm = da[i] * (bh[i] * kk[i] * decay[i]) + dpraw[i] * p[i]
                dgl = (jnp.sum(rke, axis=0, keepdims=True)
                       + jnp.exp(gl[i]) * jnp.sum(rsum(dspb[i].astype(f32) * sb[i].astype(f32)), axis=0,
                                                  keepdims=True))
                dg_h = rsum(mm) + rw * bh[i] + rsum(dqe[i] * (qh[i] * eg[i])) - rke + last * dgl
                dbeta = dbeta + jnp.where(lane8 == h, db_h, 0.0)
                dgam = dgam + jnp.where(lane8 == h, dg_h, 0.0)
                dgam_neg_t = dgam_neg_t + jnp.where(sub8 == h, jnp.sum(mm, axis=0, keepdims=True), 0.0)
            dgam = dgam - dgam_neg_t.T
            dg = _cumsum_rows((ri <= ci).astype(f32), dgam)
            sg = _sigmoid(bin_)
            dain = dg * (-ea) * vm * _sigmoid(z)
            dba_ref[rows[c], 0:DN_HEADS] = dbeta * vm * sg * (1.0 - sg)
            dba_ref[rows[c], DN_HEADS:2 * DN_HEADS] = dain
            dal = dal + jnp.sum(dg * g, axis=0, keepdims=True)
            ddt = ddt + jnp.sum(dain, axis=0, keepdims=True)

        @pl.when(step == 0)
        def _():
            dal_ref[...] = dal
            ddt_ref[...] = ddt

        @pl.when(step > 0)
        def _():
            dal_ref[...] += dal
            ddt_ref[...] += ddt

    vec = pl.BlockSpec((1, DN_HEADS), lambda s: (0, 0))
    qs = pl.BlockSpec((cb * CHUNK, DN_QK), lambda s: (s, 0))
    ks = pl.BlockSpec((cb * CHUNK, DN_QK), lambda s: (s, 1))
    vs = pl.BlockSpec((cb * CHUNK, DN_V), lambda s: (s, 1))
    v0 = pl.BlockSpec((cb * CHUNK, DN_V), lambda s: (s, 0))
    st = pl.BlockSpec((cb, DN_HEADS, DN_DK, DN_DV), lambda s: (s, 0, 0, 0))
    return pl.pallas_call(
        body, name=name, grid=(nch // cb,),
        in_specs=[qs, ks, vs, pl.BlockSpec((cb * CHUNK, BA_W), lambda s: (s, 0)), vec, vec, st, st,
                  pl.BlockSpec((cb, DN_HEADS, CHUNK, CHUNK), lambda s: (s, 0, 0, 0)),
                  v0, pl.BlockSpec((cb, DN_HEADS, 2 * CHUNK, DN_DK), lambda s: (s, 0, 0, 0)), v0, v0, v0],
        out_specs=[pl.BlockSpec((cb * CHUNK, DN_CONV_CH), lambda s: (s, 0)),
                   pl.BlockSpec((cb * CHUNK, BA_W), lambda s: (s, 0)), vec, vec],
        out_shape=[jax.ShapeDtypeStruct((l, DN_CONV_CH), f32), jax.ShapeDtypeStruct((l, BA_W), f32),
                   jax.ShapeDtypeStruct((1, DN_HEADS), f32), jax.ShapeDtypeStruct((1, DN_HEADS), f32)],
        compiler_params=_params(("arbitrary",), 48),
    )(qkv, qkv, qkv, ba, a_log, dt_bias, states, dsp_all, tinv_all, u_all, wq, vn_all, do, dvn_all)


def _ffn_fwd(h, hn, wg, wu, wd, tb, th, tag, plan, next_norm_w=None):
    fh, d = wd.shape
    a, b, s = plan.call(f"{tag}_gu", functools.partial(_ffn_gu, tm=th // 2, tn=fh // 2), hn, wg, wu, n_out=3)
    out = plan.matmul(f"{tag}_down", s, wd, mode="nn", tm=th // 2, tn=d, tk=fh, res=h, norm_w=next_norm_w)
    return out, (hn, a, b, s)


def _ffn_bwd(dh, dhb, h, nw, wg, wu, wd, saved, tb, th, tag, plan):
    hn, a, b, s = saved
    d = h.shape[1]
    fh = wd.shape[0]
    layer = tag[-1]
    gr = plan.grads
    da, db = _ffn_ds(dhb, wd, a, b, tm=th // 2, tn=fh // 2, name=f"{tag}_b_ds")
    gr["down" + layer] = _matmul(s, dhb, mode="tn", tm=fh // 2, tn=d, tk=th, out_dtype=bf16, name=f"{tag}_b_dwd")
    dh2, dh2b, dnw = plan.call(f"{tag}_b_dhn", functools.partial(_dhn_norm_bwd, tm=th // 2, tk=fh // 2),
                               [(da, wg), (db, wu)], h, nw, dh, n_out=3)
    resident = dict(mode="tn", tm=d, tn=256, tk=h.shape[0], resident_a=True, vmem_mb=56, out_dtype=bf16)
    gr["gate" + layer] = _matmul(hn, da, name=f"{tag}_b_dwg", **resident)
    gr["up" + layer] = _matmul(hn, db, name=f"{tag}_b_dwu", **resident)
    return dh2, dh2b, dnw


class _Plan:
    GATHERS = {"ret_proj": ("ret_out", "gate0", "up0"), "ret_scan": ("down0", "dn_in"),
               "dn_conv": ("dn_out", "gate1", "up1", "down1")}
    SCATTERS = {"dn_b_conv_a": ("down1", "gate1", "up1", "dn_out"), "ffn0_b_dhn": ("dn_in",),
                "ret_b_scan": ("gate0", "up0"), "ret_b_dwin": ("down0", "ret_out"), "ret_b_dhn": ("ret_in",)}

    def __init__(self, shards, wts):
        self.shards, self.wts, self.grads, self.parts = shards, wts, {}, {}

    def _exchange(self, stage):
        if self.shards is None:
            return None
        if stage in self.GATHERS:
            return _Exchange([self.shards[n] for n in self.GATHERS[stage]], True)
        if stage in self.SCATTERS:
            return _Exchange([self._dev_major(n) for n in self.SCATTERS[stage]], False)
        return None

    def _dev_major(self, name):
        g = self.grads
        if name[:-1] in ("gate", "up"):
            return _dev_major_cols(g[name], g[name].shape[1] // N_DEV)
        if name[:-1] == "down":
            dwd = g[name]
            return dwd.reshape(N_DEV, dwd.shape[0] // N_DEV, dwd.shape[1])
        if name in ("ret_out", "dn_out"):
            return g[name].reshape(N_DEV, g[name].shape[0] // N_DEV, g[name].shape[1])
        return _dev_major_cols(g[name], self.shards[name].shape[-1])

    def _landed(self, stage, outs):
        if stage in self.SCATTERS:
            self.parts.update(zip(self.SCATTERS[stage], outs))
            return
        w = self.wts
        cols = lambda t: t.transpose(1, 0, 2).reshape(t.shape[1], N_DEV * t.shape[2])
        rows = lambda t: t.reshape(N_DEV * t.shape[1], t.shape[2])
        for name, t in zip(self.GATHERS[stage], outs):
            if name in ("ret_out", "dn_out") or name.startswith("down"):
                w[name] = rows(t)
            elif name == "dn_in":
                full = cols(t)
                n_main = DN_CONV_CH + DN_V
                w["dn_main"] = full[:, :n_main]
                w["dn_ba"] = jnp.pad(full[:, n_main:], ((0, 0), (0, BA_W - (full.shape[1] - n_main))))
            else:
                w[name] = cols(t)

    def matmul(self, stage, a, b, **kw):
        comm = self._exchange(stage)
        if comm is None:
            return _matmul(a, b, name=stage, **kw)
        out, landed = _matmul(a, b, name=stage, comm=comm, **kw)
        self._landed(stage, landed)
        return out

    def call(self, stage, fn, *args, n_out):
        comm = self._exchange(stage)
        out = fn(*args, stage, comm=comm)
        if comm is not None:
            self._landed(stage, out[n_out:])
        return out[:n_out]


def _local_step(x2, target, wts, shards=None):
    plan = _Plan(shards, wts)
    s_len, d = x2.shape
    l = s_len + CHUNK
    tb = _tile(l, 3072)
    th = tb // 2 if (tb // 2) % 16 == 0 else tb
    half = RET_DK // 2
    inv_freq = (np.float32(ROPE_BASE) ** (-np.arange(half, dtype=np.float32) / np.float32(half))).astype(np.float32)
    ang = (np.arange(l) - PAD).astype(np.float32)[:, None] * inv_freq[None, :]
    cos, sin = jnp.asarray(np.cos(ang), f32), jnp.asarray(np.sin(ang), f32)
    lgs = jnp.log1p(-jnp.exp2(-5.0 - jnp.arange(RET_HEADS, dtype=f32)))
    gcs = jnp.exp(lgs * _ret_block(l))

    mixw, ffnw = wts["mix_norm"], wts["ffn_norm"]

    h0, hn0 = _embed_norm(x2, wts["meta"], mixw[0:1], "l0_norm")
    proj0 = plan.matmul("ret_proj", hn0, wts["ret_in"], mode="nn", tm=tb, tn=512, tk=d)
    (qk0,) = plan.call("ret_prep", _ret_prep, proj0, cos, sin, n_out=1)
    o0, st0, y0 = plan.call("ret_scan", _ret_scan_fwd, qk0, proj0, wts["ret_gn"], lgs, gcs, n_out=3)
    h1, hn1 = _matmul(y0, wts["ret_out"], mode="nn", tm=th // 2, tn=d, tk=RET_V, res=h0, norm_w=ffnw[0:1],
                      name="ret_out")
    (h2, hn2), ffn0 = _ffn_fwd(h1, hn1, wts["gate0"], wts["up0"], wts["down0"], tb, th, "ffn0", plan,
                               next_norm_w=mixw[1:2])

    proj1 = plan.matmul("dn_proj", hn2, wts["dn_main"], mode="nn", tm=tb, tn=512, tk=d)
    ba = _matmul(hn2, wts["dn_ba"], mode="nn", tm=tb, tn=BA_W, tk=d, name="dn_proj_ba")
    (qkv1,) = plan.call("dn_conv", _dn_conv_fwd, proj1, wts["conv_w"], n_out=1)
    tinv1, u1, wq1, pk1, egl1, kpt1, qwt1 = plan.call("dn_prep", _dn_prep, qkv1, ba, wts["a_log"], wts["dt_bias"],
                                                      n_out=7)
    o1, st1, vn1 = _dn_scan_fwd(u1, wq1, pk1, egl1, "dn_scan")
    y1 = _gnorm_fwd(o1, proj1, wts["dn_norm"], DN_HEADS, DN_DV, 2, "dn_gnorm")
    h3, hn3 = _matmul(y1, wts["dn_out"], mode="nn", tm=th // 2, tn=d, tk=DN_V, res=h2, norm_w=ffnw[1:2],
                      name="dn_out")
    h4, ffn1 = _ffn_fwd(h3, hn3, wts["gate1"], wts["up1"], wts["down1"], tb, th, "ffn1", plan)

    dh4, dh4b, dfinal, loss = _final_loss(h4, wts["final_norm"], target, "final_loss")
    gr = plan.grads
    dh3, dh3b, dffn1 = _ffn_bwd(dh4, dh4b, h3, ffnw[1:2], wts["gate1"], wts["up1"], wts["down1"], ffn1,
                                tb, th, "ffn1", plan)

    gr["dn_out"] = _matmul(y1, dh3b, mode="tn", tm=1024, tn=d, tk=tb, out_dtype=bf16, name="dn_b_dwout")
    do1, dproj1, ddn_norm = _dy_gnorm_bwd(dh3b, wts["dn_out"], o1, proj1, wts["dn_norm"], DN_DV, 2, "dn_b_gnorm",
                                          tm=th // 2, tn=1024)
    dvn1, dsp1 = _dn_scan_bwd(do1, kpt1, qwt1, egl1, "dn_b_scan")
    dqkv1, dba, dalog, ddt = _dn_post_bwd(qkv1, ba, wts["a_log"], wts["dt_bias"], st1, dsp1, tinv1, u1, wq1, vn1,
                                          do1, dvn1, "dn_b_post")
    dc1, dconv = plan.call("dn_b_conv_a", _dn_conv_bwd_a, proj1, wts["conv_w"], dqkv1, n_out=2)
    dproj1 = _dn_conv_bwd_b(dc1, wts["conv_w"], dproj1, "dn_b_conv_b")
    dbab = dba.astype(bf16)
    n_main = dproj1.shape[1]
    dhn2_ba = _matmul(dbab, wts["dn_ba"], mode="nt", tm=th, tn=d, tk=BA_W, name="dn_b_dhn_ba")
    dh2, dh2b, dmix1 = plan.call("dn_b_dhn", functools.partial(_dhn_norm_bwd, tm=th // 2, tk=n_main // 4,
                                                              init=dhn2_ba),
                                 [(dproj1, wts["dn_main"])], h2, mixw[1:2], dh3, n_out=3)
    dw_main = _matmul(hn2, dproj1, mode="tn", tm=d, tn=256, tk=l, resident_a=True, vmem_mb=56, out_dtype=bf16,
                      name="dn_b_dwin")
    dw_ba = _matmul(hn2, dbab, mode="tn", tm=d, tn=BA_W, tk=tb, out_dtype=bf16, name="dn_b_dwin_ba")
    gr["dn_in"] = jnp.concatenate([dw_main, dw_ba], axis=1)

    dh1, dh1b, dffn0 = _ffn_bwd(dh2, dh2b, h1, ffnw[0:1], wts["gate0"], wts["up0"], wts["down0"], ffn0,
                                tb, th, "ffn0", plan)

    gr["ret_out"] = _matmul(y0, dh1b, mode="tn", tm=1024, tn=d, tk=tb, out_dtype=bf16, name="ret_b_dwout")
    do0, dproj0, dret_gn = _dy_gnorm_bwd(dh1b, wts["ret_out"], o0, proj0, wts["ret_gn"], RET_DV, 2, "ret_b_gnorm",
                                         tm=th // 2, tn=1024)
    dq0, dk0, dproj0 = plan.call("ret_b_scan", _ret_scan_bwd, qk0, proj0, st0, do0, dproj0, lgs, gcs, n_out=3)
    dproj0 = _ret_prep_bwd(dq0, dk0, cos, sin, dproj0, "ret_b_prep")
    n_in = dproj0.shape[1]
    gr["ret_in"] = plan.matmul("ret_b_dwin", hn0, dproj0, mode="tn", tm=d, tn=256, tk=l, resident_a=True, vmem_mb=56,
                               out_dtype=bf16)
    dh0, _, dmix0 = plan.call("ret_b_dhn", functools.partial(_dhn_norm_bwd, tm=th // 2, tk=n_in // 4),
                              [(dproj0, wts["ret_in"])], h0, mixw[0:1], dh1, n_out=3)

    gr.update(meta=dh0[PAD:CHUNK], mix_norm=jnp.concatenate([dmix0, dmix1], axis=0),
              ffn_norm=jnp.concatenate([dffn0, dffn1], axis=0), ret_gn=dret_gn, conv_w=dconv, a_log=dalog,
              dt_bias=ddt, dn_norm=ddn_norm, final_norm=dfinal)
    return loss, dh0[CHUNK:], gr, plan


def _adamw_reduce(parts, w, m, v, name):
    _, r, c = parts.shape
    c_pad = -(-c // LANES) * LANES
    tr = _div_tile(r, max(8, (3 * MIB // 16) // c_pad // 8 * 8), 16)

    def body(p_ref, w_ref, m_ref, v_ref, g_ref, d_ref, nm_ref, nv_ref):
        g = p_ref[0].astype(f32)
        for s in range(1, N_DEV):
            g = g + p_ref[s].astype(f32)
        mm = ADAM_B1 * m_ref[...] + (1.0 - ADAM_B1) * g
        vv = ADAM_B2 * v_ref[...] + (1.0 - ADAM_B2) * (g * g)
        m_hat = mm / (1.0 - ADAM_B1 ** ADAM_STEP)
        v_hat = vv / (1.0 - ADAM_B2 ** ADAM_STEP)
        g_ref[...] = g
        d_ref[...] = -ADAM_LR * (m_hat / (jnp.sqrt(v_hat) + ADAM_EPS) + ADAM_WD * w_ref[...])
        nm_ref[...] = mm
        nv_ref[...] = vv

    blk = pl.BlockSpec((tr, c), lambda i: (i, 0))
    return pl.pallas_call(
        body, name=name, grid=(r // tr,),
        in_specs=[pl.BlockSpec((N_DEV, tr, c), lambda i: (0, i, 0)), blk, blk, blk], out_specs=[blk] * 4,
        out_shape=[jax.ShapeDtypeStruct((r, c), f32)] * 4,
        compiler_params=_params(("parallel",), 48),
    )(parts, w, m, v)


def _dev_major_cols(g, width):
    r = g.shape[0]
    return g[:, :N_DEV * width].reshape(r, N_DEV, width).transpose(1, 0, 2)


def kernel(x, meta_tokens, mix_norm_w, ffn_norm_w, ret_w_in, ret_gn_w, ret_w_out, dn_w_in, dn_conv_w, dn_a_log, dn_dt_bias, dn_norm_w, dn_w_out, ffn_w_gate, ffn_w_up, ffn_w_down, final_norm_w, loss_target, m_meta_tokens, m_mix_norm_w, m_ffn_norm_w, m_ret_w_in, m_ret_gn_w, m_ret_w_out, m_dn_w_in, m_dn_conv_w, m_dn_a_log, m_dn_dt_bias, m_dn_norm_w, m_dn_w_out, m_ffn_w_gate, m_ffn_w_up, m_ffn_w_down, m_final_norm_w, v_meta_tokens, v_mix_norm_w, v_ffn_norm_w, v_ret_w_in, v_ret_gn_w, v_ret_w_out, v_dn_w_in, v_dn_conv_w, v_dn_a_log, v_dn_dt_bias, v_dn_norm_w, v_dn_w_out, v_ffn_w_gate, v_ffn_w_up, v_ffn_w_down, v_final_norm_w):
    d = x.shape[-1]
    me = 4 * lax.axis_index("x") + 2 * lax.axis_index("y") + lax.axis_index("c")

    shards = dict(ret_in=ret_w_in[0].astype(bf16), ret_out=ret_w_out[0].astype(bf16),
                  dn_in=dn_w_in[0].astype(bf16), dn_out=dn_w_out[0].astype(bf16))
    for layer in (0, 1):
        shards[f"gate{layer}"] = ffn_w_gate[layer].astype(bf16)
        shards[f"up{layer}"] = ffn_w_up[layer].astype(bf16)
        shards[f"down{layer}"] = ffn_w_down[layer].astype(bf16)
    g_ret_in, g_meta, g_conv, g_dnn = _exchange([shards["ret_in"], meta_tokens, dn_conv_w[0], dn_norm_w], True,
                                                "gather_first")
    cols = lambda g: g.transpose(1, 0, 2).reshape(g.shape[1], N_DEV * g.shape[2])
    wts = dict(meta=cols(g_meta), mix_norm=mix_norm_w, ffn_norm=ffn_norm_w, ret_in=cols(g_ret_in), ret_gn=ret_gn_w,
               conv_w=cols(g_conv), a_log=dn_a_log, dt_bias=dn_dt_bias, dn_norm=cols(g_dnn),
               final_norm=final_norm_w.reshape(1, d))

    loss_part, grad_x, gr, plan = _local_step(x[0], loss_target[0], wts, shards)
    loss = lax.psum(loss_part[0, 0], AXES)

    pp = plan.parts
    both = lambda name: jnp.concatenate([pp[name + "0"], pp[name + "1"]], axis=1)
    big_parts = [pp["ret_in"], pp["ret_out"], pp["dn_in"], pp["dn_out"], both("gate"), both("up"), both("down")]
    big_names = ["ret_w_in", "ret_w_out", "dn_w_in", "dn_w_out", "ffn_w_gate", "ffn_w_up", "ffn_w_down"]
    big_w = [ret_w_in, ret_w_out, dn_w_in, dn_w_out, ffn_w_gate, ffn_w_up, ffn_w_down]
    big_m = [m_ret_w_in, m_ret_w_out, m_dn_w_in, m_dn_w_out, m_ffn_w_gate, m_ffn_w_up, m_ffn_w_down]
    big_v = [v_ret_w_in, v_ret_w_out, v_dn_w_in, v_dn_w_out, v_ffn_w_gate, v_ffn_w_up, v_ffn_w_down]
    res = {}
    for nm, parts, w_, m_, v_ in zip(big_names, big_parts, big_w, big_m, big_v):
        r2, c2 = parts.shape[1], parts.shape[2]
        outs = _adamw_reduce(parts, w_.reshape(r2, c2), m_.reshape(r2, c2), v_.reshape(r2, c2), f"adamw_{nm}")
        res[nm] = [o.reshape(w_.shape) for o in outs]

    small_names = ["meta_tokens", "mix_norm_w", "ffn_norm_w", "ret_gn_w", "dn_conv_w", "dn_a_log", "dn_dt_bias",
                   "dn_norm_w", "final_norm_w"]
    small_g = [gr["meta"], gr["mix_norm"], gr["ffn_norm"], gr["ret_gn"], gr["conv_w"], gr["a_log"], gr["dt_bias"],
               gr["dn_norm"], gr["final_norm"]]
    small_w = [meta_tokens, mix_norm_w, ffn_norm_w, ret_gn_w, dn_conv_w, dn_a_log, dn_dt_bias, dn_norm_w, final_norm_w]
    small_m = [m_meta_tokens, m_mix_norm_w, m_ffn_norm_w, m_ret_gn_w, m_dn_conv_w, m_dn_a_log, m_dn_dt_bias,
               m_dn_norm_w, m_final_norm_w]
    small_v = [v_meta_tokens, v_mix_norm_w, v_ffn_norm_w, v_ret_gn_w, v_dn_conv_w, v_dn_a_log, v_dn_dt_bias,
               v_dn_norm_w, v_final_norm_w]
    sharded = {"meta_tokens", "dn_conv_w", "dn_norm_w"}
    flat = jnp.concatenate([g.reshape(-1) for g in small_g])
    row = 8 * LANES
    n_flat = flat.shape[0]
    flat = jnp.pad(flat, (0, -n_flat % row)).reshape(-1, row)
    (gathered,) = _exchange([flat], True, "gather_small_grads")
    gathered = gathered.reshape(N_DEV, -1)
    pieces, off = [], 0
    for nm, g, w_ in zip(small_names, small_g, small_w):
        full = gathered[:, off:off + g.size].reshape((N_DEV,) + g.shape)
        off += g.size
        if nm in sharded:
            wloc = w_.shape[-1]
            full = lax.dynamic_slice_in_dim(full, me * wloc, wloc, axis=full.ndim - 1)
        pieces.append(full.reshape(N_DEV, -1))
    sizes = [p.shape[1] for p in pieces]
    n_loc = sum(sizes)
    pad_loc = -n_loc % row

    def pack(vs, lead):
        cat = jnp.concatenate([a.reshape(lead + (-1,)) for a in vs], axis=-1)
        cat = jnp.pad(cat, [(0, 0)] * len(lead) + [(0, pad_loc)])
        return cat.reshape(lead + (-1, row))

    outs = _adamw_reduce(pack(pieces, (N_DEV,)), pack(small_w, ()), pack(small_m, ()), pack(small_v, ()), "adamw_small")
    off = 0
    for nm, sz, w_ in zip(small_names, sizes, small_w):
        res[nm] = [o.reshape(-1)[off:off + sz].reshape(w_.shape) for o in outs]
        off += sz

    order = ["meta_tokens", "mix_norm_w", "ffn_norm_w", "ret_w_in", "ret_gn_w", "ret_w_out", "dn_w_in", "dn_conv_w",
             "dn_a_log", "dn_dt_bias", "dn_norm_w", "dn_w_out", "ffn_w_gate", "ffn_w_up", "ffn_w_down", "final_norm_w"]
    grad_x = grad_x.reshape(x.shape)
    return (loss, grad_x, *[res[nm][0] for nm in order], *[res[nm][1] for nm in order],
            *[res[nm][2] for nm in order], *[res[nm][3] for nm in order])
```
